```python
import jax, jax.numpy as jnp
from jax import lax
import numpy as np

D_MODEL = 1024
BATCH = 8
SEQ = 8192
DEPTH = 4

N_MIXERS = 2
N_ATTN_LAYERS = (DEPTH + 1) // 2
N_CONV_LAYERS = DEPTH // 2
N_HEADS = 8
QK_NOPE_DIM = 128
QK_ROPE_DIM = 64
QK_HEAD_DIM = QK_NOPE_DIM + QK_ROPE_DIM
V_HEAD_DIM = 128
Q_LORA_RANK = 256
KV_LORA_RANK = 128
DOWN_DIM = Q_LORA_RANK + KV_LORA_RANK + QK_ROPE_DIM
ROPE_THETA = 10000.0
Q_BLOCK = 128
CONV_WIDTH = 3
D_FF = 4 * D_MODEL
EPS = 1e-6
MAX_POS_OFFSET = 1024

kernel_name = 'hybrid_mla_shortconv_sqrelu'


def rms_norm(x, g):
    xf = x.astype(jnp.float32)
    xf = xf * lax.rsqrt(jnp.mean(xf * xf, axis=-1, keepdims=True) + EPS)
    return xf.astype(x.dtype) * g


def rope_tables(positions):
    inv_freq = ROPE_THETA ** (-jnp.arange(0, QK_ROPE_DIM, 2, dtype=jnp.float32) / QK_ROPE_DIM)
    ang = positions.astype(jnp.float32)[..., None] * inv_freq
    return jnp.cos(ang)[:, :, None, :], jnp.sin(ang)[:, :, None, :]


def apply_rope(x, cos, sin):
    x1, x2 = jnp.split(x, 2, axis=-1)
    cos = cos.astype(x.dtype)
    sin = sin.astype(x.dtype)
    return jnp.concatenate([x1 * cos - x2 * sin, x2 * cos + x1 * sin], axis=-1)


def mla_mixer(h, positions, cos, sin, w_down, g_q_a, g_kv_a, w_uq, w_ukv, g_qnorm, g_knorm, w_o):
    b, s, _ = h.shape
    a = h @ w_down
    c_q = rms_norm(a[..., :Q_LORA_RANK], g_q_a)
    c_kv = rms_norm(a[..., Q_LORA_RANK:Q_LORA_RANK + KV_LORA_RANK], g_kv_a)
    k_pe = a[..., Q_LORA_RANK + KV_LORA_RANK:]
    q = (c_q @ w_uq).reshape(b, s, N_HEADS, QK_HEAD_DIM)
    kv = (c_kv @ w_ukv).reshape(b, s, N_HEADS, QK_NOPE_DIM + V_HEAD_DIM)
    k_nope, v = kv[..., :QK_NOPE_DIM], kv[..., QK_NOPE_DIM:]
    k_pe = jnp.broadcast_to(k_pe[:, :, None, :], (b, s, N_HEADS, QK_ROPE_DIM))
    k = jnp.concatenate([k_nope, k_pe], axis=-1)
    q = rms_norm(q, g_qnorm)
    k = rms_norm(k, g_knorm)
    q = jnp.concatenate([q[..., :QK_NOPE_DIM], apply_rope(q[..., QK_NOPE_DIM:], cos, sin)], axis=-1)
    k = jnp.concatenate([k[..., :QK_NOPE_DIM], apply_rope(k[..., QK_NOPE_DIM:], cos, sin)], axis=-1)
    scale = QK_HEAD_DIM ** -0.5
    nb = s // Q_BLOCK
    q_blocks = q.reshape(b, nb, Q_BLOCK, N_HEADS, QK_HEAD_DIM).transpose(1, 0, 2, 3, 4)
    pos_blocks = positions.reshape(b, nb, Q_BLOCK).transpose(1, 0, 2)

    def attend(args):
        q_blk, pos_blk = args
        sc = jnp.einsum('bqhd,bkhd->bhqk', q_blk, k).astype(jnp.float32) * scale
        mask = pos_blk[:, None, :, None] >= positions[:, None, None, :]
        sc = jnp.where(mask, sc, jnp.finfo(jnp.float32).min)
        p = jax.nn.softmax(sc, axis=-1).astype(v.dtype)
        return jnp.einsum('bhqk,bkhd->bqhd', p, v)

    o = lax.map(attend, (q_blocks, pos_blocks))
    o = o.transpose(1, 0, 2, 3, 4).reshape(b, s, N_HEADS * V_HEAD_DIM)
    return o @ w_o


def short_conv_mixer(h, w_in, conv_w, w_out):
    bcu = h @ w_in
    gate_b, gate_c, u = jnp.split(bcu, 3, axis=-1)
    u = gate_c * u
    u = lax.conv_general_dilated(
        u, conv_w[:, None, :], window_strides=(1,), padding=((CONV_WIDTH - 1, 0),),
        dimension_numbers=('NWC', 'WIO', 'NWC'), feature_group_count=D_MODEL)
    return (gate_b * u) @ w_out


def sqrelu_mlp(h, w1, w2):
    return jnp.square(jax.nn.relu(h @ w1)) @ w2


def _fwd_setup_inputs(seed: int = 0) -> dict:
    key = jax.random.key(seed)
    ks = jax.random.split(key, 20)
    nrm = lambda k, shape, fan_in: jax.random.normal(k, shape, jnp.float32) * fan_in ** -0.5
    gain = lambda k, shape: 1.0 + 0.02 * jax.random.normal(k, shape, jnp.float32)
    x = jax.random.normal(ks[0], (BATCH, SEQ, D_MODEL), jnp.float32)
    offset = jax.random.randint(ks[1], (BATCH, 1), 0, MAX_POS_OFFSET, dtype=jnp.int32)
    positions = offset + jnp.arange(SEQ, dtype=jnp.int32)[None, :]
    return {
        'x': x,
        'positions': positions,
        'g_mix': gain(ks[2], (DEPTH, D_MODEL)),
        'g_mlp': gain(ks[3], (DEPTH, D_MODEL)),
        'attn_w_down': nrm(ks[4], (N_ATTN_LAYERS, D_MODEL, DOWN_DIM), D_MODEL),
        'attn_g_q_a': gain(ks[5], (N_ATTN_LAYERS, Q_LORA_RANK)),
        'attn_g_kv_a': gain(ks[6], (N_ATTN_LAYERS, KV_LORA_RANK)),
        'attn_w_uq': nrm(ks[7], (N_ATTN_LAYERS, Q_LORA_RANK, N_HEADS * QK_HEAD_DIM), Q_LORA_RANK),
        'attn_w_ukv': nrm(ks[8], (N_ATTN_LAYERS, KV_LORA_RANK, N_HEADS * (QK_NOPE_DIM + V_HEAD_DIM)), KV_LORA_RANK),
        'attn_g_qnorm': gain(ks[9], (N_ATTN_LAYERS, QK_HEAD_DIM)),
        'attn_g_knorm': gain(ks[10], (N_ATTN_LAYERS, QK_HEAD_DIM)),
        'attn_w_o': nrm(ks[11], (N_ATTN_LAYERS, N_HEADS * V_HEAD_DIM, D_MODEL), N_HEADS * V_HEAD_DIM),
        'conv_w_in': nrm(ks[12], (N_CONV_LAYERS, D_MODEL, 3 * D_MODEL), D_MODEL),
        'conv_w': nrm(ks[13], (N_CONV_LAYERS, CONV_WIDTH, D_MODEL), CONV_WIDTH),
        'conv_w_out': nrm(ks[14], (N_CONV_LAYERS, D_MODEL, D_MODEL), D_MODEL),
        'mlp_w1': nrm(ks[15], (DEPTH, D_MODEL, D_FF), D_MODEL),
        'mlp_w2': nrm(ks[16], (DEPTH, D_FF, D_MODEL), D_FF),
    }


def _fwd_reference(x, positions, g_mix, g_mlp, attn_w_down, attn_g_q_a, attn_g_kv_a, attn_w_uq,
              attn_w_ukv, attn_g_qnorm, attn_g_knorm, attn_w_o, conv_w_in, conv_w, conv_w_out,
              mlp_w1, mlp_w2):
    cos, sin = rope_tables(positions)
    for i in range(DEPTH):
        h = rms_norm(x, g_mix[i])
        j = i // N_MIXERS
        if i % N_MIXERS == 0:
            x = x + mla_mixer(h, positions, cos, sin, attn_w_down[j], attn_g_q_a[j], attn_g_kv_a[j],
                              attn_w_uq[j], attn_w_ukv[j], attn_g_qnorm[j], attn_g_knorm[j], attn_w_o[j])
        else:
            x = x + short_conv_mixer(h, conv_w_in[j], conv_w[j], conv_w_out[j])
        h = rms_norm(x, g_mlp[i])
        x = x + sqrelu_mlp(h, mlp_w1[i], mlp_w2[i])
    return x


import jax as _jax
import jax.numpy as _jnp

TWIN_FORMAT = 'train_step'
FWD_PARAMS = ['x', 'positions', 'g_mix', 'g_mlp', 'attn_w_down', 'attn_g_q_a', 'attn_g_kv_a', 'attn_w_uq', 'attn_w_ukv', 'attn_g_qnorm', 'attn_g_knorm', 'attn_w_o', 'conv_w_in', 'conv_w', 'conv_w_out', 'mlp_w1', 'mlp_w2']
TWIN_WEIGHTS = ['g_mix', 'g_mlp', 'attn_w_down', 'attn_g_q_a', 'attn_g_kv_a', 'attn_w_uq', 'attn_w_ukv', 'attn_g_qnorm', 'attn_g_knorm', 'attn_w_o', 'conv_w_in', 'conv_w', 'conv_w_out', 'mlp_w1', 'mlp_w2']
TWIN_DIFF_INPUT = 'x'
TWIN_INPUTS = ['x', 'positions', 'g_mix', 'g_mlp', 'attn_w_down', 'attn_g_q_a', 'attn_g_kv_a', 'attn_w_uq', 'attn_w_ukv', 'attn_g_qnorm', 'attn_g_knorm', 'attn_w_o', 'conv_w_in', 'conv_w', 'conv_w_out', 'mlp_w1', 'mlp_w2', 'loss_target', 'm_g_mix', 'm_g_mlp', 'm_attn_w_down', 'm_attn_g_q_a', 'm_attn_g_kv_a', 'm_attn_w_uq', 'm_attn_w_ukv', 'm_attn_g_qnorm', 'm_attn_g_knorm', 'm_attn_w_o', 'm_conv_w_in', 'm_conv_w', 'm_conv_w_out', 'm_mlp_w1', 'm_mlp_w2', 'v_g_mix', 'v_g_mlp', 'v_attn_w_down', 'v_attn_g_q_a', 'v_attn_g_kv_a', 'v_attn_w_uq', 'v_attn_w_ukv', 'v_attn_g_qnorm', 'v_attn_g_knorm', 'v_attn_w_o', 'v_conv_w_in', 'v_conv_w', 'v_conv_w_out', 'v_mlp_w1', 'v_mlp_w2']
TWIN_OUTPUTS = ['loss', 'grad_x', 'grad_g_mix', 'grad_g_mlp', 'grad_attn_w_down', 'grad_attn_g_q_a', 'grad_attn_g_kv_a', 'grad_attn_w_uq', 'grad_attn_w_ukv', 'grad_attn_g_qnorm', 'grad_attn_g_knorm', 'grad_attn_w_o', 'grad_conv_w_in', 'grad_conv_w', 'grad_conv_w_out', 'grad_mlp_w1', 'grad_mlp_w2', 'delta_g_mix', 'delta_g_mlp', 'delta_attn_w_down', 'delta_attn_g_q_a', 'delta_attn_g_kv_a', 'delta_attn_w_uq', 'delta_attn_w_ukv', 'delta_attn_g_qnorm', 'delta_attn_g_knorm', 'delta_attn_w_o', 'delta_conv_w_in', 'delta_conv_w', 'delta_conv_w_out', 'delta_mlp_w1', 'delta_mlp_w2', 'new_m_g_mix', 'new_m_g_mlp', 'new_m_attn_w_down', 'new_m_attn_g_q_a', 'new_m_attn_g_kv_a', 'new_m_attn_w_uq', 'new_m_attn_w_ukv', 'new_m_attn_g_qnorm', 'new_m_attn_g_knorm', 'new_m_attn_w_o', 'new_m_conv_w_in', 'new_m_conv_w', 'new_m_conv_w_out', 'new_m_mlp_w1', 'new_m_mlp_w2', 'new_v_g_mix', 'new_v_g_mlp', 'new_v_attn_w_down', 'new_v_attn_g_q_a', 'new_v_attn_g_kv_a', 'new_v_attn_w_uq', 'new_v_attn_w_ukv', 'new_v_attn_g_qnorm', 'new_v_attn_g_knorm', 'new_v_attn_w_o', 'new_v_conv_w_in', 'new_v_conv_w', 'new_v_conv_w_out', 'new_v_mlp_w1', 'new_v_mlp_w2']
TWIN_LEAF_KINDS = {'loss': 'loss', 'grad_x': 'grad_x', 'grad_g_mix': 'grad_w', 'grad_g_mlp': 'grad_w', 'grad_attn_w_down': 'grad_w', 'grad_attn_g_q_a': 'grad_w', 'grad_attn_g_kv_a': 'grad_w', 'grad_attn_w_uq': 'grad_w', 'grad_attn_w_ukv': 'grad_w', 'grad_attn_g_qnorm': 'grad_w', 'grad_attn_g_knorm': 'grad_w', 'grad_attn_w_o': 'grad_w', 'grad_conv_w_in': 'grad_w', 'grad_conv_w': 'grad_w', 'grad_conv_w_out': 'grad_w', 'grad_mlp_w1': 'grad_w', 'grad_mlp_w2': 'grad_w', 'delta_g_mix': 'delta_w', 'delta_g_mlp': 'delta_w', 'delta_attn_w_down': 'delta_w', 'delta_attn_g_q_a': 'delta_w', 'delta_attn_g_kv_a': 'delta_w', 'delta_attn_w_uq': 'delta_w', 'delta_attn_w_ukv': 'delta_w', 'delta_attn_g_qnorm': 'delta_w', 'delta_attn_g_knorm': 'delta_w', 'delta_attn_w_o': 'delta_w', 'delta_conv_w_in': 'delta_w', 'delta_conv_w': 'delta_w', 'delta_conv_w_out': 'delta_w', 'delta_mlp_w1': 'delta_w', 'delta_mlp_w2': 'delta_w', 'new_m_g_mix': 'new_m', 'new_m_g_mlp': 'new_m', 'new_m_attn_w_down': 'new_m', 'new_m_attn_g_q_a': 'new_m', 'new_m_attn_g_kv_a': 'new_m', 'new_m_attn_w_uq': 'new_m', 'new_m_attn_w_ukv': 'new_m', 'new_m_attn_g_qnorm': 'new_m', 'new_m_attn_g_knorm': 'new_m', 'new_m_attn_w_o': 'new_m', 'new_m_conv_w_in': 'new_m', 'new_m_conv_w': 'new_m', 'new_m_conv_w_out': 'new_m', 'new_m_mlp_w1': 'new_m', 'new_m_mlp_w2': 'new_m', 'new_v_g_mix': 'new_v', 'new_v_g_mlp': 'new_v', 'new_v_attn_w_down': 'new_v', 'new_v_attn_g_q_a': 'new_v', 'new_v_attn_g_kv_a': 'new_v', 'new_v_attn_w_uq': 'new_v', 'new_v_attn_w_ukv': 'new_v', 'new_v_attn_g_qnorm': 'new_v', 'new_v_attn_g_knorm': 'new_v', 'new_v_attn_w_o': 'new_v', 'new_v_conv_w_in': 'new_v', 'new_v_conv_w': 'new_v', 'new_v_conv_w_out': 'new_v', 'new_v_mlp_w1': 'new_v', 'new_v_mlp_w2': 'new_v'}


def _forward(args):
    return _fwd_reference(*[args[k] for k in FWD_PARAMS])


def _output_shape():
    def fwd():
        inp = _fwd_setup_inputs(0)
        return _fwd_reference(*[inp[k] for k in FWD_PARAMS])
    out = _jax.eval_shape(fwd)
    return out.shape, out.dtype

N_MICROBATCH = 1
ADAM_LR = 0.001
ADAM_B1 = 0.9
ADAM_B2 = 0.999
ADAM_EPS = 1e-08
ADAM_WD = 0.01
ADAM_STEP = 10
PER_EXAMPLE_BATCH_AXIS = {'x': 0, 'positions': 0, 'loss_target': 0}
SHARED_INPUTS = []
_WEIGHT_DTYPES = {'g_mix': _jnp.float32, 'g_mlp': _jnp.float32, 'attn_w_down': _jnp.float32, 'attn_g_q_a': _jnp.float32, 'attn_g_kv_a': _jnp.float32, 'attn_w_uq': _jnp.float32, 'attn_w_ukv': _jnp.float32, 'attn_g_qnorm': _jnp.float32, 'attn_g_knorm': _jnp.float32, 'attn_w_o': _jnp.float32, 'conv_w_in': _jnp.float32, 'conv_w': _jnp.float32, 'conv_w_out': _jnp.float32, 'mlp_w1': _jnp.float32, 'mlp_w2': _jnp.float32}
MOMENT_SCALE = {'g_mix': 1.404510e+02, 'g_mlp': 1.996569e+02, 'attn_w_down': 4.180963e+01, 'attn_g_q_a': 1.771416e+00, 'attn_g_kv_a': 1.128526e+02, 'attn_w_uq': 7.241403e-01, 'attn_w_ukv': 1.710203e+01, 'attn_g_qnorm': 2.858138e+00, 'attn_g_knorm': 2.847964e+00, 'attn_w_o': 2.332404e+01, 'conv_w_in': 1.173034e+01, 'conv_w': 3.995000e+01, 'conv_w_out': 5.100191e+00, 'mlp_w1': 1.380436e+01, 'mlp_w2': 5.708950e+01}


def _to_microbatches(a, axis):
    t = _jnp.moveaxis(a, axis, 0)
    t = t.reshape((N_MICROBATCH, t.shape[0] // N_MICROBATCH) + t.shape[1:])
    return _jnp.moveaxis(t, 1, axis + 1)


def setup_inputs(seed: int = 0) -> dict:
    inp = _fwd_setup_inputs(seed)
    key = _jax.random.fold_in(_jax.random.key(seed), 7919)
    shape, _ = _output_shape()
    out = dict(inp)
    out["loss_target"] = _jax.random.normal(_jax.random.fold_in(key, 0), shape, _jnp.float32)
    for i, name in enumerate(TWIN_WEIGHTS):
        w = inp[name].astype(_jnp.float32)
        if MOMENT_SCALE is None:
            s = _jnp.sqrt(_jnp.mean(_jnp.square(w)) + 1e-30)
        else:
            s = MOMENT_SCALE[name]
        km, kv = _jax.random.split(_jax.random.fold_in(key, i + 1))
        out[name] = w
        out["m_" + name] = s * _jax.random.normal(km, w.shape, _jnp.float32)
        out["v_" + name] = (s * s) * _jax.random.uniform(kv, w.shape, _jnp.float32, 0.5, 1.5)
    if N_MICROBATCH > 1:
        for name, axis in PER_EXAMPLE_BATCH_AXIS.items():
            out[name] = _to_microbatches(out[name], axis)
    return {'x': out['x'], 'positions': out['positions'], 'g_mix': out['g_mix'], 'g_mlp': out['g_mlp'], 'attn_w_down': out['attn_w_down'], 'attn_g_q_a': out['attn_g_q_a'], 'attn_g_kv_a': out['attn_g_kv_a'], 'attn_w_uq': out['attn_w_uq'], 'attn_w_ukv': out['attn_w_ukv'], 'attn_g_qnorm': out['attn_g_qnorm'], 'attn_g_knorm': out['attn_g_knorm'], 'attn_w_o': out['attn_w_o'], 'conv_w_in': out['conv_w_in'], 'conv_w': out['conv_w'], 'conv_w_out': out['conv_w_out'], 'mlp_w1': out['mlp_w1'], 'mlp_w2': out['mlp_w2'], 'loss_target': out['loss_target'], 'm_g_mix': out['m_g_mix'], 'm_g_mlp': out['m_g_mlp'], 'm_attn_w_down': out['m_attn_w_down'], 'm_attn_g_q_a': out['m_attn_g_q_a'], 'm_attn_g_kv_a': out['m_attn_g_kv_a'], 'm_attn_w_uq': out['m_attn_w_uq'], 'm_attn_w_ukv': out['m_attn_w_ukv'], 'm_attn_g_qnorm': out['m_attn_g_qnorm'], 'm_attn_g_knorm': out['m_attn_g_knorm'], 'm_attn_w_o': out['m_attn_w_o'], 'm_conv_w_in': out['m_conv_w_in'], 'm_conv_w': out['m_conv_w'], 'm_conv_w_out': out['m_conv_w_out'], 'm_mlp_w1': out['m_mlp_w1'], 'm_mlp_w2': out['m_mlp_w2'], 'v_g_mix': out['v_g_mix'], 'v_g_mlp': out['v_g_mlp'], 'v_attn_w_down': out['v_attn_w_down'], 'v_attn_g_q_a': out['v_attn_g_q_a'], 'v_attn_g_kv_a': out['v_attn_g_kv_a'], 'v_attn_w_uq': out['v_attn_w_uq'], 'v_attn_w_ukv': out['v_attn_w_ukv'], 'v_attn_g_qnorm': out['v_attn_g_qnorm'], 'v_attn_g_knorm': out['v_attn_g_knorm'], 'v_attn_w_o': out['v_attn_w_o'], 'v_conv_w_in': out['v_conv_w_in'], 'v_conv_w': out['v_conv_w'], 'v_conv_w_out': out['v_conv_w_out'], 'v_mlp_w1': out['v_mlp_w1'], 'v_mlp_w2': out['v_mlp_w2']}


def _loss(weights, diff, rest, loss_target):
    with _jax.named_scope("forward"):
        args = {**rest, TWIN_DIFF_INPUT: diff, **{k: w.astype(_WEIGHT_DTYPES[k]) for k, w in weights.items()}}
        y = _forward(args)
    with _jax.named_scope("loss_head"):
        err = _jnp.square(y.astype(_jnp.float32) - loss_target)
        return 0.5 * _jnp.sum(_jnp.mean(err, axis=-1)) if err.ndim else 0.5 * err


def _adamw(w, g, m, v):
    m = ADAM_B1 * m + (1.0 - ADAM_B1) * g
    v = ADAM_B2 * v + (1.0 - ADAM_B2) * _jnp.square(g)
    m_hat = m / (1.0 - ADAM_B1 ** ADAM_STEP)
    v_hat = v / (1.0 - ADAM_B2 ** ADAM_STEP)
    delta = -ADAM_LR * (m_hat / (_jnp.sqrt(v_hat) + ADAM_EPS) + ADAM_WD * w)
    return delta, m, v


def reference(x, positions, g_mix, g_mlp, attn_w_down, attn_g_q_a, attn_g_kv_a, attn_w_uq, attn_w_ukv, attn_g_qnorm, attn_g_knorm, attn_w_o, conv_w_in, conv_w, conv_w_out, mlp_w1, mlp_w2, loss_target, m_g_mix, m_g_mlp, m_attn_w_down, m_attn_g_q_a, m_attn_g_kv_a, m_attn_w_uq, m_attn_w_ukv, m_attn_g_qnorm, m_attn_g_knorm, m_attn_w_o, m_conv_w_in, m_conv_w, m_conv_w_out, m_mlp_w1, m_mlp_w2, v_g_mix, v_g_mlp, v_attn_w_down, v_attn_g_q_a, v_attn_g_kv_a, v_attn_w_uq, v_attn_w_ukv, v_attn_g_qnorm, v_attn_g_knorm, v_attn_w_o, v_conv_w_in, v_conv_w, v_conv_w_out, v_mlp_w1, v_mlp_w2):
    given = dict(x=x, positions=positions, g_mix=g_mix, g_mlp=g_mlp, attn_w_down=attn_w_down, attn_g_q_a=attn_g_q_a, attn_g_kv_a=attn_g_kv_a, attn_w_uq=attn_w_uq, attn_w_ukv=attn_w_ukv, attn_g_qnorm=attn_g_qnorm, attn_g_knorm=attn_g_knorm, attn_w_o=attn_w_o, conv_w_in=conv_w_in, conv_w=conv_w, conv_w_out=conv_w_out, mlp_w1=mlp_w1, mlp_w2=mlp_w2, loss_target=loss_target, m_g_mix=m_g_mix, m_g_mlp=m_g_mlp, m_attn_w_down=m_attn_w_down, m_attn_g_q_a=m_attn_g_q_a, m_attn_g_kv_a=m_attn_g_kv_a, m_attn_w_uq=m_attn_w_uq, m_attn_w_ukv=m_attn_w_ukv, m_attn_g_qnorm=m_attn_g_qnorm, m_attn_g_knorm=m_attn_g_knorm, m_attn_w_o=m_attn_w_o, m_conv_w_in=m_conv_w_in, m_conv_w=m_conv_w, m_conv_w_out=m_conv_w_out, m_mlp_w1=m_mlp_w1, m_mlp_w2=m_mlp_w2, v_g_mix=v_g_mix, v_g_mlp=v_g_mlp, v_attn_w_down=v_attn_w_down, v_attn_g_q_a=v_attn_g_q_a, v_attn_g_kv_a=v_attn_g_kv_a, v_attn_w_uq=v_attn_w_uq, v_attn_w_ukv=v_attn_w_ukv, v_attn_g_qnorm=v_attn_g_qnorm, v_attn_g_knorm=v_attn_g_knorm, v_attn_w_o=v_attn_w_o, v_conv_w_in=v_conv_w_in, v_conv_w=v_conv_w, v_conv_w_out=v_conv_w_out, v_mlp_w1=v_mlp_w1, v_mlp_w2=v_mlp_w2)
    weights = {n: given[n] for n in TWIN_WEIGHTS}
    shared = {n: given[n] for n in SHARED_INPUTS}
    per_example = {n: given[n] for n in ['x', 'positions']}
    grad_fn = _jax.value_and_grad(_loss, argnums=(0, 1))

    def one_microbatch(ex, loss_target):
        ex = dict(ex)
        diff = ex.pop(TWIN_DIFF_INPUT)
        return grad_fn(weights, diff, {**shared, **ex}, loss_target)

    if N_MICROBATCH == 1:
        loss, (grad_w, grad_x) = one_microbatch(per_example, given["loss_target"])
    else:
        def body(carry, xs):
            loss_sum, grad_sum = carry
            l_k, (gw_k, gx_k) = one_microbatch(xs[0], xs[1])
            with _jax.named_scope("update"):
                return (loss_sum + l_k, _jax.tree.map(_jnp.add, grad_sum, gw_k)), gx_k

        init = (_jnp.zeros((), _jnp.float32), _jax.tree.map(_jnp.zeros_like, weights))
        (loss, grad_w), grad_x = _jax.lax.scan(body, init, (per_example, given["loss_target"]))
    with _jax.named_scope("update"):
        delta_w, new_m, new_v = {}, {}, {}
        for n in TWIN_WEIGHTS:
            delta_w[n], new_m[n], new_v[n] = _adamw(weights[n], grad_w[n], given["m_" + n], given["v_" + n])
    return (loss, grad_x, *[grad_w[n] for n in TWIN_WEIGHTS], *[delta_w[n] for n in TWIN_WEIGHTS],
            *[new_m[n] for n in TWIN_WEIGHTS], *[new_v[n] for n in TWIN_WEIGHTS])
```

```python
import functools

import jax
import jax.numpy as jnp
from jax import lax
from jax.experimental import pallas as pl
from jax.experimental.pallas import tpu as pltpu

F32 = jnp.float32
BF16 = jnp.bfloat16

D_MODEL = 1024
N_HEADS = 8
QK_NOPE = 128
QK_ROPE = 64
QK_DIM = QK_NOPE + QK_ROPE
QK_PAD = 256
V_DIM = 128
Q_RANK = 256
KV_RANK = 128
DOWN_DIM = Q_RANK + KV_RANK + QK_ROPE
DOWN_PAD = 512
D_FF = 4 * D_MODEL
ROPE_THETA = 10000.0
EPS = 1e-6
NEG = -1e30
SCALE = QK_DIM ** -0.5

ADAM_LR = 0.001
ADAM_B1 = 0.9
ADAM_B2 = 0.999
ADAM_EPS = 1e-08
ADAM_WD = 0.01
ADAM_STEP = 10

N_CHIPS = 4
PACK_COLS = 1024
MESH = pl.DeviceIdType.MESH

TM = 512
TM_WIDE = 256
TQ = 512
TK = 512
T_PREP = 256
T_RED = 1024
SUM_ROWS = 640


def _tile(n, pref):
    t = min(n, pref)
    assert n % t == 0, (n, t)
    return t


def _cparams(*sem):
    return pltpu.CompilerParams(dimension_semantics=sem)


def _dot(a, b):
    return jnp.dot(a, b, preferred_element_type=F32)


def _dot_nt(a, b):
    return lax.dot_general(a, b, (((1,), (1,)), ((), ())), preferred_element_type=F32)


def _dot_tn(a, b):
    return lax.dot_general(a, b, (((0,), (0,)), ((), ())), preferred_element_type=F32)


def _rms(x, width):
    r = lax.rsqrt(jnp.sum(x * x, axis=-1, keepdims=True) * (1.0 / width) + EPS)
    return x * r, r


def _rms_bwd(xhat, r, dxhat, width):
    return r * (dxhat - xhat * (jnp.sum(dxhat * xhat, axis=-1, keepdims=True) * (1.0 / width)))


def _rope(t, cc, sa, sb):
    return t * cc + pltpu.roll(t, 96, 1) * sa + pltpu.roll(t, 32, 1) * sb


def _rope_t(g, cc, sa, sb):
    return g * cc + pltpu.roll(g * sa, 32, 1) + pltpu.roll(g * sb, 96, 1)


def _mm_nn(a, b, *, out_dtype, name, residual=None):
    m, k = a.shape
    n = b.shape[1]
    tm = _tile(m, TM)

    def body(*refs):
        if residual is None:
            a_ref, b_ref, o_ref = refs
        else:
            a_ref, b_ref, r_ref, o_ref = refs
        acc = _dot(a_ref[...].astype(BF16), b_ref[...])
        if residual is not None:
            acc = acc + r_ref[...]
        o_ref[...] = acc.astype(o_ref.dtype)

    in_specs = [pl.BlockSpec((tm, k), lambda i: (i, 0)), pl.BlockSpec((k, n), lambda i: (0, 0))]
    args = [a, b]
    if residual is not None:
        in_specs.append(pl.BlockSpec((tm, n), lambda i: (i, 0)))
        args.append(residual)
    return pl.pallas_call(
        body, name=name, grid=(m // tm,), in_specs=in_specs,
        out_specs=pl.BlockSpec((tm, n), lambda i: (i, 0)),
        out_shape=jax.ShapeDtypeStruct((m, n), out_dtype),
        compiler_params=_cparams("parallel"),
    )(*args)


def _mm_nt(a, b, *, out_dtype, name):
    m, k = a.shape
    n = b.shape[0]
    tm = _tile(m, TM)

    def body(a_ref, b_ref, o_ref):
        o_ref[...] = _dot_nt(a_ref[...].astype(BF16), b_ref[...]).astype(o_ref.dtype)

    return pl.pallas_call(
        body, name=name, grid=(m // tm,),
        in_specs=[pl.BlockSpec((tm, k), lambda i: (i, 0)), pl.BlockSpec((n, k), lambda i: (0, 0))],
        out_specs=pl.BlockSpec((tm, n), lambda i: (i, 0)),
        out_shape=jax.ShapeDtypeStruct((m, n), out_dtype),
        compiler_params=_cparams("parallel"),
    )(a, b)


def _mm_tn(a, b, *, name):
    s, ka = a.shape
    n = b.shape[1]
    ts = _tile(s, T_RED)
    tka = _tile(ka, 1024)
    tn = _tile(n, 1024)

    def body(a_ref, b_ref, o_ref):
        @pl.when(pl.program_id(2) == 0)
        def _():
            o_ref[...] = jnp.zeros_like(o_ref)

        o_ref[...] += _dot_tn(a_ref[...].astype(BF16), b_ref[...].astype(BF16))

    return pl.pallas_call(
        body, name=name, grid=(ka // tka, n // tn, s // ts),
        in_specs=[pl.BlockSpec((ts, tka), lambda i, j, t: (t, i)), pl.BlockSpec((ts, tn), lambda i, j, t: (t, j))],
        out_specs=pl.BlockSpec((tka, tn), lambda i, j, t: (i, j)),
        out_shape=jax.ShapeDtypeStruct((ka, n), F32),
        compiler_params=_cparams("parallel", "parallel", "arbitrary"),
    )(a, b)


def _norm_mm(x, g, w, *, out_dtype, name):
    s, d = x.shape
    n = w.shape[1]
    tm = _tile(s, TM)

    def body(x_ref, g_ref, w_ref, h_ref, o_ref):
        xhat, _ = _rms(x_ref[...], d)
        h = (xhat * g_ref[...]).astype(BF16)
        h_ref[...] = h
        o_ref[...] = _dot(h, w_ref[...]).astype(o_ref.dtype)

    return pl.pallas_call(
        body, name=name, grid=(s // tm,),
        in_specs=[pl.BlockSpec((tm, d), lambda i: (i, 0)), pl.BlockSpec((1, d), lambda i: (0, 0)),
                  pl.BlockSpec((d, n), lambda i: (0, 0))],
        out_specs=[pl.BlockSpec((tm, d), lambda i: (i, 0)), pl.BlockSpec((tm, n), lambda i: (i, 0))],
        out_shape=[jax.ShapeDtypeStruct((s, d), BF16), jax.ShapeDtypeStruct((s, n), out_dtype)],
        compiler_params=_cparams("parallel"),
    )(x, g, w)


def _nt_rms_bwd(dy, w, x, g, dres, *, name):
    s, n = dy.shape
    d = x.shape[1]
    tm = _tile(s, TM)

    def body(dy_ref, w_ref, x_ref, g_ref, dres_ref, dx_ref, dg_ref):
        @pl.when(pl.program_id(0) == 0)
        def _():
            dg_ref[...] = jnp.zeros_like(dg_ref)

        dh = _dot_nt(dy_ref[...], w_ref[...])
        xhat, r = _rms(x_ref[...], d)
        dg_ref[...] += jnp.sum(dh * xhat, axis=0, keepdims=True)
        dx_ref[...] = dres_ref[...] + _rms_bwd(xhat, r, dh * g_ref[...], d)

    return pl.pallas_call(
        body, name=name, grid=(s // tm,),
        in_specs=[pl.BlockSpec((tm, n), lambda i: (i, 0)), pl.BlockSpec((d, n), lambda i: (0, 0)),
                  pl.BlockSpec((tm, d), lambda i: (i, 0)), pl.BlockSpec((1, d), lambda i: (0, 0)),
                  pl.BlockSpec((tm, d), lambda i: (i, 0))],
        out_specs=[pl.BlockSpec((tm, d), lambda i: (i, 0)), pl.BlockSpec((1, d), lambda i: (0, 0))],
        out_shape=[jax.ShapeDtypeStruct((s, d), F32), jax.ShapeDtypeStruct((1, d), F32)],
        compiler_params=_cparams("arbitrary"),
    )(dy, w, x, g, dres)


def _mlp_up(x, g, w1, *, name):
    s, d = x.shape
    n = w1.shape[1]
    tm = _tile(s, TM_WIDE)

    def body(x_ref, g_ref, w_ref, h_ref, u_ref, act_ref):
        xhat, _ = _rms(x_ref[...], d)
        h = (xhat * g_ref[...]).astype(BF16)
        h_ref[...] = h
        u = _dot(h, w_ref[...])
        u_ref[...] = u.astype(BF16)
        act_ref[...] = jnp.square(jnp.maximum(u, 0.0)).astype(BF16)

    return pl.pallas_call(
        body, name=name, grid=(s // tm,),
        in_specs=[pl.BlockSpec((tm, d), lambda i: (i, 0)), pl.BlockSpec((1, d), lambda i: (0, 0)),
                  pl.BlockSpec((d, n), lambda i: (0, 0))],
        out_specs=[pl.BlockSpec((tm, d), lambda i: (i, 0)), pl.BlockSpec((tm, n), lambda i: (i, 0)),
                   pl.BlockSpec((tm, n), lambda i: (i, 0))],
        out_shape=[jax.ShapeDtypeStruct((s, d), BF16), jax.ShapeDtypeStruct((s, n), BF16),
                   jax.ShapeDtypeStruct((s, n), BF16)],
        compiler_params=_cparams("parallel"),
    )(x, g, w1)


def _mlp_down_bwd(dy, w2, u, *, name):
    s, d = dy.shape
    n = w2.shape[0]
    tm = _tile(s, TM_WIDE)

    def body(dy_ref, w_ref, u_ref, du_ref):
        dact = _dot_nt(dy_ref[...].astype(BF16), w_ref[...])
        du_ref[...] = (dact * (2.0 * jnp.maximum(u_ref[...].astype(F32), 0.0))).astype(BF16)

    return pl.pallas_call(
        body, name=name, grid=(s // tm,),
        in_specs=[pl.BlockSpec((tm, d), lambda i: (i, 0)), pl.BlockSpec((n, d), lambda i: (0, 0)),
                  pl.BlockSpec((tm, n), lambda i: (i, 0))],
        out_specs=pl.BlockSpec((tm, n), lambda i: (i, 0)),
        out_shape=jax.ShapeDtypeStruct((s, n), BF16),
        compiler_params=_cparams("parallel"),
    )(dy, w2, u)


def _conv_gate(bcu, conv_w, *, name):
    s = bcu.shape[0]
    d = D_MODEL
    tm = _tile(s, TM)
    hb = tm // 8

    def body(bcu_ref, prev_ref, w_ref, z_ref, pbuf):
        i = pl.program_id(0)
        gb = bcu_ref[:, 0:d]
        p = bcu_ref[:, d:2 * d] * bcu_ref[:, 2 * d:3 * d]
        pprev = prev_ref[:, d:2 * d] * prev_ref[:, 2 * d:3 * d]
        pbuf[0:8, :] = jnp.where(i > 0, pprev, 0.0)
        pbuf[8:8 + tm, :] = p
        cv = w_ref[2:3, :] * p + w_ref[1:2, :] * pbuf[7:7 + tm, :] + w_ref[0:1, :] * pbuf[6:6 + tm, :]
        z_ref[...] = (gb * cv).astype(BF16)

    return pl.pallas_call(
        body, name=name, grid=(s // tm,),
        in_specs=[pl.BlockSpec((tm, 3 * d), lambda i: (i, 0)),
                  pl.BlockSpec((8, 3 * d), lambda i: (jnp.maximum(i * hb - 1, 0), 0)),
                  pl.BlockSpec((3, d), lambda i: (0, 0))],
        out_specs=pl.BlockSpec((tm, d), lambda i: (i, 0)),
        out_shape=jax.ShapeDtypeStruct((s, d), BF16),
        scratch_shapes=[pltpu.VMEM((tm + 8, d), F32)],
        compiler_params=_cparams("parallel"),
    )(bcu, bcu, conv_w)


def _conv_gate_bwd(bcu, dz, conv_w, *, name):
    s = bcu.shape[0]
    d = D_MODEL
    tm = _tile(s, TM)
    hb = tm // 8
    nt = s // tm

    def body(bcu_ref, prev_ref, next_ref, dz_ref, dznext_ref, w_ref, dbcu_ref, dw_ref, pbuf, dbuf):
        i = pl.program_id(0)

        @pl.when(i == 0)
        def _():
            dw_ref[...] = jnp.zeros_like(dw_ref)

        gb = bcu_ref[:, 0:d]
        gc = bcu_ref[:, d:2 * d]
        uu = bcu_ref[:, 2 * d:3 * d]
        p = gc * uu
        pprev = prev_ref[:, d:2 * d] * prev_ref[:, 2 * d:3 * d]
        pbuf[0:8, :] = jnp.where(i > 0, pprev, 0.0)
        pbuf[8:8 + tm, :] = p
        p1 = pbuf[7:7 + tm, :]
        p2 = pbuf[6:6 + tm, :]
        cv = w_ref[2:3, :] * p + w_ref[1:2, :] * p1 + w_ref[0:1, :] * p2
        dz_t = dz_ref[...]
        dcv = dz_t * gb
        dcv_next = dznext_ref[...] * next_ref[:, 0:d]
        dbuf[0:tm, :] = dcv
        dbuf[tm:tm + 8, :] = jnp.where(i < nt - 1, dcv_next, 0.0)
        dp = w_ref[2:3, :] * dcv + w_ref[1:2, :] * dbuf[1:1 + tm, :] + w_ref[0:1, :] * dbuf[2:2 + tm, :]
        dw_ref[2:3, :] += jnp.sum(dcv * p, axis=0, keepdims=True)
        dw_ref[1:2, :] += jnp.sum(dcv * p1, axis=0, keepdims=True)
        dw_ref[0:1, :] += jnp.sum(dcv * p2, axis=0, keepdims=True)
        dbcu_ref[:, 0:d] = (dz_t * cv).astype(BF16)
        dbcu_ref[:, d:2 * d] = (dp * uu).astype(BF16)
        dbcu_ref[:, 2 * d:3 * d] = (dp * gc).astype(BF16)

    nxt = lambda i: (jnp.minimum((i + 1) * hb, s // 8 - 1), 0)
    return pl.pallas_call(
        body, name=name, grid=(nt,),
        in_specs=[pl.BlockSpec((tm, 3 * d), lambda i: (i, 0)),
                  pl.BlockSpec((8, 3 * d), lambda i: (jnp.maximum(i * hb - 1, 0), 0)),
                  pl.BlockSpec((8, 3 * d), nxt),
                  pl.BlockSpec((tm, d), lambda i: (i, 0)),
                  pl.BlockSpec((8, d), nxt),
                  pl.BlockSpec((3, d), lambda i: (0, 0))],
        out_specs=[pl.BlockSpec((tm, 3 * d), lambda i: (i, 0)), pl.BlockSpec((3, d), lambda i: (0, 0))],
        out_shape=[jax.ShapeDtypeStruct((s, 3 * d), BF16), jax.ShapeDtypeStruct((3, d), F32)],
        scratch_shapes=[pltpu.VMEM((tm + 8, d), F32), pltpu.VMEM((tm + 8, d), F32)],
        compiler_params=_cparams("arbitrary"),
    )(bcu, bcu, bcu, dz, dz, conv_w)


def _mla_prep(a, g_qa, g_kva, w_uq, w_ukv, g_q, g_k, cc, sa, sb, *, name):
    s = a.shape[0]
    ts = _tile(s, T_PREP)

    def body(a_ref, gqa_ref, gkva_ref, wuq_ref, wukv_ref, gq_ref, gk_ref, cc_ref, sa_ref, sb_ref,
             cq_ref, ckv_ref, q_ref, k_ref, v_ref):
        xq, _ = _rms(a_ref[:, 0:Q_RANK], Q_RANK)
        cq = (xq * gqa_ref[...]).astype(BF16)
        cq_ref[...] = cq
        xkv, _ = _rms(a_ref[:, Q_RANK:Q_RANK + KV_RANK], KV_RANK)
        ckv = (xkv * gkva_ref[...]).astype(BF16)
        ckv_ref[...] = ckv
        kpe = a_ref[:, Q_RANK + KV_RANK:DOWN_PAD]
        kpe_ss = jnp.sum(kpe * kpe, axis=-1, keepdims=True)
        cc_t, sa_t, sb_t = cc_ref[...], sa_ref[...], sb_ref[...]
        gq = gq_ref[...]
        gk = gk_ref[...]
        for h in range(N_HEADS):
            cols = slice(h * QK_PAD, (h + 1) * QK_PAD)
            qhat, _ = _rms(_dot(cq, wuq_ref[:, cols]), QK_DIM)
            qn = qhat * gq
            q_ref[h, :, 0:QK_NOPE] = qn[:, 0:QK_NOPE].astype(BF16)
            q_ref[h, :, QK_NOPE:QK_PAD] = _rope(qn[:, QK_NOPE:QK_PAD], cc_t, sa_t, sb_t).astype(BF16)
            kvr = _dot(ckv, wukv_ref[:, cols])
            kn = kvr[:, 0:QK_NOPE]
            rk = lax.rsqrt((jnp.sum(kn * kn, axis=-1, keepdims=True) + kpe_ss) * (1.0 / QK_DIM) + EPS)
            k_ref[h, :, 0:QK_NOPE] = (kn * rk * gk[:, 0:QK_NOPE]).astype(BF16)
            k_ref[h, :, QK_NOPE:QK_PAD] = _rope(kpe * rk * gk[:, QK_NOPE:QK_PAD], cc_t, sa_t, sb_t).astype(BF16)
            v_ref[h, :, :] = kvr[:, QK_NOPE:QK_PAD].astype(BF16)

    row = lambda i: (i, 0)
    fixed = lambda i: (0, 0)
    head = lambda i: (0, i, 0)
    return pl.pallas_call(
        body, name=name, grid=(s // ts,),
        in_specs=[pl.BlockSpec((ts, DOWN_PAD), row), pl.BlockSpec((1, Q_RANK), fixed), pl.BlockSpec((1, KV_RANK), fixed),
                  pl.BlockSpec((Q_RANK, N_HEADS * QK_PAD), fixed), pl.BlockSpec((KV_RANK, N_HEADS * QK_PAD), fixed),
                  pl.BlockSpec((1, QK_PAD), fixed), pl.BlockSpec((1, QK_PAD), fixed),
                  pl.BlockSpec((ts, 128), row), pl.BlockSpec((ts, 128), row), pl.BlockSpec((ts, 128), row)],
        out_specs=[pl.BlockSpec((ts, Q_RANK), row), pl.BlockSpec((ts, KV_RANK), row),
                   pl.BlockSpec((N_HEADS, ts, QK_PAD), head), pl.BlockSpec((N_HEADS, ts, QK_PAD), head),
                   pl.BlockSpec((N_HEADS, ts, V_DIM), head)],
        out_shape=[jax.ShapeDtypeStruct((s, Q_RANK), BF16), jax.ShapeDtypeStruct((s, KV_RANK), BF16),
                   jax.ShapeDtypeStruct((N_HEADS, s, QK_PAD), BF16), jax.ShapeDtypeStruct((N_HEADS, s, QK_PAD), BF16),
                   jax.ShapeDtypeStruct((N_HEADS, s, V_DIM), BF16)],
        compiler_params=_cparams("parallel"),
    )(a, g_qa, g_kva, w_uq, w_ukv, g_q, g_k, cc, sa, sb)


def _mla_prep_bwd(a, g_qa, g_kva, w_uq, w_ukv, g_q, g_k, cc, sa, sb, dq, dk, dv, *, name):
    s = a.shape[0]
    ts = _tile(s, T_PREP)

    def body(a_ref, gqa_ref, gkva_ref, wuq_ref, wukv_ref, gq_ref, gk_ref, cc_ref, sa_ref, sb_ref,
             dq_ref, dk_ref, dv_ref, dqr_ref, dkvr_ref, da_ref, dgq_ref, dgk_ref, dgqa_ref, dgkva_ref):
        @pl.when(pl.program_id(0) == 0)
        def _():
            dgq_ref[...] = jnp.zeros_like(dgq_ref)
            dgk_ref[...] = jnp.zeros_like(dgk_ref)
            dgqa_ref[...] = jnp.zeros_like(dgqa_ref)
            dgkva_ref[...] = jnp.zeros_like(dgkva_ref)

        xq, r_q = _rms(a_ref[:, 0:Q_RANK], Q_RANK)
        cq = (xq * gqa_ref[...]).astype(BF16)
        xkv, r_kv = _rms(a_ref[:, Q_RANK:Q_RANK + KV_RANK], KV_RANK)
        ckv = (xkv * gkva_ref[...]).astype(BF16)
        kpe = a_ref[:, Q_RANK + KV_RANK:DOWN_PAD]
        kpe_ss = jnp.sum(kpe * kpe, axis=-1, keepdims=True)
        cc_t, sa_t, sb_t = cc_ref[...], sa_ref[...], sb_ref[...]
        gq = gq_ref[...]
        gk = gk_ref[...]
        dcq = jnp.zeros((ts, Q_RANK), F32)
        dckv = jnp.zeros((ts, KV_RANK), F32)
        dkpe = jnp.zeros((ts, 128), F32)
        dgq = jnp.zeros((1, QK_PAD), F32)
        dgk_n = jnp.zeros((1, QK_NOPE), F32)
        dgk_p = jnp.zeros((1, 128), F32)
        for h in range(N_HEADS):
            cols = slice(h * QK_PAD, (h + 1) * QK_PAD)
            qhat, rq = _rms(_dot(cq, wuq_ref[:, cols]), QK_DIM)
            dqn = jnp.concatenate(
                [dq_ref[h, :, 0:QK_NOPE], _rope_t(dq_ref[h, :, QK_NOPE:QK_PAD], cc_t, sa_t, sb_t)], axis=1)
            dgq = dgq + jnp.sum(dqn * qhat, axis=0, keepdims=True)
            dqr = _rms_bwd(qhat, rq, dqn * gq, QK_DIM).astype(BF16)
            dqr_ref[:, cols] = dqr
            dcq = dcq + _dot_nt(dqr, wuq_ref[:, cols])
            kn = _dot(ckv, wukv_ref[:, h * QK_PAD:h * QK_PAD + QK_NOPE])
            rk = lax.rsqrt((jnp.sum(kn * kn, axis=-1, keepdims=True) + kpe_ss) * (1.0 / QK_DIM) + EPS)
            khat_n = kn * rk
            khat_p = kpe * rk
            dkn = dk_ref[h, :, 0:QK_NOPE]
            dkp = _rope_t(dk_ref[h, :, QK_NOPE:QK_PAD], cc_t, sa_t, sb_t)
            dgk_n = dgk_n + jnp.sum(dkn * khat_n, axis=0, keepdims=True)
            dgk_p = dgk_p + jnp.sum(dkp * khat_p, axis=0, keepdims=True)
            dxn = dkn * gk[:, 0:QK_NOPE]
            dxp = dkp * gk[:, QK_NOPE:QK_PAD]
            mean = (jnp.sum(dxn * khat_n, axis=-1, keepdims=True)
                    + jnp.sum(dxp * khat_p, axis=-1, keepdims=True)) * (1.0 / QK_DIM)
            dkpe = dkpe + rk * (dxp - khat_p * mean)
            dkvr = jnp.concatenate([rk * (dxn - khat_n * mean), dv_ref[h, :, :]], axis=1).astype(BF16)
            dkvr_ref[:, cols] = dkvr
            dckv = dckv + _dot_nt(dkvr, wukv_ref[:, cols])
        dgq_ref[...] += dgq
        dgk_ref[:, 0:QK_NOPE] += dgk_n
        dgk_ref[:, QK_NOPE:QK_PAD] += dgk_p
        dgqa_ref[...] += jnp.sum(dcq * xq, axis=0, keepdims=True)
        dgkva_ref[...] += jnp.sum(dckv * xkv, axis=0, keepdims=True)
        da_ref[:, 0:Q_RANK] = _rms_bwd(xq, r_q, dcq * gqa_ref[...], Q_RANK).astype(BF16)
        da_ref[:, Q_RANK:Q_RANK + KV_RANK] = _rms_bwd(xkv, r_kv, dckv * gkva_ref[...], KV_RANK).astype(BF16)
        da_ref[:, Q_RANK + KV_RANK:DOWN_PAD] = dkpe.astype(BF16)

    row = lambda i: (i, 0)
    fixed = lambda i: (0, 0)
    head = lambda i: (0, i, 0)
    wide = N_HEADS * QK_PAD
    return pl.pallas_call(
        body, name=name, grid=(s // ts,),
        in_specs=[pl.BlockSpec((ts, DOWN_PAD), row), pl.BlockSpec((1, Q_RANK), fixed), pl.BlockSpec((1, KV_RANK), fixed),
                  pl.BlockSpec((Q_RANK, wide), fixed), pl.BlockSpec((KV_RANK, wide), fixed),
                  pl.BlockSpec((1, QK_PAD), fixed), pl.BlockSpec((1, QK_PAD), fixed),
                  pl.BlockSpec((ts, 128), row), pl.BlockSpec((ts, 128), row), pl.BlockSpec((ts, 128), row),
                  pl.BlockSpec((N_HEADS, ts, QK_PAD), head), pl.BlockSpec((N_HEADS, ts, QK_PAD), head),
                  pl.BlockSpec((N_HEADS, ts, V_DIM), head)],
        out_specs=[pl.BlockSpec((ts, wide), row), pl.BlockSpec((ts, wide), row), pl.BlockSpec((ts, DOWN_PAD), row),
                   pl.BlockSpec((1, QK_PAD), fixed), pl.BlockSpec((1, QK_PAD), fixed),
                   pl.BlockSpec((1, Q_RANK), fixed), pl.BlockSpec((1, KV_RANK), fixed)],
        out_shape=[jax.ShapeDtypeStruct((s, wide), BF16), jax.ShapeDtypeStruct((s, wide), BF16),
                   jax.ShapeDtypeStruct((s, DOWN_PAD), BF16),
                   jax.ShapeDtypeStruct((1, QK_PAD), F32), jax.ShapeDtypeStruct((1, QK_PAD), F32),
                   jax.ShapeDtypeStruct((1, Q_RANK), F32), jax.ShapeDtypeStruct((1, KV_RANK), F32)],
        compiler_params=_cparams("arbitrary"),
    )(a, g_qa, g_kva, w_uq, w_ukv, g_q, g_k, cc, sa, sb, dq, dk, dv)


def _flash_fwd(q, k, v, pos_col, pos_row, *, name):
    nh, s, _ = q.shape
    tq = _tile(s, TQ)
    tk = _tile(s, TK)
    nq, nk = s // tq, s // tk
    k_last = lambda qb: (qb * tq + tq - 1) // tk

    def body(q_ref, k_ref, v_ref, pq_ref, pk_ref, o_ref, lse_ref, m_sc, l_sc, acc_sc):
        qb, kb = pl.program_id(1), pl.program_id(2)

        @pl.when(kb == 0)
        def _():
            m_sc[...] = jnp.full_like(m_sc, NEG)
            l_sc[...] = jnp.zeros_like(l_sc)
            acc_sc[...] = jnp.zeros_like(acc_sc)

        @pl.when(kb <= k_last(qb))
        def _():
            sc = _dot_nt(q_ref[0], k_ref[0]) * SCALE
            sc = jnp.where(pq_ref[...] >= pk_ref[...], sc, NEG)
            m_prev = m_sc[...]
            m_new = jnp.maximum(m_prev, jnp.max(sc, axis=-1, keepdims=True))
            alpha = jnp.exp(m_prev - m_new)
            p = jnp.exp(sc - m_new)
            l_sc[...] = alpha * l_sc[...] + jnp.sum(p, axis=-1, keepdims=True)
            acc_sc[...] = alpha * acc_sc[...] + _dot(p.astype(BF16), v_ref[0])
            m_sc[...] = m_new

        @pl.when(kb == nk - 1)
        def _():
            o_ref[...] = (acc_sc[...] / l_sc[...]).astype(BF16)
            lse_ref[0] = m_sc[...] + jnp.log(l_sc[...])

    kmap = lambda h, qb, kb: (h, jnp.minimum(kb, k_last(qb)), 0)
    return pl.pallas_call(
        body, name=name, grid=(nh, nq, nk),
        in_specs=[pl.BlockSpec((1, tq, QK_PAD), lambda h, qb, kb: (h, qb, 0)),
                  pl.BlockSpec((1, tk, QK_PAD), kmap), pl.BlockSpec((1, tk, V_DIM), kmap),
                  pl.BlockSpec((tq, 1), lambda h, qb, kb: (qb, 0)),
                  pl.BlockSpec((1, tk), lambda h, qb, kb: (0, jnp.minimum(kb, k_last(qb))))],
        out_specs=[pl.BlockSpec((tq, V_DIM), lambda h, qb, kb: (qb, h)),
                   pl.BlockSpec((1, tq, 1), lambda h, qb, kb: (h, qb, 0))],
        scratch_shapes=[pltpu.VMEM((tq, 1), F32), pltpu.VMEM((tq, 1), F32), pltpu.VMEM((tq, V_DIM), F32)],
        out_shape=[jax.ShapeDtypeStruct((s, nh * V_DIM), BF16), jax.ShapeDtypeStruct((nh, s, 1), F32)],
        compiler_params=_cparams("parallel", "parallel", "arbitrary"),
    )(q, k, v, pos_col, pos_row)


def _attn_delta(do, o, *, name):
    s = do.shape[0]
    tm = _tile(s, TM)

    def body(do_ref, o_ref, d_ref):
        for h in range(N_HEADS):
            cols = slice(h * V_DIM, (h + 1) * V_DIM)
            d_ref[h] = jnp.sum(do_ref[:, cols].astype(F32) * o_ref[:, cols].astype(F32), axis=-1, keepdims=True)

    return pl.pallas_call(
        body, name=name, grid=(s // tm,),
        in_specs=[pl.BlockSpec((tm, N_HEADS * V_DIM), lambda i: (i, 0))] * 2,
        out_specs=pl.BlockSpec((N_HEADS, tm, 1), lambda i: (0, i, 0)),
        out_shape=jax.ShapeDtypeStruct((N_HEADS, s, 1), F32),
        compiler_params=_cparams("parallel"),
    )(do, o)


def _flash_bwd(q, k, v, do, lse_row, delta_row, pos_col, pos_row, *, name):
    nh, s, _ = q.shape
    tq = _tile(s, TQ)
    tk = _tile(s, TK)
    nq, nk = s // tq, s // tk
    q_first = lambda kb: (kb * tk) // tq

    def body(q_ref, k_ref, v_ref, do_ref, lse_ref, delta_ref, pq_ref, pk_ref,
             dq_ref, dk_ref, dv_ref, dk_sc, dv_sc):
        kb, qb = pl.program_id(1), pl.program_id(2)

        @pl.when((kb == 0) & (qb == 0))
        def _():
            dq_ref[...] = jnp.zeros_like(dq_ref)

        @pl.when(qb == 0)
        def _():
            dk_sc[...] = jnp.zeros_like(dk_sc)
            dv_sc[...] = jnp.zeros_like(dv_sc)

        @pl.when(qb >= q_first(kb))
        def _():
            qt, kt, dot_ = q_ref[0], k_ref[0], do_ref[...]
            st = _dot_nt(kt, qt) * SCALE
            pt = jnp.where(pq_ref[...] >= pk_ref[...], jnp.exp(st - lse_ref[0]), 0.0)
            dv_sc[...] += _dot(pt.astype(BF16), dot_)
            dpt = _dot_nt(v_ref[0], dot_)
            dst = (pt * (dpt - delta_ref[0]) * SCALE).astype(BF16)
            dk_sc[...] += _dot(dst, qt)
            rows = pl.ds(pl.multiple_of(qb * tq, tq), tq)
            dq_ref[0, rows, :] += _dot_tn(dst, kt)

        @pl.when(qb == nq - 1)
        def _():
            dk_ref[0] = dk_sc[...]
            dv_ref[0] = dv_sc[...]

    qeff = lambda kb, qb: jnp.maximum(qb, q_first(kb))
    return pl.pallas_call(
        body, name=name, grid=(nh, nk, nq),
        in_specs=[pl.BlockSpec((1, tq, QK_PAD), lambda h, kb, qb: (h, qeff(kb, qb), 0)),
                  pl.BlockSpec((1, tk, QK_PAD), lambda h, kb, qb: (h, kb, 0)),
                  pl.BlockSpec((1, tk, V_DIM), lambda h, kb, qb: (h, kb, 0)),
                  pl.BlockSpec((tq, V_DIM), lambda h, kb, qb: (qeff(kb, qb), h)),
                  pl.BlockSpec((1, 1, tq), lambda h, kb, qb: (h, 0, qeff(kb, qb))),
                  pl.BlockSpec((1, 1, tq), lambda h, kb, qb: (h, 0, qeff(kb, qb))),
                  pl.BlockSpec((1, tq), lambda h, kb, qb: (0, qeff(kb, qb))),
                  pl.BlockSpec((tk, 1), lambda h, kb, qb: (kb, 0))],
        out_specs=[pl.BlockSpec((1, s, QK_PAD), lambda h, kb, qb: (h, 0, 0)),
                   pl.BlockSpec((1, tk, QK_PAD), lambda h, kb, qb: (h, kb, 0)),
                   pl.BlockSpec((1, tk, V_DIM), lambda h, kb, qb: (h, kb, 0))],
        scratch_shapes=[pltpu.VMEM((tk, QK_PAD), F32), pltpu.VMEM((tk, V_DIM), F32)],
        out_shape=[jax.ShapeDtypeStruct((nh, s, QK_PAD), F32), jax.ShapeDtypeStruct((nh, s, QK_PAD), F32),
                   jax.ShapeDtypeStruct((nh, s, V_DIM), F32)],
        compiler_params=_cparams("arbitrary", "arbitrary", "arbitrary"),
    )(q, k, v, do, lse_row, delta_row, pos_row, pos_col)


def _loss_head(y, target, *, name):
    s, d = y.shape
    tm = _tile(s, TM)
    nt = s // tm

    def body(y_ref, t_ref, dy_ref, loss_ref, acc):
        i = pl.program_id(0)

        @pl.when(i == 0)
        def _():
            acc[...] = jnp.zeros_like(acc)

        e = y_ref[...] - t_ref[...]
        dy_ref[...] = e * (1.0 / d)
        acc[...] += jnp.sum((e * e).reshape(tm // 8, 8, d), axis=0)

        @pl.when(i == nt - 1)
        def _():
            loss_ref[...] = jnp.full((1, 128), 0.5 / d, F32) * jnp.sum(acc[...])

    return pl.pallas_call(
        body, name=name, grid=(nt,),
        in_specs=[pl.BlockSpec((tm, d), lambda i: (i, 0))] * 2,
        out_specs=[pl.BlockSpec((tm, d), lambda i: (i, 0)), pl.BlockSpec((1, 128), lambda i: (0, 0))],
        out_shape=[jax.ShapeDtypeStruct((s, d), F32), jax.ShapeDtypeStruct((1, 128), F32)],
        scratch_shapes=[pltpu.VMEM((8, d), F32)],
        compiler_params=_cparams("arbitrary"),
    )(y, target)


def _adamw(w, g, m, v, *, name):
    r, c = w.shape
    tr = _tile(r, 512) if r % 8 == 0 else r

    def body(w_ref, g_ref, m_ref, v_ref, d_ref, nm_ref, nv_ref):
        g_t = g_ref[...]
        nm = ADAM_B1 * m_ref[...] + (1.0 - ADAM_B1) * g_t
        nv = ADAM_B2 * v_ref[...] + (1.0 - ADAM_B2) * (g_t * g_t)
        m_hat = nm / (1.0 - ADAM_B1 ** ADAM_STEP)
        v_hat = nv / (1.0 - ADAM_B2 ** ADAM_STEP)
        d_ref[...] = -ADAM_LR * (m_hat / (jnp.sqrt(v_hat) + ADAM_EPS) + ADAM_WD * w_ref[...])
        nm_ref[...] = nm
        nv_ref[...] = nv

    spec = pl.BlockSpec((tr, c), lambda i: (i, 0))
    return pl.pallas_call(
        body, name=name, grid=(r // tr,), in_specs=[spec] * 4, out_specs=[spec] * 3,
        out_shape=[jax.ShapeDtypeStruct((r, c), F32)] * 3,
        compiler_params=_cparams("parallel"),
    )(w, g, m, v)


def _place():
    return lax.axis_index("x"), lax.axis_index("y"), lax.axis_index("c")


def _other_chips(x, y):
    return [(1 - x, y), (x, 1 - y), (1 - x, 1 - y)]


ANY = pl.BlockSpec(memory_space=pl.ANY)


def _gather_weights(wp):
    _, rows, cols = wp.shape

    def body(wp_ref, out_ref, send_sems, recv_sems, local_sem):
        x, y, c = _place()
        chips = _other_chips(x, y)
        me = 2 * x + y

        def copy(k, src, chip_idx, half, to):
            return pltpu.make_async_remote_copy(
                src_ref=src, dst_ref=out_ref.at[chip_idx, half], send_sem=send_sems.at[k], recv_sem=recv_sems.at[k],
                device_id=to, device_id_type=MESH)

        mine = pltpu.make_async_copy(wp_ref, out_ref.at[me], local_sem)
        mine.start()
        first = [copy(j, wp_ref.at[c], me, c, (*chip, c)) for j, chip in enumerate(chips)]
        for cp in first:
            cp.start()
        passed = []
        for j, (cx, cy) in enumerate(chips):
            idx = 2 * cx + cy
            copy(j, wp_ref.at[c], idx, c, (x, y, c)).wait_recv()
            fwd = copy(3 + j, out_ref.at[idx, c], idx, c, (x, y, 1 - c))
            fwd.start()
            passed.append(fwd)
        for j, (cx, cy) in enumerate(chips):
            copy(3 + j, wp_ref.at[c], 2 * cx + cy, 1 - c, (x, y, c)).wait_recv()
        for cp in first + passed:
            cp.wait_send()
        mine.wait()

    return pl.pallas_call(
        body, name="gather_weights", in_specs=[ANY], out_specs=ANY,
        out_shape=jax.ShapeDtypeStruct((N_CHIPS, 2, rows, cols), wp.dtype),
        scratch_shapes=[pltpu.SemaphoreType.DMA((6,)), pltpu.SemaphoreType.DMA((6,)), pltpu.SemaphoreType.DMA],
    )(wp)


def _swap_halves(g):
    _, nseg, rows, cols = g.shape

    def body(g_ref, out_ref, send_sem, recv_sem):
        x, y, c = _place()
        cp = pltpu.make_async_remote_copy(
            src_ref=g_ref.at[1 - c], dst_ref=out_ref, send_sem=send_sem, recv_sem=recv_sem,
            device_id=(x, y, 1 - c), device_id_type=MESH)
        cp.start()
        cp.wait()

    return pl.pallas_call(
        body, name="grad_swap_halves", in_specs=[ANY], out_specs=ANY,
        out_shape=jax.ShapeDtypeStruct((nseg, rows, cols), g.dtype),
        scratch_shapes=[pltpu.SemaphoreType.DMA, pltpu.SemaphoreType.DMA],
    )(g)


def _chip_sum(g, r1, half):
    _, nseg, rows, cols = g.shape
    tr = _tile(rows, SUM_ROWS)

    def body(half_ref, g_ref, r_ref, o_ref):
        o_ref[...] = (g_ref[0] + r_ref[...]).astype(BF16)

    return pl.pallas_call(
        body, name="grad_chip_sum",
        grid_spec=pltpu.PrefetchScalarGridSpec(
            num_scalar_prefetch=1, grid=(nseg, rows // tr),
            in_specs=[pl.BlockSpec((1, 1, tr, cols), lambda q, i, hf: (hf[0], q, i, 0)),
                      pl.BlockSpec((1, tr, cols), lambda q, i, hf: (q, i, 0))],
            out_specs=pl.BlockSpec((1, tr, cols), lambda q, i, hf: (q, i, 0))),
        out_shape=jax.ShapeDtypeStruct((nseg, rows, cols), BF16),
        compiler_params=_cparams("parallel", "parallel"),
    )(half, g, r1)


def _scatter_partials(pb):
    _, rows, cols = pb.shape

    def body(pb_ref, out_ref, send_sems, recv_sems):
        x, y, c = _place()
        copies = []
        for j, (cx, cy) in enumerate(_other_chips(x, y)):
            cp = pltpu.make_async_remote_copy(
                src_ref=pb_ref.at[2 * cx + cy], dst_ref=out_ref.at[j], send_sem=send_sems.at[j],
                recv_sem=recv_sems.at[j], device_id=(cx, cy, c), device_id_type=MESH)
            cp.start()
            copies.append(cp)
        for cp in copies:
            cp.wait()

    return pl.pallas_call(
        body, name="grad_scatter_partials", in_specs=[ANY], out_specs=ANY,
        out_shape=jax.ShapeDtypeStruct((3, rows, cols), pb.dtype),
        scratch_shapes=[pltpu.SemaphoreType.DMA((3,)), pltpu.SemaphoreType.DMA((3,))],
    )(pb)


def _final_sum(g, r1, r2, sel):
    _, _, rows, cols = g.shape
    tr = _tile(rows, SUM_ROWS)

    def body(sel_ref, g_ref, r1_ref, r2_ref, o_ref):
        acc = g_ref[0, 0] + r1_ref[0]
        for j in range(3):
            acc = acc + r2_ref[j].astype(F32)
        o_ref[...] = acc

    return pl.pallas_call(
        body, name="grad_final_sum",
        grid_spec=pltpu.PrefetchScalarGridSpec(
            num_scalar_prefetch=1, grid=(rows // tr,),
            in_specs=[pl.BlockSpec((1, 1, tr, cols), lambda i, sl: (sl[0], sl[1], i, 0)),
                      pl.BlockSpec((1, tr, cols), lambda i, sl: (sl[1], i, 0)),
                      pl.BlockSpec((3, tr, cols), lambda i, sl: (0, i, 0))],
            out_specs=pl.BlockSpec((tr, cols), lambda i, sl: (i, 0))),
        out_shape=jax.ShapeDtypeStruct((rows, cols), F32),
        compiler_params=_cparams("parallel"),
    )(sel, g, r1, r2)


def _join_halves(f):
    rows, cols = f.shape

    def body(f_ref, out_ref, send_sem, recv_sem, local_sem):
        x, y, c = _place()
        mine = pltpu.make_async_copy(f_ref, out_ref.at[c], local_sem)
        mine.start()
        cp = pltpu.make_async_remote_copy(
            src_ref=f_ref, dst_ref=out_ref.at[c], send_sem=send_sem, recv_sem=recv_sem,
            device_id=(x, y, 1 - c), device_id_type=MESH)
        cp.start()
        cp.wait_send()
        pltpu.make_async_remote_copy(
            src_ref=f_ref, dst_ref=out_ref.at[1 - c], send_sem=send_sem, recv_sem=recv_sem,
            device_id=(x, y, 1 - c), device_id_type=MESH).wait_recv()
        mine.wait()

    return pl.pallas_call(
        body, name="grad_join_halves", in_specs=[ANY], out_specs=ANY,
        out_shape=jax.ShapeDtypeStruct((2, rows, cols), f.dtype),
        scratch_shapes=[pltpu.SemaphoreType.DMA, pltpu.SemaphoreType.DMA, pltpu.SemaphoreType.DMA],
    )(f)


def _all_reduce_small(part, *, name):
    rows, cols = part.shape
    vm = pl.BlockSpec(memory_space=pltpu.VMEM)

    def body(p_ref, o_ref, land, send_sems, recv_sems):
        x, y, c = _place()
        me = 4 * x + 2 * y + c
        flips = [(dx, dy, dc) for dx in (0, 1) for dy in (0, 1) for dc in (0, 1)][1:]
        copies = []
        for k, (dx, dy, dc) in enumerate(flips):
            cp = pltpu.make_async_remote_copy(
                src_ref=p_ref, dst_ref=land.at[me], send_sem=send_sems.at[k], recv_sem=recv_sems.at[k],
                device_id=(1 - x if dx else x, 1 - y if dy else y, 1 - c if dc else c), device_id_type=MESH)
            cp.start()
            copies.append(cp)
        land[me] = p_ref[...]
        for cp in copies:
            cp.wait()
        acc = land[0]
        for j in range(1, 8):
            acc = acc + land[j]
        o_ref[...] = acc

    return pl.pallas_call(
        body, name=name, in_specs=[vm], out_specs=vm,
        out_shape=jax.ShapeDtypeStruct((rows, cols), F32),
        scratch_shapes=[pltpu.VMEM((8, rows, cols), F32), pltpu.SemaphoreType.DMA((7,)), pltpu.SemaphoreType.DMA((7,))],
    )(part)


BIG = {
    "attn_w_down": ((2, 1024, 448), 1), "attn_w_uq": ((2, 256, 1536), 2), "attn_w_ukv": ((2, 128, 2048), 2),
    "attn_w_o": ((2, 1024, 1024), 1), "conv_w_in": ((2, 1024, 3072), 2), "conv_w": ((2, 3, 1024), 2),
    "conv_w_out": ((2, 1024, 1024), 1), "mlp_w1": ((4, 1024, 4096), 2), "mlp_w2": ((4, 4096, 1024), 1),
}
SMALL = {"g_mix": (4, 1024), "g_mlp": (4, 1024), "attn_g_q_a": (2, 256), "attn_g_kv_a": (2, 128),
         "attn_g_qnorm": (2, 192), "attn_g_knorm": (2, 192)}
WEIGHT_ORDER = ["g_mix", "g_mlp", "attn_w_down", "attn_g_q_a", "attn_g_kv_a", "attn_w_uq", "attn_w_ukv",
                "attn_g_qnorm", "attn_g_knorm", "attn_w_o", "conv_w_in", "conv_w", "conv_w_out", "mlp_w1", "mlp_w2"]


def _shard_shape(name):
    shape, axis = BIG[name]
    return tuple(n // N_CHIPS if i == axis else n for i, n in enumerate(shape))


def _prod(shape):
    n = 1
    for v in shape:
        n *= v
    return n


SEG_LEN = sum(_prod(_shard_shape(n)) for n in BIG)
HALF_ROWS = -(-SEG_LEN // (2 * PACK_COLS * SUM_ROWS)) * SUM_ROWS
SEG_PAD = 2 * HALF_ROWS * PACK_COLS


def _pack_segment(parts):
    flat = [parts[n].reshape(-1) for n in BIG]
    dtype = flat[0].dtype
    flat.append(jnp.zeros((SEG_PAD - SEG_LEN,), dtype))
    return jnp.concatenate(flat).reshape(2, HALF_ROWS, PACK_COLS)


def _unpack_segment(seg):
    flat = seg.reshape(-1)
    out, off = {}, 0
    for n in BIG:
        shp = _shard_shape(n)
        out[n] = flat[off:off + _prod(shp)].reshape(shp)
        off += _prod(shp)
    return out


def _split_shards(name, full):
    _, axis = BIG[name]
    return jnp.split(full, N_CHIPS, axis=axis)


SMALL_LEN = sum(_prod(s) for s in SMALL.values())
SMALL_ROWS = -(-SMALL_LEN // (8 * 128)) * 8


def _pack_small(parts):
    flat = [parts[n].reshape(-1) for n in SMALL]
    flat.append(jnp.zeros((SMALL_ROWS * 128 - SMALL_LEN,), F32))
    return jnp.concatenate(flat).reshape(SMALL_ROWS, 128)


def _unpack_small(buf):
    flat = buf.reshape(-1)
    out, off = {}, 0
    for n, shp in SMALL.items():
        out[n] = flat[off:off + _prod(shp)].reshape(shp)
        off += _prod(shp)
    return out


def _rope_tables(positions):
    inv_freq = ROPE_THETA ** (-jnp.arange(0, QK_ROPE, 2, dtype=F32) / QK_ROPE)
    ang = positions.astype(F32)[:, None] * inv_freq
    cos, sin = jnp.cos(ang), jnp.sin(ang)
    z32 = jnp.zeros_like(cos)
    z64 = jnp.zeros((positions.shape[0], 64), F32)
    cc = jnp.concatenate([cos, cos, z64], axis=1)
    sa = jnp.concatenate([-sin, z32, z64], axis=1)
    sb = jnp.concatenate([z32, sin, z64], axis=1)
    return cc, sa, sb


def _pad_heads(w, width):
    k = w.shape[0]
    w = w.reshape(k, N_HEADS, width)
    return jnp.pad(w, ((0, 0), (0, 0), (0, QK_PAD - width))).reshape(k, N_HEADS * QK_PAD)


def _local_step(x, positions, target, wb, gains):
    s = x.shape[0]
    cc, sa, sb = _rope_tables(positions)
    pos_col = positions.reshape(s, 1)
    pos_row = positions.reshape(1, s)

    saved = []
    for i in range(4):
        j = i // 2
        g_mix = gains["g_mix"][i:i + 1]
        g_mlp = gains["g_mlp"][i:i + 1]
        if i % 2 == 0:
            w_down = jnp.pad(wb["attn_w_down"][j], ((0, 0), (0, DOWN_PAD - DOWN_DIM)))
            w_uq = _pad_heads(wb["attn_w_uq"][j], QK_DIM)
            w_ukv = wb["attn_w_ukv"][j]
            g_qa = gains["attn_g_q_a"][j:j + 1]
            g_kva = gains["attn_g_kv_a"][j:j + 1]
            g_q = jnp.pad(gains["attn_g_qnorm"][j:j + 1], ((0, 0), (0, QK_PAD - QK_DIM)))
            g_k = jnp.pad(gains["attn_g_knorm"][j:j + 1], ((0, 0), (0, QK_PAD - QK_DIM)))
            h, a = _norm_mm(x, g_mix, w_down, out_dtype=F32, name=f"mla_down_{j}")
            cq, ckv, q, k, v = _mla_prep(a, g_qa, g_kva, w_uq, w_ukv, g_q, g_k, cc, sa, sb, name=f"mla_prep_{j}")
            o, lse = _flash_fwd(q, k, v, pos_col, pos_row, name=f"flash_fwd_{j}")
            x_mid = _mm_nn(o, wb["attn_w_o"][j], out_dtype=F32, residual=x, name=f"mla_out_{j}")
            mix = dict(h=h, a=a, cq=cq, ckv=ckv, q=q, k=k, v=v, o=o, lse=lse, w_down=w_down, w_uq=w_uq, w_ukv=w_ukv,
                       g_qa=g_qa, g_kva=g_kva, g_q=g_q, g_k=g_k)
        else:
            h, bcu = _norm_mm(x, g_mix, wb["conv_w_in"][j], out_dtype=F32, name=f"conv_in_{j}")
            z = _conv_gate(bcu, gains["conv_w"][j], name=f"conv_gate_{j}")
            x_mid = _mm_nn(z, wb["conv_w_out"][j], out_dtype=F32, residual=x, name=f"conv_out_{j}")
            mix = dict(h=h, bcu=bcu, z=z)
        h2, u, act = _mlp_up(x_mid, g_mlp, wb["mlp_w1"][i], name=f"mlp_up_{i}")
        x_out = _mm_nn(act, wb["mlp_w2"][i], out_dtype=F32, residual=x_mid, name=f"mlp_down_{i}")
        saved.append(dict(x_in=x, x_mid=x_mid, mix=mix, h2=h2, u=u, act=act, g_mix=g_mix, g_mlp=g_mlp))
        x = x_out

    dx, loss = _loss_head(x, target, name="loss_head")

    gw = {n: [None] * BIG[n][0][0] for n in BIG}
    gs = {n: [None] * SMALL[n][0] for n in SMALL}
    for i in reversed(range(4)):
        j = i // 2
        sv = saved[i]
        mix = sv["mix"]
        du = _mlp_down_bwd(dx, wb["mlp_w2"][i], sv["u"], name=f"mlp_down_bwd_{i}")
        gw["mlp_w2"][i] = _mm_tn(sv["act"], dx, name=f"mlp_w2_grad_{i}")
        gw["mlp_w1"][i] = _mm_tn(sv["h2"], du, name=f"mlp_w1_grad_{i}")
        dx, dg = _nt_rms_bwd(du, wb["mlp_w1"][i], sv["x_mid"], sv["g_mlp"], dx, name=f"mlp_up_bwd_{i}")
        gs["g_mlp"][i] = dg[0]
        if i % 2 == 0:
            do = _mm_nt(dx, wb["attn_w_o"][j], out_dtype=BF16, name=f"mla_out_bwd_{j}")
            gw["attn_w_o"][j] = _mm_tn(mix["o"], dx, name=f"mla_w_o_grad_{j}")
            delta = _attn_delta(do, mix["o"], name=f"attn_delta_{j}")
            lse_row = mix["lse"].reshape(N_HEADS, 1, s)
            delta_row = delta.reshape(N_HEADS, 1, s)
            dq, dk, dv = _flash_bwd(mix["q"], mix["k"], mix["v"], do, lse_row, delta_row, pos_col, pos_row,
                                    name=f"flash_bwd_{j}")
            dqr, dkvr, da, dgq, dgk, dgqa, dgkva = _mla_prep_bwd(
                mix["a"], mix["g_qa"], mix["g_kva"], mix["w_uq"], mix["w_ukv"], mix["g_q"], mix["g_k"], cc, sa, sb,
                dq, dk, dv, name=f"mla_prep_bwd_{j}")
            g_uq = _mm_tn(mix["cq"], dqr, name=f"mla_w_uq_grad_{j}")
            gw["attn_w_uq"][j] = g_uq.reshape(Q_RANK, N_HEADS, QK_PAD)[:, :, :QK_DIM].reshape(Q_RANK, N_HEADS * QK_DIM)
            gw["attn_w_ukv"][j] = _mm_tn(mix["ckv"], dkvr, name=f"mla_w_ukv_grad_{j}")
            gw["attn_w_down"][j] = _mm_tn(mix["h"], da, name=f"mla_w_down_grad_{j}")[:, :DOWN_DIM]
            dx, dg = _nt_rms_bwd(da, mix["w_down"], sv["x_in"], sv["g_mix"], dx, name=f"mla_down_bwd_{j}")
            gs["attn_g_qnorm"][j] = dgq[0, :QK_DIM]
            gs["attn_g_knorm"][j] = dgk[0, :QK_DIM]
            gs["attn_g_q_a"][j] = dgqa[0]
            gs["attn_g_kv_a"][j] = dgkva[0]
        else:
            dz = _mm_nt(dx, wb["conv_w_out"][j], out_dtype=F32, name=f"conv_out_bwd_{j}")
            gw["conv_w_out"][j] = _mm_tn(mix["z"], dx, name=f"conv_w_out_grad_{j}")
            dbcu, dcw = _conv_gate_bwd(mix["bcu"], dz, gains["conv_w"][j], name=f"conv_gate_bwd_{j}")
            gw["conv_w"][j] = dcw
            gw["conv_w_in"][j] = _mm_tn(mix["h"], dbcu, name=f"conv_w_in_grad_{j}")
            dx, dg = _nt_rms_bwd(dbcu, wb["conv_w_in"][j], sv["x_in"], sv["g_mix"], dx, name=f"conv_in_bwd_{j}")
        gs["g_mix"][i] = dg[0]

    grads_big = {n: jnp.stack(v) for n, v in gw.items()}
    grads_small = {n: jnp.stack(v) for n, v in gs.items()}
    return loss, dx, grads_big, grads_small


def kernel(x, positions, g_mix, g_mlp, attn_w_down, attn_g_q_a, attn_g_kv_a, attn_w_uq, attn_w_ukv, attn_g_qnorm, attn_g_knorm, attn_w_o, conv_w_in, conv_w, conv_w_out, mlp_w1, mlp_w2, loss_target, m_g_mix, m_g_mlp, m_attn_w_down, m_attn_g_q_a, m_attn_g_kv_a, m_attn_w_uq, m_attn_w_ukv, m_attn_g_qnorm, m_attn_g_knorm, m_attn_w_o, m_conv_w_in, m_conv_w, m_conv_w_out, m_mlp_w1, m_mlp_w2, v_g_mix, v_g_mlp, v_attn_w_down, v_attn_g_q_a, v_attn_g_kv_a, v_attn_w_uq, v_attn_w_ukv, v_attn_g_qnorm, v_attn_g_knorm, v_attn_w_o, v_conv_w_in, v_conv_w, v_conv_w_out, v_mlp_w1, v_mlp_w2):
    w = dict(g_mix=g_mix, g_mlp=g_mlp, attn_w_down=attn_w_down, attn_g_q_a=attn_g_q_a, attn_g_kv_a=attn_g_kv_a,
             attn_w_uq=attn_w_uq, attn_w_ukv=attn_w_ukv, attn_g_qnorm=attn_g_qnorm, attn_g_knorm=attn_g_knorm,
             attn_w_o=attn_w_o, conv_w_in=conv_w_in, conv_w=conv_w, conv_w_out=conv_w_out, mlp_w1=mlp_w1, mlp_w2=mlp_w2)
    m = dict(g_mix=m_g_mix, g_mlp=m_g_mlp, attn_w_down=m_attn_w_down, attn_g_q_a=m_attn_g_q_a,
             attn_g_kv_a=m_attn_g_kv_a, attn_w_uq=m_attn_w_uq, attn_w_ukv=m_attn_w_ukv, attn_g_qnorm=m_attn_g_qnorm,
             attn_g_knorm=m_attn_g_knorm, attn_w_o=m_attn_w_o, conv_w_in=m_conv_w_in, conv_w=m_conv_w,
             conv_w_out=m_conv_w_out, mlp_w1=m_mlp_w1, mlp_w2=m_mlp_w2)
    v = dict(g_mix=v_g_mix, g_mlp=v_g_mlp, attn_w_down=v_attn_w_down, attn_g_q_a=v_attn_g_q_a,
             attn_g_kv_a=v_attn_g_kv_a, attn_w_uq=v_attn_w_uq, attn_w_ukv=v_attn_w_ukv, attn_g_qnorm=v_attn_g_qnorm,
             attn_g_knorm=v_attn_g_knorm, attn_w_o=v_attn_w_o, conv_w_in=v_conv_w_in, conv_w=v_conv_w,
             conv_w_out=v_conv_w_out, mlp_w1=v_mlp_w1, mlp_w2=v_mlp_w2)
    cx, cy, cc_ = _place()

    shard_b = {n: w[n].astype(BF16) for n in BIG}
    gathered = _gather_weights(_pack_segment(shard_b))
    per_chip = [_unpack_segment(gathered[q]) for q in range(N_CHIPS)]
    wb = {n: jnp.concatenate([per_chip[q][n] for q in range(N_CHIPS)], axis=BIG[n][1]) for n in BIG}

    cw_cols = conv_w.shape[2]
    placed = lax.dynamic_update_slice(jnp.zeros(BIG["conv_w"][0], F32), conv_w, (0, 0, (2 * cx + cy) * cw_cols))
    conv_w_full = 0.5 * _all_reduce_small(placed.reshape(-1, 128), name="conv_w_gather").reshape(BIG["conv_w"][0])

    gains = {n: w[n] for n in SMALL}
    gains["conv_w"] = conv_w_full

    loss, grad_x, grads_big, grads_small = _local_step(x[0], positions[0], loss_target[0], wb, gains)

    segs = []
    for q in range(N_CHIPS):
        segs.append(_pack_segment({n: _split_shards(n, grads_big[n])[q] for n in BIG}))
    packed = jnp.stack(segs, axis=1)
    half = cc_.astype(jnp.int32).reshape(1)
    seg = (2 * cx + cy).astype(jnp.int32).reshape(1)
    r1 = _swap_halves(packed)
    partial = _chip_sum(packed, r1, half)
    r2 = _scatter_partials(partial)
    reduced_half = _final_sum(packed, r1, r2, jnp.concatenate([half, seg]))
    grad_shards = _unpack_segment(_join_halves(reduced_half))

    small = _unpack_small(_all_reduce_small(_pack_small(grads_small), name="gain_all_reduce"))

    loss_total = lax.psum(loss[0, 0], ("x", "y", "c"))

    grads, deltas, new_m, new_v = {}, {}, {}, {}
    for n in BIG:
        shp = w[n].shape
        two_d = (shp[0] * shp[1], shp[2])
        g2 = grad_shards[n].reshape(two_d)
        d, nm, nv = _adamw(w[n].reshape(two_d), g2, m[n].reshape(two_d), v[n].reshape(two_d), name=f"adamw_{n}")
        grads[n], deltas[n], new_m[n], new_v[n] = grad_shards[n], d.reshape(shp), nm.reshape(shp), nv.reshape(shp)
    d, nm, nv = _adamw(_pack_small({n: w[n] for n in SMALL}), _pack_small(small),
                       _pack_small({n: m[n] for n in SMALL}), _pack_small({n: v[n] for n in SMALL}), name="adamw_gains")
    d, nm, nv = _unpack_small(d), _unpack_small(nm), _unpack_small(nv)
    for n in SMALL:
        grads[n], deltas[n], new_m[n], new_v[n] = small[n], d[n], nm[n], nv[n]

    return (loss_total, grad_x[None],
            *[grads[n] for n in WEIGHT_ORDER], *[deltas[n] for n in WEIGHT_ORDER],
            *[new_m[n] for n in WEIGHT_ORDER], *[new_v[n] for n in WEIGHT_ORDER])
```

```python
import functools

import jax
import jax.numpy as jnp
from jax import lax
from jax.experimental import pallas as pl
from jax.experimental.pallas import tpu as pltpu

F32 = jnp.float32
BF16 = jnp.bfloat16

D_MODEL = 1024
N_HEADS = 8
QK_NOPE = 128
QK_ROPE = 64
QK_DIM = QK_NOPE + QK_ROPE
QK_PAD = 256
V_DIM = 128
Q_RANK = 256
KV_RANK = 128
DOWN_DIM = Q_RANK + KV_RANK + QK_ROPE
DOWN_PAD = 512
D_FF = 4 * D_MODEL
ROPE_THETA = 10000.0
EPS = 1e-6
NEG = -1e30
SCALE = QK_DIM ** -0.5
SCALE_LOG2E = SCALE * 1.4426950408889634
LOG2E = 1.4426950408889634
ATTN_CHAINS = 2

ADAM_LR = 0.001
ADAM_B1 = 0.9
ADAM_B2 = 0.999
ADAM_EPS = 1e-08
ADAM_WD = 0.01
ADAM_STEP = 10

N_CHIPS = 4
PACK_COLS = 1024
MESH = pl.DeviceIdType.MESH

TM = 512
TM_WIDE = 256
FWD_TQ = 1024
FWD_TK = 256
BWD_TQ = 256
BWD_TK = 1024
T_PREP = 256
T_RED = 1024
SUM_ROWS = 640


def _tile(n, pref):
    t = min(n, pref)
    assert n % t == 0, (n, t)
    return t


def _cparams(*sem):
    return pltpu.CompilerParams(dimension_semantics=sem)


def _dot(a, b):
    return jnp.dot(a, b, preferred_element_type=F32)


def _dot_nt(a, b):
    return lax.dot_general(a, b, (((1,), (1,)), ((), ())), preferred_element_type=F32)


def _dot_tn(a, b):
    return lax.dot_general(a, b, (((0,), (0,)), ((), ())), preferred_element_type=F32)


def _rms(x, width):
    r = lax.rsqrt(jnp.sum(x * x, axis=-1, keepdims=True) * (1.0 / width) + EPS)
    return x * r, r


def _rms_bwd(xhat, r, dxhat, width):
    return r * (dxhat - xhat * (jnp.sum(dxhat * xhat, axis=-1, keepdims=True) * (1.0 / width)))


def _rope(t, cc, sa, sb):
    return t * cc + pltpu.roll(t, 96, 1) * sa + pltpu.roll(t, 32, 1) * sb


def _rope_t(g, cc, sa, sb):
    return g * cc + pltpu.roll(g * sa, 32, 1) + pltpu.roll(g * sb, 96, 1)


def _mm_nn(a, b, *, out_dtype, name, residual=None):
    m, k = a.shape
    n = b.shape[1]
    tm = _tile(m, TM)

    def body(*refs):
        if residual is None:
            a_ref, b_ref, o_ref = refs
        else:
            a_ref, b_ref, r_ref, o_ref = refs
        acc = _dot(a_ref[...].astype(BF16), b_ref[...])
        if residual is not None:
            acc = acc + r_ref[...]
        o_ref[...] = acc.astype(o_ref.dtype)

    in_specs = [pl.BlockSpec((tm, k), lambda i: (i, 0)), pl.BlockSpec((k, n), lambda i: (0, 0))]
    args = [a, b]
    if residual is not None:
        in_specs.append(pl.BlockSpec((tm, n), lambda i: (i, 0)))
        args.append(residual)
    return pl.pallas_call(
        body, name=name, grid=(m // tm,), in_specs=in_specs,
        out_specs=pl.BlockSpec((tm, n), lambda i: (i, 0)),
        out_shape=jax.ShapeDtypeStruct((m, n), out_dtype),
        compiler_params=_cparams("parallel"),
    )(*args)


def _mm_nt(a, b, *, out_dtype, name):
    m, k = a.shape
    n = b.shape[0]
    tm = _tile(m, TM)

    def body(a_ref, b_ref, o_ref):
        o_ref[...] = _dot_nt(a_ref[...].astype(BF16), b_ref[...]).astype(o_ref.dtype)

    return pl.pallas_call(
        body, name=name, grid=(m // tm,),
        in_specs=[pl.BlockSpec((tm, k), lambda i: (i, 0)), pl.BlockSpec((n, k), lambda i: (0, 0))],
        out_specs=pl.BlockSpec((tm, n), lambda i: (i, 0)),
        out_shape=jax.ShapeDtypeStruct((m, n), out_dtype),
        compiler_params=_cparams("parallel"),
    )(a, b)


def _mm_tn(a, b, *, name):
    s, ka = a.shape
    n = b.shape[1]
    ts = _tile(s, T_RED)
    tka = _tile(ka, 1024)
    tn = _tile(n, 1024)

    def body(a_ref, b_ref, o_ref):
        @pl.when(pl.program_id(2) == 0)
        def _():
            o_ref[...] = jnp.zeros_like(o_ref)

        o_ref[...] += _dot_tn(a_ref[...].astype(BF16), b_ref[...].astype(BF16))

    return pl.pallas_call(
        body, name=name, grid=(ka // tka, n // tn, s // ts),
        in_specs=[pl.BlockSpec((ts, tka), lambda i, j, t: (t, i)), pl.BlockSpec((ts, tn), lambda i, j, t: (t, j))],
        out_specs=pl.BlockSpec((tka, tn), lambda i, j, t: (i, j)),
        out_shape=jax.ShapeDtypeStruct((ka, n), F32),
        compiler_params=_cparams("parallel", "parallel", "arbitrary"),
    )(a, b)


def _norm_mm(x, g, w, *, out_dtype, name):
    s, d = x.shape
    n = w.shape[1]
    tm = _tile(s, TM)

    def body(x_ref, g_ref, w_ref, h_ref, o_ref):
        xhat, _ = _rms(x_ref[...], d)
        h = (xhat * g_ref[...]).astype(BF16)
        h_ref[...] = h
        o_ref[...] = _dot(h, w_ref[...]).astype(o_ref.dtype)

    return pl.pallas_call(
        body, name=name, grid=(s // tm,),
        in_specs=[pl.BlockSpec((tm, d), lambda i: (i, 0)), pl.BlockSpec((1, d), lambda i: (0, 0)),
                  pl.BlockSpec((d, n), lambda i: (0, 0))],
        out_specs=[pl.BlockSpec((tm, d), lambda i: (i, 0)), pl.BlockSpec((tm, n), lambda i: (i, 0))],
        out_shape=[jax.ShapeDtypeStruct((s, d), BF16), jax.ShapeDtypeStruct((s, n), out_dtype)],
        compiler_params=_cparams("parallel"),
    )(x, g, w)


def _nt_rms_bwd(dy, w, x, g, dres, *, name):
    s, n = dy.shape
    d = x.shape[1]
    tm = _tile(s, TM)

    def body(dy_ref, w_ref, x_ref, g_ref, dres_ref, dx_ref, dg_ref):
        @pl.when(pl.program_id(0) == 0)
        def _():
            dg_ref[...] = jnp.zeros_like(dg_ref)

        dh = _dot_nt(dy_ref[...], w_ref[...])
        xhat, r = _rms(x_ref[...], d)
        dg_ref[...] += jnp.sum(dh * xhat, axis=0, keepdims=True)
        dx_ref[...] = dres_ref[...] + _rms_bwd(xhat, r, dh * g_ref[...], d)

    return pl.pallas_call(
        body, name=name, grid=(s // tm,),
        in_specs=[pl.BlockSpec((tm, n), lambda i: (i, 0)), pl.BlockSpec((d, n), lambda i: (0, 0)),
                  pl.BlockSpec((tm, d), lambda i: (i, 0)), pl.BlockSpec((1, d), lambda i: (0, 0)),
                  pl.BlockSpec((tm, d), lambda i: (i, 0))],
        out_specs=[pl.BlockSpec((tm, d), lambda i: (i, 0)), pl.BlockSpec((1, d), lambda i: (0, 0))],
        out_shape=[jax.ShapeDtypeStruct((s, d), F32), jax.ShapeDtypeStruct((1, d), F32)],
        compiler_params=_cparams("arbitrary"),
    )(dy, w, x, g, dres)


def _mlp_up(x, g, w1, *, name):
    s, d = x.shape
    n = w1.shape[1]
    tm = _tile(s, TM_WIDE)

    def body(x_ref, g_ref, w_ref, h_ref, u_ref, act_ref):
        xhat, _ = _rms(x_ref[...], d)
        h = (xhat * g_ref[...]).astype(BF16)
        h_ref[...] = h
        u = _dot(h, w_ref[...])
        u_ref[...] = u.astype(BF16)
        act_ref[...] = jnp.square(jnp.maximum(u, 0.0)).astype(BF16)

    return pl.pallas_call(
        body, name=name, grid=(s // tm,),
        in_specs=[pl.BlockSpec((tm, d), lambda i: (i, 0)), pl.BlockSpec((1, d), lambda i: (0, 0)),
                  pl.BlockSpec((d, n), lambda i: (0, 0))],
        out_specs=[pl.BlockSpec((tm, d), lambda i: (i, 0)), pl.BlockSpec((tm, n), lambda i: (i, 0)),
                   pl.BlockSpec((tm, n), lambda i: (i, 0))],
        out_shape=[jax.ShapeDtypeStruct((s, d), BF16), jax.ShapeDtypeStruct((s, n), BF16),
                   jax.ShapeDtypeStruct((s, n), BF16)],
        compiler_params=_cparams("parallel"),
    )(x, g, w1)


def _mlp_down_bwd(dy, w2, u, *, name):
    s, d = dy.shape
    n = w2.shape[0]
    tm = _tile(s, TM_WIDE)

    def body(dy_ref, w_ref, u_ref, du_ref):
        dact = _dot_nt(dy_ref[...].astype(BF16), w_ref[...])
        du_ref[...] = (dact * (2.0 * jnp.maximum(u_ref[...].astype(F32), 0.0))).astype(BF16)

    return pl.pallas_call(
        body, name=name, grid=(s // tm,),
        in_specs=[pl.BlockSpec((tm, d), lambda i: (i, 0)), pl.BlockSpec((n, d), lambda i: (0, 0)),
                  pl.BlockSpec((tm, n), lambda i: (i, 0))],
        out_specs=pl.BlockSpec((tm, n), lambda i: (i, 0)),
        out_shape=jax.ShapeDtypeStruct((s, n), BF16),
        compiler_params=_cparams("parallel"),
    )(dy, w2, u)


def _conv_gate(bcu, conv_w, *, name):
    s = bcu.shape[0]
    d = D_MODEL
    tm = _tile(s, TM)
    hb = tm // 8

    def body(bcu_ref, prev_ref, w_ref, z_ref, pbuf):
        i = pl.program_id(0)
        gb = bcu_ref[:, 0:d]
        p = bcu_ref[:, d:2 * d] * bcu_ref[:, 2 * d:3 * d]
        pprev = prev_ref[:, d:2 * d] * prev_ref[:, 2 * d:3 * d]
        pbuf[0:8, :] = jnp.where(i > 0, pprev, 0.0)
        pbuf[8:8 + tm, :] = p
        cv = w_ref[2:3, :] * p + w_ref[1:2, :] * pbuf[7:7 + tm, :] + w_ref[0:1, :] * pbuf[6:6 + tm, :]
        z_ref[...] = (gb * cv).astype(BF16)

    return pl.pallas_call(
        body, name=name, grid=(s // tm,),
        in_specs=[pl.BlockSpec((tm, 3 * d), lambda i: (i, 0)),
                  pl.BlockSpec((8, 3 * d), lambda i: (jnp.maximum(i * hb - 1, 0), 0)),
                  pl.BlockSpec((3, d), lambda i: (0, 0))],
        out_specs=pl.BlockSpec((tm, d), lambda i: (i, 0)),
        out_shape=jax.ShapeDtypeStruct((s, d), BF16),
        scratch_shapes=[pltpu.VMEM((tm + 8, d), F32)],
        compiler_params=_cparams("parallel"),
    )(bcu, bcu, conv_w)


def _conv_gate_bwd(bcu, dz, conv_w, *, name):
    s = bcu.shape[0]
    d = D_MODEL
    tm = _tile(s, TM)
    hb = tm // 8
    nt = s // tm

    def body(bcu_ref, prev_ref, next_ref, dz_ref, dznext_ref, w_ref, dbcu_ref, dw_ref, pbuf, dbuf):
        i = pl.program_id(0)

        @pl.when(i == 0)
        def _():
            dw_ref[...] = jnp.zeros_like(dw_ref)

        gb = bcu_ref[:, 0:d]
        gc = bcu_ref[:, d:2 * d]
        uu = bcu_ref[:, 2 * d:3 * d]
        p = gc * uu
        pprev = prev_ref[:, d:2 * d] * prev_ref[:, 2 * d:3 * d]
        pbuf[0:8, :] = jnp.where(i > 0, pprev, 0.0)
        pbuf[8:8 + tm, :] = p
        p1 = pbuf[7:7 + tm, :]
        p2 = pbuf[6:6 + tm, :]
        cv = w_ref[2:3, :] * p + w_ref[1:2, :] * p1 + w_ref[0:1, :] * p2
        dz_t = dz_ref[...]
        dcv = dz_t * gb
        dcv_next = dznext_ref[...] * next_ref[:, 0:d]
        dbuf[0:tm, :] = dcv
        dbuf[tm:tm + 8, :] = jnp.where(i < nt - 1, dcv_next, 0.0)
        dp = w_ref[2:3, :] * dcv + w_ref[1:2, :] * dbuf[1:1 + tm, :] + w_ref[0:1, :] * dbuf[2:2 + tm, :]
        dw_ref[2:3, :] += jnp.sum(dcv * p, axis=0, keepdims=True)
        dw_ref[1:2, :] += jnp.sum(dcv * p1, axis=0, keepdims=True)
        dw_ref[0:1, :] += jnp.sum(dcv * p2, axis=0, keepdims=True)
        dbcu_ref[:, 0:d] = (dz_t * cv).astype(BF16)
        dbcu_ref[:, d:2 * d] = (dp * uu).astype(BF16)
        dbcu_ref[:, 2 * d:3 * d] = (dp * gc).astype(BF16)

    nxt = lambda i: (jnp.minimum((i + 1) * hb, s // 8 - 1), 0)
    return pl.pallas_call(
        body, name=name, grid=(nt,),
        in_specs=[pl.BlockSpec((tm, 3 * d), lambda i: (i, 0)),
                  pl.BlockSpec((8, 3 * d), lambda i: (jnp.maximum(i * hb - 1, 0), 0)),
                  pl.BlockSpec((8, 3 * d), nxt),
                  pl.BlockSpec((tm, d), lambda i: (i, 0)),
                  pl.BlockSpec((8, d), nxt),
                  pl.BlockSpec((3, d), lambda i: (0, 0))],
        out_specs=[pl.BlockSpec((tm, 3 * d), lambda i: (i, 0)), pl.BlockSpec((3, d), lambda i: (0, 0))],
        out_shape=[jax.ShapeDtypeStruct((s, 3 * d), BF16), jax.ShapeDtypeStruct((3, d), F32)],
        scratch_shapes=[pltpu.VMEM((tm + 8, d), F32), pltpu.VMEM((tm + 8, d), F32)],
        compiler_params=_cparams("arbitrary"),
    )(bcu, bcu, bcu, dz, dz, conv_w)


def _mla_prep(a, g_qa, g_kva, w_uq, w_ukv, g_q, g_k, cc, sa, sb, *, name):
    s = a.shape[0]
    ts = _tile(s, T_PREP)

    def body(a_ref, gqa_ref, gkva_ref, wuq_ref, wukv_ref, gq_ref, gk_ref, cc_ref, sa_ref, sb_ref,
             cq_ref, ckv_ref, q_ref, k_ref, v_ref):
        xq, _ = _rms(a_ref[:, 0:Q_RANK], Q_RANK)
        cq = (xq * gqa_ref[...]).astype(BF16)
        cq_ref[...] = cq
        xkv, _ = _rms(a_ref[:, Q_RANK:Q_RANK + KV_RANK], KV_RANK)
        ckv = (xkv * gkva_ref[...]).astype(BF16)
        ckv_ref[...] = ckv
        kpe = a_ref[:, Q_RANK + KV_RANK:DOWN_PAD]
        kpe_ss = jnp.sum(kpe * kpe, axis=-1, keepdims=True)
        cc_t, sa_t, sb_t = cc_ref[...], sa_ref[...], sb_ref[...]
        gq = gq_ref[...]
        gk = gk_ref[...]
        for h in range(N_HEADS):
            cols = slice(h * QK_PAD, (h + 1) * QK_PAD)
            qhat, _ = _rms(_dot(cq, wuq_ref[:, cols]), QK_DIM)
            qn = qhat * (gq * SCALE_LOG2E)
            q_ref[h, :, 0:QK_NOPE] = qn[:, 0:QK_NOPE].astype(BF16)
            q_ref[h, :, QK_NOPE:QK_PAD] = _rope(qn[:, QK_NOPE:QK_PAD], cc_t, sa_t, sb_t).astype(BF16)
            kvr = _dot(ckv, wukv_ref[:, cols])
            kn = kvr[:, 0:QK_NOPE]
            rk = lax.rsqrt((jnp.sum(kn * kn, axis=-1, keepdims=True) + kpe_ss) * (1.0 / QK_DIM) + EPS)
            k_ref[h, :, 0:QK_NOPE] = (kn * rk * gk[:, 0:QK_NOPE]).astype(BF16)
            k_ref[h, :, QK_NOPE:QK_PAD] = _rope(kpe * rk * gk[:, QK_NOPE:QK_PAD], cc_t, sa_t, sb_t).astype(BF16)
            v_ref[h, :, 0:V_DIM] = kvr[:, QK_NOPE:QK_PAD].astype(BF16)
            v_ref[h, :, V_DIM:2 * V_DIM] = jnp.ones((ts, V_DIM), BF16)

    row = lambda i: (i, 0)
    fixed = lambda i: (0, 0)
    head = lambda i: (0, i, 0)
    return pl.pallas_call(
        body, name=name, grid=(s // ts,),
        in_specs=[pl.BlockSpec((ts, DOWN_PAD), row), pl.BlockSpec((1, Q_RANK), fixed), pl.BlockSpec((1, KV_RANK), fixed),
                  pl.BlockSpec((Q_RANK, N_HEADS * QK_PAD), fixed), pl.BlockSpec((KV_RANK, N_HEADS * QK_PAD), fixed),
                  pl.BlockSpec((1, QK_PAD), fixed), pl.BlockSpec((1, QK_PAD), fixed),
                  pl.BlockSpec((ts, 128), row), pl.BlockSpec((ts, 128), row), pl.BlockSpec((ts, 128), row)],
        out_specs=[pl.BlockSpec((ts, Q_RANK), row), pl.BlockSpec((ts, KV_RANK), row),
                   pl.BlockSpec((N_HEADS, ts, QK_PAD), head), pl.BlockSpec((N_HEADS, ts, QK_PAD), head),
                   pl.BlockSpec((N_HEADS, ts, 2 * V_DIM), head)],
        out_shape=[jax.ShapeDtypeStruct((s, Q_RANK), BF16), jax.ShapeDtypeStruct((s, KV_RANK), BF16),
                   jax.ShapeDtypeStruct((N_HEADS, s, QK_PAD), BF16), jax.ShapeDtypeStruct((N_HEADS, s, QK_PAD), BF16),
                   jax.ShapeDtypeStruct((N_HEADS, s, 2 * V_DIM), BF16)],
        compiler_params=_cparams("parallel"),
    )(a, g_qa, g_kva, w_uq, w_ukv, g_q, g_k, cc, sa, sb)


def _mla_prep_bwd(a, g_qa, g_kva, w_uq, w_ukv, g_q, g_k, cc, sa, sb, dq, dk, dv, *, name):
    s = a.shape[0]
    ts = _tile(s, T_PREP)

    def body(a_ref, gqa_ref, gkva_ref, wuq_ref, wukv_ref, gq_ref, gk_ref, cc_ref, sa_ref, sb_ref,
             dq_ref, dk_ref, dv_ref, dqr_ref, dkvr_ref, da_ref, dgq_ref, dgk_ref, dgqa_ref, dgkva_ref):
        @pl.when(pl.program_id(0) == 0)
        def _():
            dgq_ref[...] = jnp.zeros_like(dgq_ref)
            dgk_ref[...] = jnp.zeros_like(dgk_ref)
            dgqa_ref[...] = jnp.zeros_like(dgqa_ref)
            dgkva_ref[...] = jnp.zeros_like(dgkva_ref)

        xq, r_q = _rms(a_ref[:, 0:Q_RANK], Q_RANK)
        cq = (xq * gqa_ref[...]).astype(BF16)
        xkv, r_kv = _rms(a_ref[:, Q_RANK:Q_RANK + KV_RANK], KV_RANK)
        ckv = (xkv * gkva_ref[...]).astype(BF16)
        kpe = a_ref[:, Q_RANK + KV_RANK:DOWN_PAD]
        kpe_ss = jnp.sum(kpe * kpe, axis=-1, keepdims=True)
        cc_t, sa_t, sb_t = cc_ref[...], sa_ref[...], sb_ref[...]
        gq = gq_ref[...]
        gk = gk_ref[...]
        dcq = jnp.zeros((ts, Q_RANK), F32)
        dckv = jnp.zeros((ts, KV_RANK), F32)
        dkpe = jnp.zeros((ts, 128), F32)
        dgq = jnp.zeros((1, QK_PAD), F32)
        dgk_n = jnp.zeros((1, QK_NOPE), F32)
        dgk_p = jnp.zeros((1, 128), F32)
        for h in range(N_HEADS):
            cols = slice(h * QK_PAD, (h + 1) * QK_PAD)
            qhat, rq = _rms(_dot(cq, wuq_ref[:, cols]), QK_DIM)
            dqn = jnp.concatenate(
                [dq_ref[h, :, 0:QK_NOPE], _rope_t(dq_ref[h, :, QK_NOPE:QK_PAD], cc_t, sa_t, sb_t)], axis=1)
            dgq = dgq + jnp.sum(dqn * qhat, axis=0, keepdims=True)
            dqr = _rms_bwd(qhat, rq, dqn * gq, QK_DIM).astype(BF16)
            dqr_ref[:, cols] = dqr
            dcq = dcq + _dot_nt(dqr, wuq_ref[:, cols])
            kn = _dot(ckv, wukv_ref[:, h * QK_PAD:h * QK_PAD + QK_NOPE])
            rk = lax.rsqrt((jnp.sum(kn * kn, axis=-1, keepdims=True) + kpe_ss) * (1.0 / QK_DIM) + EPS)
            khat_n = kn * rk
            khat_p = kpe * rk
            dkn = dk_ref[h, :, 0:QK_NOPE]
            dkp = _rope_t(dk_ref[h, :, QK_NOPE:QK_PAD], cc_t, sa_t, sb_t)
            dgk_n = dgk_n + jnp.sum(dkn * khat_n, axis=0, keepdims=True)
            dgk_p = dgk_p + jnp.sum(dkp * khat_p, axis=0, keepdims=True)
            dxn = dkn * gk[:, 0:QK_NOPE]
            dxp = dkp * gk[:, QK_NOPE:QK_PAD]
            mean = (jnp.sum(dxn * khat_n, axis=-1, keepdims=True)
                    + jnp.sum(dxp * khat_p, axis=-1, keepdims=True)) * (1.0 / QK_DIM)
            dkpe = dkpe + rk * (dxp - khat_p * mean)
            dkvr = jnp.concatenate([rk * (dxn - khat_n * mean), dv_ref[h, :, :]], axis=1).astype(BF16)
            dkvr_ref[:, cols] = dkvr
            dckv = dckv + _dot_nt(dkvr, wukv_ref[:, cols])
        dgq_ref[...] += dgq
        dgk_ref[:, 0:QK_NOPE] += dgk_n
        dgk_ref[:, QK_NOPE:QK_PAD] += dgk_p
        dgqa_ref[...] += jnp.sum(dcq * xq, axis=0, keepdims=True)
        dgkva_ref[...] += jnp.sum(dckv * xkv, axis=0, keepdims=True)
        da_ref[:, 0:Q_RANK] = _rms_bwd(xq, r_q, dcq * gqa_ref[...], Q_RANK).astype(BF16)
        da_ref[:, Q_RANK:Q_RANK + KV_RANK] = _rms_bwd(xkv, r_kv, dckv * gkva_ref[...], KV_RANK).astype(BF16)
        da_ref[:, Q_RANK + KV_RANK:DOWN_PAD] = dkpe.astype(BF16)

    row = lambda i: (i, 0)
    fixed = lambda i: (0, 0)
    head = lambda i: (0, i, 0)
    wide = N_HEADS * QK_PAD
    return pl.pallas_call(
        body, name=name, grid=(s // ts,),
        in_specs=[pl.BlockSpec((ts, DOWN_PAD), row), pl.BlockSpec((1, Q_RANK), fixed), pl.BlockSpec((1, KV_RANK), fixed),
                  pl.BlockSpec((Q_RANK, wide), fixed), pl.BlockSpec((KV_RANK, wide), fixed),
                  pl.BlockSpec((1, QK_PAD), fixed), pl.BlockSpec((1, QK_PAD), fixed),
                  pl.BlockSpec((ts, 128), row), pl.BlockSpec((ts, 128), row), pl.BlockSpec((ts, 128), row),
                  pl.BlockSpec((N_HEADS, ts, QK_PAD), head), pl.BlockSpec((N_HEADS, ts, QK_PAD), head),
                  pl.BlockSpec((N_HEADS, ts, V_DIM), head)],
        out_specs=[pl.BlockSpec((ts, wide), row), pl.BlockSpec((ts, wide), row), pl.BlockSpec((ts, DOWN_PAD), row),
                   pl.BlockSpec((1, QK_PAD), fixed), pl.BlockSpec((1, QK_PAD), fixed),
                   pl.BlockSpec((1, Q_RANK), fixed), pl.BlockSpec((1, KV_RANK), fixed)],
        out_shape=[jax.ShapeDtypeStruct((s, wide), BF16), jax.ShapeDtypeStruct((s, wide), BF16),
                   jax.ShapeDtypeStruct((s, DOWN_PAD), BF16),
                   jax.ShapeDtypeStruct((1, QK_PAD), F32), jax.ShapeDtypeStruct((1, QK_PAD), F32),
                   jax.ShapeDtypeStruct((1, Q_RANK), F32), jax.ShapeDtypeStruct((1, KV_RANK), F32)],
        compiler_params=_cparams("arbitrary"),
    )(a, g_qa, g_kva, w_uq, w_ukv, g_q, g_k, cc, sa, sb, dq, dk, dv)


def _flash_fwd(q, k, v, pos_col, pos_row, *, name):
    nh, s, _ = q.shape
    tq = _tile(s, FWD_TQ)
    tk = _tile(s, FWD_TK)
    sq = tq // ATTN_CHAINS

    def body(q_ref, k_ref, v_ref, pq_ref, pk_ref, o_ref, lse_ref, m_sc, acc_sc):
        qb = pl.program_id(1)
        m_sc[...] = jnp.full_like(m_sc, NEG)
        acc_sc[...] = jnp.zeros_like(acc_sc)

        def step(kb, masked):
            keys = pl.ds(pl.multiple_of(kb * tk, tk), tk)
            kt = k_ref[0, keys, :]
            vt = v_ref[0, keys, :]
            scores = [_dot_nt(q_ref[0, u * sq:(u + 1) * sq, :], kt) for u in range(ATTN_CHAINS)]
            for u in range(ATTN_CHAINS):
                rows = slice(u * sq, (u + 1) * sq)
                sc = scores[u]
                if masked:
                    sc = jnp.where(pq_ref[rows, :] >= pk_ref[:, keys], sc, NEG)
                m_prev = m_sc[rows, :]
                m_new = jnp.maximum(m_prev, jnp.max(sc, axis=-1, keepdims=True))
                alpha = jnp.exp2(m_prev - m_new)
                p = jnp.exp2(sc - jnp.tile(m_new, (1, tk // 128)))
                acc_sc[rows, :] = jnp.tile(alpha, (1, 2)) * acc_sc[rows, :] + _dot(p.astype(BF16), vt)
                m_sc[rows, :] = m_new

        n_before = (qb * tq) // tk
        n_seen = (qb * tq + tq - 1) // tk + 1
        lax.fori_loop(0, n_before, lambda kb, c: (step(kb, False), c)[1], 0)
        lax.fori_loop(n_before, n_seen, lambda kb, c: (step(kb, True), c)[1], 0)
        l = acc_sc[:, V_DIM:2 * V_DIM]
        o_ref[...] = (acc_sc[:, 0:V_DIM] / l).astype(BF16)
        lse_ref[0] = m_sc[:, 0:1] * (1.0 / LOG2E) + jnp.log(l[:, 0:1])

    return pl.pallas_call(
        body, name=name, grid=(nh, s // tq),
        in_specs=[pl.BlockSpec((1, tq, QK_PAD), lambda h, qb: (h, qb, 0)),
                  pl.BlockSpec((1, s, QK_PAD), lambda h, qb: (h, 0, 0)),
                  pl.BlockSpec((1, s, 2 * V_DIM), lambda h, qb: (h, 0, 0)),
                  pl.BlockSpec((tq, 1), lambda h, qb: (qb, 0)),
                  pl.BlockSpec((1, s), lambda h, qb: (0, 0))],
        out_specs=[pl.BlockSpec((tq, V_DIM), lambda h, qb: (qb, h)),
                   pl.BlockSpec((1, tq, 1), lambda h, qb: (h, qb, 0))],
        scratch_shapes=[pltpu.VMEM((tq, 128), F32), pltpu.VMEM((tq, 2 * V_DIM), F32)],
        out_shape=[jax.ShapeDtypeStruct((s, nh * V_DIM), BF16), jax.ShapeDtypeStruct((nh, s, 1), F32)],
        compiler_params=_cparams("parallel", "parallel"),
    )(q, k, v, pos_col, pos_row)


def _attn_delta(do, o, *, name):
    s = do.shape[0]
    tm = _tile(s, TM)

    def body(do_ref, o_ref, d_ref):
        for h in range(N_HEADS):
            cols = slice(h * V_DIM, (h + 1) * V_DIM)
            d_ref[h] = jnp.sum(do_ref[:, cols].astype(F32) * o_ref[:, cols].astype(F32), axis=-1, keepdims=True)

    return pl.pallas_call(
        body, name=name, grid=(s // tm,),
        in_specs=[pl.BlockSpec((tm, N_HEADS * V_DIM), lambda i: (i, 0))] * 2,
        out_specs=pl.BlockSpec((N_HEADS, tm, 1), lambda i: (0, i, 0)),
        out_shape=jax.ShapeDtypeStruct((N_HEADS, s, 1), F32),
        compiler_params=_cparams("parallel"),
    )(do, o)


def _flash_bwd(q, k, v, do, lse_row, delta_row, pos_col, pos_row, *, name):
    nh, s, _ = q.shape
    tq = _tile(s, BWD_TQ)
    tk = _tile(s, BWD_TK)
    nq, nk = s // tq, s // tk
    sk = tk // ATTN_CHAINS

    def body(q_ref, k_ref, v_ref, do_ref, lse_ref, delta_ref, pq_ref, pk_ref,
             dq_ref, dk_ref, dv_ref, dk_sc, dv_sc):
        kb = pl.program_id(1)

        @pl.when(kb == 0)
        def _():
            dq_ref[...] = jnp.zeros_like(dq_ref)

        dk_sc[...] = jnp.zeros_like(dk_sc)
        dv_sc[...] = jnp.zeros_like(dv_sc)

        def step(qb, masked):
            qs = pl.ds(pl.multiple_of(qb * tq, tq), tq)
            qt = q_ref[0, qs, :]
            dot_ = do_ref[qs, :]
            lse2 = lse_ref[0, :, qs] * LOG2E
            delta = delta_ref[0, :, qs]
            dq_part = None
            sts = [_dot_nt(k_ref[0, u * sk:(u + 1) * sk, :], qt) for u in range(ATTN_CHAINS)]
            dpts = [_dot_nt(v_ref[0, u * sk:(u + 1) * sk, :], dot_) for u in range(ATTN_CHAINS)]
            for u in range(ATTN_CHAINS):
                rows = slice(u * sk, (u + 1) * sk)
                kt = k_ref[0, rows, :]
                pt = jnp.exp2(sts[u] - lse2)
                if masked:
                    pt = jnp.where(pq_ref[:, qs] >= pk_ref[rows, :], pt, 0.0)
                dv_sc[rows, :] += _dot(pt.astype(BF16), dot_)
                dst = (pt * (dpts[u] - delta)).astype(BF16)
                dk_sc[rows, :] += _dot(dst, qt)
                part = _dot_tn(dst, kt)
                dq_part = part if dq_part is None else dq_part + part
            dq_ref[0, qs, :] += dq_part

        q_first = (kb * tk) // tq
        q_clear = (kb * tk + tk - 1) // tq + 1
        lax.fori_loop(q_first, q_clear, lambda qb, c: (step(qb, True), c)[1], 0)
        lax.fori_loop(q_clear, nq, lambda qb, c: (step(qb, False), c)[1], 0)
        dk_ref[0] = dk_sc[...] * (1.0 / LOG2E)
        dv_ref[0] = dv_sc[...]

        @pl.when(kb == nk - 1)
        def _():
            dq_ref[...] = dq_ref[...] * SCALE

    return pl.pallas_call(
        body, name=name, grid=(nh, nk),
        in_specs=[pl.BlockSpec((1, s, QK_PAD), lambda h, kb: (h, 0, 0)),
                  pl.BlockSpec((1, tk, QK_PAD), lambda h, kb: (h, kb, 0)),
                  pl.BlockSpec((1, tk, V_DIM), lambda h, kb: (h, kb, 0)),
                  pl.BlockSpec((s, V_DIM), lambda h, kb: (0, h)),
                  pl.BlockSpec((1, 1, s), lambda h, kb: (h, 0, 0)),
                  pl.BlockSpec((1, 1, s), lambda h, kb: (h, 0, 0)),
                  pl.BlockSpec((1, s), lambda h, kb: (0, 0)),
                  pl.BlockSpec((tk, 1), lambda h, kb: (kb, 0))],
        out_specs=[pl.BlockSpec((1, s, QK_PAD), lambda h, kb: (h, 0, 0)),
                   pl.BlockSpec((1, tk, QK_PAD), lambda h, kb: (h, kb, 0)),
                   pl.BlockSpec((1, tk, V_DIM), lambda h, kb: (h, kb, 0))],
        scratch_shapes=[pltpu.VMEM((tk, QK_PAD), F32), pltpu.VMEM((tk, V_DIM), F32)],
        out_shape=[jax.ShapeDtypeStruct((nh, s, QK_PAD), F32), jax.ShapeDtypeStruct((nh, s, QK_PAD), F32),
                   jax.ShapeDtypeStruct((nh, s, V_DIM), F32)],
        compiler_params=_cparams("arbitrary", "arbitrary"),
    )(q, k, v, do, lse_row, delta_row, pos_row, pos_col)


def _loss_head(y, target, *, name):
    s, d = y.shape
    tm = _tile(s, TM)
    nt = s // tm

    def body(y_ref, t_ref, dy_ref, loss_ref, acc):
        i = pl.program_id(0)

        @pl.when(i == 0)
        def _():
            acc[...] = jnp.zeros_like(acc)

        e = y_ref[...] - t_ref[...]
        dy_ref[...] = e * (1.0 / d)
        acc[...] += jnp.sum((e * e).reshape(tm // 8, 8, d), axis=0)

        @pl.when(i == nt - 1)
        def _():
            loss_ref[...] = jnp.full((1, 128), 0.5 / d, F32) * jnp.sum(acc[...])

    return pl.pallas_call(
        body, name=name, grid=(nt,),
        in_specs=[pl.BlockSpec((tm, d), lambda i: (i, 0))] * 2,
        out_specs=[pl.BlockSpec((tm, d), lambda i: (i, 0)), pl.BlockSpec((1, 128), lambda i: (0, 0))],
        out_shape=[jax.ShapeDtypeStruct((s, d), F32), jax.ShapeDtypeStruct((1, 128), F32)],
        scratch_shapes=[pltpu.VMEM((8, d), F32)],
        compiler_params=_cparams("arbitrary"),
    )(y, target)


def _adamw(w, g, m, v, *, name):
    r, c = w.shape
    tr = _tile(r, 512) if r % 8 == 0 else r

    def body(w_ref, g_ref, m_ref, v_ref, d_ref, nm_ref, nv_ref):
        g_t = g_ref[...]
        nm = ADAM_B1 * m_ref[...] + (1.0 - ADAM_B1) * g_t
        nv = ADAM_B2 * v_ref[...] + (1.0 - ADAM_B2) * (g_t * g_t)
        m_hat = nm / (1.0 - ADAM_B1 ** ADAM_STEP)
        v_hat = nv / (1.0 - ADAM_B2 ** ADAM_STEP)
        d_ref[...] = -ADAM_LR * (m_hat / (jnp.sqrt(v_hat) + ADAM_EPS) + ADAM_WD * w_ref[...])
        nm_ref[...] = nm
        nv_ref[...] = nv

    spec = pl.BlockSpec((tr, c), lambda i: (i, 0))
    return pl.pallas_call(
        body, name=name, grid=(r // tr,), in_specs=[spec] * 4, out_specs=[spec] * 3,
        out_shape=[jax.ShapeDtypeStruct((r, c), F32)] * 3,
        compiler_params=_cparams("parallel"),
    )(w, g, m, v)


def _place():
    return lax.axis_index("x"), lax.axis_index("y"), lax.axis_index("c")


def _other_chips(x, y):
    return [(1 - x, y), (x, 1 - y), (1 - x, 1 - y)]


ANY = pl.BlockSpec(memory_space=pl.ANY)


def _gather_weights(wp):
    _, rows, cols = wp.shape

    def body(wp_ref, out_ref, send_sems, recv_sems, local_sem):
        x, y, c = _place()
        chips = _other_chips(x, y)
        me = 2 * x + y

        def copy(k, src, chip_idx, half, to):
            return pltpu.make_async_remote_copy(
                src_ref=src, dst_ref=out_ref.at[chip_idx, half], send_sem=send_sems.at[k], recv_sem=recv_sems.at[k],
                device_id=to, device_id_type=MESH)

        mine = pltpu.make_async_copy(wp_ref, out_ref.at[me], local_sem)
        mine.start()
        first = [copy(j, wp_ref.at[c], me, c, (*chip, c)) for j, chip in enumerate(chips)]
        for cp in first:
            cp.start()
        passed = []
        for j, (cx, cy) in enumerate(chips):
            idx = 2 * cx + cy
            copy(j, wp_ref.at[c], idx, c, (x, y, c)).wait_recv()
            fwd = copy(3 + j, out_ref.at[idx, c], idx, c, (x, y, 1 - c))
            fwd.start()
            passed.append(fwd)
        for j, (cx, cy) in enumerate(chips):
            copy(3 + j, wp_ref.at[c], 2 * cx + cy, 1 - c, (x, y, c)).wait_recv()
        for cp in first + passed:
            cp.wait_send()
        mine.wait()

    return pl.pallas_call(
        body, name="gather_weights", in_specs=[ANY], out_specs=ANY,
        out_shape=jax.ShapeDtypeStruct((N_CHIPS, 2, rows, cols), wp.dtype),
        scratch_shapes=[pltpu.SemaphoreType.DMA((6,)), pltpu.SemaphoreType.DMA((6,)), pltpu.SemaphoreType.DMA],
    )(wp)


def _swap_halves(g):
    _, nseg, rows, cols = g.shape

    def body(g_ref, out_ref, send_sem, recv_sem):
        x, y, c = _place()
        cp = pltpu.make_async_remote_copy(
            src_ref=g_ref.at[1 - c], dst_ref=out_ref, send_sem=send_sem, recv_sem=recv_sem,
            device_id=(x, y, 1 - c), device_id_type=MESH)
        cp.start()
        cp.wait()

    return pl.pallas_call(
        body, name="grad_swap_halves", in_specs=[ANY], out_specs=ANY,
        out_shape=jax.ShapeDtypeStruct((nseg, rows, cols), g.dtype),
        scratch_shapes=[pltpu.SemaphoreType.DMA, pltpu.SemaphoreType.DMA],
    )(g)


def _chip_sum(g, r1, half):
    _, nseg, rows, cols = g.shape
    tr = _tile(rows, SUM_ROWS)

    def body(half_ref, g_ref, r_ref, o_ref):
        o_ref[...] = (g_ref[0] + r_ref[...]).astype(BF16)

    return pl.pallas_call(
        body, name="grad_chip_sum",
        grid_spec=pltpu.PrefetchScalarGridSpec(
            num_scalar_prefetch=1, grid=(nseg, rows // tr),
            in_specs=[pl.BlockSpec((1, 1, tr, cols), lambda q, i, hf: (hf[0], q, i, 0)),
                      pl.BlockSpec((1, tr, cols), lambda q, i, hf: (q, i, 0))],
            out_specs=pl.BlockSpec((1, tr, cols), lambda q, i, hf: (q, i, 0))),
        out_shape=jax.ShapeDtypeStruct((nseg, rows, cols), BF16),
        compiler_params=_cparams("parallel", "parallel"),
    )(half, g, r1)


def _scatter_partials(pb):
    _, rows, cols = pb.shape

    def body(pb_ref, out_ref, send_sems, recv_sems):
        x, y, c = _place()
        copies = []
        for j, (cx, cy) in enumerate(_other_chips(x, y)):
            cp = pltpu.make_async_remote_copy(
                src_ref=pb_ref.at[2 * cx + cy], dst_ref=out_ref.at[j], send_sem=send_sems.at[j],
                recv_sem=recv_sems.at[j], device_id=(cx, cy, c), device_id_type=MESH)
            cp.start()
            copies.append(cp)
        for cp in copies:
            cp.wait()

    return pl.pallas_call(
        body, name="grad_scatter_partials", in_specs=[ANY], out_specs=ANY,
        out_shape=jax.ShapeDtypeStruct((3, rows, cols), pb.dtype),
        scratch_shapes=[pltpu.SemaphoreType.DMA((3,)), pltpu.SemaphoreType.DMA((3,))],
    )(pb)


def _final_sum(g, r1, r2, sel):
    _, _, rows, cols = g.shape
    tr = _tile(rows, SUM_ROWS)

    def body(sel_ref, g_ref, r1_ref, r2_ref, o_ref):
        acc = g_ref[0, 0] + r1_ref[0]
        for j in range(3):
            acc = acc + r2_ref[j].astype(F32)
        o_ref[...] = acc

    return pl.pallas_call(
        body, name="grad_final_sum",
        grid_spec=pltpu.PrefetchScalarGridSpec(
            num_scalar_prefetch=1, grid=(rows // tr,),
            in_specs=[pl.BlockSpec((1, 1, tr, cols), lambda i, sl: (sl[0], sl[1], i, 0)),
                      pl.BlockSpec((1, tr, cols), lambda i, sl: (sl[1], i, 0)),
                      pl.BlockSpec((3, tr, cols), lambda i, sl: (0, i, 0))],
            out_specs=pl.BlockSpec((tr, cols), lambda i, sl: (i, 0))),
        out_shape=jax.ShapeDtypeStruct((rows, cols), F32),
        compiler_params=_cparams("parallel"),
    )(sel, g, r1, r2)


def _join_halves(f):
    rows, cols = f.shape

    def body(f_ref, out_ref, send_sem, recv_sem, local_sem):
        x, y, c = _place()
        mine = pltpu.make_async_copy(f_ref, out_ref.at[c], local_sem)
        mine.start()
        cp = pltpu.make_async_remote_copy(
            src_ref=f_ref, dst_ref=out_ref.at[c], send_sem=send_sem, recv_sem=recv_sem,
            device_id=(x, y, 1 - c), device_id_type=MESH)
        cp.start()
        cp.wait_send()
        pltpu.make_async_remote_copy(
            src_ref=f_ref, dst_ref=out_ref.at[1 - c], send_sem=send_sem, recv_sem=recv_sem,
            device_id=(x, y, 1 - c), device_id_type=MESH).wait_recv()
        mine.wait()

    return pl.pallas_call(
        body, name="grad_join_halves", in_specs=[ANY], out_specs=ANY,
        out_shape=jax.ShapeDtypeStruct((2, rows, cols), f.dtype),
        scratch_shapes=[pltpu.SemaphoreType.DMA, pltpu.SemaphoreType.DMA, pltpu.SemaphoreType.DMA],
    )(f)


def _all_reduce_small(part, *, name):
    rows, cols = part.shape
    vm = pl.BlockSpec(memory_space=pltpu.VMEM)

    def body(p_ref, o_ref, land, send_sems, recv_sems):
        x, y, c = _place()
        me = 4 * x + 2 * y + c
        flips = [(dx, dy, dc) for dx in (0, 1) for dy in (0, 1) for dc in (0, 1)][1:]
        copies = []
        for k, (dx, dy, dc) in enumerate(flips):
            cp = pltpu.make_async_remote_copy(
                src_ref=p_ref, dst_ref=land.at[me], send_sem=send_sems.at[k], recv_sem=recv_sems.at[k],
                device_id=(1 - x if dx else x, 1 - y if dy else y, 1 - c if dc else c), device_id_type=MESH)
            cp.start()
            copies.append(cp)
        land[me] = p_ref[...]
        for cp in copies:
            cp.wait()
        acc = land[0]
        for j in range(1, 8):
            acc = acc + land[j]
        o_ref[...] = acc

    return pl.pallas_call(
        body, name=name, in_specs=[vm], out_specs=vm,
        out_shape=jax.ShapeDtypeStruct((rows, cols), F32),
        scratch_shapes=[pltpu.VMEM((8, rows, cols), F32), pltpu.SemaphoreType.DMA((7,)), pltpu.SemaphoreType.DMA((7,))],
    )(part)


BIG = {
    "attn_w_down": ((2, 1024, 448), 1), "attn_w_uq": ((2, 256, 1536), 2), "attn_w_ukv": ((2, 128, 2048), 2),
    "attn_w_o": ((2, 1024, 1024), 1), "conv_w_in": ((2, 1024, 3072), 2), "conv_w": ((2, 3, 1024), 2),
    "conv_w_out": ((2, 1024, 1024), 1), "mlp_w1": ((4, 1024, 4096), 2), "mlp_w2": ((4, 4096, 1024), 1),
}
SMALL = {"g_mix": (4, 1024), "g_mlp": (4, 1024), "attn_g_q_a": (2, 256), "attn_g_kv_a": (2, 128),
         "attn_g_qnorm": (2, 192), "attn_g_knorm": (2, 192)}
WEIGHT_ORDER = ["g_mix", "g_mlp", "attn_w_down", "attn_g_q_a", "attn_g_kv_a", "attn_w_uq", "attn_w_ukv",
                "attn_g_qnorm", "attn_g_knorm", "attn_w_o", "conv_w_in", "conv_w", "conv_w_out", "mlp_w1", "mlp_w2"]


def _shard_shape(name):
    shape, axis = BIG[name]
    return tuple(n // N_CHIPS if i == axis else n for i, n in enumerate(shape))


def _prod(shape):
    n = 1
    for v in shape:
        n *= v
    return n


SEG_LEN = sum(_prod(_shard_shape(n)) for n in BIG)
HALF_ROWS = -(-SEG_LEN // (2 * PACK_COLS * SUM_ROWS)) * SUM_ROWS
SEG_PAD = 2 * HALF_ROWS * PACK_COLS


def _pack_segment(parts):
    flat = [parts[n].reshape(-1) for n in BIG]
    dtype = flat[0].dtype
    flat.append(jnp.zeros((SEG_PAD - SEG_LEN,), dtype))
    return jnp.concatenate(flat).reshape(2, HALF_ROWS, PACK_COLS)


def _unpack_segment(seg):
    flat = seg.reshape(-1)
    out, off = {}, 0
    for n in BIG:
        shp = _shard_shape(n)
        out[n] = flat[off:off + _prod(shp)].reshape(shp)
        off += _prod(shp)
    return out


def _split_shards(name, full):
    _, axis = BIG[name]
    return jnp.split(full, N_CHIPS, axis=axis)


SMALL_LEN = sum(_prod(s) for s in SMALL.values())
SMALL_ROWS = -(-SMALL_LEN // (8 * 128)) * 8


def _pack_small(parts):
    flat = [parts[n].reshape(-1) for n in SMALL]
    flat.append(jnp.zeros((SMALL_ROWS * 128 - SMALL_LEN,), F32))
    return jnp.concatenate(flat).reshape(SMALL_ROWS, 128)


def _unpack_small(buf):
    flat = buf.reshape(-1)
    out, off = {}, 0
    for n, shp in SMALL.items():
        out[n] = flat[off:off + _prod(shp)].reshape(shp)
        off += _prod(shp)
    return out


def _rope_tables(positions):
    inv_freq = ROPE_THETA ** (-jnp.arange(0, QK_ROPE, 2, dtype=F32) / QK_ROPE)
    ang = positions.astype(F32)[:, None] * inv_freq
    cos, sin = jnp.cos(ang), jnp.sin(ang)
    z32 = jnp.zeros_like(cos)
    z64 = jnp.zeros((positions.shape[0], 64), F32)
    cc = jnp.concatenate([cos, cos, z64], axis=1)
    sa = jnp.concatenate([-sin, z32, z64], axis=1)
    sb = jnp.concatenate([z32, sin, z64], axis=1)
    return cc, sa, sb


def _pad_heads(w, width):
    k = w.shape[0]
    w = w.reshape(k, N_HEADS, width)
    return jnp.pad(w, ((0, 0), (0, 0), (0, QK_PAD - width))).reshape(k, N_HEADS * QK_PAD)


def _local_step(x, positions, target, wb, gains):
    s = x.shape[0]
    cc, sa, sb = _rope_tables(positions)
    pos_col = positions.reshape(s, 1)
    pos_row = positions.reshape(1, s)

    saved = []
    for i in range(4):
        j = i // 2
        g_mix = gains["g_mix"][i:i + 1]
        g_mlp = gains["g_mlp"][i:i + 1]
        if i % 2 == 0:
            w_down = jnp.pad(wb["attn_w_down"][j], ((0, 0), (0, DOWN_PAD - DOWN_DIM)))
            w_uq = _pad_heads(wb["attn_w_uq"][j], QK_DIM)
            w_ukv = wb["attn_w_ukv"][j]
            g_qa = gains["attn_g_q_a"][j:j + 1]
            g_kva = gains["attn_g_kv_a"][j:j + 1]
            g_q = jnp.pad(gains["attn_g_qnorm"][j:j + 1], ((0, 0), (0, QK_PAD - QK_DIM)))
            g_k = jnp.pad(gains["attn_g_knorm"][j:j + 1], ((0, 0), (0, QK_PAD - QK_DIM)))
            h, a = _norm_mm(x, g_mix, w_down, out_dtype=F32, name=f"mla_down_{j}")
            cq, ckv, q, k, v = _mla_prep(a, g_qa, g_kva, w_uq, w_ukv, g_q, g_k, cc, sa, sb, name=f"mla_prep_{j}")
            o, lse = _flash_fwd(q, k, v, pos_col, pos_row, name=f"flash_fwd_{j}")
            x_mid = _mm_nn(o, wb["attn_w_o"][j], out_dtype=F32, residual=x, name=f"mla_out_{j}")
            mix = dict(h=h, a=a, cq=cq, ckv=ckv, q=q, k=k, v=v, o=o, lse=lse, w_down=w_down, w_uq=w_uq, w_ukv=w_ukv,
                       g_qa=g_qa, g_kva=g_kva, g_q=g_q, g_k=g_k)
        else:
            h, bcu = _norm_mm(x, g_mix, wb["conv_w_in"][j], out_dtype=F32, name=f"conv_in_{j}")
            z = _conv_gate(bcu, gains["conv_w"][j], name=f"conv_gate_{j}")
            x_mid = _mm_nn(z, wb["conv_w_out"][j], out_dtype=F32, residual=x, name=f"conv_out_{j}")
            mix = dict(h=h, bcu=bcu, z=z)
        h2, u, act = _mlp_up(x_mid, g_mlp, wb["mlp_w1"][i], name=f"mlp_up_{i}")
        x_out = _mm_nn(act, wb["mlp_w2"][i], out_dtype=F32, residual=x_mid, name=f"mlp_down_{i}")
        saved.append(dict(x_in=x, x_mid=x_mid, mix=mix, h2=h2, u=u, act=act, g_mix=g_mix, g_mlp=g_mlp))
        x = x_out

    dx, loss = _loss_head(x, target, name="loss_head")

    gw = {n: [None] * BIG[n][0][0] for n in BIG}
    gs = {n: [None] * SMALL[n][0] for n in SMALL}
    for i in reversed(range(4)):
        j = i // 2
        sv = saved[i]
        mix = sv["mix"]
        du = _mlp_down_bwd(dx, wb["mlp_w2"][i], sv["u"], name=f"mlp_down_bwd_{i}")
        gw["mlp_w2"][i] = _mm_tn(sv["act"], dx, name=f"mlp_w2_grad_{i}")
        gw["mlp_w1"][i] = _mm_tn(sv["h2"], du, name=f"mlp_w1_grad_{i}")
        dx, dg = _nt_rms_bwd(du, wb["mlp_w1"][i], sv["x_mid"], sv["g_mlp"], dx, name=f"mlp_up_bwd_{i}")
        gs["g_mlp"][i] = dg[0]
        if i % 2 == 0:
            do = _mm_nt(dx, wb["attn_w_o"][j], out_dtype=BF16, name=f"mla_out_bwd_{j}")
            gw["attn_w_o"][j] = _mm_tn(mix["o"], dx, name=f"mla_w_o_grad_{j}")
            delta = _attn_delta(do, mix["o"], name=f"attn_delta_{j}")
            lse_row = mix["lse"].reshape(N_HEADS, 1, s)
            delta_row = delta.reshape(N_HEADS, 1, s)
            dq, dk, dv = _flash_bwd(mix["q"], mix["k"], mix["v"], do, lse_row, delta_row, pos_col, pos_row,
                                    name=f"flash_bwd_{j}")
            dqr, dkvr, da, dgq, dgk, dgqa, dgkva = _mla_prep_bwd(
                mix["a"], mix["g_qa"], mix["g_kva"], mix["w_uq"], mix["w_ukv"], mix["g_q"], mix["g_k"], cc, sa, sb,
                dq, dk, dv, name=f"mla_prep_bwd_{j}")
            g_uq = _mm_tn(mix["cq"], dqr, name=f"mla_w_uq_grad_{j}")
            gw["attn_w_uq"][j] = g_uq.reshape(Q_RANK, N_HEADS, QK_PAD)[:, :, :QK_DIM].reshape(Q_RANK, N_HEADS * QK_DIM)
            gw["attn_w_ukv"][j] = _mm_tn(mix["ckv"], dkvr, name=f"mla_w_ukv_grad_{j}")
            gw["attn_w_down"][j] = _mm_tn(mix["h"], da, name=f"mla_w_down_grad_{j}")[:, :DOWN_DIM]
            dx, dg = _nt_rms_bwd(da, mix["w_down"], sv["x_in"], sv["g_mix"], dx, name=f"mla_down_bwd_{j}")
            gs["attn_g_qnorm"][j] = dgq[0, :QK_DIM]
            gs["attn_g_knorm"][j] = dgk[0, :QK_DIM]
            gs["attn_g_q_a"][j] = dgqa[0]
            gs["attn_g_kv_a"][j] = dgkva[0]
        else:
            dz = _mm_nt(dx, wb["conv_w_out"][j], out_dtype=F32, name=f"conv_out_bwd_{j}")
            gw["conv_w_out"][j] = _mm_tn(mix["z"], dx, name=f"conv_w_out_grad_{j}")
            dbcu, dcw = _conv_gate_bwd(mix["bcu"], dz, gains["conv_w"][j], name=f"conv_gate_bwd_{j}")
            gw["conv_w"][j] = dcw
            gw["conv_w_in"][j] = _mm_tn(mix["h"], dbcu, name=f"conv_w_in_grad_{j}")
            dx, dg = _nt_rms_bwd(dbcu, wb["conv_w_in"][j], sv["x_in"], sv["g_mix"], dx, name=f"conv_in_bwd_{j}")
        gs["g_mix"][i] = dg[0]

    grads_big = {n: jnp.stack(v) for n, v in gw.items()}
    grads_small = {n: jnp.stack(v) for n, v in gs.items()}
    return loss, dx, grads_big, grads_small


def kernel(x, positions, g_mix, g_mlp, attn_w_down, attn_g_q_a, attn_g_kv_a, attn_w_uq, attn_w_ukv, attn_g_qnorm, attn_g_knorm, attn_w_o, conv_w_in, conv_w, conv_w_out, mlp_w1, mlp_w2, loss_target, m_g_mix, m_g_mlp, m_attn_w_down, m_attn_g_q_a, m_attn_g_kv_a, m_attn_w_uq, m_attn_w_ukv, m_attn_g_qnorm, m_attn_g_knorm, m_attn_w_o, m_conv_w_in, m_conv_w, m_conv_w_out, m_mlp_w1, m_mlp_w2, v_g_mix, v_g_mlp, v_attn_w_down, v_attn_g_q_a, v_attn_g_kv_a, v_attn_w_uq, v_attn_w_ukv, v_attn_g_qnorm, v_attn_g_knorm, v_attn_w_o, v_conv_w_in, v_conv_w, v_conv_w_out, v_mlp_w1, v_mlp_w2):
    w = dict(g_mix=g_mix, g_mlp=g_mlp, attn_w_down=attn_w_down, attn_g_q_a=attn_g_q_a, attn_g_kv_a=attn_g_kv_a,
             attn_w_uq=attn_w_uq, attn_w_ukv=attn_w_ukv, attn_g_qnorm=attn_g_qnorm, attn_g_knorm=attn_g_knorm,
             attn_w_o=attn_w_o, conv_w_in=conv_w_in, conv_w=conv_w, conv_w_out=conv_w_out, mlp_w1=mlp_w1, mlp_w2=mlp_w2)
    m = dict(g_mix=m_g_mix, g_mlp=m_g_mlp, attn_w_down=m_attn_w_down, attn_g_q_a=m_attn_g_q_a,
             attn_g_kv_a=m_attn_g_kv_a, attn_w_uq=m_attn_w_uq, attn_w_ukv=m_attn_w_ukv, attn_g_qnorm=m_attn_g_qnorm,
             attn_g_knorm=m_attn_g_knorm, attn_w_o=m_attn_w_o, conv_w_in=m_conv_w_in, conv_w=m_conv_w,
             conv_w_out=m_conv_w_out, mlp_w1=m_mlp_w1, mlp_w2=m_mlp_w2)
    v = dict(g_mix=v_g_mix, g_mlp=v_g_mlp, attn_w_down=v_attn_w_down, attn_g_q_a=v_attn_g_q_a,
             attn_g_kv_a=v_attn_g_kv_a, attn_w_uq=v_attn_w_uq, attn_w_ukv=v_attn_w_ukv, attn_g_qnorm=v_attn_g_qnorm,
             attn_g_knorm=v_attn_g_knorm, attn_w_o=v_attn_w_o, conv_w_in=v_conv_w_in, conv_w=v_conv_w,
             conv_w_out=v_conv_w_out, mlp_w1=v_mlp_w1, mlp_w2=v_mlp_w2)
    cx, cy, cc_ = _place()

    shard_b = {n: w[n].astype(BF16) for n in BIG}
    gathered = _gather_weights(_pack_segment(shard_b))
    per_chip = [_unpack_segment(gathered[q]) for q in range(N_CHIPS)]
    wb = {n: jnp.concatenate([per_chip[q][n] for q in range(N_CHIPS)], axis=BIG[n][1]) for n in BIG}

    cw_cols = conv_w.shape[2]
    placed = lax.dynamic_update_slice(jnp.zeros(BIG["conv_w"][0], F32), conv_w, (0, 0, (2 * cx + cy) * cw_cols))
    conv_w_full = 0.5 * _all_reduce_small(placed.reshape(-1, 128), name="conv_w_gather").reshape(BIG["conv_w"][0])

    gains = {n: w[n] for n in SMALL}
    gains["conv_w"] = conv_w_full

    loss, grad_x, grads_big, grads_small = _local_step(x[0], positions[0], loss_target[0], wb, gains)

    segs = []
    for q in range(N_CHIPS):
        segs.append(_pack_segment({n: _split_shards(n, grads_big[n])[q] for n in BIG}))
    packed = jnp.stack(segs, axis=1)
    half = cc_.astype(jnp.int32).reshape(1)
    seg = (2 * cx + cy).astype(jnp.int32).reshape(1)
    r1 = _swap_halves(packed)
    partial = _chip_sum(packed, r1, half)
    r2 = _scatter_partials(partial)
    reduced_half = _final_sum(packed, r1, r2, jnp.concatenate([half, seg]))
    grad_shards = _unpack_segment(_join_halves(reduced_half))

    small = _unpack_small(_all_reduce_small(_pack_small(grads_small), name="gain_all_reduce"))

    loss_total = lax.psum(loss[0, 0], ("x", "y", "c"))

    grads, deltas, new_m, new_v = {}, {}, {}, {}
    for n in BIG:
        shp = w[n].shape
        two_d = (shp[0] * shp[1], shp[2])
        g2 = grad_shards[n].reshape(two_d)
        d, nm, nv = _adamw(w[n].reshape(two_d), g2, m[n].reshape(two_d), v[n].reshape(two_d), name=f"adamw_{n}")
        grads[n], deltas[n], new_m[n], new_v[n] = grad_shards[n], d.reshape(shp), nm.reshape(shp), nv.reshape(shp)
    d, nm, nv = _adamw(_pack_small({n: w[n] for n in SMALL}), _pack_small(small),
                       _pack_small({n: m[n] for n in SMALL}), _pack_small({n: v[n] for n in SMALL}), name="adamw_gains")
    d, nm, nv = _unpack_small(d), _unpack_small(nm), _unpack_small(nv)
    for n in SMALL:
        grads[n], deltas[n], new_m[n], new_v[n] = small[n], d[n], nm[n], nv[n]

    return (loss_total, grad_x[None],
            *[grads[n] for n in WEIGHT_ORDER], *[deltas[n] for n in WEIGHT_ORDER],
            *[new_m[n] for n in WEIGHT_ORDER], *[new_v[n] for n in WEIGHT_ORDER])
```

```python
import functools

import jax
import jax.numpy as jnp
from jax import lax
from jax.experimental import pallas as pl
from jax.experimental.pallas import tpu as pltpu

F32 = jnp.float32
BF16 = jnp.bfloat16

D_MODEL = 1024
N_HEADS = 8
QK_NOPE = 128
QK_ROPE = 64
QK_DIM = QK_NOPE + QK_ROPE
QK_PAD = 256
V_DIM = 128
Q_RANK = 256
KV_RANK = 128
DOWN_DIM = Q_RANK + KV_RANK + QK_ROPE
DOWN_PAD = 512
D_FF = 4 * D_MODEL
ROPE_THETA = 10000.0
EPS = 1e-6
NEG = -1e30
SCALE = QK_DIM ** -0.5
SCALE_LOG2E = SCALE * 1.4426950408889634
LOG2E = 1.4426950408889634
ATTN_CHAINS = 2

ADAM_LR = 0.001
ADAM_B1 = 0.9
ADAM_B2 = 0.999
ADAM_EPS = 1e-08
ADAM_WD = 0.01
ADAM_STEP = 10

N_CHIPS = 4
MESH = pl.DeviceIdType.MESH
ANY = pl.BlockSpec(memory_space=pl.ANY)

TM = 512
TM_WIDE = 256
FWD_TQ = 1024
FWD_TK = 256
BWD_TQ = 256
BWD_TK = 1024
T_PREP = 256
T_RED = 1024
SUM_BLOCK_BYTES = 2 * 1024 * 1024


def _tile(n, pref):
    t = min(n, pref)
    assert n % t == 0, (n, t)
    return t


def _cparams(*sem):
    return pltpu.CompilerParams(dimension_semantics=sem)


def _dot(a, b):
    return jnp.dot(a, b, preferred_element_type=F32)


def _dot_nt(a, b):
    return lax.dot_general(a, b, (((1,), (1,)), ((), ())), preferred_element_type=F32)


def _dot_tn(a, b):
    return lax.dot_general(a, b, (((0,), (0,)), ((), ())), preferred_element_type=F32)


def _rms(x, width):
    r = lax.rsqrt(jnp.sum(x * x, axis=-1, keepdims=True) * (1.0 / width) + EPS)
    return x * r, r


def _rms_bwd(xhat, r, dxhat, width):
    return r * (dxhat - xhat * (jnp.sum(dxhat * xhat, axis=-1, keepdims=True) * (1.0 / width)))


def _rope(t, cc, sa, sb):
    return t * cc + pltpu.roll(t, 96, 1) * sa + pltpu.roll(t, 32, 1) * sb


def _rope_t(g, cc, sa, sb):
    return g * cc + pltpu.roll(g * sa, 32, 1) + pltpu.roll(g * sb, 96, 1)


def _wspec(w, layer):
    if w.ndim == 2:
        return pl.BlockSpec(w.shape, lambda *_: (0, 0))
    return pl.BlockSpec((None,) + w.shape[1:], lambda *_: (layer, 0, 0))


def _mm_nn(a, b, *, out_dtype, name, residual=None, layer=0):
    m, k = a.shape
    n = b.shape[-1]
    tm = _tile(m, TM)

    def body(*refs):
        if residual is None:
            a_ref, b_ref, o_ref = refs
        else:
            a_ref, b_ref, r_ref, o_ref = refs
        acc = _dot(a_ref[...].astype(BF16), b_ref[...])
        if residual is not None:
            acc = acc + r_ref[...]
        o_ref[...] = acc.astype(o_ref.dtype)

    in_specs = [pl.BlockSpec((tm, k), lambda i: (i, 0)), _wspec(b, layer)]
    args = [a, b]
    if residual is not None:
        in_specs.append(pl.BlockSpec((tm, n), lambda i: (i, 0)))
        args.append(residual)
    return pl.pallas_call(
        body, name=name, grid=(m // tm,), in_specs=in_specs,
        out_specs=pl.BlockSpec((tm, n), lambda i: (i, 0)),
        out_shape=jax.ShapeDtypeStruct((m, n), out_dtype),
        compiler_params=_cparams("parallel"),
    )(*args)


def _mm_nt(a, b, *, out_dtype, name, layer=0):
    m, k = a.shape
    n = b.shape[-2]
    tm = _tile(m, TM)

    def body(a_ref, b_ref, o_ref):
        o_ref[...] = _dot_nt(a_ref[...].astype(BF16), b_ref[...]).astype(o_ref.dtype)

    return pl.pallas_call(
        body, name=name, grid=(m // tm,),
        in_specs=[pl.BlockSpec((tm, k), lambda i: (i, 0)), _wspec(b, layer)],
        out_specs=pl.BlockSpec((tm, n), lambda i: (i, 0)),
        out_shape=jax.ShapeDtypeStruct((m, n), out_dtype),
        compiler_params=_cparams("parallel"),
    )(a, b)


def _mm_tn(a, b, *, name, stack=None, layer=0, layers=1, keep=None):
    s, ka = a.shape
    n = b.shape[1]
    ts = _tile(s, T_RED)
    tka = _tile(ka, 1024)
    tn = _tile(n, 1024)
    n_out = n if keep is None else keep
    assert keep is None or tn == n

    def body(a_ref, b_ref, *rest):
        o_ref = rest[-1]

        @pl.when(pl.program_id(2) == 0)
        def _():
            o_ref[...] = jnp.zeros_like(o_ref)

        o_ref[...] += _dot_tn(a_ref[...].astype(BF16), b_ref[...].astype(BF16))[:, :n_out if keep else tn]

    in_specs = [pl.BlockSpec((ts, tka), lambda i, j, t: (t, i)), pl.BlockSpec((ts, tn), lambda i, j, t: (t, j))]
    args = [a, b]
    if stack is not None:
        in_specs.append(ANY)
        args.append(stack)
    return pl.pallas_call(
        body, name=name, grid=(ka // tka, n // tn, s // ts), in_specs=in_specs,
        out_specs=pl.BlockSpec((None, tka, tn if keep is None else keep), lambda i, j, t: (layer, i, j)),
        out_shape=jax.ShapeDtypeStruct((layers, ka, n_out), F32),
        input_output_aliases={} if stack is None else {2: 0},
        compiler_params=_cparams("parallel", "parallel", "arbitrary"),
    )(*args)


def _norm_mm(x, g, w, *, out_dtype, name, layer=0):
    s, d = x.shape
    n = w.shape[-1]
    tm = _tile(s, TM)

    def body(x_ref, g_ref, w_ref, h_ref, o_ref):
        xhat, _ = _rms(x_ref[...], d)
        h = (xhat * g_ref[...]).astype(BF16)
        h_ref[...] = h
        o_ref[...] = _dot(h, w_ref[...]).astype(o_ref.dtype)

    return pl.pallas_call(
        body, name=name, grid=(s // tm,),
        in_specs=[pl.BlockSpec((tm, d), lambda i: (i, 0)), pl.BlockSpec((1, d), lambda i: (0, 0)), _wspec(w, layer)],
        out_specs=[pl.BlockSpec((tm, d), lambda i: (i, 0)), pl.BlockSpec((tm, n), lambda i: (i, 0))],
        out_shape=[jax.ShapeDtypeStruct((s, d), BF16), jax.ShapeDtypeStruct((s, n), out_dtype)],
        compiler_params=_cparams("parallel"),
    )(x, g, w)


def _nt_rms_bwd(dy, w, x, g, dres, *, name, layer=0):
    s, n = dy.shape
    d = x.shape[1]
    tm = _tile(s, TM)

    def body(dy_ref, w_ref, x_ref, g_ref, dres_ref, dx_ref, dg_ref):
        @pl.when(pl.program_id(0) == 0)
        def _():
            dg_ref[...] = jnp.zeros_like(dg_ref)

        dh = _dot_nt(dy_ref[...], w_ref[...])
        xhat, r = _rms(x_ref[...], d)
        dg_ref[...] += jnp.sum(dh * xhat, axis=0, keepdims=True)
        dx_ref[...] = dres_ref[...] + _rms_bwd(xhat, r, dh * g_ref[...], d)

    return pl.pallas_call(
        body, name=name, grid=(s // tm,),
        in_specs=[pl.BlockSpec((tm, n), lambda i: (i, 0)), _wspec(w, layer),
                  pl.BlockSpec((tm, d), lambda i: (i, 0)), pl.BlockSpec((1, d), lambda i: (0, 0)),
                  pl.BlockSpec((tm, d), lambda i: (i, 0))],
        out_specs=[pl.BlockSpec((tm, d), lambda i: (i, 0)), pl.BlockSpec((1, d), lambda i: (0, 0))],
        out_shape=[jax.ShapeDtypeStruct((s, d), F32), jax.ShapeDtypeStruct((1, d), F32)],
        compiler_params=_cparams("arbitrary"),
    )(dy, w, x, g, dres)


def _mlp_up(x, g, w1, *, name, layer=0):
    s, d = x.shape
    n = w1.shape[-1]
    tm = _tile(s, TM_WIDE)

    def body(x_ref, g_ref, w_ref, h_ref, u_ref, act_ref):
        xhat, _ = _rms(x_ref[...], d)
        h = (xhat * g_ref[...]).astype(BF16)
        h_ref[...] = h
        u = _dot(h, w_ref[...])
        u_ref[...] = u.astype(BF16)
        act_ref[...] = jnp.square(jnp.maximum(u, 0.0)).astype(BF16)

    return pl.pallas_call(
        body, name=name, grid=(s // tm,),
        in_specs=[pl.BlockSpec((tm, d), lambda i: (i, 0)), pl.BlockSpec((1, d), lambda i: (0, 0)), _wspec(w1, layer)],
        out_specs=[pl.BlockSpec((tm, d), lambda i: (i, 0)), pl.BlockSpec((tm, n), lambda i: (i, 0)),
                   pl.BlockSpec((tm, n), lambda i: (i, 0))],
        out_shape=[jax.ShapeDtypeStruct((s, d), BF16), jax.ShapeDtypeStruct((s, n), BF16),
                   jax.ShapeDtypeStruct((s, n), BF16)],
        compiler_params=_cparams("parallel"),
    )(x, g, w1)


def _mlp_down_bwd(dy, w2, u, *, name, layer=0):
    s, d = dy.shape
    n = w2.shape[-2]
    tm = _tile(s, TM_WIDE)

    def body(dy_ref, w_ref, u_ref, du_ref):
        dact = _dot_nt(dy_ref[...].astype(BF16), w_ref[...])
        du_ref[...] = (dact * (2.0 * jnp.maximum(u_ref[...].astype(F32), 0.0))).astype(BF16)

    return pl.pallas_call(
        body, name=name, grid=(s // tm,),
        in_specs=[pl.BlockSpec((tm, d), lambda i: (i, 0)), _wspec(w2, layer),
                  pl.BlockSpec((tm, n), lambda i: (i, 0))],
        out_specs=pl.BlockSpec((tm, n), lambda i: (i, 0)),
        out_shape=jax.ShapeDtypeStruct((s, n), BF16),
        compiler_params=_cparams("parallel"),
    )(dy, w2, u)


def _conv_gate(bcu, conv_w, *, name):
    s = bcu.shape[0]
    d = D_MODEL
    tm = _tile(s, TM)
    hb = tm // 8

    def body(bcu_ref, prev_ref, w_ref, z_ref, pbuf):
        i = pl.program_id(0)
        gb = bcu_ref[:, 0:d]
        p = bcu_ref[:, d:2 * d] * bcu_ref[:, 2 * d:3 * d]
        pprev = prev_ref[:, d:2 * d] * prev_ref[:, 2 * d:3 * d]
        pbuf[0:8, :] = jnp.where(i > 0, pprev, 0.0)
        pbuf[8:8 + tm, :] = p
        cv = w_ref[2:3, :] * p + w_ref[1:2, :] * pbuf[7:7 + tm, :] + w_ref[0:1, :] * pbuf[6:6 + tm, :]
        z_ref[...] = (gb * cv).astype(BF16)

    return pl.pallas_call(
        body, name=name, grid=(s // tm,),
        in_specs=[pl.BlockSpec((tm, 3 * d), lambda i: (i, 0)),
                  pl.BlockSpec((8, 3 * d), lambda i: (jnp.maximum(i * hb - 1, 0), 0)),
                  pl.BlockSpec((3, d), lambda i: (0, 0))],
        out_specs=pl.BlockSpec((tm, d), lambda i: (i, 0)),
        out_shape=jax.ShapeDtypeStruct((s, d), BF16),
        scratch_shapes=[pltpu.VMEM((tm + 8, d), F32)],
        compiler_params=_cparams("parallel"),
    )(bcu, bcu, conv_w)


def _conv_gate_bwd(bcu, dz, conv_w, *, name):
    s = bcu.shape[0]
    d = D_MODEL
    tm = _tile(s, TM)
    hb = tm // 8
    nt = s // tm

    def body(bcu_ref, prev_ref, next_ref, dz_ref, dznext_ref, w_ref, dbcu_ref, dw_ref, pbuf, dbuf):
        i = pl.program_id(0)

        @pl.when(i == 0)
        def _():
            dw_ref[...] = jnp.zeros_like(dw_ref)

        gb = bcu_ref[:, 0:d]
        gc = bcu_ref[:, d:2 * d]
        uu = bcu_ref[:, 2 * d:3 * d]
        p = gc * uu
        pprev = prev_ref[:, d:2 * d] * prev_ref[:, 2 * d:3 * d]
        pbuf[0:8, :] = jnp.where(i > 0, pprev, 0.0)
        pbuf[8:8 + tm, :] = p
        p1 = pbuf[7:7 + tm, :]
        p2 = pbuf[6:6 + tm, :]
        cv = w_ref[2:3, :] * p + w_ref[1:2, :] * p1 + w_ref[0:1, :] * p2
        dz_t = dz_ref[...]
        dcv = dz_t * gb
        dcv_next = dznext_ref[...] * next_ref[:, 0:d]
        dbuf[0:tm, :] = dcv
        dbuf[tm:tm + 8, :] = jnp.where(i < nt - 1, dcv_next, 0.0)
        dp = w_ref[2:3, :] * dcv + w_ref[1:2, :] * dbuf[1:1 + tm, :] + w_ref[0:1, :] * dbuf[2:2 + tm, :]
        dw_ref[2:3, :] += jnp.sum(dcv * p, axis=0, keepdims=True)
        dw_ref[1:2, :] += jnp.sum(dcv * p1, axis=0, keepdims=True)
        dw_ref[0:1, :] += jnp.sum(dcv * p2, axis=0, keepdims=True)
        dbcu_ref[:, 0:d] = (dz_t * cv).astype(BF16)
        dbcu_ref[:, d:2 * d] = (dp * uu).astype(BF16)
        dbcu_ref[:, 2 * d:3 * d] = (dp * gc).astype(BF16)

    nxt = lambda i: (jnp.minimum((i + 1) * hb, s // 8 - 1), 0)
    return pl.pallas_call(
        body, name=name, grid=(nt,),
        in_specs=[pl.BlockSpec((tm, 3 * d), lambda i: (i, 0)),
                  pl.BlockSpec((8, 3 * d), lambda i: (jnp.maximum(i * hb - 1, 0), 0)),
                  pl.BlockSpec((8, 3 * d), nxt),
                  pl.BlockSpec((tm, d), lambda i: (i, 0)),
                  pl.BlockSpec((8, d), nxt),
                  pl.BlockSpec((3, d), lambda i: (0, 0))],
        out_specs=[pl.BlockSpec((tm, 3 * d), lambda i: (i, 0)), pl.BlockSpec((3, d), lambda i: (0, 0))],
        out_shape=[jax.ShapeDtypeStruct((s, 3 * d), BF16), jax.ShapeDtypeStruct((3, d), F32)],
        scratch_shapes=[pltpu.VMEM((tm + 8, d), F32), pltpu.VMEM((tm + 8, d), F32)],
        compiler_params=_cparams("arbitrary"),
    )(bcu, bcu, bcu, dz, dz, conv_w)


def _mla_prep(a, g_qa, g_kva, w_uq, w_ukv, g_q, g_k, cc, sa, sb, *, name):
    s = a.shape[0]
    ts = _tile(s, T_PREP)

    def body(a_ref, gqa_ref, gkva_ref, wuq_ref, wukv_ref, gq_ref, gk_ref, cc_ref, sa_ref, sb_ref,
             cq_ref, ckv_ref, q_ref, k_ref, v_ref):
        xq, _ = _rms(a_ref[:, 0:Q_RANK], Q_RANK)
        cq = (xq * gqa_ref[...]).astype(BF16)
        cq_ref[...] = cq
        xkv, _ = _rms(a_ref[:, Q_RANK:Q_RANK + KV_RANK], KV_RANK)
        ckv = (xkv * gkva_ref[...]).astype(BF16)
        ckv_ref[...] = ckv
        kpe = a_ref[:, Q_RANK + KV_RANK:DOWN_PAD]
        kpe_ss = jnp.sum(kpe * kpe, axis=-1, keepdims=True)
        cc_t, sa_t, sb_t = cc_ref[...], sa_ref[...], sb_ref[...]
        gq = gq_ref[...]
        gk = gk_ref[...]
        for h in range(N_HEADS):
            cols = slice(h * QK_PAD, (h + 1) * QK_PAD)
            qhat, _ = _rms(_dot(cq, wuq_ref[:, cols]), QK_DIM)
            qn = qhat * (gq * SCALE_LOG2E)
            q_ref[h, :, 0:QK_NOPE] = qn[:, 0:QK_NOPE].astype(BF16)
            q_ref[h, :, QK_NOPE:QK_PAD] = _rope(qn[:, QK_NOPE:QK_PAD], cc_t, sa_t, sb_t).astype(BF16)
            kvr = _dot(ckv, wukv_ref[:, cols])
            kn = kvr[:, 0:QK_NOPE]
            rk = lax.rsqrt((jnp.sum(kn * kn, axis=-1, keepdims=True) + kpe_ss) * (1.0 / QK_DIM) + EPS)
            k_ref[h, :, 0:QK_NOPE] = (kn * rk * gk[:, 0:QK_NOPE]).astype(BF16)
            k_ref[h, :, QK_NOPE:QK_PAD] = _rope(kpe * rk * gk[:, QK_NOPE:QK_PAD], cc_t, sa_t, sb_t).astype(BF16)
            v_ref[h, :, 0:V_DIM] = kvr[:, QK_NOPE:QK_PAD].astype(BF16)
            v_ref[h, :, V_DIM:2 * V_DIM] = jnp.ones((ts, V_DIM), BF16)

    row = lambda i: (i, 0)
    fixed = lambda i: (0, 0)
    head = lambda i: (0, i, 0)
    return pl.pallas_call(
        body, name=name, grid=(s // ts,),
        in_specs=[pl.BlockSpec((ts, DOWN_PAD), row), pl.BlockSpec((1, Q_RANK), fixed), pl.BlockSpec((1, KV_RANK), fixed),
                  pl.BlockSpec((Q_RANK, N_HEADS * QK_PAD), fixed), pl.BlockSpec((KV_RANK, N_HEADS * QK_PAD), fixed),
                  pl.BlockSpec((1, QK_PAD), fixed), pl.BlockSpec((1, QK_PAD), fixed),
                  pl.BlockSpec((ts, 128), row), pl.BlockSpec((ts, 128), row), pl.BlockSpec((ts, 128), row)],
        out_specs=[pl.BlockSpec((ts, Q_RANK), row), pl.BlockSpec((ts, KV_RANK), row),
                   pl.BlockSpec((N_HEADS, ts, QK_PAD), head), pl.BlockSpec((N_HEADS, ts, QK_PAD), head),
                   pl.BlockSpec((N_HEADS, ts, 2 * V_DIM), head)],
        out_shape=[jax.ShapeDtypeStruct((s, Q_RANK), BF16), jax.ShapeDtypeStruct((s, KV_RANK), BF16),
                   jax.ShapeDtypeStruct((N_HEADS, s, QK_PAD), BF16), jax.ShapeDtypeStruct((N_HEADS, s, QK_PAD), BF16),
                   jax.ShapeDtypeStruct((N_HEADS, s, 2 * V_DIM), BF16)],
        compiler_params=_cparams("parallel"),
    )(a, g_qa, g_kva, w_uq, w_ukv, g_q, g_k, cc, sa, sb)


def _mla_prep_bwd(a, g_qa, g_kva, w_uq, w_ukv, g_q, g_k, cc, sa, sb, dq, dk, dv, *, name):
    s = a.shape[0]
    ts = _tile(s, T_PREP)

    def body(a_ref, gqa_ref, gkva_ref, wuq_ref, wukv_ref, gq_ref, gk_ref, cc_ref, sa_ref, sb_ref,
             dq_ref, dk_ref, dv_ref, dqr_ref, dkvr_ref, da_ref, dgq_ref, dgk_ref, dgqa_ref, dgkva_ref):
        @pl.when(pl.program_id(0) == 0)
        def _():
            dgq_ref[...] = jnp.zeros_like(dgq_ref)
            dgk_ref[...] = jnp.zeros_like(dgk_ref)
            dgqa_ref[...] = jnp.zeros_like(dgqa_ref)
            dgkva_ref[...] = jnp.zeros_like(dgkva_ref)

        xq, r_q = _rms(a_ref[:, 0:Q_RANK], Q_RANK)
        cq = (xq * gqa_ref[...]).astype(BF16)
        xkv, r_kv = _rms(a_ref[:, Q_RANK:Q_RANK + KV_RANK], KV_RANK)
        ckv = (xkv * gkva_ref[...]).astype(BF16)
        kpe = a_ref[:, Q_RANK + KV_RANK:DOWN_PAD]
        kpe_ss = jnp.sum(kpe * kpe, axis=-1, keepdims=True)
        cc_t, sa_t, sb_t = cc_ref[...], sa_ref[...], sb_ref[...]
        gq = gq_ref[...]
        gk = gk_ref[...]
        dcq = jnp.zeros((ts, Q_RANK), F32)
        dckv = jnp.zeros((ts, KV_RANK), F32)
        dkpe = jnp.zeros((ts, 128), F32)
        dgq = jnp.zeros((1, QK_PAD), F32)
        dgk_n = jnp.zeros((1, QK_NOPE), F32)
        dgk_p = jnp.zeros((1, 128), F32)
        for h in range(N_HEADS):
            cols = slice(h * QK_PAD, (h + 1) * QK_PAD)
            qhat, rq = _rms(_dot(cq, wuq_ref[:, cols]), QK_DIM)
            dqn = jnp.concatenate(
                [dq_ref[h, :, 0:QK_NOPE], _rope_t(dq_ref[h, :, QK_NOPE:QK_PAD], cc_t, sa_t, sb_t)], axis=1)
            dgq = dgq + jnp.sum(dqn * qhat, axis=0, keepdims=True)
            dqr = _rms_bwd(qhat, rq, dqn * gq, QK_DIM).astype(BF16)
            dqr_ref[:, cols] = dqr
            dcq = dcq + _dot_nt(dqr, wuq_ref[:, cols])
            kn = _dot(ckv, wukv_ref[:, h * QK_PAD:h * QK_PAD + QK_NOPE])
            rk = lax.rsqrt((jnp.sum(kn * kn, axis=-1, keepdims=True) + kpe_ss) * (1.0 / QK_DIM) + EPS)
            khat_n = kn * rk
            khat_p = kpe * rk
            dkn = dk_ref[h, :, 0:QK_NOPE]
            dkp = _rope_t(dk_ref[h, :, QK_NOPE:QK_PAD], cc_t, sa_t, sb_t)
            dgk_n = dgk_n + jnp.sum(dkn * khat_n, axis=0, keepdims=True)
            dgk_p = dgk_p + jnp.sum(dkp * khat_p, axis=0, keepdims=True)
            dxn = dkn * gk[:, 0:QK_NOPE]
            dxp = dkp * gk[:, QK_NOPE:QK_PAD]
            mean = (jnp.sum(dxn * khat_n, axis=-1, keepdims=True)
                    + jnp.sum(dxp * khat_p, axis=-1, keepdims=True)) * (1.0 / QK_DIM)
            dkpe = dkpe + rk * (dxp - khat_p * mean)
            dkvr = jnp.concatenate([rk * (dxn - khat_n * mean), dv_ref[h, :, :]], axis=1).astype(BF16)
            dkvr_ref[:, cols] = dkvr
            dckv = dckv + _dot_nt(dkvr, wukv_ref[:, cols])
        dgq_ref[...] += dgq
        dgk_ref[:, 0:QK_NOPE] += dgk_n
        dgk_ref[:, QK_NOPE:QK_PAD] += dgk_p
        dgqa_ref[...] += jnp.sum(dcq * xq, axis=0, keepdims=True)
        dgkva_ref[...] += jnp.sum(dckv * xkv, axis=0, keepdims=True)
        da_ref[:, 0:Q_RANK] = _rms_bwd(xq, r_q, dcq * gqa_ref[...], Q_RANK).astype(BF16)
        da_ref[:, Q_RANK:Q_RANK + KV_RANK] = _rms_bwd(xkv, r_kv, dckv * gkva_ref[...], KV_RANK).astype(BF16)
        da_ref[:, Q_RANK + KV_RANK:DOWN_PAD] = dkpe.astype(BF16)

    row = lambda i: (i, 0)
    fixed = lambda i: (0, 0)
    head = lambda i: (0, i, 0)
    wide = N_HEADS * QK_PAD
    return pl.pallas_call(
        body, name=name, grid=(s // ts,),
        in_specs=[pl.BlockSpec((ts, DOWN_PAD), row), pl.BlockSpec((1, Q_RANK), fixed), pl.BlockSpec((1, KV_RANK), fixed),
                  pl.BlockSpec((Q_RANK, wide), fixed), pl.BlockSpec((KV_RANK, wide), fixed),
                  pl.BlockSpec((1, QK_PAD), fixed), pl.BlockSpec((1, QK_PAD), fixed),
                  pl.BlockSpec((ts, 128), row), pl.BlockSpec((ts, 128), row), pl.BlockSpec((ts, 128), row),
                  pl.BlockSpec((N_HEADS, ts, QK_PAD), head), pl.BlockSpec((N_HEADS, ts, QK_PAD), head),
                  pl.BlockSpec((N_HEADS, ts, V_DIM), head)],
        out_specs=[pl.BlockSpec((ts, wide), row), pl.BlockSpec((ts, wide), row), pl.BlockSpec((ts, DOWN_PAD), row),
                   pl.BlockSpec((1, QK_PAD), fixed), pl.BlockSpec((1, QK_PAD), fixed),
                   pl.BlockSpec((1, Q_RANK), fixed), pl.BlockSpec((1, KV_RANK), fixed)],
        out_shape=[jax.ShapeDtypeStruct((s, wide), BF16), jax.ShapeDtypeStruct((s, wide), BF16),
                   jax.ShapeDtypeStruct((s, DOWN_PAD), BF16),
                   jax.ShapeDtypeStruct((1, QK_PAD), F32), jax.ShapeDtypeStruct((1, QK_PAD), F32),
                   jax.ShapeDtypeStruct((1, Q_RANK), F32), jax.ShapeDtypeStruct((1, KV_RANK), F32)],
        compiler_params=_cparams("arbitrary"),
    )(a, g_qa, g_kva, w_uq, w_ukv, g_q, g_k, cc, sa, sb, dq, dk, dv)


def _flash_fwd(q, k, v, pos_col, pos_row, *, name):
    nh, s, _ = q.shape
    tq = _tile(s, FWD_TQ)
    tk = _tile(s, FWD_TK)
    sq = tq // ATTN_CHAINS

    def body(q_ref, k_ref, v_ref, pq_ref, pk_ref, o_ref, lse_ref, m_sc, acc_sc):
        qb = pl.program_id(1)
        m_sc[...] = jnp.full_like(m_sc, NEG)
        acc_sc[...] = jnp.zeros_like(acc_sc)

        def step(kb, masked):
            keys = pl.ds(pl.multiple_of(kb * tk, tk), tk)
            kt = k_ref[0, keys, :]
            vt = v_ref[0, keys, :]
            scores = [_dot_nt(q_ref[0, u * sq:(u + 1) * sq, :], kt) for u in range(ATTN_CHAINS)]
            for u in range(ATTN_CHAINS):
                rows = slice(u * sq, (u + 1) * sq)
                sc = scores[u]
                if masked:
                    sc = jnp.where(pq_ref[rows, :] >= pk_ref[:, keys], sc, NEG)
                m_prev = m_sc[rows, :]
                m_new = jnp.maximum(m_prev, jnp.max(sc, axis=-1, keepdims=True))
                alpha = jnp.exp2(m_prev - m_new)
                p = jnp.exp2(sc - jnp.tile(m_new, (1, tk // 128)))
                acc_sc[rows, :] = jnp.tile(alpha, (1, 2)) * acc_sc[rows, :] + _dot(p.astype(BF16), vt)
                m_sc[rows, :] = m_new

        n_before = (qb * tq) // tk
        n_seen = (qb * tq + tq - 1) // tk + 1
        lax.fori_loop(0, n_before, lambda kb, c: (step(kb, False), c)[1], 0)
        lax.fori_loop(n_before, n_seen, lambda kb, c: (step(kb, True), c)[1], 0)
        l = acc_sc[:, V_DIM:2 * V_DIM]
        o_ref[...] = (acc_sc[:, 0:V_DIM] / l).astype(BF16)
        lse_ref[0] = m_sc[:, 0:1] * (1.0 / LOG2E) + jnp.log(l[:, 0:1])

    return pl.pallas_call(
        body, name=name, grid=(nh, s // tq),
        in_specs=[pl.BlockSpec((1, tq, QK_PAD), lambda h, qb: (h, qb, 0)),
                  pl.BlockSpec((1, s, QK_PAD), lambda h, qb: (h, 0, 0)),
                  pl.BlockSpec((1, s, 2 * V_DIM), lambda h, qb: (h, 0, 0)),
                  pl.BlockSpec((tq, 1), lambda h, qb: (qb, 0)),
                  pl.BlockSpec((1, s), lambda h, qb: (0, 0))],
        out_specs=[pl.BlockSpec((tq, V_DIM), lambda h, qb: (qb, h)),
                   pl.BlockSpec((1, tq, 1), lambda h, qb: (h, qb, 0))],
        scratch_shapes=[pltpu.VMEM((tq, 128), F32), pltpu.VMEM((tq, 2 * V_DIM), F32)],
        out_shape=[jax.ShapeDtypeStruct((s, nh * V_DIM), BF16), jax.ShapeDtypeStruct((nh, s, 1), F32)],
        compiler_params=_cparams("parallel", "parallel"),
    )(q, k, v, pos_col, pos_row)


def _attn_delta(do, o, *, name):
    s = do.shape[0]
    tm = _tile(s, TM)

    def body(do_ref, o_ref, d_ref):
        for h in range(N_HEADS):
            cols = slice(h * V_DIM, (h + 1) * V_DIM)
            d_ref[h] = jnp.sum(do_ref[:, cols].astype(F32) * o_ref[:, cols].astype(F32), axis=-1, keepdims=True)

    return pl.pallas_call(
        body, name=name, grid=(s // tm,),
        in_specs=[pl.BlockSpec((tm, N_HEADS * V_DIM), lambda i: (i, 0))] * 2,
        out_specs=pl.BlockSpec((N_HEADS, tm, 1), lambda i: (0, i, 0)),
        out_shape=jax.ShapeDtypeStruct((N_HEADS, s, 1), F32),
        compiler_params=_cparams("parallel"),
    )(do, o)


def _flash_bwd(q, k, v, do, lse_row, delta_row, pos_col, pos_row, *, name):
    nh, s, _ = q.shape
    tq = _tile(s, BWD_TQ)
    tk = _tile(s, BWD_TK)
    nq, nk = s // tq, s // tk
    sk = tk // ATTN_CHAINS

    def body(q_ref, k_ref, v_ref, do_ref, lse_ref, delta_ref, pq_ref, pk_ref,
             dq_ref, dk_ref, dv_ref, dk_sc, dv_sc):
        kb = pl.program_id(1)

        @pl.when(kb == 0)
        def _():
            dq_ref[...] = jnp.zeros_like(dq_ref)

        dk_sc[...] = jnp.zeros_like(dk_sc)
        dv_sc[...] = jnp.zeros_like(dv_sc)

        def step(qb, masked):
            qs = pl.ds(pl.multiple_of(qb * tq, tq), tq)
            qt = q_ref[0, qs, :]
            dot_ = do_ref[qs, :]
            lse2 = lse_ref[0, :, qs] * LOG2E
            delta = delta_ref[0, :, qs]
            dq_part = None
            sts = [_dot_nt(k_ref[0, u * sk:(u + 1) * sk, :], qt) for u in range(ATTN_CHAINS)]
            dpts = [_dot_nt(v_ref[0, u * sk:(u + 1) * sk, :], dot_) for u in range(ATTN_CHAINS)]
            for u in range(ATTN_CHAINS):
                rows = slice(u * sk, (u + 1) * sk)
                kt = k_ref[0, rows, :]
                pt = jnp.exp2(sts[u] - lse2)
                if masked:
                    pt = jnp.where(pq_ref[:, qs] >= pk_ref[rows, :], pt, 0.0)
                dv_sc[rows, :] += _dot(pt.astype(BF16), dot_)
                dst = (pt * (dpts[u] - delta)).astype(BF16)
                dk_sc[rows, :] += _dot(dst, qt)
                part = _dot_tn(dst, kt)
                dq_part = part if dq_part is None else dq_part + part
            dq_ref[0, qs, :] += dq_part

        q_first = (kb * tk) // tq
        q_clear = (kb * tk + tk - 1) // tq + 1
        lax.fori_loop(q_first, q_clear, lambda qb, c: (step(qb, True), c)[1], 0)
        lax.fori_loop(q_clear, nq, lambda qb, c: (step(qb, False), c)[1], 0)
        dk_ref[0] = dk_sc[...] * (1.0 / LOG2E)
        dv_ref[0] = dv_sc[...]

        @pl.when(kb == nk - 1)
        def _():
            dq_ref[...] = dq_ref[...] * SCALE

    return pl.pallas_call(
        body, name=name, grid=(nh, nk),
        in_specs=[pl.BlockSpec((1, s, QK_PAD), lambda h, kb: (h, 0, 0)),
                  pl.BlockSpec((1, tk, QK_PAD), lambda h, kb: (h, kb, 0)),
                  pl.BlockSpec((1, tk, V_DIM), lambda h, kb: (h, kb, 0)),
                  pl.BlockSpec((s, V_DIM), lambda h, kb: (0, h)),
                  pl.BlockSpec((1, 1, s), lambda h, kb: (h, 0, 0)),
                  pl.BlockSpec((1, 1, s), lambda h, kb: (h, 0, 0)),
                  pl.BlockSpec((1, s), lambda h, kb: (0, 0)),
                  pl.BlockSpec((tk, 1), lambda h, kb: (kb, 0))],
        out_specs=[pl.BlockSpec((1, s, QK_PAD), lambda h, kb: (h, 0, 0)),
                   pl.BlockSpec((1, tk, QK_PAD), lambda h, kb: (h, kb, 0)),
                   pl.BlockSpec((1, tk, V_DIM), lambda h, kb: (h, kb, 0))],
        scratch_shapes=[pltpu.VMEM((tk, QK_PAD), F32), pltpu.VMEM((tk, V_DIM), F32)],
        out_shape=[jax.ShapeDtypeStruct((nh, s, QK_PAD), F32), jax.ShapeDtypeStruct((nh, s, QK_PAD), F32),
                   jax.ShapeDtypeStruct((nh, s, V_DIM), F32)],
        compiler_params=_cparams("arbitrary", "arbitrary"),
    )(q, k, v, do, lse_row, delta_row, pos_row, pos_col)


def _loss_head(y, target, *, name):
    s, d = y.shape
    tm = _tile(s, TM)
    nt = s // tm

    def body(y_ref, t_ref, dy_ref, loss_ref, acc):
        i = pl.program_id(0)

        @pl.when(i == 0)
        def _():
            acc[...] = jnp.zeros_like(acc)

        e = y_ref[...] - t_ref[...]
        dy_ref[...] = e * (1.0 / d)
        acc[...] += jnp.sum((e * e).reshape(tm // 8, 8, d), axis=0)

        @pl.when(i == nt - 1)
        def _():
            loss_ref[...] = jnp.full((1, 128), 0.5 / d, F32) * jnp.sum(acc[...])

    return pl.pallas_call(
        body, name=name, grid=(nt,),
        in_specs=[pl.BlockSpec((tm, d), lambda i: (i, 0))] * 2,
        out_specs=[pl.BlockSpec((tm, d), lambda i: (i, 0)), pl.BlockSpec((1, 128), lambda i: (0, 0))],
        out_shape=[jax.ShapeDtypeStruct((s, d), F32), jax.ShapeDtypeStruct((1, 128), F32)],
        scratch_shapes=[pltpu.VMEM((8, d), F32)],
        compiler_params=_cparams("arbitrary"),
    )(y, target)


def _adamw(w, g, m, v, *, name):
    r, c = w.shape
    tr = _tile(r, 512) if r % 8 == 0 else r

    def body(w_ref, g_ref, m_ref, v_ref, d_ref, nm_ref, nv_ref):
        g_t = g_ref[...]
        nm = ADAM_B1 * m_ref[...] + (1.0 - ADAM_B1) * g_t
        nv = ADAM_B2 * v_ref[...] + (1.0 - ADAM_B2) * (g_t * g_t)
        m_hat = nm / (1.0 - ADAM_B1 ** ADAM_STEP)
        v_hat = nv / (1.0 - ADAM_B2 ** ADAM_STEP)
        d_ref[...] = -ADAM_LR * (m_hat / (jnp.sqrt(v_hat) + ADAM_EPS) + ADAM_WD * w_ref[...])
        nm_ref[...] = nm
        nv_ref[...] = nv

    spec = pl.BlockSpec((tr, c), lambda i: (i, 0))
    return pl.pallas_call(
        body, name=name, grid=(r // tr,), in_specs=[spec] * 4, out_specs=[spec] * 3,
        out_shape=[jax.ShapeDtypeStruct((r, c), F32)] * 3,
        compiler_params=_cparams("parallel"),
    )(w, g, m, v)


def _place():
    return lax.axis_index("x"), lax.axis_index("y"), lax.axis_index("c")


def _other_chips(x, y):
    return [(1 - x, y), (x, 1 - y), (1 - x, 1 - y)]


BIG = {
    "attn_w_down": ((2, 1024, 448), 1), "attn_w_uq": ((2, 256, 1536), 2), "attn_w_ukv": ((2, 128, 2048), 2),
    "attn_w_o": ((2, 1024, 1024), 1), "conv_w_in": ((2, 1024, 3072), 2),
    "conv_w_out": ((2, 1024, 1024), 1), "mlp_w1": ((4, 1024, 4096), 2), "mlp_w2": ((4, 4096, 1024), 1),
}
CONV_W = (2, 3, 1024)


def _shard_shape(name):
    shape, axis = BIG[name]
    return tuple(n // N_CHIPS if i == axis else n for i, n in enumerate(shape))


def _band(ref, name, layers, chip):
    shape, axis = BIG[name]
    width = shape[axis] // N_CHIPS
    if axis == 1:
        return ref.at[layers, pl.ds(chip * width, width), :]
    return ref.at[layers, :, pl.ds(chip * width, width)]


def _half(name, c):
    hl = BIG[name][0][0] // 2
    return pl.ds(c * hl, hl)


def _gather_weights(fulls):
    names = list(fulls)
    n = len(names)

    def body(*refs):
        outs = refs[n:2 * n]
        send_sems, recv_sems = refs[2 * n:]
        x, y, c = _place()
        chips = _other_chips(x, y)
        me = 2 * x + y

        def copy(k, ref, nm, layers, chip, to):
            band = _band(ref, nm, layers, chip)
            return pltpu.make_async_remote_copy(
                src_ref=band, dst_ref=band, send_sem=send_sems.at[k], recv_sem=recv_sems.at[k],
                device_id=to, device_id_type=MESH)

        sent = []
        for i, nm in enumerate(names):
            for j, chip in enumerate(chips):
                cp = copy(6 * i + j, outs[i], nm, _half(nm, c), me, (*chip, c))
                cp.start()
                sent.append(cp)
        for i, nm in enumerate(names):
            for j, (cx, cy) in enumerate(chips):
                copy(6 * i + j, outs[i], nm, _half(nm, c), 2 * cx + cy, (x, y, c)).wait_recv()
                fwd = copy(6 * i + 3 + j, outs[i], nm, _half(nm, c), 2 * cx + cy, (x, y, 1 - c))
                fwd.start()
                sent.append(fwd)
        for i, nm in enumerate(names):
            for j, (cx, cy) in enumerate(chips):
                copy(6 * i + 3 + j, outs[i], nm, _half(nm, 1 - c), 2 * cx + cy, (x, y, c)).wait_recv()
        for cp in sent:
            cp.wait_send()

    arrays = [fulls[nm] for nm in names]
    out = pl.pallas_call(
        body, name="gather_weights", in_specs=[ANY] * n, out_specs=[ANY] * n,
        out_shape=[jax.ShapeDtypeStruct(a.shape, a.dtype) for a in arrays],
        input_output_aliases={i: i for i in range(n)},
        scratch_shapes=[pltpu.SemaphoreType.DMA((6 * n,)), pltpu.SemaphoreType.DMA((6 * n,))],
    )(*arrays)
    return dict(zip(names, out))


def _swap_halves(grads):
    names = list(grads)
    n = len(names)

    def body(*refs):
        ins, outs = refs[:n], refs[n:2 * n]
        send_sems, recv_sems = refs[2 * n:]
        x, y, c = _place()
        copies = []
        for i, nm in enumerate(names):
            cp = pltpu.make_async_remote_copy(
                src_ref=ins[i].at[_half(nm, 1 - c)], dst_ref=outs[i], send_sem=send_sems.at[i],
                recv_sem=recv_sems.at[i], device_id=(x, y, 1 - c), device_id_type=MESH)
            cp.start()
            copies.append(cp)
        for cp in copies:
            cp.wait()

    arrays = [grads[nm] for nm in names]
    out = pl.pallas_call(
        body, name="grad_swap_halves", in_specs=[ANY] * n, out_specs=[ANY] * n,
        out_shape=[jax.ShapeDtypeStruct((a.shape[0] // 2,) + a.shape[1:], a.dtype) for a in arrays],
        scratch_shapes=[pltpu.SemaphoreType.DMA((n,)), pltpu.SemaphoreType.DMA((n,))],
    )(*arrays)
    return dict(zip(names, out))


def _sum_rows(rows, cols):
    t = rows
    while t * cols * 4 > SUM_BLOCK_BYTES and t % 16 == 0:
        t //= 2
    return t


def _chip_sum(g, r1, core, *, name):
    layers, rows, cols = g.shape
    hl = layers // 2
    tr = _sum_rows(rows, cols)

    def body(core_ref, g_ref, r_ref, o_ref):
        o_ref[...] = (g_ref[...] + r_ref[...]).astype(BF16)

    return pl.pallas_call(
        body, name=name,
        grid_spec=pltpu.PrefetchScalarGridSpec(
            num_scalar_prefetch=1, grid=(hl, rows // tr),
            in_specs=[pl.BlockSpec((1, tr, cols), lambda l, i, cr: (cr[0] * hl + l, i, 0)),
                      pl.BlockSpec((1, tr, cols), lambda l, i, cr: (l, i, 0))],
            out_specs=pl.BlockSpec((1, tr, cols), lambda l, i, cr: (l, i, 0))),
        out_shape=jax.ShapeDtypeStruct((hl, rows, cols), BF16),
        compiler_params=_cparams("parallel", "parallel"),
    )(core, g, r1)


def _scatter_partials(partials):
    names = list(partials)
    n = len(names)

    def body(*refs):
        ins, outs = refs[:n], refs[n:2 * n]
        send_sems, recv_sems = refs[2 * n:]
        x, y, c = _place()
        copies = []
        for i, nm in enumerate(names):
            for j, (cx, cy) in enumerate(_other_chips(x, y)):
                cp = pltpu.make_async_remote_copy(
                    src_ref=_band(ins[i], nm, slice(None), 2 * cx + cy), dst_ref=outs[i].at[j],
                    send_sem=send_sems.at[3 * i + j], recv_sem=recv_sems.at[3 * i + j],
                    device_id=(cx, cy, c), device_id_type=MESH)
                cp.start()
                copies.append(cp)
        for cp in copies:
            cp.wait()

    arrays = [partials[nm] for nm in names]
    out = pl.pallas_call(
        body, name="grad_scatter_partials", in_specs=[ANY] * n, out_specs=[ANY] * n,
        out_shape=[jax.ShapeDtypeStruct((3, a.shape[0]) + _shard_shape(nm)[1:], a.dtype)
                   for nm, a in zip(names, arrays)],
        scratch_shapes=[pltpu.SemaphoreType.DMA((3 * n,)), pltpu.SemaphoreType.DMA((3 * n,))],
    )(*arrays)
    return dict(zip(names, out))


def _final_sum(g, r1, r2, place, nm, *, name):
    (layers, _, _), axis = BIG[nm]
    hl = layers // 2
    _, rows, cols = _shard_shape(nm)
    tr = _sum_rows(rows, cols)
    nrb = rows // tr
    if axis == 1:
        blk = lambda l, i, pc: (l, pc[1] * nrb + i, 0)
    else:
        blk = lambda l, i, pc: (l, i, pc[1])

    def body(place_ref, g_ref, r1_ref, r2_ref, o_ref):
        acc = g_ref[...] + r1_ref[...]
        for j in range(3):
            acc = acc + r2_ref[j].astype(F32)
        o_ref[...] = acc

    return pl.pallas_call(
        body, name=name,
        grid_spec=pltpu.PrefetchScalarGridSpec(
            num_scalar_prefetch=1, grid=(hl, nrb),
            in_specs=[pl.BlockSpec((1, tr, cols), lambda l, i, pc: blk(pc[0] * hl + l, i, pc)),
                      pl.BlockSpec((1, tr, cols), lambda l, i, pc: blk(l, i, pc)),
                      pl.BlockSpec((3, 1, tr, cols), lambda l, i, pc: (0, l, i, 0))],
            out_specs=pl.BlockSpec((1, tr, cols), lambda l, i, pc: (pc[0] * hl + l, i, 0))),
        out_shape=jax.ShapeDtypeStruct((layers, rows, cols), F32),
        compiler_params=_cparams("parallel", "parallel"),
    )(place, g, r1, r2)


def _join_halves(shards):
    names = list(shards)
    n = len(names)

    def body(*refs):
        outs = refs[n:2 * n]
        send_sems, recv_sems = refs[2 * n:]
        x, y, c = _place()
        copies = []
        for i, nm in enumerate(names):
            mine = outs[i].at[_half(nm, c)]
            cp = pltpu.make_async_remote_copy(
                src_ref=mine, dst_ref=mine, send_sem=send_sems.at[i], recv_sem=recv_sems.at[i],
                device_id=(x, y, 1 - c), device_id_type=MESH)
            cp.start()
            copies.append(cp)
        for i, nm in enumerate(names):
            theirs = outs[i].at[_half(nm, 1 - c)]
            pltpu.make_async_remote_copy(
                src_ref=theirs, dst_ref=theirs, send_sem=send_sems.at[i], recv_sem=recv_sems.at[i],
                device_id=(x, y, 1 - c), device_id_type=MESH).wait_recv()
        for cp in copies:
            cp.wait_send()

    arrays = [shards[nm] for nm in names]
    out = pl.pallas_call(
        body, name="grad_join_halves", in_specs=[ANY] * n, out_specs=[ANY] * n,
        out_shape=[jax.ShapeDtypeStruct(a.shape, a.dtype) for a in arrays],
        input_output_aliases={i: i for i in range(n)},
        scratch_shapes=[pltpu.SemaphoreType.DMA((n,)), pltpu.SemaphoreType.DMA((n,))],
    )(*arrays)
    return dict(zip(names, out))


def _all_reduce_small(part, *, name):
    rows, cols = part.shape
    vm = pl.BlockSpec(memory_space=pltpu.VMEM)

    def body(p_ref, o_ref, land, send_sems, recv_sems):
        x, y, c = _place()
        me = 4 * x + 2 * y + c
        flips = [(dx, dy, dc) for dx in (0, 1) for dy in (0, 1) for dc in (0, 1)][1:]
        copies = []
        for k, (dx, dy, dc) in enumerate(flips):
            cp = pltpu.make_async_remote_copy(
                src_ref=p_ref, dst_ref=land.at[me], send_sem=send_sems.at[k], recv_sem=recv_sems.at[k],
                device_id=(1 - x if dx else x, 1 - y if dy else y, 1 - c if dc else c), device_id_type=MESH)
            cp.start()
            copies.append(cp)
        land[me] = p_ref[...]
        for cp in copies:
            cp.wait()
        acc = land[0]
        for j in range(1, 8):
            acc = acc + land[j]
        o_ref[...] = acc

    return pl.pallas_call(
        body, name=name, in_specs=[vm], out_specs=vm,
        out_shape=jax.ShapeDtypeStruct((rows, cols), F32),
        scratch_shapes=[pltpu.VMEM((8, rows, cols), F32), pltpu.SemaphoreType.DMA((7,)), pltpu.SemaphoreType.DMA((7,))],
    )(part)


SMALL = {"g_mix": (4, 1024), "g_mlp": (4, 1024), "attn_g_q_a": (2, 256), "attn_g_kv_a": (2, 128),
         "attn_g_qnorm": (2, 192), "attn_g_knorm": (2, 192)}
SMALL_GRADS = {**SMALL, "conv_w": CONV_W}
WEIGHT_ORDER = ["g_mix", "g_mlp", "attn_w_down", "attn_g_q_a", "attn_g_kv_a", "attn_w_uq", "attn_w_ukv",
                "attn_g_qnorm", "attn_g_knorm", "attn_w_o", "conv_w_in", "conv_w", "conv_w_out", "mlp_w1", "mlp_w2"]


def _prod(shape):
    n = 1
    for v in shape:
        n *= v
    return n


def _pack_small(parts, table):
    flat = [parts[n].reshape(-1) for n in table]
    size = sum(_prod(s) for s in table.values())
    rows = -(-size // (8 * 128)) * 8
    flat.append(jnp.zeros((rows * 128 - size,), F32))
    return jnp.concatenate(flat).reshape(rows, 128)


def _unpack_small(buf, table):
    flat = buf.reshape(-1)
    out, off = {}, 0
    for n, shp in table.items():
        out[n] = flat[off:off + _prod(shp)].reshape(shp)
        off += _prod(shp)
    return out


def _rope_tables(positions):
    inv_freq = ROPE_THETA ** (-jnp.arange(0, QK_ROPE, 2, dtype=F32) / QK_ROPE)
    ang = positions.astype(F32)[:, None] * inv_freq
    cos, sin = jnp.cos(ang), jnp.sin(ang)
    z32 = jnp.zeros_like(cos)
    z64 = jnp.zeros((positions.shape[0], 64), F32)
    cc = jnp.concatenate([cos, cos, z64], axis=1)
    sa = jnp.concatenate([-sin, z32, z64], axis=1)
    sb = jnp.concatenate([z32, sin, z64], axis=1)
    return cc, sa, sb


def _pad_heads(w, width):
    k = w.shape[0]
    w = w.reshape(k, N_HEADS, width)
    return jnp.pad(w, ((0, 0), (0, 0), (0, QK_PAD - width))).reshape(k, N_HEADS * QK_PAD)


def _local_step(x, positions, target, wb, gains):
    s = x.shape[0]
    cc, sa, sb = _rope_tables(positions)
    pos_col = positions.reshape(s, 1)
    pos_row = positions.reshape(1, s)

    saved = []
    for i in range(4):
        j = i // 2
        g_mix = gains["g_mix"][i:i + 1]
        g_mlp = gains["g_mlp"][i:i + 1]
        if i % 2 == 0:
            w_down = jnp.pad(wb["attn_w_down"][j], ((0, 0), (0, DOWN_PAD - DOWN_DIM)))
            w_uq = _pad_heads(wb["attn_w_uq"][j], QK_DIM)
            w_ukv = wb["attn_w_ukv"][j]
            g_qa = gains["attn_g_q_a"][j:j + 1]
            g_kva = gains["attn_g_kv_a"][j:j + 1]
            g_q = jnp.pad(gains["attn_g_qnorm"][j:j + 1], ((0, 0), (0, QK_PAD - QK_DIM)))
            g_k = jnp.pad(gains["attn_g_knorm"][j:j + 1], ((0, 0), (0, QK_PAD - QK_DIM)))
            h, a = _norm_mm(x, g_mix, w_down, out_dtype=F32, name=f"mla_down_{j}")
            cq, ckv, q, k, v = _mla_prep(a, g_qa, g_kva, w_uq, w_ukv, g_q, g_k, cc, sa, sb, name=f"mla_prep_{j}")
            o, lse = _flash_fwd(q, k, v, pos_col, pos_row, name=f"flash_fwd_{j}")
            x_mid = _mm_nn(o, wb["attn_w_o"], layer=j, out_dtype=F32, residual=x, name=f"mla_out_{j}")
            mix = dict(h=h, a=a, cq=cq, ckv=ckv, q=q, k=k, v=v, o=o, lse=lse, w_down=w_down, w_uq=w_uq, w_ukv=w_ukv,
                       g_qa=g_qa, g_kva=g_kva, g_q=g_q, g_k=g_k)
        else:
            h, bcu = _norm_mm(x, g_mix, wb["conv_w_in"], layer=j, out_dtype=F32, name=f"conv_in_{j}")
            z = _conv_gate(bcu, gains["conv_w"][j], name=f"conv_gate_{j}")
            x_mid = _mm_nn(z, wb["conv_w_out"], layer=j, out_dtype=F32, residual=x, name=f"conv_out_{j}")
            mix = dict(h=h, bcu=bcu, z=z)
        h2, u, act = _mlp_up(x_mid, g_mlp, wb["mlp_w1"], layer=i, name=f"mlp_up_{i}")
        x_out = _mm_nn(act, wb["mlp_w2"], layer=i, out_dtype=F32, residual=x_mid, name=f"mlp_down_{i}")
        saved.append(dict(x_in=x, x_mid=x_mid, mix=mix, h2=h2, u=u, act=act, g_mix=g_mix, g_mlp=g_mlp))
        x = x_out

    dx, loss = _loss_head(x, target, name="loss_head")

    gw = {n: None for n in BIG}
    g_uq = [None, None]
    gs = {n: [None] * SMALL_GRADS[n][0] for n in SMALL_GRADS}

    def wgrad(nm, layer, a, b, **kw):
        gw[nm] = _mm_tn(a, b, stack=gw[nm], layer=layer, layers=BIG[nm][0][0], name=f"{nm}_grad_{layer}", **kw)

    for i in reversed(range(4)):
        j = i // 2
        sv = saved[i]
        mix = sv["mix"]
        du = _mlp_down_bwd(dx, wb["mlp_w2"], sv["u"], layer=i, name=f"mlp_down_bwd_{i}")
        wgrad("mlp_w2", i, sv["act"], dx)
        wgrad("mlp_w1", i, sv["h2"], du)
        dx, dg = _nt_rms_bwd(du, wb["mlp_w1"], sv["x_mid"], sv["g_mlp"], dx, layer=i, name=f"mlp_up_bwd_{i}")
        gs["g_mlp"][i] = dg[0]
        if i % 2 == 0:
            do = _mm_nt(dx, wb["attn_w_o"], layer=j, out_dtype=BF16, name=f"mla_out_bwd_{j}")
            wgrad("attn_w_o", j, mix["o"], dx)
            delta = _attn_delta(do, mix["o"], name=f"attn_delta_{j}")
            lse_row = mix["lse"].reshape(N_HEADS, 1, s)
            delta_row = delta.reshape(N_HEADS, 1, s)
            dq, dk, dv = _flash_bwd(mix["q"], mix["k"], mix["v"], do, lse_row, delta_row, pos_col, pos_row,
                                    name=f"flash_bwd_{j}")
            dqr, dkvr, da, dgq, dgk, dgqa, dgkva = _mla_prep_bwd(
                mix["a"], mix["g_qa"], mix["g_kva"], mix["w_uq"], mix["w_ukv"], mix["g_q"], mix["g_k"], cc, sa, sb,
                dq, dk, dv, name=f"mla_prep_bwd_{j}")
            g_uq[j] = _mm_tn(mix["cq"], dqr, name=f"attn_w_uq_grad_{j}")[0]
            wgrad("attn_w_ukv", j, mix["ckv"], dkvr)
            wgrad("attn_w_down", j, mix["h"], da, keep=DOWN_DIM)
            dx, dg = _nt_rms_bwd(da, mix["w_down"], sv["x_in"], sv["g_mix"], dx, name=f"mla_down_bwd_{j}")
            gs["attn_g_qnorm"][j] = dgq[0, :QK_DIM]
            gs["attn_g_knorm"][j] = dgk[0, :QK_DIM]
            gs["attn_g_q_a"][j] = dgqa[0]
            gs["attn_g_kv_a"][j] = dgkva[0]
        else:
            dz = _mm_nt(dx, wb["conv_w_out"], layer=j, out_dtype=F32, name=f"conv_out_bwd_{j}")
            wgrad("conv_w_out", j, mix["z"], dx)
            dbcu, dcw = _conv_gate_bwd(mix["bcu"], dz, gains["conv_w"][j], name=f"conv_gate_bwd_{j}")
            gs["conv_w"][j] = dcw
            wgrad("conv_w_in", j, mix["h"], dbcu)
            dx, dg = _nt_rms_bwd(dbcu, wb["conv_w_in"], sv["x_in"], sv["g_mix"], dx, layer=j, name=f"conv_in_bwd_{j}")
        gs["g_mix"][i] = dg[0]

    gw["attn_w_uq"] = jnp.stack(g_uq).reshape(2, Q_RANK, N_HEADS, QK_PAD)[..., :QK_DIM].reshape(BIG["attn_w_uq"][0])
    grads_small = {n: jnp.stack(v) for n, v in gs.items()}
    return loss, dx, gw, grads_small


def kernel(x, positions, g_mix, g_mlp, attn_w_down, attn_g_q_a, attn_g_kv_a, attn_w_uq, attn_w_ukv, attn_g_qnorm, attn_g_knorm, attn_w_o, conv_w_in, conv_w, conv_w_out, mlp_w1, mlp_w2, loss_target, m_g_mix, m_g_mlp, m_attn_w_down, m_attn_g_q_a, m_attn_g_kv_a, m_attn_w_uq, m_attn_w_ukv, m_attn_g_qnorm, m_attn_g_knorm, m_attn_w_o, m_conv_w_in, m_conv_w, m_conv_w_out, m_mlp_w1, m_mlp_w2, v_g_mix, v_g_mlp, v_attn_w_down, v_attn_g_q_a, v_attn_g_kv_a, v_attn_w_uq, v_attn_w_ukv, v_attn_g_qnorm, v_attn_g_knorm, v_attn_w_o, v_conv_w_in, v_conv_w, v_conv_w_out, v_mlp_w1, v_mlp_w2):
    w = dict(g_mix=g_mix, g_mlp=g_mlp, attn_w_down=attn_w_down, attn_g_q_a=attn_g_q_a, attn_g_kv_a=attn_g_kv_a,
             attn_w_uq=attn_w_uq, attn_w_ukv=attn_w_ukv, attn_g_qnorm=attn_g_qnorm, attn_g_knorm=attn_g_knorm,
             attn_w_o=attn_w_o, conv_w_in=conv_w_in, conv_w=conv_w, conv_w_out=conv_w_out, mlp_w1=mlp_w1, mlp_w2=mlp_w2)
    m = dict(g_mix=m_g_mix, g_mlp=m_g_mlp, attn_w_down=m_attn_w_down, attn_g_q_a=m_attn_g_q_a,
             attn_g_kv_a=m_attn_g_kv_a, attn_w_uq=m_attn_w_uq, attn_w_ukv=m_attn_w_ukv, attn_g_qnorm=m_attn_g_qnorm,
             attn_g_knorm=m_attn_g_knorm, attn_w_o=m_attn_w_o, conv_w_in=m_conv_w_in, conv_w=m_conv_w,
             conv_w_out=m_conv_w_out, mlp_w1=m_mlp_w1, mlp_w2=m_mlp_w2)
    v = dict(g_mix=v_g_mix, g_mlp=v_g_mlp, attn_w_down=v_attn_w_down, attn_g_q_a=v_attn_g_q_a,
             attn_g_kv_a=v_attn_g_kv_a, attn_w_uq=v_attn_w_uq, attn_w_ukv=v_attn_w_ukv, attn_g_qnorm=v_attn_g_qnorm,
             attn_g_knorm=v_attn_g_knorm, attn_w_o=v_attn_w_o, conv_w_in=v_conv_w_in, conv_w=v_conv_w,
             conv_w_out=v_conv_w_out, mlp_w1=v_mlp_w1, mlp_w2=v_mlp_w2)
    cx, cy, cc_ = _place()

    chip = 2 * cx + cy

    def own_offset(shape, axis):
        return tuple(chip * (shape[axis] // N_CHIPS) if i == axis else 0 for i in range(3))

    fulls = {}
    for n, (shape, axis) in BIG.items():
        fulls[n] = lax.dynamic_update_slice(jnp.zeros(shape, BF16), w[n].astype(BF16), own_offset(shape, axis))
    wb = _gather_weights(fulls)

    placed = lax.dynamic_update_slice(jnp.zeros(CONV_W, F32), conv_w, own_offset(CONV_W, 2))
    conv_w_full = 0.5 * _all_reduce_small(placed.reshape(-1, 128), name="conv_w_gather").reshape(CONV_W)

    gains = {n: w[n] for n in SMALL}
    gains["conv_w"] = conv_w_full

    loss, grad_x, grads_big, grads_small = _local_step(x[0], positions[0], loss_target[0], wb, gains)

    core = cc_.astype(jnp.int32).reshape(1)
    place = jnp.stack([cc_, chip]).astype(jnp.int32)
    r1 = _swap_halves(grads_big)
    partials = {n: _chip_sum(grads_big[n], r1[n], core, name=f"grad_chip_sum_{n}") for n in BIG}
    r2 = _scatter_partials(partials)
    halves = {n: _final_sum(grads_big[n], r1[n], r2[n], place, n, name=f"grad_final_sum_{n}") for n in BIG}
    grad_shards = _join_halves(halves)

    small = _unpack_small(_all_reduce_small(_pack_small(grads_small, SMALL_GRADS), name="gain_all_reduce"), SMALL_GRADS)
    grad_shards["conv_w"] = lax.dynamic_slice(small["conv_w"], own_offset(CONV_W, 2), conv_w.shape)

    loss_total = lax.psum(loss[0, 0], ("x", "y", "c"))

    grads, deltas, new_m, new_v = {}, {}, {}, {}
    for n in [*BIG, "conv_w"]:
        shp = w[n].shape
        two_d = (shp[0] * shp[1], shp[2])
        g2 = grad_shards[n].reshape(two_d)
        d, nm, nv = _adamw(w[n].reshape(two_d), g2, m[n].reshape(two_d), v[n].reshape(two_d), name=f"adamw_{n}")
        grads[n], deltas[n], new_m[n], new_v[n] = grad_shards[n], d.reshape(shp), nm.reshape(shp), nv.reshape(shp)
    d, nm, nv = _adamw(_pack_small(w, SMALL), _pack_small(small, SMALL), _pack_small(m, SMALL), _pack_small(v, SMALL),
                       name="adamw_gains")
    d, nm, nv = _unpack_small(d, SMALL), _unpack_small(nm, SMALL), _unpack_small(nv, SMALL)
    for n in SMALL:
        grads[n], deltas[n], new_m[n], new_v[n] = small[n], d[n], nm[n], nv[n]

    return (loss_total, grad_x[None],
            *[grads[n] for n in WEIGHT_ORDER], *[deltas[n] for n in WEIGHT_ORDER],
            *[new_m[n] for n in WEIGHT_ORDER], *[new_v[n] for n in WEIGHT_ORDER])
```

```python
import functools

import jax
import jax.numpy as jnp
from jax import lax
from jax.experimental import pallas as pl
from jax.experimental.pallas import tpu as pltpu

F32 = jnp.float32
BF16 = jnp.bfloat16

D_MODEL = 1024
N_HEADS = 8
QK_NOPE = 128
QK_ROPE = 64
QK_DIM = QK_NOPE + QK_ROPE
QK_PAD = 256
V_DIM = 128
Q_RANK = 256
KV_RANK = 128
DOWN_DIM = Q_RANK + KV_RANK + QK_ROPE
DOWN_PAD = 512
D_FF = 4 * D_MODEL
ROPE_THETA = 10000.0
EPS = 1e-6
NEG = -1e30
SCALE = QK_DIM ** -0.5
SCALE_LOG2E = SCALE * 1.4426950408889634
LOG2E = 1.4426950408889634
ATTN_CHAINS = 2

ADAM_LR = 0.001
ADAM_B1 = 0.9
ADAM_B2 = 0.999
ADAM_EPS = 1e-08
ADAM_WD = 0.01
ADAM_STEP = 10

N_CHIPS = 4
MESH = pl.DeviceIdType.MESH
ANY = pl.BlockSpec(memory_space=pl.ANY)

TM = 512
TM_WIDE = 256
FWD_TQ = 1024
FWD_TK = 1024
BWD_TQ = 1024
BWD_TK = 1024
T_PREP = 256
T_RED = 1024
SUM_BLOCK_BYTES = 2 * 1024 * 1024


def _tile(n, pref):
    t = min(n, pref)
    assert n % t == 0, (n, t)
    return t


def _cparams(*sem):
    return pltpu.CompilerParams(dimension_semantics=sem)


def _dot(a, b):
    return jnp.dot(a, b, preferred_element_type=F32)


def _dot_nt(a, b):
    return lax.dot_general(a, b, (((1,), (1,)), ((), ())), preferred_element_type=F32)


def _dot_tn(a, b):
    return lax.dot_general(a, b, (((0,), (0,)), ((), ())), preferred_element_type=F32)


def _rms(x, width):
    r = lax.rsqrt(jnp.sum(x * x, axis=-1, keepdims=True) * (1.0 / width) + EPS)
    return x * r, r


def _rms_bwd(xhat, r, dxhat, width):
    return r * (dxhat - xhat * (jnp.sum(dxhat * xhat, axis=-1, keepdims=True) * (1.0 / width)))


def _rope(t, cc, sa, sb):
    return t * cc + pltpu.roll(t, 96, 1) * sa + pltpu.roll(t, 32, 1) * sb


def _rope_t(g, cc, sa, sb):
    return g * cc + pltpu.roll(g * sa, 32, 1) + pltpu.roll(g * sb, 96, 1)


def _wspec(w, layer):
    if w.ndim == 2:
        return pl.BlockSpec(w.shape, lambda *_: (0, 0))
    return pl.BlockSpec((None,) + w.shape[1:], lambda *_: (layer, 0, 0))


def _mm_nn(a, b, *, out_dtype, name, residual=None, layer=0):
    m, k = a.shape
    n = b.shape[-1]
    tm = _tile(m, TM)

    def body(*refs):
        if residual is None:
            a_ref, b_ref, o_ref = refs
        else:
            a_ref, b_ref, r_ref, o_ref = refs
        acc = _dot(a_ref[...].astype(BF16), b_ref[...])
        if residual is not None:
            acc = acc + r_ref[...]
        o_ref[...] = acc.astype(o_ref.dtype)

    in_specs = [pl.BlockSpec((tm, k), lambda i: (i, 0)), _wspec(b, layer)]
    args = [a, b]
    if residual is not None:
        in_specs.append(pl.BlockSpec((tm, n), lambda i: (i, 0)))
        args.append(residual)
    return pl.pallas_call(
        body, name=name, grid=(m // tm,), in_specs=in_specs,
        out_specs=pl.BlockSpec((tm, n), lambda i: (i, 0)),
        out_shape=jax.ShapeDtypeStruct((m, n), out_dtype),
        compiler_params=_cparams("parallel"),
    )(*args)


def _mm_nt(a, b, *, out_dtype, name, layer=0):
    m, k = a.shape
    n = b.shape[-2]
    tm = _tile(m, TM)

    def body(a_ref, b_ref, o_ref):
        o_ref[...] = _dot_nt(a_ref[...].astype(BF16), b_ref[...]).astype(o_ref.dtype)

    return pl.pallas_call(
        body, name=name, grid=(m // tm,),
        in_specs=[pl.BlockSpec((tm, k), lambda i: (i, 0)), _wspec(b, layer)],
        out_specs=pl.BlockSpec((tm, n), lambda i: (i, 0)),
        out_shape=jax.ShapeDtypeStruct((m, n), out_dtype),
        compiler_params=_cparams("parallel"),
    )(a, b)


def _mm_tn(a, b, *, name, stack=None, layer=0, layers=1, keep=None):
    s, ka = a.shape
    n = b.shape[1]
    ts = _tile(s, T_RED)
    tka = _tile(ka, 1024)
    tn = _tile(n, 1024)
    n_out = n if keep is None else keep
    assert keep is None or tn == n

    def body(a_ref, b_ref, *rest):
        o_ref = rest[-1]

        @pl.when(pl.program_id(2) == 0)
        def _():
            o_ref[...] = jnp.zeros_like(o_ref)

        o_ref[...] += _dot_tn(a_ref[...].astype(BF16), b_ref[...].astype(BF16))[:, :n_out if keep else tn]

    in_specs = [pl.BlockSpec((ts, tka), lambda i, j, t: (t, i)), pl.BlockSpec((ts, tn), lambda i, j, t: (t, j))]
    args = [a, b]
    if stack is not None:
        in_specs.append(ANY)
        args.append(stack)
    return pl.pallas_call(
        body, name=name, grid=(ka // tka, n // tn, s // ts), in_specs=in_specs,
        out_specs=pl.BlockSpec((None, tka, tn if keep is None else keep), lambda i, j, t: (layer, i, j)),
        out_shape=jax.ShapeDtypeStruct((layers, ka, n_out), F32),
        input_output_aliases={} if stack is None else {2: 0},
        compiler_params=_cparams("parallel", "parallel", "arbitrary"),
    )(*args)


def _norm_mm(x, g, w, *, out_dtype, name, layer=0):
    s, d = x.shape
    n = w.shape[-1]
    tm = _tile(s, TM)

    def body(x_ref, g_ref, w_ref, h_ref, o_ref):
        xhat, _ = _rms(x_ref[...], d)
        h = (xhat * g_ref[...]).astype(BF16)
        h_ref[...] = h
        o_ref[...] = _dot(h, w_ref[...]).astype(o_ref.dtype)

    return pl.pallas_call(
        body, name=name, grid=(s // tm,),
        in_specs=[pl.BlockSpec((tm, d), lambda i: (i, 0)), pl.BlockSpec((1, d), lambda i: (0, 0)), _wspec(w, layer)],
        out_specs=[pl.BlockSpec((tm, d), lambda i: (i, 0)), pl.BlockSpec((tm, n), lambda i: (i, 0))],
        out_shape=[jax.ShapeDtypeStruct((s, d), BF16), jax.ShapeDtypeStruct((s, n), out_dtype)],
        compiler_params=_cparams("parallel"),
    )(x, g, w)


def _nt_rms_bwd(dy, w, x, g, dres, *, name, layer=0):
    s, n = dy.shape
    d = x.shape[1]
    tm = _tile(s, TM)

    def body(dy_ref, w_ref, x_ref, g_ref, dres_ref, dx_ref, dg_ref):
        @pl.when(pl.program_id(0) == 0)
        def _():
            dg_ref[...] = jnp.zeros_like(dg_ref)

        dh = _dot_nt(dy_ref[...], w_ref[...])
        xhat, r = _rms(x_ref[...], d)
        dg_ref[...] += jnp.sum(dh * xhat, axis=0, keepdims=True)
        dx_ref[...] = dres_ref[...] + _rms_bwd(xhat, r, dh * g_ref[...], d)

    return pl.pallas_call(
        body, name=name, grid=(s // tm,),
        in_specs=[pl.BlockSpec((tm, n), lambda i: (i, 0)), _wspec(w, layer),
                  pl.BlockSpec((tm, d), lambda i: (i, 0)), pl.BlockSpec((1, d), lambda i: (0, 0)),
                  pl.BlockSpec((tm, d), lambda i: (i, 0))],
        out_specs=[pl.BlockSpec((tm, d), lambda i: (i, 0)), pl.BlockSpec((1, d), lambda i: (0, 0))],
        out_shape=[jax.ShapeDtypeStruct((s, d), F32), jax.ShapeDtypeStruct((1, d), F32)],
        compiler_params=_cparams("arbitrary"),
    )(dy, w, x, g, dres)


def _mlp_up(x, g, w1, *, name, layer=0):
    s, d = x.shape
    n = w1.shape[-1]
    tm = _tile(s, TM_WIDE)

    def body(x_ref, g_ref, w_ref, h_ref, u_ref, act_ref):
        xhat, _ = _rms(x_ref[...], d)
        h = (xhat * g_ref[...]).astype(BF16)
        h_ref[...] = h
        u = _dot(h, w_ref[...])
        u_ref[...] = u.astype(BF16)
        act_ref[...] = jnp.square(jnp.maximum(u, 0.0)).astype(BF16)

    return pl.pallas_call(
        body, name=name, grid=(s // tm,),
        in_specs=[pl.BlockSpec((tm, d), lambda i: (i, 0)), pl.BlockSpec((1, d), lambda i: (0, 0)), _wspec(w1, layer)],
        out_specs=[pl.BlockSpec((tm, d), lambda i: (i, 0)), pl.BlockSpec((tm, n), lambda i: (i, 0)),
                   pl.BlockSpec((tm, n), lambda i: (i, 0))],
        out_shape=[jax.ShapeDtypeStruct((s, d), BF16), jax.ShapeDtypeStruct((s, n), BF16),
                   jax.ShapeDtypeStruct((s, n), BF16)],
        compiler_params=_cparams("parallel"),
    )(x, g, w1)


def _mlp_down_bwd(dy, w2, u, *, name, layer=0):
    s, d = dy.shape
    n = w2.shape[-2]
    tm = _tile(s, TM_WIDE)

    def body(dy_ref, w_ref, u_ref, du_ref):
        dact = _dot_nt(dy_ref[...].astype(BF16), w_ref[...])
        du_ref[...] = (dact * (2.0 * jnp.maximum(u_ref[...].astype(F32), 0.0))).astype(BF16)

    return pl.pallas_call(
        body, name=name, grid=(s // tm,),
        in_specs=[pl.BlockSpec((tm, d), lambda i: (i, 0)), _wspec(w2, layer),
                  pl.BlockSpec((tm, n), lambda i: (i, 0))],
        out_specs=pl.BlockSpec((tm, n), lambda i: (i, 0)),
        out_shape=jax.ShapeDtypeStruct((s, n), BF16),
        compiler_params=_cparams("parallel"),
    )(dy, w2, u)


def _conv_gate(bcu, conv_w, *, name):
    s = bcu.shape[0]
    d = D_MODEL
    tm = _tile(s, TM)
    hb = tm // 8

    def body(bcu_ref, prev_ref, w_ref, z_ref, pbuf):
        i = pl.program_id(0)
        gb = bcu_ref[:, 0:d]
        p = bcu_ref[:, d:2 * d] * bcu_ref[:, 2 * d:3 * d]
        pprev = prev_ref[:, d:2 * d] * prev_ref[:, 2 * d:3 * d]
        pbuf[0:8, :] = jnp.where(i > 0, pprev, 0.0)
        pbuf[8:8 + tm, :] = p
        cv = w_ref[2:3, :] * p + w_ref[1:2, :] * pbuf[7:7 + tm, :] + w_ref[0:1, :] * pbuf[6:6 + tm, :]
        z_ref[...] = (gb * cv).astype(BF16)

    return pl.pallas_call(
        body, name=name, grid=(s // tm,),
        in_specs=[pl.BlockSpec((tm, 3 * d), lambda i: (i, 0)),
                  pl.BlockSpec((8, 3 * d), lambda i: (jnp.maximum(i * hb - 1, 0), 0)),
                  pl.BlockSpec((3, d), lambda i: (0, 0))],
        out_specs=pl.BlockSpec((tm, d), lambda i: (i, 0)),
        out_shape=jax.ShapeDtypeStruct((s, d), BF16),
        scratch_shapes=[pltpu.VMEM((tm + 8, d), F32)],
        compiler_params=_cparams("parallel"),
    )(bcu, bcu, conv_w)


def _conv_gate_bwd(bcu, dz, conv_w, *, name):
    s = bcu.shape[0]
    d = D_MODEL
    tm = _tile(s, TM)
    hb = tm // 8
    nt = s // tm

    def body(bcu_ref, prev_ref, next_ref, dz_ref, dznext_ref, w_ref, dbcu_ref, dw_ref, pbuf, dbuf):
        i = pl.program_id(0)

        @pl.when(i == 0)
        def _():
            dw_ref[...] = jnp.zeros_like(dw_ref)

        gb = bcu_ref[:, 0:d]
        gc = bcu_ref[:, d:2 * d]
        uu = bcu_ref[:, 2 * d:3 * d]
        p = gc * uu
        pprev = prev_ref[:, d:2 * d] * prev_ref[:, 2 * d:3 * d]
        pbuf[0:8, :] = jnp.where(i > 0, pprev, 0.0)
        pbuf[8:8 + tm, :] = p
        p1 = pbuf[7:7 + tm, :]
        p2 = pbuf[6:6 + tm, :]
        cv = w_ref[2:3, :] * p + w_ref[1:2, :] * p1 + w_ref[0:1, :] * p2
        dz_t = dz_ref[...]
        dcv = dz_t * gb
        dcv_next = dznext_ref[...] * next_ref[:, 0:d]
        dbuf[0:tm, :] = dcv
        dbuf[tm:tm + 8, :] = jnp.where(i < nt - 1, dcv_next, 0.0)
        dp = w_ref[2:3, :] * dcv + w_ref[1:2, :] * dbuf[1:1 + tm, :] + w_ref[0:1, :] * dbuf[2:2 + tm, :]
        dw_ref[2:3, :] += jnp.sum(dcv * p, axis=0, keepdims=True)
        dw_ref[1:2, :] += jnp.sum(dcv * p1, axis=0, keepdims=True)
        dw_ref[0:1, :] += jnp.sum(dcv * p2, axis=0, keepdims=True)
        dbcu_ref[:, 0:d] = (dz_t * cv).astype(BF16)
        dbcu_ref[:, d:2 * d] = (dp * uu).astype(BF16)
        dbcu_ref[:, 2 * d:3 * d] = (dp * gc).astype(BF16)

    nxt = lambda i: (jnp.minimum((i + 1) * hb, s // 8 - 1), 0)
    return pl.pallas_call(
        body, name=name, grid=(nt,),
        in_specs=[pl.BlockSpec((tm, 3 * d), lambda i: (i, 0)),
                  pl.BlockSpec((8, 3 * d), lambda i: (jnp.maximum(i * hb - 1, 0), 0)),
                  pl.BlockSpec((8, 3 * d), nxt),
                  pl.BlockSpec((tm, d), lambda i: (i, 0)),
                  pl.BlockSpec((8, d), nxt),
                  pl.BlockSpec((3, d), lambda i: (0, 0))],
        out_specs=[pl.BlockSpec((tm, 3 * d), lambda i: (i, 0)), pl.BlockSpec((3, d), lambda i: (0, 0))],
        out_shape=[jax.ShapeDtypeStruct((s, 3 * d), BF16), jax.ShapeDtypeStruct((3, d), F32)],
        scratch_shapes=[pltpu.VMEM((tm + 8, d), F32), pltpu.VMEM((tm + 8, d), F32)],
        compiler_params=_cparams("arbitrary"),
    )(bcu, bcu, bcu, dz, dz, conv_w)


def _mla_prep(a, g_qa, g_kva, w_uq, w_ukv, g_q, g_k, cc, sa, sb, *, name):
    s = a.shape[0]
    ts = _tile(s, T_PREP)

    def body(a_ref, gqa_ref, gkva_ref, wuq_ref, wukv_ref, gq_ref, gk_ref, cc_ref, sa_ref, sb_ref,
             cq_ref, ckv_ref, q_ref, k_ref, v_ref):
        xq, _ = _rms(a_ref[:, 0:Q_RANK], Q_RANK)
        cq = (xq * gqa_ref[...]).astype(BF16)
        cq_ref[...] = cq
        xkv, _ = _rms(a_ref[:, Q_RANK:Q_RANK + KV_RANK], KV_RANK)
        ckv = (xkv * gkva_ref[...]).astype(BF16)
        ckv_ref[...] = ckv
        kpe = a_ref[:, Q_RANK + KV_RANK:DOWN_PAD]
        kpe_ss = jnp.sum(kpe * kpe, axis=-1, keepdims=True)
        cc_t, sa_t, sb_t = cc_ref[...], sa_ref[...], sb_ref[...]
        gq = gq_ref[...]
        gk = gk_ref[...]
        for h in range(N_HEADS):
            cols = slice(h * QK_PAD, (h + 1) * QK_PAD)
            qhat, _ = _rms(_dot(cq, wuq_ref[:, cols]), QK_DIM)
            qn = qhat * (gq * SCALE_LOG2E)
            q_ref[h, :, 0:QK_NOPE] = qn[:, 0:QK_NOPE].astype(BF16)
            q_ref[h, :, QK_NOPE:QK_PAD] = _rope(qn[:, QK_NOPE:QK_PAD], cc_t, sa_t, sb_t).astype(BF16)
            kvr = _dot(ckv, wukv_ref[:, cols])
            kn = kvr[:, 0:QK_NOPE]
            rk = lax.rsqrt((jnp.sum(kn * kn, axis=-1, keepdims=True) + kpe_ss) * (1.0 / QK_DIM) + EPS)
            k_ref[h, :, 0:QK_NOPE] = (kn * rk * gk[:, 0:QK_NOPE]).astype(BF16)
            k_ref[h, :, QK_NOPE:QK_PAD] = _rope(kpe * rk * gk[:, QK_NOPE:QK_PAD], cc_t, sa_t, sb_t).astype(BF16)
            v_ref[h, :, 0:V_DIM] = kvr[:, QK_NOPE:QK_PAD].astype(BF16)
            v_ref[h, :, V_DIM:2 * V_DIM] = jnp.ones((ts, V_DIM), BF16)

    row = lambda i: (i, 0)
    fixed = lambda i: (0, 0)
    head = lambda i: (0, i, 0)
    return pl.pallas_call(
        body, name=name, grid=(s // ts,),
        in_specs=[pl.BlockSpec((ts, DOWN_PAD), row), pl.BlockSpec((1, Q_RANK), fixed), pl.BlockSpec((1, KV_RANK), fixed),
                  pl.BlockSpec((Q_RANK, N_HEADS * QK_PAD), fixed), pl.BlockSpec((KV_RANK, N_HEADS * QK_PAD), fixed),
                  pl.BlockSpec((1, QK_PAD), fixed), pl.BlockSpec((1, QK_PAD), fixed),
                  pl.BlockSpec((ts, 128), row), pl.BlockSpec((ts, 128), row), pl.BlockSpec((ts, 128), row)],
        out_specs=[pl.BlockSpec((ts, Q_RANK), row), pl.BlockSpec((ts, KV_RANK), row),
                   pl.BlockSpec((N_HEADS, ts, QK_PAD), head), pl.BlockSpec((N_HEADS, ts, QK_PAD), head),
                   pl.BlockSpec((N_HEADS, ts, 2 * V_DIM), head)],
        out_shape=[jax.ShapeDtypeStruct((s, Q_RANK), BF16), jax.ShapeDtypeStruct((s, KV_RANK), BF16),
                   jax.ShapeDtypeStruct((N_HEADS, s, QK_PAD), BF16), jax.ShapeDtypeStruct((N_HEADS, s, QK_PAD), BF16),
                   jax.ShapeDtypeStruct((N_HEADS, s, 2 * V_DIM), BF16)],
        compiler_params=_cparams("parallel"),
    )(a, g_qa, g_kva, w_uq, w_ukv, g_q, g_k, cc, sa, sb)


def _mla_prep_bwd(a, g_qa, g_kva, w_uq, w_ukv, g_q, g_k, cc, sa, sb, dq, dk, dv, *, name):
    s = a.shape[0]
    ts = _tile(s, T_PREP)

    def body(a_ref, gqa_ref, gkva_ref, wuq_ref, wukv_ref, gq_ref, gk_ref, cc_ref, sa_ref, sb_ref,
             dq_ref, dk_ref, dv_ref, dqr_ref, dkvr_ref, da_ref, dgq_ref, dgk_ref, dgqa_ref, dgkva_ref):
        @pl.when(pl.program_id(0) == 0)
        def _():
            dgq_ref[...] = jnp.zeros_like(dgq_ref)
            dgk_ref[...] = jnp.zeros_like(dgk_ref)
            dgqa_ref[...] = jnp.zeros_like(dgqa_ref)
            dgkva_ref[...] = jnp.zeros_like(dgkva_ref)

        xq, r_q = _rms(a_ref[:, 0:Q_RANK], Q_RANK)
        cq = (xq * gqa_ref[...]).astype(BF16)
        xkv, r_kv = _rms(a_ref[:, Q_RANK:Q_RANK + KV_RANK], KV_RANK)
        ckv = (xkv * gkva_ref[...]).astype(BF16)
        kpe = a_ref[:, Q_RANK + KV_RANK:DOWN_PAD]
        kpe_ss = jnp.sum(kpe * kpe, axis=-1, keepdims=True)
        cc_t, sa_t, sb_t = cc_ref[...], sa_ref[...], sb_ref[...]
        gq = gq_ref[...]
        gk = gk_ref[...]
        dcq = jnp.zeros((ts, Q_RANK), F32)
        dckv = jnp.zeros((ts, KV_RANK), F32)
        dkpe = jnp.zeros((ts, 128), F32)
        dgq = jnp.zeros((1, QK_PAD), F32)
        dgk_n = jnp.zeros((1, QK_NOPE), F32)
        dgk_p = jnp.zeros((1, 128), F32)
        for h in range(N_HEADS):
            cols = slice(h * QK_PAD, (h + 1) * QK_PAD)
            qhat, rq = _rms(_dot(cq, wuq_ref[:, cols]), QK_DIM)
            dqn = jnp.concatenate(
                [dq_ref[h, :, 0:QK_NOPE], _rope_t(dq_ref[h, :, QK_NOPE:QK_PAD], cc_t, sa_t, sb_t)], axis=1)
            dgq = dgq + jnp.sum(dqn * qhat, axis=0, keepdims=True)
            dqr = _rms_bwd(qhat, rq, dqn * gq, QK_DIM).astype(BF16)
            dqr_ref[:, cols] = dqr
            dcq = dcq + _dot_nt(dqr, wuq_ref[:, cols])
            kn = _dot(ckv, wukv_ref[:, h * QK_PAD:h * QK_PAD + QK_NOPE])
            rk = lax.rsqrt((jnp.sum(kn * kn, axis=-1, keepdims=True) + kpe_ss) * (1.0 / QK_DIM) + EPS)
            khat_n = kn * rk
            khat_p = kpe * rk
            dkn = dk_ref[h, :, 0:QK_NOPE]
            dkp = _rope_t(dk_ref[h, :, QK_NOPE:QK_PAD], cc_t, sa_t, sb_t)
            dgk_n = dgk_n + jnp.sum(dkn * khat_n, axis=0, keepdims=True)
            dgk_p = dgk_p + jnp.sum(dkp * khat_p, axis=0, keepdims=True)
            dxn = dkn * gk[:, 0:QK_NOPE]
            dxp = dkp * gk[:, QK_NOPE:QK_PAD]
            mean = (jnp.sum(dxn * khat_n, axis=-1, keepdims=True)
                    + jnp.sum(dxp * khat_p, axis=-1, keepdims=True)) * (1.0 / QK_DIM)
            dkpe = dkpe + rk * (dxp - khat_p * mean)
            dkvr = jnp.concatenate([rk * (dxn - khat_n * mean), dv_ref[h, :, :]], axis=1).astype(BF16)
            dkvr_ref[:, cols] = dkvr
            dckv = dckv + _dot_nt(dkvr, wukv_ref[:, cols])
        dgq_ref[...] += dgq
        dgk_ref[:, 0:QK_NOPE] += dgk_n
        dgk_ref[:, QK_NOPE:QK_PAD] += dgk_p
        dgqa_ref[...] += jnp.sum(dcq * xq, axis=0, keepdims=True)
        dgkva_ref[...] += jnp.sum(dckv * xkv, axis=0, keepdims=True)
        da_ref[:, 0:Q_RANK] = _rms_bwd(xq, r_q, dcq * gqa_ref[...], Q_RANK).astype(BF16)
        da_ref[:, Q_RANK:Q_RANK + KV_RANK] = _rms_bwd(xkv, r_kv, dckv * gkva_ref[...], KV_RANK).astype(BF16)
        da_ref[:, Q_RANK + KV_RANK:DOWN_PAD] = dkpe.astype(BF16)

    row = lambda i: (i, 0)
    fixed = lambda i: (0, 0)
    head = lambda i: (0, i, 0)
    wide = N_HEADS * QK_PAD
    return pl.pallas_call(
        body, name=name, grid=(s // ts,),
        in_specs=[pl.BlockSpec((ts, DOWN_PAD), row), pl.BlockSpec((1, Q_RANK), fixed), pl.BlockSpec((1, KV_RANK), fixed),
                  pl.BlockSpec((Q_RANK, wide), fixed), pl.BlockSpec((KV_RANK, wide), fixed),
                  pl.BlockSpec((1, QK_PAD), fixed), pl.BlockSpec((1, QK_PAD), fixed),
                  pl.BlockSpec((ts, 128), row), pl.BlockSpec((ts, 128), row), pl.BlockSpec((ts, 128), row),
                  pl.BlockSpec((N_HEADS, ts, QK_PAD), head), pl.BlockSpec((N_HEADS, ts, QK_PAD), head),
                  pl.BlockSpec((N_HEADS, ts, V_DIM), head)],
        out_specs=[pl.BlockSpec((ts, wide), row), pl.BlockSpec((ts, wide), row), pl.BlockSpec((ts, DOWN_PAD), row),
                   pl.BlockSpec((1, QK_PAD), fixed), pl.BlockSpec((1, QK_PAD), fixed),
                   pl.BlockSpec((1, Q_RANK), fixed), pl.BlockSpec((1, KV_RANK), fixed)],
        out_shape=[jax.ShapeDtypeStruct((s, wide), BF16), jax.ShapeDtypeStruct((s, wide), BF16),
                   jax.ShapeDtypeStruct((s, DOWN_PAD), BF16),
                   jax.ShapeDtypeStruct((1, QK_PAD), F32), jax.ShapeDtypeStruct((1, QK_PAD), F32),
                   jax.ShapeDtypeStruct((1, Q_RANK), F32), jax.ShapeDtypeStruct((1, KV_RANK), F32)],
        compiler_params=_cparams("arbitrary"),
    )(a, g_qa, g_kva, w_uq, w_ukv, g_q, g_k, cc, sa, sb, dq, dk, dv)


def _flash_fwd(q, k, v, pos_col, pos_row, *, name):
    nh, s, _ = q.shape
    tq = _tile(s, FWD_TQ)
    tk = _tile(s, FWD_TK)
    sq = tq // ATTN_CHAINS

    def body(q_ref, k_ref, v_ref, pq_ref, pk_ref, o_ref, lse_ref, m_sc, acc_sc):
        qb = pl.program_id(1)
        m_sc[...] = jnp.full_like(m_sc, NEG)
        acc_sc[...] = jnp.zeros_like(acc_sc)

        def step(kb, masked):
            keys = pl.ds(pl.multiple_of(kb * tk, tk), tk)
            kt = k_ref[0, keys, :]
            vt = v_ref[0, keys, :]
            scores = [_dot_nt(q_ref[0, u * sq:(u + 1) * sq, :], kt) for u in range(ATTN_CHAINS)]
            for u in range(ATTN_CHAINS):
                rows = slice(u * sq, (u + 1) * sq)
                sc = scores[u]
                if masked:
                    sc = jnp.where(pq_ref[rows, :] >= pk_ref[:, keys], sc, NEG)
                m_prev = m_sc[rows, :]
                m_new = jnp.maximum(m_prev, jnp.max(sc, axis=-1, keepdims=True))
                alpha = jnp.exp2(m_prev - m_new)
                p = jnp.exp2(sc - jnp.tile(m_new, (1, tk // 128)))
                acc_sc[rows, :] = jnp.tile(alpha, (1, 2)) * acc_sc[rows, :] + _dot(p.astype(BF16), vt)
                m_sc[rows, :] = m_new

        n_before = (qb * tq) // tk
        n_seen = (qb * tq + tq - 1) // tk + 1
        lax.fori_loop(0, n_before, lambda kb, c: (step(kb, False), c)[1], 0)
        lax.fori_loop(n_before, n_seen, lambda kb, c: (step(kb, True), c)[1], 0)
        l = acc_sc[:, V_DIM:2 * V_DIM]
        o_ref[...] = (acc_sc[:, 0:V_DIM] / l).astype(BF16)
        lse_ref[0] = m_sc[:, 0:1] * (1.0 / LOG2E) + jnp.log(l[:, 0:1])

    return pl.pallas_call(
        body, name=name, grid=(nh, s // tq),
        in_specs=[pl.BlockSpec((1, tq, QK_PAD), lambda h, qb: (h, qb, 0)),
                  pl.BlockSpec((1, s, QK_PAD), lambda h, qb: (h, 0, 0)),
                  pl.BlockSpec((1, s, 2 * V_DIM), lambda h, qb: (h, 0, 0)),
                  pl.BlockSpec((tq, 1), lambda h, qb: (qb, 0)),
                  pl.BlockSpec((1, s), lambda h, qb: (0, 0))],
        out_specs=[pl.BlockSpec((tq, V_DIM), lambda h, qb: (qb, h)),
                   pl.BlockSpec((1, tq, 1), lambda h, qb: (h, qb, 0))],
        scratch_shapes=[pltpu.VMEM((tq, 128), F32), pltpu.VMEM((tq, 2 * V_DIM), F32)],
        out_shape=[jax.ShapeDtypeStruct((s, nh * V_DIM), BF16), jax.ShapeDtypeStruct((nh, s, 1), F32)],
        compiler_params=_cparams("parallel", "parallel"),
    )(q, k, v, pos_col, pos_row)


def _attn_delta(do, o, *, name):
    s = do.shape[0]
    tm = _tile(s, TM)

    def body(do_ref, o_ref, d_ref):
        for h in range(N_HEADS):
            cols = slice(h * V_DIM, (h + 1) * V_DIM)
            d_ref[h] = jnp.sum(do_ref[:, cols].astype(F32) * o_ref[:, cols].astype(F32), axis=-1, keepdims=True)

    return pl.pallas_call(
        body, name=name, grid=(s // tm,),
        in_specs=[pl.BlockSpec((tm, N_HEADS * V_DIM), lambda i: (i, 0))] * 2,
        out_specs=pl.BlockSpec((N_HEADS, tm, 1), lambda i: (0, i, 0)),
        out_shape=jax.ShapeDtypeStruct((N_HEADS, s, 1), F32),
        compiler_params=_cparams("parallel"),
    )(do, o)


def _flash_bwd(q, k, v, do, lse_row, delta_row, pos_col, pos_row, *, name):
    nh, s, _ = q.shape
    tq = _tile(s, BWD_TQ)
    tk = _tile(s, BWD_TK)
    nq, nk = s // tq, s // tk
    sk = tk // ATTN_CHAINS

    def body(q_ref, k_ref, v_ref, do_ref, lse_ref, delta_ref, pq_ref, pk_ref,
             dq_ref, dk_ref, dv_ref, dk_sc, dv_sc):
        kb = pl.program_id(1)

        @pl.when(kb == 0)
        def _():
            dq_ref[...] = jnp.zeros_like(dq_ref)

        dk_sc[...] = jnp.zeros_like(dk_sc)
        dv_sc[...] = jnp.zeros_like(dv_sc)

        def step(qb, masked):
            trim = masked and tq == tk
            start = pl.multiple_of(qb * tq, tq)
            offs = [u * sk if trim else 0 for u in range(ATTN_CHAINS)]
            qss = [pl.ds(start + offs[u], tq - offs[u]) for u in range(ATTN_CHAINS)]
            qts = [q_ref[0, qss[u], :] for u in range(ATTN_CHAINS)]
            dots = [do_ref[qss[u], :] for u in range(ATTN_CHAINS)]
            sts = [_dot_nt(k_ref[0, u * sk:(u + 1) * sk, :], qts[u]) for u in range(ATTN_CHAINS)]
            dpts = [_dot_nt(v_ref[0, u * sk:(u + 1) * sk, :], dots[u]) for u in range(ATTN_CHAINS)]
            parts = []
            for u in range(ATTN_CHAINS):
                rows = slice(u * sk, (u + 1) * sk)
                pt = jnp.exp2(sts[u] - lse_ref[0, :, qss[u]] * LOG2E)
                if masked:
                    pt = jnp.where(pq_ref[:, qss[u]] >= pk_ref[rows, :], pt, 0.0)
                dv_sc[rows, :] += _dot(pt.astype(BF16), dots[u])
                dst = (pt * (dpts[u] - delta_ref[0, :, qss[u]])).astype(BF16)
                dk_sc[rows, :] += _dot(dst, qts[u])
                parts.append(_dot_tn(dst, k_ref[0, rows, :]))
            if trim:
                for u in range(ATTN_CHAINS):
                    dq_ref[0, qss[u], :] += parts[u]
            else:
                dq_ref[0, qss[0], :] += functools.reduce(lambda a, b: a + b, parts)

        q_first = (kb * tk) // tq
        q_clear = (kb * tk + tk - 1) // tq + 1
        lax.fori_loop(q_first, q_clear, lambda qb, c: (step(qb, True), c)[1], 0)
        lax.fori_loop(q_clear, nq, lambda qb, c: (step(qb, False), c)[1], 0)
        dk_ref[0] = dk_sc[...] * (1.0 / LOG2E)
        dv_ref[0] = dv_sc[...]

        @pl.when(kb == nk - 1)
        def _():
            dq_ref[...] = dq_ref[...] * SCALE

    return pl.pallas_call(
        body, name=name, grid=(nh, nk),
        in_specs=[pl.BlockSpec((1, s, QK_PAD), lambda h, kb: (h, 0, 0)),
                  pl.BlockSpec((1, tk, QK_PAD), lambda h, kb: (h, kb, 0)),
                  pl.BlockSpec((1, tk, V_DIM), lambda h, kb: (h, kb, 0)),
                  pl.BlockSpec((s, V_DIM), lambda h, kb: (0, h)),
                  pl.BlockSpec((1, 1, s), lambda h, kb: (h, 0, 0)),
                  pl.BlockSpec((1, 1, s), lambda h, kb: (h, 0, 0)),
                  pl.BlockSpec((1, s), lambda h, kb: (0, 0)),
                  pl.BlockSpec((tk, 1), lambda h, kb: (kb, 0))],
        out_specs=[pl.BlockSpec((1, s, QK_PAD), lambda h, kb: (h, 0, 0)),
                   pl.BlockSpec((1, tk, QK_PAD), lambda h, kb: (h, kb, 0)),
                   pl.BlockSpec((1, tk, V_DIM), lambda h, kb: (h, kb, 0))],
        scratch_shapes=[pltpu.VMEM((tk, QK_PAD), F32), pltpu.VMEM((tk, V_DIM), F32)],
        out_shape=[jax.ShapeDtypeStruct((nh, s, QK_PAD), F32), jax.ShapeDtypeStruct((nh, s, QK_PAD), F32),
                   jax.ShapeDtypeStruct((nh, s, V_DIM), F32)],
        compiler_params=_cparams("arbitrary", "arbitrary"),
    )(q, k, v, do, lse_row, delta_row, pos_row, pos_col)


def _loss_head(y, target, *, name):
    s, d = y.shape
    tm = _tile(s, TM)
    nt = s // tm

    def body(y_ref, t_ref, dy_ref, loss_ref, acc):
        i = pl.program_id(0)

        @pl.when(i == 0)
        def _():
            acc[...] = jnp.zeros_like(acc)

        e = y_ref[...] - t_ref[...]
        dy_ref[...] = e * (1.0 / d)
        acc[...] += jnp.sum((e * e).reshape(tm // 8, 8, d), axis=0)

        @pl.when(i == nt - 1)
        def _():
            loss_ref[...] = jnp.full((1, 128), 0.5 / d, F32) * jnp.sum(acc[...])

    return pl.pallas_call(
        body, name=name, grid=(nt,),
        in_specs=[pl.BlockSpec((tm, d), lambda i: (i, 0))] * 2,
        out_specs=[pl.BlockSpec((tm, d), lambda i: (i, 0)), pl.BlockSpec((1, 128), lambda i: (0, 0))],
        out_shape=[jax.ShapeDtypeStruct((s, d), F32), jax.ShapeDtypeStruct((1, 128), F32)],
        scratch_shapes=[pltpu.VMEM((8, d), F32)],
        compiler_params=_cparams("arbitrary"),
    )(y, target)


def _adamw(w, g, m, v, *, name):
    r, c = w.shape
    tr = _tile(r, 512) if r % 8 == 0 else r

    def body(w_ref, g_ref, m_ref, v_ref, d_ref, nm_ref, nv_ref):
        g_t = g_ref[...]
        nm = ADAM_B1 * m_ref[...] + (1.0 - ADAM_B1) * g_t
        nv = ADAM_B2 * v_ref[...] + (1.0 - ADAM_B2) * (g_t * g_t)
        m_hat = nm / (1.0 - ADAM_B1 ** ADAM_STEP)
        v_hat = nv / (1.0 - ADAM_B2 ** ADAM_STEP)
        d_ref[...] = -ADAM_LR * (m_hat / (jnp.sqrt(v_hat) + ADAM_EPS) + ADAM_WD * w_ref[...])
        nm_ref[...] = nm
        nv_ref[...] = nv

    spec = pl.BlockSpec((tr, c), lambda i: (i, 0))
    return pl.pallas_call(
        body, name=name, grid=(r // tr,), in_specs=[spec] * 4, out_specs=[spec] * 3,
        out_shape=[jax.ShapeDtypeStruct((r, c), F32)] * 3,
        compiler_params=_cparams("parallel"),
    )(w, g, m, v)


def _place():
    return lax.axis_index("x"), lax.axis_index("y"), lax.axis_index("c")


def _other_chips(x, y):
    return [(1 - x, y), (x, 1 - y), (1 - x, 1 - y)]


BIG = {
    "attn_w_down": ((2, 1024, 448), 1), "attn_w_uq": ((2, 256, 1536), 2), "attn_w_ukv": ((2, 128, 2048), 2),
    "attn_w_o": ((2, 1024, 1024), 1), "conv_w_in": ((2, 1024, 3072), 2),
    "conv_w_out": ((2, 1024, 1024), 1), "mlp_w1": ((4, 1024, 4096), 2), "mlp_w2": ((4, 4096, 1024), 1),
}
CONV_W = (2, 3, 1024)


def _shard_shape(name):
    shape, axis = BIG[name]
    return tuple(n // N_CHIPS if i == axis else n for i, n in enumerate(shape))


def _band(ref, name, layers, chip):
    shape, axis = BIG[name]
    width = shape[axis] // N_CHIPS
    if axis == 1:
        return ref.at[layers, pl.ds(chip * width, width), :]
    return ref.at[layers, :, pl.ds(chip * width, width)]


def _half(name, c):
    hl = BIG[name][0][0] // 2
    return pl.ds(c * hl, hl)


def _gather_weights(fulls):
    names = list(fulls)
    n = len(names)

    def body(*refs):
        outs = refs[n:2 * n]
        send_sems, recv_sems = refs[2 * n:]
        x, y, c = _place()
        chips = _other_chips(x, y)
        me = 2 * x + y

        def copy(k, ref, nm, layers, chip, to):
            band = _band(ref, nm, layers, chip)
            return pltpu.make_async_remote_copy(
                src_ref=band, dst_ref=band, send_sem=send_sems.at[k], recv_sem=recv_sems.at[k],
                device_id=to, device_id_type=MESH)

        sent = []
        for i, nm in enumerate(names):
            for j, chip in enumerate(chips):
                cp = copy(6 * i + j, outs[i], nm, _half(nm, c), me, (*chip, c))
                cp.start()
                sent.append(cp)
        for i, nm in enumerate(names):
            for j, (cx, cy) in enumerate(chips):
                copy(6 * i + j, outs[i], nm, _half(nm, c), 2 * cx + cy, (x, y, c)).wait_recv()
                fwd = copy(6 * i + 3 + j, outs[i], nm, _half(nm, c), 2 * cx + cy, (x, y, 1 - c))
                fwd.start()
                sent.append(fwd)
        for i, nm in enumerate(names):
            for j, (cx, cy) in enumerate(chips):
                copy(6 * i + 3 + j, outs[i], nm, _half(nm, 1 - c), 2 * cx + cy, (x, y, c)).wait_recv()
        for cp in sent:
            cp.wait_send()

    arrays = [fulls[nm] for nm in names]
    out = pl.pallas_call(
        body, name="gather_weights", in_specs=[ANY] * n, out_specs=[ANY] * n,
        out_shape=[jax.ShapeDtypeStruct(a.shape, a.dtype) for a in arrays],
        input_output_aliases={i: i for i in range(n)},
        scratch_shapes=[pltpu.SemaphoreType.DMA((6 * n,)), pltpu.SemaphoreType.DMA((6 * n,))],
    )(*arrays)
    return dict(zip(names, out))


def _swap_halves(grads):
    names = list(grads)
    n = len(names)

    def body(*refs):
        ins, outs = refs[:n], refs[n:2 * n]
        send_sems, recv_sems = refs[2 * n:]
        x, y, c = _place()
        copies = []
        for i, nm in enumerate(names):
            cp = pltpu.make_async_remote_copy(
                src_ref=ins[i].at[_half(nm, 1 - c)], dst_ref=outs[i], send_sem=send_sems.at[i],
                recv_sem=recv_sems.at[i], device_id=(x, y, 1 - c), device_id_type=MESH)
            cp.start()
            copies.append(cp)
        for cp in copies:
            cp.wait()

    arrays = [grads[nm] for nm in names]
    out = pl.pallas_call(
        body, name="grad_swap_halves", in_specs=[ANY] * n, out_specs=[ANY] * n,
        out_shape=[jax.ShapeDtypeStruct((a.shape[0] // 2,) + a.shape[1:], a.dtype) for a in arrays],
        scratch_shapes=[pltpu.SemaphoreType.DMA((n,)), pltpu.SemaphoreType.DMA((n,))],
    )(*arrays)
    return dict(zip(names, out))


def _sum_rows(rows, cols):
    t = rows
    while t * cols * 4 > SUM_BLOCK_BYTES and t % 16 == 0:
        t //= 2
    return t


def _chip_sum(g, r1, core, *, name):
    layers, rows, cols = g.shape
    hl = layers // 2
    tr = _sum_rows(rows, cols)

    def body(core_ref, g_ref, r_ref, o_ref):
        o_ref[...] = (g_ref[...] + r_ref[...]).astype(BF16)

    return pl.pallas_call(
        body, name=name,
        grid_spec=pltpu.PrefetchScalarGridSpec(
            num_scalar_prefetch=1, grid=(hl, rows // tr),
            in_specs=[pl.BlockSpec((1, tr, cols), lambda l, i, cr: (cr[0] * hl + l, i, 0)),
                      pl.BlockSpec((1, tr, cols), lambda l, i, cr: (l, i, 0))],
            out_specs=pl.BlockSpec((1, tr, cols), lambda l, i, cr: (l, i, 0))),
        out_shape=jax.ShapeDtypeStruct((hl, rows, cols), BF16),
        compiler_params=_cparams("parallel", "parallel"),
    )(core, g, r1)


def _scatter_partials(partials):
    names = list(partials)
    n = len(names)

    def body(*refs):
        ins, outs = refs[:n], refs[n:2 * n]
        send_sems, recv_sems = refs[2 * n:]
        x, y, c = _place()
        copies = []
        for i, nm in enumerate(names):
            for j, (cx, cy) in enumerate(_other_chips(x, y)):
                cp = pltpu.make_async_remote_copy(
                    src_ref=_band(ins[i], nm, slice(None), 2 * cx + cy), dst_ref=outs[i].at[j],
                    send_sem=send_sems.at[3 * i + j], recv_sem=recv_sems.at[3 * i + j],
                    device_id=(cx, cy, c), device_id_type=MESH)
                cp.start()
                copies.append(cp)
        for cp in copies:
            cp.wait()

    arrays = [partials[nm] for nm in names]
    out = pl.pallas_call(
        body, name="grad_scatter_partials", in_specs=[ANY] * n, out_specs=[ANY] * n,
        out_shape=[jax.ShapeDtypeStruct((3, a.shape[0]) + _shard_shape(nm)[1:], a.dtype)
                   for nm, a in zip(names, arrays)],
        scratch_shapes=[pltpu.SemaphoreType.DMA((3 * n,)), pltpu.SemaphoreType.DMA((3 * n,))],
    )(*arrays)
    return dict(zip(names, out))


def _final_sum(g, r1, r2, place, nm, *, name):
    (layers, _, _), axis = BIG[nm]
    hl = layers // 2
    _, rows, cols = _shard_shape(nm)
    tr = _sum_rows(rows, cols)
    nrb = rows // tr
    if axis == 1:
        blk = lambda l, i, pc: (l, pc[1] * nrb + i, 0)
    else:
        blk = lambda l, i, pc: (l, i, pc[1])

    def body(place_ref, g_ref, r1_ref, r2_ref, o_ref):
        acc = g_ref[...] + r1_ref[...]
        for j in range(3):
            acc = acc + r2_ref[j].astype(F32)
        o_ref[...] = acc

    return pl.pallas_call(
        body, name=name,
        grid_spec=pltpu.PrefetchScalarGridSpec(
            num_scalar_prefetch=1, grid=(hl, nrb),
            in_specs=[pl.BlockSpec((1, tr, cols), lambda l, i, pc: blk(pc[0] * hl + l, i, pc)),
                      pl.BlockSpec((1, tr, cols), lambda l, i, pc: blk(l, i, pc)),
                      pl.BlockSpec((3, 1, tr, cols), lambda l, i, pc: (0, l, i, 0))],
            out_specs=pl.BlockSpec((1, tr, cols), lambda l, i, pc: (pc[0] * hl + l, i, 0))),
        out_shape=jax.ShapeDtypeStruct((layers, rows, cols), F32),
        compiler_params=_cparams("parallel", "parallel"),
    )(place, g, r1, r2)


def _join_halves(shards):
    names = list(shards)
    n = len(names)

    def body(*refs):
        outs = refs[n:2 * n]
        send_sems, recv_sems = refs[2 * n:]
        x, y, c = _place()
        copies = []
        for i, nm in enumerate(names):
            mine = outs[i].at[_half(nm, c)]
            cp = pltpu.make_async_remote_copy(
                src_ref=mine, dst_ref=mine, send_sem=send_sems.at[i], recv_sem=recv_sems.at[i],
                device_id=(x, y, 1 - c), device_id_type=MESH)
            cp.start()
            copies.append(cp)
        for i, nm in enumerate(names):
            theirs = outs[i].at[_half(nm, 1 - c)]
            pltpu.make_async_remote_copy(
                src_ref=theirs, dst_ref=theirs, send_sem=send_sems.at[i], recv_sem=recv_sems.at[i],
                device_id=(x, y, 1 - c), device_id_type=MESH).wait_recv()
        for cp in copies:
            cp.wait_send()

    arrays = [shards[nm] for nm in names]
    out = pl.pallas_call(
        body, name="grad_join_halves", in_specs=[ANY] * n, out_specs=[ANY] * n,
        out_shape=[jax.ShapeDtypeStruct(a.shape, a.dtype) for a in arrays],
        input_output_aliases={i: i for i in range(n)},
        scratch_shapes=[pltpu.SemaphoreType.DMA((n,)), pltpu.SemaphoreType.DMA((n,))],
    )(*arrays)
    return dict(zip(names, out))


def _all_reduce_small(part, *, name):
    rows, cols = part.shape
    vm = pl.BlockSpec(memory_space=pltpu.VMEM)

    def body(p_ref, o_ref, land, send_sems, recv_sems):
        x, y, c = _place()
        me = 4 * x + 2 * y + c
        flips = [(dx, dy, dc) for dx in (0, 1) for dy in (0, 1) for dc in (0, 1)][1:]
        copies = []
        for k, (dx, dy, dc) in enumerate(flips):
            cp = pltpu.make_async_remote_copy(
                src_ref=p_ref, dst_ref=land.at[me], send_sem=send_sems.at[k], recv_sem=recv_sems.at[k],
                device_id=(1 - x if dx else x, 1 - y if dy else y, 1 - c if dc else c), device_id_type=MESH)
            cp.start()
            copies.append(cp)
        land[me] = p_ref[...]
        for cp in copies:
            cp.wait()
        acc = land[0]
        for j in range(1, 8):
            acc = acc + land[j]
        o_ref[...] = acc

    return pl.pallas_call(
        body, name=name, in_specs=[vm], out_specs=vm,
        out_shape=jax.ShapeDtypeStruct((rows, cols), F32),
        scratch_shapes=[pltpu.VMEM((8, rows, cols), F32), pltpu.SemaphoreType.DMA((7,)), pltpu.SemaphoreType.DMA((7,))],
    )(part)


SMALL = {"g_mix": (4, 1024), "g_mlp": (4, 1024), "attn_g_q_a": (2, 256), "attn_g_kv_a": (2, 128),
         "attn_g_qnorm": (2, 192), "attn_g_knorm": (2, 192)}
SMALL_GRADS = {**SMALL, "conv_w": CONV_W}
WEIGHT_ORDER = ["g_mix", "g_mlp", "attn_w_down", "attn_g_q_a", "attn_g_kv_a", "attn_w_uq", "attn_w_ukv",
                "attn_g_qnorm", "attn_g_knorm", "attn_w_o", "conv_w_in", "conv_w", "conv_w_out", "mlp_w1", "mlp_w2"]


def _prod(shape):
    n = 1
    for v in shape:
        n *= v
    return n


def _pack_small(parts, table):
    flat = [parts[n].reshape(-1) for n in table]
    size = sum(_prod(s) for s in table.values())
    rows = -(-size // (8 * 128)) * 8
    flat.append(jnp.zeros((rows * 128 - size,), F32))
    return jnp.concatenate(flat).reshape(rows, 128)


def _unpack_small(buf, table):
    flat = buf.reshape(-1)
    out, off = {}, 0
    for n, shp in table.items():
        out[n] = flat[off:off + _prod(shp)].reshape(shp)
        off += _prod(shp)
    return out


def _rope_tables(positions):
    inv_freq = ROPE_THETA ** (-jnp.arange(0, QK_ROPE, 2, dtype=F32) / QK_ROPE)
    ang = positions.astype(F32)[:, None] * inv_freq
    cos, sin = jnp.cos(ang), jnp.sin(ang)
    z32 = jnp.zeros_like(cos)
    z64 = jnp.zeros((positions.shape[0], 64), F32)
    cc = jnp.concatenate([cos, cos, z64], axis=1)
    sa = jnp.concatenate([-sin, z32, z64], axis=1)
    sb = jnp.concatenate([z32, sin, z64], axis=1)
    return cc, sa, sb


def _pad_heads(w, width):
    k = w.shape[0]
    w = w.reshape(k, N_HEADS, width)
    return jnp.pad(w, ((0, 0), (0, 0), (0, QK_PAD - width))).reshape(k, N_HEADS * QK_PAD)


def _local_step(x, positions, target, wb, gains):
    s = x.shape[0]
    cc, sa, sb = _rope_tables(positions)
    pos_col = positions.reshape(s, 1)
    pos_row = positions.reshape(1, s)

    saved = []
    for i in range(4):
        j = i // 2
        g_mix = gains["g_mix"][i:i + 1]
        g_mlp = gains["g_mlp"][i:i + 1]
        if i % 2 == 0:
            w_down = jnp.pad(wb["attn_w_down"][j], ((0, 0), (0, DOWN_PAD - DOWN_DIM)))
            w_uq = _pad_heads(wb["attn_w_uq"][j], QK_DIM)
            w_ukv = wb["attn_w_ukv"][j]
            g_qa = gains["attn_g_q_a"][j:j + 1]
            g_kva = gains["attn_g_kv_a"][j:j + 1]
            g_q = jnp.pad(gains["attn_g_qnorm"][j:j + 1], ((0, 0), (0, QK_PAD - QK_DIM)))
            g_k = jnp.pad(gains["attn_g_knorm"][j:j + 1], ((0, 0), (0, QK_PAD - QK_DIM)))
            h, a = _norm_mm(x, g_mix, w_down, out_dtype=F32, name=f"mla_down_{j}")
            cq, ckv, q, k, v = _mla_prep(a, g_qa, g_kva, w_uq, w_ukv, g_q, g_k, cc, sa, sb, name=f"mla_prep_{j}")
            o, lse = _flash_fwd(q, k, v, pos_col, pos_row, name=f"flash_fwd_{j}")
            x_mid = _mm_nn(o, wb["attn_w_o"], layer=j, out_dtype=F32, residual=x, name=f"mla_out_{j}")
            mix = dict(h=h, a=a, cq=cq, ckv=ckv, q=q, k=k, v=v, o=o, lse=lse, w_down=w_down, w_uq=w_uq, w_ukv=w_ukv,
                       g_qa=g_qa, g_kva=g_kva, g_q=g_q, g_k=g_k)
        else:
            h, bcu = _norm_mm(x, g_mix, wb["conv_w_in"], layer=j, out_dtype=F32, name=f"conv_in_{j}")
            z = _conv_gate(bcu, gains["conv_w"][j], name=f"conv_gate_{j}")
            x_mid = _mm_nn(z, wb["conv_w_out"], layer=j, out_dtype=F32, residual=x, name=f"conv_out_{j}")
            mix = dict(h=h, bcu=bcu, z=z)
        h2, u, act = _mlp_up(x_mid, g_mlp, wb["mlp_w1"], layer=i, name=f"mlp_up_{i}")
        x_out = _mm_nn(act, wb["mlp_w2"], layer=i, out_dtype=F32, residual=x_mid, name=f"mlp_down_{i}")
        saved.append(dict(x_in=x, x_mid=x_mid, mix=mix, h2=h2, u=u, act=act, g_mix=g_mix, g_mlp=g_mlp))
        x = x_out

    dx, loss = _loss_head(x, target, name="loss_head")

    gw = {n: None for n in BIG}
    g_uq = [None, None]
    gs = {n: [None] * SMALL_GRADS[n][0] for n in SMALL_GRADS}

    def wgrad(nm, layer, a, b, **kw):
        gw[nm] = _mm_tn(a, b, stack=gw[nm], layer=layer, layers=BIG[nm][0][0], name=f"{nm}_grad_{layer}", **kw)

    for i in reversed(range(4)):
        j = i // 2
        sv = saved[i]
        mix = sv["mix"]
        du = _mlp_down_bwd(dx, wb["mlp_w2"], sv["u"], layer=i, name=f"mlp_down_bwd_{i}")
        wgrad("mlp_w2", i, sv["act"], dx)
        wgrad("mlp_w1", i, sv["h2"], du)
        dx, dg = _nt_rms_bwd(du, wb["mlp_w1"], sv["x_mid"], sv["g_mlp"], dx, layer=i, name=f"mlp_up_bwd_{i}")
        gs["g_mlp"][i] = dg[0]
        if i % 2 == 0:
            do = _mm_nt(dx, wb["attn_w_o"], layer=j, out_dtype=BF16, name=f"mla_out_bwd_{j}")
            wgrad("attn_w_o", j, mix["o"], dx)
            delta = _attn_delta(do, mix["o"], name=f"attn_delta_{j}")
            lse_row = mix["lse"].reshape(N_HEADS, 1, s)
            delta_row = delta.reshape(N_HEADS, 1, s)
            dq, dk, dv = _flash_bwd(mix["q"], mix["k"], mix["v"], do, lse_row, delta_row, pos_col, pos_row,
                                    name=f"flash_bwd_{j}")
            dqr, dkvr, da, dgq, dgk, dgqa, dgkva = _mla_prep_bwd(
                mix["a"], mix["g_qa"], mix["g_kva"], mix["w_uq"], mix["w_ukv"], mix["g_q"], mix["g_k"], cc, sa, sb,
                dq, dk, dv, name=f"mla_prep_bwd_{j}")
            g_uq[j] = _mm_tn(mix["cq"], dqr, name=f"attn_w_uq_grad_{j}")[0]
            wgrad("attn_w_ukv", j, mix["ckv"], dkvr)
            wgrad("attn_w_down", j, mix["h"], da, keep=DOWN_DIM)
            dx, dg = _nt_rms_bwd(da, mix["w_down"], sv["x_in"], sv["g_mix"], dx, name=f"mla_down_bwd_{j}")
            gs["attn_g_qnorm"][j] = dgq[0, :QK_DIM]
            gs["attn_g_knorm"][j] = dgk[0, :QK_DIM]
            gs["attn_g_q_a"][j] = dgqa[0]
            gs["attn_g_kv_a"][j] = dgkva[0]
        else:
            dz = _mm_nt(dx, wb["conv_w_out"], layer=j, out_dtype=F32, name=f"conv_out_bwd_{j}")
            wgrad("conv_w_out", j, mix["z"], dx)
            dbcu, dcw = _conv_gate_bwd(mix["bcu"], dz, gains["conv_w"][j], name=f"conv_gate_bwd_{j}")
            gs["conv_w"][j] = dcw
            wgrad("conv_w_in", j, mix["h"], dbcu)
            dx, dg = _nt_rms_bwd(dbcu, wb["conv_w_in"], sv["x_in"], sv["g_mix"], dx, layer=j, name=f"conv_in_bwd_{j}")
        gs["g_mix"][i] = dg[0]

    gw["attn_w_uq"] = jnp.stack(g_uq).reshape(2, Q_RANK, N_HEADS, QK_PAD)[..., :QK_DIM].reshape(BIG["attn_w_uq"][0])
    grads_small = {n: jnp.stack(v) for n, v in gs.items()}
    return loss, dx, gw, grads_small


def kernel(x, positions, g_mix, g_mlp, attn_w_down, attn_g_q_a, attn_g_kv_a, attn_w_uq, attn_w_ukv, attn_g_qnorm, attn_g_knorm, attn_w_o, conv_w_in, conv_w, conv_w_out, mlp_w1, mlp_w2, loss_target, m_g_mix, m_g_mlp, m_attn_w_down, m_attn_g_q_a, m_attn_g_kv_a, m_attn_w_uq, m_attn_w_ukv, m_attn_g_qnorm, m_attn_g_knorm, m_attn_w_o, m_conv_w_in, m_conv_w, m_conv_w_out, m_mlp_w1, m_mlp_w2, v_g_mix, v_g_mlp, v_attn_w_down, v_attn_g_q_a, v_attn_g_kv_a, v_attn_w_uq, v_attn_w_ukv, v_attn_g_qnorm, v_attn_g_knorm, v_attn_w_o, v_conv_w_in, v_conv_w, v_conv_w_out, v_mlp_w1, v_mlp_w2):
    w = dict(g_mix=g_mix, g_mlp=g_mlp, attn_w_down=attn_w_down, attn_g_q_a=attn_g_q_a, attn_g_kv_a=attn_g_kv_a,
             attn_w_uq=attn_w_uq, attn_w_ukv=attn_w_ukv, attn_g_qnorm=attn_g_qnorm, attn_g_knorm=attn_g_knorm,
             attn_w_o=attn_w_o, conv_w_in=conv_w_in, conv_w=conv_w, conv_w_out=conv_w_out, mlp_w1=mlp_w1, mlp_w2=mlp_w2)
    m = dict(g_mix=m_g_mix, g_mlp=m_g_mlp, attn_w_down=m_attn_w_down, attn_g_q_a=m_attn_g_q_a,
             attn_g_kv_a=m_attn_g_kv_a, attn_w_uq=m_attn_w_uq, attn_w_ukv=m_attn_w_ukv, attn_g_qnorm=m_attn_g_qnorm,
             attn_g_knorm=m_attn_g_knorm, attn_w_o=m_attn_w_o, conv_w_in=m_conv_w_in, conv_w=m_conv_w,
             conv_w_out=m_conv_w_out, mlp_w1=m_mlp_w1, mlp_w2=m_mlp_w2)
    v = dict(g_mix=v_g_mix, g_mlp=v_g_mlp, attn_w_down=v_attn_w_down, attn_g_q_a=v_attn_g_q_a,
             attn_g_kv_a=v_attn_g_kv_a, attn_w_uq=v_attn_w_uq, attn_w_ukv=v_attn_w_ukv, attn_g_qnorm=v_attn_g_qnorm,
             attn_g_knorm=v_attn_g_knorm, attn_w_o=v_attn_w_o, conv_w_in=v_conv_w_in, conv_w=v_conv_w,
             conv_w_out=v_conv_w_out, mlp_w1=v_mlp_w1, mlp_w2=v_mlp_w2)
    cx, cy, cc_ = _place()

    chip = 2 * cx + cy

    def own_offset(shape, axis):
        return tuple(chip * (shape[axis] // N_CHIPS) if i == axis else 0 for i in range(3))

    fulls = {}
    for n, (shape, axis) in BIG.items():
        fulls[n] = lax.dynamic_update_slice(jnp.zeros(shape, BF16), w[n].astype(BF16), own_offset(shape, axis))
    wb = _gather_weights(fulls)

    placed = lax.dynamic_update_slice(jnp.zeros(CONV_W, F32), conv_w, own_offset(CONV_W, 2))
    conv_w_full = 0.5 * _all_reduce_small(placed.reshape(-1, 128), name="conv_w_gather").reshape(CONV_W)

    gains = {n: w[n] for n in SMALL}
    gains["conv_w"] = conv_w_full

    loss, grad_x, grads_big, grads_small = _local_step(x[0], positions[0], loss_target[0], wb, gains)

    core = cc_.astype(jnp.int32).reshape(1)
    place = jnp.stack([cc_, chip]).astype(jnp.int32)
    r1 = _swap_halves(grads_big)
    partials = {n: _chip_sum(grads_big[n], r1[n], core, name=f"grad_chip_sum_{n}") for n in BIG}
    r2 = _scatter_partials(partials)
    halves = {n: _final_sum(grads_big[n], r1[n], r2[n], place, n, name=f"grad_final_sum_{n}") for n in BIG}
    grad_shards = _join_halves(halves)

    small = _unpack_small(_all_reduce_small(_pack_small(grads_small, SMALL_GRADS), name="gain_all_reduce"), SMALL_GRADS)
    grad_shards["conv_w"] = lax.dynamic_slice(small["conv_w"], own_offset(CONV_W, 2), conv_w.shape)

    loss_total = lax.psum(loss[0, 0], ("x", "y", "c"))

    grads, deltas, new_m, new_v = {}, {}, {}, {}
    for n in [*BIG, "conv_w"]:
        shp = w[n].shape
        two_d = (shp[0] * shp[1], shp[2])
        g2 = grad_shards[n].reshape(two_d)
        d, nm, nv = _adamw(w[n].reshape(two_d), g2, m[n].reshape(two_d), v[n].reshape(two_d), name=f"adamw_{n}")
        grads[n], deltas[n], new_m[n], new_v[n] = grad_shards[n], d.reshape(shp), nm.reshape(shp), nv.reshape(shp)
    d, nm, nv = _adamw(_pack_small(w, SMALL), _pack_small(small, SMALL), _pack_small(m, SMALL), _pack_small(v, SMALL),
                       name="adamw_gains")
    d, nm, nv = _unpack_small(d, SMALL), _unpack_small(nm, SMALL), _unpack_small(nv, SMALL)
    for n in SMALL:
        grads[n], deltas[n], new_m[n], new_v[n] = small[n], d[n], nm[n], nv[n]

    return (loss_total, grad_x[None],
            *[grads[n] for n in WEIGHT_ORDER], *[deltas[n] for n in WEIGHT_ORDER],
            *[new_m[n] for n in WEIGHT_ORDER], *[new_v[n] for n in WEIGHT_ORDER])
```

```python
import functools

import jax
import jax.numpy as jnp
from jax import lax
from jax.experimental import pallas as pl
from jax.experimental.pallas import tpu as pltpu

F32 = jnp.float32
BF16 = jnp.bfloat16

D_MODEL = 1024
N_HEADS = 8
QK_NOPE = 128
QK_ROPE = 64
QK_DIM = QK_NOPE + QK_ROPE
QK_PAD = 256
V_DIM = 128
Q_RANK = 256
KV_RANK = 128
DOWN_DIM = Q_RANK + KV_RANK + QK_ROPE
DOWN_PAD = 512
D_FF = 4 * D_MODEL
ROPE_THETA = 10000.0
EPS = 1e-6
NEG = -1e30
SCALE = QK_DIM ** -0.5
SCALE_LOG2E = SCALE * 1.4426950408889634
LOG2E = 1.4426950408889634
ATTN_CHAINS = 2

ADAM_LR = 0.001
ADAM_B1 = 0.9
ADAM_B2 = 0.999
ADAM_EPS = 1e-08
ADAM_WD = 0.01
ADAM_STEP = 10

N_CHIPS = 4
MESH = pl.DeviceIdType.MESH
ANY = pl.BlockSpec(memory_space=pl.ANY)

TM = 512
TM_WIDE = 256
FWD_TQ = 1024
FWD_TK = 1024
BWD_TQ = 1024
BWD_TK = 1024
T_PREP = 256
T_RED = 1024
SUM_BLOCK_BYTES = 2 * 1024 * 1024


def _tile(n, pref):
    t = min(n, pref)
    assert n % t == 0, (n, t)
    return t


def _cparams(*sem):
    return pltpu.CompilerParams(dimension_semantics=sem)


def _dot(a, b):
    return jnp.dot(a, b, preferred_element_type=F32)


def _dot_nt(a, b):
    return lax.dot_general(a, b, (((1,), (1,)), ((), ())), preferred_element_type=F32)


def _dot_tn(a, b):
    return lax.dot_general(a, b, (((0,), (0,)), ((), ())), preferred_element_type=F32)


def _rms(x, width):
    r = lax.rsqrt(jnp.sum(x * x, axis=-1, keepdims=True) * (1.0 / width) + EPS)
    return x * r, r


def _rms_bwd(xhat, r, dxhat, width):
    return r * (dxhat - xhat * (jnp.sum(dxhat * xhat, axis=-1, keepdims=True) * (1.0 / width)))


def _rope(t, cc, sa, sb):
    return t * cc + pltpu.roll(t, 96, 1) * sa + pltpu.roll(t, 32, 1) * sb


def _rope_t(g, cc, sa, sb):
    return g * cc + pltpu.roll(g * sa, 32, 1) + pltpu.roll(g * sb, 96, 1)


def _wspec(w, layer):
    if w.ndim == 2:
        return pl.BlockSpec(w.shape, lambda *_: (0, 0))
    return pl.BlockSpec((None,) + w.shape[1:], lambda *_: (layer, 0, 0))


def _mm_nn(a, b, *, out_dtype, name, residual=None, layer=0):
    m, k = a.shape
    n = b.shape[-1]
    tm = _tile(m, TM)

    def body(*refs):
        if residual is None:
            a_ref, b_ref, o_ref = refs
        else:
            a_ref, b_ref, r_ref, o_ref = refs
        acc = _dot(a_ref[...].astype(BF16), b_ref[...])
        if residual is not None:
            acc = acc + r_ref[...]
        o_ref[...] = acc.astype(o_ref.dtype)

    in_specs = [pl.BlockSpec((tm, k), lambda i: (i, 0)), _wspec(b, layer)]
    args = [a, b]
    if residual is not None:
        in_specs.append(pl.BlockSpec((tm, n), lambda i: (i, 0)))
        args.append(residual)
    return pl.pallas_call(
        body, name=name, grid=(m // tm,), in_specs=in_specs,
        out_specs=pl.BlockSpec((tm, n), lambda i: (i, 0)),
        out_shape=jax.ShapeDtypeStruct((m, n), out_dtype),
        compiler_params=_cparams("parallel"),
    )(*args)


def _mm_nt(a, b, *, out_dtype, name, layer=0):
    m, k = a.shape
    n = b.shape[-2]
    tm = _tile(m, TM)

    def body(a_ref, b_ref, o_ref):
        o_ref[...] = _dot_nt(a_ref[...].astype(BF16), b_ref[...]).astype(o_ref.dtype)

    return pl.pallas_call(
        body, name=name, grid=(m // tm,),
        in_specs=[pl.BlockSpec((tm, k), lambda i: (i, 0)), _wspec(b, layer)],
        out_specs=pl.BlockSpec((tm, n), lambda i: (i, 0)),
        out_shape=jax.ShapeDtypeStruct((m, n), out_dtype),
        compiler_params=_cparams("parallel"),
    )(a, b)


def _mm_tn(a, b, *, name, stack=None, layer=0, layers=1, keep=None):
    s, ka = a.shape
    n = b.shape[1]
    ts = _tile(s, T_RED)
    tka = _tile(ka, 1024)
    tn = _tile(n, 1024)
    n_out = n if keep is None else keep
    assert keep is None or tn == n

    def body(a_ref, b_ref, *rest):
        o_ref = rest[-1]

        @pl.when(pl.program_id(2) == 0)
        def _():
            o_ref[...] = jnp.zeros_like(o_ref)

        o_ref[...] += _dot_tn(a_ref[...].astype(BF16), b_ref[...].astype(BF16))[:, :n_out if keep else tn]

    in_specs = [pl.BlockSpec((ts, tka), lambda i, j, t: (t, i)), pl.BlockSpec((ts, tn), lambda i, j, t: (t, j))]
    args = [a, b]
    if stack is not None:
        in_specs.append(ANY)
        args.append(stack)
    return pl.pallas_call(
        body, name=name, grid=(ka // tka, n // tn, s // ts), in_specs=in_specs,
        out_specs=pl.BlockSpec((None, tka, tn if keep is None else keep), lambda i, j, t: (layer, i, j)),
        out_shape=jax.ShapeDtypeStruct((layers, ka, n_out), F32),
        input_output_aliases={} if stack is None else {2: 0},
        compiler_params=_cparams("parallel", "parallel", "arbitrary"),
    )(*args)


def _norm_mm(x, g, w, *, out_dtype, name, layer=0):
    s, d = x.shape
    n = w.shape[-1]
    tm = _tile(s, TM)

    def body(x_ref, g_ref, w_ref, h_ref, o_ref):
        xhat, _ = _rms(x_ref[...], d)
        h = (xhat * g_ref[...]).astype(BF16)
        h_ref[...] = h
        o_ref[...] = _dot(h, w_ref[...]).astype(o_ref.dtype)

    return pl.pallas_call(
        body, name=name, grid=(s // tm,),
        in_specs=[pl.BlockSpec((tm, d), lambda i: (i, 0)), pl.BlockSpec((1, d), lambda i: (0, 0)), _wspec(w, layer)],
        out_specs=[pl.BlockSpec((tm, d), lambda i: (i, 0)), pl.BlockSpec((tm, n), lambda i: (i, 0))],
        out_shape=[jax.ShapeDtypeStruct((s, d), BF16), jax.ShapeDtypeStruct((s, n), out_dtype)],
        compiler_params=_cparams("parallel"),
    )(x, g, w)


def _nt_rms_bwd(dy, w, x, g, dres, *, name, layer=0):
    s, n = dy.shape
    d = x.shape[1]
    tm = _tile(s, TM)

    def body(dy_ref, w_ref, x_ref, g_ref, dres_ref, dx_ref, dg_ref):
        @pl.when(pl.program_id(0) == 0)
        def _():
            dg_ref[...] = jnp.zeros_like(dg_ref)

        dh = _dot_nt(dy_ref[...], w_ref[...])
        xhat, r = _rms(x_ref[...], d)
        dg_ref[...] += jnp.sum(dh * xhat, axis=0, keepdims=True)
        dx_ref[...] = dres_ref[...] + _rms_bwd(xhat, r, dh * g_ref[...], d)

    return pl.pallas_call(
        body, name=name, grid=(s // tm,),
        in_specs=[pl.BlockSpec((tm, n), lambda i: (i, 0)), _wspec(w, layer),
                  pl.BlockSpec((tm, d), lambda i: (i, 0)), pl.BlockSpec((1, d), lambda i: (0, 0)),
                  pl.BlockSpec((tm, d), lambda i: (i, 0))],
        out_specs=[pl.BlockSpec((tm, d), lambda i: (i, 0)), pl.BlockSpec((1, d), lambda i: (0, 0))],
        out_shape=[jax.ShapeDtypeStruct((s, d), F32), jax.ShapeDtypeStruct((1, d), F32)],
        compiler_params=_cparams("arbitrary"),
    )(dy, w, x, g, dres)


def _mlp_up(x, g, w1, *, name, layer=0):
    s, d = x.shape
    n = w1.shape[-1]
    tm = _tile(s, TM_WIDE)

    def body(x_ref, g_ref, w_ref, h_ref, u_ref, act_ref):
        xhat, _ = _rms(x_ref[...], d)
        h = (xhat * g_ref[...]).astype(BF16)
        h_ref[...] = h
        u = _dot(h, w_ref[...])
        u_ref[...] = u.astype(BF16)
        act_ref[...] = jnp.square(jnp.maximum(u, 0.0)).astype(BF16)

    return pl.pallas_call(
        body, name=name, grid=(s // tm,),
        in_specs=[pl.BlockSpec((tm, d), lambda i: (i, 0)), pl.BlockSpec((1, d), lambda i: (0, 0)), _wspec(w1, layer)],
        out_specs=[pl.BlockSpec((tm, d), lambda i: (i, 0)), pl.BlockSpec((tm, n), lambda i: (i, 0)),
                   pl.BlockSpec((tm, n), lambda i: (i, 0))],
        out_shape=[jax.ShapeDtypeStruct((s, d), BF16), jax.ShapeDtypeStruct((s, n), BF16),
                   jax.ShapeDtypeStruct((s, n), BF16)],
        compiler_params=_cparams("parallel"),
    )(x, g, w1)


def _mlp_down_bwd(dy, w2, u, *, name, layer=0):
    s, d = dy.shape
    n = w2.shape[-2]
    tm = _tile(s, TM_WIDE)

    def body(dy_ref, w_ref, u_ref, du_ref):
        dact = _dot_nt(dy_ref[...].astype(BF16), w_ref[...])
        du_ref[...] = (dact * (2.0 * jnp.maximum(u_ref[...].astype(F32), 0.0))).astype(BF16)

    return pl.pallas_call(
        body, name=name, grid=(s // tm,),
        in_specs=[pl.BlockSpec((tm, d), lambda i: (i, 0)), _wspec(w2, layer),
                  pl.BlockSpec((tm, n), lambda i: (i, 0))],
        out_specs=pl.BlockSpec((tm, n), lambda i: (i, 0)),
        out_shape=jax.ShapeDtypeStruct((s, n), BF16),
        compiler_params=_cparams("parallel"),
    )(dy, w2, u)


def _conv_gate(bcu, conv_w, *, name):
    s = bcu.shape[0]
    d = D_MODEL
    tm = _tile(s, TM)
    hb = tm // 8

    def body(bcu_ref, prev_ref, w_ref, z_ref, pbuf):
        i = pl.program_id(0)
        gb = bcu_ref[:, 0:d]
        p = bcu_ref[:, d:2 * d] * bcu_ref[:, 2 * d:3 * d]
        pprev = prev_ref[:, d:2 * d] * prev_ref[:, 2 * d:3 * d]
        pbuf[0:8, :] = jnp.where(i > 0, pprev, 0.0)
        pbuf[8:8 + tm, :] = p
        cv = w_ref[2:3, :] * p + w_ref[1:2, :] * pbuf[7:7 + tm, :] + w_ref[0:1, :] * pbuf[6:6 + tm, :]
        z_ref[...] = (gb * cv).astype(BF16)

    return pl.pallas_call(
        body, name=name, grid=(s // tm,),
        in_specs=[pl.BlockSpec((tm, 3 * d), lambda i: (i, 0)),
                  pl.BlockSpec((8, 3 * d), lambda i: (jnp.maximum(i * hb - 1, 0), 0)),
                  pl.BlockSpec((3, d), lambda i: (0, 0))],
        out_specs=pl.BlockSpec((tm, d), lambda i: (i, 0)),
        out_shape=jax.ShapeDtypeStruct((s, d), BF16),
        scratch_shapes=[pltpu.VMEM((tm + 8, d), F32)],
        compiler_params=_cparams("parallel"),
    )(bcu, bcu, conv_w)


def _conv_gate_bwd(bcu, dz, conv_w, *, name):
    s = bcu.shape[0]
    d = D_MODEL
    tm = _tile(s, TM)
    hb = tm // 8
    nt = s // tm

    def body(bcu_ref, prev_ref, next_ref, dz_ref, dznext_ref, w_ref, dbcu_ref, dw_ref, pbuf, dbuf):
        i = pl.program_id(0)

        @pl.when(i == 0)
        def _():
            dw_ref[...] = jnp.zeros_like(dw_ref)

        gb = bcu_ref[:, 0:d]
        gc = bcu_ref[:, d:2 * d]
        uu = bcu_ref[:, 2 * d:3 * d]
        p = gc * uu
        pprev = prev_ref[:, d:2 * d] * prev_ref[:, 2 * d:3 * d]
        pbuf[0:8, :] = jnp.where(i > 0, pprev, 0.0)
        pbuf[8:8 + tm, :] = p
        p1 = pbuf[7:7 + tm, :]
        p2 = pbuf[6:6 + tm, :]
        cv = w_ref[2:3, :] * p + w_ref[1:2, :] * p1 + w_ref[0:1, :] * p2
        dz_t = dz_ref[...]
        dcv = dz_t * gb
        dcv_next = dznext_ref[...] * next_ref[:, 0:d]
        dbuf[0:tm, :] = dcv
        dbuf[tm:tm + 8, :] = jnp.where(i < nt - 1, dcv_next, 0.0)
        dp = w_ref[2:3, :] * dcv + w_ref[1:2, :] * dbuf[1:1 + tm, :] + w_ref[0:1, :] * dbuf[2:2 + tm, :]
        dw_ref[2:3, :] += jnp.sum(dcv * p, axis=0, keepdims=True)
        dw_ref[1:2, :] += jnp.sum(dcv * p1, axis=0, keepdims=True)
        dw_ref[0:1, :] += jnp.sum(dcv * p2, axis=0, keepdims=True)
        dbcu_ref[:, 0:d] = (dz_t * cv).astype(BF16)
        dbcu_ref[:, d:2 * d] = (dp * uu).astype(BF16)
        dbcu_ref[:, 2 * d:3 * d] = (dp * gc).astype(BF16)

    nxt = lambda i: (jnp.minimum((i + 1) * hb, s // 8 - 1), 0)
    return pl.pallas_call(
        body, name=name, grid=(nt,),
        in_specs=[pl.BlockSpec((tm, 3 * d), lambda i: (i, 0)),
                  pl.BlockSpec((8, 3 * d), lambda i: (jnp.maximum(i * hb - 1, 0), 0)),
                  pl.BlockSpec((8, 3 * d), nxt),
                  pl.BlockSpec((tm, d), lambda i: (i, 0)),
                  pl.BlockSpec((8, d), nxt),
                  pl.BlockSpec((3, d), lambda i: (0, 0))],
        out_specs=[pl.BlockSpec((tm, 3 * d), lambda i: (i, 0)), pl.BlockSpec((3, d), lambda i: (0, 0))],
        out_shape=[jax.ShapeDtypeStruct((s, 3 * d), BF16), jax.ShapeDtypeStruct((3, d), F32)],
        scratch_shapes=[pltpu.VMEM((tm + 8, d), F32), pltpu.VMEM((tm + 8, d), F32)],
        compiler_params=_cparams("arbitrary"),
    )(bcu, bcu, bcu, dz, dz, conv_w)


def _mla_prep(a, g_qa, g_kva, w_uq, w_ukv, g_q, g_k, cc, sa, sb, *, name):
    s = a.shape[0]
    ts = _tile(s, T_PREP)

    def body(a_ref, gqa_ref, gkva_ref, wuq_ref, wukv_ref, gq_ref, gk_ref, cc_ref, sa_ref, sb_ref,
             cq_ref, ckv_ref, q_ref, k_ref, v_ref):
        xq, _ = _rms(a_ref[:, 0:Q_RANK], Q_RANK)
        cq = (xq * gqa_ref[...]).astype(BF16)
        cq_ref[...] = cq
        xkv, _ = _rms(a_ref[:, Q_RANK:Q_RANK + KV_RANK], KV_RANK)
        ckv = (xkv * gkva_ref[...]).astype(BF16)
        ckv_ref[...] = ckv
        kpe = a_ref[:, Q_RANK + KV_RANK:DOWN_PAD]
        kpe_ss = jnp.sum(kpe * kpe, axis=-1, keepdims=True)
        cc_t, sa_t, sb_t = cc_ref[...], sa_ref[...], sb_ref[...]
        gq = gq_ref[...]
        gk = gk_ref[...]
        for h in range(N_HEADS):
            cols = slice(h * QK_PAD, (h + 1) * QK_PAD)
            qhat, _ = _rms(_dot(cq, wuq_ref[:, cols]), QK_DIM)
            qn = qhat * (gq * SCALE_LOG2E)
            q_ref[h, :, 0:QK_NOPE] = qn[:, 0:QK_NOPE].astype(BF16)
            q_ref[h, :, QK_NOPE:QK_PAD] = _rope(qn[:, QK_NOPE:QK_PAD], cc_t, sa_t, sb_t).astype(BF16)
            kvr = _dot(ckv, wukv_ref[:, cols])
            kn = kvr[:, 0:QK_NOPE]
            rk = lax.rsqrt((jnp.sum(kn * kn, axis=-1, keepdims=True) + kpe_ss) * (1.0 / QK_DIM) + EPS)
            k_ref[h, :, 0:QK_NOPE] = (kn * rk * gk[:, 0:QK_NOPE]).astype(BF16)
            k_ref[h, :, QK_NOPE:QK_PAD] = _rope(kpe * rk * gk[:, QK_NOPE:QK_PAD], cc_t, sa_t, sb_t).astype(BF16)
            v_ref[h, :, 0:V_DIM] = kvr[:, QK_NOPE:QK_PAD].astype(BF16)
            v_ref[h, :, V_DIM:2 * V_DIM] = jnp.ones((ts, V_DIM), BF16)

    row = lambda i: (i, 0)
    fixed = lambda i: (0, 0)
    head = lambda i: (0, i, 0)
    return pl.pallas_call(
        body, name=name, grid=(s // ts,),
        in_specs=[pl.BlockSpec((ts, DOWN_PAD), row), pl.BlockSpec((1, Q_RANK), fixed), pl.BlockSpec((1, KV_RANK), fixed),
                  pl.BlockSpec((Q_RANK, N_HEADS * QK_PAD), fixed), pl.BlockSpec((KV_RANK, N_HEADS * QK_PAD), fixed),
                  pl.BlockSpec((1, QK_PAD), fixed), pl.BlockSpec((1, QK_PAD), fixed),
                  pl.BlockSpec((ts, 128), row), pl.BlockSpec((ts, 128), row), pl.BlockSpec((ts, 128), row)],
        out_specs=[pl.BlockSpec((ts, Q_RANK), row), pl.BlockSpec((ts, KV_RANK), row),
                   pl.BlockSpec((N_HEADS, ts, QK_PAD), head), pl.BlockSpec((N_HEADS, ts, QK_PAD), head),
                   pl.BlockSpec((N_HEADS, ts, 2 * V_DIM), head)],
        out_shape=[jax.ShapeDtypeStruct((s, Q_RANK), BF16), jax.ShapeDtypeStruct((s, KV_RANK), BF16),
                   jax.ShapeDtypeStruct((N_HEADS, s, QK_PAD), BF16), jax.ShapeDtypeStruct((N_HEADS, s, QK_PAD), BF16),
                   jax.ShapeDtypeStruct((N_HEADS, s, 2 * V_DIM), BF16)],
        compiler_params=_cparams("parallel"),
    )(a, g_qa, g_kva, w_uq, w_ukv, g_q, g_k, cc, sa, sb)


def _mla_prep_bwd(a, g_qa, g_kva, w_uq, w_ukv, g_q, g_k, cc, sa, sb, dq, dk, dv, *, name):
    s = a.shape[0]
    ts = _tile(s, T_PREP)

    def body(a_ref, gqa_ref, gkva_ref, wuq_ref, wukv_ref, gq_ref, gk_ref, cc_ref, sa_ref, sb_ref,
             dq_ref, dk_ref, dv_ref, dqr_ref, dkvr_ref, da_ref, dgq_ref, dgk_ref, dgqa_ref, dgkva_ref):
        @pl.when(pl.program_id(0) == 0)
        def _():
            dgq_ref[...] = jnp.zeros_like(dgq_ref)
            dgk_ref[...] = jnp.zeros_like(dgk_ref)
            dgqa_ref[...] = jnp.zeros_like(dgqa_ref)
            dgkva_ref[...] = jnp.zeros_like(dgkva_ref)

        xq, r_q = _rms(a_ref[:, 0:Q_RANK], Q_RANK)
        cq = (xq * gqa_ref[...]).astype(BF16)
        xkv, r_kv = _rms(a_ref[:, Q_RANK:Q_RANK + KV_RANK], KV_RANK)
        ckv = (xkv * gkva_ref[...]).astype(BF16)
        kpe = a_ref[:, Q_RANK + KV_RANK:DOWN_PAD]
        kpe_ss = jnp.sum(kpe * kpe, axis=-1, keepdims=True)
        cc_t, sa_t, sb_t = cc_ref[...], sa_ref[...], sb_ref[...]
        gq = gq_ref[...]
        gk = gk_ref[...]
        dcq = jnp.zeros((ts, Q_RANK), F32)
        dckv = jnp.zeros((ts, KV_RANK), F32)
        dkpe = jnp.zeros((ts, 128), F32)
        dgq = jnp.zeros((1, QK_PAD), F32)
        dgk_n = jnp.zeros((1, QK_NOPE), F32)
        dgk_p = jnp.zeros((1, 128), F32)
        for h in range(N_HEADS):
            cols = slice(h * QK_PAD, (h + 1) * QK_PAD)
            qhat, rq = _rms(_dot(cq, wuq_ref[:, cols]), QK_DIM)
            dqn = jnp.concatenate(
                [dq_ref[h, :, 0:QK_NOPE], _rope_t(dq_ref[h, :, QK_NOPE:QK_PAD], cc_t, sa_t, sb_t)], axis=1)
            dgq = dgq + jnp.sum(dqn * qhat, axis=0, keepdims=True)
            dqr = _rms_bwd(qhat, rq, dqn * gq, QK_DIM).astype(BF16)
            dqr_ref[:, cols] = dqr
            dcq = dcq + _dot_nt(dqr, wuq_ref[:, cols])
            kn = _dot(ckv, wukv_ref[:, h * QK_PAD:h * QK_PAD + QK_NOPE])
            rk = lax.rsqrt((jnp.sum(kn * kn, axis=-1, keepdims=True) + kpe_ss) * (1.0 / QK_DIM) + EPS)
            khat_n = kn * rk
            khat_p = kpe * rk
            dkn = dk_ref[h, :, 0:QK_NOPE]
            dkp = _rope_t(dk_ref[h, :, QK_NOPE:QK_PAD], cc_t, sa_t, sb_t)
            dgk_n = dgk_n + jnp.sum(dkn * khat_n, axis=0, keepdims=True)
            dgk_p = dgk_p + jnp.sum(dkp * khat_p, axis=0, keepdims=True)
            dxn = dkn * gk[:, 0:QK_NOPE]
            dxp = dkp * gk[:, QK_NOPE:QK_PAD]
            mean = (jnp.sum(dxn * khat_n, axis=-1, keepdims=True)
                    + jnp.sum(dxp * khat_p, axis=-1, keepdims=True)) * (1.0 / QK_DIM)
            dkpe = dkpe + rk * (dxp - khat_p * mean)
            dkvr = jnp.concatenate([rk * (dxn - khat_n * mean), dv_ref[h, :, :]], axis=1).astype(BF16)
            dkvr_ref[:, cols] = dkvr
            dckv = dckv + _dot_nt(dkvr, wukv_ref[:, cols])
        dgq_ref[...] += dgq
        dgk_ref[:, 0:QK_NOPE] += dgk_n
        dgk_ref[:, QK_NOPE:QK_PAD] += dgk_p
        dgqa_ref[...] += jnp.sum(dcq * xq, axis=0, keepdims=True)
        dgkva_ref[...] += jnp.sum(dckv * xkv, axis=0, keepdims=True)
        da_ref[:, 0:Q_RANK] = _rms_bwd(xq, r_q, dcq * gqa_ref[...], Q_RANK).astype(BF16)
        da_ref[:, Q_RANK:Q_RANK + KV_RANK] = _rms_bwd(xkv, r_kv, dckv * gkva_ref[...], KV_RANK).astype(BF16)
        da_ref[:, Q_RANK + KV_RANK:DOWN_PAD] = dkpe.astype(BF16)

    row = lambda i: (i, 0)
    fixed = lambda i: (0, 0)
    head = lambda i: (0, i, 0)
    wide = N_HEADS * QK_PAD
    return pl.pallas_call(
        body, name=name, grid=(s // ts,),
        in_specs=[pl.BlockSpec((ts, DOWN_PAD), row), pl.BlockSpec((1, Q_RANK), fixed), pl.BlockSpec((1, KV_RANK), fixed),
                  pl.BlockSpec((Q_RANK, wide), fixed), pl.BlockSpec((KV_RANK, wide), fixed),
                  pl.BlockSpec((1, QK_PAD), fixed), pl.BlockSpec((1, QK_PAD), fixed),
                  pl.BlockSpec((ts, 128), row), pl.BlockSpec((ts, 128), row), pl.BlockSpec((ts, 128), row),
                  pl.BlockSpec((N_HEADS, ts, QK_PAD), head), pl.BlockSpec((N_HEADS, ts, QK_PAD), head),
                  pl.BlockSpec((N_HEADS, ts, V_DIM), head)],
        out_specs=[pl.BlockSpec((ts, wide), row), pl.BlockSpec((ts, wide), row), pl.BlockSpec((ts, DOWN_PAD), row),
                   pl.BlockSpec((1, QK_PAD), fixed), pl.BlockSpec((1, QK_PAD), fixed),
                   pl.BlockSpec((1, Q_RANK), fixed), pl.BlockSpec((1, KV_RANK), fixed)],
        out_shape=[jax.ShapeDtypeStruct((s, wide), BF16), jax.ShapeDtypeStruct((s, wide), BF16),
                   jax.ShapeDtypeStruct((s, DOWN_PAD), BF16),
                   jax.ShapeDtypeStruct((1, QK_PAD), F32), jax.ShapeDtypeStruct((1, QK_PAD), F32),
                   jax.ShapeDtypeStruct((1, Q_RANK), F32), jax.ShapeDtypeStruct((1, KV_RANK), F32)],
        compiler_params=_cparams("arbitrary"),
    )(a, g_qa, g_kva, w_uq, w_ukv, g_q, g_k, cc, sa, sb, dq, dk, dv)


def _flash_fwd(q, k, v, pos_col, pos_row, *, name, gather=None):
    nh, s, _ = q.shape
    tq = _tile(s, FWD_TQ)
    tk = _tile(s, FWD_TK)
    sq = tq // ATTN_CHAINS
    nq = s // tq
    names = list(gather or {})
    ng = len(names)

    def body(q_ref, k_ref, v_ref, pq_ref, pk_ref, *rest):
        o_ref, lse_ref = rest[ng:ng + 2]
        m_sc, acc_sc = rest[2 * ng + 2:2 * ng + 4]
        qb = pl.program_id(1)
        if ng:
            sends, recvs = _gather_ici_copies(rest[ng + 2:2 * ng + 2], names, *rest[2 * ng + 4:], base=0, stride=3)

            @pl.when((pl.program_id(0) == 0) & (qb == 0))
            def _():
                for cp in sends:
                    cp.start()

        m_sc[...] = jnp.full_like(m_sc, NEG)
        acc_sc[...] = jnp.zeros_like(acc_sc)

        def step(kb, masked):
            keys = pl.ds(pl.multiple_of(kb * tk, tk), tk)
            kt = k_ref[0, keys, :]
            vt = v_ref[0, keys, :]
            scores = [_dot_nt(q_ref[0, u * sq:(u + 1) * sq, :], kt) for u in range(ATTN_CHAINS)]
            for u in range(ATTN_CHAINS):
                rows = slice(u * sq, (u + 1) * sq)
                sc = scores[u]
                if masked:
                    sc = jnp.where(pq_ref[rows, :] >= pk_ref[:, keys], sc, NEG)
                m_prev = m_sc[rows, :]
                m_new = jnp.maximum(m_prev, jnp.max(sc, axis=-1, keepdims=True))
                alpha = jnp.exp2(m_prev - m_new)
                p = jnp.exp2(sc - jnp.tile(m_new, (1, tk // 128)))
                acc_sc[rows, :] = jnp.tile(alpha, (1, 2)) * acc_sc[rows, :] + _dot(p.astype(BF16), vt)
                m_sc[rows, :] = m_new

        n_before = (qb * tq) // tk
        n_seen = (qb * tq + tq - 1) // tk + 1
        lax.fori_loop(0, n_before, lambda kb, c: (step(kb, False), c)[1], 0)
        lax.fori_loop(n_before, n_seen, lambda kb, c: (step(kb, True), c)[1], 0)
        l = acc_sc[:, V_DIM:2 * V_DIM]
        o_ref[...] = (acc_sc[:, 0:V_DIM] / l).astype(BF16)
        lse_ref[0] = m_sc[:, 0:1] * (1.0 / LOG2E) + jnp.log(l[:, 0:1])

        if ng:
            @pl.when((pl.program_id(0) == nh - 1) & (qb == nq - 1))
            def _():
                for cp in recvs:
                    cp.wait_recv()
                for cp in sends:
                    cp.wait_send()

    arrays = [gather[nm] for nm in names]
    out = pl.pallas_call(
        body, name=name, grid=(nh, nq),
        in_specs=[pl.BlockSpec((1, tq, QK_PAD), lambda h, qb: (h, qb, 0)),
                  pl.BlockSpec((1, s, QK_PAD), lambda h, qb: (h, 0, 0)),
                  pl.BlockSpec((1, s, 2 * V_DIM), lambda h, qb: (h, 0, 0)),
                  pl.BlockSpec((tq, 1), lambda h, qb: (qb, 0)),
                  pl.BlockSpec((1, s), lambda h, qb: (0, 0))] + [ANY] * ng,
        out_specs=[pl.BlockSpec((tq, V_DIM), lambda h, qb: (qb, h)),
                   pl.BlockSpec((1, tq, 1), lambda h, qb: (h, qb, 0))] + [ANY] * ng,
        scratch_shapes=[pltpu.VMEM((tq, 128), F32), pltpu.VMEM((tq, 2 * V_DIM), F32)]
        + ([pltpu.SemaphoreType.DMA((3 * ng,)), pltpu.SemaphoreType.DMA((3 * ng,))] if ng else []),
        out_shape=[jax.ShapeDtypeStruct((s, nh * V_DIM), BF16), jax.ShapeDtypeStruct((nh, s, 1), F32)]
        + [jax.ShapeDtypeStruct(a.shape, a.dtype) for a in arrays],
        input_output_aliases={5 + i: 2 + i for i in range(ng)},
        compiler_params=_cparams("arbitrary", "arbitrary") if ng else _cparams("parallel", "parallel"),
    )(q, k, v, pos_col, pos_row, *arrays)
    return out[0], out[1], dict(zip(names, out[2:]))


def _attn_delta(do, o, *, name):
    s = do.shape[0]
    tm = _tile(s, TM)

    def body(do_ref, o_ref, d_ref):
        for h in range(N_HEADS):
            cols = slice(h * V_DIM, (h + 1) * V_DIM)
            d_ref[h] = jnp.sum(do_ref[:, cols].astype(F32) * o_ref[:, cols].astype(F32), axis=-1, keepdims=True)

    return pl.pallas_call(
        body, name=name, grid=(s // tm,),
        in_specs=[pl.BlockSpec((tm, N_HEADS * V_DIM), lambda i: (i, 0))] * 2,
        out_specs=pl.BlockSpec((N_HEADS, tm, 1), lambda i: (0, i, 0)),
        out_shape=jax.ShapeDtypeStruct((N_HEADS, s, 1), F32),
        compiler_params=_cparams("parallel"),
    )(do, o)


def _flash_bwd(q, k, v, do, lse_row, delta_row, pos_col, pos_row, *, name, scatter=None):
    nh, s, _ = q.shape
    tq = _tile(s, BWD_TQ)
    tk = _tile(s, BWD_TK)
    nq, nk = s // tq, s // tk
    sk = tk // ATTN_CHAINS
    names = list(scatter or {})
    ng = len(names)

    def body(q_ref, k_ref, v_ref, do_ref, lse_ref, delta_ref, pq_ref, pk_ref, *rest):
        dq_ref, dk_ref, dv_ref = rest[ng:ng + 3]
        dk_sc, dv_sc = rest[2 * ng + 3:2 * ng + 5]
        kb = pl.program_id(1)
        if ng:
            copies = _scatter_copies(rest[:ng], rest[ng + 3:2 * ng + 3], names, *rest[2 * ng + 5:])

            @pl.when((pl.program_id(0) == 0) & (kb == 0))
            def _():
                for cp in copies:
                    cp.start()

        @pl.when(kb == 0)
        def _():
            dq_ref[...] = jnp.zeros_like(dq_ref)

        dk_sc[...] = jnp.zeros_like(dk_sc)
        dv_sc[...] = jnp.zeros_like(dv_sc)

        def step(qb, masked):
            trim = masked and tq == tk
            start = pl.multiple_of(qb * tq, tq)
            offs = [u * sk if trim else 0 for u in range(ATTN_CHAINS)]
            qss = [pl.ds(start + offs[u], tq - offs[u]) for u in range(ATTN_CHAINS)]
            qts = [q_ref[0, qss[u], :] for u in range(ATTN_CHAINS)]
            dots = [do_ref[qss[u], :] for u in range(ATTN_CHAINS)]
            sts = [_dot_nt(k_ref[0, u * sk:(u + 1) * sk, :], qts[u]) for u in range(ATTN_CHAINS)]
            dpts = [_dot_nt(v_ref[0, u * sk:(u + 1) * sk, :], dots[u]) for u in range(ATTN_CHAINS)]
            parts = []
            for u in range(ATTN_CHAINS):
                rows = slice(u * sk, (u + 1) * sk)
                pt = jnp.exp2(sts[u] - lse_ref[0, :, qss[u]] * LOG2E)
                if masked:
                    pt = jnp.where(pq_ref[:, qss[u]] >= pk_ref[rows, :], pt, 0.0)
                dv_sc[rows, :] += _dot(pt.astype(BF16), dots[u])
                dst = (pt * (dpts[u] - delta_ref[0, :, qss[u]])).astype(BF16)
                dk_sc[rows, :] += _dot(dst, qts[u])
                parts.append(_dot_tn(dst, k_ref[0, rows, :]))
            if trim:
                for u in range(ATTN_CHAINS):
                    dq_ref[0, qss[u], :] += parts[u]
            else:
                dq_ref[0, qss[0], :] += functools.reduce(lambda a, b: a + b, parts)

        q_first = (kb * tk) // tq
        q_clear = (kb * tk + tk - 1) // tq + 1
        lax.fori_loop(q_first, q_clear, lambda qb, c: (step(qb, True), c)[1], 0)
        lax.fori_loop(q_clear, nq, lambda qb, c: (step(qb, False), c)[1], 0)
        dk_ref[0] = dk_sc[...] * (1.0 / LOG2E)
        dv_ref[0] = dv_sc[...]

        @pl.when(kb == nk - 1)
        def _():
            dq_ref[...] = dq_ref[...] * SCALE

        if ng:
            @pl.when((pl.program_id(0) == nh - 1) & (kb == nk - 1))
            def _():
                for cp in copies:
                    cp.wait()

    arrays = [scatter[nm] for nm in names]
    out = pl.pallas_call(
        body, name=name, grid=(nh, nk),
        in_specs=[pl.BlockSpec((1, s, QK_PAD), lambda h, kb: (h, 0, 0)),
                  pl.BlockSpec((1, tk, QK_PAD), lambda h, kb: (h, kb, 0)),
                  pl.BlockSpec((1, tk, V_DIM), lambda h, kb: (h, kb, 0)),
                  pl.BlockSpec((s, V_DIM), lambda h, kb: (0, h)),
                  pl.BlockSpec((1, 1, s), lambda h, kb: (h, 0, 0)),
                  pl.BlockSpec((1, 1, s), lambda h, kb: (h, 0, 0)),
                  pl.BlockSpec((1, s), lambda h, kb: (0, 0)),
                  pl.BlockSpec((tk, 1), lambda h, kb: (kb, 0))] + [ANY] * ng,
        out_specs=[pl.BlockSpec((1, s, QK_PAD), lambda h, kb: (h, 0, 0)),
                   pl.BlockSpec((1, tk, QK_PAD), lambda h, kb: (h, kb, 0)),
                   pl.BlockSpec((1, tk, V_DIM), lambda h, kb: (h, kb, 0))] + [ANY] * ng,
        scratch_shapes=[pltpu.VMEM((tk, QK_PAD), F32), pltpu.VMEM((tk, V_DIM), F32)]
        + ([pltpu.SemaphoreType.DMA((3 * ng,)), pltpu.SemaphoreType.DMA((3 * ng,))] if ng else []),
        out_shape=[jax.ShapeDtypeStruct((nh, s, QK_PAD), F32), jax.ShapeDtypeStruct((nh, s, QK_PAD), F32),
                   jax.ShapeDtypeStruct((nh, s, V_DIM), F32)] + _scatter_out_shapes(names, arrays),
        compiler_params=_cparams("arbitrary", "arbitrary"),
    )(q, k, v, do, lse_row, delta_row, pos_row, pos_col, *arrays)
    return out[0], out[1], out[2], dict(zip(names, out[3:]))


def _loss_head(y, target, *, name):
    s, d = y.shape
    tm = _tile(s, TM)
    nt = s // tm

    def body(y_ref, t_ref, dy_ref, loss_ref, acc):
        i = pl.program_id(0)

        @pl.when(i == 0)
        def _():
            acc[...] = jnp.zeros_like(acc)

        e = y_ref[...] - t_ref[...]
        dy_ref[...] = e * (1.0 / d)
        acc[...] += jnp.sum((e * e).reshape(tm // 8, 8, d), axis=0)

        @pl.when(i == nt - 1)
        def _():
            loss_ref[...] = jnp.full((1, 128), 0.5 / d, F32) * jnp.sum(acc[...])

    return pl.pallas_call(
        body, name=name, grid=(nt,),
        in_specs=[pl.BlockSpec((tm, d), lambda i: (i, 0))] * 2,
        out_specs=[pl.BlockSpec((tm, d), lambda i: (i, 0)), pl.BlockSpec((1, 128), lambda i: (0, 0))],
        out_shape=[jax.ShapeDtypeStruct((s, d), F32), jax.ShapeDtypeStruct((1, 128), F32)],
        scratch_shapes=[pltpu.VMEM((8, d), F32)],
        compiler_params=_cparams("arbitrary"),
    )(y, target)


def _adamw(w, g, m, v, *, name):
    r, c = w.shape
    tr = _tile(r, 512) if r % 8 == 0 else r

    def body(w_ref, g_ref, m_ref, v_ref, d_ref, nm_ref, nv_ref):
        g_t = g_ref[...]
        nm = ADAM_B1 * m_ref[...] + (1.0 - ADAM_B1) * g_t
        nv = ADAM_B2 * v_ref[...] + (1.0 - ADAM_B2) * (g_t * g_t)
        m_hat = nm / (1.0 - ADAM_B1 ** ADAM_STEP)
        v_hat = nv / (1.0 - ADAM_B2 ** ADAM_STEP)
        d_ref[...] = -ADAM_LR * (m_hat / (jnp.sqrt(v_hat) + ADAM_EPS) + ADAM_WD * w_ref[...])
        nm_ref[...] = nm
        nv_ref[...] = nv

    spec = pl.BlockSpec((tr, c), lambda i: (i, 0))
    return pl.pallas_call(
        body, name=name, grid=(r // tr,), in_specs=[spec] * 4, out_specs=[spec] * 3,
        out_shape=[jax.ShapeDtypeStruct((r, c), F32)] * 3,
        compiler_params=_cparams("parallel"),
    )(w, g, m, v)


def _place():
    return lax.axis_index("x"), lax.axis_index("y"), lax.axis_index("c")


def _other_chips(x, y):
    return [(1 - x, y), (x, 1 - y), (1 - x, 1 - y)]


BIG = {
    "attn_w_down": ((2, 1024, 448), 1), "attn_w_uq": ((2, 256, 1536), 2), "attn_w_ukv": ((2, 128, 2048), 2),
    "attn_w_o": ((2, 1024, 1024), 1), "conv_w_in": ((2, 1024, 3072), 2),
    "conv_w_out": ((2, 1024, 1024), 1), "mlp_w1": ((4, 1024, 4096), 2), "mlp_w2": ((4, 4096, 1024), 1),
}
CONV_W = (2, 3, 1024)


def _shard_shape(name):
    shape, axis = BIG[name]
    return tuple(n // N_CHIPS if i == axis else n for i, n in enumerate(shape))


def _band(ref, name, layers, chip):
    shape, axis = BIG[name]
    width = shape[axis] // N_CHIPS
    if axis == 1:
        return ref.at[layers, pl.ds(chip * width, width), :]
    return ref.at[layers, :, pl.ds(chip * width, width)]


def _half(name, c):
    hl = BIG[name][0][0] // 2
    return pl.ds(c * hl, hl)


def _gather_copies(outs, names, send_sems, recv_sems, *, base, stride, to_sibling):
    x, y, c = _place()
    me = 2 * x + y

    def copy(k, ref, nm, layers, chip, to):
        band = _band(ref, nm, layers, chip)
        return pltpu.make_async_remote_copy(
            src_ref=band, dst_ref=band, send_sem=send_sems.at[k], recv_sem=recv_sems.at[k],
            device_id=to, device_id_type=MESH)

    sends, recvs = [], []
    for i, nm in enumerate(names):
        for j, (cx, cy) in enumerate(_other_chips(x, y)):
            k = base + stride * i + j
            if to_sibling:
                sends.append(copy(k, outs[i], nm, _half(nm, c), 2 * cx + cy, (x, y, 1 - c)))
                recvs.append(copy(k, outs[i], nm, _half(nm, 1 - c), 2 * cx + cy, (x, y, c)))
            else:
                sends.append(copy(k, outs[i], nm, _half(nm, c), me, (cx, cy, c)))
                recvs.append(copy(k, outs[i], nm, _half(nm, c), 2 * cx + cy, (x, y, c)))
    return sends, recvs


def _gather_ici_copies(outs, names, send_sems, recv_sems, *, base, stride):
    return _gather_copies(outs, names, send_sems, recv_sems, base=base, stride=stride, to_sibling=False)


def _gather_weights(fulls, *, name, ici=True):
    names = list(fulls)
    n = len(names)

    def body(*refs):
        outs = refs[n:2 * n]
        sems = refs[2 * n:]
        sent = []
        if ici:
            sends, recvs = _gather_copies(outs, names, *sems, base=0, stride=6, to_sibling=False)
            for cp in sends:
                cp.start()
            for cp in recvs:
                cp.wait_recv()
            sent += sends
        sends, recvs = _gather_copies(outs, names, *sems, base=3, stride=6, to_sibling=True)
        for cp in sends:
            cp.start()
        for cp in recvs:
            cp.wait_recv()
        for cp in sent + sends:
            cp.wait_send()

    arrays = [fulls[nm] for nm in names]
    out = pl.pallas_call(
        body, name=name, in_specs=[ANY] * n, out_specs=[ANY] * n,
        out_shape=[jax.ShapeDtypeStruct(a.shape, a.dtype) for a in arrays],
        input_output_aliases={i: i for i in range(n)},
        scratch_shapes=[pltpu.SemaphoreType.DMA((6 * n,)), pltpu.SemaphoreType.DMA((6 * n,))],
    )(*arrays)
    return dict(zip(names, out))


def _swap_halves(grads, *, name):
    names = list(grads)
    n = len(names)

    def body(*refs):
        ins, outs = refs[:n], refs[n:2 * n]
        send_sems, recv_sems = refs[2 * n:]
        x, y, c = _place()
        copies = []
        for i, nm in enumerate(names):
            cp = pltpu.make_async_remote_copy(
                src_ref=ins[i].at[_half(nm, 1 - c)], dst_ref=outs[i], send_sem=send_sems.at[i],
                recv_sem=recv_sems.at[i], device_id=(x, y, 1 - c), device_id_type=MESH)
            cp.start()
            copies.append(cp)
        for cp in copies:
            cp.wait()

    arrays = [grads[nm] for nm in names]
    out = pl.pallas_call(
        body, name=name, in_specs=[ANY] * n, out_specs=[ANY] * n,
        out_shape=[jax.ShapeDtypeStruct((a.shape[0] // 2,) + a.shape[1:], a.dtype) for a in arrays],
        scratch_shapes=[pltpu.SemaphoreType.DMA((n,)), pltpu.SemaphoreType.DMA((n,))],
    )(*arrays)
    return dict(zip(names, out))


def _sum_rows(rows, cols):
    t = rows
    while t * cols * 4 > SUM_BLOCK_BYTES and t % 16 == 0:
        t //= 2
    return t


def _chip_sum(g, r1, core, *, name):
    layers, rows, cols = g.shape
    hl = layers // 2
    tr = _sum_rows(rows, cols)

    def body(core_ref, g_ref, r_ref, o_ref):
        o_ref[...] = (g_ref[...] + r_ref[...]).astype(BF16)

    return pl.pallas_call(
        body, name=name,
        grid_spec=pltpu.PrefetchScalarGridSpec(
            num_scalar_prefetch=1, grid=(hl, rows // tr),
            in_specs=[pl.BlockSpec((1, tr, cols), lambda l, i, cr: (cr[0] * hl + l, i, 0)),
                      pl.BlockSpec((1, tr, cols), lambda l, i, cr: (l, i, 0))],
            out_specs=pl.BlockSpec((1, tr, cols), lambda l, i, cr: (l, i, 0))),
        out_shape=jax.ShapeDtypeStruct((hl, rows, cols), BF16),
        compiler_params=_cparams("parallel", "parallel"),
    )(core, g, r1)


def _chip_partials(grads, names, *, tag):
    core = lax.axis_index("c").astype(jnp.int32).reshape(1)
    r1 = _swap_halves({n: grads[n] for n in names}, name=f"grad_swap_halves_{tag}")
    return r1, {n: _chip_sum(grads[n], r1[n], core, name=f"grad_chip_sum_{n}") for n in names}


def _scatter_partials(partials):
    names = list(partials)
    n = len(names)

    def body(*refs):
        copies = _scatter_copies(refs[:n], refs[n:2 * n], names, *refs[2 * n:])
        for cp in copies:
            cp.start()
        for cp in copies:
            cp.wait()

    arrays = [partials[nm] for nm in names]
    out = pl.pallas_call(
        body, name="grad_scatter_partials", in_specs=[ANY] * n, out_specs=[ANY] * n,
        out_shape=_scatter_out_shapes(names, arrays),
        scratch_shapes=[pltpu.SemaphoreType.DMA((3 * n,)), pltpu.SemaphoreType.DMA((3 * n,))],
    )(*arrays)
    return dict(zip(names, out))


def _scatter_copies(ins, outs, names, send_sems, recv_sems):
    x, y, c = _place()
    copies = []
    for i, nm in enumerate(names):
        for j, (cx, cy) in enumerate(_other_chips(x, y)):
            copies.append(pltpu.make_async_remote_copy(
                src_ref=_band(ins[i], nm, slice(None), 2 * cx + cy), dst_ref=outs[i].at[j],
                send_sem=send_sems.at[3 * i + j], recv_sem=recv_sems.at[3 * i + j],
                device_id=(cx, cy, c), device_id_type=MESH))
    return copies


def _scatter_out_shapes(names, arrays):
    return [jax.ShapeDtypeStruct((3, a.shape[0]) + _shard_shape(nm)[1:], a.dtype) for nm, a in zip(names, arrays)]


def _final_sum(g, r1, r2, place, nm, *, name):
    (layers, _, _), axis = BIG[nm]
    hl = layers // 2
    _, rows, cols = _shard_shape(nm)
    tr = _sum_rows(rows, cols)
    nrb = rows // tr
    if axis == 1:
        blk = lambda l, i, pc: (l, pc[1] * nrb + i, 0)
    else:
        blk = lambda l, i, pc: (l, i, pc[1])

    def body(place_ref, g_ref, r1_ref, r2_ref, o_ref):
        acc = g_ref[...] + r1_ref[...]
        for j in range(3):
            acc = acc + r2_ref[j].astype(F32)
        o_ref[...] = acc

    return pl.pallas_call(
        body, name=name,
        grid_spec=pltpu.PrefetchScalarGridSpec(
            num_scalar_prefetch=1, grid=(hl, nrb),
            in_specs=[pl.BlockSpec((1, tr, cols), lambda l, i, pc: blk(pc[0] * hl + l, i, pc)),
                      pl.BlockSpec((1, tr, cols), lambda l, i, pc: blk(l, i, pc)),
                      pl.BlockSpec((3, 1, tr, cols), lambda l, i, pc: (0, l, i, 0))],
            out_specs=pl.BlockSpec((1, tr, cols), lambda l, i, pc: (pc[0] * hl + l, i, 0))),
        out_shape=jax.ShapeDtypeStruct((layers, rows, cols), F32),
        compiler_params=_cparams("parallel", "parallel"),
    )(place, g, r1, r2)


def _join_halves(shards):
    names = list(shards)
    n = len(names)

    def body(*refs):
        outs = refs[n:2 * n]
        send_sems, recv_sems = refs[2 * n:]
        x, y, c = _place()
        copies = []
        for i, nm in enumerate(names):
            mine = outs[i].at[_half(nm, c)]
            cp = pltpu.make_async_remote_copy(
                src_ref=mine, dst_ref=mine, send_sem=send_sems.at[i], recv_sem=recv_sems.at[i],
                device_id=(x, y, 1 - c), device_id_type=MESH)
            cp.start()
            copies.append(cp)
        for i, nm in enumerate(names):
            theirs = outs[i].at[_half(nm, 1 - c)]
            pltpu.make_async_remote_copy(
                src_ref=theirs, dst_ref=theirs, send_sem=send_sems.at[i], recv_sem=recv_sems.at[i],
                device_id=(x, y, 1 - c), device_id_type=MESH).wait_recv()
        for cp in copies:
            cp.wait_send()

    arrays = [shards[nm] for nm in names]
    out = pl.pallas_call(
        body, name="grad_join_halves", in_specs=[ANY] * n, out_specs=[ANY] * n,
        out_shape=[jax.ShapeDtypeStruct(a.shape, a.dtype) for a in arrays],
        input_output_aliases={i: i for i in range(n)},
        scratch_shapes=[pltpu.SemaphoreType.DMA((n,)), pltpu.SemaphoreType.DMA((n,))],
    )(*arrays)
    return dict(zip(names, out))


def _all_reduce_small(part, *, name):
    rows, cols = part.shape
    vm = pl.BlockSpec(memory_space=pltpu.VMEM)

    def body(p_ref, o_ref, land, send_sems, recv_sems):
        x, y, c = _place()
        me = 4 * x + 2 * y + c
        flips = [(dx, dy, dc) for dx in (0, 1) for dy in (0, 1) for dc in (0, 1)][1:]
        copies = []
        for k, (dx, dy, dc) in enumerate(flips):
            cp = pltpu.make_async_remote_copy(
                src_ref=p_ref, dst_ref=land.at[me], send_sem=send_sems.at[k], recv_sem=recv_sems.at[k],
                device_id=(1 - x if dx else x, 1 - y if dy else y, 1 - c if dc else c), device_id_type=MESH)
            cp.start()
            copies.append(cp)
        land[me] = p_ref[...]
        for cp in copies:
            cp.wait()
        acc = land[0]
        for j in range(1, 8):
            acc = acc + land[j]
        o_ref[...] = acc

    return pl.pallas_call(
        body, name=name, in_specs=[vm], out_specs=vm,
        out_shape=jax.ShapeDtypeStruct((rows, cols), F32),
        scratch_shapes=[pltpu.VMEM((8, rows, cols), F32), pltpu.SemaphoreType.DMA((7,)), pltpu.SemaphoreType.DMA((7,))],
    )(part)


SMALL = {"g_mix": (4, 1024), "g_mlp": (4, 1024), "attn_g_q_a": (2, 256), "attn_g_kv_a": (2, 128),
         "attn_g_qnorm": (2, 192), "attn_g_knorm": (2, 192)}
SMALL_GRADS = {**SMALL, "conv_w": CONV_W}
WEIGHT_ORDER = ["g_mix", "g_mlp", "attn_w_down", "attn_g_q_a", "attn_g_kv_a", "attn_w_uq", "attn_w_ukv",
                "attn_g_qnorm", "attn_g_knorm", "attn_w_o", "conv_w_in", "conv_w", "conv_w_out", "mlp_w1", "mlp_w2"]


def _prod(shape):
    n = 1
    for v in shape:
        n *= v
    return n


def _pack_small(parts, table):
    flat = [parts[n].reshape(-1) for n in table]
    size = sum(_prod(s) for s in table.values())
    rows = -(-size // (8 * 128)) * 8
    flat.append(jnp.zeros((rows * 128 - size,), F32))
    return jnp.concatenate(flat).reshape(rows, 128)


def _unpack_small(buf, table):
    flat = buf.reshape(-1)
    out, off = {}, 0
    for n, shp in table.items():
        out[n] = flat[off:off + _prod(shp)].reshape(shp)
        off += _prod(shp)
    return out


def _rope_tables(positions):
    inv_freq = ROPE_THETA ** (-jnp.arange(0, QK_ROPE, 2, dtype=F32) / QK_ROPE)
    ang = positions.astype(F32)[:, None] * inv_freq
    cos, sin = jnp.cos(ang), jnp.sin(ang)
    z32 = jnp.zeros_like(cos)
    z64 = jnp.zeros((positions.shape[0], 64), F32)
    cc = jnp.concatenate([cos, cos, z64], axis=1)
    sa = jnp.concatenate([-sin, z32, z64], axis=1)
    sb = jnp.concatenate([z32, sin, z64], axis=1)
    return cc, sa, sb


def _pad_heads(w, width):
    k = w.shape[0]
    w = w.reshape(k, N_HEADS, width)
    return jnp.pad(w, ((0, 0), (0, 0), (0, QK_PAD - width))).reshape(k, N_HEADS * QK_PAD)


EARLY = ("mlp_w1", "mlp_w2", "conv_w_in", "conv_w_out")
LATE = ("attn_w_down", "attn_w_uq", "attn_w_ukv", "attn_w_o")


def _local_step(x, positions, target, wb, gains, mlp_fulls=None):
    s = x.shape[0]
    cc, sa, sb = _rope_tables(positions)
    pos_col = positions.reshape(s, 1)
    pos_row = positions.reshape(1, s)

    saved = []
    for i in range(4):
        j = i // 2
        g_mix = gains["g_mix"][i:i + 1]
        g_mlp = gains["g_mlp"][i:i + 1]
        if i % 2 == 0:
            w_down = jnp.pad(wb["attn_w_down"][j], ((0, 0), (0, DOWN_PAD - DOWN_DIM)))
            w_uq = _pad_heads(wb["attn_w_uq"][j], QK_DIM)
            w_ukv = wb["attn_w_ukv"][j]
            g_qa = gains["attn_g_q_a"][j:j + 1]
            g_kva = gains["attn_g_kv_a"][j:j + 1]
            g_q = jnp.pad(gains["attn_g_qnorm"][j:j + 1], ((0, 0), (0, QK_PAD - QK_DIM)))
            g_k = jnp.pad(gains["attn_g_knorm"][j:j + 1], ((0, 0), (0, QK_PAD - QK_DIM)))
            h, a = _norm_mm(x, g_mix, w_down, out_dtype=F32, name=f"mla_down_{j}")
            cq, ckv, q, k, v = _mla_prep(a, g_qa, g_kva, w_uq, w_ukv, g_q, g_k, cc, sa, sb, name=f"mla_prep_{j}")
            o, lse, got = _flash_fwd(q, k, v, pos_col, pos_row, name=f"flash_fwd_{j}",
                                     gather=mlp_fulls if i == 0 else None)
            if got:
                wb = {**wb, **_gather_weights(got, name="gather_mlp_forward", ici=False)}
            x_mid = _mm_nn(o, wb["attn_w_o"], layer=j, out_dtype=F32, residual=x, name=f"mla_out_{j}")
            mix = dict(h=h, a=a, cq=cq, ckv=ckv, q=q, k=k, v=v, o=o, lse=lse, w_down=w_down, w_uq=w_uq, w_ukv=w_ukv,
                       g_qa=g_qa, g_kva=g_kva, g_q=g_q, g_k=g_k)
        else:
            h, bcu = _norm_mm(x, g_mix, wb["conv_w_in"], layer=j, out_dtype=F32, name=f"conv_in_{j}")
            z = _conv_gate(bcu, gains["conv_w"][j], name=f"conv_gate_{j}")
            x_mid = _mm_nn(z, wb["conv_w_out"], layer=j, out_dtype=F32, residual=x, name=f"conv_out_{j}")
            mix = dict(h=h, bcu=bcu, z=z)
        h2, u, act = _mlp_up(x_mid, g_mlp, wb["mlp_w1"], layer=i, name=f"mlp_up_{i}")
        x_out = _mm_nn(act, wb["mlp_w2"], layer=i, out_dtype=F32, residual=x_mid, name=f"mlp_down_{i}")
        saved.append(dict(x_in=x, x_mid=x_mid, mix=mix, h2=h2, u=u, act=act, g_mix=g_mix, g_mlp=g_mlp))
        x = x_out

    dx, loss = _loss_head(x, target, name="loss_head")

    gw = {n: None for n in BIG}
    exchanged = None
    g_uq = [None, None]
    gs = {n: [None] * SMALL_GRADS[n][0] for n in SMALL_GRADS}

    def wgrad(nm, layer, a, b, **kw):
        gw[nm] = _mm_tn(a, b, stack=gw[nm], layer=layer, layers=BIG[nm][0][0], name=f"{nm}_grad_{layer}", **kw)

    for i in reversed(range(4)):
        j = i // 2
        sv = saved[i]
        mix = sv["mix"]
        du = _mlp_down_bwd(dx, wb["mlp_w2"], sv["u"], layer=i, name=f"mlp_down_bwd_{i}")
        wgrad("mlp_w2", i, sv["act"], dx)
        wgrad("mlp_w1", i, sv["h2"], du)
        dx, dg = _nt_rms_bwd(du, wb["mlp_w1"], sv["x_mid"], sv["g_mlp"], dx, layer=i, name=f"mlp_up_bwd_{i}")
        gs["g_mlp"][i] = dg[0]
        if i % 2 == 0:
            do = _mm_nt(dx, wb["attn_w_o"], layer=j, out_dtype=BF16, name=f"mla_out_bwd_{j}")
            wgrad("attn_w_o", j, mix["o"], dx)
            delta = _attn_delta(do, mix["o"], name=f"attn_delta_{j}")
            lse_row = mix["lse"].reshape(N_HEADS, 1, s)
            delta_row = delta.reshape(N_HEADS, 1, s)
            partials = None
            if i == 0 and mlp_fulls is not None:
                r1_early, partials = _chip_partials(gw, EARLY, tag="early")
            dq, dk, dv, arrived = _flash_bwd(mix["q"], mix["k"], mix["v"], do, lse_row, delta_row, pos_col, pos_row,
                                             name=f"flash_bwd_{j}", scatter=partials)
            if partials is not None:
                exchanged = (r1_early, arrived)
            dqr, dkvr, da, dgq, dgk, dgqa, dgkva = _mla_prep_bwd(
                mix["a"], mix["g_qa"], mix["g_kva"], mix["w_uq"], mix["w_ukv"], mix["g_q"], mix["g_k"], cc, sa, sb,
                dq, dk, dv, name=f"mla_prep_bwd_{j}")
            g_uq[j] = _mm_tn(mix["cq"], dqr, name=f"attn_w_uq_grad_{j}")[0]
            wgrad("attn_w_ukv", j, mix["ckv"], dkvr)
            wgrad("attn_w_down", j, mix["h"], da, keep=DOWN_DIM)
            dx, dg = _nt_rms_bwd(da, mix["w_down"], sv["x_in"], sv["g_mix"], dx, name=f"mla_down_bwd_{j}")
            gs["attn_g_qnorm"][j] = dgq[0, :QK_DIM]
            gs["attn_g_knorm"][j] = dgk[0, :QK_DIM]
            gs["attn_g_q_a"][j] = dgqa[0]
            gs["attn_g_kv_a"][j] = dgkva[0]
        else:
            dz = _mm_nt(dx, wb["conv_w_out"], layer=j, out_dtype=F32, name=f"conv_out_bwd_{j}")
            wgrad("conv_w_out", j, mix["z"], dx)
            dbcu, dcw = _conv_gate_bwd(mix["bcu"], dz, gains["conv_w"][j], name=f"conv_gate_bwd_{j}")
            gs["conv_w"][j] = dcw
            wgrad("conv_w_in", j, mix["h"], dbcu)
            dx, dg = _nt_rms_bwd(dbcu, wb["conv_w_in"], sv["x_in"], sv["g_mix"], dx, layer=j, name=f"conv_in_bwd_{j}")
        gs["g_mix"][i] = dg[0]

    gw["attn_w_uq"] = jnp.stack(g_uq).reshape(2, Q_RANK, N_HEADS, QK_PAD)[..., :QK_DIM].reshape(BIG["attn_w_uq"][0])
    grads_small = {n: jnp.stack(v) for n, v in gs.items()}
    return loss, dx, gw, grads_small, exchanged


def kernel(x, positions, g_mix, g_mlp, attn_w_down, attn_g_q_a, attn_g_kv_a, attn_w_uq, attn_w_ukv, attn_g_qnorm, attn_g_knorm, attn_w_o, conv_w_in, conv_w, conv_w_out, mlp_w1, mlp_w2, loss_target, m_g_mix, m_g_mlp, m_attn_w_down, m_attn_g_q_a, m_attn_g_kv_a, m_attn_w_uq, m_attn_w_ukv, m_attn_g_qnorm, m_attn_g_knorm, m_attn_w_o, m_conv_w_in, m_conv_w, m_conv_w_out, m_mlp_w1, m_mlp_w2, v_g_mix, v_g_mlp, v_attn_w_down, v_attn_g_q_a, v_attn_g_kv_a, v_attn_w_uq, v_attn_w_ukv, v_attn_g_qnorm, v_attn_g_knorm, v_attn_w_o, v_conv_w_in, v_conv_w, v_conv_w_out, v_mlp_w1, v_mlp_w2):
    w = dict(g_mix=g_mix, g_mlp=g_mlp, attn_w_down=attn_w_down, attn_g_q_a=attn_g_q_a, attn_g_kv_a=attn_g_kv_a,
             attn_w_uq=attn_w_uq, attn_w_ukv=attn_w_ukv, attn_g_qnorm=attn_g_qnorm, attn_g_knorm=attn_g_knorm,
             attn_w_o=attn_w_o, conv_w_in=conv_w_in, conv_w=conv_w, conv_w_out=conv_w_out, mlp_w1=mlp_w1, mlp_w2=mlp_w2)
    m = dict(g_mix=m_g_mix, g_mlp=m_g_mlp, attn_w_down=m_attn_w_down, attn_g_q_a=m_attn_g_q_a,
             attn_g_kv_a=m_attn_g_kv_a, attn_w_uq=m_attn_w_uq, attn_w_ukv=m_attn_w_ukv, attn_g_qnorm=m_attn_g_qnorm,
             attn_g_knorm=m_attn_g_knorm, attn_w_o=m_attn_w_o, conv_w_in=m_conv_w_in, conv_w=m_conv_w,
             conv_w_out=m_conv_w_out, mlp_w1=m_mlp_w1, mlp_w2=m_mlp_w2)
    v = dict(g_mix=v_g_mix, g_mlp=v_g_mlp, attn_w_down=v_attn_w_down, attn_g_q_a=v_attn_g_q_a,
             attn_g_kv_a=v_attn_g_kv_a, attn_w_uq=v_attn_w_uq, attn_w_ukv=v_attn_w_ukv, attn_g_qnorm=v_attn_g_qnorm,
             attn_g_knorm=v_attn_g_knorm, attn_w_o=v_attn_w_o, conv_w_in=v_conv_w_in, conv_w=v_conv_w,
             conv_w_out=v_conv_w_out, mlp_w1=v_mlp_w1, mlp_w2=v_mlp_w2)
    cx, cy, cc_ = _place()

    chip = 2 * cx + cy

    def own_offset(shape, axis):
        return tuple(chip * (shape[axis] // N_CHIPS) if i == axis else 0 for i in range(3))

    fulls = {}
    for n, (shape, axis) in BIG.items():
        fulls[n] = lax.dynamic_update_slice(jnp.zeros(shape, BF16), w[n].astype(BF16), own_offset(shape, axis))
    mlp_fulls = {n: fulls.pop(n) for n in ("mlp_w1", "mlp_w2")}
    wb = _gather_weights(fulls, name="gather_weights")

    placed = lax.dynamic_update_slice(jnp.zeros(CONV_W, F32), conv_w, own_offset(CONV_W, 2))
    conv_w_full = 0.5 * _all_reduce_small(placed.reshape(-1, 128), name="conv_w_gather").reshape(CONV_W)

    gains = {n: w[n] for n in SMALL}
    gains["conv_w"] = conv_w_full

    loss, grad_x, grads_big, grads_small, (r1_early, r2_early) = _local_step(
        x[0], positions[0], loss_target[0], wb, gains, mlp_fulls)

    place = jnp.stack([cc_, chip]).astype(jnp.int32)
    r1_late, partials = _chip_partials(grads_big, LATE, tag="late")
    r1 = {**r1_early, **r1_late}
    r2 = {**r2_early, **_scatter_partials(partials)}
    halves = {n: _final_sum(grads_big[n], r1[n], r2[n], place, n, name=f"grad_final_sum_{n}") for n in BIG}
    grad_shards = _join_halves(halves)

    small = _unpack_small(_all_reduce_small(_pack_small(grads_small, SMALL_GRADS), name="gain_all_reduce"), SMALL_GRADS)
    grad_shards["conv_w"] = lax.dynamic_slice(small["conv_w"], own_offset(CONV_W, 2), conv_w.shape)

    loss_total = lax.psum(loss[0, 0], ("x", "y", "c"))

    grads, deltas, new_m, new_v = {}, {}, {}, {}
    for n in [*BIG, "conv_w"]:
        shp = w[n].shape
        two_d = (shp[0] * shp[1], shp[2])
        g2 = grad_shards[n].reshape(two_d)
        d, nm, nv = _adamw(w[n].reshape(two_d), g2, m[n].reshape(two_d), v[n].reshape(two_d), name=f"adamw_{n}")
        grads[n], deltas[n], new_m[n], new_v[n] = grad_shards[n], d.reshape(shp), nm.reshape(shp), nv.reshape(shp)
    d, nm, nv = _adamw(_pack_small(w, SMALL), _pack_small(small, SMALL), _pack_small(m, SMALL), _pack_small(v, SMALL),
                       name="adamw_gains")
    d, nm, nv = _unpack_small(d, SMALL), _unpack_small(nm, SMALL), _unpack_small(nv, SMALL)
    for n in SMALL:
        grads[n], deltas[n], new_m[n], new_v[n] = small[n], d[n], nm[n], nv[n]

    return (loss_total, grad_x[None],
            *[grads[n] for n in WEIGHT_ORDER], *[deltas[n] for n in WEIGHT_ORDER],
            *[new_m[n] for n in WEIGHT_ORDER], *[new_v[n] for n in WEIGHT_ORDER])
```

```python
import functools

import jax
import jax.numpy as jnp
from jax import lax
from jax.experimental import pallas as pl
from jax.experimental.pallas import tpu as pltpu

F32 = jnp.float32
BF16 = jnp.bfloat16

D_MODEL = 1024
N_HEADS = 8
QK_NOPE = 128
QK_ROPE = 64
QK_DIM = QK_NOPE + QK_ROPE
QK_PAD = 256
V_DIM = 128
Q_RANK = 256
KV_RANK = 128
DOWN_DIM = Q_RANK + KV_RANK + QK_ROPE
DOWN_PAD = 512
D_FF = 4 * D_MODEL
ROPE_THETA = 10000.0
EPS = 1e-6
NEG = -1e30
SCALE = QK_DIM ** -0.5
SCALE_LOG2E = SCALE * 1.4426950408889634
LOG2E = 1.4426950408889634
ATTN_CHAINS = 2

ADAM_LR = 0.001
ADAM_B1 = 0.9
ADAM_B2 = 0.999
ADAM_EPS = 1e-08
ADAM_WD = 0.01
ADAM_STEP = 10

N_CHIPS = 4
MESH = pl.DeviceIdType.MESH
ANY = pl.BlockSpec(memory_space=pl.ANY)

TM = 512
TM_WIDE = 512
FWD_TQ = 1024
FWD_TK = 1024
BWD_TQ = 1024
BWD_TK = 1024
T_PREP = 256
T_RED = 1024
SUM_BLOCK_BYTES = 2 * 1024 * 1024


def _tile(n, pref):
    t = min(n, pref)
    assert n % t == 0, (n, t)
    return t


def _cparams(*sem):
    return pltpu.CompilerParams(dimension_semantics=sem)


def _dot(a, b):
    return jnp.dot(a, b, preferred_element_type=F32)


def _dot_nt(a, b):
    return lax.dot_general(a, b, (((1,), (1,)), ((), ())), preferred_element_type=F32)


def _dot_tn(a, b):
    return lax.dot_general(a, b, (((0,), (0,)), ((), ())), preferred_element_type=F32)


def _rms(x, width):
    r = lax.rsqrt(jnp.sum(x * x, axis=-1, keepdims=True) * (1.0 / width) + EPS)
    return x * r, r


def _rms_bwd(xhat, r, dxhat, width):
    return r * (dxhat - xhat * (jnp.sum(dxhat * xhat, axis=-1, keepdims=True) * (1.0 / width)))


def _rope(t, cc, sa, sb):
    return t * cc + pltpu.roll(t, 96, 1) * sa + pltpu.roll(t, 32, 1) * sb


def _rope_t(g, cc, sa, sb):
    return g * cc + pltpu.roll(g * sa, 32, 1) + pltpu.roll(g * sb, 96, 1)


def _wspec(w, layer):
    once = pl.Buffered(1)
    if w.ndim == 2:
        return pl.BlockSpec(w.shape, lambda *_: (0, 0), pipeline_mode=once)
    return pl.BlockSpec((None,) + w.shape[1:], lambda *_: (layer, 0, 0), pipeline_mode=once)


def _mm_nn(a, b, *, out_dtype, name, residual=None, layer=0):
    m, k = a.shape
    n = b.shape[-1]
    tm = _tile(m, TM)

    def body(*refs):
        if residual is None:
            a_ref, b_ref, o_ref = refs
        else:
            a_ref, b_ref, r_ref, o_ref = refs
        acc = _dot(a_ref[...].astype(BF16), b_ref[...])
        if residual is not None:
            acc = acc + r_ref[...]
        o_ref[...] = acc.astype(o_ref.dtype)

    in_specs = [pl.BlockSpec((tm, k), lambda i: (i, 0)), _wspec(b, layer)]
    args = [a, b]
    if residual is not None:
        in_specs.append(pl.BlockSpec((tm, n), lambda i: (i, 0)))
        args.append(residual)
    return pl.pallas_call(
        body, name=name, grid=(m // tm,), in_specs=in_specs,
        out_specs=pl.BlockSpec((tm, n), lambda i: (i, 0)),
        out_shape=jax.ShapeDtypeStruct((m, n), out_dtype),
        compiler_params=_cparams("parallel"),
    )(*args)


def _mm_nt(a, b, *, out_dtype, name, layer=0):
    m, k = a.shape
    n = b.shape[-2]
    tm = _tile(m, TM)

    def body(a_ref, b_ref, o_ref):
        o_ref[...] = _dot_nt(a_ref[...].astype(BF16), b_ref[...]).astype(o_ref.dtype)

    return pl.pallas_call(
        body, name=name, grid=(m // tm,),
        in_specs=[pl.BlockSpec((tm, k), lambda i: (i, 0)), _wspec(b, layer)],
        out_specs=pl.BlockSpec((tm, n), lambda i: (i, 0)),
        out_shape=jax.ShapeDtypeStruct((m, n), out_dtype),
        compiler_params=_cparams("parallel"),
    )(a, b)


def _mm_tn(a, b, *, name, stack=None, layer=0, layers=1, keep=None):
    s, ka = a.shape
    n = b.shape[1]
    ts = _tile(s, T_RED)
    tka = _tile(ka, 1024)
    tn = _tile(n, 1024)
    n_out = n if keep is None else keep
    assert keep is None or tn == n

    def body(a_ref, b_ref, *rest):
        o_ref = rest[-1]

        @pl.when(pl.program_id(2) == 0)
        def _():
            o_ref[...] = jnp.zeros_like(o_ref)

        o_ref[...] += _dot_tn(a_ref[...].astype(BF16), b_ref[...].astype(BF16))[:, :n_out if keep else tn]

    in_specs = [pl.BlockSpec((ts, tka), lambda i, j, t: (t, i)), pl.BlockSpec((ts, tn), lambda i, j, t: (t, j))]
    args = [a, b]
    if stack is not None:
        in_specs.append(ANY)
        args.append(stack)
    return pl.pallas_call(
        body, name=name, grid=(ka // tka, n // tn, s // ts), in_specs=in_specs,
        out_specs=pl.BlockSpec((None, tka, tn if keep is None else keep), lambda i, j, t: (layer, i, j)),
        out_shape=jax.ShapeDtypeStruct((layers, ka, n_out), F32),
        input_output_aliases={} if stack is None else {2: 0},
        compiler_params=_cparams("parallel", "parallel", "arbitrary"),
    )(*args)


def _norm_mm(x, g, w, *, out_dtype, name, layer=0):
    s, d = x.shape
    n = w.shape[-1]
    tm = _tile(s, TM)

    def body(x_ref, g_ref, w_ref, h_ref, o_ref):
        xhat, _ = _rms(x_ref[...], d)
        h = (xhat * g_ref[...]).astype(BF16)
        h_ref[...] = h
        o_ref[...] = _dot(h, w_ref[...]).astype(o_ref.dtype)

    return pl.pallas_call(
        body, name=name, grid=(s // tm,),
        in_specs=[pl.BlockSpec((tm, d), lambda i: (i, 0)), pl.BlockSpec((1, d), lambda i: (0, 0)), _wspec(w, layer)],
        out_specs=[pl.BlockSpec((tm, d), lambda i: (i, 0)), pl.BlockSpec((tm, n), lambda i: (i, 0))],
        out_shape=[jax.ShapeDtypeStruct((s, d), BF16), jax.ShapeDtypeStruct((s, n), out_dtype)],
        compiler_params=_cparams("parallel"),
    )(x, g, w)


def _nt_rms_bwd(dy, w, x, g, dres, *, name, layer=0):
    s, n = dy.shape
    d = x.shape[1]
    tm = _tile(s, TM)

    def body(dy_ref, w_ref, x_ref, g_ref, dres_ref, dx_ref, dg_ref):
        @pl.when(pl.program_id(0) == 0)
        def _():
            dg_ref[...] = jnp.zeros_like(dg_ref)

        dh = _dot_nt(dy_ref[...], w_ref[...])
        xhat, r = _rms(x_ref[...], d)
        dg_ref[...] += jnp.sum(dh * xhat, axis=0, keepdims=True)
        dx_ref[...] = dres_ref[...] + _rms_bwd(xhat, r, dh * g_ref[...], d)

    return pl.pallas_call(
        body, name=name, grid=(s // tm,),
        in_specs=[pl.BlockSpec((tm, n), lambda i: (i, 0)), _wspec(w, layer),
                  pl.BlockSpec((tm, d), lambda i: (i, 0)), pl.BlockSpec((1, d), lambda i: (0, 0)),
                  pl.BlockSpec((tm, d), lambda i: (i, 0))],
        out_specs=[pl.BlockSpec((tm, d), lambda i: (i, 0)), pl.BlockSpec((1, d), lambda i: (0, 0))],
        out_shape=[jax.ShapeDtypeStruct((s, d), F32), jax.ShapeDtypeStruct((1, d), F32)],
        compiler_params=_cparams("arbitrary"),
    )(dy, w, x, g, dres)


def _mlp_up(x, g, w1, *, name, layer=0):
    s, d = x.shape
    n = w1.shape[-1]
    tm = _tile(s, TM_WIDE)

    def body(x_ref, g_ref, w_ref, h_ref, u_ref, act_ref):
        xhat, _ = _rms(x_ref[...], d)
        h = (xhat * g_ref[...]).astype(BF16)
        h_ref[...] = h
        u = _dot(h, w_ref[...])
        u_ref[...] = u.astype(BF16)
        act_ref[...] = jnp.square(jnp.maximum(u, 0.0)).astype(BF16)

    return pl.pallas_call(
        body, name=name, grid=(s // tm,),
        in_specs=[pl.BlockSpec((tm, d), lambda i: (i, 0)), pl.BlockSpec((1, d), lambda i: (0, 0)), _wspec(w1, layer)],
        out_specs=[pl.BlockSpec((tm, d), lambda i: (i, 0)), pl.BlockSpec((tm, n), lambda i: (i, 0)),
                   pl.BlockSpec((tm, n), lambda i: (i, 0))],
        out_shape=[jax.ShapeDtypeStruct((s, d), BF16), jax.ShapeDtypeStruct((s, n), BF16),
                   jax.ShapeDtypeStruct((s, n), BF16)],
        compiler_params=_cparams("parallel"),
    )(x, g, w1)


def _mlp_down_bwd(dy, w2, u, *, name, layer=0):
    s, d = dy.shape
    n = w2.shape[-2]
    tm = _tile(s, TM_WIDE)

    def body(dy_ref, w_ref, u_ref, du_ref):
        dact = _dot_nt(dy_ref[...].astype(BF16), w_ref[...])
        du_ref[...] = (dact * (2.0 * jnp.maximum(u_ref[...].astype(F32), 0.0))).astype(BF16)

    return pl.pallas_call(
        body, name=name, grid=(s // tm,),
        in_specs=[pl.BlockSpec((tm, d), lambda i: (i, 0)), _wspec(w2, layer),
                  pl.BlockSpec((tm, n), lambda i: (i, 0))],
        out_specs=pl.BlockSpec((tm, n), lambda i: (i, 0)),
        out_shape=jax.ShapeDtypeStruct((s, n), BF16),
        compiler_params=_cparams("parallel"),
    )(dy, w2, u)


def _conv_gate(bcu, conv_w, *, name):
    s = bcu.shape[0]
    d = D_MODEL
    tm = _tile(s, TM)
    hb = tm // 8

    def body(bcu_ref, prev_ref, w_ref, z_ref, pbuf):
        i = pl.program_id(0)
        gb = bcu_ref[:, 0:d]
        p = bcu_ref[:, d:2 * d] * bcu_ref[:, 2 * d:3 * d]
        pprev = prev_ref[:, d:2 * d] * prev_ref[:, 2 * d:3 * d]
        pbuf[0:8, :] = jnp.where(i > 0, pprev, 0.0)
        pbuf[8:8 + tm, :] = p
        cv = w_ref[2:3, :] * p + w_ref[1:2, :] * pbuf[7:7 + tm, :] + w_ref[0:1, :] * pbuf[6:6 + tm, :]
        z_ref[...] = (gb * cv).astype(BF16)

    return pl.pallas_call(
        body, name=name, grid=(s // tm,),
        in_specs=[pl.BlockSpec((tm, 3 * d), lambda i: (i, 0)),
                  pl.BlockSpec((8, 3 * d), lambda i: (jnp.maximum(i * hb - 1, 0), 0)),
                  pl.BlockSpec((3, d), lambda i: (0, 0))],
        out_specs=pl.BlockSpec((tm, d), lambda i: (i, 0)),
        out_shape=jax.ShapeDtypeStruct((s, d), BF16),
        scratch_shapes=[pltpu.VMEM((tm + 8, d), F32)],
        compiler_params=_cparams("parallel"),
    )(bcu, bcu, conv_w)


def _conv_gate_bwd(bcu, dz, conv_w, *, name):
    s = bcu.shape[0]
    d = D_MODEL
    tm = _tile(s, TM)
    hb = tm // 8
    nt = s // tm

    def body(bcu_ref, prev_ref, next_ref, dz_ref, dznext_ref, w_ref, dbcu_ref, dw_ref, pbuf, dbuf):
        i = pl.program_id(0)

        @pl.when(i == 0)
        def _():
            dw_ref[...] = jnp.zeros_like(dw_ref)

        gb = bcu_ref[:, 0:d]
        gc = bcu_ref[:, d:2 * d]
        uu = bcu_ref[:, 2 * d:3 * d]
        p = gc * uu
        pprev = prev_ref[:, d:2 * d] * prev_ref[:, 2 * d:3 * d]
        pbuf[0:8, :] = jnp.where(i > 0, pprev, 0.0)
        pbuf[8:8 + tm, :] = p
        p1 = pbuf[7:7 + tm, :]
        p2 = pbuf[6:6 + tm, :]
        cv = w_ref[2:3, :] * p + w_ref[1:2, :] * p1 + w_ref[0:1, :] * p2
        dz_t = dz_ref[...]
        dcv = dz_t * gb
        dcv_next = dznext_ref[...] * next_ref[:, 0:d]
        dbuf[0:tm, :] = dcv
        dbuf[tm:tm + 8, :] = jnp.where(i < nt - 1, dcv_next, 0.0)
        dp = w_ref[2:3, :] * dcv + w_ref[1:2, :] * dbuf[1:1 + tm, :] + w_ref[0:1, :] * dbuf[2:2 + tm, :]
        dw_ref[2:3, :] += jnp.sum(dcv * p, axis=0, keepdims=True)
        dw_ref[1:2, :] += jnp.sum(dcv * p1, axis=0, keepdims=True)
        dw_ref[0:1, :] += jnp.sum(dcv * p2, axis=0, keepdims=True)
        dbcu_ref[:, 0:d] = (dz_t * cv).astype(BF16)
        dbcu_ref[:, d:2 * d] = (dp * uu).astype(BF16)
        dbcu_ref[:, 2 * d:3 * d] = (dp * gc).astype(BF16)

    nxt = lambda i: (jnp.minimum((i + 1) * hb, s // 8 - 1), 0)
    return pl.pallas_call(
        body, name=name, grid=(nt,),
        in_specs=[pl.BlockSpec((tm, 3 * d), lambda i: (i, 0)),
                  pl.BlockSpec((8, 3 * d), lambda i: (jnp.maximum(i * hb - 1, 0), 0)),
                  pl.BlockSpec((8, 3 * d), nxt),
                  pl.BlockSpec((tm, d), lambda i: (i, 0)),
                  pl.BlockSpec((8, d), nxt),
                  pl.BlockSpec((3, d), lambda i: (0, 0))],
        out_specs=[pl.BlockSpec((tm, 3 * d), lambda i: (i, 0)), pl.BlockSpec((3, d), lambda i: (0, 0))],
        out_shape=[jax.ShapeDtypeStruct((s, 3 * d), BF16), jax.ShapeDtypeStruct((3, d), F32)],
        scratch_shapes=[pltpu.VMEM((tm + 8, d), F32), pltpu.VMEM((tm + 8, d), F32)],
        compiler_params=_cparams("arbitrary"),
    )(bcu, bcu, bcu, dz, dz, conv_w)


def _mla_prep(a, g_qa, g_kva, w_uq, w_ukv, g_q, g_k, cc, sa, sb, *, name):
    s = a.shape[0]
    ts = _tile(s, T_PREP)

    def body(a_ref, gqa_ref, gkva_ref, wuq_ref, wukv_ref, gq_ref, gk_ref, cc_ref, sa_ref, sb_ref,
             cq_ref, ckv_ref, q_ref, k_ref, v_ref):
        xq, _ = _rms(a_ref[:, 0:Q_RANK], Q_RANK)
        cq = (xq * gqa_ref[...]).astype(BF16)
        cq_ref[...] = cq
        xkv, _ = _rms(a_ref[:, Q_RANK:Q_RANK + KV_RANK], KV_RANK)
        ckv = (xkv * gkva_ref[...]).astype(BF16)
        ckv_ref[...] = ckv
        kpe = a_ref[:, Q_RANK + KV_RANK:DOWN_PAD]
        kpe_ss = jnp.sum(kpe * kpe, axis=-1, keepdims=True)
        cc_t, sa_t, sb_t = cc_ref[...], sa_ref[...], sb_ref[...]
        gq = gq_ref[...]
        gk = gk_ref[...]
        for h in range(N_HEADS):
            cols = slice(h * QK_PAD, (h + 1) * QK_PAD)
            qhat, _ = _rms(_dot(cq, wuq_ref[:, cols]), QK_DIM)
            qn = qhat * (gq * SCALE_LOG2E)
            q_ref[h, :, 0:QK_NOPE] = qn[:, 0:QK_NOPE].astype(BF16)
            q_ref[h, :, QK_NOPE:QK_PAD] = _rope(qn[:, QK_NOPE:QK_PAD], cc_t, sa_t, sb_t).astype(BF16)
            kvr = _dot(ckv, wukv_ref[:, cols])
            kn = kvr[:, 0:QK_NOPE]
            rk = lax.rsqrt((jnp.sum(kn * kn, axis=-1, keepdims=True) + kpe_ss) * (1.0 / QK_DIM) + EPS)
            k_ref[h, :, 0:QK_NOPE] = (kn * rk * gk[:, 0:QK_NOPE]).astype(BF16)
            k_ref[h, :, QK_NOPE:QK_PAD] = _rope(kpe * rk * gk[:, QK_NOPE:QK_PAD], cc_t, sa_t, sb_t).astype(BF16)
            v_ref[h, :, 0:V_DIM] = kvr[:, QK_NOPE:QK_PAD].astype(BF16)
            v_ref[h, :, V_DIM:2 * V_DIM] = jnp.ones((ts, V_DIM), BF16)

    row = lambda i: (i, 0)
    fixed = lambda i: (0, 0)
    head = lambda i: (0, i, 0)
    return pl.pallas_call(
        body, name=name, grid=(s // ts,),
        in_specs=[pl.BlockSpec((ts, DOWN_PAD), row), pl.BlockSpec((1, Q_RANK), fixed), pl.BlockSpec((1, KV_RANK), fixed),
                  pl.BlockSpec((Q_RANK, N_HEADS * QK_PAD), fixed), pl.BlockSpec((KV_RANK, N_HEADS * QK_PAD), fixed),
                  pl.BlockSpec((1, QK_PAD), fixed), pl.BlockSpec((1, QK_PAD), fixed),
                  pl.BlockSpec((ts, 128), row), pl.BlockSpec((ts, 128), row), pl.BlockSpec((ts, 128), row)],
        out_specs=[pl.BlockSpec((ts, Q_RANK), row), pl.BlockSpec((ts, KV_RANK), row),
                   pl.BlockSpec((N_HEADS, ts, QK_PAD), head), pl.BlockSpec((N_HEADS, ts, QK_PAD), head),
                   pl.BlockSpec((N_HEADS, ts, 2 * V_DIM), head)],
        out_shape=[jax.ShapeDtypeStruct((s, Q_RANK), BF16), jax.ShapeDtypeStruct((s, KV_RANK), BF16),
                   jax.ShapeDtypeStruct((N_HEADS, s, QK_PAD), BF16), jax.ShapeDtypeStruct((N_HEADS, s, QK_PAD), BF16),
                   jax.ShapeDtypeStruct((N_HEADS, s, 2 * V_DIM), BF16)],
        compiler_params=_cparams("parallel"),
    )(a, g_qa, g_kva, w_uq, w_ukv, g_q, g_k, cc, sa, sb)


def _mla_prep_bwd(a, g_qa, g_kva, w_uq, w_ukv, g_q, g_k, cc, sa, sb, dq, dk, dv, *, name):
    s = a.shape[0]
    ts = _tile(s, T_PREP)

    def body(a_ref, gqa_ref, gkva_ref, wuq_ref, wukv_ref, gq_ref, gk_ref, cc_ref, sa_ref, sb_ref,
             dq_ref, dk_ref, dv_ref, dqr_ref, dkvr_ref, da_ref, dgq_ref, dgk_ref, dgqa_ref, dgkva_ref):
        @pl.when(pl.program_id(0) == 0)
        def _():
            dgq_ref[...] = jnp.zeros_like(dgq_ref)
            dgk_ref[...] = jnp.zeros_like(dgk_ref)
            dgqa_ref[...] = jnp.zeros_like(dgqa_ref)
            dgkva_ref[...] = jnp.zeros_like(dgkva_ref)

        xq, r_q = _rms(a_ref[:, 0:Q_RANK], Q_RANK)
        cq = (xq * gqa_ref[...]).astype(BF16)
        xkv, r_kv = _rms(a_ref[:, Q_RANK:Q_RANK + KV_RANK], KV_RANK)
        ckv = (xkv * gkva_ref[...]).astype(BF16)
        kpe = a_ref[:, Q_RANK + KV_RANK:DOWN_PAD]
        kpe_ss = jnp.sum(kpe * kpe, axis=-1, keepdims=True)
        cc_t, sa_t, sb_t = cc_ref[...], sa_ref[...], sb_ref[...]
        gq = gq_ref[...]
        gk = gk_ref[...]
        dcq = jnp.zeros((ts, Q_RANK), F32)
        dckv = jnp.zeros((ts, KV_RANK), F32)
        dkpe = jnp.zeros((ts, 128), F32)
        dgq = jnp.zeros((1, QK_PAD), F32)
        dgk_n = jnp.zeros((1, QK_NOPE), F32)
        dgk_p = jnp.zeros((1, 128), F32)
        for h in range(N_HEADS):
            cols = slice(h * QK_PAD, (h + 1) * QK_PAD)
            qhat, rq = _rms(_dot(cq, wuq_ref[:, cols]), QK_DIM)
            dqn = jnp.concatenate(
                [dq_ref[h, :, 0:QK_NOPE], _rope_t(dq_ref[h, :, QK_NOPE:QK_PAD], cc_t, sa_t, sb_t)], axis=1)
            dgq = dgq + jnp.sum(dqn * qhat, axis=0, keepdims=True)
            dqr = _rms_bwd(qhat, rq, dqn * gq, QK_DIM).astype(BF16)
            dqr_ref[:, cols] = dqr
            dcq = dcq + _dot_nt(dqr, wuq_ref[:, cols])
            kn = _dot(ckv, wukv_ref[:, h * QK_PAD:h * QK_PAD + QK_NOPE])
            rk = lax.rsqrt((jnp.sum(kn * kn, axis=-1, keepdims=True) + kpe_ss) * (1.0 / QK_DIM) + EPS)
            khat_n = kn * rk
            khat_p = kpe * rk
            dkn = dk_ref[h, :, 0:QK_NOPE]
            dkp = _rope_t(dk_ref[h, :, QK_NOPE:QK_PAD], cc_t, sa_t, sb_t)
            dgk_n = dgk_n + jnp.sum(dkn * khat_n, axis=0, keepdims=True)
            dgk_p = dgk_p + jnp.sum(dkp * khat_p, axis=0, keepdims=True)
            dxn = dkn * gk[:, 0:QK_NOPE]
            dxp = dkp * gk[:, QK_NOPE:QK_PAD]
            mean = (jnp.sum(dxn * khat_n, axis=-1, keepdims=True)
                    + jnp.sum(dxp * khat_p, axis=-1, keepdims=True)) * (1.0 / QK_DIM)
            dkpe = dkpe + rk * (dxp - khat_p * mean)
            dkvr = jnp.concatenate([rk * (dxn - khat_n * mean), dv_ref[h, :, :]], axis=1).astype(BF16)
            dkvr_ref[:, cols] = dkvr
            dckv = dckv + _dot_nt(dkvr, wukv_ref[:, cols])
        dgq_ref[...] += dgq
        dgk_ref[:, 0:QK_NOPE] += dgk_n
        dgk_ref[:, QK_NOPE:QK_PAD] += dgk_p
        dgqa_ref[...] += jnp.sum(dcq * xq, axis=0, keepdims=True)
        dgkva_ref[...] += jnp.sum(dckv * xkv, axis=0, keepdims=True)
        da_ref[:, 0:Q_RANK] = _rms_bwd(xq, r_q, dcq * gqa_ref[...], Q_RANK).astype(BF16)
        da_ref[:, Q_RANK:Q_RANK + KV_RANK] = _rms_bwd(xkv, r_kv, dckv * gkva_ref[...], KV_RANK).astype(BF16)
        da_ref[:, Q_RANK + KV_RANK:DOWN_PAD] = dkpe.astype(BF16)

    row = lambda i: (i, 0)
    fixed = lambda i: (0, 0)
    head = lambda i: (0, i, 0)
    wide = N_HEADS * QK_PAD
    return pl.pallas_call(
        body, name=name, grid=(s // ts,),
        in_specs=[pl.BlockSpec((ts, DOWN_PAD), row), pl.BlockSpec((1, Q_RANK), fixed), pl.BlockSpec((1, KV_RANK), fixed),
                  pl.BlockSpec((Q_RANK, wide), fixed), pl.BlockSpec((KV_RANK, wide), fixed),
                  pl.BlockSpec((1, QK_PAD), fixed), pl.BlockSpec((1, QK_PAD), fixed),
                  pl.BlockSpec((ts, 128), row), pl.BlockSpec((ts, 128), row), pl.BlockSpec((ts, 128), row),
                  pl.BlockSpec((N_HEADS, ts, QK_PAD), head), pl.BlockSpec((N_HEADS, ts, QK_PAD), head),
                  pl.BlockSpec((N_HEADS, ts, V_DIM), head)],
        out_specs=[pl.BlockSpec((ts, wide), row), pl.BlockSpec((ts, wide), row), pl.BlockSpec((ts, DOWN_PAD), row),
                   pl.BlockSpec((1, QK_PAD), fixed), pl.BlockSpec((1, QK_PAD), fixed),
                   pl.BlockSpec((1, Q_RANK), fixed), pl.BlockSpec((1, KV_RANK), fixed)],
        out_shape=[jax.ShapeDtypeStruct((s, wide), BF16), jax.ShapeDtypeStruct((s, wide), BF16),
                   jax.ShapeDtypeStruct((s, DOWN_PAD), BF16),
                   jax.ShapeDtypeStruct((1, QK_PAD), F32), jax.ShapeDtypeStruct((1, QK_PAD), F32),
                   jax.ShapeDtypeStruct((1, Q_RANK), F32), jax.ShapeDtypeStruct((1, KV_RANK), F32)],
        compiler_params=_cparams("arbitrary"),
    )(a, g_qa, g_kva, w_uq, w_ukv, g_q, g_k, cc, sa, sb, dq, dk, dv)


def _flash_fwd(q, k, v, pos_col, pos_row, *, name, gather=None):
    nh, s, _ = q.shape
    tq = _tile(s, FWD_TQ)
    tk = _tile(s, FWD_TK)
    sq = tq // ATTN_CHAINS
    nq = s // tq
    names = list(gather or {})
    ng = len(names)

    def body(q_ref, k_ref, v_ref, pq_ref, pk_ref, *rest):
        o_ref, lse_ref = rest[ng:ng + 2]
        m_sc, acc_sc = rest[2 * ng + 2:2 * ng + 4]
        qb = pl.program_id(1)
        if ng:
            sends, recvs = _gather_ici_copies(rest[ng + 2:2 * ng + 2], names, *rest[2 * ng + 4:], base=0, stride=3)

            @pl.when((pl.program_id(0) == 0) & (qb == 0))
            def _():
                for cp in sends:
                    cp.start()

        m_sc[...] = jnp.full_like(m_sc, NEG)
        acc_sc[...] = jnp.zeros_like(acc_sc)

        def step(kb, masked):
            keys = pl.ds(pl.multiple_of(kb * tk, tk), tk)
            kt = k_ref[0, keys, :]
            vt = v_ref[0, keys, :]
            scores = [_dot_nt(q_ref[0, u * sq:(u + 1) * sq, :], kt) for u in range(ATTN_CHAINS)]
            for u in range(ATTN_CHAINS):
                rows = slice(u * sq, (u + 1) * sq)
                sc = scores[u]
                if masked:
                    sc = jnp.where(pq_ref[rows, :] >= pk_ref[:, keys], sc, NEG)
                m_prev = m_sc[rows, :]
                m_new = jnp.maximum(m_prev, jnp.max(sc, axis=-1, keepdims=True))
                alpha = jnp.exp2(m_prev - m_new)
                p = jnp.exp2(sc - jnp.tile(m_new, (1, tk // 128)))
                acc_sc[rows, :] = jnp.tile(alpha, (1, 2)) * acc_sc[rows, :] + _dot(p.astype(BF16), vt)
                m_sc[rows, :] = m_new

        n_before = (qb * tq) // tk
        n_seen = (qb * tq + tq - 1) // tk + 1
        lax.fori_loop(0, n_before, lambda kb, c: (step(kb, False), c)[1], 0)
        lax.fori_loop(n_before, n_seen, lambda kb, c: (step(kb, True), c)[1], 0)
        l = acc_sc[:, V_DIM:2 * V_DIM]
        o_ref[...] = (acc_sc[:, 0:V_DIM] / l).astype(BF16)
        lse_ref[0] = m_sc[:, 0:1] * (1.0 / LOG2E) + jnp.log(l[:, 0:1])

        if ng:
            @pl.when((pl.program_id(0) == nh - 1) & (qb == nq - 1))
            def _():
                for cp in recvs:
                    cp.wait_recv()
                for cp in sends:
                    cp.wait_send()

    arrays = [gather[nm] for nm in names]
    out = pl.pallas_call(
        body, name=name, grid=(nh, nq),
        in_specs=[pl.BlockSpec((1, tq, QK_PAD), lambda h, qb: (h, qb, 0)),
                  pl.BlockSpec((1, s, QK_PAD), lambda h, qb: (h, 0, 0)),
                  pl.BlockSpec((1, s, 2 * V_DIM), lambda h, qb: (h, 0, 0)),
                  pl.BlockSpec((tq, 1), lambda h, qb: (qb, 0)),
                  pl.BlockSpec((1, s), lambda h, qb: (0, 0))] + [ANY] * ng,
        out_specs=[pl.BlockSpec((tq, V_DIM), lambda h, qb: (qb, h)),
                   pl.BlockSpec((1, tq, 1), lambda h, qb: (h, qb, 0))] + [ANY] * ng,
        scratch_shapes=[pltpu.VMEM((tq, 128), F32), pltpu.VMEM((tq, 2 * V_DIM), F32)]
        + ([pltpu.SemaphoreType.DMA((3 * ng,)), pltpu.SemaphoreType.DMA((3 * ng,))] if ng else []),
        out_shape=[jax.ShapeDtypeStruct((s, nh * V_DIM), BF16), jax.ShapeDtypeStruct((nh, s, 1), F32)]
        + [jax.ShapeDtypeStruct(a.shape, a.dtype) for a in arrays],
        input_output_aliases={5 + i: 2 + i for i in range(ng)},
        compiler_params=_cparams("arbitrary", "arbitrary") if ng else _cparams("parallel", "parallel"),
    )(q, k, v, pos_col, pos_row, *arrays)
    return out[0], out[1], dict(zip(names, out[2:]))


def _attn_delta(do, o, *, name):
    s = do.shape[0]
    tm = _tile(s, TM)

    def body(do_ref, o_ref, d_ref):
        for h in range(N_HEADS):
            cols = slice(h * V_DIM, (h + 1) * V_DIM)
            d_ref[h] = jnp.sum(do_ref[:, cols].astype(F32) * o_ref[:, cols].astype(F32), axis=-1, keepdims=True)

    return pl.pallas_call(
        body, name=name, grid=(s // tm,),
        in_specs=[pl.BlockSpec((tm, N_HEADS * V_DIM), lambda i: (i, 0))] * 2,
        out_specs=pl.BlockSpec((N_HEADS, tm, 1), lambda i: (0, i, 0)),
        out_shape=jax.ShapeDtypeStruct((N_HEADS, s, 1), F32),
        compiler_params=_cparams("parallel"),
    )(do, o)


def _flash_bwd(q, k, v, do, lse_row, delta_row, pos_col, pos_row, *, name, scatter=None):
    nh, s, _ = q.shape
    tq = _tile(s, BWD_TQ)
    tk = _tile(s, BWD_TK)
    nq, nk = s // tq, s // tk
    sk = tk // ATTN_CHAINS
    names = list(scatter or {})
    ng = len(names)

    def body(q_ref, k_ref, v_ref, do_ref, lse_ref, delta_ref, pq_ref, pk_ref, *rest):
        dq_ref, dk_ref, dv_ref = rest[ng:ng + 3]
        dk_sc, dv_sc = rest[2 * ng + 3:2 * ng + 5]
        kb = pl.program_id(1)
        if ng:
            copies = _scatter_copies(rest[:ng], rest[ng + 3:2 * ng + 3], names, *rest[2 * ng + 5:])

            @pl.when((pl.program_id(0) == 0) & (kb == 0))
            def _():
                for cp in copies:
                    cp.start()

        @pl.when(kb == 0)
        def _():
            dq_ref[...] = jnp.zeros_like(dq_ref)

        dk_sc[...] = jnp.zeros_like(dk_sc)
        dv_sc[...] = jnp.zeros_like(dv_sc)

        def step(qb, masked):
            trim = masked and tq == tk
            start = pl.multiple_of(qb * tq, tq)
            offs = [u * sk if trim else 0 for u in range(ATTN_CHAINS)]
            qss = [pl.ds(start + offs[u], tq - offs[u]) for u in range(ATTN_CHAINS)]
            qts = [q_ref[0, qss[u], :] for u in range(ATTN_CHAINS)]
            dots = [do_ref[qss[u], :] for u in range(ATTN_CHAINS)]
            sts = [_dot_nt(k_ref[0, u * sk:(u + 1) * sk, :], qts[u]) for u in range(ATTN_CHAINS)]
            dpts = [_dot_nt(v_ref[0, u * sk:(u + 1) * sk, :], dots[u]) for u in range(ATTN_CHAINS)]
            parts = []
            for u in range(ATTN_CHAINS):
                rows = slice(u * sk, (u + 1) * sk)
                pt = jnp.exp2(sts[u] - lse_ref[0, :, qss[u]] * LOG2E)
                if masked:
                    pt = jnp.where(pq_ref[:, qss[u]] >= pk_ref[rows, :], pt, 0.0)
                dv_sc[rows, :] += _dot(pt.astype(BF16), dots[u])
                dst = (pt * (dpts[u] - delta_ref[0, :, qss[u]])).astype(BF16)
                dk_sc[rows, :] += _dot(dst, qts[u])
                parts.append(_dot_tn(dst, k_ref[0, rows, :]))
            if trim:
                for u in range(ATTN_CHAINS):
                    dq_ref[0, qss[u], :] += parts[u]
            else:
                dq_ref[0, qss[0], :] += functools.reduce(lambda a, b: a + b, parts)

        q_first = (kb * tk) // tq
        q_clear = (kb * tk + tk - 1) // tq + 1
        lax.fori_loop(q_first, q_clear, lambda qb, c: (step(qb, True), c)[1], 0)
        lax.fori_loop(q_clear, nq, lambda qb, c: (step(qb, False), c)[1], 0)
        dk_ref[0] = dk_sc[...] * (1.0 / LOG2E)
        dv_ref[0] = dv_sc[...]

        @pl.when(kb == nk - 1)
        def _():
            dq_ref[...] = dq_ref[...] * SCALE

        if ng:
            @pl.when((pl.program_id(0) == nh - 1) & (kb == nk - 1))
            def _():
                for cp in copies:
                    cp.wait()

    arrays = [scatter[nm] for nm in names]
    out = pl.pallas_call(
        body, name=name, grid=(nh, nk),
        in_specs=[pl.BlockSpec((1, s, QK_PAD), lambda h, kb: (h, 0, 0)),
                  pl.BlockSpec((1, tk, QK_PAD), lambda h, kb: (h, kb, 0)),
                  pl.BlockSpec((1, tk, V_DIM), lambda h, kb: (h, kb, 0)),
                  pl.BlockSpec((s, V_DIM), lambda h, kb: (0, h)),
                  pl.BlockSpec((1, 1, s), lambda h, kb: (h, 0, 0)),
                  pl.BlockSpec((1, 1, s), lambda h, kb: (h, 0, 0)),
                  pl.BlockSpec((1, s), lambda h, kb: (0, 0)),
                  pl.BlockSpec((tk, 1), lambda h, kb: (kb, 0))] + [ANY] * ng,
        out_specs=[pl.BlockSpec((1, s, QK_PAD), lambda h, kb: (h, 0, 0)),
                   pl.BlockSpec((1, tk, QK_PAD), lambda h, kb: (h, kb, 0)),
                   pl.BlockSpec((1, tk, V_DIM), lambda h, kb: (h, kb, 0))] + [ANY] * ng,
        scratch_shapes=[pltpu.VMEM((tk, QK_PAD), F32), pltpu.VMEM((tk, V_DIM), F32)]
        + ([pltpu.SemaphoreType.DMA((3 * ng,)), pltpu.SemaphoreType.DMA((3 * ng,))] if ng else []),
        out_shape=[jax.ShapeDtypeStruct((nh, s, QK_PAD), F32), jax.ShapeDtypeStruct((nh, s, QK_PAD), F32),
                   jax.ShapeDtypeStruct((nh, s, V_DIM), F32)] + _scatter_out_shapes(names, arrays),
        compiler_params=_cparams("arbitrary", "arbitrary"),
    )(q, k, v, do, lse_row, delta_row, pos_row, pos_col, *arrays)
    return out[0], out[1], out[2], dict(zip(names, out[3:]))


def _loss_head(y, target, *, name):
    s, d = y.shape
    tm = _tile(s, TM)
    nt = s // tm

    def body(y_ref, t_ref, dy_ref, loss_ref, acc):
        i = pl.program_id(0)

        @pl.when(i == 0)
        def _():
            acc[...] = jnp.zeros_like(acc)

        e = y_ref[...] - t_ref[...]
        dy_ref[...] = e * (1.0 / d)
        acc[...] += jnp.sum((e * e).reshape(tm // 8, 8, d), axis=0)

        @pl.when(i == nt - 1)
        def _():
            loss_ref[...] = jnp.full((1, 128), 0.5 / d, F32) * jnp.sum(acc[...])

    return pl.pallas_call(
        body, name=name, grid=(nt,),
        in_specs=[pl.BlockSpec((tm, d), lambda i: (i, 0))] * 2,
        out_specs=[pl.BlockSpec((tm, d), lambda i: (i, 0)), pl.BlockSpec((1, 128), lambda i: (0, 0))],
        out_shape=[jax.ShapeDtypeStruct((s, d), F32), jax.ShapeDtypeStruct((1, 128), F32)],
        scratch_shapes=[pltpu.VMEM((8, d), F32)],
        compiler_params=_cparams("arbitrary"),
    )(y, target)


def _adamw(w, g, m, v, *, name):
    r, c = w.shape
    tr = _tile(r, 512) if r % 8 == 0 else r

    def body(w_ref, g_ref, m_ref, v_ref, d_ref, nm_ref, nv_ref):
        g_t = g_ref[...]
        nm = ADAM_B1 * m_ref[...] + (1.0 - ADAM_B1) * g_t
        nv = ADAM_B2 * v_ref[...] + (1.0 - ADAM_B2) * (g_t * g_t)
        m_hat = nm / (1.0 - ADAM_B1 ** ADAM_STEP)
        v_hat = nv / (1.0 - ADAM_B2 ** ADAM_STEP)
        d_ref[...] = -ADAM_LR * (m_hat / (jnp.sqrt(v_hat) + ADAM_EPS) + ADAM_WD * w_ref[...])
        nm_ref[...] = nm
        nv_ref[...] = nv

    spec = pl.BlockSpec((tr, c), lambda i: (i, 0))
    return pl.pallas_call(
        body, name=name, grid=(r // tr,), in_specs=[spec] * 4, out_specs=[spec] * 3,
        out_shape=[jax.ShapeDtypeStruct((r, c), F32)] * 3,
        compiler_params=_cparams("parallel"),
    )(w, g, m, v)


def _place():
    return lax.axis_index("x"), lax.axis_index("y"), lax.axis_index("c")


def _other_chips(x, y):
    return [(1 - x, y), (x, 1 - y), (1 - x, 1 - y)]


BIG = {
    "attn_w_down": ((2, 1024, 448), 1), "attn_w_uq": ((2, 256, 1536), 2), "attn_w_ukv": ((2, 128, 2048), 2),
    "attn_w_o": ((2, 1024, 1024), 1), "conv_w_in": ((2, 1024, 3072), 2),
    "conv_w_out": ((2, 1024, 1024), 1), "mlp_w1": ((4, 1024, 4096), 2), "mlp_w2": ((4, 4096, 1024), 1),
}
CONV_W = (2, 3, 1024)


def _shard_shape(name):
    shape, axis = BIG[name]
    return tuple(n // N_CHIPS if i == axis else n for i, n in enumerate(shape))


def _band(ref, name, layers, chip):
    shape, axis = BIG[name]
    width = shape[axis] // N_CHIPS
    if axis == 1:
        return ref.at[layers, pl.ds(chip * width, width), :]
    return ref.at[layers, :, pl.ds(chip * width, width)]


def _half(name, c):
    hl = BIG[name][0][0] // 2
    return pl.ds(c * hl, hl)


def _place_own(w, nm, chip, *, name):
    shape, axis = BIG[nm]
    layers, rows, cols = w.shape
    tr = _sum_rows(rows, cols)
    nrb = rows // tr
    if axis == 1:
        band = lambda l, i, ch: (l, ch[0] * nrb + i, 0)
    else:
        band = lambda l, i, ch: (l, i, ch[0])

    def body(chip_ref, w_ref, o_ref):
        o_ref[...] = w_ref[...].astype(BF16)

    return pl.pallas_call(
        body, name=name,
        grid_spec=pltpu.PrefetchScalarGridSpec(
            num_scalar_prefetch=1, grid=(layers, nrb),
            in_specs=[pl.BlockSpec((1, tr, cols), lambda l, i, ch: (l, i, 0))],
            out_specs=pl.BlockSpec((1, tr, cols), band)),
        out_shape=jax.ShapeDtypeStruct(shape, BF16),
        compiler_params=_cparams("parallel", "parallel"),
    )(chip, w)


def _gather_copies(outs, names, send_sems, recv_sems, *, base, stride, to_sibling):
    x, y, c = _place()
    me = 2 * x + y

    def copy(k, ref, nm, layers, chip, to):
        band = _band(ref, nm, layers, chip)
        return pltpu.make_async_remote_copy(
            src_ref=band, dst_ref=band, send_sem=send_sems.at[k], recv_sem=recv_sems.at[k],
            device_id=to, device_id_type=MESH)

    sends, recvs = [], []
    for i, nm in enumerate(names):
        for j, (cx, cy) in enumerate(_other_chips(x, y)):
            k = base + stride * i + j
            if to_sibling:
                sends.append(copy(k, outs[i], nm, _half(nm, c), 2 * cx + cy, (x, y, 1 - c)))
                recvs.append(copy(k, outs[i], nm, _half(nm, 1 - c), 2 * cx + cy, (x, y, c)))
            else:
                sends.append(copy(k, outs[i], nm, _half(nm, c), me, (cx, cy, c)))
                recvs.append(copy(k, outs[i], nm, _half(nm, c), 2 * cx + cy, (x, y, c)))
    return sends, recvs


def _gather_ici_copies(outs, names, send_sems, recv_sems, *, base, stride):
    return _gather_copies(outs, names, send_sems, recv_sems, base=base, stride=stride, to_sibling=False)


def _gather_weights(fulls, *, name, ici=True):
    names = list(fulls)
    n = len(names)

    def body(*refs):
        outs = refs[n:2 * n]
        sems = refs[2 * n:]
        sent = []
        if ici:
            sends, recvs = _gather_copies(outs, names, *sems, base=0, stride=6, to_sibling=False)
            for cp in sends:
                cp.start()
            for cp in recvs:
                cp.wait_recv()
            sent += sends
        sends, recvs = _gather_copies(outs, names, *sems, base=3, stride=6, to_sibling=True)
        for cp in sends:
            cp.start()
        for cp in recvs:
            cp.wait_recv()
        for cp in sent + sends:
            cp.wait_send()

    arrays = [fulls[nm] for nm in names]
    out = pl.pallas_call(
        body, name=name, in_specs=[ANY] * n, out_specs=[ANY] * n,
        out_shape=[jax.ShapeDtypeStruct(a.shape, a.dtype) for a in arrays],
        input_output_aliases={i: i for i in range(n)},
        scratch_shapes=[pltpu.SemaphoreType.DMA((6 * n,)), pltpu.SemaphoreType.DMA((6 * n,))],
    )(*arrays)
    return dict(zip(names, out))


def _swap_halves(grads, *, name):
    names = list(grads)
    n = len(names)

    def body(*refs):
        ins, outs = refs[:n], refs[n:2 * n]
        send_sems, recv_sems = refs[2 * n:]
        x, y, c = _place()
        copies = []
        for i, nm in enumerate(names):
            cp = pltpu.make_async_remote_copy(
                src_ref=ins[i].at[_half(nm, 1 - c)], dst_ref=outs[i], send_sem=send_sems.at[i],
                recv_sem=recv_sems.at[i], device_id=(x, y, 1 - c), device_id_type=MESH)
            cp.start()
            copies.append(cp)
        for cp in copies:
            cp.wait()

    arrays = [grads[nm] for nm in names]
    out = pl.pallas_call(
        body, name=name, in_specs=[ANY] * n, out_specs=[ANY] * n,
        out_shape=[jax.ShapeDtypeStruct((a.shape[0] // 2,) + a.shape[1:], a.dtype) for a in arrays],
        scratch_shapes=[pltpu.SemaphoreType.DMA((n,)), pltpu.SemaphoreType.DMA((n,))],
    )(*arrays)
    return dict(zip(names, out))


def _sum_rows(rows, cols):
    t = rows
    while t * cols * 4 > SUM_BLOCK_BYTES and t % 16 == 0:
        t //= 2
    return t


def _chip_sum(g, r1, core, *, name):
    layers, rows, cols = g.shape
    hl = layers // 2
    tr = _sum_rows(rows, cols)

    def body(core_ref, g_ref, r_ref, o_ref):
        o_ref[...] = (g_ref[...] + r_ref[...]).astype(BF16)

    return pl.pallas_call(
        body, name=name,
        grid_spec=pltpu.PrefetchScalarGridSpec(
            num_scalar_prefetch=1, grid=(hl, rows // tr),
            in_specs=[pl.BlockSpec((1, tr, cols), lambda l, i, cr: (cr[0] * hl + l, i, 0)),
                      pl.BlockSpec((1, tr, cols), lambda l, i, cr: (l, i, 0))],
            out_specs=pl.BlockSpec((1, tr, cols), lambda l, i, cr: (l, i, 0))),
        out_shape=jax.ShapeDtypeStruct((hl, rows, cols), BF16),
        compiler_params=_cparams("parallel", "parallel"),
    )(core, g, r1)


def _chip_partials(grads, names, *, tag):
    core = lax.axis_index("c").astype(jnp.int32).reshape(1)
    r1 = _swap_halves({n: grads[n] for n in names}, name=f"grad_swap_halves_{tag}")
    return r1, {n: _chip_sum(grads[n], r1[n], core, name=f"grad_chip_sum_{n}") for n in names}


def _scatter_partials(partials):
    names = list(partials)
    n = len(names)

    def body(*refs):
        copies = _scatter_copies(refs[:n], refs[n:2 * n], names, *refs[2 * n:])
        for cp in copies:
            cp.start()
        for cp in copies:
            cp.wait()

    arrays = [partials[nm] for nm in names]
    out = pl.pallas_call(
        body, name="grad_scatter_partials", in_specs=[ANY] * n, out_specs=[ANY] * n,
        out_shape=_scatter_out_shapes(names, arrays),
        scratch_shapes=[pltpu.SemaphoreType.DMA((3 * n,)), pltpu.SemaphoreType.DMA((3 * n,))],
    )(*arrays)
    return dict(zip(names, out))


def _scatter_copies(ins, outs, names, send_sems, recv_sems):
    x, y, c = _place()
    copies = []
    for i, nm in enumerate(names):
        for j, (cx, cy) in enumerate(_other_chips(x, y)):
            copies.append(pltpu.make_async_remote_copy(
                src_ref=_band(ins[i], nm, slice(None), 2 * cx + cy), dst_ref=outs[i].at[j],
                send_sem=send_sems.at[3 * i + j], recv_sem=recv_sems.at[3 * i + j],
                device_id=(cx, cy, c), device_id_type=MESH))
    return copies


def _scatter_out_shapes(names, arrays):
    return [jax.ShapeDtypeStruct((3, a.shape[0]) + _shard_shape(nm)[1:], a.dtype) for nm, a in zip(names, arrays)]


def _final_sum(g, r1, r2, place, nm, *, name):
    (layers, _, _), axis = BIG[nm]
    hl = layers // 2
    _, rows, cols = _shard_shape(nm)
    tr = _sum_rows(rows, cols)
    nrb = rows // tr
    if axis == 1:
        blk = lambda l, i, pc: (l, pc[1] * nrb + i, 0)
    else:
        blk = lambda l, i, pc: (l, i, pc[1])

    def body(place_ref, g_ref, r1_ref, r2_ref, o_ref):
        acc = g_ref[...] + r1_ref[...]
        for j in range(3):
            acc = acc + r2_ref[j].astype(F32)
        o_ref[...] = acc

    return pl.pallas_call(
        body, name=name,
        grid_spec=pltpu.PrefetchScalarGridSpec(
            num_scalar_prefetch=1, grid=(hl, nrb),
            in_specs=[pl.BlockSpec((1, tr, cols), lambda l, i, pc: blk(pc[0] * hl + l, i, pc)),
                      pl.BlockSpec((1, tr, cols), lambda l, i, pc: blk(l, i, pc)),
                      pl.BlockSpec((3, 1, tr, cols), lambda l, i, pc: (0, l, i, 0))],
            out_specs=pl.BlockSpec((1, tr, cols), lambda l, i, pc: (pc[0] * hl + l, i, 0))),
        out_shape=jax.ShapeDtypeStruct((layers, rows, cols), F32),
        compiler_params=_cparams("parallel", "parallel"),
    )(place, g, r1, r2)


def _join_halves(shards):
    names = list(shards)
    n = len(names)

    def body(*refs):
        outs = refs[n:2 * n]
        send_sems, recv_sems = refs[2 * n:]
        x, y, c = _place()
        copies = []
        for i, nm in enumerate(names):
            mine = outs[i].at[_half(nm, c)]
            cp = pltpu.make_async_remote_copy(
                src_ref=mine, dst_ref=mine, send_sem=send_sems.at[i], recv_sem=recv_sems.at[i],
                device_id=(x, y, 1 - c), device_id_type=MESH)
            cp.start()
            copies.append(cp)
        for i, nm in enumerate(names):
            theirs = outs[i].at[_half(nm, 1 - c)]
            pltpu.make_async_remote_copy(
                src_ref=theirs, dst_ref=theirs, send_sem=send_sems.at[i], recv_sem=recv_sems.at[i],
                device_id=(x, y, 1 - c), device_id_type=MESH).wait_recv()
        for cp in copies:
            cp.wait_send()

    arrays = [shards[nm] for nm in names]
    out = pl.pallas_call(
        body, name="grad_join_halves", in_specs=[ANY] * n, out_specs=[ANY] * n,
        out_shape=[jax.ShapeDtypeStruct(a.shape, a.dtype) for a in arrays],
        input_output_aliases={i: i for i in range(n)},
        scratch_shapes=[pltpu.SemaphoreType.DMA((n,)), pltpu.SemaphoreType.DMA((n,))],
    )(*arrays)
    return dict(zip(names, out))


def _all_reduce_small(part, *, name):
    rows, cols = part.shape
    vm = pl.BlockSpec(memory_space=pltpu.VMEM)

    def body(p_ref, o_ref, land, send_sems, recv_sems):
        x, y, c = _place()
        me = 4 * x + 2 * y + c
        flips = [(dx, dy, dc) for dx in (0, 1) for dy in (0, 1) for dc in (0, 1)][1:]
        copies = []
        for k, (dx, dy, dc) in enumerate(flips):
            cp = pltpu.make_async_remote_copy(
                src_ref=p_ref, dst_ref=land.at[me], send_sem=send_sems.at[k], recv_sem=recv_sems.at[k],
                device_id=(1 - x if dx else x, 1 - y if dy else y, 1 - c if dc else c), device_id_type=MESH)
            cp.start()
            copies.append(cp)
        land[me] = p_ref[...]
        for cp in copies:
            cp.wait()
        acc = land[0]
        for j in range(1, 8):
            acc = acc + land[j]
        o_ref[...] = acc

    return pl.pallas_call(
        body, name=name, in_specs=[vm], out_specs=vm,
        out_shape=jax.ShapeDtypeStruct((rows, cols), F32),
        scratch_shapes=[pltpu.VMEM((8, rows, cols), F32), pltpu.SemaphoreType.DMA((7,)), pltpu.SemaphoreType.DMA((7,))],
    )(part)


SMALL = {"g_mix": (4, 1024), "g_mlp": (4, 1024), "attn_g_q_a": (2, 256), "attn_g_kv_a": (2, 128),
         "attn_g_qnorm": (2, 192), "attn_g_knorm": (2, 192)}
SMALL_GRADS = {**SMALL, "conv_w": CONV_W}
WEIGHT_ORDER = ["g_mix", "g_mlp", "attn_w_down", "attn_g_q_a", "attn_g_kv_a", "attn_w_uq", "attn_w_ukv",
                "attn_g_qnorm", "attn_g_knorm", "attn_w_o", "conv_w_in", "conv_w", "conv_w_out", "mlp_w1", "mlp_w2"]


def _prod(shape):
    n = 1
    for v in shape:
        n *= v
    return n


def _pack_small(parts, table):
    flat = [parts[n].reshape(-1) for n in table]
    size = sum(_prod(s) for s in table.values())
    rows = -(-size // (8 * 128)) * 8
    flat.append(jnp.zeros((rows * 128 - size,), F32))
    return jnp.concatenate(flat).reshape(rows, 128)


def _unpack_small(buf, table):
    flat = buf.reshape(-1)
    out, off = {}, 0
    for n, shp in table.items():
        out[n] = flat[off:off + _prod(shp)].reshape(shp)
        off += _prod(shp)
    return out


def _rope_tables(positions):
    inv_freq = ROPE_THETA ** (-jnp.arange(0, QK_ROPE, 2, dtype=F32) / QK_ROPE)
    ang = positions.astype(F32)[:, None] * inv_freq
    cos, sin = jnp.cos(ang), jnp.sin(ang)
    z32 = jnp.zeros_like(cos)
    z64 = jnp.zeros((positions.shape[0], 64), F32)
    cc = jnp.concatenate([cos, cos, z64], axis=1)
    sa = jnp.concatenate([-sin, z32, z64], axis=1)
    sb = jnp.concatenate([z32, sin, z64], axis=1)
    return cc, sa, sb


def _pad_heads(w, width):
    k = w.shape[0]
    w = w.reshape(k, N_HEADS, width)
    return jnp.pad(w, ((0, 0), (0, 0), (0, QK_PAD - width))).reshape(k, N_HEADS * QK_PAD)


EARLY = ("mlp_w1", "mlp_w2", "conv_w_in", "conv_w_out")
LATE = ("attn_w_down", "attn_w_uq", "attn_w_ukv", "attn_w_o")
GATHER_LATER = EARLY


def _local_step(x, positions, target, wb, gains, later=None):
    s = x.shape[0]
    cc, sa, sb = _rope_tables(positions)
    pos_col = positions.reshape(s, 1)
    pos_row = positions.reshape(1, s)

    saved = []
    for i in range(4):
        j = i // 2
        g_mix = gains["g_mix"][i:i + 1]
        g_mlp = gains["g_mlp"][i:i + 1]
        if i % 2 == 0:
            w_down = jnp.pad(wb["attn_w_down"][j], ((0, 0), (0, DOWN_PAD - DOWN_DIM)))
            w_uq = _pad_heads(wb["attn_w_uq"][j], QK_DIM)
            w_ukv = wb["attn_w_ukv"][j]
            g_qa = gains["attn_g_q_a"][j:j + 1]
            g_kva = gains["attn_g_kv_a"][j:j + 1]
            g_q = jnp.pad(gains["attn_g_qnorm"][j:j + 1], ((0, 0), (0, QK_PAD - QK_DIM)))
            g_k = jnp.pad(gains["attn_g_knorm"][j:j + 1], ((0, 0), (0, QK_PAD - QK_DIM)))
            h, a = _norm_mm(x, g_mix, w_down, out_dtype=F32, name=f"mla_down_{j}")
            cq, ckv, q, k, v = _mla_prep(a, g_qa, g_kva, w_uq, w_ukv, g_q, g_k, cc, sa, sb, name=f"mla_prep_{j}")
            o, lse, got = _flash_fwd(q, k, v, pos_col, pos_row, name=f"flash_fwd_{j}",
                                     gather=later if i == 0 else None)
            if got:
                wb = {**wb, **_gather_weights(got, name="gather_later_forward", ici=False)}
            x_mid = _mm_nn(o, wb["attn_w_o"], layer=j, out_dtype=F32, residual=x, name=f"mla_out_{j}")
            mix = dict(h=h, a=a, cq=cq, ckv=ckv, q=q, k=k, v=v, o=o, lse=lse, w_down=w_down, w_uq=w_uq, w_ukv=w_ukv,
                       g_qa=g_qa, g_kva=g_kva, g_q=g_q, g_k=g_k)
        else:
            h, bcu = _norm_mm(x, g_mix, wb["conv_w_in"], layer=j, out_dtype=F32, name=f"conv_in_{j}")
            z = _conv_gate(bcu, gains["conv_w"][j], name=f"conv_gate_{j}")
            x_mid = _mm_nn(z, wb["conv_w_out"], layer=j, out_dtype=F32, residual=x, name=f"conv_out_{j}")
            mix = dict(h=h, bcu=bcu, z=z)
        h2, u, act = _mlp_up(x_mid, g_mlp, wb["mlp_w1"], layer=i, name=f"mlp_up_{i}")
        x_out = _mm_nn(act, wb["mlp_w2"], layer=i, out_dtype=F32, residual=x_mid, name=f"mlp_down_{i}")
        saved.append(dict(x_in=x, x_mid=x_mid, mix=mix, h2=h2, u=u, act=act, g_mix=g_mix, g_mlp=g_mlp))
        x = x_out

    dx, loss = _loss_head(x, target, name="loss_head")

    gw = {n: None for n in BIG}
    exchanged = None
    g_uq = [None, None]
    gs = {n: [None] * SMALL_GRADS[n][0] for n in SMALL_GRADS}

    def wgrad(nm, layer, a, b, **kw):
        gw[nm] = _mm_tn(a, b, stack=gw[nm], layer=layer, layers=BIG[nm][0][0], name=f"{nm}_grad_{layer}", **kw)

    for i in reversed(range(4)):
        j = i // 2
        sv = saved[i]
        mix = sv["mix"]
        du = _mlp_down_bwd(dx, wb["mlp_w2"], sv["u"], layer=i, name=f"mlp_down_bwd_{i}")
        wgrad("mlp_w2", i, sv["act"], dx)
        wgrad("mlp_w1", i, sv["h2"], du)
        dx, dg = _nt_rms_bwd(du, wb["mlp_w1"], sv["x_mid"], sv["g_mlp"], dx, layer=i, name=f"mlp_up_bwd_{i}")
        gs["g_mlp"][i] = dg[0]
        if i % 2 == 0:
            do = _mm_nt(dx, wb["attn_w_o"], layer=j, out_dtype=BF16, name=f"mla_out_bwd_{j}")
            wgrad("attn_w_o", j, mix["o"], dx)
            delta = _attn_delta(do, mix["o"], name=f"attn_delta_{j}")
            lse_row = mix["lse"].reshape(N_HEADS, 1, s)
            delta_row = delta.reshape(N_HEADS, 1, s)
            partials = None
            if i == 0 and later is not None:
                r1_early, partials = _chip_partials(gw, EARLY, tag="early")
            dq, dk, dv, arrived = _flash_bwd(mix["q"], mix["k"], mix["v"], do, lse_row, delta_row, pos_col, pos_row,
                                             name=f"flash_bwd_{j}", scatter=partials)
            if partials is not None:
                exchanged = (r1_early, arrived)
            dqr, dkvr, da, dgq, dgk, dgqa, dgkva = _mla_prep_bwd(
                mix["a"], mix["g_qa"], mix["g_kva"], mix["w_uq"], mix["w_ukv"], mix["g_q"], mix["g_k"], cc, sa, sb,
                dq, dk, dv, name=f"mla_prep_bwd_{j}")
            g_uq[j] = _mm_tn(mix["cq"], dqr, name=f"attn_w_uq_grad_{j}")[0]
            wgrad("attn_w_ukv", j, mix["ckv"], dkvr)
            wgrad("attn_w_down", j, mix["h"], da, keep=DOWN_DIM)
            dx, dg = _nt_rms_bwd(da, mix["w_down"], sv["x_in"], sv["g_mix"], dx, name=f"mla_down_bwd_{j}")
            gs["attn_g_qnorm"][j] = dgq[0, :QK_DIM]
            gs["attn_g_knorm"][j] = dgk[0, :QK_DIM]
            gs["attn_g_q_a"][j] = dgqa[0]
            gs["attn_g_kv_a"][j] = dgkva[0]
        else:
            dz = _mm_nt(dx, wb["conv_w_out"], layer=j, out_dtype=F32, name=f"conv_out_bwd_{j}")
            wgrad("conv_w_out", j, mix["z"], dx)
            dbcu, dcw = _conv_gate_bwd(mix["bcu"], dz, gains["conv_w"][j], name=f"conv_gate_bwd_{j}")
            gs["conv_w"][j] = dcw
            wgrad("conv_w_in", j, mix["h"], dbcu)
            dx, dg = _nt_rms_bwd(dbcu, wb["conv_w_in"], sv["x_in"], sv["g_mix"], dx, layer=j, name=f"conv_in_bwd_{j}")
        gs["g_mix"][i] = dg[0]

    gw["attn_w_uq"] = jnp.stack(g_uq).reshape(2, Q_RANK, N_HEADS, QK_PAD)[..., :QK_DIM].reshape(BIG["attn_w_uq"][0])
    grads_small = {n: jnp.stack(v) for n, v in gs.items()}
    return loss, dx, gw, grads_small, exchanged


def kernel(x, positions, g_mix, g_mlp, attn_w_down, attn_g_q_a, attn_g_kv_a, attn_w_uq, attn_w_ukv, attn_g_qnorm, attn_g_knorm, attn_w_o, conv_w_in, conv_w, conv_w_out, mlp_w1, mlp_w2, loss_target, m_g_mix, m_g_mlp, m_attn_w_down, m_attn_g_q_a, m_attn_g_kv_a, m_attn_w_uq, m_attn_w_ukv, m_attn_g_qnorm, m_attn_g_knorm, m_attn_w_o, m_conv_w_in, m_conv_w, m_conv_w_out, m_mlp_w1, m_mlp_w2, v_g_mix, v_g_mlp, v_attn_w_down, v_attn_g_q_a, v_attn_g_kv_a, v_attn_w_uq, v_attn_w_ukv, v_attn_g_qnorm, v_attn_g_knorm, v_attn_w_o, v_conv_w_in, v_conv_w, v_conv_w_out, v_mlp_w1, v_mlp_w2):
    w = dict(g_mix=g_mix, g_mlp=g_mlp, attn_w_down=attn_w_down, attn_g_q_a=attn_g_q_a, attn_g_kv_a=attn_g_kv_a,
             attn_w_uq=attn_w_uq, attn_w_ukv=attn_w_ukv, attn_g_qnorm=attn_g_qnorm, attn_g_knorm=attn_g_knorm,
             attn_w_o=attn_w_o, conv_w_in=conv_w_in, conv_w=conv_w, conv_w_out=conv_w_out, mlp_w1=mlp_w1, mlp_w2=mlp_w2)
    m = dict(g_mix=m_g_mix, g_mlp=m_g_mlp, attn_w_down=m_attn_w_down, attn_g_q_a=m_attn_g_q_a,
             attn_g_kv_a=m_attn_g_kv_a, attn_w_uq=m_attn_w_uq, attn_w_ukv=m_attn_w_ukv, attn_g_qnorm=m_attn_g_qnorm,
             attn_g_knorm=m_attn_g_knorm, attn_w_o=m_attn_w_o, conv_w_in=m_conv_w_in, conv_w=m_conv_w,
             conv_w_out=m_conv_w_out, mlp_w1=m_mlp_w1, mlp_w2=m_mlp_w2)
    v = dict(g_mix=v_g_mix, g_mlp=v_g_mlp, attn_w_down=v_attn_w_down, attn_g_q_a=v_attn_g_q_a,
             attn_g_kv_a=v_attn_g_kv_a, attn_w_uq=v_attn_w_uq, attn_w_ukv=v_attn_w_ukv, attn_g_qnorm=v_attn_g_qnorm,
             attn_g_knorm=v_attn_g_knorm, attn_w_o=v_attn_w_o, conv_w_in=v_conv_w_in, conv_w=v_conv_w,
             conv_w_out=v_conv_w_out, mlp_w1=v_mlp_w1, mlp_w2=v_mlp_w2)
    cx, cy, cc_ = _place()

    chip = 2 * cx + cy

    def own_offset(shape, axis):
        return tuple(chip * (shape[axis] // N_CHIPS) if i == axis else 0 for i in range(3))

    chip_arr = chip.astype(jnp.int32).reshape(1)
    fulls = {n: _place_own(w[n], n, chip_arr, name=f"place_{n}") for n in BIG}
    later = {n: fulls.pop(n) for n in GATHER_LATER}
    wb = _gather_weights(fulls, name="gather_weights")

    placed = lax.dynamic_update_slice(jnp.zeros(CONV_W, F32), conv_w, own_offset(CONV_W, 2))
    conv_w_full = 0.5 * _all_reduce_small(placed.reshape(-1, 128), name="conv_w_gather").reshape(CONV_W)

    gains = {n: w[n] for n in SMALL}
    gains["conv_w"] = conv_w_full

    loss, grad_x, grads_big, grads_small, (r1_early, r2_early) = _local_step(
        x[0], positions[0], loss_target[0], wb, gains, later)

    place = jnp.stack([cc_, chip]).astype(jnp.int32)
    r1_late, partials = _chip_partials(grads_big, LATE, tag="late")
    r1 = {**r1_early, **r1_late}
    r2 = {**r2_early, **_scatter_partials(partials)}
    halves = {n: _final_sum(grads_big[n], r1[n], r2[n], place, n, name=f"grad_final_sum_{n}") for n in BIG}
    grad_shards = _join_halves(halves)

    small = _unpack_small(_all_reduce_small(_pack_small(grads_small, SMALL_GRADS), name="gain_all_reduce"), SMALL_GRADS)
    grad_shards["conv_w"] = lax.dynamic_slice(small["conv_w"], own_offset(CONV_W, 2), conv_w.shape)

    loss_total = lax.psum(loss[0, 0], ("x", "y", "c"))

    grads, deltas, new_m, new_v = {}, {}, {}, {}
    for n in [*BIG, "conv_w"]:
        shp = w[n].shape
        two_d = (shp[0] * shp[1], shp[2])
        g2 = grad_shards[n].reshape(two_d)
        d, nm, nv = _adamw(w[n].reshape(two_d), g2, m[n].reshape(two_d), v[n].reshape(two_d), name=f"adamw_{n}")
        grads[n], deltas[n], new_m[n], new_v[n] = grad_shards[n], d.reshape(shp), nm.reshape(shp), nv.reshape(shp)
    d, nm, nv = _adamw(_pack_small(w, SMALL), _pack_small(small, SMALL), _pack_small(m, SMALL), _pack_small(v, SMALL),
                       name="adamw_gains")
    d, nm, nv = _unpack_small(d, SMALL), _unpack_small(nm, SMALL), _unpack_small(nv, SMALL)
    for n in SMALL:
        grads[n], deltas[n], new_m[n], new_v[n] = small[n], d[n], nm[n], nv[n]

    return (loss_total, grad_x[None],
            *[grads[n] for n in WEIGHT_ORDER], *[deltas[n] for n in WEIGHT_ORDER],
            *[new_m[n] for n in WEIGHT_ORDER], *[new_v[n] for n in WEIGHT_ORDER])
```

```python
import functools

import jax
import jax.numpy as jnp
from jax import lax
from jax.experimental import pallas as pl
from jax.experimental.pallas import tpu as pltpu

F32 = jnp.float32
BF16 = jnp.bfloat16

D_MODEL = 1024
N_HEADS = 8
QK_NOPE = 128
QK_ROPE = 64
QK_DIM = QK_NOPE + QK_ROPE
QK_PAD = 256
V_DIM = 128
Q_RANK = 256
KV_RANK = 128
DOWN_DIM = Q_RANK + KV_RANK + QK_ROPE
DOWN_PAD = 512
D_FF = 4 * D_MODEL
ROPE_THETA = 10000.0
EPS = 1e-6
NEG = -1e30
SCALE = QK_DIM ** -0.5
SCALE_LOG2E = SCALE * 1.4426950408889634
LOG2E = 1.4426950408889634
ATTN_CHAINS = 2

ADAM_LR = 0.001
ADAM_B1 = 0.9
ADAM_B2 = 0.999
ADAM_EPS = 1e-08
ADAM_WD = 0.01
ADAM_STEP = 10

N_CHIPS = 4
MESH = pl.DeviceIdType.MESH
ANY = pl.BlockSpec(memory_space=pl.ANY)

TM = 512
TM_WIDE = 512
FWD_TQ = 1024
FWD_TK = 1024
BWD_TQ = 1024
BWD_TK = 1024
HALO = 16
T_PREP = 256
T_RED = 1024
SUM_BLOCK_BYTES = 2 * 1024 * 1024


def _tile(n, pref):
    t = min(n, pref)
    assert n % t == 0, (n, t)
    return t


def _cparams(*sem):
    return pltpu.CompilerParams(dimension_semantics=sem)


def _dot(a, b):
    return jnp.dot(a, b, preferred_element_type=F32)


def _dot_nt(a, b):
    return lax.dot_general(a, b, (((1,), (1,)), ((), ())), preferred_element_type=F32)


def _dot_tn(a, b):
    return lax.dot_general(a, b, (((0,), (0,)), ((), ())), preferred_element_type=F32)


def _rms(x, width):
    r = lax.rsqrt(jnp.sum(x * x, axis=-1, keepdims=True) * (1.0 / width) + EPS)
    return x * r, r


def _rms_bwd(xhat, r, dxhat, width):
    return r * (dxhat - xhat * (jnp.sum(dxhat * xhat, axis=-1, keepdims=True) * (1.0 / width)))


def _rope(t, cc, sa, sb):
    return t * cc + pltpu.roll(t, 96, 1) * sa + pltpu.roll(t, 32, 1) * sb


def _rope_t(g, cc, sa, sb):
    return g * cc + pltpu.roll(g * sa, 32, 1) + pltpu.roll(g * sb, 96, 1)


def _wspec(w, layer):
    once = pl.Buffered(1)
    if w.ndim == 2:
        return pl.BlockSpec(w.shape, lambda *_: (0, 0), pipeline_mode=once)
    return pl.BlockSpec((None,) + w.shape[1:], lambda *_: (layer, 0, 0), pipeline_mode=once)


def _mm_nn(a, b, *, out_dtype, name, residual=None, layer=0):
    m, k = a.shape
    n = b.shape[-1]
    tm = _tile(m, TM)

    def body(*refs):
        if residual is None:
            a_ref, b_ref, o_ref = refs
        else:
            a_ref, b_ref, r_ref, o_ref = refs
        acc = _dot(a_ref[...].astype(BF16), b_ref[...])
        if residual is not None:
            acc = acc + r_ref[...]
        o_ref[...] = acc.astype(o_ref.dtype)

    in_specs = [pl.BlockSpec((tm, k), lambda i: (i, 0)), _wspec(b, layer)]
    args = [a, b]
    if residual is not None:
        in_specs.append(pl.BlockSpec((tm, n), lambda i: (i, 0)))
        args.append(residual)
    return pl.pallas_call(
        body, name=name, grid=(m // tm,), in_specs=in_specs,
        out_specs=pl.BlockSpec((tm, n), lambda i: (i, 0)),
        out_shape=jax.ShapeDtypeStruct((m, n), out_dtype),
        compiler_params=_cparams("parallel"),
    )(*args)


def _mm_nt(a, b, *, out_dtype, name, layer=0):
    m, k = a.shape
    n = b.shape[-2]
    tm = _tile(m, TM)

    def body(a_ref, b_ref, o_ref):
        o_ref[...] = _dot_nt(a_ref[...].astype(BF16), b_ref[...]).astype(o_ref.dtype)

    return pl.pallas_call(
        body, name=name, grid=(m // tm,),
        in_specs=[pl.BlockSpec((tm, k), lambda i: (i, 0)), _wspec(b, layer)],
        out_specs=pl.BlockSpec((tm, n), lambda i: (i, 0)),
        out_shape=jax.ShapeDtypeStruct((m, n), out_dtype),
        compiler_params=_cparams("parallel"),
    )(a, b)


def _mm_tn(a, b, *, name, stack=None, layer=0, layers=1, keep=None, sqrelu_a=False):
    s, ka = a.shape
    n = b.shape[1]
    ts = _tile(s, T_RED)
    tka = _tile(ka, 1024)
    tn = _tile(n, 1024)
    n_out = n if keep is None else keep
    assert keep is None or tn == n

    def body(a_ref, b_ref, *rest):
        o_ref = rest[-1]

        @pl.when(pl.program_id(2) == 0)
        def _():
            o_ref[...] = jnp.zeros_like(o_ref)

        a_t = a_ref[...]
        if sqrelu_a:
            a_t = _sqrelu(a_t.astype(F32))
        o_ref[...] += _dot_tn(a_t.astype(BF16), b_ref[...].astype(BF16))[:, :n_out if keep else tn]

    in_specs = [pl.BlockSpec((ts, tka), lambda i, j, t: (t, i)), pl.BlockSpec((ts, tn), lambda i, j, t: (t, j))]
    args = [a, b]
    if stack is not None:
        in_specs.append(ANY)
        args.append(stack)
    return pl.pallas_call(
        body, name=name, grid=(ka // tka, n // tn, s // ts), in_specs=in_specs,
        out_specs=pl.BlockSpec((None, tka, tn if keep is None else keep), lambda i, j, t: (layer, i, j)),
        out_shape=jax.ShapeDtypeStruct((layers, ka, n_out), F32),
        input_output_aliases={} if stack is None else {2: 0},
        compiler_params=_cparams("parallel", "parallel", "arbitrary"),
    )(*args)


def _norm_mm(x, g, w, *, out_dtype, name, layer=0):
    s, d = x.shape
    n = w.shape[-1]
    tm = _tile(s, TM)

    def body(x_ref, g_ref, w_ref, h_ref, o_ref):
        xhat, _ = _rms(x_ref[...], d)
        h = (xhat * g_ref[...]).astype(BF16)
        h_ref[...] = h
        o_ref[...] = _dot(h, w_ref[...]).astype(o_ref.dtype)

    return pl.pallas_call(
        body, name=name, grid=(s // tm,),
        in_specs=[pl.BlockSpec((tm, d), lambda i: (i, 0)), pl.BlockSpec((1, d), lambda i: (0, 0)), _wspec(w, layer)],
        out_specs=[pl.BlockSpec((tm, d), lambda i: (i, 0)), pl.BlockSpec((tm, n), lambda i: (i, 0))],
        out_shape=[jax.ShapeDtypeStruct((s, d), BF16), jax.ShapeDtypeStruct((s, n), out_dtype)],
        compiler_params=_cparams("parallel"),
    )(x, g, w)


def _nt_rms_bwd(dy, w, x, g, dres, *, name, layer=0):
    s, n = dy.shape
    d = x.shape[1]
    tm = _tile(s, TM)

    def body(dy_ref, w_ref, x_ref, g_ref, dres_ref, dx_ref, dg_ref):
        @pl.when(pl.program_id(0) == 0)
        def _():
            dg_ref[...] = jnp.zeros_like(dg_ref)

        dh = _dot_nt(dy_ref[...], w_ref[...])
        xhat, r = _rms(x_ref[...], d)
        dg_ref[...] += jnp.sum(dh * xhat, axis=0, keepdims=True)
        dx_ref[...] = dres_ref[...] + _rms_bwd(xhat, r, dh * g_ref[...], d)

    return pl.pallas_call(
        body, name=name, grid=(s // tm,),
        in_specs=[pl.BlockSpec((tm, n), lambda i: (i, 0)), _wspec(w, layer),
                  pl.BlockSpec((tm, d), lambda i: (i, 0)), pl.BlockSpec((1, d), lambda i: (0, 0)),
                  pl.BlockSpec((tm, d), lambda i: (i, 0))],
        out_specs=[pl.BlockSpec((tm, d), lambda i: (i, 0)), pl.BlockSpec((1, d), lambda i: (0, 0))],
        out_shape=[jax.ShapeDtypeStruct((s, d), F32), jax.ShapeDtypeStruct((1, d), F32)],
        compiler_params=_cparams("arbitrary"),
    )(dy, w, x, g, dres)


def _sqrelu(u):
    return jnp.square(jnp.maximum(u, 0.0))


def _mlp_fwd(x, g, w1, w2, *, name, layer=0):
    s, d = x.shape
    n = w1.shape[-1]
    tm = _tile(s, TM_WIDE)

    def body(x_ref, g_ref, w1_ref, w2_ref, h_ref, u_ref, y_ref):
        x_t = x_ref[...]
        xhat, _ = _rms(x_t, d)
        h = (xhat * g_ref[...]).astype(BF16)
        h_ref[...] = h
        u = _dot(h, w1_ref[...])
        u_ref[...] = u.astype(BF16)
        y_ref[...] = x_t + _dot(_sqrelu(u).astype(BF16), w2_ref[...])

    return pl.pallas_call(
        body, name=name, grid=(s // tm,),
        in_specs=[pl.BlockSpec((tm, d), lambda i: (i, 0)), pl.BlockSpec((1, d), lambda i: (0, 0)),
                  _wspec(w1, layer), _wspec(w2, layer)],
        out_specs=[pl.BlockSpec((tm, d), lambda i: (i, 0)), pl.BlockSpec((tm, n), lambda i: (i, 0)),
                   pl.BlockSpec((tm, d), lambda i: (i, 0))],
        out_shape=[jax.ShapeDtypeStruct((s, d), BF16), jax.ShapeDtypeStruct((s, n), BF16),
                   jax.ShapeDtypeStruct((s, d), F32)],
        compiler_params=_cparams("parallel"),
    )(x, g, w1, w2)


def _mlp_down_bwd(dy, w2, u, *, name, layer=0):
    s, d = dy.shape
    n = w2.shape[-2]
    tm = _tile(s, TM_WIDE)

    def body(dy_ref, w_ref, u_ref, du_ref):
        dact = _dot_nt(dy_ref[...].astype(BF16), w_ref[...])
        du_ref[...] = (dact * (2.0 * jnp.maximum(u_ref[...].astype(F32), 0.0))).astype(BF16)

    return pl.pallas_call(
        body, name=name, grid=(s // tm,),
        in_specs=[pl.BlockSpec((tm, d), lambda i: (i, 0)), _wspec(w2, layer),
                  pl.BlockSpec((tm, n), lambda i: (i, 0))],
        out_specs=pl.BlockSpec((tm, n), lambda i: (i, 0)),
        out_shape=jax.ShapeDtypeStruct((s, n), BF16),
        compiler_params=_cparams("parallel"),
    )(dy, w2, u)


def _conv_gate(bcu, conv_w, *, name):
    s = bcu.shape[0]
    d = D_MODEL
    tm = _tile(s, TM)
    hb = tm // HALO

    def body(bcu_ref, prev_ref, w_ref, z_ref, pbuf):
        i = pl.program_id(0)
        gb = bcu_ref[:, 0:d].astype(F32)
        p = bcu_ref[:, d:2 * d].astype(F32) * bcu_ref[:, 2 * d:3 * d].astype(F32)
        pprev = prev_ref[:, d:2 * d].astype(F32) * prev_ref[:, 2 * d:3 * d].astype(F32)
        pbuf[0:HALO, :] = jnp.where(i > 0, pprev, 0.0)
        pbuf[HALO:HALO + tm, :] = p
        cv = (w_ref[2:3, :] * p + w_ref[1:2, :] * pbuf[HALO - 1:HALO - 1 + tm, :]
              + w_ref[0:1, :] * pbuf[HALO - 2:HALO - 2 + tm, :])
        z_ref[...] = (gb * cv).astype(BF16)

    return pl.pallas_call(
        body, name=name, grid=(s // tm,),
        in_specs=[pl.BlockSpec((tm, 3 * d), lambda i: (i, 0)),
                  pl.BlockSpec((HALO, 3 * d), lambda i: (jnp.maximum(i * hb - 1, 0), 0)),
                  pl.BlockSpec((3, d), lambda i: (0, 0))],
        out_specs=pl.BlockSpec((tm, d), lambda i: (i, 0)),
        out_shape=jax.ShapeDtypeStruct((s, d), BF16),
        scratch_shapes=[pltpu.VMEM((tm + HALO, d), F32)],
        compiler_params=_cparams("parallel"),
    )(bcu, bcu, conv_w)


def _conv_gate_bwd(bcu, dz, conv_w, *, name):
    s = bcu.shape[0]
    d = D_MODEL
    tm = _tile(s, TM)
    hb = tm // HALO
    nt = s // tm

    def body(bcu_ref, prev_ref, next_ref, dz_ref, dznext_ref, w_ref, dbcu_ref, dw_ref, pbuf, dbuf):
        i = pl.program_id(0)

        @pl.when(i == 0)
        def _():
            dw_ref[...] = jnp.zeros_like(dw_ref)

        gb = bcu_ref[:, 0:d].astype(F32)
        gc = bcu_ref[:, d:2 * d].astype(F32)
        uu = bcu_ref[:, 2 * d:3 * d].astype(F32)
        p = gc * uu
        pprev = prev_ref[:, d:2 * d].astype(F32) * prev_ref[:, 2 * d:3 * d].astype(F32)
        pbuf[0:HALO, :] = jnp.where(i > 0, pprev, 0.0)
        pbuf[HALO:HALO + tm, :] = p
        p1 = pbuf[HALO - 1:HALO - 1 + tm, :]
        p2 = pbuf[HALO - 2:HALO - 2 + tm, :]
        cv = w_ref[2:3, :] * p + w_ref[1:2, :] * p1 + w_ref[0:1, :] * p2
        dz_t = dz_ref[...]
        dcv = dz_t * gb
        dcv_next = dznext_ref[...] * next_ref[:, 0:d].astype(F32)
        dbuf[0:tm, :] = dcv
        dbuf[tm:tm + HALO, :] = jnp.where(i < nt - 1, dcv_next, 0.0)
        dp = w_ref[2:3, :] * dcv + w_ref[1:2, :] * dbuf[1:1 + tm, :] + w_ref[0:1, :] * dbuf[2:2 + tm, :]
        dw_ref[2:3, :] += jnp.sum(dcv * p, axis=0, keepdims=True)
        dw_ref[1:2, :] += jnp.sum(dcv * p1, axis=0, keepdims=True)
        dw_ref[0:1, :] += jnp.sum(dcv * p2, axis=0, keepdims=True)
        dbcu_ref[:, 0:d] = (dz_t * cv).astype(BF16)
        dbcu_ref[:, d:2 * d] = (dp * uu).astype(BF16)
        dbcu_ref[:, 2 * d:3 * d] = (dp * gc).astype(BF16)

    nxt = lambda i: (jnp.minimum((i + 1) * hb, s // HALO - 1), 0)
    return pl.pallas_call(
        body, name=name, grid=(nt,),
        in_specs=[pl.BlockSpec((tm, 3 * d), lambda i: (i, 0)),
                  pl.BlockSpec((HALO, 3 * d), lambda i: (jnp.maximum(i * hb - 1, 0), 0)),
                  pl.BlockSpec((HALO, 3 * d), nxt),
                  pl.BlockSpec((tm, d), lambda i: (i, 0)),
                  pl.BlockSpec((HALO, d), nxt),
                  pl.BlockSpec((3, d), lambda i: (0, 0))],
        out_specs=[pl.BlockSpec((tm, 3 * d), lambda i: (i, 0)), pl.BlockSpec((3, d), lambda i: (0, 0))],
        out_shape=[jax.ShapeDtypeStruct((s, 3 * d), BF16), jax.ShapeDtypeStruct((3, d), F32)],
        scratch_shapes=[pltpu.VMEM((tm + HALO, d), F32), pltpu.VMEM((tm + HALO, d), F32)],
        compiler_params=_cparams("arbitrary"),
    )(bcu, bcu, bcu, dz, dz, conv_w)


def _mla_prep(a, g_qa, g_kva, w_uq, w_ukv, g_q, g_k, cc, sa, sb, *, name):
    s = a.shape[0]
    ts = _tile(s, T_PREP)

    def body(a_ref, gqa_ref, gkva_ref, wuq_ref, wukv_ref, gq_ref, gk_ref, cc_ref, sa_ref, sb_ref,
             cq_ref, ckv_ref, q_ref, k_ref, v_ref):
        xq, _ = _rms(a_ref[:, 0:Q_RANK], Q_RANK)
        cq = (xq * gqa_ref[...]).astype(BF16)
        cq_ref[...] = cq
        xkv, _ = _rms(a_ref[:, Q_RANK:Q_RANK + KV_RANK], KV_RANK)
        ckv = (xkv * gkva_ref[...]).astype(BF16)
        ckv_ref[...] = ckv
        kpe = a_ref[:, Q_RANK + KV_RANK:DOWN_PAD]
        kpe_ss = jnp.sum(kpe * kpe, axis=-1, keepdims=True)
        cc_t, sa_t, sb_t = cc_ref[...], sa_ref[...], sb_ref[...]
        gq = gq_ref[...]
        gk = gk_ref[...]
        for h in range(N_HEADS):
            cols = slice(h * QK_PAD, (h + 1) * QK_PAD)
            qhat, _ = _rms(_dot(cq, wuq_ref[:, cols]), QK_DIM)
            qn = qhat * (gq * SCALE_LOG2E)
            q_ref[h, :, 0:QK_NOPE] = qn[:, 0:QK_NOPE].astype(BF16)
            q_ref[h, :, QK_NOPE:QK_PAD] = _rope(qn[:, QK_NOPE:QK_PAD], cc_t, sa_t, sb_t).astype(BF16)
            kvr = _dot(ckv, wukv_ref[:, cols])
            kn = kvr[:, 0:QK_NOPE]
            rk = lax.rsqrt((jnp.sum(kn * kn, axis=-1, keepdims=True) + kpe_ss) * (1.0 / QK_DIM) + EPS)
            k_ref[h, :, 0:QK_NOPE] = (kn * rk * gk[:, 0:QK_NOPE]).astype(BF16)
            k_ref[h, :, QK_NOPE:QK_PAD] = _rope(kpe * rk * gk[:, QK_NOPE:QK_PAD], cc_t, sa_t, sb_t).astype(BF16)
            v_ref[h, :, 0:V_DIM] = kvr[:, QK_NOPE:QK_PAD].astype(BF16)
            v_ref[h, :, V_DIM:2 * V_DIM] = jnp.ones((ts, V_DIM), BF16)

    row = lambda i: (i, 0)
    fixed = lambda i: (0, 0)
    head = lambda i: (0, i, 0)
    return pl.pallas_call(
        body, name=name, grid=(s // ts,),
        in_specs=[pl.BlockSpec((ts, DOWN_PAD), row), pl.BlockSpec((1, Q_RANK), fixed), pl.BlockSpec((1, KV_RANK), fixed),
                  pl.BlockSpec((Q_RANK, N_HEADS * QK_PAD), fixed), pl.BlockSpec((KV_RANK, N_HEADS * QK_PAD), fixed),
                  pl.BlockSpec((1, QK_PAD), fixed), pl.BlockSpec((1, QK_PAD), fixed),
                  pl.BlockSpec((ts, 128), row), pl.BlockSpec((ts, 128), row), pl.BlockSpec((ts, 128), row)],
        out_specs=[pl.BlockSpec((ts, Q_RANK), row), pl.BlockSpec((ts, KV_RANK), row),
                   pl.BlockSpec((N_HEADS, ts, QK_PAD), head), pl.BlockSpec((N_HEADS, ts, QK_PAD), head),
                   pl.BlockSpec((N_HEADS, ts, 2 * V_DIM), head)],
        out_shape=[jax.ShapeDtypeStruct((s, Q_RANK), BF16), jax.ShapeDtypeStruct((s, KV_RANK), BF16),
                   jax.ShapeDtypeStruct((N_HEADS, s, QK_PAD), BF16), jax.ShapeDtypeStruct((N_HEADS, s, QK_PAD), BF16),
                   jax.ShapeDtypeStruct((N_HEADS, s, 2 * V_DIM), BF16)],
        compiler_params=_cparams("parallel"),
    )(a, g_qa, g_kva, w_uq, w_ukv, g_q, g_k, cc, sa, sb)


def _mla_prep_bwd(a, g_qa, g_kva, w_uq, w_ukv, g_q, g_k, cc, sa, sb, dq, dk, dv, *, name):
    s = a.shape[0]
    ts = _tile(s, T_PREP)

    def body(a_ref, gqa_ref, gkva_ref, wuq_ref, wukv_ref, gq_ref, gk_ref, cc_ref, sa_ref, sb_ref,
             dq_ref, dk_ref, dv_ref, dqr_ref, dkvr_ref, da_ref, dgq_ref, dgk_ref, dgqa_ref, dgkva_ref):
        @pl.when(pl.program_id(0) == 0)
        def _():
            dgq_ref[...] = jnp.zeros_like(dgq_ref)
            dgk_ref[...] = jnp.zeros_like(dgk_ref)
            dgqa_ref[...] = jnp.zeros_like(dgqa_ref)
            dgkva_ref[...] = jnp.zeros_like(dgkva_ref)

        xq, r_q = _rms(a_ref[:, 0:Q_RANK], Q_RANK)
        cq = (xq * gqa_ref[...]).astype(BF16)
        xkv, r_kv = _rms(a_ref[:, Q_RANK:Q_RANK + KV_RANK], KV_RANK)
        ckv = (xkv * gkva_ref[...]).astype(BF16)
        kpe = a_ref[:, Q_RANK + KV_RANK:DOWN_PAD]
        kpe_ss = jnp.sum(kpe * kpe, axis=-1, keepdims=True)
        cc_t, sa_t, sb_t = cc_ref[...], sa_ref[...], sb_ref[...]
        gq = gq_ref[...]
        gk = gk_ref[...]
        dcq = jnp.zeros((ts, Q_RANK), F32)
        dckv = jnp.zeros((ts, KV_RANK), F32)
        dkpe = jnp.zeros((ts, 128), F32)
        dgq = jnp.zeros((1, QK_PAD), F32)
        dgk_n = jnp.zeros((1, QK_NOPE), F32)
        dgk_p = jnp.zeros((1, 128), F32)
        for h in range(N_HEADS):
            cols = slice(h * QK_PAD, (h + 1) * QK_PAD)
            qhat, rq = _rms(_dot(cq, wuq_ref[:, cols]), QK_DIM)
            dqn = jnp.concatenate(
                [dq_ref[h, :, 0:QK_NOPE], _rope_t(dq_ref[h, :, QK_NOPE:QK_PAD], cc_t, sa_t, sb_t)], axis=1)
            dgq = dgq + jnp.sum(dqn * qhat, axis=0, keepdims=True)
            dqr = _rms_bwd(qhat, rq, dqn * gq, QK_DIM).astype(BF16)
            dqr_ref[:, cols] = dqr
            dcq = dcq + _dot_nt(dqr, wuq_ref[:, cols])
            kn = _dot(ckv, wukv_ref[:, h * QK_PAD:h * QK_PAD + QK_NOPE])
            rk = lax.rsqrt((jnp.sum(kn * kn, axis=-1, keepdims=True) + kpe_ss) * (1.0 / QK_DIM) + EPS)
            khat_n = kn * rk
            khat_p = kpe * rk
            dkn = dk_ref[h, :, 0:QK_NOPE]
            dkp = _rope_t(dk_ref[h, :, QK_NOPE:QK_PAD], cc_t, sa_t, sb_t)
            dgk_n = dgk_n + jnp.sum(dkn * khat_n, axis=0, keepdims=True)
            dgk_p = dgk_p + jnp.sum(dkp * khat_p, axis=0, keepdims=True)
            dxn = dkn * gk[:, 0:QK_NOPE]
            dxp = dkp * gk[:, QK_NOPE:QK_PAD]
            mean = (jnp.sum(dxn * khat_n, axis=-1, keepdims=True)
                    + jnp.sum(dxp * khat_p, axis=-1, keepdims=True)) * (1.0 / QK_DIM)
            dkpe = dkpe + rk * (dxp - khat_p * mean)
            dkvr = jnp.concatenate([rk * (dxn - khat_n * mean), dv_ref[h, :, :]], axis=1).astype(BF16)
            dkvr_ref[:, cols] = dkvr
            dckv = dckv + _dot_nt(dkvr, wukv_ref[:, cols])
        dgq_ref[...] += dgq
        dgk_ref[:, 0:QK_NOPE] += dgk_n
        dgk_ref[:, QK_NOPE:QK_PAD] += dgk_p
        dgqa_ref[...] += jnp.sum(dcq * xq, axis=0, keepdims=True)
        dgkva_ref[...] += jnp.sum(dckv * xkv, axis=0, keepdims=True)
        da_ref[:, 0:Q_RANK] = _rms_bwd(xq, r_q, dcq * gqa_ref[...], Q_RANK).astype(BF16)
        da_ref[:, Q_RANK:Q_RANK + KV_RANK] = _rms_bwd(xkv, r_kv, dckv * gkva_ref[...], KV_RANK).astype(BF16)
        da_ref[:, Q_RANK + KV_RANK:DOWN_PAD] = dkpe.astype(BF16)

    row = lambda i: (i, 0)
    fixed = lambda i: (0, 0)
    head = lambda i: (0, i, 0)
    wide = N_HEADS * QK_PAD
    return pl.pallas_call(
        body, name=name, grid=(s // ts,),
        in_specs=[pl.BlockSpec((ts, DOWN_PAD), row), pl.BlockSpec((1, Q_RANK), fixed), pl.BlockSpec((1, KV_RANK), fixed),
                  pl.BlockSpec((Q_RANK, wide), fixed), pl.BlockSpec((KV_RANK, wide), fixed),
                  pl.BlockSpec((1, QK_PAD), fixed), pl.BlockSpec((1, QK_PAD), fixed),
                  pl.BlockSpec((ts, 128), row), pl.BlockSpec((ts, 128), row), pl.BlockSpec((ts, 128), row),
                  pl.BlockSpec((N_HEADS, ts, QK_PAD), head), pl.BlockSpec((N_HEADS, ts, QK_PAD), head),
                  pl.BlockSpec((N_HEADS, ts, V_DIM), head)],
        out_specs=[pl.BlockSpec((ts, wide), row), pl.BlockSpec((ts, wide), row), pl.BlockSpec((ts, DOWN_PAD), row),
                   pl.BlockSpec((1, QK_PAD), fixed), pl.BlockSpec((1, QK_PAD), fixed),
                   pl.BlockSpec((1, Q_RANK), fixed), pl.BlockSpec((1, KV_RANK), fixed)],
        out_shape=[jax.ShapeDtypeStruct((s, wide), BF16), jax.ShapeDtypeStruct((s, wide), BF16),
                   jax.ShapeDtypeStruct((s, DOWN_PAD), BF16),
                   jax.ShapeDtypeStruct((1, QK_PAD), F32), jax.ShapeDtypeStruct((1, QK_PAD), F32),
                   jax.ShapeDtypeStruct((1, Q_RANK), F32), jax.ShapeDtypeStruct((1, KV_RANK), F32)],
        compiler_params=_cparams("arbitrary"),
    )(a, g_qa, g_kva, w_uq, w_ukv, g_q, g_k, cc, sa, sb, dq, dk, dv)


def _flash_fwd(q, k, v, pos_col, pos_row, *, name, gather=None):
    nh, s, _ = q.shape
    tq = _tile(s, FWD_TQ)
    tk = _tile(s, FWD_TK)
    sq = tq // ATTN_CHAINS
    nq = s // tq
    names = list(gather or {})
    ng = len(names)

    def body(q_ref, k_ref, v_ref, pq_ref, pk_ref, *rest):
        o_ref, lse_ref = rest[ng:ng + 2]
        m_sc, acc_sc = rest[2 * ng + 2:2 * ng + 4]
        qb = pl.program_id(1)
        if ng:
            sends, recvs = _gather_ici_copies(rest[ng + 2:2 * ng + 2], names, *rest[2 * ng + 4:], base=0, stride=3)

            @pl.when((pl.program_id(0) == 0) & (qb == 0))
            def _():
                for cp in sends:
                    cp.start()

        m_sc[...] = jnp.full_like(m_sc, NEG)
        acc_sc[...] = jnp.zeros_like(acc_sc)

        def step(kb, masked):
            keys = pl.ds(pl.multiple_of(kb * tk, tk), tk)
            kt = k_ref[0, keys, :]
            vt = v_ref[0, keys, :]
            scores = [_dot_nt(q_ref[0, u * sq:(u + 1) * sq, :], kt) for u in range(ATTN_CHAINS)]
            for u in range(ATTN_CHAINS):
                rows = slice(u * sq, (u + 1) * sq)
                sc = scores[u]
                if masked:
                    sc = jnp.where(pq_ref[rows, :] >= pk_ref[:, keys], sc, NEG)
                m_prev = m_sc[rows, :]
                m_new = jnp.maximum(m_prev, jnp.max(sc, axis=-1, keepdims=True))
                alpha = jnp.exp2(m_prev - m_new)
                p = jnp.exp2(sc - jnp.tile(m_new, (1, tk // 128)))
                acc_sc[rows, :] = jnp.tile(alpha, (1, 2)) * acc_sc[rows, :] + _dot(p.astype(BF16), vt)
                m_sc[rows, :] = m_new

        n_before = (qb * tq) // tk
        n_seen = (qb * tq + tq - 1) // tk + 1
        lax.fori_loop(0, n_before, lambda kb, c: (step(kb, False), c)[1], 0)
        lax.fori_loop(n_before, n_seen, lambda kb, c: (step(kb, True), c)[1], 0)
        l = acc_sc[:, V_DIM:2 * V_DIM]
        o_ref[...] = (acc_sc[:, 0:V_DIM] / l).astype(BF16)
        lse = m_sc[...] * (1.0 / LOG2E) + jnp.log(l)
        lse_ref[0] = lse.T[0:1, :]

        if ng:
            @pl.when((pl.program_id(0) == nh - 1) & (qb == nq - 1))
            def _():
                for cp in recvs:
                    cp.wait_recv()
                for cp in sends:
                    cp.wait_send()

    arrays = [gather[nm] for nm in names]
    out = pl.pallas_call(
        body, name=name, grid=(nh, nq),
        in_specs=[pl.BlockSpec((1, tq, QK_PAD), lambda h, qb: (h, qb, 0)),
                  pl.BlockSpec((1, s, QK_PAD), lambda h, qb: (h, 0, 0)),
                  pl.BlockSpec((1, s, 2 * V_DIM), lambda h, qb: (h, 0, 0)),
                  pl.BlockSpec((tq, 1), lambda h, qb: (qb, 0)),
                  pl.BlockSpec((1, s), lambda h, qb: (0, 0))] + [ANY] * ng,
        out_specs=[pl.BlockSpec((tq, V_DIM), lambda h, qb: (qb, h)),
                   pl.BlockSpec((1, 1, tq), lambda h, qb: (h, 0, qb))] + [ANY] * ng,
        scratch_shapes=[pltpu.VMEM((tq, 128), F32), pltpu.VMEM((tq, 2 * V_DIM), F32)]
        + ([pltpu.SemaphoreType.DMA((3 * ng,)), pltpu.SemaphoreType.DMA((3 * ng,))] if ng else []),
        out_shape=[jax.ShapeDtypeStruct((s, nh * V_DIM), BF16), jax.ShapeDtypeStruct((nh, 1, s), F32)]
        + [jax.ShapeDtypeStruct(a.shape, a.dtype) for a in arrays],
        input_output_aliases={5 + i: 2 + i for i in range(ng)},
        compiler_params=_cparams("arbitrary", "arbitrary") if ng else _cparams("parallel", "parallel"),
    )(q, k, v, pos_col, pos_row, *arrays)
    return out[0], out[1], dict(zip(names, out[2:]))


def _attn_delta(do, o, *, name):
    s = do.shape[0]
    tm = _tile(s, TM)

    def body(do_ref, o_ref, d_ref):
        for h in range(N_HEADS):
            cols = slice(h * V_DIM, (h + 1) * V_DIM)
            prod = do_ref[:, cols].astype(F32) * o_ref[:, cols].astype(F32)
            d_ref[h] = jnp.sum(prod.T, axis=0, keepdims=True)

    return pl.pallas_call(
        body, name=name, grid=(s // tm,),
        in_specs=[pl.BlockSpec((tm, N_HEADS * V_DIM), lambda i: (i, 0))] * 2,
        out_specs=pl.BlockSpec((N_HEADS, 1, tm), lambda i: (0, 0, i)),
        out_shape=jax.ShapeDtypeStruct((N_HEADS, 1, s), F32),
        compiler_params=_cparams("parallel"),
    )(do, o)


def _flash_bwd(q, k, v, do, lse_row, delta_row, pos_col, pos_row, *, name, scatter=None):
    nh, s, _ = q.shape
    tq = _tile(s, BWD_TQ)
    tk = _tile(s, BWD_TK)
    nq, nk = s // tq, s // tk
    sk = tk // ATTN_CHAINS
    names = list(scatter or {})
    ng = len(names)

    def body(q_ref, k_ref, v_ref, do_ref, lse_ref, delta_ref, pq_ref, pk_ref, *rest):
        dq_ref, dk_ref, dv_ref = rest[ng:ng + 3]
        dk_sc, dv_sc = rest[2 * ng + 3:2 * ng + 5]
        kb = pl.program_id(1)
        if ng:
            copies = _scatter_copies(rest[:ng], rest[ng + 3:2 * ng + 3], names, *rest[2 * ng + 5:])

            @pl.when((pl.program_id(0) == 0) & (kb == 0))
            def _():
                for cp in copies:
                    cp.start()

        @pl.when(kb == 0)
        def _():
            dq_ref[...] = jnp.zeros_like(dq_ref)

        dk_sc[...] = jnp.zeros_like(dk_sc)
        dv_sc[...] = jnp.zeros_like(dv_sc)

        def step(qb, masked):
            trim = masked and tq == tk
            start = pl.multiple_of(qb * tq, tq)
            offs = [u * sk if trim else 0 for u in range(ATTN_CHAINS)]
            qss = [pl.ds(start + offs[u], tq - offs[u]) for u in range(ATTN_CHAINS)]
            qts = [q_ref[0, qss[u], :] for u in range(ATTN_CHAINS)]
            dots = [do_ref[qss[u], :] for u in range(ATTN_CHAINS)]
            sts = [_dot_nt(k_ref[0, u * sk:(u + 1) * sk, :], qts[u]) for u in range(ATTN_CHAINS)]
            dpts = [_dot_nt(v_ref[0, u * sk:(u + 1) * sk, :], dots[u]) for u in range(ATTN_CHAINS)]
            parts = []
            for u in range(ATTN_CHAINS):
                rows = slice(u * sk, (u + 1) * sk)
                pt = jnp.exp2(sts[u] - lse_ref[0, :, qss[u]] * LOG2E)
                if masked:
                    pt = jnp.where(pq_ref[:, qss[u]] >= pk_ref[rows, :], pt, 0.0)
                dv_sc[rows, :] += _dot(pt.astype(BF16), dots[u])
                dst = (pt * (dpts[u] - delta_ref[0, :, qss[u]])).astype(BF16)
                dk_sc[rows, :] += _dot(dst, qts[u])
                parts.append(_dot_tn(dst, k_ref[0, rows, :]))
            if trim:
                for u in range(ATTN_CHAINS):
                    dq_ref[0, qss[u], :] += parts[u]
            else:
                dq_ref[0, qss[0], :] += functools.reduce(lambda a, b: a + b, parts)

        q_first = (kb * tk) // tq
        q_clear = (kb * tk + tk - 1) // tq + 1
        lax.fori_loop(q_first, q_clear, lambda qb, c: (step(qb, True), c)[1], 0)
        lax.fori_loop(q_clear, nq, lambda qb, c: (step(qb, False), c)[1], 0)
        dk_ref[0] = dk_sc[...] * (1.0 / LOG2E)
        dv_ref[0] = dv_sc[...]

        @pl.when(kb == nk - 1)
        def _():
            dq_ref[...] = dq_ref[...] * SCALE

        if ng:
            @pl.when((pl.program_id(0) == nh - 1) & (kb == nk - 1))
            def _():
                for cp in copies:
                    cp.wait()

    arrays = [scatter[nm] for nm in names]
    out = pl.pallas_call(
        body, name=name, grid=(nh, nk),
        in_specs=[pl.BlockSpec((1, s, QK_PAD), lambda h, kb: (h, 0, 0)),
                  pl.BlockSpec((1, tk, QK_PAD), lambda h, kb: (h, kb, 0)),
                  pl.BlockSpec((1, tk, V_DIM), lambda h, kb: (h, kb, 0)),
                  pl.BlockSpec((s, V_DIM), lambda h, kb: (0, h)),
                  pl.BlockSpec((1, 1, s), lambda h, kb: (h, 0, 0)),
                  pl.BlockSpec((1, 1, s), lambda h, kb: (h, 0, 0)),
                  pl.BlockSpec((1, s), lambda h, kb: (0, 0)),
                  pl.BlockSpec((tk, 1), lambda h, kb: (kb, 0))] + [ANY] * ng,
        out_specs=[pl.BlockSpec((1, s, QK_PAD), lambda h, kb: (h, 0, 0)),
                   pl.BlockSpec((1, tk, QK_PAD), lambda h, kb: (h, kb, 0)),
                   pl.BlockSpec((1, tk, V_DIM), lambda h, kb: (h, kb, 0))] + [ANY] * ng,
        scratch_shapes=[pltpu.VMEM((tk, QK_PAD), F32), pltpu.VMEM((tk, V_DIM), F32)]
        + ([pltpu.SemaphoreType.DMA((3 * ng,)), pltpu.SemaphoreType.DMA((3 * ng,))] if ng else []),
        out_shape=[jax.ShapeDtypeStruct((nh, s, QK_PAD), F32), jax.ShapeDtypeStruct((nh, s, QK_PAD), F32),
                   jax.ShapeDtypeStruct((nh, s, V_DIM), F32)] + _scatter_out_shapes(names, arrays),
        compiler_params=_cparams("arbitrary", "arbitrary"),
    )(q, k, v, do, lse_row, delta_row, pos_row, pos_col, *arrays)
    return out[0], out[1], out[2], dict(zip(names, out[3:]))


def _loss_head(y, target, *, name):
    s, d = y.shape
    tm = _tile(s, TM)
    nt = s // tm

    def body(y_ref, t_ref, dy_ref, loss_ref, acc):
        i = pl.program_id(0)

        @pl.when(i == 0)
        def _():
            acc[...] = jnp.zeros_like(acc)

        e = y_ref[...] - t_ref[...]
        dy_ref[...] = e * (1.0 / d)
        acc[...] += jnp.sum((e * e).reshape(tm // 8, 8, d), axis=0)

        @pl.when(i == nt - 1)
        def _():
            loss_ref[...] = jnp.full((1, 128), 0.5 / d, F32) * jnp.sum(acc[...])

    return pl.pallas_call(
        body, name=name, grid=(nt,),
        in_specs=[pl.BlockSpec((tm, d), lambda i: (i, 0))] * 2,
        out_specs=[pl.BlockSpec((tm, d), lambda i: (i, 0)), pl.BlockSpec((1, 128), lambda i: (0, 0))],
        out_shape=[jax.ShapeDtypeStruct((s, d), F32), jax.ShapeDtypeStruct((1, 128), F32)],
        scratch_shapes=[pltpu.VMEM((8, d), F32)],
        compiler_params=_cparams("arbitrary"),
    )(y, target)


def _adamw(w, g, m, v, *, name):
    r, c = w.shape
    tr = _tile(r, 512) if r % 8 == 0 else r

    def body(w_ref, g_ref, m_ref, v_ref, d_ref, nm_ref, nv_ref):
        g_t = g_ref[...]
        nm = ADAM_B1 * m_ref[...] + (1.0 - ADAM_B1) * g_t
        nv = ADAM_B2 * v_ref[...] + (1.0 - ADAM_B2) * (g_t * g_t)
        m_hat = nm / (1.0 - ADAM_B1 ** ADAM_STEP)
        v_hat = nv / (1.0 - ADAM_B2 ** ADAM_STEP)
        d_ref[...] = -ADAM_LR * (m_hat / (jnp.sqrt(v_hat) + ADAM_EPS) + ADAM_WD * w_ref[...])
        nm_ref[...] = nm
        nv_ref[...] = nv

    spec = pl.BlockSpec((tr, c), lambda i: (i, 0))
    return pl.pallas_call(
        body, name=name, grid=(r // tr,), in_specs=[spec] * 4, out_specs=[spec] * 3,
        out_shape=[jax.ShapeDtypeStruct((r, c), F32)] * 3,
        compiler_params=_cparams("parallel"),
    )(w, g, m, v)


def _place():
    return lax.axis_index("x"), lax.axis_index("y"), lax.axis_index("c")


def _other_chips(x, y):
    return [(1 - x, y), (x, 1 - y), (1 - x, 1 - y)]


BIG = {
    "attn_w_down": ((2, 1024, 448), 1), "attn_w_uq": ((2, 256, 1536), 2), "attn_w_ukv": ((2, 128, 2048), 2),
    "attn_w_o": ((2, 1024, 1024), 1), "conv_w_in": ((2, 1024, 3072), 2),
    "conv_w_out": ((2, 1024, 1024), 1), "mlp_w1": ((4, 1024, 4096), 2), "mlp_w2": ((4, 4096, 1024), 1),
}
CONV_W = (2, 3, 1024)


def _shard_shape(name):
    shape, axis = BIG[name]
    return tuple(n // N_CHIPS if i == axis else n for i, n in enumerate(shape))


def _band(ref, name, layers, chip):
    shape, axis = BIG[name]
    width = shape[axis] // N_CHIPS
    if axis == 1:
        return ref.at[layers, pl.ds(chip * width, width), :]
    return ref.at[layers, :, pl.ds(chip * width, width)]


def _half(name, c):
    hl = BIG[name][0][0] // 2
    return pl.ds(c * hl, hl)


def _place_own(w, nm, chip, *, name):
    shape, axis = BIG[nm]
    layers, rows, cols = w.shape
    tr = _sum_rows(rows, cols)
    nrb = rows // tr
    if axis == 1:
        band = lambda l, i, ch: (l, ch[0] * nrb + i, 0)
    else:
        band = lambda l, i, ch: (l, i, ch[0])

    def body(chip_ref, w_ref, o_ref):
        o_ref[...] = w_ref[...].astype(BF16)

    return pl.pallas_call(
        body, name=name,
        grid_spec=pltpu.PrefetchScalarGridSpec(
            num_scalar_prefetch=1, grid=(layers, nrb),
            in_specs=[pl.BlockSpec((1, tr, cols), lambda l, i, ch: (l, i, 0))],
            out_specs=pl.BlockSpec((1, tr, cols), band)),
        out_shape=jax.ShapeDtypeStruct(shape, BF16),
        compiler_params=_cparams("parallel", "parallel"),
    )(chip, w)


def _gather_copies(outs, names, send_sems, recv_sems, *, base, stride, to_sibling):
    x, y, c = _place()
    me = 2 * x + y

    def copy(k, ref, nm, layers, chip, to):
        band = _band(ref, nm, layers, chip)
        return pltpu.make_async_remote_copy(
            src_ref=band, dst_ref=band, send_sem=send_sems.at[k], recv_sem=recv_sems.at[k],
            device_id=to, device_id_type=MESH)

    sends, recvs = [], []
    for i, nm in enumerate(names):
        for j, (cx, cy) in enumerate(_other_chips(x, y)):
            k = base + stride * i + j
            if to_sibling:
                sends.append(copy(k, outs[i], nm, _half(nm, c), 2 * cx + cy, (x, y, 1 - c)))
                recvs.append(copy(k, outs[i], nm, _half(nm, 1 - c), 2 * cx + cy, (x, y, c)))
            else:
                sends.append(copy(k, outs[i], nm, _half(nm, c), me, (cx, cy, c)))
                recvs.append(copy(k, outs[i], nm, _half(nm, c), 2 * cx + cy, (x, y, c)))
    return sends, recvs


def _gather_ici_copies(outs, names, send_sems, recv_sems, *, base, stride):
    return _gather_copies(outs, names, send_sems, recv_sems, base=base, stride=stride, to_sibling=False)


def _gather_weights(fulls, *, name, ici=True):
    names = list(fulls)
    n = len(names)

    def body(*refs):
        outs = refs[n:2 * n]
        sems = refs[2 * n:]
        sent = []
        if ici:
            sends, recvs = _gather_copies(outs, names, *sems, base=0, stride=6, to_sibling=False)
            for cp in sends:
                cp.start()
            for cp in recvs:
                cp.wait_recv()
            sent += sends
        sends, recvs = _gather_copies(outs, names, *sems, base=3, stride=6, to_sibling=True)
        for cp in sends:
            cp.start()
        for cp in recvs:
            cp.wait_recv()
        for cp in sent + sends:
            cp.wait_send()

    arrays = [fulls[nm] for nm in names]
    out = pl.pallas_call(
        body, name=name, in_specs=[ANY] * n, out_specs=[ANY] * n,
        out_shape=[jax.ShapeDtypeStruct(a.shape, a.dtype) for a in arrays],
        input_output_aliases={i: i for i in range(n)},
        scratch_shapes=[pltpu.SemaphoreType.DMA((6 * n,)), pltpu.SemaphoreType.DMA((6 * n,))],
    )(*arrays)
    return dict(zip(names, out))


def _swap_halves(grads, *, name):
    names = list(grads)
    n = len(names)

    def body(*refs):
        ins, outs = refs[:n], refs[n:2 * n]
        send_sems, recv_sems = refs[2 * n:]
        x, y, c = _place()
        copies = []
        for i, nm in enumerate(names):
            cp = pltpu.make_async_remote_copy(
                src_ref=ins[i].at[_half(nm, 1 - c)], dst_ref=outs[i], send_sem=send_sems.at[i],
                recv_sem=recv_sems.at[i], device_id=(x, y, 1 - c), device_id_type=MESH)
            cp.start()
            copies.append(cp)
        for cp in copies:
            cp.wait()

    arrays = [grads[nm] for nm in names]
    out = pl.pallas_call(
        body, name=name, in_specs=[ANY] * n, out_specs=[ANY] * n,
        out_shape=[jax.ShapeDtypeStruct((a.shape[0] // 2,) + a.shape[1:], a.dtype) for a in arrays],
        scratch_shapes=[pltpu.SemaphoreType.DMA((n,)), pltpu.SemaphoreType.DMA((n,))],
    )(*arrays)
    return dict(zip(names, out))


def _sum_rows(rows, cols):
    t = rows
    while t * cols * 4 > SUM_BLOCK_BYTES and t % 16 == 0:
        t //= 2
    return t


def _chip_sum(g, r1, core, *, name):
    layers, rows, cols = g.shape
    hl = layers // 2
    tr = _sum_rows(rows, cols)

    def body(core_ref, g_ref, r_ref, o_ref):
        o_ref[...] = (g_ref[...] + r_ref[...]).astype(BF16)

    return pl.pallas_call(
        body, name=name,
        grid_spec=pltpu.PrefetchScalarGridSpec(
            num_scalar_prefetch=1, grid=(hl, rows // tr),
            in_specs=[pl.BlockSpec((1, tr, cols), lambda l, i, cr: (cr[0] * hl + l, i, 0)),
                      pl.BlockSpec((1, tr, cols), lambda l, i, cr: (l, i, 0))],
            out_specs=pl.BlockSpec((1, tr, cols), lambda l, i, cr: (l, i, 0))),
        out_shape=jax.ShapeDtypeStruct((hl, rows, cols), BF16),
        compiler_params=_cparams("parallel", "parallel"),
    )(core, g, r1)


def _chip_partials(grads, names, *, tag):
    core = lax.axis_index("c").astype(jnp.int32).reshape(1)
    r1 = _swap_halves({n: grads[n] for n in names}, name=f"grad_swap_halves_{tag}")
    return r1, {n: _chip_sum(grads[n], r1[n], core, name=f"grad_chip_sum_{n}") for n in names}


def _scatter_partials(partials):
    names = list(partials)
    n = len(names)

    def body(*refs):
        copies = _scatter_copies(refs[:n], refs[n:2 * n], names, *refs[2 * n:])
        for cp in copies:
            cp.start()
        for cp in copies:
            cp.wait()

    arrays = [partials[nm] for nm in names]
    out = pl.pallas_call(
        body, name="grad_scatter_partials", in_specs=[ANY] * n, out_specs=[ANY] * n,
        out_shape=_scatter_out_shapes(names, arrays),
        scratch_shapes=[pltpu.SemaphoreType.DMA((3 * n,)), pltpu.SemaphoreType.DMA((3 * n,))],
    )(*arrays)
    return dict(zip(names, out))


def _scatter_copies(ins, outs, names, send_sems, recv_sems):
    x, y, c = _place()
    copies = []
    for i, nm in enumerate(names):
        for j, (cx, cy) in enumerate(_other_chips(x, y)):
            copies.append(pltpu.make_async_remote_copy(
                src_ref=_band(ins[i], nm, slice(None), 2 * cx + cy), dst_ref=outs[i].at[j],
                send_sem=send_sems.at[3 * i + j], recv_sem=recv_sems.at[3 * i + j],
                device_id=(cx, cy, c), device_id_type=MESH))
    return copies


def _scatter_out_shapes(names, arrays):
    return [jax.ShapeDtypeStruct((3, a.shape[0]) + _shard_shape(nm)[1:], a.dtype) for nm, a in zip(names, arrays)]


def _final_sum(g, r1, r2, place, nm, *, name):
    (layers, _, _), axis = BIG[nm]
    hl = layers // 2
    _, rows, cols = _shard_shape(nm)
    tr = _sum_rows(rows, cols)
    nrb = rows // tr
    if axis == 1:
        blk = lambda l, i, pc: (l, pc[1] * nrb + i, 0)
    else:
        blk = lambda l, i, pc: (l, i, pc[1])

    def body(place_ref, g_ref, r1_ref, r2_ref, o_ref):
        acc = g_ref[...] + r1_ref[...]
        for j in range(3):
            acc = acc + r2_ref[j].astype(F32)
        o_ref[...] = acc

    return pl.pallas_call(
        body, name=name,
        grid_spec=pltpu.PrefetchScalarGridSpec(
            num_scalar_prefetch=1, grid=(hl, nrb),
            in_specs=[pl.BlockSpec((1, tr, cols), lambda l, i, pc: blk(pc[0] * hl + l, i, pc)),
                      pl.BlockSpec((1, tr, cols), lambda l, i, pc: blk(l, i, pc)),
                      pl.BlockSpec((3, 1, tr, cols), lambda l, i, pc: (0, l, i, 0))],
            out_specs=pl.BlockSpec((1, tr, cols), lambda l, i, pc: (pc[0] * hl + l, i, 0))),
        out_shape=jax.ShapeDtypeStruct((layers, rows, cols), F32),
        compiler_params=_cparams("parallel", "parallel"),
    )(place, g, r1, r2)


def _join_halves(shards):
    names = list(shards)
    n = len(names)

    def body(*refs):
        outs = refs[n:2 * n]
        send_sems, recv_sems = refs[2 * n:]
        x, y, c = _place()
        copies = []
        for i, nm in enumerate(names):
            mine = outs[i].at[_half(nm, c)]
            cp = pltpu.make_async_remote_copy(
                src_ref=mine, dst_ref=mine, send_sem=send_sems.at[i], recv_sem=recv_sems.at[i],
                device_id=(x, y, 1 - c), device_id_type=MESH)
            cp.start()
            copies.append(cp)
        for i, nm in enumerate(names):
            theirs = outs[i].at[_half(nm, 1 - c)]
            pltpu.make_async_remote_copy(
                src_ref=theirs, dst_ref=theirs, send_sem=send_sems.at[i], recv_sem=recv_sems.at[i],
                device_id=(x, y, 1 - c), device_id_type=MESH).wait_recv()
        for cp in copies:
            cp.wait_send()

    arrays = [shards[nm] for nm in names]
    out = pl.pallas_call(
        body, name="grad_join_halves", in_specs=[ANY] * n, out_specs=[ANY] * n,
        out_shape=[jax.ShapeDtypeStruct(a.shape, a.dtype) for a in arrays],
        input_output_aliases={i: i for i in range(n)},
        scratch_shapes=[pltpu.SemaphoreType.DMA((n,)), pltpu.SemaphoreType.DMA((n,))],
    )(*arrays)
    return dict(zip(names, out))


def _all_reduce_small(part, *, name):
    rows, cols = part.shape
    vm = pl.BlockSpec(memory_space=pltpu.VMEM)

    def body(p_ref, o_ref, land, send_sems, recv_sems):
        x, y, c = _place()
        me = 4 * x + 2 * y + c
        flips = [(dx, dy, dc) for dx in (0, 1) for dy in (0, 1) for dc in (0, 1)][1:]
        copies = []
        for k, (dx, dy, dc) in enumerate(flips):
            cp = pltpu.make_async_remote_copy(
                src_ref=p_ref, dst_ref=land.at[me], send_sem=send_sems.at[k], recv_sem=recv_sems.at[k],
                device_id=(1 - x if dx else x, 1 - y if dy else y, 1 - c if dc else c), device_id_type=MESH)
            cp.start()
            copies.append(cp)
        land[me] = p_ref[...]
        for cp in copies:
            cp.wait()
        acc = land[0]
        for j in range(1, 8):
            acc = acc + land[j]
        o_ref[...] = acc

    return pl.pallas_call(
        body, name=name, in_specs=[vm], out_specs=vm,
        out_shape=jax.ShapeDtypeStruct((rows, cols), F32),
        scratch_shapes=[pltpu.VMEM((8, rows, cols), F32), pltpu.SemaphoreType.DMA((7,)), pltpu.SemaphoreType.DMA((7,))],
    )(part)


SMALL = {"g_mix": (4, 1024), "g_mlp": (4, 1024), "attn_g_q_a": (2, 256), "attn_g_kv_a": (2, 128),
         "attn_g_qnorm": (2, 192), "attn_g_knorm": (2, 192)}
SMALL_GRADS = {**SMALL, "conv_w": CONV_W}
WEIGHT_ORDER = ["g_mix", "g_mlp", "attn_w_down", "attn_g_q_a", "attn_g_kv_a", "attn_w_uq", "attn_w_ukv",
                "attn_g_qnorm", "attn_g_knorm", "attn_w_o", "conv_w_in", "conv_w", "conv_w_out", "mlp_w1", "mlp_w2"]


def _prod(shape):
    n = 1
    for v in shape:
        n *= v
    return n


def _pack_small(parts, table):
    flat = [parts[n].reshape(-1) for n in table]
    size = sum(_prod(s) for s in table.values())
    rows = -(-size // (8 * 128)) * 8
    flat.append(jnp.zeros((rows * 128 - size,), F32))
    return jnp.concatenate(flat).reshape(rows, 128)


def _unpack_small(buf, table):
    flat = buf.reshape(-1)
    out, off = {}, 0
    for n, shp in table.items():
        out[n] = flat[off:off + _prod(shp)].reshape(shp)
        off += _prod(shp)
    return out


def _rope_tables(positions):
    inv_freq = ROPE_THETA ** (-jnp.arange(0, QK_ROPE, 2, dtype=F32) / QK_ROPE)
    ang = positions.astype(F32)[:, None] * inv_freq
    cos, sin = jnp.cos(ang), jnp.sin(ang)
    z32 = jnp.zeros_like(cos)
    z64 = jnp.zeros((positions.shape[0], 64), F32)
    cc = jnp.concatenate([cos, cos, z64], axis=1)
    sa = jnp.concatenate([-sin, z32, z64], axis=1)
    sb = jnp.concatenate([z32, sin, z64], axis=1)
    return cc, sa, sb


def _pad_heads(w, width):
    k = w.shape[0]
    w = w.reshape(k, N_HEADS, width)
    return jnp.pad(w, ((0, 0), (0, 0), (0, QK_PAD - width))).reshape(k, N_HEADS * QK_PAD)


EARLY = ("mlp_w1", "mlp_w2", "conv_w_in", "conv_w_out")
LATE = ("attn_w_down", "attn_w_uq", "attn_w_ukv", "attn_w_o")
GATHER_LATER = EARLY


def _local_step(x, positions, target, wb, gains, later=None):
    s = x.shape[0]
    cc, sa, sb = _rope_tables(positions)
    pos_col = positions.reshape(s, 1)
    pos_row = positions.reshape(1, s)

    saved = []
    for i in range(4):
        j = i // 2
        g_mix = gains["g_mix"][i:i + 1]
        g_mlp = gains["g_mlp"][i:i + 1]
        if i % 2 == 0:
            w_down = jnp.pad(wb["attn_w_down"][j], ((0, 0), (0, DOWN_PAD - DOWN_DIM)))
            w_uq = _pad_heads(wb["attn_w_uq"][j], QK_DIM)
            w_ukv = wb["attn_w_ukv"][j]
            g_qa = gains["attn_g_q_a"][j:j + 1]
            g_kva = gains["attn_g_kv_a"][j:j + 1]
            g_q = jnp.pad(gains["attn_g_qnorm"][j:j + 1], ((0, 0), (0, QK_PAD - QK_DIM)))
            g_k = jnp.pad(gains["attn_g_knorm"][j:j + 1], ((0, 0), (0, QK_PAD - QK_DIM)))
            h, a = _norm_mm(x, g_mix, w_down, out_dtype=F32, name=f"mla_down_{j}")
            cq, ckv, q, k, v = _mla_prep(a, g_qa, g_kva, w_uq, w_ukv, g_q, g_k, cc, sa, sb, name=f"mla_prep_{j}")
            o, lse, got = _flash_fwd(q, k, v, pos_col, pos_row, name=f"flash_fwd_{j}",
                                     gather=later if i == 0 else None)
            if got:
                wb = {**wb, **_gather_weights(got, name="gather_later_forward", ici=False)}
            x_mid = _mm_nn(o, wb["attn_w_o"], layer=j, out_dtype=F32, residual=x, name=f"mla_out_{j}")
            mix = dict(h=h, a=a, cq=cq, ckv=ckv, q=q, k=k, v=v, o=o, lse=lse, w_down=w_down, w_uq=w_uq, w_ukv=w_ukv,
                       g_qa=g_qa, g_kva=g_kva, g_q=g_q, g_k=g_k)
        else:
            h, bcu = _norm_mm(x, g_mix, wb["conv_w_in"], layer=j, out_dtype=BF16, name=f"conv_in_{j}")
            z = _conv_gate(bcu, gains["conv_w"][j], name=f"conv_gate_{j}")
            x_mid = _mm_nn(z, wb["conv_w_out"], layer=j, out_dtype=F32, residual=x, name=f"conv_out_{j}")
            mix = dict(h=h, bcu=bcu, z=z)
        h2, u, x_out = _mlp_fwd(x_mid, g_mlp, wb["mlp_w1"], wb["mlp_w2"], layer=i, name=f"mlp_fwd_{i}")
        saved.append(dict(x_in=x, x_mid=x_mid, mix=mix, h2=h2, u=u, g_mix=g_mix, g_mlp=g_mlp))
        x = x_out

    dx, loss = _loss_head(x, target, name="loss_head")

    gw = {n: None for n in BIG}
    exchanged = None
    g_uq = [None, None]
    gs = {n: [None] * SMALL_GRADS[n][0] for n in SMALL_GRADS}

    def wgrad(nm, layer, a, b, **kw):
        gw[nm] = _mm_tn(a, b, stack=gw[nm], layer=layer, layers=BIG[nm][0][0], name=f"{nm}_grad_{layer}", **kw)

    for i in reversed(range(4)):
        j = i // 2
        sv = saved[i]
        mix = sv["mix"]
        du = _mlp_down_bwd(dx, wb["mlp_w2"], sv["u"], layer=i, name=f"mlp_down_bwd_{i}")
        wgrad("mlp_w2", i, sv["u"], dx, sqrelu_a=True)
        wgrad("mlp_w1", i, sv["h2"], du)
        dx, dg = _nt_rms_bwd(du, wb["mlp_w1"], sv["x_mid"], sv["g_mlp"], dx, layer=i, name=f"mlp_up_bwd_{i}")
        gs["g_mlp"][i] = dg[0]
        if i % 2 == 0:
            do = _mm_nt(dx, wb["attn_w_o"], layer=j, out_dtype=BF16, name=f"mla_out_bwd_{j}")
            wgrad("attn_w_o", j, mix["o"], dx)
            delta = _attn_delta(do, mix["o"], name=f"attn_delta_{j}")
            lse_row, delta_row = mix["lse"], delta
            partials = None
            if i == 0 and later is not None:
                r1_early, partials = _chip_partials(gw, EARLY, tag="early")
            dq, dk, dv, arrived = _flash_bwd(mix["q"], mix["k"], mix["v"], do, lse_row, delta_row, pos_col, pos_row,
                                             name=f"flash_bwd_{j}", scatter=partials)
            if partials is not None:
                exchanged = (r1_early, arrived)
            dqr, dkvr, da, dgq, dgk, dgqa, dgkva = _mla_prep_bwd(
                mix["a"], mix["g_qa"], mix["g_kva"], mix["w_uq"], mix["w_ukv"], mix["g_q"], mix["g_k"], cc, sa, sb,
                dq, dk, dv, name=f"mla_prep_bwd_{j}")
            g_uq[j] = _mm_tn(mix["cq"], dqr, name=f"attn_w_uq_grad_{j}")[0]
            wgrad("attn_w_ukv", j, mix["ckv"], dkvr)
            wgrad("attn_w_down", j, mix["h"], da, keep=DOWN_DIM)
            dx, dg = _nt_rms_bwd(da, mix["w_down"], sv["x_in"], sv["g_mix"], dx, name=f"mla_down_bwd_{j}")
            gs["attn_g_qnorm"][j] = dgq[0, :QK_DIM]
            gs["attn_g_knorm"][j] = dgk[0, :QK_DIM]
            gs["attn_g_q_a"][j] = dgqa[0]
            gs["attn_g_kv_a"][j] = dgkva[0]
        else:
            dz = _mm_nt(dx, wb["conv_w_out"], layer=j, out_dtype=F32, name=f"conv_out_bwd_{j}")
            wgrad("conv_w_out", j, mix["z"], dx)
            dbcu, dcw = _conv_gate_bwd(mix["bcu"], dz, gains["conv_w"][j], name=f"conv_gate_bwd_{j}")
            gs["conv_w"][j] = dcw
            wgrad("conv_w_in", j, mix["h"], dbcu)
            dx, dg = _nt_rms_bwd(dbcu, wb["conv_w_in"], sv["x_in"], sv["g_mix"], dx, layer=j, name=f"conv_in_bwd_{j}")
        gs["g_mix"][i] = dg[0]

    gw["attn_w_uq"] = jnp.stack(g_uq).reshape(2, Q_RANK, N_HEADS, QK_PAD)[..., :QK_DIM].reshape(BIG["attn_w_uq"][0])
    grads_small = {n: jnp.stack(v) for n, v in gs.items()}
    return loss, dx, gw, grads_small, exchanged


def kernel(x, positions, g_mix, g_mlp, attn_w_down, attn_g_q_a, attn_g_kv_a, attn_w_uq, attn_w_ukv, attn_g_qnorm, attn_g_knorm, attn_w_o, conv_w_in, conv_w, conv_w_out, mlp_w1, mlp_w2, loss_target, m_g_mix, m_g_mlp, m_attn_w_down, m_attn_g_q_a, m_attn_g_kv_a, m_attn_w_uq, m_attn_w_ukv, m_attn_g_qnorm, m_attn_g_knorm, m_attn_w_o, m_conv_w_in, m_conv_w, m_conv_w_out, m_mlp_w1, m_mlp_w2, v_g_mix, v_g_mlp, v_attn_w_down, v_attn_g_q_a, v_attn_g_kv_a, v_attn_w_uq, v_attn_w_ukv, v_attn_g_qnorm, v_attn_g_knorm, v_attn_w_o, v_conv_w_in, v_conv_w, v_conv_w_out, v_mlp_w1, v_mlp_w2):
    w = dict(g_mix=g_mix, g_mlp=g_mlp, attn_w_down=attn_w_down, attn_g_q_a=attn_g_q_a, attn_g_kv_a=attn_g_kv_a,
             attn_w_uq=attn_w_uq, attn_w_ukv=attn_w_ukv, attn_g_qnorm=attn_g_qnorm, attn_g_knorm=attn_g_knorm,
             attn_w_o=attn_w_o, conv_w_in=conv_w_in, conv_w=conv_w, conv_w_out=conv_w_out, mlp_w1=mlp_w1, mlp_w2=mlp_w2)
    m = dict(g_mix=m_g_mix, g_mlp=m_g_mlp, attn_w_down=m_attn_w_down, attn_g_q_a=m_attn_g_q_a,
             attn_g_kv_a=m_attn_g_kv_a, attn_w_uq=m_attn_w_uq, attn_w_ukv=m_attn_w_ukv, attn_g_qnorm=m_attn_g_qnorm,
             attn_g_knorm=m_attn_g_knorm, attn_w_o=m_attn_w_o, conv_w_in=m_conv_w_in, conv_w=m_conv_w,
             conv_w_out=m_conv_w_out, mlp_w1=m_mlp_w1, mlp_w2=m_mlp_w2)
    v = dict(g_mix=v_g_mix, g_mlp=v_g_mlp, attn_w_down=v_attn_w_down, attn_g_q_a=v_attn_g_q_a,
             attn_g_kv_a=v_attn_g_kv_a, attn_w_uq=v_attn_w_uq, attn_w_ukv=v_attn_w_ukv, attn_g_qnorm=v_attn_g_qnorm,
             attn_g_knorm=v_attn_g_knorm, attn_w_o=v_attn_w_o, conv_w_in=v_conv_w_in, conv_w=v_conv_w,
             conv_w_out=v_conv_w_out, mlp_w1=v_mlp_w1, mlp_w2=v_mlp_w2)
    cx, cy, cc_ = _place()

    chip = 2 * cx + cy

    def own_offset(shape, axis):
        return tuple(chip * (shape[axis] // N_CHIPS) if i == axis else 0 for i in range(3))

    chip_arr = chip.astype(jnp.int32).reshape(1)
    fulls = {n: _place_own(w[n], n, chip_arr, name=f"place_{n}") for n in BIG}
    later = {n: fulls.pop(n) for n in GATHER_LATER}
    wb = _gather_weights(fulls, name="gather_weights")

    placed = lax.dynamic_update_slice(jnp.zeros(CONV_W, F32), conv_w, own_offset(CONV_W, 2))
    conv_w_full = 0.5 * _all_reduce_small(placed.reshape(-1, 128), name="conv_w_gather").reshape(CONV_W)

    gains = {n: w[n] for n in SMALL}
    gains["conv_w"] = conv_w_full

    loss, grad_x, grads_big, grads_small, (r1_early, r2_early) = _local_step(
        x[0], positions[0], loss_target[0], wb, gains, later)

    place = jnp.stack([cc_, chip]).astype(jnp.int32)
    r1_late, partials = _chip_partials(grads_big, LATE, tag="late")
    r1 = {**r1_early, **r1_late}
    r2 = {**r2_early, **_scatter_partials(partials)}
    halves = {n: _final_sum(grads_big[n], r1[n], r2[n], place, n, name=f"grad_final_sum_{n}") for n in BIG}
    grad_shards = _join_halves(halves)

    small = _unpack_small(_all_reduce_small(_pack_small(grads_small, SMALL_GRADS), name="gain_all_reduce"), SMALL_GRADS)
    grad_shards["conv_w"] = lax.dynamic_slice(small["conv_w"], own_offset(CONV_W, 2), conv_w.shape)

    loss_total = lax.psum(loss[0, 0], ("x", "y", "c"))

    grads, deltas, new_m, new_v = {}, {}, {}, {}
    for n in [*BIG, "conv_w"]:
        shp = w[n].shape
        two_d = (shp[0] * shp[1], shp[2])
        g2 = grad_shards[n].reshape(two_d)
        d, nm, nv = _adamw(w[n].reshape(two_d), g2, m[n].reshape(two_d), v[n].reshape(two_d), name=f"adamw_{n}")
        grads[n], deltas[n], new_m[n], new_v[n] = grad_shards[n], d.reshape(shp), nm.reshape(shp), nv.reshape(shp)
    d, nm, nv = _adamw(_pack_small(w, SMALL), _pack_small(small, SMALL), _pack_small(m, SMALL), _pack_small(v, SMALL),
                       name="adamw_gains")
    d, nm, nv = _unpack_small(d, SMALL), _unpack_small(nm, SMALL), _unpack_small(nv, SMALL)
    for n in SMALL:
        grads[n], deltas[n], new_m[n], new_v[n] = small[n], d[n], nm[n], nv[n]

    return (loss_total, grad_x[None],
            *[grads[n] for n in WEIGHT_ORDER], *[deltas[n] for n in WEIGHT_ORDER],
            *[new_m[n] for n in WEIGHT_ORDER], *[new_v[n] for n in WEIGHT_ORDER])
```

```python
import functools

import jax
import jax.numpy as jnp
from jax import lax
from jax.experimental import pallas as pl
from jax.experimental.pallas import tpu as pltpu

F32 = jnp.float32
BF16 = jnp.bfloat16

D_MODEL = 1024
N_HEADS = 8
QK_NOPE = 128
QK_ROPE = 64
QK_DIM = QK_NOPE + QK_ROPE
QK_PAD = 256
V_DIM = 128
Q_RANK = 256
KV_RANK = 128
DOWN_DIM = Q_RANK + KV_RANK + QK_ROPE
DOWN_PAD = 512
D_FF = 4 * D_MODEL
ROPE_THETA = 10000.0
EPS = 1e-6
NEG = -1e30
SCALE = QK_DIM ** -0.5
SCALE_LOG2E = SCALE * 1.4426950408889634
LOG2E = 1.4426950408889634
ATTN_CHAINS = 2

ADAM_LR = 0.001
ADAM_B1 = 0.9
ADAM_B2 = 0.999
ADAM_EPS = 1e-08
ADAM_WD = 0.01
ADAM_STEP = 10

N_CHIPS = 4
MESH = pl.DeviceIdType.MESH
ANY = pl.BlockSpec(memory_space=pl.ANY)

TM = 512
TM_WIDE = 512
FWD_TQ = 1024
FWD_TK = 1024
BWD_TQ = 1024
BWD_TK = 1024
HALO = 16
T_PREP = 512
T_RED = 1024
SUM_BLOCK_BYTES = 2 * 1024 * 1024


def _tile(n, pref):
    t = min(n, pref)
    assert n % t == 0, (n, t)
    return t


def _cparams(*sem):
    return pltpu.CompilerParams(dimension_semantics=sem)


def _dot(a, b):
    return jnp.dot(a, b, preferred_element_type=F32)


def _dot_nt(a, b):
    return lax.dot_general(a, b, (((1,), (1,)), ((), ())), preferred_element_type=F32)


def _dot_tn(a, b):
    return lax.dot_general(a, b, (((0,), (0,)), ((), ())), preferred_element_type=F32)


def _rms(x, width):
    r = lax.rsqrt(jnp.sum(x * x, axis=-1, keepdims=True) * (1.0 / width) + EPS)
    return x * r, r


def _rms_bwd(xhat, r, dxhat, width):
    return r * (dxhat - xhat * (jnp.sum(dxhat * xhat, axis=-1, keepdims=True) * (1.0 / width)))


def _rope(t, cc, sa, sb):
    return t * cc + pltpu.roll(t, 96, 1) * sa + pltpu.roll(t, 32, 1) * sb


def _rope_t(g, cc, sa, sb):
    return g * cc + pltpu.roll(g * sa, 32, 1) + pltpu.roll(g * sb, 96, 1)


def _wspec(w, layer):
    once = pl.Buffered(1)
    if w.ndim == 2:
        return pl.BlockSpec(w.shape, lambda *_: (0, 0), pipeline_mode=once)
    return pl.BlockSpec((None,) + w.shape[1:], lambda *_: (layer, 0, 0), pipeline_mode=once)


def _mm_nn(a, b, *, out_dtype, name, residual=None, layer=0):
    m, k = a.shape
    n = b.shape[-1]
    tm = _tile(m, TM)

    def body(*refs):
        if residual is None:
            a_ref, b_ref, o_ref = refs
        else:
            a_ref, b_ref, r_ref, o_ref = refs
        acc = _dot(a_ref[...].astype(BF16), b_ref[...])
        if residual is not None:
            acc = acc + r_ref[...]
        o_ref[...] = acc.astype(o_ref.dtype)

    in_specs = [pl.BlockSpec((tm, k), lambda i: (i, 0)), _wspec(b, layer)]
    args = [a, b]
    if residual is not None:
        in_specs.append(pl.BlockSpec((tm, n), lambda i: (i, 0)))
        args.append(residual)
    return pl.pallas_call(
        body, name=name, grid=(m // tm,), in_specs=in_specs,
        out_specs=pl.BlockSpec((tm, n), lambda i: (i, 0)),
        out_shape=jax.ShapeDtypeStruct((m, n), out_dtype),
        compiler_params=_cparams("parallel"),
    )(*args)


def _mm_nt(a, b, *, out_dtype, name, layer=0):
    m, k = a.shape
    n = b.shape[-2]
    tm = _tile(m, TM)

    def body(a_ref, b_ref, o_ref):
        o_ref[...] = _dot_nt(a_ref[...].astype(BF16), b_ref[...]).astype(o_ref.dtype)

    return pl.pallas_call(
        body, name=name, grid=(m // tm,),
        in_specs=[pl.BlockSpec((tm, k), lambda i: (i, 0)), _wspec(b, layer)],
        out_specs=pl.BlockSpec((tm, n), lambda i: (i, 0)),
        out_shape=jax.ShapeDtypeStruct((m, n), out_dtype),
        compiler_params=_cparams("parallel"),
    )(a, b)


def _mm_tn(a, b, *, name, stack=None, layer=0, layers=1, keep=None, sqrelu_a=False, swap=None):
    s, ka = a.shape
    n = b.shape[1]
    ts = _tile(s, T_RED)
    tka = _tile(ka, 1024)
    tn = _tile(n, 1024)
    n_out = n if keep is None else keep
    assert keep is None or tn == n
    grid = (ka // tka, n // tn, s // ts)
    names = list(swap or {})
    ns = len(names)
    n_in = 2 + (stack is not None)

    def body(*refs):
        a_ref, b_ref = refs[:2]
        o_ref = refs[n_in + ns]
        if ns:
            copies = _swap_copies(refs[n_in:n_in + ns], refs[n_in + ns + 1:n_in + 2 * ns + 1], names,
                                  *refs[n_in + 2 * ns + 1:])
            _start_at_first_step(copies, grid)

        @pl.when(pl.program_id(2) == 0)
        def _():
            o_ref[...] = jnp.zeros_like(o_ref)

        a_t = a_ref[...]
        if sqrelu_a:
            a_t = _sqrelu(a_t.astype(F32))
        o_ref[...] += _dot_tn(a_t.astype(BF16), b_ref[...].astype(BF16))[:, :n_out if keep else tn]
        if ns:
            _wait_at_last_step(copies, grid)

    in_specs = [pl.BlockSpec((ts, tka), lambda i, j, t: (t, i)), pl.BlockSpec((ts, tn), lambda i, j, t: (t, j))]
    args = [a, b]
    if stack is not None:
        in_specs.append(ANY)
        args.append(stack)
    sent = [swap[nm] for nm in names]
    out = pl.pallas_call(
        body, name=name, grid=grid, in_specs=in_specs + [ANY] * ns,
        out_specs=[pl.BlockSpec((None, tka, tn if keep is None else keep), lambda i, j, t: (layer, i, j))] + [ANY] * ns,
        out_shape=[jax.ShapeDtypeStruct((layers, ka, n_out), F32)] + _swap_out_shapes(sent),
        scratch_shapes=_swap_sems(ns),
        input_output_aliases={} if stack is None else {2: 0},
        compiler_params=_cparams(*(["arbitrary"] * 3 if ns else ["parallel", "parallel", "arbitrary"])),
    )(*args, *sent)
    return (out[0], dict(zip(names, out[1:]))) if ns else out[0]


def _norm_mm(x, g, w, *, out_dtype, name, layer=0):
    s, d = x.shape
    n = w.shape[-1]
    tm = _tile(s, TM)

    def body(x_ref, g_ref, w_ref, h_ref, o_ref):
        xhat, _ = _rms(x_ref[...], d)
        h = (xhat * g_ref[...]).astype(BF16)
        h_ref[...] = h
        o_ref[...] = _dot(h, w_ref[...]).astype(o_ref.dtype)

    return pl.pallas_call(
        body, name=name, grid=(s // tm,),
        in_specs=[pl.BlockSpec((tm, d), lambda i: (i, 0)), pl.BlockSpec((1, d), lambda i: (0, 0)), _wspec(w, layer)],
        out_specs=[pl.BlockSpec((tm, d), lambda i: (i, 0)), pl.BlockSpec((tm, n), lambda i: (i, 0))],
        out_shape=[jax.ShapeDtypeStruct((s, d), BF16), jax.ShapeDtypeStruct((s, n), out_dtype)],
        compiler_params=_cparams("parallel"),
    )(x, g, w)


def _nt_rms_bwd(dy, w, x, g, dres, *, name, layer=0, swap=None):
    s, n = dy.shape
    d = x.shape[1]
    tm = _tile(s, TM)
    grid = (s // tm,)
    names = list(swap or {})
    ns = len(names)

    def body(dy_ref, w_ref, x_ref, g_ref, dres_ref, *rest):
        dx_ref, dg_ref = rest[ns:ns + 2]
        if ns:
            copies = _swap_copies(rest[:ns], rest[ns + 2:2 * ns + 2], names, *rest[2 * ns + 2:])
            _start_at_first_step(copies, grid)

        @pl.when(pl.program_id(0) == 0)
        def _():
            dg_ref[...] = jnp.zeros_like(dg_ref)

        dh = _dot_nt(dy_ref[...], w_ref[...])
        xhat, r = _rms(x_ref[...], d)
        dg_ref[...] += jnp.sum(dh * xhat, axis=0, keepdims=True)
        dx_ref[...] = dres_ref[...] + _rms_bwd(xhat, r, dh * g_ref[...], d)
        if ns:
            _wait_at_last_step(copies, grid)

    sent = [swap[nm] for nm in names]
    out = pl.pallas_call(
        body, name=name, grid=grid,
        in_specs=[pl.BlockSpec((tm, n), lambda i: (i, 0)), _wspec(w, layer),
                  pl.BlockSpec((tm, d), lambda i: (i, 0)), pl.BlockSpec((1, d), lambda i: (0, 0)),
                  pl.BlockSpec((tm, d), lambda i: (i, 0))] + [ANY] * ns,
        out_specs=[pl.BlockSpec((tm, d), lambda i: (i, 0)), pl.BlockSpec((1, d), lambda i: (0, 0))] + [ANY] * ns,
        out_shape=[jax.ShapeDtypeStruct((s, d), F32), jax.ShapeDtypeStruct((1, d), F32)] + _swap_out_shapes(sent),
        scratch_shapes=_swap_sems(ns),
        compiler_params=_cparams("arbitrary"),
    )(dy, w, x, g, dres, *sent)
    return (out[0], out[1], dict(zip(names, out[2:]))) if ns else (out[0], out[1])


def _sqrelu(u):
    return jnp.square(jnp.maximum(u, 0.0))


def _mlp_fwd(x, g, w1, w2, *, name, layer=0):
    s, d = x.shape
    n = w1.shape[-1]
    tm = _tile(s, TM_WIDE)

    def body(x_ref, g_ref, w1_ref, w2_ref, h_ref, u_ref, y_ref):
        x_t = x_ref[...]
        xhat, _ = _rms(x_t, d)
        h = (xhat * g_ref[...]).astype(BF16)
        h_ref[...] = h
        u = _dot(h, w1_ref[...])
        u_ref[...] = u.astype(BF16)
        y_ref[...] = x_t + _dot(_sqrelu(u).astype(BF16), w2_ref[...])

    return pl.pallas_call(
        body, name=name, grid=(s // tm,),
        in_specs=[pl.BlockSpec((tm, d), lambda i: (i, 0)), pl.BlockSpec((1, d), lambda i: (0, 0)),
                  _wspec(w1, layer), _wspec(w2, layer)],
        out_specs=[pl.BlockSpec((tm, d), lambda i: (i, 0)), pl.BlockSpec((tm, n), lambda i: (i, 0)),
                   pl.BlockSpec((tm, d), lambda i: (i, 0))],
        out_shape=[jax.ShapeDtypeStruct((s, d), BF16), jax.ShapeDtypeStruct((s, n), BF16),
                   jax.ShapeDtypeStruct((s, d), F32)],
        compiler_params=_cparams("parallel"),
    )(x, g, w1, w2)


def _mlp_down_bwd(dy, w2, u, *, name, layer=0):
    s, d = dy.shape
    n = w2.shape[-2]
    tm = _tile(s, TM_WIDE)

    def body(dy_ref, w_ref, u_ref, du_ref):
        dact = _dot_nt(dy_ref[...].astype(BF16), w_ref[...])
        du_ref[...] = (dact * (2.0 * jnp.maximum(u_ref[...].astype(F32), 0.0))).astype(BF16)

    return pl.pallas_call(
        body, name=name, grid=(s // tm,),
        in_specs=[pl.BlockSpec((tm, d), lambda i: (i, 0)), _wspec(w2, layer),
                  pl.BlockSpec((tm, n), lambda i: (i, 0))],
        out_specs=pl.BlockSpec((tm, n), lambda i: (i, 0)),
        out_shape=jax.ShapeDtypeStruct((s, n), BF16),
        compiler_params=_cparams("parallel"),
    )(dy, w2, u)


def _conv_gate(bcu, conv_w, *, name):
    s = bcu.shape[0]
    d = D_MODEL
    tm = _tile(s, TM)
    hb = tm // HALO

    def body(bcu_ref, prev_ref, w_ref, z_ref, pbuf):
        i = pl.program_id(0)
        gb = bcu_ref[:, 0:d].astype(F32)
        p = bcu_ref[:, d:2 * d].astype(F32) * bcu_ref[:, 2 * d:3 * d].astype(F32)
        pprev = prev_ref[:, d:2 * d].astype(F32) * prev_ref[:, 2 * d:3 * d].astype(F32)
        pbuf[0:HALO, :] = jnp.where(i > 0, pprev, 0.0)
        pbuf[HALO:HALO + tm, :] = p
        cv = (w_ref[2:3, :] * p + w_ref[1:2, :] * pbuf[HALO - 1:HALO - 1 + tm, :]
              + w_ref[0:1, :] * pbuf[HALO - 2:HALO - 2 + tm, :])
        z_ref[...] = (gb * cv).astype(BF16)

    return pl.pallas_call(
        body, name=name, grid=(s // tm,),
        in_specs=[pl.BlockSpec((tm, 3 * d), lambda i: (i, 0)),
                  pl.BlockSpec((HALO, 3 * d), lambda i: (jnp.maximum(i * hb - 1, 0), 0)),
                  pl.BlockSpec((3, d), lambda i: (0, 0))],
        out_specs=pl.BlockSpec((tm, d), lambda i: (i, 0)),
        out_shape=jax.ShapeDtypeStruct((s, d), BF16),
        scratch_shapes=[pltpu.VMEM((tm + HALO, d), F32)],
        compiler_params=_cparams("parallel"),
    )(bcu, bcu, conv_w)


def _conv_gate_bwd(bcu, dz, conv_w, *, name):
    s = bcu.shape[0]
    d = D_MODEL
    tm = _tile(s, TM)
    hb = tm // HALO
    nt = s // tm

    def body(bcu_ref, prev_ref, next_ref, dz_ref, dznext_ref, w_ref, dbcu_ref, dw_ref, pbuf, dbuf):
        i = pl.program_id(0)

        @pl.when(i == 0)
        def _():
            dw_ref[...] = jnp.zeros_like(dw_ref)

        gb = bcu_ref[:, 0:d].astype(F32)
        gc = bcu_ref[:, d:2 * d].astype(F32)
        uu = bcu_ref[:, 2 * d:3 * d].astype(F32)
        p = gc * uu
        pprev = prev_ref[:, d:2 * d].astype(F32) * prev_ref[:, 2 * d:3 * d].astype(F32)
        pbuf[0:HALO, :] = jnp.where(i > 0, pprev, 0.0)
        pbuf[HALO:HALO + tm, :] = p
        p1 = pbuf[HALO - 1:HALO - 1 + tm, :]
        p2 = pbuf[HALO - 2:HALO - 2 + tm, :]
        cv = w_ref[2:3, :] * p + w_ref[1:2, :] * p1 + w_ref[0:1, :] * p2
        dz_t = dz_ref[...]
        dcv = dz_t * gb
        dcv_next = dznext_ref[...] * next_ref[:, 0:d].astype(F32)
        dbuf[0:tm, :] = dcv
        dbuf[tm:tm + HALO, :] = jnp.where(i < nt - 1, dcv_next, 0.0)
        dp = w_ref[2:3, :] * dcv + w_ref[1:2, :] * dbuf[1:1 + tm, :] + w_ref[0:1, :] * dbuf[2:2 + tm, :]
        dw_ref[2:3, :] += jnp.sum(dcv * p, axis=0, keepdims=True)
        dw_ref[1:2, :] += jnp.sum(dcv * p1, axis=0, keepdims=True)
        dw_ref[0:1, :] += jnp.sum(dcv * p2, axis=0, keepdims=True)
        dbcu_ref[:, 0:d] = (dz_t * cv).astype(BF16)
        dbcu_ref[:, d:2 * d] = (dp * uu).astype(BF16)
        dbcu_ref[:, 2 * d:3 * d] = (dp * gc).astype(BF16)

    nxt = lambda i: (jnp.minimum((i + 1) * hb, s // HALO - 1), 0)
    return pl.pallas_call(
        body, name=name, grid=(nt,),
        in_specs=[pl.BlockSpec((tm, 3 * d), lambda i: (i, 0)),
                  pl.BlockSpec((HALO, 3 * d), lambda i: (jnp.maximum(i * hb - 1, 0), 0)),
                  pl.BlockSpec((HALO, 3 * d), nxt),
                  pl.BlockSpec((tm, d), lambda i: (i, 0)),
                  pl.BlockSpec((HALO, d), nxt),
                  pl.BlockSpec((3, d), lambda i: (0, 0))],
        out_specs=[pl.BlockSpec((tm, 3 * d), lambda i: (i, 0)), pl.BlockSpec((3, d), lambda i: (0, 0))],
        out_shape=[jax.ShapeDtypeStruct((s, 3 * d), BF16), jax.ShapeDtypeStruct((3, d), F32)],
        scratch_shapes=[pltpu.VMEM((tm + HALO, d), F32), pltpu.VMEM((tm + HALO, d), F32)],
        compiler_params=_cparams("arbitrary"),
    )(bcu, bcu, bcu, dz, dz, conv_w)


def _mla_prep(a, g_qa, g_kva, w_uq, w_ukv, g_q, g_k, cc, sa, sb, *, name):
    s = a.shape[0]
    ts = _tile(s, T_PREP)

    def body(a_ref, gqa_ref, gkva_ref, wuq_ref, wukv_ref, gq_ref, gk_ref, cc_ref, sa_ref, sb_ref,
             cq_ref, ckv_ref, q_ref, k_ref, v_ref):
        xq, _ = _rms(a_ref[:, 0:Q_RANK], Q_RANK)
        cq = (xq * gqa_ref[...]).astype(BF16)
        cq_ref[...] = cq
        xkv, _ = _rms(a_ref[:, Q_RANK:Q_RANK + KV_RANK], KV_RANK)
        ckv = (xkv * gkva_ref[...]).astype(BF16)
        ckv_ref[...] = ckv
        kpe = a_ref[:, Q_RANK + KV_RANK:DOWN_PAD]
        kpe_ss = jnp.sum(kpe * kpe, axis=-1, keepdims=True)
        cc_t, sa_t, sb_t = cc_ref[...], sa_ref[...], sb_ref[...]
        gq = gq_ref[...]
        gk = gk_ref[...]
        for h in range(N_HEADS):
            cols = slice(h * QK_PAD, (h + 1) * QK_PAD)
            qhat, _ = _rms(_dot(cq, wuq_ref[:, cols]), QK_DIM)
            qn = qhat * (gq * SCALE_LOG2E)
            q_ref[h, :, 0:QK_NOPE] = qn[:, 0:QK_NOPE].astype(BF16)
            q_ref[h, :, QK_NOPE:QK_PAD] = _rope(qn[:, QK_NOPE:QK_PAD], cc_t, sa_t, sb_t).astype(BF16)
            kvr = _dot(ckv, wukv_ref[:, cols])
            kn = kvr[:, 0:QK_NOPE]
            rk = lax.rsqrt((jnp.sum(kn * kn, axis=-1, keepdims=True) + kpe_ss) * (1.0 / QK_DIM) + EPS)
            k_ref[h, :, 0:QK_NOPE] = (kn * rk * gk[:, 0:QK_NOPE]).astype(BF16)
            k_ref[h, :, QK_NOPE:QK_PAD] = _rope(kpe * rk * gk[:, QK_NOPE:QK_PAD], cc_t, sa_t, sb_t).astype(BF16)
            v_ref[h, :, 0:V_DIM] = kvr[:, QK_NOPE:QK_PAD].astype(BF16)
            v_ref[h, :, V_DIM:2 * V_DIM] = jnp.ones((ts, V_DIM), BF16)

    row = lambda i: (i, 0)
    fixed = lambda i: (0, 0)
    head = lambda i: (0, i, 0)
    return pl.pallas_call(
        body, name=name, grid=(s // ts,),
        in_specs=[pl.BlockSpec((ts, DOWN_PAD), row), pl.BlockSpec((1, Q_RANK), fixed), pl.BlockSpec((1, KV_RANK), fixed),
                  pl.BlockSpec((Q_RANK, N_HEADS * QK_PAD), fixed), pl.BlockSpec((KV_RANK, N_HEADS * QK_PAD), fixed),
                  pl.BlockSpec((1, QK_PAD), fixed), pl.BlockSpec((1, QK_PAD), fixed),
                  pl.BlockSpec((ts, 128), row), pl.BlockSpec((ts, 128), row), pl.BlockSpec((ts, 128), row)],
        out_specs=[pl.BlockSpec((ts, Q_RANK), row), pl.BlockSpec((ts, KV_RANK), row),
                   pl.BlockSpec((N_HEADS, ts, QK_PAD), head), pl.BlockSpec((N_HEADS, ts, QK_PAD), head),
                   pl.BlockSpec((N_HEADS, ts, 2 * V_DIM), head)],
        out_shape=[jax.ShapeDtypeStruct((s, Q_RANK), BF16), jax.ShapeDtypeStruct((s, KV_RANK), BF16),
                   jax.ShapeDtypeStruct((N_HEADS, s, QK_PAD), BF16), jax.ShapeDtypeStruct((N_HEADS, s, QK_PAD), BF16),
                   jax.ShapeDtypeStruct((N_HEADS, s, 2 * V_DIM), BF16)],
        compiler_params=_cparams("parallel"),
    )(a, g_qa, g_kva, w_uq, w_ukv, g_q, g_k, cc, sa, sb)


def _mla_prep_bwd(a, g_qa, g_kva, w_uq, w_ukv, g_q, g_k, cc, sa, sb, dq, dk, dv, *, name):
    s = a.shape[0]
    ts = _tile(s, T_PREP)

    def body(a_ref, gqa_ref, gkva_ref, wuq_ref, wukv_ref, gq_ref, gk_ref, cc_ref, sa_ref, sb_ref,
             dq_ref, dk_ref, dv_ref, dqr_ref, dkvr_ref, da_ref, dgq_ref, dgk_ref, dgqa_ref, dgkva_ref):
        @pl.when(pl.program_id(0) == 0)
        def _():
            dgq_ref[...] = jnp.zeros_like(dgq_ref)
            dgk_ref[...] = jnp.zeros_like(dgk_ref)
            dgqa_ref[...] = jnp.zeros_like(dgqa_ref)
            dgkva_ref[...] = jnp.zeros_like(dgkva_ref)

        xq, r_q = _rms(a_ref[:, 0:Q_RANK], Q_RANK)
        cq = (xq * gqa_ref[...]).astype(BF16)
        xkv, r_kv = _rms(a_ref[:, Q_RANK:Q_RANK + KV_RANK], KV_RANK)
        ckv = (xkv * gkva_ref[...]).astype(BF16)
        kpe = a_ref[:, Q_RANK + KV_RANK:DOWN_PAD]
        kpe_ss = jnp.sum(kpe * kpe, axis=-1, keepdims=True)
        cc_t, sa_t, sb_t = cc_ref[...], sa_ref[...], sb_ref[...]
        gq = gq_ref[...]
        gk = gk_ref[...]
        dcq = jnp.zeros((ts, Q_RANK), F32)
        dckv = jnp.zeros((ts, KV_RANK), F32)
        dkpe = jnp.zeros((ts, 128), F32)
        dgq = jnp.zeros((1, QK_PAD), F32)
        dgk_n = jnp.zeros((1, QK_NOPE), F32)
        dgk_p = jnp.zeros((1, 128), F32)
        for h in range(N_HEADS):
            cols = slice(h * QK_PAD, (h + 1) * QK_PAD)
            qhat, rq = _rms(_dot(cq, wuq_ref[:, cols]), QK_DIM)
            dqn = jnp.concatenate(
                [dq_ref[h, :, 0:QK_NOPE], _rope_t(dq_ref[h, :, QK_NOPE:QK_PAD], cc_t, sa_t, sb_t)], axis=1)
            dgq = dgq + jnp.sum(dqn * qhat, axis=0, keepdims=True)
            dqr = _rms_bwd(qhat, rq, dqn * gq, QK_DIM).astype(BF16)
            dqr_ref[:, cols] = dqr
            dcq = dcq + _dot_nt(dqr, wuq_ref[:, cols])
            kn = _dot(ckv, wukv_ref[:, h * QK_PAD:h * QK_PAD + QK_NOPE])
            rk = lax.rsqrt((jnp.sum(kn * kn, axis=-1, keepdims=True) + kpe_ss) * (1.0 / QK_DIM) + EPS)
            khat_n = kn * rk
            khat_p = kpe * rk
            dkn = dk_ref[h, :, 0:QK_NOPE]
            dkp = _rope_t(dk_ref[h, :, QK_NOPE:QK_PAD], cc_t, sa_t, sb_t)
            dgk_n = dgk_n + jnp.sum(dkn * khat_n, axis=0, keepdims=True)
            dgk_p = dgk_p + jnp.sum(dkp * khat_p, axis=0, keepdims=True)
            dxn = dkn * gk[:, 0:QK_NOPE]
            dxp = dkp * gk[:, QK_NOPE:QK_PAD]
            mean = (jnp.sum(dxn * khat_n, axis=-1, keepdims=True)
                    + jnp.sum(dxp * khat_p, axis=-1, keepdims=True)) * (1.0 / QK_DIM)
            dkpe = dkpe + rk * (dxp - khat_p * mean)
            dkvr = jnp.concatenate([rk * (dxn - khat_n * mean), dv_ref[h, :, :]], axis=1).astype(BF16)
            dkvr_ref[:, cols] = dkvr
            dckv = dckv + _dot_nt(dkvr, wukv_ref[:, cols])
        dgq_ref[...] += dgq
        dgk_ref[:, 0:QK_NOPE] += dgk_n
        dgk_ref[:, QK_NOPE:QK_PAD] += dgk_p
        dgqa_ref[...] += jnp.sum(dcq * xq, axis=0, keepdims=True)
        dgkva_ref[...] += jnp.sum(dckv * xkv, axis=0, keepdims=True)
        da_ref[:, 0:Q_RANK] = _rms_bwd(xq, r_q, dcq * gqa_ref[...], Q_RANK).astype(BF16)
        da_ref[:, Q_RANK:Q_RANK + KV_RANK] = _rms_bwd(xkv, r_kv, dckv * gkva_ref[...], KV_RANK).astype(BF16)
        da_ref[:, Q_RANK + KV_RANK:DOWN_PAD] = dkpe.astype(BF16)

    row = lambda i: (i, 0)
    fixed = lambda i: (0, 0)
    head = lambda i: (0, i, 0)
    wide = N_HEADS * QK_PAD
    return pl.pallas_call(
        body, name=name, grid=(s // ts,),
        in_specs=[pl.BlockSpec((ts, DOWN_PAD), row), pl.BlockSpec((1, Q_RANK), fixed), pl.BlockSpec((1, KV_RANK), fixed),
                  pl.BlockSpec((Q_RANK, wide), fixed), pl.BlockSpec((KV_RANK, wide), fixed),
                  pl.BlockSpec((1, QK_PAD), fixed), pl.BlockSpec((1, QK_PAD), fixed),
                  pl.BlockSpec((ts, 128), row), pl.BlockSpec((ts, 128), row), pl.BlockSpec((ts, 128), row),
                  pl.BlockSpec((N_HEADS, ts, QK_PAD), head), pl.BlockSpec((N_HEADS, ts, QK_PAD), head),
                  pl.BlockSpec((N_HEADS, ts, V_DIM), head)],
        out_specs=[pl.BlockSpec((ts, wide), row), pl.BlockSpec((ts, wide), row), pl.BlockSpec((ts, DOWN_PAD), row),
                   pl.BlockSpec((1, QK_PAD), fixed), pl.BlockSpec((1, QK_PAD), fixed),
                   pl.BlockSpec((1, Q_RANK), fixed), pl.BlockSpec((1, KV_RANK), fixed)],
        out_shape=[jax.ShapeDtypeStruct((s, wide), BF16), jax.ShapeDtypeStruct((s, wide), BF16),
                   jax.ShapeDtypeStruct((s, DOWN_PAD), BF16),
                   jax.ShapeDtypeStruct((1, QK_PAD), F32), jax.ShapeDtypeStruct((1, QK_PAD), F32),
                   jax.ShapeDtypeStruct((1, Q_RANK), F32), jax.ShapeDtypeStruct((1, KV_RANK), F32)],
        compiler_params=_cparams("arbitrary"),
    )(a, g_qa, g_kva, w_uq, w_ukv, g_q, g_k, cc, sa, sb, dq, dk, dv)


def _flash_fwd(q, k, v, pos_col, pos_row, *, name, gather=None):
    nh, s, _ = q.shape
    tq = _tile(s, FWD_TQ)
    tk = _tile(s, FWD_TK)
    sq = tq // ATTN_CHAINS
    nq = s // tq
    names = list(gather or {})
    ng = len(names)

    def body(q_ref, k_ref, v_ref, pq_ref, pk_ref, *rest):
        o_ref, lse_ref = rest[ng:ng + 2]
        m_sc, acc_sc = rest[2 * ng + 2:2 * ng + 4]
        qb = pl.program_id(1)
        if ng:
            sends, recvs = _gather_ici_copies(rest[ng + 2:2 * ng + 2], names, *rest[2 * ng + 4:], base=0, stride=3)

            @pl.when((pl.program_id(0) == 0) & (qb == 0))
            def _():
                for cp in sends:
                    cp.start()

        m_sc[...] = jnp.full_like(m_sc, NEG)
        acc_sc[...] = jnp.zeros_like(acc_sc)

        def step(kb, masked):
            keys = pl.ds(pl.multiple_of(kb * tk, tk), tk)
            kt = k_ref[0, keys, :]
            vt = v_ref[0, keys, :]
            scores = [_dot_nt(q_ref[0, u * sq:(u + 1) * sq, :], kt) for u in range(ATTN_CHAINS)]
            for u in range(ATTN_CHAINS):
                rows = slice(u * sq, (u + 1) * sq)
                sc = scores[u]
                if masked:
                    sc = jnp.where(pq_ref[rows, :] >= pk_ref[:, keys], sc, NEG)
                m_prev = m_sc[rows, :]
                m_new = jnp.maximum(m_prev, jnp.max(sc, axis=-1, keepdims=True))
                alpha = jnp.exp2(m_prev - m_new)
                p = jnp.exp2(sc - jnp.tile(m_new, (1, tk // 128)))
                acc_sc[rows, :] = jnp.tile(alpha, (1, 2)) * acc_sc[rows, :] + _dot(p.astype(BF16), vt)
                m_sc[rows, :] = m_new

        n_before = (qb * tq) // tk
        n_seen = (qb * tq + tq - 1) // tk + 1
        lax.fori_loop(0, n_before, lambda kb, c: (step(kb, False), c)[1], 0)
        lax.fori_loop(n_before, n_seen, lambda kb, c: (step(kb, True), c)[1], 0)
        l = acc_sc[:, V_DIM:2 * V_DIM]
        o_ref[...] = (acc_sc[:, 0:V_DIM] / l).astype(BF16)
        lse = m_sc[...] * (1.0 / LOG2E) + jnp.log(l)
        lse_ref[0] = lse.T[0:1, :]

        if ng:
            @pl.when((pl.program_id(0) == nh - 1) & (qb == nq - 1))
            def _():
                for cp in recvs:
                    cp.wait_recv()
                for cp in sends:
                    cp.wait_send()

    arrays = [gather[nm] for nm in names]
    out = pl.pallas_call(
        body, name=name, grid=(nh, nq),
        in_specs=[pl.BlockSpec((1, tq, QK_PAD), lambda h, qb: (h, qb, 0)),
                  pl.BlockSpec((1, s, QK_PAD), lambda h, qb: (h, 0, 0)),
                  pl.BlockSpec((1, s, 2 * V_DIM), lambda h, qb: (h, 0, 0)),
                  pl.BlockSpec((tq, 1), lambda h, qb: (qb, 0)),
                  pl.BlockSpec((1, s), lambda h, qb: (0, 0))] + [ANY] * ng,
        out_specs=[pl.BlockSpec((tq, V_DIM), lambda h, qb: (qb, h)),
                   pl.BlockSpec((1, 1, tq), lambda h, qb: (h, 0, qb))] + [ANY] * ng,
        scratch_shapes=[pltpu.VMEM((tq, 128), F32), pltpu.VMEM((tq, 2 * V_DIM), F32)]
        + ([pltpu.SemaphoreType.DMA((3 * ng,)), pltpu.SemaphoreType.DMA((3 * ng,))] if ng else []),
        out_shape=[jax.ShapeDtypeStruct((s, nh * V_DIM), BF16), jax.ShapeDtypeStruct((nh, 1, s), F32)]
        + [jax.ShapeDtypeStruct(a.shape, a.dtype) for a in arrays],
        input_output_aliases={5 + i: 2 + i for i in range(ng)},
        compiler_params=_cparams("arbitrary", "arbitrary") if ng else _cparams("parallel", "parallel"),
    )(q, k, v, pos_col, pos_row, *arrays)
    return out[0], out[1], dict(zip(names, out[2:]))


def _attn_delta(do, o, *, name):
    s = do.shape[0]
    tm = _tile(s, TM)

    def body(do_ref, o_ref, d_ref):
        for h in range(N_HEADS):
            cols = slice(h * V_DIM, (h + 1) * V_DIM)
            prod = do_ref[:, cols].astype(F32) * o_ref[:, cols].astype(F32)
            d_ref[h] = jnp.sum(prod.T, axis=0, keepdims=True)

    return pl.pallas_call(
        body, name=name, grid=(s // tm,),
        in_specs=[pl.BlockSpec((tm, N_HEADS * V_DIM), lambda i: (i, 0))] * 2,
        out_specs=pl.BlockSpec((N_HEADS, 1, tm), lambda i: (0, 0, i)),
        out_shape=jax.ShapeDtypeStruct((N_HEADS, 1, s), F32),
        compiler_params=_cparams("parallel"),
    )(do, o)


def _flash_bwd(q, k, v, do, lse_row, delta_row, pos_col, pos_row, *, name, scatter=None):
    nh, s, _ = q.shape
    tq = _tile(s, BWD_TQ)
    tk = _tile(s, BWD_TK)
    nq, nk = s // tq, s // tk
    sk = tk // ATTN_CHAINS
    names = list(scatter or {})
    ng = len(names)

    def body(q_ref, k_ref, v_ref, do_ref, lse_ref, delta_ref, pq_ref, pk_ref, *rest):
        dq_ref, dk_ref, dv_ref = rest[ng:ng + 3]
        dk_sc, dv_sc = rest[2 * ng + 3:2 * ng + 5]
        kb = pl.program_id(1)
        if ng:
            copies = _scatter_copies(rest[:ng], rest[ng + 3:2 * ng + 3], names, *rest[2 * ng + 5:])

            @pl.when((pl.program_id(0) == 0) & (kb == 0))
            def _():
                for cp in copies:
                    cp.start()

        @pl.when(kb == 0)
        def _():
            dq_ref[...] = jnp.zeros_like(dq_ref)

        dk_sc[...] = jnp.zeros_like(dk_sc)
        dv_sc[...] = jnp.zeros_like(dv_sc)

        def step(qb, masked):
            trim = masked and tq == tk
            start = pl.multiple_of(qb * tq, tq)
            offs = [u * sk if trim else 0 for u in range(ATTN_CHAINS)]
            qss = [pl.ds(start + offs[u], tq - offs[u]) for u in range(ATTN_CHAINS)]
            qts = [q_ref[0, qss[u], :] for u in range(ATTN_CHAINS)]
            dots = [do_ref[qss[u], :] for u in range(ATTN_CHAINS)]
            sts = [_dot_nt(k_ref[0, u * sk:(u + 1) * sk, :], qts[u]) for u in range(ATTN_CHAINS)]
            dpts = [_dot_nt(v_ref[0, u * sk:(u + 1) * sk, :], dots[u]) for u in range(ATTN_CHAINS)]
            parts = []
            for u in range(ATTN_CHAINS):
                rows = slice(u * sk, (u + 1) * sk)
                pt = jnp.exp2(sts[u] - lse_ref[0, :, qss[u]] * LOG2E)
                if masked:
                    pt = jnp.where(pq_ref[:, qss[u]] >= pk_ref[rows, :], pt, 0.0)
                dv_sc[rows, :] += _dot(pt.astype(BF16), dots[u])
                dst = (pt * (dpts[u] - delta_ref[0, :, qss[u]])).astype(BF16)
                dk_sc[rows, :] += _dot(dst, qts[u])
                parts.append(_dot_tn(dst, k_ref[0, rows, :]))
            if trim:
                for u in range(ATTN_CHAINS):
                    dq_ref[0, qss[u], :] += parts[u]
            else:
                dq_ref[0, qss[0], :] += functools.reduce(lambda a, b: a + b, parts)

        q_first = (kb * tk) // tq
        q_clear = (kb * tk + tk - 1) // tq + 1
        lax.fori_loop(q_first, q_clear, lambda qb, c: (step(qb, True), c)[1], 0)
        lax.fori_loop(q_clear, nq, lambda qb, c: (step(qb, False), c)[1], 0)
        dk_ref[0] = dk_sc[...] * (1.0 / LOG2E)
        dv_ref[0] = dv_sc[...]

        @pl.when(kb == nk - 1)
        def _():
            dq_ref[...] = dq_ref[...] * SCALE

        if ng:
            @pl.when((pl.program_id(0) == nh - 1) & (kb == nk - 1))
            def _():
                for cp in copies:
                    cp.wait()

    arrays = [scatter[nm] for nm in names]
    out = pl.pallas_call(
        body, name=name, grid=(nh, nk),
        in_specs=[pl.BlockSpec((1, s, QK_PAD), lambda h, kb: (h, 0, 0)),
                  pl.BlockSpec((1, tk, QK_PAD), lambda h, kb: (h, kb, 0)),
                  pl.BlockSpec((1, tk, V_DIM), lambda h, kb: (h, kb, 0)),
                  pl.BlockSpec((s, V_DIM), lambda h, kb: (0, h)),
                  pl.BlockSpec((1, 1, s), lambda h, kb: (h, 0, 0)),
                  pl.BlockSpec((1, 1, s), lambda h, kb: (h, 0, 0)),
                  pl.BlockSpec((1, s), lambda h, kb: (0, 0)),
                  pl.BlockSpec((tk, 1), lambda h, kb: (kb, 0))] + [ANY] * ng,
        out_specs=[pl.BlockSpec((1, s, QK_PAD), lambda h, kb: (h, 0, 0)),
                   pl.BlockSpec((1, tk, QK_PAD), lambda h, kb: (h, kb, 0)),
                   pl.BlockSpec((1, tk, V_DIM), lambda h, kb: (h, kb, 0))] + [ANY] * ng,
        scratch_shapes=[pltpu.VMEM((tk, QK_PAD), F32), pltpu.VMEM((tk, V_DIM), F32)]
        + ([pltpu.SemaphoreType.DMA((3 * ng,)), pltpu.SemaphoreType.DMA((3 * ng,))] if ng else []),
        out_shape=[jax.ShapeDtypeStruct((nh, s, QK_PAD), F32), jax.ShapeDtypeStruct((nh, s, QK_PAD), F32),
                   jax.ShapeDtypeStruct((nh, s, V_DIM), F32)] + _scatter_out_shapes(names, arrays),
        compiler_params=_cparams("arbitrary", "arbitrary"),
    )(q, k, v, do, lse_row, delta_row, pos_row, pos_col, *arrays)
    return out[0], out[1], out[2], dict(zip(names, out[3:]))


def _loss_head(y, target, *, name):
    s, d = y.shape
    tm = _tile(s, TM)
    nt = s // tm

    def body(y_ref, t_ref, dy_ref, loss_ref, acc):
        i = pl.program_id(0)

        @pl.when(i == 0)
        def _():
            acc[...] = jnp.zeros_like(acc)

        e = y_ref[...] - t_ref[...]
        dy_ref[...] = e * (1.0 / d)
        acc[...] += jnp.sum((e * e).reshape(tm // 8, 8, d), axis=0)

        @pl.when(i == nt - 1)
        def _():
            loss_ref[...] = jnp.full((1, 128), 0.5 / d, F32) * jnp.sum(acc[...])

    return pl.pallas_call(
        body, name=name, grid=(nt,),
        in_specs=[pl.BlockSpec((tm, d), lambda i: (i, 0))] * 2,
        out_specs=[pl.BlockSpec((tm, d), lambda i: (i, 0)), pl.BlockSpec((1, 128), lambda i: (0, 0))],
        out_shape=[jax.ShapeDtypeStruct((s, d), F32), jax.ShapeDtypeStruct((1, 128), F32)],
        scratch_shapes=[pltpu.VMEM((8, d), F32)],
        compiler_params=_cparams("arbitrary"),
    )(y, target)


def _adamw(w, g, m, v, *, name):
    r, c = w.shape
    tr = _tile(r, 512) if r % 8 == 0 else r

    def body(w_ref, g_ref, m_ref, v_ref, d_ref, nm_ref, nv_ref):
        g_t = g_ref[...]
        nm = ADAM_B1 * m_ref[...] + (1.0 - ADAM_B1) * g_t
        nv = ADAM_B2 * v_ref[...] + (1.0 - ADAM_B2) * (g_t * g_t)
        m_hat = nm / (1.0 - ADAM_B1 ** ADAM_STEP)
        v_hat = nv / (1.0 - ADAM_B2 ** ADAM_STEP)
        d_ref[...] = -ADAM_LR * (m_hat / (jnp.sqrt(v_hat) + ADAM_EPS) + ADAM_WD * w_ref[...])
        nm_ref[...] = nm
        nv_ref[...] = nv

    spec = pl.BlockSpec((tr, c), lambda i: (i, 0))
    return pl.pallas_call(
        body, name=name, grid=(r // tr,), in_specs=[spec] * 4, out_specs=[spec] * 3,
        out_shape=[jax.ShapeDtypeStruct((r, c), F32)] * 3,
        compiler_params=_cparams("parallel"),
    )(w, g, m, v)


def _place():
    return lax.axis_index("x"), lax.axis_index("y"), lax.axis_index("c")


def _other_chips(x, y):
    return [(1 - x, y), (x, 1 - y), (1 - x, 1 - y)]


BIG = {
    "attn_w_down": ((2, 1024, 448), 1), "attn_w_uq": ((2, 256, 1536), 2), "attn_w_ukv": ((2, 128, 2048), 2),
    "attn_w_o": ((2, 1024, 1024), 1), "conv_w_in": ((2, 1024, 3072), 2),
    "conv_w_out": ((2, 1024, 1024), 1), "mlp_w1": ((4, 1024, 4096), 2), "mlp_w2": ((4, 4096, 1024), 1),
}
CONV_W = (2, 3, 1024)


def _shard_shape(name):
    shape, axis = BIG[name]
    return tuple(n // N_CHIPS if i == axis else n for i, n in enumerate(shape))


def _band(ref, name, layers, chip):
    shape, axis = BIG[name]
    width = shape[axis] // N_CHIPS
    if axis == 1:
        return ref.at[layers, pl.ds(chip * width, width), :]
    return ref.at[layers, :, pl.ds(chip * width, width)]


def _half(name, c):
    hl = BIG[name][0][0] // 2
    return pl.ds(c * hl, hl)


def _place_own(w, nm, chip, *, name):
    shape, axis = BIG[nm]
    layers, rows, cols = w.shape
    tr = _sum_rows(rows, cols)
    nrb = rows // tr
    if axis == 1:
        band = lambda l, i, ch: (l, ch[0] * nrb + i, 0)
    else:
        band = lambda l, i, ch: (l, i, ch[0])

    def body(chip_ref, w_ref, o_ref):
        o_ref[...] = w_ref[...].astype(BF16)

    return pl.pallas_call(
        body, name=name,
        grid_spec=pltpu.PrefetchScalarGridSpec(
            num_scalar_prefetch=1, grid=(layers, nrb),
            in_specs=[pl.BlockSpec((1, tr, cols), lambda l, i, ch: (l, i, 0))],
            out_specs=pl.BlockSpec((1, tr, cols), band)),
        out_shape=jax.ShapeDtypeStruct(shape, BF16),
        compiler_params=_cparams("parallel", "parallel"),
    )(chip, w)


def _gather_copies(outs, names, send_sems, recv_sems, *, base, stride, to_sibling):
    x, y, c = _place()
    me = 2 * x + y

    def copy(k, ref, nm, layers, chip, to):
        band = _band(ref, nm, layers, chip)
        return pltpu.make_async_remote_copy(
            src_ref=band, dst_ref=band, send_sem=send_sems.at[k], recv_sem=recv_sems.at[k],
            device_id=to, device_id_type=MESH)

    sends, recvs = [], []
    for i, nm in enumerate(names):
        for j, (cx, cy) in enumerate(_other_chips(x, y)):
            k = base + stride * i + j
            if to_sibling:
                sends.append(copy(k, outs[i], nm, _half(nm, c), 2 * cx + cy, (x, y, 1 - c)))
                recvs.append(copy(k, outs[i], nm, _half(nm, 1 - c), 2 * cx + cy, (x, y, c)))
            else:
                sends.append(copy(k, outs[i], nm, _half(nm, c), me, (cx, cy, c)))
                recvs.append(copy(k, outs[i], nm, _half(nm, c), 2 * cx + cy, (x, y, c)))
    return sends, recvs


def _gather_ici_copies(outs, names, send_sems, recv_sems, *, base, stride):
    return _gather_copies(outs, names, send_sems, recv_sems, base=base, stride=stride, to_sibling=False)


def _gather_weights(fulls, *, name, ici=True):
    names = list(fulls)
    n = len(names)

    def body(*refs):
        outs = refs[n:2 * n]
        sems = refs[2 * n:]
        sent = []
        if ici:
            sends, recvs = _gather_copies(outs, names, *sems, base=0, stride=6, to_sibling=False)
            for cp in sends:
                cp.start()
            for cp in recvs:
                cp.wait_recv()
            sent += sends
        sends, recvs = _gather_copies(outs, names, *sems, base=3, stride=6, to_sibling=True)
        for cp in sends:
            cp.start()
        for cp in recvs:
            cp.wait_recv()
        for cp in sent + sends:
            cp.wait_send()

    arrays = [fulls[nm] for nm in names]
    out = pl.pallas_call(
        body, name=name, in_specs=[ANY] * n, out_specs=[ANY] * n,
        out_shape=[jax.ShapeDtypeStruct(a.shape, a.dtype) for a in arrays],
        input_output_aliases={i: i for i in range(n)},
        scratch_shapes=[pltpu.SemaphoreType.DMA((6 * n,)), pltpu.SemaphoreType.DMA((6 * n,))],
    )(*arrays)
    return dict(zip(names, out))


def _swap_halves(grads, *, name):
    names = list(grads)
    n = len(names)

    def body(*refs):
        copies = _swap_copies(refs[:n], refs[n:2 * n], names, *refs[2 * n:])
        for cp in copies:
            cp.start()
        for cp in copies:
            cp.wait()

    arrays = [grads[nm] for nm in names]
    out = pl.pallas_call(
        body, name=name, in_specs=[ANY] * n, out_specs=[ANY] * n,
        out_shape=_swap_out_shapes(arrays), scratch_shapes=_swap_sems(n),
    )(*arrays)
    return dict(zip(names, out))


def _swap_copies(ins, outs, names, send_sems, recv_sems):
    x, y, c = _place()
    return [pltpu.make_async_remote_copy(
        src_ref=ins[i].at[_half(nm, 1 - c)], dst_ref=outs[i], send_sem=send_sems.at[i], recv_sem=recv_sems.at[i],
        device_id=(x, y, 1 - c), device_id_type=MESH) for i, nm in enumerate(names)]


def _swap_out_shapes(arrays):
    return [jax.ShapeDtypeStruct((a.shape[0] // 2,) + a.shape[1:], a.dtype) for a in arrays]


def _swap_sems(n):
    return [pltpu.SemaphoreType.DMA((n,)), pltpu.SemaphoreType.DMA((n,))] if n else []


def _all_steps(grid, at):
    cond = None
    for axis, size in enumerate(grid):
        this = pl.program_id(axis) == (0 if at == "first" else size - 1)
        cond = this if cond is None else cond & this
    return cond


def _start_at_first_step(copies, grid):
    @pl.when(_all_steps(grid, "first"))
    def _():
        for cp in copies:
            cp.start()


def _wait_at_last_step(copies, grid):
    @pl.when(_all_steps(grid, "last"))
    def _():
        for cp in copies:
            cp.wait()


def _sum_rows(rows, cols):
    t = rows
    while t * cols * 4 > SUM_BLOCK_BYTES and t % 16 == 0:
        t //= 2
    return t


def _chip_sum(g, r1, core, *, name):
    layers, rows, cols = g.shape
    hl = layers // 2
    tr = _sum_rows(rows, cols)

    def body(core_ref, g_ref, r_ref, o_ref):
        o_ref[...] = (g_ref[...] + r_ref[...]).astype(BF16)

    return pl.pallas_call(
        body, name=name,
        grid_spec=pltpu.PrefetchScalarGridSpec(
            num_scalar_prefetch=1, grid=(hl, rows // tr),
            in_specs=[pl.BlockSpec((1, tr, cols), lambda l, i, cr: (cr[0] * hl + l, i, 0)),
                      pl.BlockSpec((1, tr, cols), lambda l, i, cr: (l, i, 0))],
            out_specs=pl.BlockSpec((1, tr, cols), lambda l, i, cr: (l, i, 0))),
        out_shape=jax.ShapeDtypeStruct((hl, rows, cols), BF16),
        compiler_params=_cparams("parallel", "parallel"),
    )(core, g, r1)


def _chip_partials(grads, names, *, tag):
    core = lax.axis_index("c").astype(jnp.int32).reshape(1)
    r1 = _swap_halves({n: grads[n] for n in names}, name=f"grad_swap_halves_{tag}")
    return r1, {n: _chip_sum(grads[n], r1[n], core, name=f"grad_chip_sum_{n}") for n in names}


def _scatter_partials(partials):
    names = list(partials)
    n = len(names)

    def body(*refs):
        copies = _scatter_copies(refs[:n], refs[n:2 * n], names, *refs[2 * n:])
        for cp in copies:
            cp.start()
        for cp in copies:
            cp.wait()

    arrays = [partials[nm] for nm in names]
    out = pl.pallas_call(
        body, name="grad_scatter_partials", in_specs=[ANY] * n, out_specs=[ANY] * n,
        out_shape=_scatter_out_shapes(names, arrays),
        scratch_shapes=[pltpu.SemaphoreType.DMA((3 * n,)), pltpu.SemaphoreType.DMA((3 * n,))],
    )(*arrays)
    return dict(zip(names, out))


def _scatter_copies(ins, outs, names, send_sems, recv_sems):
    x, y, c = _place()
    copies = []
    for i, nm in enumerate(names):
        for j, (cx, cy) in enumerate(_other_chips(x, y)):
            copies.append(pltpu.make_async_remote_copy(
                src_ref=_band(ins[i], nm, slice(None), 2 * cx + cy), dst_ref=outs[i].at[j],
                send_sem=send_sems.at[3 * i + j], recv_sem=recv_sems.at[3 * i + j],
                device_id=(cx, cy, c), device_id_type=MESH))
    return copies


def _scatter_out_shapes(names, arrays):
    return [jax.ShapeDtypeStruct((3, a.shape[0]) + _shard_shape(nm)[1:], a.dtype) for nm, a in zip(names, arrays)]


def _final_sum(g, r1, r2, place, nm, *, name):
    (layers, _, _), axis = BIG[nm]
    hl = layers // 2
    _, rows, cols = _shard_shape(nm)
    tr = _sum_rows(rows, cols)
    nrb = rows // tr
    if axis == 1:
        blk = lambda l, i, pc: (l, pc[1] * nrb + i, 0)
    else:
        blk = lambda l, i, pc: (l, i, pc[1])

    def body(place_ref, g_ref, r1_ref, r2_ref, o_ref):
        acc = g_ref[...] + r1_ref[...]
        for j in range(3):
            acc = acc + r2_ref[j].astype(F32)
        o_ref[...] = acc

    return pl.pallas_call(
        body, name=name,
        grid_spec=pltpu.PrefetchScalarGridSpec(
            num_scalar_prefetch=1, grid=(hl, nrb),
            in_specs=[pl.BlockSpec((1, tr, cols), lambda l, i, pc: blk(pc[0] * hl + l, i, pc)),
                      pl.BlockSpec((1, tr, cols), lambda l, i, pc: blk(l, i, pc)),
                      pl.BlockSpec((3, 1, tr, cols), lambda l, i, pc: (0, l, i, 0))],
            out_specs=pl.BlockSpec((1, tr, cols), lambda l, i, pc: (pc[0] * hl + l, i, 0))),
        out_shape=jax.ShapeDtypeStruct((layers, rows, cols), F32),
        compiler_params=_cparams("parallel", "parallel"),
    )(place, g, r1, r2)


def _join_halves(shards):
    names = list(shards)
    n = len(names)

    def body(*refs):
        outs = refs[n:2 * n]
        send_sems, recv_sems = refs[2 * n:]
        x, y, c = _place()
        copies = []
        for i, nm in enumerate(names):
            mine = outs[i].at[_half(nm, c)]
            cp = pltpu.make_async_remote_copy(
                src_ref=mine, dst_ref=mine, send_sem=send_sems.at[i], recv_sem=recv_sems.at[i],
                device_id=(x, y, 1 - c), device_id_type=MESH)
            cp.start()
            copies.append(cp)
        for i, nm in enumerate(names):
            theirs = outs[i].at[_half(nm, 1 - c)]
            pltpu.make_async_remote_copy(
                src_ref=theirs, dst_ref=theirs, send_sem=send_sems.at[i], recv_sem=recv_sems.at[i],
                device_id=(x, y, 1 - c), device_id_type=MESH).wait_recv()
        for cp in copies:
            cp.wait_send()

    arrays = [shards[nm] for nm in names]
    out = pl.pallas_call(
        body, name="grad_join_halves", in_specs=[ANY] * n, out_specs=[ANY] * n,
        out_shape=[jax.ShapeDtypeStruct(a.shape, a.dtype) for a in arrays],
        input_output_aliases={i: i for i in range(n)},
        scratch_shapes=[pltpu.SemaphoreType.DMA((n,)), pltpu.SemaphoreType.DMA((n,))],
    )(*arrays)
    return dict(zip(names, out))


def _all_reduce_small(part, *, name):
    rows, cols = part.shape
    vm = pl.BlockSpec(memory_space=pltpu.VMEM)

    def body(p_ref, o_ref, land, send_sems, recv_sems):
        x, y, c = _place()
        me = 4 * x + 2 * y + c
        flips = [(dx, dy, dc) for dx in (0, 1) for dy in (0, 1) for dc in (0, 1)][1:]
        copies = []
        for k, (dx, dy, dc) in enumerate(flips):
            cp = pltpu.make_async_remote_copy(
                src_ref=p_ref, dst_ref=land.at[me], send_sem=send_sems.at[k], recv_sem=recv_sems.at[k],
                device_id=(1 - x if dx else x, 1 - y if dy else y, 1 - c if dc else c), device_id_type=MESH)
            cp.start()
            copies.append(cp)
        land[me] = p_ref[...]
        for cp in copies:
            cp.wait()
        acc = land[0]
        for j in range(1, 8):
            acc = acc + land[j]
        o_ref[...] = acc

    return pl.pallas_call(
        body, name=name, in_specs=[vm], out_specs=vm,
        out_shape=jax.ShapeDtypeStruct((rows, cols), F32),
        scratch_shapes=[pltpu.VMEM((8, rows, cols), F32), pltpu.SemaphoreType.DMA((7,)), pltpu.SemaphoreType.DMA((7,))],
    )(part)


SMALL = {"g_mix": (4, 1024), "g_mlp": (4, 1024), "attn_g_q_a": (2, 256), "attn_g_kv_a": (2, 128),
         "attn_g_qnorm": (2, 192), "attn_g_knorm": (2, 192)}
SMALL_GRADS = {**SMALL, "conv_w": CONV_W}
WEIGHT_ORDER = ["g_mix", "g_mlp", "attn_w_down", "attn_g_q_a", "attn_g_kv_a", "attn_w_uq", "attn_w_ukv",
                "attn_g_qnorm", "attn_g_knorm", "attn_w_o", "conv_w_in", "conv_w", "conv_w_out", "mlp_w1", "mlp_w2"]


def _prod(shape):
    n = 1
    for v in shape:
        n *= v
    return n


def _pack_small(parts, table):
    flat = [parts[n].reshape(-1) for n in table]
    size = sum(_prod(s) for s in table.values())
    rows = -(-size // (8 * 128)) * 8
    flat.append(jnp.zeros((rows * 128 - size,), F32))
    return jnp.concatenate(flat).reshape(rows, 128)


def _unpack_small(buf, table):
    flat = buf.reshape(-1)
    out, off = {}, 0
    for n, shp in table.items():
        out[n] = flat[off:off + _prod(shp)].reshape(shp)
        off += _prod(shp)
    return out


def _rope_tables(positions):
    inv_freq = ROPE_THETA ** (-jnp.arange(0, QK_ROPE, 2, dtype=F32) / QK_ROPE)
    ang = positions.astype(F32)[:, None] * inv_freq
    cos, sin = jnp.cos(ang), jnp.sin(ang)
    z32 = jnp.zeros_like(cos)
    z64 = jnp.zeros((positions.shape[0], 64), F32)
    cc = jnp.concatenate([cos, cos, z64], axis=1)
    sa = jnp.concatenate([-sin, z32, z64], axis=1)
    sb = jnp.concatenate([z32, sin, z64], axis=1)
    return cc, sa, sb


def _pad_heads(w, width):
    k = w.shape[0]
    w = w.reshape(k, N_HEADS, width)
    return jnp.pad(w, ((0, 0), (0, 0), (0, QK_PAD - width))).reshape(k, N_HEADS * QK_PAD)


EARLY = ("mlp_w1", "mlp_w2", "conv_w_in", "conv_w_out")
LATE = ("attn_w_down", "attn_w_uq", "attn_w_ukv", "attn_w_o")
GATHER_LATER = EARLY


def _local_step(x, positions, target, wb, gains, later=None):
    s = x.shape[0]
    cc, sa, sb = _rope_tables(positions)
    pos_col = positions.reshape(s, 1)
    pos_row = positions.reshape(1, s)

    saved = []
    for i in range(4):
        j = i // 2
        g_mix = gains["g_mix"][i:i + 1]
        g_mlp = gains["g_mlp"][i:i + 1]
        if i % 2 == 0:
            w_down = jnp.pad(wb["attn_w_down"][j], ((0, 0), (0, DOWN_PAD - DOWN_DIM)))
            w_uq = _pad_heads(wb["attn_w_uq"][j], QK_DIM)
            w_ukv = wb["attn_w_ukv"][j]
            g_qa = gains["attn_g_q_a"][j:j + 1]
            g_kva = gains["attn_g_kv_a"][j:j + 1]
            g_q = jnp.pad(gains["attn_g_qnorm"][j:j + 1], ((0, 0), (0, QK_PAD - QK_DIM)))
            g_k = jnp.pad(gains["attn_g_knorm"][j:j + 1], ((0, 0), (0, QK_PAD - QK_DIM)))
            h, a = _norm_mm(x, g_mix, w_down, out_dtype=F32, name=f"mla_down_{j}")
            cq, ckv, q, k, v = _mla_prep(a, g_qa, g_kva, w_uq, w_ukv, g_q, g_k, cc, sa, sb, name=f"mla_prep_{j}")
            o, lse, got = _flash_fwd(q, k, v, pos_col, pos_row, name=f"flash_fwd_{j}",
                                     gather=later if i == 0 else None)
            if got:
                wb = {**wb, **_gather_weights(got, name="gather_later_forward", ici=False)}
            x_mid = _mm_nn(o, wb["attn_w_o"], layer=j, out_dtype=F32, residual=x, name=f"mla_out_{j}")
            mix = dict(h=h, a=a, cq=cq, ckv=ckv, q=q, k=k, v=v, o=o, lse=lse, w_down=w_down, w_uq=w_uq, w_ukv=w_ukv,
                       g_qa=g_qa, g_kva=g_kva, g_q=g_q, g_k=g_k)
        else:
            h, bcu = _norm_mm(x, g_mix, wb["conv_w_in"], layer=j, out_dtype=BF16, name=f"conv_in_{j}")
            z = _conv_gate(bcu, gains["conv_w"][j], name=f"conv_gate_{j}")
            x_mid = _mm_nn(z, wb["conv_w_out"], layer=j, out_dtype=F32, residual=x, name=f"conv_out_{j}")
            mix = dict(h=h, bcu=bcu, z=z)
        h2, u, x_out = _mlp_fwd(x_mid, g_mlp, wb["mlp_w1"], wb["mlp_w2"], layer=i, name=f"mlp_fwd_{i}")
        saved.append(dict(x_in=x, x_mid=x_mid, mix=mix, h2=h2, u=u, g_mix=g_mix, g_mlp=g_mlp))
        x = x_out

    dx, loss = _loss_head(x, target, name="loss_head")

    gw = {n: None for n in BIG}
    exchanged = None
    g_uq = [None, None]
    gs = {n: [None] * SMALL_GRADS[n][0] for n in SMALL_GRADS}

    def wgrad(nm, layer, a, b, **kw):
        out = _mm_tn(a, b, stack=gw[nm], layer=layer, layers=BIG[nm][0][0], name=f"{nm}_grad_{layer}", **kw)
        gw[nm], arrived = out if kw.get("swap") else (out, None)
        return arrived

    for i in reversed(range(4)):
        j = i // 2
        sv = saved[i]
        mix = sv["mix"]
        ride = i == 0 and later is not None
        du = _mlp_down_bwd(dx, wb["mlp_w2"], sv["u"], layer=i, name=f"mlp_down_bwd_{i}")
        wgrad("mlp_w2", i, sv["u"], dx, sqrelu_a=True)
        r1_early = wgrad("mlp_w1", i, sv["h2"], du,
                         swap={n: gw[n] for n in ("mlp_w2", "conv_w_in", "conv_w_out")} if ride else None)
        dx, dg, *arrived = _nt_rms_bwd(du, wb["mlp_w1"], sv["x_mid"], sv["g_mlp"], dx, layer=i, name=f"mlp_up_bwd_{i}",
                                       swap={"mlp_w1": gw["mlp_w1"]} if ride else None)
        if ride:
            r1_early.update(arrived[0])
        gs["g_mlp"][i] = dg[0]
        if i % 2 == 0:
            do = _mm_nt(dx, wb["attn_w_o"], layer=j, out_dtype=BF16, name=f"mla_out_bwd_{j}")
            wgrad("attn_w_o", j, mix["o"], dx)
            delta = _attn_delta(do, mix["o"], name=f"attn_delta_{j}")
            lse_row, delta_row = mix["lse"], delta
            partials = None
            if ride:
                core = lax.axis_index("c").astype(jnp.int32).reshape(1)
                partials = {n: _chip_sum(gw[n], r1_early[n], core, name=f"grad_chip_sum_{n}") for n in EARLY}
            dq, dk, dv, arrived = _flash_bwd(mix["q"], mix["k"], mix["v"], do, lse_row, delta_row, pos_col, pos_row,
                                             name=f"flash_bwd_{j}", scatter=partials)
            if partials is not None:
                exchanged = (r1_early, arrived)
            dqr, dkvr, da, dgq, dgk, dgqa, dgkva = _mla_prep_bwd(
                mix["a"], mix["g_qa"], mix["g_kva"], mix["w_uq"], mix["w_ukv"], mix["g_q"], mix["g_k"], cc, sa, sb,
                dq, dk, dv, name=f"mla_prep_bwd_{j}")
            g_uq[j] = _mm_tn(mix["cq"], dqr, name=f"attn_w_uq_grad_{j}")[0]
            wgrad("attn_w_ukv", j, mix["ckv"], dkvr)
            wgrad("attn_w_down", j, mix["h"], da, keep=DOWN_DIM)
            dx, dg = _nt_rms_bwd(da, mix["w_down"], sv["x_in"], sv["g_mix"], dx, name=f"mla_down_bwd_{j}")
            gs["attn_g_qnorm"][j] = dgq[0, :QK_DIM]
            gs["attn_g_knorm"][j] = dgk[0, :QK_DIM]
            gs["attn_g_q_a"][j] = dgqa[0]
            gs["attn_g_kv_a"][j] = dgkva[0]
        else:
            dz = _mm_nt(dx, wb["conv_w_out"], layer=j, out_dtype=F32, name=f"conv_out_bwd_{j}")
            wgrad("conv_w_out", j, mix["z"], dx)
            dbcu, dcw = _conv_gate_bwd(mix["bcu"], dz, gains["conv_w"][j], name=f"conv_gate_bwd_{j}")
            gs["conv_w"][j] = dcw
            wgrad("conv_w_in", j, mix["h"], dbcu)
            dx, dg = _nt_rms_bwd(dbcu, wb["conv_w_in"], sv["x_in"], sv["g_mix"], dx, layer=j, name=f"conv_in_bwd_{j}")
        gs["g_mix"][i] = dg[0]

    gw["attn_w_uq"] = jnp.stack(g_uq).reshape(2, Q_RANK, N_HEADS, QK_PAD)[..., :QK_DIM].reshape(BIG["attn_w_uq"][0])
    grads_small = {n: jnp.stack(v) for n, v in gs.items()}
    return loss, dx, gw, grads_small, exchanged


def kernel(x, positions, g_mix, g_mlp, attn_w_down, attn_g_q_a, attn_g_kv_a, attn_w_uq, attn_w_ukv, attn_g_qnorm, attn_g_knorm, attn_w_o, conv_w_in, conv_w, conv_w_out, mlp_w1, mlp_w2, loss_target, m_g_mix, m_g_mlp, m_attn_w_down, m_attn_g_q_a, m_attn_g_kv_a, m_attn_w_uq, m_attn_w_ukv, m_attn_g_qnorm, m_attn_g_knorm, m_attn_w_o, m_conv_w_in, m_conv_w, m_conv_w_out, m_mlp_w1, m_mlp_w2, v_g_mix, v_g_mlp, v_attn_w_down, v_attn_g_q_a, v_attn_g_kv_a, v_attn_w_uq, v_attn_w_ukv, v_attn_g_qnorm, v_attn_g_knorm, v_attn_w_o, v_conv_w_in, v_conv_w, v_conv_w_out, v_mlp_w1, v_mlp_w2):
    w = dict(g_mix=g_mix, g_mlp=g_mlp, attn_w_down=attn_w_down, attn_g_q_a=attn_g_q_a, attn_g_kv_a=attn_g_kv_a,
             attn_w_uq=attn_w_uq, attn_w_ukv=attn_w_ukv, attn_g_qnorm=attn_g_qnorm, attn_g_knorm=attn_g_knorm,
             attn_w_o=attn_w_o, conv_w_in=conv_w_in, conv_w=conv_w, conv_w_out=conv_w_out, mlp_w1=mlp_w1, mlp_w2=mlp_w2)
    m = dict(g_mix=m_g_mix, g_mlp=m_g_mlp, attn_w_down=m_attn_w_down, attn_g_q_a=m_attn_g_q_a,
             attn_g_kv_a=m_attn_g_kv_a, attn_w_uq=m_attn_w_uq, attn_w_ukv=m_attn_w_ukv, attn_g_qnorm=m_attn_g_qnorm,
             attn_g_knorm=m_attn_g_knorm, attn_w_o=m_attn_w_o, conv_w_in=m_conv_w_in, conv_w=m_conv_w,
             conv_w_out=m_conv_w_out, mlp_w1=m_mlp_w1, mlp_w2=m_mlp_w2)
    v = dict(g_mix=v_g_mix, g_mlp=v_g_mlp, attn_w_down=v_attn_w_down, attn_g_q_a=v_attn_g_q_a,
             attn_g_kv_a=v_attn_g_kv_a, attn_w_uq=v_attn_w_uq, attn_w_ukv=v_attn_w_ukv, attn_g_qnorm=v_attn_g_qnorm,
             attn_g_knorm=v_attn_g_knorm, attn_w_o=v_attn_w_o, conv_w_in=v_conv_w_in, conv_w=v_conv_w,
             conv_w_out=v_conv_w_out, mlp_w1=v_mlp_w1, mlp_w2=v_mlp_w2)
    cx, cy, cc_ = _place()

    chip = 2 * cx + cy

    def own_offset(shape, axis):
        return tuple(chip * (shape[axis] // N_CHIPS) if i == axis else 0 for i in range(3))

    chip_arr = chip.astype(jnp.int32).reshape(1)
    fulls = {n: _place_own(w[n], n, chip_arr, name=f"place_{n}") for n in BIG}
    later = {n: fulls.pop(n) for n in GATHER_LATER}
    wb = _gather_weights(fulls, name="gather_weights")

    placed = lax.dynamic_update_slice(jnp.zeros(CONV_W, F32), conv_w, own_offset(CONV_W, 2))
    conv_w_full = 0.5 * _all_reduce_small(placed.reshape(-1, 128), name="conv_w_gather").reshape(CONV_W)

    gains = {n: w[n] for n in SMALL}
    gains["conv_w"] = conv_w_full

    loss, grad_x, grads_big, grads_small, (r1_early, r2_early) = _local_step(
        x[0], positions[0], loss_target[0], wb, gains, later)

    place = jnp.stack([cc_, chip]).astype(jnp.int32)
    r1_late, partials = _chip_partials(grads_big, LATE, tag="late")
    r1 = {**r1_early, **r1_late}
    r2 = {**r2_early, **_scatter_partials(partials)}
    halves = {n: _final_sum(grads_big[n], r1[n], r2[n], place, n, name=f"grad_final_sum_{n}") for n in BIG}
    grad_shards = _join_halves(halves)

    small = _unpack_small(_all_reduce_small(_pack_small(grads_small, SMALL_GRADS), name="gain_all_reduce"), SMALL_GRADS)
    grad_shards["conv_w"] = lax.dynamic_slice(small["conv_w"], own_offset(CONV_W, 2), conv_w.shape)

    loss_total = lax.psum(loss[0, 0], ("x", "y", "c"))

    grads, deltas, new_m, new_v = {}, {}, {}, {}
    for n in [*BIG, "conv_w"]:
        shp = w[n].shape
        two_d = (shp[0] * shp[1], shp[2])
        g2 = grad_shards[n].reshape(two_d)
        d, nm, nv = _adamw(w[n].reshape(two_d), g2, m[n].reshape(two_d), v[n].reshape(two_d), name=f"adamw_{n}")
        grads[n], deltas[n], new_m[n], new_v[n] = grad_shards[n], d.reshape(shp), nm.reshape(shp), nv.reshape(shp)
    d, nm, nv = _adamw(_pack_small(w, SMALL), _pack_small(small, SMALL), _pack_small(m, SMALL), _pack_small(v, SMALL),
                       name="adamw_gains")
    d, nm, nv = _unpack_small(d, SMALL), _unpack_small(nm, SMALL), _unpack_small(nv, SMALL)
    for n in SMALL:
        grads[n], deltas[n], new_m[n], new_v[n] = small[n], d[n], nm[n], nv[n]

    return (loss_total, grad_x[None],
            *[grads[n] for n in WEIGHT_ORDER], *[deltas[n] for n in WEIGHT_ORDER],
            *[new_m[n] for n in WEIGHT_ORDER], *[new_v[n] for n in WEIGHT_ORDER])
```

```python
import functools

import jax
import jax.numpy as jnp
from jax import lax
from jax.experimental import pallas as pl
from jax.experimental.pallas import tpu as pltpu

F32 = jnp.float32
BF16 = jnp.bfloat16

D_MODEL = 1024
N_HEADS = 8
QK_NOPE = 128
QK_ROPE = 64
QK_DIM = QK_NOPE + QK_ROPE
QK_PAD = 256
V_DIM = 128
Q_RANK = 256
KV_RANK = 128
DOWN_DIM = Q_RANK + KV_RANK + QK_ROPE
DOWN_PAD = 512
D_FF = 4 * D_MODEL
ROPE_THETA = 10000.0
EPS = 1e-6
NEG = -1e30
SCALE = QK_DIM ** -0.5
SCALE_LOG2E = SCALE * 1.4426950408889634
LOG2E = 1.4426950408889634
ATTN_CHAINS = 2

ADAM_LR = 0.001
ADAM_B1 = 0.9
ADAM_B2 = 0.999
ADAM_EPS = 1e-08
ADAM_WD = 0.01
ADAM_STEP = 10

N_CHIPS = 4
MESH = pl.DeviceIdType.MESH
ANY = pl.BlockSpec(memory_space=pl.ANY)

TM = 512
TM_WIDE = 512
FWD_TQ = 1024
FWD_TK = 1024
BWD_TQ = 1024
BWD_TK = 1024
HALO = 16
T_PREP = 512
T_RED = 1024
SUM_BLOCK_BYTES = 2 * 1024 * 1024


def _tile(n, pref):
    t = min(n, pref)
    assert n % t == 0, (n, t)
    return t


def _cparams(*sem):
    return pltpu.CompilerParams(dimension_semantics=sem)


def _dot(a, b):
    return jnp.dot(a, b, preferred_element_type=F32)


def _dot_nt(a, b):
    return lax.dot_general(a, b, (((1,), (1,)), ((), ())), preferred_element_type=F32)


def _dot_tn(a, b):
    return lax.dot_general(a, b, (((0,), (0,)), ((), ())), preferred_element_type=F32)


def _rms(x, width):
    r = lax.rsqrt(jnp.sum(x * x, axis=-1, keepdims=True) * (1.0 / width) + EPS)
    return x * r, r


def _rms_bwd(xhat, r, dxhat, width):
    return r * (dxhat - xhat * (jnp.sum(dxhat * xhat, axis=-1, keepdims=True) * (1.0 / width)))


def _rope(t, cc, sa, sb):
    return t * cc + pltpu.roll(t, 96, 1) * sa + pltpu.roll(t, 32, 1) * sb


def _rope_t(g, cc, sa, sb):
    return g * cc + pltpu.roll(g * sa, 32, 1) + pltpu.roll(g * sb, 96, 1)


def _wspec(w, layer):
    once = pl.Buffered(1)
    if w.ndim == 2:
        return pl.BlockSpec(w.shape, lambda *_: (0, 0), pipeline_mode=once)
    return pl.BlockSpec((None,) + w.shape[1:], lambda *_: (layer, 0, 0), pipeline_mode=once)


def _mm_nn(a, b, *, out_dtype, name, residual=None, layer=0):
    m, k = a.shape
    n = b.shape[-1]
    tm = _tile(m, TM)

    def body(*refs):
        if residual is None:
            a_ref, b_ref, o_ref = refs
        else:
            a_ref, b_ref, r_ref, o_ref = refs
        acc = _dot(a_ref[...].astype(BF16), b_ref[...])
        if residual is not None:
            acc = acc + r_ref[...]
        o_ref[...] = acc.astype(o_ref.dtype)

    in_specs = [pl.BlockSpec((tm, k), lambda i: (i, 0)), _wspec(b, layer)]
    args = [a, b]
    if residual is not None:
        in_specs.append(pl.BlockSpec((tm, n), lambda i: (i, 0)))
        args.append(residual)
    return pl.pallas_call(
        body, name=name, grid=(m // tm,), in_specs=in_specs,
        out_specs=pl.BlockSpec((tm, n), lambda i: (i, 0)),
        out_shape=jax.ShapeDtypeStruct((m, n), out_dtype),
        compiler_params=_cparams("parallel"),
    )(*args)


def _mm_nt(a, b, *, out_dtype, name, layer=0):
    m, k = a.shape
    n = b.shape[-2]
    tm = _tile(m, TM)

    def body(a_ref, b_ref, o_ref):
        o_ref[...] = _dot_nt(a_ref[...].astype(BF16), b_ref[...]).astype(o_ref.dtype)

    return pl.pallas_call(
        body, name=name, grid=(m // tm,),
        in_specs=[pl.BlockSpec((tm, k), lambda i: (i, 0)), _wspec(b, layer)],
        out_specs=pl.BlockSpec((tm, n), lambda i: (i, 0)),
        out_shape=jax.ShapeDtypeStruct((m, n), out_dtype),
        compiler_params=_cparams("parallel"),
    )(a, b)


def _mm_tn(a, b, *, name, stack=None, layer=0, layers=1, keep=None, sqrelu_a=False, swap=None):
    s, ka = a.shape
    n = b.shape[1]
    ts = _tile(s, T_RED)
    tka = _tile(ka, 1024)
    tn = _tile(n, 1024)
    n_out = n if keep is None else keep
    assert keep is None or tn == n
    grid = (ka // tka, n // tn, s // ts)
    names = list(swap or {})
    ns = len(names)
    n_in = 2 + (stack is not None)

    def body(*refs):
        a_ref, b_ref = refs[:2]
        o_ref = refs[n_in + ns]
        if ns:
            copies = _swap_copies(refs[n_in:n_in + ns], refs[n_in + ns + 1:n_in + 2 * ns + 1], names,
                                  *refs[n_in + 2 * ns + 1:])
            _start_at_first_step(copies, grid)

        @pl.when(pl.program_id(2) == 0)
        def _():
            o_ref[...] = jnp.zeros_like(o_ref)

        a_t = a_ref[...]
        if sqrelu_a:
            a_t = _sqrelu(a_t.astype(F32))
        o_ref[...] += _dot_tn(a_t.astype(BF16), b_ref[...].astype(BF16))[:, :n_out if keep else tn]
        if ns:
            _wait_at_last_step(copies, grid)

    in_specs = [pl.BlockSpec((ts, tka), lambda i, j, t: (t, i)), pl.BlockSpec((ts, tn), lambda i, j, t: (t, j))]
    args = [a, b]
    if stack is not None:
        in_specs.append(ANY)
        args.append(stack)
    sent = [swap[nm] for nm in names]
    out = pl.pallas_call(
        body, name=name, grid=grid, in_specs=in_specs + [ANY] * ns,
        out_specs=[pl.BlockSpec((None, tka, tn if keep is None else keep), lambda i, j, t: (layer, i, j))] + [ANY] * ns,
        out_shape=[jax.ShapeDtypeStruct((layers, ka, n_out), F32)] + _swap_out_shapes(sent),
        scratch_shapes=_swap_sems(ns),
        input_output_aliases={} if stack is None else {2: 0},
        compiler_params=_cparams(*(["arbitrary"] * 3 if ns else ["parallel", "parallel", "arbitrary"])),
    )(*args, *sent)
    return (out[0], dict(zip(names, out[1:]))) if ns else out[0]


def _norm_mm(x, g, w, *, out_dtype, name, layer=0):
    s, d = x.shape
    n = w.shape[-1]
    tm = _tile(s, TM)

    def body(x_ref, g_ref, w_ref, h_ref, o_ref):
        xhat, _ = _rms(x_ref[...], d)
        h = (xhat * g_ref[...]).astype(BF16)
        h_ref[...] = h
        o_ref[...] = _dot(h, w_ref[...]).astype(o_ref.dtype)

    return pl.pallas_call(
        body, name=name, grid=(s // tm,),
        in_specs=[pl.BlockSpec((tm, d), lambda i: (i, 0)), pl.BlockSpec((1, d), lambda i: (0, 0)), _wspec(w, layer)],
        out_specs=[pl.BlockSpec((tm, d), lambda i: (i, 0)), pl.BlockSpec((tm, n), lambda i: (i, 0))],
        out_shape=[jax.ShapeDtypeStruct((s, d), BF16), jax.ShapeDtypeStruct((s, n), out_dtype)],
        compiler_params=_cparams("parallel"),
    )(x, g, w)


def _nt_rms_bwd(dy, w, x, g, dres, *, name, layer=0, swap=None):
    s, n = dy.shape
    d = x.shape[1]
    tm = _tile(s, TM)
    grid = (s // tm,)
    names = list(swap or {})
    ns = len(names)

    def body(dy_ref, w_ref, x_ref, g_ref, dres_ref, *rest):
        dx_ref, dg_ref = rest[ns:ns + 2]
        if ns:
            copies = _swap_copies(rest[:ns], rest[ns + 2:2 * ns + 2], names, *rest[2 * ns + 2:])
            _start_at_first_step(copies, grid)

        @pl.when(pl.program_id(0) == 0)
        def _():
            dg_ref[...] = jnp.zeros_like(dg_ref)

        dh = _dot_nt(dy_ref[...], w_ref[...])
        xhat, r = _rms(x_ref[...], d)
        dg_ref[...] += jnp.sum(dh * xhat, axis=0, keepdims=True)
        dx_ref[...] = dres_ref[...] + _rms_bwd(xhat, r, dh * g_ref[...], d)
        if ns:
            _wait_at_last_step(copies, grid)

    sent = [swap[nm] for nm in names]
    out = pl.pallas_call(
        body, name=name, grid=grid,
        in_specs=[pl.BlockSpec((tm, n), lambda i: (i, 0)), _wspec(w, layer),
                  pl.BlockSpec((tm, d), lambda i: (i, 0)), pl.BlockSpec((1, d), lambda i: (0, 0)),
                  pl.BlockSpec((tm, d), lambda i: (i, 0))] + [ANY] * ns,
        out_specs=[pl.BlockSpec((tm, d), lambda i: (i, 0)), pl.BlockSpec((1, d), lambda i: (0, 0))] + [ANY] * ns,
        out_shape=[jax.ShapeDtypeStruct((s, d), F32), jax.ShapeDtypeStruct((1, d), F32)] + _swap_out_shapes(sent),
        scratch_shapes=_swap_sems(ns),
        compiler_params=_cparams("arbitrary"),
    )(dy, w, x, g, dres, *sent)
    return (out[0], out[1], dict(zip(names, out[2:]))) if ns else (out[0], out[1])


def _sqrelu(u):
    return jnp.square(jnp.maximum(u, 0.0))


def _mlp_fwd(x, g, w1, w2, *, name, layer=0):
    s, d = x.shape
    n = w1.shape[-1]
    tm = _tile(s, TM_WIDE)

    def body(x_ref, g_ref, w1_ref, w2_ref, h_ref, u_ref, y_ref):
        x_t = x_ref[...]
        xhat, _ = _rms(x_t, d)
        h = (xhat * g_ref[...]).astype(BF16)
        h_ref[...] = h
        u = _dot(h, w1_ref[...])
        u_ref[...] = u.astype(BF16)
        y_ref[...] = x_t + _dot(_sqrelu(u).astype(BF16), w2_ref[...])

    return pl.pallas_call(
        body, name=name, grid=(s // tm,),
        in_specs=[pl.BlockSpec((tm, d), lambda i: (i, 0)), pl.BlockSpec((1, d), lambda i: (0, 0)),
                  _wspec(w1, layer), _wspec(w2, layer)],
        out_specs=[pl.BlockSpec((tm, d), lambda i: (i, 0)), pl.BlockSpec((tm, n), lambda i: (i, 0)),
                   pl.BlockSpec((tm, d), lambda i: (i, 0))],
        out_shape=[jax.ShapeDtypeStruct((s, d), BF16), jax.ShapeDtypeStruct((s, n), BF16),
                   jax.ShapeDtypeStruct((s, d), F32)],
        compiler_params=_cparams("parallel"),
    )(x, g, w1, w2)


def _mlp_down_bwd(dy, w2, u, *, name, layer=0):
    s, d = dy.shape
    n = w2.shape[-2]
    tm = _tile(s, TM_WIDE)

    def body(dy_ref, w_ref, u_ref, du_ref):
        dact = _dot_nt(dy_ref[...].astype(BF16), w_ref[...])
        du_ref[...] = (dact * (2.0 * jnp.maximum(u_ref[...].astype(F32), 0.0))).astype(BF16)

    return pl.pallas_call(
        body, name=name, grid=(s // tm,),
        in_specs=[pl.BlockSpec((tm, d), lambda i: (i, 0)), _wspec(w2, layer),
                  pl.BlockSpec((tm, n), lambda i: (i, 0))],
        out_specs=pl.BlockSpec((tm, n), lambda i: (i, 0)),
        out_shape=jax.ShapeDtypeStruct((s, n), BF16),
        compiler_params=_cparams("parallel"),
    )(dy, w2, u)


def _conv_gate(bcu, conv_w, *, name):
    s = bcu.shape[0]
    d = D_MODEL
    tm = _tile(s, TM)
    hb = tm // HALO

    def body(bcu_ref, prev_ref, w_ref, z_ref, pbuf):
        i = pl.program_id(0)
        gb = bcu_ref[:, 0:d].astype(F32)
        p = bcu_ref[:, d:2 * d].astype(F32) * bcu_ref[:, 2 * d:3 * d].astype(F32)
        pprev = prev_ref[:, d:2 * d].astype(F32) * prev_ref[:, 2 * d:3 * d].astype(F32)
        pbuf[0:HALO, :] = jnp.where(i > 0, pprev, 0.0)
        pbuf[HALO:HALO + tm, :] = p
        cv = (w_ref[2:3, :] * p + w_ref[1:2, :] * pbuf[HALO - 1:HALO - 1 + tm, :]
              + w_ref[0:1, :] * pbuf[HALO - 2:HALO - 2 + tm, :])
        z_ref[...] = (gb * cv).astype(BF16)

    return pl.pallas_call(
        body, name=name, grid=(s // tm,),
        in_specs=[pl.BlockSpec((tm, 3 * d), lambda i: (i, 0)),
                  pl.BlockSpec((HALO, 3 * d), lambda i: (jnp.maximum(i * hb - 1, 0), 0)),
                  pl.BlockSpec((3, d), lambda i: (0, 0))],
        out_specs=pl.BlockSpec((tm, d), lambda i: (i, 0)),
        out_shape=jax.ShapeDtypeStruct((s, d), BF16),
        scratch_shapes=[pltpu.VMEM((tm + HALO, d), F32)],
        compiler_params=_cparams("parallel"),
    )(bcu, bcu, conv_w)


def _conv_gate_bwd(bcu, dz, conv_w, *, name):
    s = bcu.shape[0]
    d = D_MODEL
    tm = _tile(s, TM)
    hb = tm // HALO
    nt = s // tm

    def body(bcu_ref, prev_ref, next_ref, dz_ref, dznext_ref, w_ref, dbcu_ref, dw_ref, pbuf, dbuf):
        i = pl.program_id(0)

        @pl.when(i == 0)
        def _():
            dw_ref[...] = jnp.zeros_like(dw_ref)

        gb = bcu_ref[:, 0:d].astype(F32)
        gc = bcu_ref[:, d:2 * d].astype(F32)
        uu = bcu_ref[:, 2 * d:3 * d].astype(F32)
        p = gc * uu
        pprev = prev_ref[:, d:2 * d].astype(F32) * prev_ref[:, 2 * d:3 * d].astype(F32)
        pbuf[0:HALO, :] = jnp.where(i > 0, pprev, 0.0)
        pbuf[HALO:HALO + tm, :] = p
        p1 = pbuf[HALO - 1:HALO - 1 + tm, :]
        p2 = pbuf[HALO - 2:HALO - 2 + tm, :]
        cv = w_ref[2:3, :] * p + w_ref[1:2, :] * p1 + w_ref[0:1, :] * p2
        dz_t = dz_ref[...]
        dcv = dz_t * gb
        dcv_next = dznext_ref[...] * next_ref[:, 0:d].astype(F32)
        dbuf[0:tm, :] = dcv
        dbuf[tm:tm + HALO, :] = jnp.where(i < nt - 1, dcv_next, 0.0)
        dp = w_ref[2:3, :] * dcv + w_ref[1:2, :] * dbuf[1:1 + tm, :] + w_ref[0:1, :] * dbuf[2:2 + tm, :]
        dw_ref[2:3, :] += jnp.sum(dcv * p, axis=0, keepdims=True)
        dw_ref[1:2, :] += jnp.sum(dcv * p1, axis=0, keepdims=True)
        dw_ref[0:1, :] += jnp.sum(dcv * p2, axis=0, keepdims=True)
        dbcu_ref[:, 0:d] = (dz_t * cv).astype(BF16)
        dbcu_ref[:, d:2 * d] = (dp * uu).astype(BF16)
        dbcu_ref[:, 2 * d:3 * d] = (dp * gc).astype(BF16)

    nxt = lambda i: (jnp.minimum((i + 1) * hb, s // HALO - 1), 0)
    return pl.pallas_call(
        body, name=name, grid=(nt,),
        in_specs=[pl.BlockSpec((tm, 3 * d), lambda i: (i, 0)),
                  pl.BlockSpec((HALO, 3 * d), lambda i: (jnp.maximum(i * hb - 1, 0), 0)),
                  pl.BlockSpec((HALO, 3 * d), nxt),
                  pl.BlockSpec((tm, d), lambda i: (i, 0)),
                  pl.BlockSpec((HALO, d), nxt),
                  pl.BlockSpec((3, d), lambda i: (0, 0))],
        out_specs=[pl.BlockSpec((tm, 3 * d), lambda i: (i, 0)), pl.BlockSpec((3, d), lambda i: (0, 0))],
        out_shape=[jax.ShapeDtypeStruct((s, 3 * d), BF16), jax.ShapeDtypeStruct((3, d), F32)],
        scratch_shapes=[pltpu.VMEM((tm + HALO, d), F32), pltpu.VMEM((tm + HALO, d), F32)],
        compiler_params=_cparams("arbitrary"),
    )(bcu, bcu, bcu, dz, dz, conv_w)


def _mla_prep(a, g_qa, g_kva, w_uq, w_ukv, g_q, g_k, cc, sa, sb, *, name):
    s = a.shape[0]
    ts = _tile(s, T_PREP)

    def body(a_ref, gqa_ref, gkva_ref, wuq_ref, wukv_ref, gq_ref, gk_ref, cc_ref, sa_ref, sb_ref,
             cq_ref, ckv_ref, q_ref, k_ref, v_ref):
        xq, _ = _rms(a_ref[:, 0:Q_RANK], Q_RANK)
        cq = (xq * gqa_ref[...]).astype(BF16)
        cq_ref[...] = cq
        xkv, _ = _rms(a_ref[:, Q_RANK:Q_RANK + KV_RANK], KV_RANK)
        ckv = (xkv * gkva_ref[...]).astype(BF16)
        ckv_ref[...] = ckv
        kpe = a_ref[:, Q_RANK + KV_RANK:DOWN_PAD]
        kpe_ss = jnp.sum(kpe * kpe, axis=-1, keepdims=True)
        cc_t, sa_t, sb_t = cc_ref[...], sa_ref[...], sb_ref[...]
        gq = gq_ref[...]
        gk = gk_ref[...]
        for h in range(N_HEADS):
            cols = slice(h * QK_PAD, (h + 1) * QK_PAD)
            qhat, _ = _rms(_dot(cq, wuq_ref[:, cols]), QK_DIM)
            qn = qhat * (gq * SCALE_LOG2E)
            q_ref[h, :, 0:QK_NOPE] = qn[:, 0:QK_NOPE].astype(BF16)
            q_ref[h, :, QK_NOPE:QK_PAD] = _rope(qn[:, QK_NOPE:QK_PAD], cc_t, sa_t, sb_t).astype(BF16)
            kvr = _dot(ckv, wukv_ref[:, cols])
            kn = kvr[:, 0:QK_NOPE]
            rk = lax.rsqrt((jnp.sum(kn * kn, axis=-1, keepdims=True) + kpe_ss) * (1.0 / QK_DIM) + EPS)
            k_ref[h, :, 0:QK_NOPE] = (kn * rk * gk[:, 0:QK_NOPE]).astype(BF16)
            k_ref[h, :, QK_NOPE:QK_PAD] = _rope(kpe * rk * gk[:, QK_NOPE:QK_PAD], cc_t, sa_t, sb_t).astype(BF16)
            v_ref[h, :, 0:V_DIM] = kvr[:, QK_NOPE:QK_PAD].astype(BF16)
            v_ref[h, :, V_DIM:2 * V_DIM] = jnp.ones((ts, V_DIM), BF16)

    row = lambda i: (i, 0)
    fixed = lambda i: (0, 0)
    head = lambda i: (0, i, 0)
    return pl.pallas_call(
        body, name=name, grid=(s // ts,),
        in_specs=[pl.BlockSpec((ts, DOWN_PAD), row), pl.BlockSpec((1, Q_RANK), fixed), pl.BlockSpec((1, KV_RANK), fixed),
                  pl.BlockSpec((Q_RANK, N_HEADS * QK_PAD), fixed), pl.BlockSpec((KV_RANK, N_HEADS * QK_PAD), fixed),
                  pl.BlockSpec((1, QK_PAD), fixed), pl.BlockSpec((1, QK_PAD), fixed),
                  pl.BlockSpec((ts, 128), row), pl.BlockSpec((ts, 128), row), pl.BlockSpec((ts, 128), row)],
        out_specs=[pl.BlockSpec((ts, Q_RANK), row), pl.BlockSpec((ts, KV_RANK), row),
                   pl.BlockSpec((N_HEADS, ts, QK_PAD), head), pl.BlockSpec((N_HEADS, ts, QK_PAD), head),
                   pl.BlockSpec((N_HEADS, ts, 2 * V_DIM), head)],
        out_shape=[jax.ShapeDtypeStruct((s, Q_RANK), BF16), jax.ShapeDtypeStruct((s, KV_RANK), BF16),
                   jax.ShapeDtypeStruct((N_HEADS, s, QK_PAD), BF16), jax.ShapeDtypeStruct((N_HEADS, s, QK_PAD), BF16),
                   jax.ShapeDtypeStruct((N_HEADS, s, 2 * V_DIM), BF16)],
        compiler_params=_cparams("parallel"),
    )(a, g_qa, g_kva, w_uq, w_ukv, g_q, g_k, cc, sa, sb)


def _mla_prep_bwd(a, g_qa, g_kva, w_uq, w_ukv, g_q, g_k, cc, sa, sb, dq, dk, dv, *, name):
    s = a.shape[0]
    ts = _tile(s, T_PREP)

    def body(a_ref, gqa_ref, gkva_ref, wuq_ref, wukv_ref, gq_ref, gk_ref, cc_ref, sa_ref, sb_ref,
             dq_ref, dk_ref, dv_ref, dqr_ref, dkvr_ref, da_ref, dgq_ref, dgk_ref, dgqa_ref, dgkva_ref):
        @pl.when(pl.program_id(0) == 0)
        def _():
            dgq_ref[...] = jnp.zeros_like(dgq_ref)
            dgk_ref[...] = jnp.zeros_like(dgk_ref)
            dgqa_ref[...] = jnp.zeros_like(dgqa_ref)
            dgkva_ref[...] = jnp.zeros_like(dgkva_ref)

        xq, r_q = _rms(a_ref[:, 0:Q_RANK], Q_RANK)
        cq = (xq * gqa_ref[...]).astype(BF16)
        xkv, r_kv = _rms(a_ref[:, Q_RANK:Q_RANK + KV_RANK], KV_RANK)
        ckv = (xkv * gkva_ref[...]).astype(BF16)
        kpe = a_ref[:, Q_RANK + KV_RANK:DOWN_PAD]
        kpe_ss = jnp.sum(kpe * kpe, axis=-1, keepdims=True)
        cc_t, sa_t, sb_t = cc_ref[...], sa_ref[...], sb_ref[...]
        gq = gq_ref[...]
        gk = gk_ref[...]
        dcq = jnp.zeros((ts, Q_RANK), F32)
        dckv = jnp.zeros((ts, KV_RANK), F32)
        dkpe = jnp.zeros((ts, 128), F32)
        dgq = jnp.zeros((1, QK_PAD), F32)
        dgk_n = jnp.zeros((1, QK_NOPE), F32)
        dgk_p = jnp.zeros((1, 128), F32)
        for h in range(N_HEADS):
            cols = slice(h * QK_PAD, (h + 1) * QK_PAD)
            qhat, rq = _rms(_dot(cq, wuq_ref[:, cols]), QK_DIM)
            dqn = jnp.concatenate(
                [dq_ref[h, :, 0:QK_NOPE], _rope_t(dq_ref[h, :, QK_NOPE:QK_PAD], cc_t, sa_t, sb_t)], axis=1)
            dgq = dgq + jnp.sum(dqn * qhat, axis=0, keepdims=True)
            dqr = _rms_bwd(qhat, rq, dqn * gq, QK_DIM).astype(BF16)
            dqr_ref[:, cols] = dqr
            dcq = dcq + _dot_nt(dqr, wuq_ref[:, cols])
            kn = _dot(ckv, wukv_ref[:, h * QK_PAD:h * QK_PAD + QK_NOPE])
            rk = lax.rsqrt((jnp.sum(kn * kn, axis=-1, keepdims=True) + kpe_ss) * (1.0 / QK_DIM) + EPS)
            khat_n = kn * rk
            khat_p = kpe * rk
            dkn = dk_ref[h, :, 0:QK_NOPE]
            dkp = _rope_t(dk_ref[h, :, QK_NOPE:QK_PAD], cc_t, sa_t, sb_t)
            dgk_n = dgk_n + jnp.sum(dkn * khat_n, axis=0, keepdims=True)
            dgk_p = dgk_p + jnp.sum(dkp * khat_p, axis=0, keepdims=True)
            dxn = dkn * gk[:, 0:QK_NOPE]
            dxp = dkp * gk[:, QK_NOPE:QK_PAD]
            mean = (jnp.sum(dxn * khat_n, axis=-1, keepdims=True)
                    + jnp.sum(dxp * khat_p, axis=-1, keepdims=True)) * (1.0 / QK_DIM)
            dkpe = dkpe + rk * (dxp - khat_p * mean)
            dkvr = jnp.concatenate([rk * (dxn - khat_n * mean), dv_ref[h, :, :]], axis=1).astype(BF16)
            dkvr_ref[:, cols] = dkvr
            dckv = dckv + _dot_nt(dkvr, wukv_ref[:, cols])
        dgq_ref[...] += dgq
        dgk_ref[:, 0:QK_NOPE] += dgk_n
        dgk_ref[:, QK_NOPE:QK_PAD] += dgk_p
        dgqa_ref[...] += jnp.sum(dcq * xq, axis=0, keepdims=True)
        dgkva_ref[...] += jnp.sum(dckv * xkv, axis=0, keepdims=True)
        da_ref[:, 0:Q_RANK] = _rms_bwd(xq, r_q, dcq * gqa_ref[...], Q_RANK).astype(BF16)
        da_ref[:, Q_RANK:Q_RANK + KV_RANK] = _rms_bwd(xkv, r_kv, dckv * gkva_ref[...], KV_RANK).astype(BF16)
        da_ref[:, Q_RANK + KV_RANK:DOWN_PAD] = dkpe.astype(BF16)

    row = lambda i: (i, 0)
    fixed = lambda i: (0, 0)
    head = lambda i: (0, i, 0)
    wide = N_HEADS * QK_PAD
    return pl.pallas_call(
        body, name=name, grid=(s // ts,),
        in_specs=[pl.BlockSpec((ts, DOWN_PAD), row), pl.BlockSpec((1, Q_RANK), fixed), pl.BlockSpec((1, KV_RANK), fixed),
                  pl.BlockSpec((Q_RANK, wide), fixed), pl.BlockSpec((KV_RANK, wide), fixed),
                  pl.BlockSpec((1, QK_PAD), fixed), pl.BlockSpec((1, QK_PAD), fixed),
                  pl.BlockSpec((ts, 128), row), pl.BlockSpec((ts, 128), row), pl.BlockSpec((ts, 128), row),
                  pl.BlockSpec((N_HEADS, ts, QK_PAD), head), pl.BlockSpec((N_HEADS, ts, QK_PAD), head),
                  pl.BlockSpec((N_HEADS, ts, V_DIM), head)],
        out_specs=[pl.BlockSpec((ts, wide), row), pl.BlockSpec((ts, wide), row), pl.BlockSpec((ts, DOWN_PAD), row),
                   pl.BlockSpec((1, QK_PAD), fixed), pl.BlockSpec((1, QK_PAD), fixed),
                   pl.BlockSpec((1, Q_RANK), fixed), pl.BlockSpec((1, KV_RANK), fixed)],
        out_shape=[jax.ShapeDtypeStruct((s, wide), BF16), jax.ShapeDtypeStruct((s, wide), BF16),
                   jax.ShapeDtypeStruct((s, DOWN_PAD), BF16),
                   jax.ShapeDtypeStruct((1, QK_PAD), F32), jax.ShapeDtypeStruct((1, QK_PAD), F32),
                   jax.ShapeDtypeStruct((1, Q_RANK), F32), jax.ShapeDtypeStruct((1, KV_RANK), F32)],
        compiler_params=_cparams("arbitrary"),
    )(a, g_qa, g_kva, w_uq, w_ukv, g_q, g_k, cc, sa, sb, dq, dk, dv)


def _flash_fwd(q, k, v, pos_col, pos_row, *, name, gather=None):
    nh, s, _ = q.shape
    tq = _tile(s, FWD_TQ)
    tk = _tile(s, FWD_TK)
    sq = tq // ATTN_CHAINS
    nq = s // tq
    names = list(gather or {})
    ng = len(names)

    def body(q_ref, k_ref, v_ref, pq_ref, pk_ref, *rest):
        o_ref, lse_ref = rest[ng:ng + 2]
        m_sc, acc_sc = rest[2 * ng + 2:2 * ng + 4]
        qb = pl.program_id(1)
        if ng:
            sends, recvs = _gather_ici_copies(rest[ng + 2:2 * ng + 2], names, *rest[2 * ng + 4:], base=0, stride=3)

            @pl.when((pl.program_id(0) == 0) & (qb == 0))
            def _():
                for cp in sends:
                    cp.start()

        m_sc[...] = jnp.full_like(m_sc, NEG)
        acc_sc[...] = jnp.zeros_like(acc_sc)

        def step(kb, masked):
            keys = pl.ds(pl.multiple_of(kb * tk, tk), tk)
            kt = k_ref[0, keys, :]
            vt = v_ref[0, keys, :]
            scores = [_dot_nt(q_ref[0, u * sq:(u + 1) * sq, :], kt) for u in range(ATTN_CHAINS)]
            for u in range(ATTN_CHAINS):
                rows = slice(u * sq, (u + 1) * sq)
                sc = scores[u]
                if masked:
                    sc = jnp.where(pq_ref[rows, :] >= pk_ref[:, keys], sc, NEG)
                m_prev = m_sc[rows, :]
                m_new = jnp.maximum(m_prev, jnp.max(sc, axis=-1, keepdims=True))
                alpha = jnp.exp2(m_prev - m_new)
                p = jnp.exp2(sc - jnp.tile(m_new, (1, tk // 128)))
                acc_sc[rows, :] = jnp.tile(alpha, (1, 2)) * acc_sc[rows, :] + _dot(p.astype(BF16), vt)
                m_sc[rows, :] = m_new

        n_before = (qb * tq) // tk
        n_seen = (qb * tq + tq - 1) // tk + 1
        lax.fori_loop(0, n_before, lambda kb, c: (step(kb, False), c)[1], 0)
        lax.fori_loop(n_before, n_seen, lambda kb, c: (step(kb, True), c)[1], 0)
        l = acc_sc[:, V_DIM:2 * V_DIM]
        o_ref[...] = (acc_sc[:, 0:V_DIM] / l).astype(BF16)
        lse = m_sc[...] * (1.0 / LOG2E) + jnp.log(l)
        lse_ref[0] = lse.T[0:1, :]

        if ng:
            @pl.when((pl.program_id(0) == nh - 1) & (qb == nq - 1))
            def _():
                for cp in recvs:
                    cp.wait_recv()
                for cp in sends:
                    cp.wait_send()

    arrays = [gather[nm] for nm in names]
    out = pl.pallas_call(
        body, name=name, grid=(nh, nq),
        in_specs=[pl.BlockSpec((1, tq, QK_PAD), lambda h, qb: (h, qb, 0)),
                  pl.BlockSpec((1, s, QK_PAD), lambda h, qb: (h, 0, 0)),
                  pl.BlockSpec((1, s, 2 * V_DIM), lambda h, qb: (h, 0, 0)),
                  pl.BlockSpec((tq, 1), lambda h, qb: (qb, 0)),
                  pl.BlockSpec((1, s), lambda h, qb: (0, 0))] + [ANY] * ng,
        out_specs=[pl.BlockSpec((tq, V_DIM), lambda h, qb: (qb, h)),
                   pl.BlockSpec((1, 1, tq), lambda h, qb: (h, 0, qb))] + [ANY] * ng,
        scratch_shapes=[pltpu.VMEM((tq, 128), F32), pltpu.VMEM((tq, 2 * V_DIM), F32)]
        + ([pltpu.SemaphoreType.DMA((3 * ng,)), pltpu.SemaphoreType.DMA((3 * ng,))] if ng else []),
        out_shape=[jax.ShapeDtypeStruct((s, nh * V_DIM), BF16), jax.ShapeDtypeStruct((nh, 1, s), F32)]
        + [jax.ShapeDtypeStruct(a.shape, a.dtype) for a in arrays],
        input_output_aliases={5 + i: 2 + i for i in range(ng)},
        compiler_params=_cparams("arbitrary", "arbitrary") if ng else _cparams("parallel", "parallel"),
    )(q, k, v, pos_col, pos_row, *arrays)
    return out[0], out[1], dict(zip(names, out[2:]))


def _attn_out_bwd(dy, w_o, o, *, name, layer=0):
    s, d = dy.shape
    n = w_o.shape[-2]
    tm = _tile(s, TM)

    def body(dy_ref, w_ref, o_ref, do_ref, d_ref):
        do = _dot_nt(dy_ref[...].astype(BF16), w_ref[...]).astype(BF16)
        do_ref[...] = do
        for h in range(N_HEADS):
            cols = slice(h * V_DIM, (h + 1) * V_DIM)
            prod = do[:, cols].astype(F32) * o_ref[:, cols].astype(F32)
            d_ref[h] = jnp.sum(prod.T, axis=0, keepdims=True)

    return pl.pallas_call(
        body, name=name, grid=(s // tm,),
        in_specs=[pl.BlockSpec((tm, d), lambda i: (i, 0)), _wspec(w_o, layer), pl.BlockSpec((tm, n), lambda i: (i, 0))],
        out_specs=[pl.BlockSpec((tm, n), lambda i: (i, 0)), pl.BlockSpec((N_HEADS, 1, tm), lambda i: (0, 0, i))],
        out_shape=[jax.ShapeDtypeStruct((s, n), BF16), jax.ShapeDtypeStruct((N_HEADS, 1, s), F32)],
        compiler_params=_cparams("parallel"),
    )(dy, w_o, o)


def _flash_bwd(q, k, v, do, lse_row, delta_row, pos_col, pos_row, *, name, scatter=None):
    nh, s, _ = q.shape
    tq = _tile(s, BWD_TQ)
    tk = _tile(s, BWD_TK)
    nq, nk = s // tq, s // tk
    sk = tk // ATTN_CHAINS
    names = list(scatter or {})
    ng = len(names)

    def body(q_ref, k_ref, v_ref, do_ref, lse_ref, delta_ref, pq_ref, pk_ref, *rest):
        dq_ref, dk_ref, dv_ref = rest[ng:ng + 3]
        dk_sc, dv_sc = rest[2 * ng + 3:2 * ng + 5]
        kb = pl.program_id(1)
        if ng:
            copies = _scatter_copies(rest[:ng], rest[ng + 3:2 * ng + 3], names, *rest[2 * ng + 5:])

            @pl.when((pl.program_id(0) == 0) & (kb == 0))
            def _():
                for cp in copies:
                    cp.start()

        @pl.when(kb == 0)
        def _():
            dq_ref[...] = jnp.zeros_like(dq_ref)

        dk_sc[...] = jnp.zeros_like(dk_sc)
        dv_sc[...] = jnp.zeros_like(dv_sc)

        def step(qb, masked):
            trim = masked and tq == tk
            start = pl.multiple_of(qb * tq, tq)
            offs = [u * sk if trim else 0 for u in range(ATTN_CHAINS)]
            qss = [pl.ds(start + offs[u], tq - offs[u]) for u in range(ATTN_CHAINS)]
            qts = [q_ref[0, qss[u], :] for u in range(ATTN_CHAINS)]
            dots = [do_ref[qss[u], :] for u in range(ATTN_CHAINS)]
            sts = [_dot_nt(k_ref[0, u * sk:(u + 1) * sk, :], qts[u]) for u in range(ATTN_CHAINS)]
            dpts = [_dot_nt(v_ref[0, u * sk:(u + 1) * sk, :], dots[u]) for u in range(ATTN_CHAINS)]
            parts = []
            for u in range(ATTN_CHAINS):
                rows = slice(u * sk, (u + 1) * sk)
                pt = jnp.exp2(sts[u] - lse_ref[0, :, qss[u]] * LOG2E)
                if masked:
                    pt = jnp.where(pq_ref[:, qss[u]] >= pk_ref[rows, :], pt, 0.0)
                dv_sc[rows, :] += _dot(pt.astype(BF16), dots[u])
                dst = (pt * (dpts[u] - delta_ref[0, :, qss[u]])).astype(BF16)
                dk_sc[rows, :] += _dot(dst, qts[u])
                parts.append(_dot_tn(dst, k_ref[0, rows, :]))
            if trim:
                for u in range(ATTN_CHAINS):
                    dq_ref[0, qss[u], :] += parts[u]
            else:
                dq_ref[0, qss[0], :] += functools.reduce(lambda a, b: a + b, parts)

        q_first = (kb * tk) // tq
        q_clear = (kb * tk + tk - 1) // tq + 1
        lax.fori_loop(q_first, q_clear, lambda qb, c: (step(qb, True), c)[1], 0)
        lax.fori_loop(q_clear, nq, lambda qb, c: (step(qb, False), c)[1], 0)
        dk_ref[0] = dk_sc[...] * (1.0 / LOG2E)
        dv_ref[0] = dv_sc[...]

        @pl.when(kb == nk - 1)
        def _():
            dq_ref[...] = dq_ref[...] * SCALE

        if ng:
            @pl.when((pl.program_id(0) == nh - 1) & (kb == nk - 1))
            def _():
                for cp in copies:
                    cp.wait()

    arrays = [scatter[nm] for nm in names]
    out = pl.pallas_call(
        body, name=name, grid=(nh, nk),
        in_specs=[pl.BlockSpec((1, s, QK_PAD), lambda h, kb: (h, 0, 0)),
                  pl.BlockSpec((1, tk, QK_PAD), lambda h, kb: (h, kb, 0)),
                  pl.BlockSpec((1, tk, V_DIM), lambda h, kb: (h, kb, 0)),
                  pl.BlockSpec((s, V_DIM), lambda h, kb: (0, h)),
                  pl.BlockSpec((1, 1, s), lambda h, kb: (h, 0, 0)),
                  pl.BlockSpec((1, 1, s), lambda h, kb: (h, 0, 0)),
                  pl.BlockSpec((1, s), lambda h, kb: (0, 0)),
                  pl.BlockSpec((tk, 1), lambda h, kb: (kb, 0))] + [ANY] * ng,
        out_specs=[pl.BlockSpec((1, s, QK_PAD), lambda h, kb: (h, 0, 0)),
                   pl.BlockSpec((1, tk, QK_PAD), lambda h, kb: (h, kb, 0)),
                   pl.BlockSpec((1, tk, V_DIM), lambda h, kb: (h, kb, 0))] + [ANY] * ng,
        scratch_shapes=[pltpu.VMEM((tk, QK_PAD), F32), pltpu.VMEM((tk, V_DIM), F32)]
        + ([pltpu.SemaphoreType.DMA((3 * ng,)), pltpu.SemaphoreType.DMA((3 * ng,))] if ng else []),
        out_shape=[jax.ShapeDtypeStruct((nh, s, QK_PAD), F32), jax.ShapeDtypeStruct((nh, s, QK_PAD), F32),
                   jax.ShapeDtypeStruct((nh, s, V_DIM), F32)] + _scatter_out_shapes(names, arrays),
        compiler_params=_cparams("arbitrary", "arbitrary"),
    )(q, k, v, do, lse_row, delta_row, pos_row, pos_col, *arrays)
    return out[0], out[1], out[2], dict(zip(names, out[3:]))


def _loss_head(y, target, *, name):
    s, d = y.shape
    tm = _tile(s, TM)
    nt = s // tm

    def body(y_ref, t_ref, dy_ref, loss_ref, acc):
        i = pl.program_id(0)

        @pl.when(i == 0)
        def _():
            acc[...] = jnp.zeros_like(acc)

        e = y_ref[...] - t_ref[...]
        dy_ref[...] = e * (1.0 / d)
        acc[...] += jnp.sum((e * e).reshape(tm // 8, 8, d), axis=0)

        @pl.when(i == nt - 1)
        def _():
            loss_ref[...] = jnp.full((1, 128), 0.5 / d, F32) * jnp.sum(acc[...])

    return pl.pallas_call(
        body, name=name, grid=(nt,),
        in_specs=[pl.BlockSpec((tm, d), lambda i: (i, 0))] * 2,
        out_specs=[pl.BlockSpec((tm, d), lambda i: (i, 0)), pl.BlockSpec((1, 128), lambda i: (0, 0))],
        out_shape=[jax.ShapeDtypeStruct((s, d), F32), jax.ShapeDtypeStruct((1, 128), F32)],
        scratch_shapes=[pltpu.VMEM((8, d), F32)],
        compiler_params=_cparams("arbitrary"),
    )(y, target)


def _adamw(w, g, m, v, *, name):
    r, c = w.shape
    tr = _tile(r, 512) if r % 8 == 0 else r

    def body(w_ref, g_ref, m_ref, v_ref, d_ref, nm_ref, nv_ref, go_ref):
        g_t = g_ref[...]
        go_ref[...] = g_t
        nm = ADAM_B1 * m_ref[...] + (1.0 - ADAM_B1) * g_t
        nv = ADAM_B2 * v_ref[...] + (1.0 - ADAM_B2) * (g_t * g_t)
        m_hat = nm / (1.0 - ADAM_B1 ** ADAM_STEP)
        v_hat = nv / (1.0 - ADAM_B2 ** ADAM_STEP)
        d_ref[...] = -ADAM_LR * (m_hat / (jnp.sqrt(v_hat) + ADAM_EPS) + ADAM_WD * w_ref[...])
        nm_ref[...] = nm
        nv_ref[...] = nv

    spec = pl.BlockSpec((tr, c), lambda i: (i, 0))
    return pl.pallas_call(
        body, name=name, grid=(r // tr,), in_specs=[spec] * 4, out_specs=[spec] * 4,
        out_shape=[jax.ShapeDtypeStruct((r, c), F32)] * 4,
        compiler_params=_cparams("parallel"),
    )(w, g, m, v)


def _place():
    return lax.axis_index("x"), lax.axis_index("y"), lax.axis_index("c")


def _other_chips(x, y):
    return [(1 - x, y), (x, 1 - y), (1 - x, 1 - y)]


BIG = {
    "attn_w_down": ((2, 1024, 448), 1), "attn_w_uq": ((2, 256, 1536), 2), "attn_w_ukv": ((2, 128, 2048), 2),
    "attn_w_o": ((2, 1024, 1024), 1), "conv_w_in": ((2, 1024, 3072), 2),
    "conv_w_out": ((2, 1024, 1024), 1), "mlp_w1": ((4, 1024, 4096), 2), "mlp_w2": ((4, 4096, 1024), 1),
}
CONV_W = (2, 3, 1024)


def _shard_shape(name):
    shape, axis = BIG[name]
    return tuple(n // N_CHIPS if i == axis else n for i, n in enumerate(shape))


def _band(ref, name, layers, chip):
    shape, axis = BIG[name]
    width = shape[axis] // N_CHIPS
    if axis == 1:
        return ref.at[layers, pl.ds(chip * width, width), :]
    return ref.at[layers, :, pl.ds(chip * width, width)]


def _half(name, c):
    hl = BIG[name][0][0] // 2
    return pl.ds(c * hl, hl)


def _place_own(w, nm, chip, *, name):
    shape, axis = BIG[nm]
    layers, rows, cols = w.shape
    tr = _sum_rows(rows, cols)
    nrb = rows // tr
    if axis == 1:
        band = lambda l, i, ch: (l, ch[0] * nrb + i, 0)
    else:
        band = lambda l, i, ch: (l, i, ch[0])

    def body(chip_ref, w_ref, o_ref):
        o_ref[...] = w_ref[...].astype(BF16)

    return pl.pallas_call(
        body, name=name,
        grid_spec=pltpu.PrefetchScalarGridSpec(
            num_scalar_prefetch=1, grid=(layers, nrb),
            in_specs=[pl.BlockSpec((1, tr, cols), lambda l, i, ch: (l, i, 0))],
            out_specs=pl.BlockSpec((1, tr, cols), band)),
        out_shape=jax.ShapeDtypeStruct(shape, BF16),
        compiler_params=_cparams("parallel", "parallel"),
    )(chip, w)


def _gather_copies(outs, names, send_sems, recv_sems, *, base, stride, to_sibling):
    x, y, c = _place()
    me = 2 * x + y

    def copy(k, ref, nm, layers, chip, to):
        band = _band(ref, nm, layers, chip)
        return pltpu.make_async_remote_copy(
            src_ref=band, dst_ref=band, send_sem=send_sems.at[k], recv_sem=recv_sems.at[k],
            device_id=to, device_id_type=MESH)

    sends, recvs = [], []
    for i, nm in enumerate(names):
        for j, (cx, cy) in enumerate(_other_chips(x, y)):
            k = base + stride * i + j
            if to_sibling:
                sends.append(copy(k, outs[i], nm, _half(nm, c), 2 * cx + cy, (x, y, 1 - c)))
                recvs.append(copy(k, outs[i], nm, _half(nm, 1 - c), 2 * cx + cy, (x, y, c)))
            else:
                sends.append(copy(k, outs[i], nm, _half(nm, c), me, (cx, cy, c)))
                recvs.append(copy(k, outs[i], nm, _half(nm, c), 2 * cx + cy, (x, y, c)))
    return sends, recvs


def _gather_ici_copies(outs, names, send_sems, recv_sems, *, base, stride):
    return _gather_copies(outs, names, send_sems, recv_sems, base=base, stride=stride, to_sibling=False)


def _gather_weights(fulls, *, name, ici=True):
    names = list(fulls)
    n = len(names)

    def body(*refs):
        outs = refs[n:2 * n]
        sems = refs[2 * n:]
        sent = []
        if ici:
            sends, recvs = _gather_copies(outs, names, *sems, base=0, stride=6, to_sibling=False)
            for cp in sends:
                cp.start()
            for cp in recvs:
                cp.wait_recv()
            sent += sends
        sends, recvs = _gather_copies(outs, names, *sems, base=3, stride=6, to_sibling=True)
        for cp in sends:
            cp.start()
        for cp in recvs:
            cp.wait_recv()
        for cp in sent + sends:
            cp.wait_send()

    arrays = [fulls[nm] for nm in names]
    out = pl.pallas_call(
        body, name=name, in_specs=[ANY] * n, out_specs=[ANY] * n,
        out_shape=[jax.ShapeDtypeStruct(a.shape, a.dtype) for a in arrays],
        input_output_aliases={i: i for i in range(n)},
        scratch_shapes=[pltpu.SemaphoreType.DMA((6 * n,)), pltpu.SemaphoreType.DMA((6 * n,))],
    )(*arrays)
    return dict(zip(names, out))


def _swap_halves(grads, *, name):
    names = list(grads)
    n = len(names)

    def body(*refs):
        copies = _swap_copies(refs[:n], refs[n:2 * n], names, *refs[2 * n:])
        for cp in copies:
            cp.start()
        for cp in copies:
            cp.wait()

    arrays = [grads[nm] for nm in names]
    out = pl.pallas_call(
        body, name=name, in_specs=[ANY] * n, out_specs=[ANY] * n,
        out_shape=_swap_out_shapes(arrays), scratch_shapes=_swap_sems(n),
    )(*arrays)
    return dict(zip(names, out))


def _swap_copies(ins, outs, names, send_sems, recv_sems):
    x, y, c = _place()
    return [pltpu.make_async_remote_copy(
        src_ref=ins[i].at[_half(nm, 1 - c)], dst_ref=outs[i], send_sem=send_sems.at[i], recv_sem=recv_sems.at[i],
        device_id=(x, y, 1 - c), device_id_type=MESH) for i, nm in enumerate(names)]


def _swap_out_shapes(arrays):
    return [jax.ShapeDtypeStruct((a.shape[0] // 2,) + a.shape[1:], a.dtype) for a in arrays]


def _swap_sems(n):
    return [pltpu.SemaphoreType.DMA((n,)), pltpu.SemaphoreType.DMA((n,))] if n else []


def _all_steps(grid, at):
    cond = None
    for axis, size in enumerate(grid):
        this = pl.program_id(axis) == (0 if at == "first" else size - 1)
        cond = this if cond is None else cond & this
    return cond


def _start_at_first_step(copies, grid):
    @pl.when(_all_steps(grid, "first"))
    def _():
        for cp in copies:
            cp.start()


def _wait_at_last_step(copies, grid):
    @pl.when(_all_steps(grid, "last"))
    def _():
        for cp in copies:
            cp.wait()


def _sum_rows(rows, cols):
    t = rows
    while t * cols * 4 > SUM_BLOCK_BYTES and t % 16 == 0:
        t //= 2
    return t


def _chip_sum(g, r1, core, *, name):
    layers, rows, cols = g.shape
    hl = layers // 2
    tr = _sum_rows(rows, cols)

    def body(core_ref, g_ref, r_ref, o_ref):
        o_ref[...] = (g_ref[...] + r_ref[...]).astype(BF16)

    return pl.pallas_call(
        body, name=name,
        grid_spec=pltpu.PrefetchScalarGridSpec(
            num_scalar_prefetch=1, grid=(hl, rows // tr),
            in_specs=[pl.BlockSpec((1, tr, cols), lambda l, i, cr: (cr[0] * hl + l, i, 0)),
                      pl.BlockSpec((1, tr, cols), lambda l, i, cr: (l, i, 0))],
            out_specs=pl.BlockSpec((1, tr, cols), lambda l, i, cr: (l, i, 0))),
        out_shape=jax.ShapeDtypeStruct((hl, rows, cols), BF16),
        compiler_params=_cparams("parallel", "parallel"),
    )(core, g, r1)


def _chip_partials(grads, names, *, tag):
    core = lax.axis_index("c").astype(jnp.int32).reshape(1)
    r1 = _swap_halves({n: grads[n] for n in names}, name=f"grad_swap_halves_{tag}")
    return r1, {n: _chip_sum(grads[n], r1[n], core, name=f"grad_chip_sum_{n}") for n in names}


def _scatter_partials(partials):
    names = list(partials)
    n = len(names)

    def body(*refs):
        copies = _scatter_copies(refs[:n], refs[n:2 * n], names, *refs[2 * n:])
        for cp in copies:
            cp.start()
        for cp in copies:
            cp.wait()

    arrays = [partials[nm] for nm in names]
    out = pl.pallas_call(
        body, name="grad_scatter_partials", in_specs=[ANY] * n, out_specs=[ANY] * n,
        out_shape=_scatter_out_shapes(names, arrays),
        scratch_shapes=[pltpu.SemaphoreType.DMA((3 * n,)), pltpu.SemaphoreType.DMA((3 * n,))],
    )(*arrays)
    return dict(zip(names, out))


def _scatter_copies(ins, outs, names, send_sems, recv_sems):
    x, y, c = _place()
    copies = []
    for i, nm in enumerate(names):
        for j, (cx, cy) in enumerate(_other_chips(x, y)):
            copies.append(pltpu.make_async_remote_copy(
                src_ref=_band(ins[i], nm, slice(None), 2 * cx + cy), dst_ref=outs[i].at[j],
                send_sem=send_sems.at[3 * i + j], recv_sem=recv_sems.at[3 * i + j],
                device_id=(cx, cy, c), device_id_type=MESH))
    return copies


def _scatter_out_shapes(names, arrays):
    return [jax.ShapeDtypeStruct((3, a.shape[0]) + _shard_shape(nm)[1:], a.dtype) for nm, a in zip(names, arrays)]


def _final_sum(g, r1, r2, place, nm, *, name):
    (layers, _, _), axis = BIG[nm]
    hl = layers // 2
    _, rows, cols = _shard_shape(nm)
    tr = _sum_rows(rows, cols)
    nrb = rows // tr
    if axis == 1:
        blk = lambda l, i, pc: (l, pc[1] * nrb + i, 0)
    else:
        blk = lambda l, i, pc: (l, i, pc[1])

    def body(place_ref, g_ref, r1_ref, r2_ref, o_ref):
        acc = g_ref[...] + r1_ref[...]
        for j in range(3):
            acc = acc + r2_ref[j].astype(F32)
        o_ref[...] = acc

    return pl.pallas_call(
        body, name=name,
        grid_spec=pltpu.PrefetchScalarGridSpec(
            num_scalar_prefetch=1, grid=(hl, nrb),
            in_specs=[pl.BlockSpec((1, tr, cols), lambda l, i, pc: blk(pc[0] * hl + l, i, pc)),
                      pl.BlockSpec((1, tr, cols), lambda l, i, pc: blk(l, i, pc)),
                      pl.BlockSpec((3, 1, tr, cols), lambda l, i, pc: (0, l, i, 0))],
            out_specs=pl.BlockSpec((1, tr, cols), lambda l, i, pc: (pc[0] * hl + l, i, 0))),
        out_shape=jax.ShapeDtypeStruct((layers, rows, cols), F32),
        compiler_params=_cparams("parallel", "parallel"),
    )(place, g, r1, r2)


def _join_halves(shards):
    names = list(shards)
    n = len(names)

    def body(*refs):
        outs = refs[n:2 * n]
        send_sems, recv_sems = refs[2 * n:]
        x, y, c = _place()
        copies = []
        for i, nm in enumerate(names):
            mine = outs[i].at[_half(nm, c)]
            cp = pltpu.make_async_remote_copy(
                src_ref=mine, dst_ref=mine, send_sem=send_sems.at[i], recv_sem=recv_sems.at[i],
                device_id=(x, y, 1 - c), device_id_type=MESH)
            cp.start()
            copies.append(cp)
        for i, nm in enumerate(names):
            theirs = outs[i].at[_half(nm, 1 - c)]
            pltpu.make_async_remote_copy(
                src_ref=theirs, dst_ref=theirs, send_sem=send_sems.at[i], recv_sem=recv_sems.at[i],
                device_id=(x, y, 1 - c), device_id_type=MESH).wait_recv()
        for cp in copies:
            cp.wait_send()

    arrays = [shards[nm] for nm in names]
    out = pl.pallas_call(
        body, name="grad_join_halves", in_specs=[ANY] * n, out_specs=[ANY] * n,
        out_shape=[jax.ShapeDtypeStruct(a.shape, a.dtype) for a in arrays],
        input_output_aliases={i: i for i in range(n)},
        scratch_shapes=[pltpu.SemaphoreType.DMA((n,)), pltpu.SemaphoreType.DMA((n,))],
    )(*arrays)
    return dict(zip(names, out))


def _all_reduce_small(part, *, name):
    rows, cols = part.shape
    vm = pl.BlockSpec(memory_space=pltpu.VMEM)

    def body(p_ref, o_ref, land, send_sems, recv_sems):
        x, y, c = _place()
        me = 4 * x + 2 * y + c
        flips = [(dx, dy, dc) for dx in (0, 1) for dy in (0, 1) for dc in (0, 1)][1:]
        copies = []
        for k, (dx, dy, dc) in enumerate(flips):
            cp = pltpu.make_async_remote_copy(
                src_ref=p_ref, dst_ref=land.at[me], send_sem=send_sems.at[k], recv_sem=recv_sems.at[k],
                device_id=(1 - x if dx else x, 1 - y if dy else y, 1 - c if dc else c), device_id_type=MESH)
            cp.start()
            copies.append(cp)
        land[me] = p_ref[...]
        for cp in copies:
            cp.wait()
        acc = land[0]
        for j in range(1, 8):
            acc = acc + land[j]
        o_ref[...] = acc

    return pl.pallas_call(
        body, name=name, in_specs=[vm], out_specs=vm,
        out_shape=jax.ShapeDtypeStruct((rows, cols), F32),
        scratch_shapes=[pltpu.VMEM((8, rows, cols), F32), pltpu.SemaphoreType.DMA((7,)), pltpu.SemaphoreType.DMA((7,))],
    )(part)


SMALL = {"g_mix": (4, 1024), "g_mlp": (4, 1024), "attn_g_q_a": (2, 256), "attn_g_kv_a": (2, 128),
         "attn_g_qnorm": (2, 192), "attn_g_knorm": (2, 192)}
SMALL_GRADS = {**SMALL, "conv_w": CONV_W}
WEIGHT_ORDER = ["g_mix", "g_mlp", "attn_w_down", "attn_g_q_a", "attn_g_kv_a", "attn_w_uq", "attn_w_ukv",
                "attn_g_qnorm", "attn_g_knorm", "attn_w_o", "conv_w_in", "conv_w", "conv_w_out", "mlp_w1", "mlp_w2"]


def _prod(shape):
    n = 1
    for v in shape:
        n *= v
    return n


def _pack_small(parts, table):
    flat = [parts[n].reshape(-1) for n in table]
    size = sum(_prod(s) for s in table.values())
    rows = -(-size // (8 * 128)) * 8
    flat.append(jnp.zeros((rows * 128 - size,), F32))
    return jnp.concatenate(flat).reshape(rows, 128)


def _unpack_small(buf, table):
    flat = buf.reshape(-1)
    out, off = {}, 0
    for n, shp in table.items():
        out[n] = flat[off:off + _prod(shp)].reshape(shp)
        off += _prod(shp)
    return out


def _rope_tables(positions):
    inv_freq = ROPE_THETA ** (-jnp.arange(0, QK_ROPE, 2, dtype=F32) / QK_ROPE)
    ang = positions.astype(F32)[:, None] * inv_freq
    cos, sin = jnp.cos(ang), jnp.sin(ang)
    z32 = jnp.zeros_like(cos)
    z64 = jnp.zeros((positions.shape[0], 64), F32)
    cc = jnp.concatenate([cos, cos, z64], axis=1)
    sa = jnp.concatenate([-sin, z32, z64], axis=1)
    sb = jnp.concatenate([z32, sin, z64], axis=1)
    return cc, sa, sb


def _pad_heads(w, width):
    k = w.shape[0]
    w = w.reshape(k, N_HEADS, width)
    return jnp.pad(w, ((0, 0), (0, 0), (0, QK_PAD - width))).reshape(k, N_HEADS * QK_PAD)


EARLY = ("mlp_w1", "mlp_w2", "conv_w_in", "conv_w_out")
LATE = ("attn_w_down", "attn_w_uq", "attn_w_ukv", "attn_w_o")
GATHER_LATER = EARLY


def _local_step(x, positions, target, wb, gains, later=None):
    s = x.shape[0]
    cc, sa, sb = _rope_tables(positions)
    pos_col = positions.reshape(s, 1)
    pos_row = positions.reshape(1, s)

    saved = []
    for i in range(4):
        j = i // 2
        g_mix = gains["g_mix"][i:i + 1]
        g_mlp = gains["g_mlp"][i:i + 1]
        if i % 2 == 0:
            w_down = jnp.pad(wb["attn_w_down"][j], ((0, 0), (0, DOWN_PAD - DOWN_DIM)))
            w_uq = _pad_heads(wb["attn_w_uq"][j], QK_DIM)
            w_ukv = wb["attn_w_ukv"][j]
            g_qa = gains["attn_g_q_a"][j:j + 1]
            g_kva = gains["attn_g_kv_a"][j:j + 1]
            g_q = jnp.pad(gains["attn_g_qnorm"][j:j + 1], ((0, 0), (0, QK_PAD - QK_DIM)))
            g_k = jnp.pad(gains["attn_g_knorm"][j:j + 1], ((0, 0), (0, QK_PAD - QK_DIM)))
            h, a = _norm_mm(x, g_mix, w_down, out_dtype=F32, name=f"mla_down_{j}")
            cq, ckv, q, k, v = _mla_prep(a, g_qa, g_kva, w_uq, w_ukv, g_q, g_k, cc, sa, sb, name=f"mla_prep_{j}")
            o, lse, got = _flash_fwd(q, k, v, pos_col, pos_row, name=f"flash_fwd_{j}",
                                     gather=later if i == 0 else None)
            if got:
                wb = {**wb, **_gather_weights(got, name="gather_later_forward", ici=False)}
            x_mid = _mm_nn(o, wb["attn_w_o"], layer=j, out_dtype=F32, residual=x, name=f"mla_out_{j}")
            mix = dict(h=h, a=a, cq=cq, ckv=ckv, q=q, k=k, v=v, o=o, lse=lse, w_down=w_down, w_uq=w_uq, w_ukv=w_ukv,
                       g_qa=g_qa, g_kva=g_kva, g_q=g_q, g_k=g_k)
        else:
            h, bcu = _norm_mm(x, g_mix, wb["conv_w_in"], layer=j, out_dtype=BF16, name=f"conv_in_{j}")
            z = _conv_gate(bcu, gains["conv_w"][j], name=f"conv_gate_{j}")
            x_mid = _mm_nn(z, wb["conv_w_out"], layer=j, out_dtype=F32, residual=x, name=f"conv_out_{j}")
            mix = dict(h=h, bcu=bcu, z=z)
        h2, u, x_out = _mlp_fwd(x_mid, g_mlp, wb["mlp_w1"], wb["mlp_w2"], layer=i, name=f"mlp_fwd_{i}")
        saved.append(dict(x_in=x, x_mid=x_mid, mix=mix, h2=h2, u=u, g_mix=g_mix, g_mlp=g_mlp))
        x = x_out

    dx, loss = _loss_head(x, target, name="loss_head")

    gw = {n: None for n in BIG}
    exchanged = None
    g_uq = [None, None]
    gs = {n: [None] * SMALL_GRADS[n][0] for n in SMALL_GRADS}

    def wgrad(nm, layer, a, b, **kw):
        out = _mm_tn(a, b, stack=gw[nm], layer=layer, layers=BIG[nm][0][0], name=f"{nm}_grad_{layer}", **kw)
        gw[nm], arrived = out if kw.get("swap") else (out, None)
        return arrived

    for i in reversed(range(4)):
        j = i // 2
        sv = saved[i]
        mix = sv["mix"]
        ride = i == 0 and later is not None
        du = _mlp_down_bwd(dx, wb["mlp_w2"], sv["u"], layer=i, name=f"mlp_down_bwd_{i}")
        wgrad("mlp_w2", i, sv["u"], dx, sqrelu_a=True)
        r1_early = wgrad("mlp_w1", i, sv["h2"], du,
                         swap={n: gw[n] for n in ("mlp_w2", "conv_w_in", "conv_w_out")} if ride else None)
        dx, dg, *arrived = _nt_rms_bwd(du, wb["mlp_w1"], sv["x_mid"], sv["g_mlp"], dx, layer=i, name=f"mlp_up_bwd_{i}",
                                       swap={"mlp_w1": gw["mlp_w1"]} if ride else None)
        if ride:
            r1_early.update(arrived[0])
        gs["g_mlp"][i] = dg[0]
        if i % 2 == 0:
            do, delta_row = _attn_out_bwd(dx, wb["attn_w_o"], mix["o"], layer=j, name=f"mla_out_bwd_{j}")
            wgrad("attn_w_o", j, mix["o"], dx)
            lse_row = mix["lse"]
            partials = None
            if ride:
                core = lax.axis_index("c").astype(jnp.int32).reshape(1)
                partials = {n: _chip_sum(gw[n], r1_early[n], core, name=f"grad_chip_sum_{n}") for n in EARLY}
            dq, dk, dv, arrived = _flash_bwd(mix["q"], mix["k"], mix["v"], do, lse_row, delta_row, pos_col, pos_row,
                                             name=f"flash_bwd_{j}", scatter=partials)
            if partials is not None:
                exchanged = (r1_early, arrived)
            dqr, dkvr, da, dgq, dgk, dgqa, dgkva = _mla_prep_bwd(
                mix["a"], mix["g_qa"], mix["g_kva"], mix["w_uq"], mix["w_ukv"], mix["g_q"], mix["g_k"], cc, sa, sb,
                dq, dk, dv, name=f"mla_prep_bwd_{j}")
            g_uq[j] = _mm_tn(mix["cq"], dqr, name=f"attn_w_uq_grad_{j}")[0]
            wgrad("attn_w_ukv", j, mix["ckv"], dkvr)
            wgrad("attn_w_down", j, mix["h"], da, keep=DOWN_DIM)
            dx, dg = _nt_rms_bwd(da, mix["w_down"], sv["x_in"], sv["g_mix"], dx, name=f"mla_down_bwd_{j}")
            gs["attn_g_qnorm"][j] = dgq[0, :QK_DIM]
            gs["attn_g_knorm"][j] = dgk[0, :QK_DIM]
            gs["attn_g_q_a"][j] = dgqa[0]
            gs["attn_g_kv_a"][j] = dgkva[0]
        else:
            dz = _mm_nt(dx, wb["conv_w_out"], layer=j, out_dtype=F32, name=f"conv_out_bwd_{j}")
            wgrad("conv_w_out", j, mix["z"], dx)
            dbcu, dcw = _conv_gate_bwd(mix["bcu"], dz, gains["conv_w"][j], name=f"conv_gate_bwd_{j}")
            gs["conv_w"][j] = dcw
            wgrad("conv_w_in", j, mix["h"], dbcu)
            dx, dg = _nt_rms_bwd(dbcu, wb["conv_w_in"], sv["x_in"], sv["g_mix"], dx, layer=j, name=f"conv_in_bwd_{j}")
        gs["g_mix"][i] = dg[0]

    gw["attn_w_uq"] = jnp.stack(g_uq).reshape(2, Q_RANK, N_HEADS, QK_PAD)[..., :QK_DIM].reshape(BIG["attn_w_uq"][0])
    grads_small = {n: jnp.stack(v) for n, v in gs.items()}
    return loss, dx, gw, grads_small, exchanged


def kernel(x, positions, g_mix, g_mlp, attn_w_down, attn_g_q_a, attn_g_kv_a, attn_w_uq, attn_w_ukv, attn_g_qnorm, attn_g_knorm, attn_w_o, conv_w_in, conv_w, conv_w_out, mlp_w1, mlp_w2, loss_target, m_g_mix, m_g_mlp, m_attn_w_down, m_attn_g_q_a, m_attn_g_kv_a, m_attn_w_uq, m_attn_w_ukv, m_attn_g_qnorm, m_attn_g_knorm, m_attn_w_o, m_conv_w_in, m_conv_w, m_conv_w_out, m_mlp_w1, m_mlp_w2, v_g_mix, v_g_mlp, v_attn_w_down, v_attn_g_q_a, v_attn_g_kv_a, v_attn_w_uq, v_attn_w_ukv, v_attn_g_qnorm, v_attn_g_knorm, v_attn_w_o, v_conv_w_in, v_conv_w, v_conv_w_out, v_mlp_w1, v_mlp_w2):
    w = dict(g_mix=g_mix, g_mlp=g_mlp, attn_w_down=attn_w_down, attn_g_q_a=attn_g_q_a, attn_g_kv_a=attn_g_kv_a,
             attn_w_uq=attn_w_uq, attn_w_ukv=attn_w_ukv, attn_g_qnorm=attn_g_qnorm, attn_g_knorm=attn_g_knorm,
             attn_w_o=attn_w_o, conv_w_in=conv_w_in, conv_w=conv_w, conv_w_out=conv_w_out, mlp_w1=mlp_w1, mlp_w2=mlp_w2)
    m = dict(g_mix=m_g_mix, g_mlp=m_g_mlp, attn_w_down=m_attn_w_down, attn_g_q_a=m_attn_g_q_a,
             attn_g_kv_a=m_attn_g_kv_a, attn_w_uq=m_attn_w_uq, attn_w_ukv=m_attn_w_ukv, attn_g_qnorm=m_attn_g_qnorm,
             attn_g_knorm=m_attn_g_knorm, attn_w_o=m_attn_w_o, conv_w_in=m_conv_w_in, conv_w=m_conv_w,
             conv_w_out=m_conv_w_out, mlp_w1=m_mlp_w1, mlp_w2=m_mlp_w2)
    v = dict(g_mix=v_g_mix, g_mlp=v_g_mlp, attn_w_down=v_attn_w_down, attn_g_q_a=v_attn_g_q_a,
             attn_g_kv_a=v_attn_g_kv_a, attn_w_uq=v_attn_w_uq, attn_w_ukv=v_attn_w_ukv, attn_g_qnorm=v_attn_g_qnorm,
             attn_g_knorm=v_attn_g_knorm, attn_w_o=v_attn_w_o, conv_w_in=v_conv_w_in, conv_w=v_conv_w,
             conv_w_out=v_conv_w_out, mlp_w1=v_mlp_w1, mlp_w2=v_mlp_w2)
    cx, cy, cc_ = _place()

    chip = 2 * cx + cy

    def own_offset(shape, axis):
        return tuple(chip * (shape[axis] // N_CHIPS) if i == axis else 0 for i in range(3))

    chip_arr = chip.astype(jnp.int32).reshape(1)
    fulls = {n: _place_own(w[n], n, chip_arr, name=f"place_{n}") for n in BIG}
    later = {n: fulls.pop(n) for n in GATHER_LATER}
    wb = _gather_weights(fulls, name="gather_weights")

    placed = lax.dynamic_update_slice(jnp.zeros(CONV_W, F32), conv_w, own_offset(CONV_W, 2))
    conv_w_full = 0.5 * _all_reduce_small(placed.reshape(-1, 128), name="conv_w_gather").reshape(CONV_W)

    gains = {n: w[n] for n in SMALL}
    gains["conv_w"] = conv_w_full

    loss, grad_x, grads_big, grads_small, (r1_early, r2_early) = _local_step(
        x[0], positions[0], loss_target[0], wb, gains, later)

    place = jnp.stack([cc_, chip]).astype(jnp.int32)
    r1_late, partials = _chip_partials(grads_big, LATE, tag="late")
    r1 = {**r1_early, **r1_late}
    r2 = {**r2_early, **_scatter_partials(partials)}
    halves = {n: _final_sum(grads_big[n], r1[n], r2[n], place, n, name=f"grad_final_sum_{n}") for n in BIG}
    grad_shards = _join_halves(halves)

    small = _unpack_small(_all_reduce_small(_pack_small(grads_small, SMALL_GRADS), name="gain_all_reduce"), SMALL_GRADS)
    grad_shards["conv_w"] = lax.dynamic_slice(small["conv_w"], own_offset(CONV_W, 2), conv_w.shape)

    loss_total = lax.psum(loss[0, 0], ("x", "y", "c"))

    grads, deltas, new_m, new_v = {}, {}, {}, {}
    for n in [*BIG, "conv_w"]:
        shp = w[n].shape
        two_d = (shp[0] * shp[1], shp[2])
        g2 = grad_shards[n].reshape(two_d)
        d, nm, nv, g = _adamw(w[n].reshape(two_d), g2, m[n].reshape(two_d), v[n].reshape(two_d), name=f"adamw_{n}")
        grads[n], deltas[n], new_m[n], new_v[n] = g.reshape(shp), d.reshape(shp), nm.reshape(shp), nv.reshape(shp)
    d, nm, nv, g = _adamw(_pack_small(w, SMALL), _pack_small(small, SMALL), _pack_small(m, SMALL),
                          _pack_small(v, SMALL), name="adamw_gains")
    d, nm, nv, g = (_unpack_small(t, SMALL) for t in (d, nm, nv, g))
    for n in SMALL:
        grads[n], deltas[n], new_m[n], new_v[n] = g[n], d[n], nm[n], nv[n]

    return (loss_total, grad_x[None],
            *[grads[n] for n in WEIGHT_ORDER], *[deltas[n] for n in WEIGHT_ORDER],
            *[new_m[n] for n in WEIGHT_ORDER], *[new_v[n] for n in WEIGHT_ORDER])
```

```python
import functools

import jax
import jax.numpy as jnp
from jax import lax
from jax.experimental import pallas as pl
from jax.experimental.pallas import tpu as pltpu

F32 = jnp.float32
BF16 = jnp.bfloat16

D_MODEL = 1024
N_HEADS = 8
QK_NOPE = 128
QK_ROPE = 64
QK_DIM = QK_NOPE + QK_ROPE
QK_PAD = 256
V_DIM = 128
Q_RANK = 256
KV_RANK = 128
DOWN_DIM = Q_RANK + KV_RANK + QK_ROPE
DOWN_PAD = 512
ROPE_THETA = 10000.0
EPS = 1e-6
NEG = -1e30
SCALE = QK_DIM ** -0.5
SCALE_LOG2E = SCALE * 1.4426950408889634
LOG2E = 1.4426950408889634
ATTN_CHAINS = 2

ADAM_LR = 0.001
ADAM_B1 = 0.9
ADAM_B2 = 0.999
ADAM_EPS = 1e-08
ADAM_WD = 0.01
ADAM_STEP = 10

N_CHIPS = 4
MESH = pl.DeviceIdType.MESH
ANY = pl.BlockSpec(memory_space=pl.ANY)

TM = 512
TM_WIDE = 512
FWD_TQ = 1024
FWD_TK = 1024
BWD_TQ = 1024
BWD_TK = 1024
HALO = 16
T_PREP = 512
T_RED = 2048
SUM_BLOCK_BYTES = 2 * 1024 * 1024


def _tile(n, pref):
    t = min(n, pref)
    assert n % t == 0, (n, t)
    return t


def _cparams(*sem):
    return pltpu.CompilerParams(dimension_semantics=sem)


def _dot(a, b):
    return jnp.dot(a, b, preferred_element_type=F32)


def _dot_nt(a, b):
    return lax.dot_general(a, b, (((1,), (1,)), ((), ())), preferred_element_type=F32)


def _dot_tn(a, b):
    return lax.dot_general(a, b, (((0,), (0,)), ((), ())), preferred_element_type=F32)


def _rms(x, width):
    r = lax.rsqrt(jnp.sum(x * x, axis=-1, keepdims=True) * (1.0 / width) + EPS)
    return x * r, r


def _rms_bwd(xhat, r, dxhat, width):
    return r * (dxhat - xhat * (jnp.sum(dxhat * xhat, axis=-1, keepdims=True) * (1.0 / width)))


def _rope(t, cc, sa, sb):
    return t * cc + pltpu.roll(t, 96, 1) * sa + pltpu.roll(t, 32, 1) * sb


def _rope_t(g, cc, sa, sb):
    return g * cc + pltpu.roll(g * sa, 32, 1) + pltpu.roll(g * sb, 96, 1)


def _wspec(w, layer):
    once = pl.Buffered(1)
    if w.ndim == 2:
        return pl.BlockSpec(w.shape, lambda *_: (0, 0), pipeline_mode=once)
    return pl.BlockSpec((None,) + w.shape[1:], lambda *_: (layer, 0, 0), pipeline_mode=once)


def _mm_nn(a, b, *, out_dtype, name, residual=None, layer=0):
    m, k = a.shape
    n = b.shape[-1]
    tm = _tile(m, TM)

    def body(*refs):
        if residual is None:
            a_ref, b_ref, o_ref = refs
        else:
            a_ref, b_ref, r_ref, o_ref = refs
        acc = _dot(a_ref[...].astype(BF16), b_ref[...])
        if residual is not None:
            acc = acc + r_ref[...]
        o_ref[...] = acc.astype(o_ref.dtype)

    in_specs = [pl.BlockSpec((tm, k), lambda i: (i, 0)), _wspec(b, layer)]
    args = [a, b]
    if residual is not None:
        in_specs.append(pl.BlockSpec((tm, n), lambda i: (i, 0)))
        args.append(residual)
    return pl.pallas_call(
        body, name=name, grid=(m // tm,), in_specs=in_specs,
        out_specs=pl.BlockSpec((tm, n), lambda i: (i, 0)),
        out_shape=jax.ShapeDtypeStruct((m, n), out_dtype),
        compiler_params=_cparams("parallel"),
    )(*args)


def _mm_nt(a, b, *, out_dtype, name, layer=0):
    m, k = a.shape
    n = b.shape[-2]
    tm = _tile(m, TM)

    def body(a_ref, b_ref, o_ref):
        o_ref[...] = _dot_nt(a_ref[...].astype(BF16), b_ref[...]).astype(o_ref.dtype)

    return pl.pallas_call(
        body, name=name, grid=(m // tm,),
        in_specs=[pl.BlockSpec((tm, k), lambda i: (i, 0)), _wspec(b, layer)],
        out_specs=pl.BlockSpec((tm, n), lambda i: (i, 0)),
        out_shape=jax.ShapeDtypeStruct((m, n), out_dtype),
        compiler_params=_cparams("parallel"),
    )(a, b)


def _mm_tn(a, b, *, name, stack=None, layer=0, layers=1, keep=None, sqrelu_a=False, swap=None):
    s, ka = a.shape
    n = b.shape[1]
    ts = _tile(s, T_RED)
    tka = _tile(ka, 1024)
    tn = _tile(n, 1024)
    n_out = n if keep is None else keep
    assert keep is None or tn == n
    grid = (ka // tka, n // tn, s // ts)
    names = list(swap or {})
    ns = len(names)
    n_in = 2 + (stack is not None)

    def body(*refs):
        a_ref, b_ref = refs[:2]
        o_ref = refs[n_in + ns]
        if ns:
            copies = _swap_copies(refs[n_in:n_in + ns], refs[n_in + ns + 1:n_in + 2 * ns + 1], names,
                                  *refs[n_in + 2 * ns + 1:])
            _start_at_first_step(copies, grid)

        @pl.when(pl.program_id(2) == 0)
        def _():
            o_ref[...] = jnp.zeros_like(o_ref)

        a_t = a_ref[...]
        if sqrelu_a:
            a_t = _sqrelu(a_t.astype(F32))
        o_ref[...] += _dot_tn(a_t.astype(BF16), b_ref[...].astype(BF16))[:, :n_out if keep else tn]
        if ns:
            _wait_at_last_step(copies, grid)

    in_specs = [pl.BlockSpec((ts, tka), lambda i, j, t: (t, i)), pl.BlockSpec((ts, tn), lambda i, j, t: (t, j))]
    args = [a, b]
    if stack is not None:
        in_specs.append(ANY)
        args.append(stack)
    sent = [swap[nm] for nm in names]
    out = pl.pallas_call(
        body, name=name, grid=grid, in_specs=in_specs + [ANY] * ns,
        out_specs=[pl.BlockSpec((None, tka, tn if keep is None else keep), lambda i, j, t: (layer, i, j))] + [ANY] * ns,
        out_shape=[jax.ShapeDtypeStruct((layers, ka, n_out), F32)] + _swap_out_shapes(sent),
        scratch_shapes=_swap_sems(ns),
        input_output_aliases={} if stack is None else {2: 0},
        compiler_params=_cparams(*(["arbitrary"] * 3 if ns else ["parallel", "parallel", "arbitrary"])),
    )(*args, *sent)
    return (out[0], dict(zip(names, out[1:]))) if ns else out[0]


def _norm_mm(x, g, w, *, out_dtype, name, layer=0):
    s, d = x.shape
    n = w.shape[-1]
    tm = _tile(s, TM)

    def body(x_ref, g_ref, w_ref, h_ref, o_ref):
        xhat, _ = _rms(x_ref[...], d)
        h = (xhat * g_ref[...]).astype(BF16)
        h_ref[...] = h
        o_ref[...] = _dot(h, w_ref[...]).astype(o_ref.dtype)

    return pl.pallas_call(
        body, name=name, grid=(s // tm,),
        in_specs=[pl.BlockSpec((tm, d), lambda i: (i, 0)), pl.BlockSpec((1, d), lambda i: (0, 0)), _wspec(w, layer)],
        out_specs=[pl.BlockSpec((tm, d), lambda i: (i, 0)), pl.BlockSpec((tm, n), lambda i: (i, 0))],
        out_shape=[jax.ShapeDtypeStruct((s, d), BF16), jax.ShapeDtypeStruct((s, n), out_dtype)],
        compiler_params=_cparams("parallel"),
    )(x, g, w)


def _nt_rms_bwd(dy, w, x, g, dres, *, name, layer=0, swap=None):
    s, n = dy.shape
    d = x.shape[1]
    tm = _tile(s, TM)
    grid = (s // tm,)
    names = list(swap or {})
    ns = len(names)

    def body(dy_ref, w_ref, x_ref, g_ref, dres_ref, *rest):
        dx_ref, dg_ref = rest[ns:ns + 2]
        if ns:
            copies = _swap_copies(rest[:ns], rest[ns + 2:2 * ns + 2], names, *rest[2 * ns + 2:])
            _start_at_first_step(copies, grid)

        @pl.when(pl.program_id(0) == 0)
        def _():
            dg_ref[...] = jnp.zeros_like(dg_ref)

        dh = _dot_nt(dy_ref[...], w_ref[...])
        xhat, r = _rms(x_ref[...], d)
        dg_ref[...] += jnp.sum(dh * xhat, axis=0, keepdims=True)
        dx_ref[...] = dres_ref[...] + _rms_bwd(xhat, r, dh * g_ref[...], d)
        if ns:
            _wait_at_last_step(copies, grid)

    sent = [swap[nm] for nm in names]
    out = pl.pallas_call(
        body, name=name, grid=grid,
        in_specs=[pl.BlockSpec((tm, n), lambda i: (i, 0)), _wspec(w, layer),
                  pl.BlockSpec((tm, d), lambda i: (i, 0)), pl.BlockSpec((1, d), lambda i: (0, 0)),
                  pl.BlockSpec((tm, d), lambda i: (i, 0))] + [ANY] * ns,
        out_specs=[pl.BlockSpec((tm, d), lambda i: (i, 0)), pl.BlockSpec((1, d), lambda i: (0, 0))] + [ANY] * ns,
        out_shape=[jax.ShapeDtypeStruct((s, d), F32), jax.ShapeDtypeStruct((1, d), F32)] + _swap_out_shapes(sent),
        scratch_shapes=_swap_sems(ns),
        compiler_params=_cparams("arbitrary"),
    )(dy, w, x, g, dres, *sent)
    return (out[0], out[1], dict(zip(names, out[2:]))) if ns else (out[0], out[1])


def _sqrelu(u):
    return jnp.square(jnp.maximum(u, 0.0))


def _mlp_fwd(x, g, w1, w2, *, name, layer=0):
    s, d = x.shape
    n = w1.shape[-1]
    tm = _tile(s, TM_WIDE)

    def body(x_ref, g_ref, w1_ref, w2_ref, h_ref, u_ref, y_ref):
        x_t = x_ref[...]
        xhat, _ = _rms(x_t, d)
        h = (xhat * g_ref[...]).astype(BF16)
        h_ref[...] = h
        u = _dot(h, w1_ref[...])
        u_ref[...] = u.astype(BF16)
        y_ref[...] = x_t + _dot(_sqrelu(u).astype(BF16), w2_ref[...])

    return pl.pallas_call(
        body, name=name, grid=(s // tm,),
        in_specs=[pl.BlockSpec((tm, d), lambda i: (i, 0)), pl.BlockSpec((1, d), lambda i: (0, 0)),
                  _wspec(w1, layer), _wspec(w2, layer)],
        out_specs=[pl.BlockSpec((tm, d), lambda i: (i, 0)), pl.BlockSpec((tm, n), lambda i: (i, 0)),
                   pl.BlockSpec((tm, d), lambda i: (i, 0))],
        out_shape=[jax.ShapeDtypeStruct((s, d), BF16), jax.ShapeDtypeStruct((s, n), BF16),
                   jax.ShapeDtypeStruct((s, d), F32)],
        compiler_params=_cparams("parallel"),
    )(x, g, w1, w2)


def _mlp_down_bwd(dy, w2, u, *, name, layer=0):
    s, d = dy.shape
    n = w2.shape[-2]
    tm = _tile(s, TM_WIDE)

    def body(dy_ref, w_ref, u_ref, du_ref):
        dact = _dot_nt(dy_ref[...].astype(BF16), w_ref[...])
        du_ref[...] = (dact * (2.0 * jnp.maximum(u_ref[...].astype(F32), 0.0))).astype(BF16)

    return pl.pallas_call(
        body, name=name, grid=(s // tm,),
        in_specs=[pl.BlockSpec((tm, d), lambda i: (i, 0)), _wspec(w2, layer),
                  pl.BlockSpec((tm, n), lambda i: (i, 0))],
        out_specs=pl.BlockSpec((tm, n), lambda i: (i, 0)),
        out_shape=jax.ShapeDtypeStruct((s, n), BF16),
        compiler_params=_cparams("parallel"),
    )(dy, w2, u)


def _conv_gate(bcu, conv_w, *, name):
    s = bcu.shape[0]
    d = D_MODEL
    tm = _tile(s, TM)
    hb = tm // HALO

    def body(bcu_ref, prev_ref, w_ref, z_ref, pbuf):
        i = pl.program_id(0)
        gb = bcu_ref[:, 0:d].astype(F32)
        p = bcu_ref[:, d:2 * d].astype(F32) * bcu_ref[:, 2 * d:3 * d].astype(F32)
        pprev = prev_ref[:, d:2 * d].astype(F32) * prev_ref[:, 2 * d:3 * d].astype(F32)
        pbuf[0:HALO, :] = jnp.where(i > 0, pprev, 0.0)
        pbuf[HALO:HALO + tm, :] = p
        cv = (w_ref[2:3, :] * p + w_ref[1:2, :] * pbuf[HALO - 1:HALO - 1 + tm, :]
              + w_ref[0:1, :] * pbuf[HALO - 2:HALO - 2 + tm, :])
        z_ref[...] = (gb * cv).astype(BF16)

    return pl.pallas_call(
        body, name=name, grid=(s // tm,),
        in_specs=[pl.BlockSpec((tm, 3 * d), lambda i: (i, 0)),
                  pl.BlockSpec((HALO, 3 * d), lambda i: (jnp.maximum(i * hb - 1, 0), 0)),
                  pl.BlockSpec((3, d), lambda i: (0, 0))],
        out_specs=pl.BlockSpec((tm, d), lambda i: (i, 0)),
        out_shape=jax.ShapeDtypeStruct((s, d), BF16),
        scratch_shapes=[pltpu.VMEM((tm + HALO, d), F32)],
        compiler_params=_cparams("parallel"),
    )(bcu, bcu, conv_w)


def _conv_gate_bwd(bcu, dz, conv_w, *, name):
    s = bcu.shape[0]
    d = D_MODEL
    tm = _tile(s, TM)
    hb = tm // HALO
    nt = s // tm

    def body(bcu_ref, prev_ref, next_ref, dz_ref, dznext_ref, w_ref, dbcu_ref, dw_ref, pbuf, dbuf):
        i = pl.program_id(0)

        @pl.when(i == 0)
        def _():
            dw_ref[...] = jnp.zeros_like(dw_ref)

        gb = bcu_ref[:, 0:d].astype(F32)
        gc = bcu_ref[:, d:2 * d].astype(F32)
        uu = bcu_ref[:, 2 * d:3 * d].astype(F32)
        p = gc * uu
        pprev = prev_ref[:, d:2 * d].astype(F32) * prev_ref[:, 2 * d:3 * d].astype(F32)
        pbuf[0:HALO, :] = jnp.where(i > 0, pprev, 0.0)
        pbuf[HALO:HALO + tm, :] = p
        p1 = pbuf[HALO - 1:HALO - 1 + tm, :]
        p2 = pbuf[HALO - 2:HALO - 2 + tm, :]
        cv = w_ref[2:3, :] * p + w_ref[1:2, :] * p1 + w_ref[0:1, :] * p2
        dz_t = dz_ref[...]
        dcv = dz_t * gb
        dcv_next = dznext_ref[...] * next_ref[:, 0:d].astype(F32)
        dbuf[0:tm, :] = dcv
        dbuf[tm:tm + HALO, :] = jnp.where(i < nt - 1, dcv_next, 0.0)
        dp = w_ref[2:3, :] * dcv + w_ref[1:2, :] * dbuf[1:1 + tm, :] + w_ref[0:1, :] * dbuf[2:2 + tm, :]
        dw_ref[2:3, :] += jnp.sum(dcv * p, axis=0, keepdims=True)
        dw_ref[1:2, :] += jnp.sum(dcv * p1, axis=0, keepdims=True)
        dw_ref[0:1, :] += jnp.sum(dcv * p2, axis=0, keepdims=True)
        dbcu_ref[:, 0:d] = (dz_t * cv).astype(BF16)
        dbcu_ref[:, d:2 * d] = (dp * uu).astype(BF16)
        dbcu_ref[:, 2 * d:3 * d] = (dp * gc).astype(BF16)

    nxt = lambda i: (jnp.minimum((i + 1) * hb, s // HALO - 1), 0)
    return pl.pallas_call(
        body, name=name, grid=(nt,),
        in_specs=[pl.BlockSpec((tm, 3 * d), lambda i: (i, 0)),
                  pl.BlockSpec((HALO, 3 * d), lambda i: (jnp.maximum(i * hb - 1, 0), 0)),
                  pl.BlockSpec((HALO, 3 * d), nxt),
                  pl.BlockSpec((tm, d), lambda i: (i, 0)),
                  pl.BlockSpec((HALO, d), nxt),
                  pl.BlockSpec((3, d), lambda i: (0, 0))],
        out_specs=[pl.BlockSpec((tm, 3 * d), lambda i: (i, 0)), pl.BlockSpec((3, d), lambda i: (0, 0))],
        out_shape=[jax.ShapeDtypeStruct((s, 3 * d), BF16), jax.ShapeDtypeStruct((3, d), F32)],
        scratch_shapes=[pltpu.VMEM((tm + HALO, d), F32), pltpu.VMEM((tm + HALO, d), F32)],
        compiler_params=_cparams("arbitrary"),
    )(bcu, bcu, bcu, dz, dz, conv_w)


def _mla_prep(a, g_qa, g_kva, w_uq, w_ukv, g_q, g_k, cc, sa, sb, *, name):
    s = a.shape[0]
    ts = _tile(s, T_PREP)

    def body(a_ref, gqa_ref, gkva_ref, wuq_ref, wukv_ref, gq_ref, gk_ref, cc_ref, sa_ref, sb_ref,
             cq_ref, ckv_ref, q_ref, k_ref, v_ref):
        xq, _ = _rms(a_ref[:, 0:Q_RANK], Q_RANK)
        cq = (xq * gqa_ref[...]).astype(BF16)
        cq_ref[...] = cq
        xkv, _ = _rms(a_ref[:, Q_RANK:Q_RANK + KV_RANK], KV_RANK)
        ckv = (xkv * gkva_ref[...]).astype(BF16)
        ckv_ref[...] = ckv
        kpe = a_ref[:, Q_RANK + KV_RANK:DOWN_PAD]
        kpe_ss = jnp.sum(kpe * kpe, axis=-1, keepdims=True)
        cc_t, sa_t, sb_t = cc_ref[...], sa_ref[...], sb_ref[...]
        gq = gq_ref[...]
        gk = gk_ref[...]
        for h in range(N_HEADS):
            cols = slice(h * QK_PAD, (h + 1) * QK_PAD)
            qhat, _ = _rms(_dot(cq, wuq_ref[:, cols]), QK_DIM)
            qn = qhat * (gq * SCALE_LOG2E)
            q_ref[h, :, 0:QK_NOPE] = qn[:, 0:QK_NOPE].astype(BF16)
            q_ref[h, :, QK_NOPE:QK_PAD] = _rope(qn[:, QK_NOPE:QK_PAD], cc_t, sa_t, sb_t).astype(BF16)
            kvr = _dot(ckv, wukv_ref[:, cols])
            kn = kvr[:, 0:QK_NOPE]
            rk = lax.rsqrt((jnp.sum(kn * kn, axis=-1, keepdims=True) + kpe_ss) * (1.0 / QK_DIM) + EPS)
            k_ref[h, :, 0:QK_NOPE] = (kn * rk * gk[:, 0:QK_NOPE]).astype(BF16)
            k_ref[h, :, QK_NOPE:QK_PAD] = _rope(kpe * rk * gk[:, QK_NOPE:QK_PAD], cc_t, sa_t, sb_t).astype(BF16)
            v_ref[h, :, 0:V_DIM] = kvr[:, QK_NOPE:QK_PAD].astype(BF16)
            v_ref[h, :, V_DIM:2 * V_DIM] = jnp.ones((ts, V_DIM), BF16)

    row = lambda i: (i, 0)
    fixed = lambda i: (0, 0)
    head = lambda i: (0, i, 0)
    return pl.pallas_call(
        body, name=name, grid=(s // ts,),
        in_specs=[pl.BlockSpec((ts, DOWN_PAD), row), pl.BlockSpec((1, Q_RANK), fixed), pl.BlockSpec((1, KV_RANK), fixed),
                  pl.BlockSpec((Q_RANK, N_HEADS * QK_PAD), fixed), pl.BlockSpec((KV_RANK, N_HEADS * QK_PAD), fixed),
                  pl.BlockSpec((1, QK_PAD), fixed), pl.BlockSpec((1, QK_PAD), fixed),
                  pl.BlockSpec((ts, 128), row), pl.BlockSpec((ts, 128), row), pl.BlockSpec((ts, 128), row)],
        out_specs=[pl.BlockSpec((ts, Q_RANK), row), pl.BlockSpec((ts, KV_RANK), row),
                   pl.BlockSpec((N_HEADS, ts, QK_PAD), head), pl.BlockSpec((N_HEADS, ts, QK_PAD), head),
                   pl.BlockSpec((N_HEADS, ts, 2 * V_DIM), head)],
        out_shape=[jax.ShapeDtypeStruct((s, Q_RANK), BF16), jax.ShapeDtypeStruct((s, KV_RANK), BF16),
                   jax.ShapeDtypeStruct((N_HEADS, s, QK_PAD), BF16), jax.ShapeDtypeStruct((N_HEADS, s, QK_PAD), BF16),
                   jax.ShapeDtypeStruct((N_HEADS, s, 2 * V_DIM), BF16)],
        compiler_params=_cparams("parallel"),
    )(a, g_qa, g_kva, w_uq, w_ukv, g_q, g_k, cc, sa, sb)


def _mla_prep_bwd(a, g_qa, g_kva, w_uq, w_ukv, g_q, g_k, cc, sa, sb, dq, dk, dv, *, name):
    s = a.shape[0]
    ts = _tile(s, T_PREP)

    def body(a_ref, gqa_ref, gkva_ref, wuq_ref, wukv_ref, gq_ref, gk_ref, cc_ref, sa_ref, sb_ref,
             dq_ref, dk_ref, dv_ref, dqr_ref, dkvr_ref, da_ref, dgq_ref, dgk_ref, dgqa_ref, dgkva_ref):
        @pl.when(pl.program_id(0) == 0)
        def _():
            dgq_ref[...] = jnp.zeros_like(dgq_ref)
            dgk_ref[...] = jnp.zeros_like(dgk_ref)
            dgqa_ref[...] = jnp.zeros_like(dgqa_ref)
            dgkva_ref[...] = jnp.zeros_like(dgkva_ref)

        xq, r_q = _rms(a_ref[:, 0:Q_RANK], Q_RANK)
        cq = (xq * gqa_ref[...]).astype(BF16)
        xkv, r_kv = _rms(a_ref[:, Q_RANK:Q_RANK + KV_RANK], KV_RANK)
        ckv = (xkv * gkva_ref[...]).astype(BF16)
        kpe = a_ref[:, Q_RANK + KV_RANK:DOWN_PAD]
        kpe_ss = jnp.sum(kpe * kpe, axis=-1, keepdims=True)
        cc_t, sa_t, sb_t = cc_ref[...], sa_ref[...], sb_ref[...]
        gq = gq_ref[...]
        gk = gk_ref[...]
        dcq = jnp.zeros((ts, Q_RANK), F32)
        dckv = jnp.zeros((ts, KV_RANK), F32)
        dkpe = jnp.zeros((ts, 128), F32)
        dgq = jnp.zeros((1, QK_PAD), F32)
        dgk_n = jnp.zeros((1, QK_NOPE), F32)
        dgk_p = jnp.zeros((1, 128), F32)
        for h in range(N_HEADS):
            cols = slice(h * QK_PAD, (h + 1) * QK_PAD)
            qhat, rq = _rms(_dot(cq, wuq_ref[:, cols]), QK_DIM)
            dqn = jnp.concatenate(
                [dq_ref[h, :, 0:QK_NOPE], _rope_t(dq_ref[h, :, QK_NOPE:QK_PAD], cc_t, sa_t, sb_t)], axis=1)
            dgq = dgq + jnp.sum(dqn * qhat, axis=0, keepdims=True)
            dqr = _rms_bwd(qhat, rq, dqn * gq, QK_DIM).astype(BF16)
            dqr_ref[:, cols] = dqr
            dcq = dcq + _dot_nt(dqr, wuq_ref[:, cols])
            kn = _dot(ckv, wukv_ref[:, h * QK_PAD:h * QK_PAD + QK_NOPE])
            rk = lax.rsqrt((jnp.sum(kn * kn, axis=-1, keepdims=True) + kpe_ss) * (1.0 / QK_DIM) + EPS)
            khat_n = kn * rk
            khat_p = kpe * rk
            dkn = dk_ref[h, :, 0:QK_NOPE]
            dkp = _rope_t(dk_ref[h, :, QK_NOPE:QK_PAD], cc_t, sa_t, sb_t)
            dgk_n = dgk_n + jnp.sum(dkn * khat_n, axis=0, keepdims=True)
            dgk_p = dgk_p + jnp.sum(dkp * khat_p, axis=0, keepdims=True)
            dxn = dkn * gk[:, 0:QK_NOPE]
            dxp = dkp * gk[:, QK_NOPE:QK_PAD]
            mean = (jnp.sum(dxn * khat_n, axis=-1, keepdims=True)
                    + jnp.sum(dxp * khat_p, axis=-1, keepdims=True)) * (1.0 / QK_DIM)
            dkpe = dkpe + rk * (dxp - khat_p * mean)
            dkvr = jnp.concatenate([rk * (dxn - khat_n * mean), dv_ref[h, :, :]], axis=1).astype(BF16)
            dkvr_ref[:, cols] = dkvr
            dckv = dckv + _dot_nt(dkvr, wukv_ref[:, cols])
        dgq_ref[...] += dgq
        dgk_ref[:, 0:QK_NOPE] += dgk_n
        dgk_ref[:, QK_NOPE:QK_PAD] += dgk_p
        dgqa_ref[...] += jnp.sum(dcq * xq, axis=0, keepdims=True)
        dgkva_ref[...] += jnp.sum(dckv * xkv, axis=0, keepdims=True)
        da_ref[:, 0:Q_RANK] = _rms_bwd(xq, r_q, dcq * gqa_ref[...], Q_RANK).astype(BF16)
        da_ref[:, Q_RANK:Q_RANK + KV_RANK] = _rms_bwd(xkv, r_kv, dckv * gkva_ref[...], KV_RANK).astype(BF16)
        da_ref[:, Q_RANK + KV_RANK:DOWN_PAD] = dkpe.astype(BF16)

    row = lambda i: (i, 0)
    fixed = lambda i: (0, 0)
    head = lambda i: (0, i, 0)
    wide = N_HEADS * QK_PAD
    return pl.pallas_call(
        body, name=name, grid=(s // ts,),
        in_specs=[pl.BlockSpec((ts, DOWN_PAD), row), pl.BlockSpec((1, Q_RANK), fixed), pl.BlockSpec((1, KV_RANK), fixed),
                  pl.BlockSpec((Q_RANK, wide), fixed), pl.BlockSpec((KV_RANK, wide), fixed),
                  pl.BlockSpec((1, QK_PAD), fixed), pl.BlockSpec((1, QK_PAD), fixed),
                  pl.BlockSpec((ts, 128), row), pl.BlockSpec((ts, 128), row), pl.BlockSpec((ts, 128), row),
                  pl.BlockSpec((N_HEADS, ts, QK_PAD), head), pl.BlockSpec((N_HEADS, ts, QK_PAD), head),
                  pl.BlockSpec((N_HEADS, ts, V_DIM), head)],
        out_specs=[pl.BlockSpec((ts, wide), row), pl.BlockSpec((ts, wide), row), pl.BlockSpec((ts, DOWN_PAD), row),
                   pl.BlockSpec((1, QK_PAD), fixed), pl.BlockSpec((1, QK_PAD), fixed),
                   pl.BlockSpec((1, Q_RANK), fixed), pl.BlockSpec((1, KV_RANK), fixed)],
        out_shape=[jax.ShapeDtypeStruct((s, wide), BF16), jax.ShapeDtypeStruct((s, wide), BF16),
                   jax.ShapeDtypeStruct((s, DOWN_PAD), BF16),
                   jax.ShapeDtypeStruct((1, QK_PAD), F32), jax.ShapeDtypeStruct((1, QK_PAD), F32),
                   jax.ShapeDtypeStruct((1, Q_RANK), F32), jax.ShapeDtypeStruct((1, KV_RANK), F32)],
        compiler_params=_cparams("arbitrary"),
    )(a, g_qa, g_kva, w_uq, w_ukv, g_q, g_k, cc, sa, sb, dq, dk, dv)


def _flash_fwd(q, k, v, pos_col, pos_row, *, name, gather=None):
    nh, s, _ = q.shape
    tq = _tile(s, FWD_TQ)
    tk = _tile(s, FWD_TK)
    sq = tq // ATTN_CHAINS
    nq = s // tq
    names = list(gather or {})
    ng = len(names)

    def body(q_ref, k_ref, v_ref, pq_ref, pk_ref, *rest):
        o_ref, lse_ref = rest[ng:ng + 2]
        m_sc, acc_sc = rest[2 * ng + 2:2 * ng + 4]
        qb = pl.program_id(1)
        if ng:
            sends, recvs = _gather_ici_copies(rest[ng + 2:2 * ng + 2], names, *rest[2 * ng + 4:], base=0, stride=3)

            @pl.when((pl.program_id(0) == 0) & (qb == 0))
            def _():
                for cp in sends:
                    cp.start()

        m_sc[...] = jnp.full_like(m_sc, NEG)
        acc_sc[...] = jnp.zeros_like(acc_sc)

        def step(kb, masked):
            keys = pl.ds(pl.multiple_of(kb * tk, tk), tk)
            kt = k_ref[0, keys, :]
            vt = v_ref[0, keys, :]
            scores = [_dot_nt(q_ref[0, u * sq:(u + 1) * sq, :], kt) for u in range(ATTN_CHAINS)]
            for u in range(ATTN_CHAINS):
                rows = slice(u * sq, (u + 1) * sq)
                sc = scores[u]
                if masked:
                    sc = jnp.where(pq_ref[rows, :] >= pk_ref[:, keys], sc, NEG)
                m_prev = m_sc[rows, :]
                m_new = jnp.maximum(m_prev, jnp.max(sc, axis=-1, keepdims=True))
                alpha = jnp.exp2(m_prev - m_new)
                p = jnp.exp2(sc - jnp.tile(m_new, (1, tk // 128)))
                acc_sc[rows, :] = jnp.tile(alpha, (1, 2)) * acc_sc[rows, :] + _dot(p.astype(BF16), vt)
                m_sc[rows, :] = m_new

        n_before = (qb * tq) // tk
        n_seen = (qb * tq + tq - 1) // tk + 1
        lax.fori_loop(0, n_before, lambda kb, c: (step(kb, False), c)[1], 0)
        lax.fori_loop(n_before, n_seen, lambda kb, c: (step(kb, True), c)[1], 0)
        l = acc_sc[:, V_DIM:2 * V_DIM]
        o_ref[...] = (acc_sc[:, 0:V_DIM] / l).astype(BF16)
        lse = m_sc[...] * (1.0 / LOG2E) + jnp.log(l)
        lse_ref[0] = lse.T[0:1, :]

        if ng:
            @pl.when((pl.program_id(0) == nh - 1) & (qb == nq - 1))
            def _():
                for cp in recvs:
                    cp.wait_recv()
                for cp in sends:
                    cp.wait_send()

    arrays = [gather[nm] for nm in names]
    out = pl.pallas_call(
        body, name=name, grid=(nh, nq),
        in_specs=[pl.BlockSpec((1, tq, QK_PAD), lambda h, qb: (h, qb, 0)),
                  pl.BlockSpec((1, s, QK_PAD), lambda h, qb: (h, 0, 0)),
                  pl.BlockSpec((1, s, 2 * V_DIM), lambda h, qb: (h, 0, 0)),
                  pl.BlockSpec((tq, 1), lambda h, qb: (qb, 0)),
                  pl.BlockSpec((1, s), lambda h, qb: (0, 0))] + [ANY] * ng,
        out_specs=[pl.BlockSpec((tq, V_DIM), lambda h, qb: (qb, h)),
                   pl.BlockSpec((1, 1, tq), lambda h, qb: (h, 0, qb))] + [ANY] * ng,
        scratch_shapes=[pltpu.VMEM((tq, 128), F32), pltpu.VMEM((tq, 2 * V_DIM), F32)]
        + ([pltpu.SemaphoreType.DMA((3 * ng,)), pltpu.SemaphoreType.DMA((3 * ng,))] if ng else []),
        out_shape=[jax.ShapeDtypeStruct((s, nh * V_DIM), BF16), jax.ShapeDtypeStruct((nh, 1, s), F32)]
        + [jax.ShapeDtypeStruct(a.shape, a.dtype) for a in arrays],
        input_output_aliases={5 + i: 2 + i for i in range(ng)},
        compiler_params=_cparams("arbitrary", "arbitrary") if ng else _cparams("parallel", "parallel"),
    )(q, k, v, pos_col, pos_row, *arrays)
    return out[0], out[1], dict(zip(names, out[2:]))


def _attn_out_bwd(dy, w_o, o, *, name, layer=0):
    s, d = dy.shape
    n = w_o.shape[-2]
    tm = _tile(s, TM)

    def body(dy_ref, w_ref, o_ref, do_ref, d_ref):
        do = _dot_nt(dy_ref[...].astype(BF16), w_ref[...]).astype(BF16)
        do_ref[...] = do
        for h in range(N_HEADS):
            cols = slice(h * V_DIM, (h + 1) * V_DIM)
            prod = do[:, cols].astype(F32) * o_ref[:, cols].astype(F32)
            d_ref[h] = jnp.sum(prod.T, axis=0, keepdims=True)

    return pl.pallas_call(
        body, name=name, grid=(s // tm,),
        in_specs=[pl.BlockSpec((tm, d), lambda i: (i, 0)), _wspec(w_o, layer), pl.BlockSpec((tm, n), lambda i: (i, 0))],
        out_specs=[pl.BlockSpec((tm, n), lambda i: (i, 0)), pl.BlockSpec((N_HEADS, 1, tm), lambda i: (0, 0, i))],
        out_shape=[jax.ShapeDtypeStruct((s, n), BF16), jax.ShapeDtypeStruct((N_HEADS, 1, s), F32)],
        compiler_params=_cparams("parallel"),
    )(dy, w_o, o)


def _flash_bwd(q, k, v, do, lse_row, delta_row, pos_col, pos_row, *, name, scatter=None):
    nh, s, _ = q.shape
    tq = _tile(s, BWD_TQ)
    tk = _tile(s, BWD_TK)
    nq, nk = s // tq, s // tk
    sk = tk // ATTN_CHAINS
    names = list(scatter or {})
    ng = len(names)

    def body(q_ref, k_ref, v_ref, do_ref, lse_ref, delta_ref, pq_ref, pk_ref, *rest):
        dq_ref, dk_ref, dv_ref = rest[ng:ng + 3]
        dk_sc, dv_sc = rest[2 * ng + 3:2 * ng + 5]
        kb = pl.program_id(1)
        if ng:
            copies = _scatter_copies(rest[:ng], rest[ng + 3:2 * ng + 3], names, *rest[2 * ng + 5:])

            @pl.when((pl.program_id(0) == 0) & (kb == 0))
            def _():
                for cp in copies:
                    cp.start()

        @pl.when(kb == 0)
        def _():
            dq_ref[...] = jnp.zeros_like(dq_ref)

        dk_sc[...] = jnp.zeros_like(dk_sc)
        dv_sc[...] = jnp.zeros_like(dv_sc)

        def step(qb, masked):
            trim = masked and tq == tk
            start = pl.multiple_of(qb * tq, tq)
            offs = [u * sk if trim else 0 for u in range(ATTN_CHAINS)]
            qss = [pl.ds(start + offs[u], tq - offs[u]) for u in range(ATTN_CHAINS)]
            qts = [q_ref[0, qss[u], :] for u in range(ATTN_CHAINS)]
            dots = [do_ref[qss[u], :] for u in range(ATTN_CHAINS)]
            sts = [_dot_nt(k_ref[0, u * sk:(u + 1) * sk, :], qts[u]) for u in range(ATTN_CHAINS)]
            dpts = [_dot_nt(v_ref[0, u * sk:(u + 1) * sk, :], dots[u]) for u in range(ATTN_CHAINS)]
            parts = []
            for u in range(ATTN_CHAINS):
                rows = slice(u * sk, (u + 1) * sk)
                pt = jnp.exp2(sts[u] - lse_ref[0, :, qss[u]] * LOG2E)
                if masked:
                    pt = jnp.where(pq_ref[:, qss[u]] >= pk_ref[rows, :], pt, 0.0)
                dv_sc[rows, :] += _dot(pt.astype(BF16), dots[u])
                dst = (pt * (dpts[u] - delta_ref[0, :, qss[u]])).astype(BF16)
                dk_sc[rows, :] += _dot(dst, qts[u])
                parts.append(_dot_tn(dst, k_ref[0, rows, :]))
            if trim:
                for u in range(ATTN_CHAINS):
                    dq_ref[0, qss[u], :] += parts[u]
            else:
                dq_ref[0, qss[0], :] += functools.reduce(lambda a, b: a + b, parts)

        q_first = (kb * tk) // tq
        q_clear = (kb * tk + tk - 1) // tq + 1
        lax.fori_loop(q_first, q_clear, lambda qb, c: (step(qb, True), c)[1], 0)
        lax.fori_loop(q_clear, nq, lambda qb, c: (step(qb, False), c)[1], 0)
        dk_ref[0] = dk_sc[...] * (1.0 / LOG2E)
        dv_ref[0] = dv_sc[...]

        @pl.when(kb == nk - 1)
        def _():
            dq_ref[...] = dq_ref[...] * SCALE

        if ng:
            @pl.when((pl.program_id(0) == nh - 1) & (kb == nk - 1))
            def _():
                for cp in copies:
                    cp.wait()

    arrays = [scatter[nm] for nm in names]
    out = pl.pallas_call(
        body, name=name, grid=(nh, nk),
        in_specs=[pl.BlockSpec((1, s, QK_PAD), lambda h, kb: (h, 0, 0)),
                  pl.BlockSpec((1, tk, QK_PAD), lambda h, kb: (h, kb, 0)),
                  pl.BlockSpec((1, tk, V_DIM), lambda h, kb: (h, kb, 0)),
                  pl.BlockSpec((s, V_DIM), lambda h, kb: (0, h)),
                  pl.BlockSpec((1, 1, s), lambda h, kb: (h, 0, 0)),
                  pl.BlockSpec((1, 1, s), lambda h, kb: (h, 0, 0)),
                  pl.BlockSpec((1, s), lambda h, kb: (0, 0)),
                  pl.BlockSpec((tk, 1), lambda h, kb: (kb, 0))] + [ANY] * ng,
        out_specs=[pl.BlockSpec((1, s, QK_PAD), lambda h, kb: (h, 0, 0)),
                   pl.BlockSpec((1, tk, QK_PAD), lambda h, kb: (h, kb, 0)),
                   pl.BlockSpec((1, tk, V_DIM), lambda h, kb: (h, kb, 0))] + [ANY] * ng,
        scratch_shapes=[pltpu.VMEM((tk, QK_PAD), F32), pltpu.VMEM((tk, V_DIM), F32)]
        + ([pltpu.SemaphoreType.DMA((3 * ng,)), pltpu.SemaphoreType.DMA((3 * ng,))] if ng else []),
        out_shape=[jax.ShapeDtypeStruct((nh, s, QK_PAD), F32), jax.ShapeDtypeStruct((nh, s, QK_PAD), F32),
                   jax.ShapeDtypeStruct((nh, s, V_DIM), F32)] + _scatter_out_shapes(names, arrays),
        compiler_params=_cparams("arbitrary", "arbitrary"),
    )(q, k, v, do, lse_row, delta_row, pos_row, pos_col, *arrays)
    return out[0], out[1], out[2], dict(zip(names, out[3:]))


def _loss_head(y, target, *, name):
    s, d = y.shape
    tm = _tile(s, TM)
    nt = s // tm

    def body(y_ref, t_ref, dy_ref, loss_ref, acc):
        i = pl.program_id(0)

        @pl.when(i == 0)
        def _():
            acc[...] = jnp.zeros_like(acc)

        e = y_ref[...] - t_ref[...]
        dy_ref[...] = e * (1.0 / d)
        acc[...] += jnp.sum((e * e).reshape(tm // 8, 8, d), axis=0)

        @pl.when(i == nt - 1)
        def _():
            loss_ref[...] = jnp.full((1, 128), 0.5 / d, F32) * jnp.sum(acc[...])

    return pl.pallas_call(
        body, name=name, grid=(nt,),
        in_specs=[pl.BlockSpec((tm, d), lambda i: (i, 0))] * 2,
        out_specs=[pl.BlockSpec((tm, d), lambda i: (i, 0)), pl.BlockSpec((1, 128), lambda i: (0, 0))],
        out_shape=[jax.ShapeDtypeStruct((s, d), F32), jax.ShapeDtypeStruct((1, 128), F32)],
        scratch_shapes=[pltpu.VMEM((8, d), F32)],
        compiler_params=_cparams("arbitrary"),
    )(y, target)


def _adamw(w, g, m, v, *, name):
    r, c = w.shape
    tr = _tile(r, 512) if r % 8 == 0 else r

    def body(w_ref, g_ref, m_ref, v_ref, d_ref, nm_ref, nv_ref, go_ref):
        g_t = g_ref[...]
        go_ref[...] = g_t
        nm = ADAM_B1 * m_ref[...] + (1.0 - ADAM_B1) * g_t
        nv = ADAM_B2 * v_ref[...] + (1.0 - ADAM_B2) * (g_t * g_t)
        m_hat = nm / (1.0 - ADAM_B1 ** ADAM_STEP)
        v_hat = nv / (1.0 - ADAM_B2 ** ADAM_STEP)
        d_ref[...] = -ADAM_LR * (m_hat / (jnp.sqrt(v_hat) + ADAM_EPS) + ADAM_WD * w_ref[...])
        nm_ref[...] = nm
        nv_ref[...] = nv

    spec = pl.BlockSpec((tr, c), lambda i: (i, 0))
    return pl.pallas_call(
        body, name=name, grid=(r // tr,), in_specs=[spec] * 4, out_specs=[spec] * 4,
        out_shape=[jax.ShapeDtypeStruct((r, c), F32)] * 4,
        compiler_params=_cparams("parallel"),
    )(w, g, m, v)


def _place():
    return lax.axis_index("x"), lax.axis_index("y"), lax.axis_index("c")


def _other_chips(x, y):
    return [(1 - x, y), (x, 1 - y), (1 - x, 1 - y)]


BIG = {
    "attn_w_down": ((2, 1024, 448), 1), "attn_w_uq": ((2, 256, 1536), 2), "attn_w_ukv": ((2, 128, 2048), 2),
    "attn_w_o": ((2, 1024, 1024), 1), "conv_w_in": ((2, 1024, 3072), 2),
    "conv_w_out": ((2, 1024, 1024), 1), "mlp_w1": ((4, 1024, 4096), 2), "mlp_w2": ((4, 4096, 1024), 1),
}
CONV_W = (2, 3, 1024)


def _shard_shape(name):
    shape, axis = BIG[name]
    return tuple(n // N_CHIPS if i == axis else n for i, n in enumerate(shape))


def _band(ref, name, layers, chip):
    shape, axis = BIG[name]
    width = shape[axis] // N_CHIPS
    if axis == 1:
        return ref.at[layers, pl.ds(chip * width, width), :]
    return ref.at[layers, :, pl.ds(chip * width, width)]


def _half(name, c):
    hl = BIG[name][0][0] // 2
    return pl.ds(c * hl, hl)


def _place_own(w, nm, chip, *, name):
    shape, axis = BIG[nm]
    layers, rows, cols = w.shape
    tr = _sum_rows(rows, cols)
    nrb = rows // tr
    if axis == 1:
        band = lambda l, i, ch: (l, ch[0] * nrb + i, 0)
    else:
        band = lambda l, i, ch: (l, i, ch[0])

    def body(chip_ref, w_ref, o_ref):
        o_ref[...] = w_ref[...].astype(BF16)

    return pl.pallas_call(
        body, name=name,
        grid_spec=pltpu.PrefetchScalarGridSpec(
            num_scalar_prefetch=1, grid=(layers, nrb),
            in_specs=[pl.BlockSpec((1, tr, cols), lambda l, i, ch: (l, i, 0))],
            out_specs=pl.BlockSpec((1, tr, cols), band)),
        out_shape=jax.ShapeDtypeStruct(shape, BF16),
        compiler_params=_cparams("parallel", "parallel"),
    )(chip, w)


def _gather_copies(outs, names, send_sems, recv_sems, *, base, stride, to_sibling):
    x, y, c = _place()
    me = 2 * x + y

    def copy(k, ref, nm, layers, chip, to):
        band = _band(ref, nm, layers, chip)
        return pltpu.make_async_remote_copy(
            src_ref=band, dst_ref=band, send_sem=send_sems.at[k], recv_sem=recv_sems.at[k],
            device_id=to, device_id_type=MESH)

    sends, recvs = [], []
    for i, nm in enumerate(names):
        for j, (cx, cy) in enumerate(_other_chips(x, y)):
            k = base + stride * i + j
            if to_sibling:
                sends.append(copy(k, outs[i], nm, _half(nm, c), 2 * cx + cy, (x, y, 1 - c)))
                recvs.append(copy(k, outs[i], nm, _half(nm, 1 - c), 2 * cx + cy, (x, y, c)))
            else:
                sends.append(copy(k, outs[i], nm, _half(nm, c), me, (cx, cy, c)))
                recvs.append(copy(k, outs[i], nm, _half(nm, c), 2 * cx + cy, (x, y, c)))
    return sends, recvs


def _gather_ici_copies(outs, names, send_sems, recv_sems, *, base, stride):
    return _gather_copies(outs, names, send_sems, recv_sems, base=base, stride=stride, to_sibling=False)


def _gather_weights(fulls, *, name, ici=True):
    names = list(fulls)
    n = len(names)

    def body(*refs):
        outs = refs[n:2 * n]
        sems = refs[2 * n:]
        sent = []
        if ici:
            sends, recvs = _gather_copies(outs, names, *sems, base=0, stride=6, to_sibling=False)
            for cp in sends:
                cp.start()
            for cp in recvs:
                cp.wait_recv()
            sent += sends
        sends, recvs = _gather_copies(outs, names, *sems, base=3, stride=6, to_sibling=True)
        for cp in sends:
            cp.start()
        for cp in recvs:
            cp.wait_recv()
        for cp in sent + sends:
            cp.wait_send()

    arrays = [fulls[nm] for nm in names]
    out = pl.pallas_call(
        body, name=name, in_specs=[ANY] * n, out_specs=[ANY] * n,
        out_shape=[jax.ShapeDtypeStruct(a.shape, a.dtype) for a in arrays],
        input_output_aliases={i: i for i in range(n)},
        scratch_shapes=[pltpu.SemaphoreType.DMA((6 * n,)), pltpu.SemaphoreType.DMA((6 * n,))],
    )(*arrays)
    return dict(zip(names, out))


def _swap_halves(grads, *, name):
    names = list(grads)
    n = len(names)

    def body(*refs):
        copies = _swap_copies(refs[:n], refs[n:2 * n], names, *refs[2 * n:])
        for cp in copies:
            cp.start()
        for cp in copies:
            cp.wait()

    arrays = [grads[nm] for nm in names]
    out = pl.pallas_call(
        body, name=name, in_specs=[ANY] * n, out_specs=[ANY] * n,
        out_shape=_swap_out_shapes(arrays), scratch_shapes=_swap_sems(n),
    )(*arrays)
    return dict(zip(names, out))


def _swap_copies(ins, outs, names, send_sems, recv_sems):
    x, y, c = _place()
    return [pltpu.make_async_remote_copy(
        src_ref=ins[i].at[_half(nm, 1 - c)], dst_ref=outs[i], send_sem=send_sems.at[i], recv_sem=recv_sems.at[i],
        device_id=(x, y, 1 - c), device_id_type=MESH) for i, nm in enumerate(names)]


def _swap_out_shapes(arrays):
    return [jax.ShapeDtypeStruct((a.shape[0] // 2,) + a.shape[1:], a.dtype) for a in arrays]


def _swap_sems(n):
    return [pltpu.SemaphoreType.DMA((n,)), pltpu.SemaphoreType.DMA((n,))] if n else []


def _all_steps(grid, at):
    cond = None
    for axis, size in enumerate(grid):
        this = pl.program_id(axis) == (0 if at == "first" else size - 1)
        cond = this if cond is None else cond & this
    return cond


def _start_at_first_step(copies, grid):
    @pl.when(_all_steps(grid, "first"))
    def _():
        for cp in copies:
            cp.start()


def _wait_at_last_step(copies, grid):
    @pl.when(_all_steps(grid, "last"))
    def _():
        for cp in copies:
            cp.wait()


def _sum_rows(rows, cols):
    t = rows
    while t * cols * 4 > SUM_BLOCK_BYTES and t % 16 == 0:
        t //= 2
    return t


def _chip_sum(g, r1, core, *, name):
    layers, rows, cols = g.shape
    hl = layers // 2
    tr = _sum_rows(rows, cols)

    def body(core_ref, g_ref, r_ref, o_ref):
        o_ref[...] = (g_ref[...] + r_ref[...]).astype(BF16)

    return pl.pallas_call(
        body, name=name,
        grid_spec=pltpu.PrefetchScalarGridSpec(
            num_scalar_prefetch=1, grid=(hl, rows // tr),
            in_specs=[pl.BlockSpec((1, tr, cols), lambda l, i, cr: (cr[0] * hl + l, i, 0)),
                      pl.BlockSpec((1, tr, cols), lambda l, i, cr: (l, i, 0))],
            out_specs=pl.BlockSpec((1, tr, cols), lambda l, i, cr: (l, i, 0))),
        out_shape=jax.ShapeDtypeStruct((hl, rows, cols), BF16),
        compiler_params=_cparams("parallel", "parallel"),
    )(core, g, r1)


def _chip_partials(grads, names, *, tag):
    core = lax.axis_index("c").astype(jnp.int32).reshape(1)
    r1 = _swap_halves({n: grads[n] for n in names}, name=f"grad_swap_halves_{tag}")
    return r1, {n: _chip_sum(grads[n], r1[n], core, name=f"grad_chip_sum_{n}") for n in names}


def _scatter_partials(partials):
    names = list(partials)
    n = len(names)

    def body(*refs):
        copies = _scatter_copies(refs[:n], refs[n:2 * n], names, *refs[2 * n:])
        for cp in copies:
            cp.start()
        for cp in copies:
            cp.wait()

    arrays = [partials[nm] for nm in names]
    out = pl.pallas_call(
        body, name="grad_scatter_partials", in_specs=[ANY] * n, out_specs=[ANY] * n,
        out_shape=_scatter_out_shapes(names, arrays),
        scratch_shapes=[pltpu.SemaphoreType.DMA((3 * n,)), pltpu.SemaphoreType.DMA((3 * n,))],
    )(*arrays)
    return dict(zip(names, out))


def _scatter_copies(ins, outs, names, send_sems, recv_sems):
    x, y, c = _place()
    copies = []
    for i, nm in enumerate(names):
        for j, (cx, cy) in enumerate(_other_chips(x, y)):
            copies.append(pltpu.make_async_remote_copy(
                src_ref=_band(ins[i], nm, slice(None), 2 * cx + cy), dst_ref=outs[i].at[j],
                send_sem=send_sems.at[3 * i + j], recv_sem=recv_sems.at[3 * i + j],
                device_id=(cx, cy, c), device_id_type=MESH))
    return copies


def _scatter_out_shapes(names, arrays):
    return [jax.ShapeDtypeStruct((3, a.shape[0]) + _shard_shape(nm)[1:], a.dtype) for nm, a in zip(names, arrays)]


def _final_sum(g, r1, r2, place, nm, *, name):
    (layers, _, _), axis = BIG[nm]
    hl = layers // 2
    _, rows, cols = _shard_shape(nm)
    tr = _sum_rows(rows, cols)
    nrb = rows // tr
    if axis == 1:
        blk = lambda l, i, pc: (l, pc[1] * nrb + i, 0)
    else:
        blk = lambda l, i, pc: (l, i, pc[1])

    def body(place_ref, g_ref, r1_ref, r2_ref, o_ref):
        acc = g_ref[...] + r1_ref[...]
        for j in range(3):
            acc = acc + r2_ref[j].astype(F32)
        o_ref[...] = acc

    return pl.pallas_call(
        body, name=name,
        grid_spec=pltpu.PrefetchScalarGridSpec(
            num_scalar_prefetch=1, grid=(hl, nrb),
            in_specs=[pl.BlockSpec((1, tr, cols), lambda l, i, pc: blk(pc[0] * hl + l, i, pc)),
                      pl.BlockSpec((1, tr, cols), lambda l, i, pc: blk(l, i, pc)),
                      pl.BlockSpec((3, 1, tr, cols), lambda l, i, pc: (0, l, i, 0))],
            out_specs=pl.BlockSpec((1, tr, cols), lambda l, i, pc: (pc[0] * hl + l, i, 0))),
        out_shape=jax.ShapeDtypeStruct((layers, rows, cols), F32),
        compiler_params=_cparams("parallel", "parallel"),
    )(place, g, r1, r2)


def _join_halves(shards):
    names = list(shards)
    n = len(names)

    def body(*refs):
        outs = refs[n:2 * n]
        send_sems, recv_sems = refs[2 * n:]
        x, y, c = _place()
        copies = []
        for i, nm in enumerate(names):
            mine = outs[i].at[_half(nm, c)]
            cp = pltpu.make_async_remote_copy(
                src_ref=mine, dst_ref=mine, send_sem=send_sems.at[i], recv_sem=recv_sems.at[i],
                device_id=(x, y, 1 - c), device_id_type=MESH)
            cp.start()
            copies.append(cp)
        for i, nm in enumerate(names):
            theirs = outs[i].at[_half(nm, 1 - c)]
            pltpu.make_async_remote_copy(
                src_ref=theirs, dst_ref=theirs, send_sem=send_sems.at[i], recv_sem=recv_sems.at[i],
                device_id=(x, y, 1 - c), device_id_type=MESH).wait_recv()
        for cp in copies:
            cp.wait_send()

    arrays = [shards[nm] for nm in names]
    out = pl.pallas_call(
        body, name="grad_join_halves", in_specs=[ANY] * n, out_specs=[ANY] * n,
        out_shape=[jax.ShapeDtypeStruct(a.shape, a.dtype) for a in arrays],
        input_output_aliases={i: i for i in range(n)},
        scratch_shapes=[pltpu.SemaphoreType.DMA((n,)), pltpu.SemaphoreType.DMA((n,))],
    )(*arrays)
    return dict(zip(names, out))


def _all_reduce_small(part, *, name):
    rows, cols = part.shape
    vm = pl.BlockSpec(memory_space=pltpu.VMEM)

    def body(p_ref, o_ref, land, send_sems, recv_sems):
        x, y, c = _place()
        me = 4 * x + 2 * y + c
        flips = [(dx, dy, dc) for dx in (0, 1) for dy in (0, 1) for dc in (0, 1)][1:]
        copies = []
        for k, (dx, dy, dc) in enumerate(flips):
            cp = pltpu.make_async_remote_copy(
                src_ref=p_ref, dst_ref=land.at[me], send_sem=send_sems.at[k], recv_sem=recv_sems.at[k],
                device_id=(1 - x if dx else x, 1 - y if dy else y, 1 - c if dc else c), device_id_type=MESH)
            cp.start()
            copies.append(cp)
        land[me] = p_ref[...]
        for cp in copies:
            cp.wait()
        acc = land[0]
        for j in range(1, 8):
            acc = acc + land[j]
        o_ref[...] = acc

    return pl.pallas_call(
        body, name=name, in_specs=[vm], out_specs=vm,
        out_shape=jax.ShapeDtypeStruct((rows, cols), F32),
        scratch_shapes=[pltpu.VMEM((8, rows, cols), F32), pltpu.SemaphoreType.DMA((7,)), pltpu.SemaphoreType.DMA((7,))],
    )(part)


SMALL = {"g_mix": (4, 1024), "g_mlp": (4, 1024), "attn_g_q_a": (2, 256), "attn_g_kv_a": (2, 128),
         "attn_g_qnorm": (2, 192), "attn_g_knorm": (2, 192)}
SMALL_GRADS = {**SMALL, "conv_w": CONV_W}
WEIGHT_ORDER = ["g_mix", "g_mlp", "attn_w_down", "attn_g_q_a", "attn_g_kv_a", "attn_w_uq", "attn_w_ukv",
                "attn_g_qnorm", "attn_g_knorm", "attn_w_o", "conv_w_in", "conv_w", "conv_w_out", "mlp_w1", "mlp_w2"]


def _prod(shape):
    n = 1
    for v in shape:
        n *= v
    return n


def _pack_small(parts, table):
    flat = [parts[n].reshape(-1) for n in table]
    size = sum(_prod(s) for s in table.values())
    rows = -(-size // (8 * 128)) * 8
    flat.append(jnp.zeros((rows * 128 - size,), F32))
    return jnp.concatenate(flat).reshape(rows, 128)


def _unpack_small(buf, table):
    flat = buf.reshape(-1)
    out, off = {}, 0
    for n, shp in table.items():
        out[n] = flat[off:off + _prod(shp)].reshape(shp)
        off += _prod(shp)
    return out


def _rope_tables(positions):
    inv_freq = ROPE_THETA ** (-jnp.arange(0, QK_ROPE, 2, dtype=F32) / QK_ROPE)
    ang = positions.astype(F32)[:, None] * inv_freq
    cos, sin = jnp.cos(ang), jnp.sin(ang)
    z32 = jnp.zeros_like(cos)
    z64 = jnp.zeros((positions.shape[0], 64), F32)
    cc = jnp.concatenate([cos, cos, z64], axis=1)
    sa = jnp.concatenate([-sin, z32, z64], axis=1)
    sb = jnp.concatenate([z32, sin, z64], axis=1)
    return cc, sa, sb


def _pad_heads(w, width):
    k = w.shape[0]
    w = w.reshape(k, N_HEADS, width)
    return jnp.pad(w, ((0, 0), (0, 0), (0, QK_PAD - width))).reshape(k, N_HEADS * QK_PAD)


EARLY = ("mlp_w1", "mlp_w2", "conv_w_in", "conv_w_out")
LATE = ("attn_w_down", "attn_w_uq", "attn_w_ukv", "attn_w_o")
GATHER_LATER = EARLY


def _local_step(x, positions, target, wb, gains, later=None):
    s = x.shape[0]
    cc, sa, sb = _rope_tables(positions)
    pos_col = positions.reshape(s, 1)
    pos_row = positions.reshape(1, s)

    saved = []
    for i in range(4):
        j = i // 2
        g_mix = gains["g_mix"][i:i + 1]
        g_mlp = gains["g_mlp"][i:i + 1]
        if i % 2 == 0:
            w_down = jnp.pad(wb["attn_w_down"][j], ((0, 0), (0, DOWN_PAD - DOWN_DIM)))
            w_uq = _pad_heads(wb["attn_w_uq"][j], QK_DIM)
            w_ukv = wb["attn_w_ukv"][j]
            g_qa = gains["attn_g_q_a"][j:j + 1]
            g_kva = gains["attn_g_kv_a"][j:j + 1]
            g_q = jnp.pad(gains["attn_g_qnorm"][j:j + 1], ((0, 0), (0, QK_PAD - QK_DIM)))
            g_k = jnp.pad(gains["attn_g_knorm"][j:j + 1], ((0, 0), (0, QK_PAD - QK_DIM)))
            h, a = _norm_mm(x, g_mix, w_down, out_dtype=F32, name=f"mla_down_{j}")
            cq, ckv, q, k, v = _mla_prep(a, g_qa, g_kva, w_uq, w_ukv, g_q, g_k, cc, sa, sb, name=f"mla_prep_{j}")
            o, lse, got = _flash_fwd(q, k, v, pos_col, pos_row, name=f"flash_fwd_{j}",
                                     gather=later if i == 0 else None)
            if got:
                wb = {**wb, **_gather_weights(got, name="gather_later_forward", ici=False)}
            x_mid = _mm_nn(o, wb["attn_w_o"], layer=j, out_dtype=F32, residual=x, name=f"mla_out_{j}")
            mix = dict(h=h, a=a, cq=cq, ckv=ckv, q=q, k=k, v=v, o=o, lse=lse, w_down=w_down, w_uq=w_uq, w_ukv=w_ukv,
                       g_qa=g_qa, g_kva=g_kva, g_q=g_q, g_k=g_k)
        else:
            h, bcu = _norm_mm(x, g_mix, wb["conv_w_in"], layer=j, out_dtype=BF16, name=f"conv_in_{j}")
            z = _conv_gate(bcu, gains["conv_w"][j], name=f"conv_gate_{j}")
            x_mid = _mm_nn(z, wb["conv_w_out"], layer=j, out_dtype=F32, residual=x, name=f"conv_out_{j}")
            mix = dict(h=h, bcu=bcu, z=z)
        h2, u, x_out = _mlp_fwd(x_mid, g_mlp, wb["mlp_w1"], wb["mlp_w2"], layer=i, name=f"mlp_fwd_{i}")
        saved.append(dict(x_in=x, x_mid=x_mid, mix=mix, h2=h2, u=u, g_mix=g_mix, g_mlp=g_mlp))
        x = x_out

    dx, loss = _loss_head(x, target, name="loss_head")

    gw = {n: None for n in BIG}
    exchanged = None
    g_uq = [None, None]
    gs = {n: [None] * SMALL_GRADS[n][0] for n in SMALL_GRADS}

    def wgrad(nm, layer, a, b, **kw):
        out = _mm_tn(a, b, stack=gw[nm], layer=layer, layers=BIG[nm][0][0], name=f"{nm}_grad_{layer}", **kw)
        gw[nm], arrived = out if kw.get("swap") else (out, None)
        return arrived

    for i in reversed(range(4)):
        j = i // 2
        sv = saved[i]
        mix = sv["mix"]
        ride = i == 0 and later is not None
        du = _mlp_down_bwd(dx, wb["mlp_w2"], sv["u"], layer=i, name=f"mlp_down_bwd_{i}")
        wgrad("mlp_w2", i, sv["u"], dx, sqrelu_a=True)
        r1_early = wgrad("mlp_w1", i, sv["h2"], du,
                         swap={n: gw[n] for n in ("mlp_w2", "conv_w_in", "conv_w_out")} if ride else None)
        dx, dg, *arrived = _nt_rms_bwd(du, wb["mlp_w1"], sv["x_mid"], sv["g_mlp"], dx, layer=i, name=f"mlp_up_bwd_{i}",
                                       swap={"mlp_w1": gw["mlp_w1"]} if ride else None)
        if ride:
            r1_early.update(arrived[0])
        gs["g_mlp"][i] = dg[0]
        if i % 2 == 0:
            do, delta_row = _attn_out_bwd(dx, wb["attn_w_o"], mix["o"], layer=j, name=f"mla_out_bwd_{j}")
            wgrad("attn_w_o", j, mix["o"], dx)
            lse_row = mix["lse"]
            partials = None
            if ride:
                core = lax.axis_index("c").astype(jnp.int32).reshape(1)
                partials = {n: _chip_sum(gw[n], r1_early[n], core, name=f"grad_chip_sum_{n}") for n in EARLY}
            dq, dk, dv, arrived = _flash_bwd(mix["q"], mix["k"], mix["v"], do, lse_row, delta_row, pos_col, pos_row,
                                             name=f"flash_bwd_{j}", scatter=partials)
            if partials is not None:
                exchanged = (r1_early, arrived)
            dqr, dkvr, da, dgq, dgk, dgqa, dgkva = _mla_prep_bwd(
                mix["a"], mix["g_qa"], mix["g_kva"], mix["w_uq"], mix["w_ukv"], mix["g_q"], mix["g_k"], cc, sa, sb,
                dq, dk, dv, name=f"mla_prep_bwd_{j}")
            g_uq[j] = _mm_tn(mix["cq"], dqr, name=f"attn_w_uq_grad_{j}")[0]
            wgrad("attn_w_ukv", j, mix["ckv"], dkvr)
            wgrad("attn_w_down", j, mix["h"], da, keep=DOWN_DIM)
            dx, dg = _nt_rms_bwd(da, mix["w_down"], sv["x_in"], sv["g_mix"], dx, name=f"mla_down_bwd_{j}")
            gs["attn_g_qnorm"][j] = dgq[0, :QK_DIM]
            gs["attn_g_knorm"][j] = dgk[0, :QK_DIM]
            gs["attn_g_q_a"][j] = dgqa[0]
            gs["attn_g_kv_a"][j] = dgkva[0]
        else:
            dz = _mm_nt(dx, wb["conv_w_out"], layer=j, out_dtype=F32, name=f"conv_out_bwd_{j}")
            wgrad("conv_w_out", j, mix["z"], dx)
            dbcu, dcw = _conv_gate_bwd(mix["bcu"], dz, gains["conv_w"][j], name=f"conv_gate_bwd_{j}")
            gs["conv_w"][j] = dcw
            wgrad("conv_w_in", j, mix["h"], dbcu)
            dx, dg = _nt_rms_bwd(dbcu, wb["conv_w_in"], sv["x_in"], sv["g_mix"], dx, layer=j, name=f"conv_in_bwd_{j}")
        gs["g_mix"][i] = dg[0]

    gw["attn_w_uq"] = jnp.stack(g_uq).reshape(2, Q_RANK, N_HEADS, QK_PAD)[..., :QK_DIM].reshape(BIG["attn_w_uq"][0])
    grads_small = {n: jnp.stack(v) for n, v in gs.items()}
    return loss, dx, gw, grads_small, exchanged


def kernel(x, positions, g_mix, g_mlp, attn_w_down, attn_g_q_a, attn_g_kv_a, attn_w_uq, attn_w_ukv, attn_g_qnorm, attn_g_knorm, attn_w_o, conv_w_in, conv_w, conv_w_out, mlp_w1, mlp_w2, loss_target, m_g_mix, m_g_mlp, m_attn_w_down, m_attn_g_q_a, m_attn_g_kv_a, m_attn_w_uq, m_attn_w_ukv, m_attn_g_qnorm, m_attn_g_knorm, m_attn_w_o, m_conv_w_in, m_conv_w, m_conv_w_out, m_mlp_w1, m_mlp_w2, v_g_mix, v_g_mlp, v_attn_w_down, v_attn_g_q_a, v_attn_g_kv_a, v_attn_w_uq, v_attn_w_ukv, v_attn_g_qnorm, v_attn_g_knorm, v_attn_w_o, v_conv_w_in, v_conv_w, v_conv_w_out, v_mlp_w1, v_mlp_w2):
    w = dict(g_mix=g_mix, g_mlp=g_mlp, attn_w_down=attn_w_down, attn_g_q_a=attn_g_q_a, attn_g_kv_a=attn_g_kv_a,
             attn_w_uq=attn_w_uq, attn_w_ukv=attn_w_ukv, attn_g_qnorm=attn_g_qnorm, attn_g_knorm=attn_g_knorm,
             attn_w_o=attn_w_o, conv_w_in=conv_w_in, conv_w=conv_w, conv_w_out=conv_w_out, mlp_w1=mlp_w1, mlp_w2=mlp_w2)
    m = dict(g_mix=m_g_mix, g_mlp=m_g_mlp, attn_w_down=m_attn_w_down, attn_g_q_a=m_attn_g_q_a,
             attn_g_kv_a=m_attn_g_kv_a, attn_w_uq=m_attn_w_uq, attn_w_ukv=m_attn_w_ukv, attn_g_qnorm=m_attn_g_qnorm,
             attn_g_knorm=m_attn_g_knorm, attn_w_o=m_attn_w_o, conv_w_in=m_conv_w_in, conv_w=m_conv_w,
             conv_w_out=m_conv_w_out, mlp_w1=m_mlp_w1, mlp_w2=m_mlp_w2)
    v = dict(g_mix=v_g_mix, g_mlp=v_g_mlp, attn_w_down=v_attn_w_down, attn_g_q_a=v_attn_g_q_a,
             attn_g_kv_a=v_attn_g_kv_a, attn_w_uq=v_attn_w_uq, attn_w_ukv=v_attn_w_ukv, attn_g_qnorm=v_attn_g_qnorm,
             attn_g_knorm=v_attn_g_knorm, attn_w_o=v_attn_w_o, conv_w_in=v_conv_w_in, conv_w=v_conv_w,
             conv_w_out=v_conv_w_out, mlp_w1=v_mlp_w1, mlp_w2=v_mlp_w2)
    cx, cy, cc_ = _place()

    chip = 2 * cx + cy

    def own_offset(shape, axis):
        return tuple(chip * (shape[axis] // N_CHIPS) if i == axis else 0 for i in range(3))

    chip_arr = chip.astype(jnp.int32).reshape(1)
    fulls = {n: _place_own(w[n], n, chip_arr, name=f"place_{n}") for n in BIG}
    later = {n: fulls.pop(n) for n in GATHER_LATER}
    wb = _gather_weights(fulls, name="gather_weights")

    placed = lax.dynamic_update_slice(jnp.zeros(CONV_W, F32), conv_w, own_offset(CONV_W, 2))
    conv_w_full = 0.5 * _all_reduce_small(placed.reshape(-1, 128), name="conv_w_gather").reshape(CONV_W)

    gains = {n: w[n] for n in SMALL}
    gains["conv_w"] = conv_w_full

    loss, grad_x, grads_big, grads_small, (r1_early, r2_early) = _local_step(
        x[0], positions[0], loss_target[0], wb, gains, later)

    place = jnp.stack([cc_, chip]).astype(jnp.int32)
    r1_late, partials = _chip_partials(grads_big, LATE, tag="late")
    r1 = {**r1_early, **r1_late}
    r2 = {**r2_early, **_scatter_partials(partials)}
    halves = {n: _final_sum(grads_big[n], r1[n], r2[n], place, n, name=f"grad_final_sum_{n}") for n in BIG}
    grad_shards = _join_halves(halves)

    small = _unpack_small(_all_reduce_small(_pack_small(grads_small, SMALL_GRADS), name="gain_all_reduce"), SMALL_GRADS)
    grad_shards["conv_w"] = lax.dynamic_slice(small["conv_w"], own_offset(CONV_W, 2), conv_w.shape)

    loss_total = lax.psum(loss[0, 0], ("x", "y", "c"))

    grads, deltas, new_m, new_v = {}, {}, {}, {}
    for n in [*BIG, "conv_w"]:
        shp = w[n].shape
        two_d = (shp[0] * shp[1], shp[2])
        g2 = grad_shards[n].reshape(two_d)
        d, nm, nv, g = _adamw(w[n].reshape(two_d), g2, m[n].reshape(two_d), v[n].reshape(two_d), name=f"adamw_{n}")
        grads[n], deltas[n], new_m[n], new_v[n] = g.reshape(shp), d.reshape(shp), nm.reshape(shp), nv.reshape(shp)
    d, nm, nv, g = _adamw(_pack_small(w, SMALL), _pack_small(small, SMALL), _pack_small(m, SMALL),
                          _pack_small(v, SMALL), name="adamw_gains")
    d, nm, nv, g = (_unpack_small(t, SMALL) for t in (d, nm, nv, g))
    for n in SMALL:
        grads[n], deltas[n], new_m[n], new_v[n] = g[n], d[n], nm[n], nv[n]

    return (loss_total, grad_x[None],
            *[grads[n] for n in WEIGHT_ORDER], *[deltas[n] for n in WEIGHT_ORDER],
            *[new_m[n] for n in WEIGHT_ORDER], *[new_v[n] for n in WEIGHT_ORDER])
```

```python
import functools

import jax
import jax.numpy as jnp
from jax import lax
from jax.experimental import pallas as pl
from jax.experimental.pallas import tpu as pltpu

F32 = jnp.float32
BF16 = jnp.bfloat16

D_MODEL = 1024
N_HEADS = 8
QK_NOPE = 128
QK_ROPE = 64
QK_DIM = QK_NOPE + QK_ROPE
QK_PAD = 256
V_DIM = 128
Q_RANK = 256
KV_RANK = 128
DOWN_DIM = Q_RANK + KV_RANK + QK_ROPE
DOWN_PAD = 512
ROPE_THETA = 10000.0
EPS = 1e-6
NEG = -1e30
SCALE = QK_DIM ** -0.5
SCALE_LOG2E = SCALE * 1.4426950408889634
LOG2E = 1.4426950408889634
ATTN_CHAINS = 2
DIAG_CHAINS = 4

ADAM_LR = 0.001
ADAM_B1 = 0.9
ADAM_B2 = 0.999
ADAM_EPS = 1e-08
ADAM_WD = 0.01
ADAM_STEP = 10

N_CHIPS = 4
MESH = pl.DeviceIdType.MESH
ANY = pl.BlockSpec(memory_space=pl.ANY)

TM = 512
TM_WIDE = 512
FWD_TQ = 1024
FWD_TK = 1024
BWD_TQ = 1024
BWD_TK = 1024
HALO = 16
T_PREP = 512
T_RED = 2048
SUM_BLOCK_BYTES = 2 * 1024 * 1024


def _tile(n, pref):
    t = min(n, pref)
    assert n % t == 0, (n, t)
    return t


def _cparams(*sem):
    return pltpu.CompilerParams(dimension_semantics=sem)


def _dot(a, b):
    return jnp.dot(a, b, preferred_element_type=F32)


def _dot_nt(a, b):
    return lax.dot_general(a, b, (((1,), (1,)), ((), ())), preferred_element_type=F32)


def _dot_tn(a, b):
    return lax.dot_general(a, b, (((0,), (0,)), ((), ())), preferred_element_type=F32)


def _rms(x, width):
    r = lax.rsqrt(jnp.sum(x * x, axis=-1, keepdims=True) * (1.0 / width) + EPS)
    return x * r, r


def _rms_bwd(xhat, r, dxhat, width):
    return r * (dxhat - xhat * (jnp.sum(dxhat * xhat, axis=-1, keepdims=True) * (1.0 / width)))


def _rope(t, cc, sa, sb):
    return t * cc + pltpu.roll(t, 96, 1) * sa + pltpu.roll(t, 32, 1) * sb


def _rope_t(g, cc, sa, sb):
    return g * cc + pltpu.roll(g * sa, 32, 1) + pltpu.roll(g * sb, 96, 1)


def _wspec(w, layer):
    once = pl.Buffered(1)
    if w.ndim == 2:
        return pl.BlockSpec(w.shape, lambda *_: (0, 0), pipeline_mode=once)
    return pl.BlockSpec((None,) + w.shape[1:], lambda *_: (layer, 0, 0), pipeline_mode=once)


def _mm_nn(a, b, *, out_dtype, name, residual=None, layer=0):
    m, k = a.shape
    n = b.shape[-1]
    tm = _tile(m, TM)

    def body(*refs):
        if residual is None:
            a_ref, b_ref, o_ref = refs
        else:
            a_ref, b_ref, r_ref, o_ref = refs
        acc = _dot(a_ref[...].astype(BF16), b_ref[...])
        if residual is not None:
            acc = acc + r_ref[...]
        o_ref[...] = acc.astype(o_ref.dtype)

    in_specs = [pl.BlockSpec((tm, k), lambda i: (i, 0)), _wspec(b, layer)]
    args = [a, b]
    if residual is not None:
        in_specs.append(pl.BlockSpec((tm, n), lambda i: (i, 0)))
        args.append(residual)
    return pl.pallas_call(
        body, name=name, grid=(m // tm,), in_specs=in_specs,
        out_specs=pl.BlockSpec((tm, n), lambda i: (i, 0)),
        out_shape=jax.ShapeDtypeStruct((m, n), out_dtype),
        compiler_params=_cparams("parallel"),
    )(*args)


def _mm_nt(a, b, *, out_dtype, name, layer=0):
    m, k = a.shape
    n = b.shape[-2]
    tm = _tile(m, TM)

    def body(a_ref, b_ref, o_ref):
        o_ref[...] = _dot_nt(a_ref[...].astype(BF16), b_ref[...]).astype(o_ref.dtype)

    return pl.pallas_call(
        body, name=name, grid=(m // tm,),
        in_specs=[pl.BlockSpec((tm, k), lambda i: (i, 0)), _wspec(b, layer)],
        out_specs=pl.BlockSpec((tm, n), lambda i: (i, 0)),
        out_shape=jax.ShapeDtypeStruct((m, n), out_dtype),
        compiler_params=_cparams("parallel"),
    )(a, b)


def _mm_tn(a, b, *, name, stack=None, layer=0, layers=1, keep=None, sqrelu_a=False, swap=None):
    s, ka = a.shape
    n = b.shape[1]
    ts = _tile(s, T_RED)
    tka = _tile(ka, 1024)
    tn = _tile(n, 1024)
    n_out = n if keep is None else keep
    assert keep is None or tn == n
    grid = (ka // tka, n // tn, s // ts)
    names = list(swap or {})
    ns = len(names)
    n_in = 2 + (stack is not None)

    def body(*refs):
        a_ref, b_ref = refs[:2]
        o_ref = refs[n_in + ns]
        if ns:
            copies = _swap_copies(refs[n_in:n_in + ns], refs[n_in + ns + 1:n_in + 2 * ns + 1], names,
                                  *refs[n_in + 2 * ns + 1:])
            _start_at_first_step(copies, grid)

        @pl.when(pl.program_id(2) == 0)
        def _():
            o_ref[...] = jnp.zeros_like(o_ref)

        a_t = a_ref[...]
        if sqrelu_a:
            a_t = _sqrelu(a_t.astype(F32))
        o_ref[...] += _dot_tn(a_t.astype(BF16), b_ref[...].astype(BF16))[:, :n_out if keep else tn]
        if ns:
            _wait_at_last_step(copies, grid)

    in_specs = [pl.BlockSpec((ts, tka), lambda i, j, t: (t, i)), pl.BlockSpec((ts, tn), lambda i, j, t: (t, j))]
    args = [a, b]
    if stack is not None:
        in_specs.append(ANY)
        args.append(stack)
    sent = [swap[nm] for nm in names]
    out = pl.pallas_call(
        body, name=name, grid=grid, in_specs=in_specs + [ANY] * ns,
        out_specs=[pl.BlockSpec((None, tka, tn if keep is None else keep), lambda i, j, t: (layer, i, j))] + [ANY] * ns,
        out_shape=[jax.ShapeDtypeStruct((layers, ka, n_out), F32)] + _swap_out_shapes(sent),
        scratch_shapes=_swap_sems(ns),
        input_output_aliases={} if stack is None else {2: 0},
        compiler_params=_cparams(*(["arbitrary"] * 3 if ns else ["parallel", "parallel", "arbitrary"])),
    )(*args, *sent)
    return (out[0], dict(zip(names, out[1:]))) if ns else out[0]


def _norm_mm(x, g, w, *, out_dtype, name, layer=0):
    s, d = x.shape
    n = w.shape[-1]
    tm = _tile(s, TM)

    def body(x_ref, g_ref, w_ref, h_ref, o_ref):
        xhat, _ = _rms(x_ref[...], d)
        h = (xhat * g_ref[...]).astype(BF16)
        h_ref[...] = h
        o_ref[...] = _dot(h, w_ref[...]).astype(o_ref.dtype)

    return pl.pallas_call(
        body, name=name, grid=(s // tm,),
        in_specs=[pl.BlockSpec((tm, d), lambda i: (i, 0)), pl.BlockSpec((1, d), lambda i: (0, 0)), _wspec(w, layer)],
        out_specs=[pl.BlockSpec((tm, d), lambda i: (i, 0)), pl.BlockSpec((tm, n), lambda i: (i, 0))],
        out_shape=[jax.ShapeDtypeStruct((s, d), BF16), jax.ShapeDtypeStruct((s, n), out_dtype)],
        compiler_params=_cparams("parallel"),
    )(x, g, w)


def _nt_rms_bwd(dy, w, x, g, dres, *, name, layer=0, swap=None):
    s, n = dy.shape
    d = x.shape[1]
    tm = _tile(s, TM)
    grid = (s // tm,)
    names = list(swap or {})
    ns = len(names)

    def body(dy_ref, w_ref, x_ref, g_ref, dres_ref, *rest):
        dx_ref, dg_ref = rest[ns:ns + 2]
        if ns:
            copies = _swap_copies(rest[:ns], rest[ns + 2:2 * ns + 2], names, *rest[2 * ns + 2:])
            _start_at_first_step(copies, grid)

        @pl.when(pl.program_id(0) == 0)
        def _():
            dg_ref[...] = jnp.zeros_like(dg_ref)

        dh = _dot_nt(dy_ref[...], w_ref[...])
        xhat, r = _rms(x_ref[...], d)
        dg_ref[...] += jnp.sum(dh * xhat, axis=0, keepdims=True)
        dx_ref[...] = dres_ref[...] + _rms_bwd(xhat, r, dh * g_ref[...], d)
        if ns:
            _wait_at_last_step(copies, grid)

    sent = [swap[nm] for nm in names]
    out = pl.pallas_call(
        body, name=name, grid=grid,
        in_specs=[pl.BlockSpec((tm, n), lambda i: (i, 0)), _wspec(w, layer),
                  pl.BlockSpec((tm, d), lambda i: (i, 0)), pl.BlockSpec((1, d), lambda i: (0, 0)),
                  pl.BlockSpec((tm, d), lambda i: (i, 0))] + [ANY] * ns,
        out_specs=[pl.BlockSpec((tm, d), lambda i: (i, 0)), pl.BlockSpec((1, d), lambda i: (0, 0))] + [ANY] * ns,
        out_shape=[jax.ShapeDtypeStruct((s, d), F32), jax.ShapeDtypeStruct((1, d), F32)] + _swap_out_shapes(sent),
        scratch_shapes=_swap_sems(ns),
        compiler_params=_cparams("arbitrary"),
    )(dy, w, x, g, dres, *sent)
    return (out[0], out[1], dict(zip(names, out[2:]))) if ns else (out[0], out[1])


def _sqrelu(u):
    return jnp.square(jnp.maximum(u, 0.0))


def _mlp_fwd(x, g, w1, w2, *, name, layer=0):
    s, d = x.shape
    n = w1.shape[-1]
    tm = _tile(s, TM_WIDE)

    def body(x_ref, g_ref, w1_ref, w2_ref, h_ref, u_ref, y_ref):
        x_t = x_ref[...]
        xhat, _ = _rms(x_t, d)
        h = (xhat * g_ref[...]).astype(BF16)
        h_ref[...] = h
        u = _dot(h, w1_ref[...])
        u_ref[...] = u.astype(BF16)
        y_ref[...] = x_t + _dot(_sqrelu(u).astype(BF16), w2_ref[...])

    return pl.pallas_call(
        body, name=name, grid=(s // tm,),
        in_specs=[pl.BlockSpec((tm, d), lambda i: (i, 0)), pl.BlockSpec((1, d), lambda i: (0, 0)),
                  _wspec(w1, layer), _wspec(w2, layer)],
        out_specs=[pl.BlockSpec((tm, d), lambda i: (i, 0)), pl.BlockSpec((tm, n), lambda i: (i, 0)),
                   pl.BlockSpec((tm, d), lambda i: (i, 0))],
        out_shape=[jax.ShapeDtypeStruct((s, d), BF16), jax.ShapeDtypeStruct((s, n), BF16),
                   jax.ShapeDtypeStruct((s, d), F32)],
        compiler_params=_cparams("parallel"),
    )(x, g, w1, w2)


def _mlp_down_bwd(dy, w2, u, *, name, layer=0):
    s, d = dy.shape
    n = w2.shape[-2]
    tm = _tile(s, TM_WIDE)

    def body(dy_ref, w_ref, u_ref, du_ref):
        dact = _dot_nt(dy_ref[...].astype(BF16), w_ref[...])
        du_ref[...] = (dact * (2.0 * jnp.maximum(u_ref[...].astype(F32), 0.0))).astype(BF16)

    return pl.pallas_call(
        body, name=name, grid=(s // tm,),
        in_specs=[pl.BlockSpec((tm, d), lambda i: (i, 0)), _wspec(w2, layer),
                  pl.BlockSpec((tm, n), lambda i: (i, 0))],
        out_specs=pl.BlockSpec((tm, n), lambda i: (i, 0)),
        out_shape=jax.ShapeDtypeStruct((s, n), BF16),
        compiler_params=_cparams("parallel"),
    )(dy, w2, u)


def _conv_gate(bcu, conv_w, *, name):
    s = bcu.shape[0]
    d = D_MODEL
    tm = _tile(s, TM)
    hb = tm // HALO

    def body(bcu_ref, prev_ref, w_ref, z_ref, pbuf):
        i = pl.program_id(0)
        gb = bcu_ref[:, 0:d].astype(F32)
        p = bcu_ref[:, d:2 * d].astype(F32) * bcu_ref[:, 2 * d:3 * d].astype(F32)
        pprev = prev_ref[:, d:2 * d].astype(F32) * prev_ref[:, 2 * d:3 * d].astype(F32)
        pbuf[0:HALO, :] = jnp.where(i > 0, pprev, 0.0)
        pbuf[HALO:HALO + tm, :] = p
        cv = (w_ref[2:3, :] * p + w_ref[1:2, :] * pbuf[HALO - 1:HALO - 1 + tm, :]
              + w_ref[0:1, :] * pbuf[HALO - 2:HALO - 2 + tm, :])
        z_ref[...] = (gb * cv).astype(BF16)

    return pl.pallas_call(
        body, name=name, grid=(s // tm,),
        in_specs=[pl.BlockSpec((tm, 3 * d), lambda i: (i, 0)),
                  pl.BlockSpec((HALO, 3 * d), lambda i: (jnp.maximum(i * hb - 1, 0), 0)),
                  pl.BlockSpec((3, d), lambda i: (0, 0))],
        out_specs=pl.BlockSpec((tm, d), lambda i: (i, 0)),
        out_shape=jax.ShapeDtypeStruct((s, d), BF16),
        scratch_shapes=[pltpu.VMEM((tm + HALO, d), F32)],
        compiler_params=_cparams("parallel"),
    )(bcu, bcu, conv_w)


def _conv_gate_bwd(bcu, dz, conv_w, *, name):
    s = bcu.shape[0]
    d = D_MODEL
    tm = _tile(s, TM)
    hb = tm // HALO
    nt = s // tm

    def body(bcu_ref, prev_ref, next_ref, dz_ref, dznext_ref, w_ref, dbcu_ref, dw_ref, pbuf, dbuf):
        i = pl.program_id(0)

        @pl.when(i == 0)
        def _():
            dw_ref[...] = jnp.zeros_like(dw_ref)

        gb = bcu_ref[:, 0:d].astype(F32)
        gc = bcu_ref[:, d:2 * d].astype(F32)
        uu = bcu_ref[:, 2 * d:3 * d].astype(F32)
        p = gc * uu
        pprev = prev_ref[:, d:2 * d].astype(F32) * prev_ref[:, 2 * d:3 * d].astype(F32)
        pbuf[0:HALO, :] = jnp.where(i > 0, pprev, 0.0)
        pbuf[HALO:HALO + tm, :] = p
        p1 = pbuf[HALO - 1:HALO - 1 + tm, :]
        p2 = pbuf[HALO - 2:HALO - 2 + tm, :]
        cv = w_ref[2:3, :] * p + w_ref[1:2, :] * p1 + w_ref[0:1, :] * p2
        dz_t = dz_ref[...]
        dcv = dz_t * gb
        dcv_next = dznext_ref[...] * next_ref[:, 0:d].astype(F32)
        dbuf[0:tm, :] = dcv
        dbuf[tm:tm + HALO, :] = jnp.where(i < nt - 1, dcv_next, 0.0)
        dp = w_ref[2:3, :] * dcv + w_ref[1:2, :] * dbuf[1:1 + tm, :] + w_ref[0:1, :] * dbuf[2:2 + tm, :]
        dw_ref[2:3, :] += jnp.sum(dcv * p, axis=0, keepdims=True)
        dw_ref[1:2, :] += jnp.sum(dcv * p1, axis=0, keepdims=True)
        dw_ref[0:1, :] += jnp.sum(dcv * p2, axis=0, keepdims=True)
        dbcu_ref[:, 0:d] = (dz_t * cv).astype(BF16)
        dbcu_ref[:, d:2 * d] = (dp * uu).astype(BF16)
        dbcu_ref[:, 2 * d:3 * d] = (dp * gc).astype(BF16)

    nxt = lambda i: (jnp.minimum((i + 1) * hb, s // HALO - 1), 0)
    return pl.pallas_call(
        body, name=name, grid=(nt,),
        in_specs=[pl.BlockSpec((tm, 3 * d), lambda i: (i, 0)),
                  pl.BlockSpec((HALO, 3 * d), lambda i: (jnp.maximum(i * hb - 1, 0), 0)),
                  pl.BlockSpec((HALO, 3 * d), nxt),
                  pl.BlockSpec((tm, d), lambda i: (i, 0)),
                  pl.BlockSpec((HALO, d), nxt),
                  pl.BlockSpec((3, d), lambda i: (0, 0))],
        out_specs=[pl.BlockSpec((tm, 3 * d), lambda i: (i, 0)), pl.BlockSpec((3, d), lambda i: (0, 0))],
        out_shape=[jax.ShapeDtypeStruct((s, 3 * d), BF16), jax.ShapeDtypeStruct((3, d), F32)],
        scratch_shapes=[pltpu.VMEM((tm + HALO, d), F32), pltpu.VMEM((tm + HALO, d), F32)],
        compiler_params=_cparams("arbitrary"),
    )(bcu, bcu, bcu, dz, dz, conv_w)


def _mla_prep(a, g_qa, g_kva, w_uq, w_ukv, g_q, g_k, cc, sa, sb, *, name):
    s = a.shape[0]
    ts = _tile(s, T_PREP)

    def body(a_ref, gqa_ref, gkva_ref, wuq_ref, wukv_ref, gq_ref, gk_ref, cc_ref, sa_ref, sb_ref,
             cq_ref, ckv_ref, q_ref, k_ref, v_ref):
        xq, _ = _rms(a_ref[:, 0:Q_RANK], Q_RANK)
        cq = (xq * gqa_ref[...]).astype(BF16)
        cq_ref[...] = cq
        xkv, _ = _rms(a_ref[:, Q_RANK:Q_RANK + KV_RANK], KV_RANK)
        ckv = (xkv * gkva_ref[...]).astype(BF16)
        ckv_ref[...] = ckv
        kpe = a_ref[:, Q_RANK + KV_RANK:DOWN_PAD]
        kpe_ss = jnp.sum(kpe * kpe, axis=-1, keepdims=True)
        cc_t, sa_t, sb_t = cc_ref[...], sa_ref[...], sb_ref[...]
        gq = gq_ref[...]
        gk = gk_ref[...]
        for h in range(N_HEADS):
            cols = slice(h * QK_PAD, (h + 1) * QK_PAD)
            qhat, _ = _rms(_dot(cq, wuq_ref[:, cols]), QK_DIM)
            qn = qhat * (gq * SCALE_LOG2E)
            q_ref[h, :, 0:QK_NOPE] = qn[:, 0:QK_NOPE].astype(BF16)
            q_ref[h, :, QK_NOPE:QK_PAD] = _rope(qn[:, QK_NOPE:QK_PAD], cc_t, sa_t, sb_t).astype(BF16)
            kvr = _dot(ckv, wukv_ref[:, cols])
            kn = kvr[:, 0:QK_NOPE]
            rk = lax.rsqrt((jnp.sum(kn * kn, axis=-1, keepdims=True) + kpe_ss) * (1.0 / QK_DIM) + EPS)
            k_ref[h, :, 0:QK_NOPE] = (kn * rk * gk[:, 0:QK_NOPE]).astype(BF16)
            k_ref[h, :, QK_NOPE:QK_PAD] = _rope(kpe * rk * gk[:, QK_NOPE:QK_PAD], cc_t, sa_t, sb_t).astype(BF16)
            v_ref[h, :, 0:V_DIM] = kvr[:, QK_NOPE:QK_PAD].astype(BF16)
            v_ref[h, :, V_DIM:2 * V_DIM] = jnp.ones((ts, V_DIM), BF16)

    row = lambda i: (i, 0)
    fixed = lambda i: (0, 0)
    head = lambda i: (0, i, 0)
    return pl.pallas_call(
        body, name=name, grid=(s // ts,),
        in_specs=[pl.BlockSpec((ts, DOWN_PAD), row), pl.BlockSpec((1, Q_RANK), fixed), pl.BlockSpec((1, KV_RANK), fixed),
                  pl.BlockSpec((Q_RANK, N_HEADS * QK_PAD), fixed), pl.BlockSpec((KV_RANK, N_HEADS * QK_PAD), fixed),
                  pl.BlockSpec((1, QK_PAD), fixed), pl.BlockSpec((1, QK_PAD), fixed),
                  pl.BlockSpec((ts, 128), row), pl.BlockSpec((ts, 128), row), pl.BlockSpec((ts, 128), row)],
        out_specs=[pl.BlockSpec((ts, Q_RANK), row), pl.BlockSpec((ts, KV_RANK), row),
                   pl.BlockSpec((N_HEADS, ts, QK_PAD), head), pl.BlockSpec((N_HEADS, ts, QK_PAD), head),
                   pl.BlockSpec((N_HEADS, ts, 2 * V_DIM), head)],
        out_shape=[jax.ShapeDtypeStruct((s, Q_RANK), BF16), jax.ShapeDtypeStruct((s, KV_RANK), BF16),
                   jax.ShapeDtypeStruct((N_HEADS, s, QK_PAD), BF16), jax.ShapeDtypeStruct((N_HEADS, s, QK_PAD), BF16),
                   jax.ShapeDtypeStruct((N_HEADS, s, 2 * V_DIM), BF16)],
        compiler_params=_cparams("parallel"),
    )(a, g_qa, g_kva, w_uq, w_ukv, g_q, g_k, cc, sa, sb)


def _mla_prep_bwd(a, g_qa, g_kva, w_uq, w_ukv, g_q, g_k, cc, sa, sb, dq, dk, dv, *, name):
    s = a.shape[0]
    ts = _tile(s, T_PREP)

    def body(a_ref, gqa_ref, gkva_ref, wuq_ref, wukv_ref, gq_ref, gk_ref, cc_ref, sa_ref, sb_ref,
             dq_ref, dk_ref, dv_ref, dqr_ref, dkvr_ref, da_ref, dgq_ref, dgk_ref, dgqa_ref, dgkva_ref):
        @pl.when(pl.program_id(0) == 0)
        def _():
            dgq_ref[...] = jnp.zeros_like(dgq_ref)
            dgk_ref[...] = jnp.zeros_like(dgk_ref)
            dgqa_ref[...] = jnp.zeros_like(dgqa_ref)
            dgkva_ref[...] = jnp.zeros_like(dgkva_ref)

        xq, r_q = _rms(a_ref[:, 0:Q_RANK], Q_RANK)
        cq = (xq * gqa_ref[...]).astype(BF16)
        xkv, r_kv = _rms(a_ref[:, Q_RANK:Q_RANK + KV_RANK], KV_RANK)
        ckv = (xkv * gkva_ref[...]).astype(BF16)
        kpe = a_ref[:, Q_RANK + KV_RANK:DOWN_PAD]
        kpe_ss = jnp.sum(kpe * kpe, axis=-1, keepdims=True)
        cc_t, sa_t, sb_t = cc_ref[...], sa_ref[...], sb_ref[...]
        gq = gq_ref[...]
        gk = gk_ref[...]
        dcq = jnp.zeros((ts, Q_RANK), F32)
        dckv = jnp.zeros((ts, KV_RANK), F32)
        dkpe = jnp.zeros((ts, 128), F32)
        dgq = jnp.zeros((1, QK_PAD), F32)
        dgk_n = jnp.zeros((1, QK_NOPE), F32)
        dgk_p = jnp.zeros((1, 128), F32)
        for h in range(N_HEADS):
            cols = slice(h * QK_PAD, (h + 1) * QK_PAD)
            qhat, rq = _rms(_dot(cq, wuq_ref[:, cols]), QK_DIM)
            dqn = jnp.concatenate(
                [dq_ref[h, :, 0:QK_NOPE], _rope_t(dq_ref[h, :, QK_NOPE:QK_PAD], cc_t, sa_t, sb_t)], axis=1)
            dgq = dgq + jnp.sum(dqn * qhat, axis=0, keepdims=True)
            dqr = _rms_bwd(qhat, rq, dqn * gq, QK_DIM).astype(BF16)
            dqr_ref[:, cols] = dqr
            dcq = dcq + _dot_nt(dqr, wuq_ref[:, cols])
            kn = _dot(ckv, wukv_ref[:, h * QK_PAD:h * QK_PAD + QK_NOPE])
            rk = lax.rsqrt((jnp.sum(kn * kn, axis=-1, keepdims=True) + kpe_ss) * (1.0 / QK_DIM) + EPS)
            khat_n = kn * rk
            khat_p = kpe * rk
            dkn = dk_ref[h, :, 0:QK_NOPE]
            dkp = _rope_t(dk_ref[h, :, QK_NOPE:QK_PAD], cc_t, sa_t, sb_t)
            dgk_n = dgk_n + jnp.sum(dkn * khat_n, axis=0, keepdims=True)
            dgk_p = dgk_p + jnp.sum(dkp * khat_p, axis=0, keepdims=True)
            dxn = dkn * gk[:, 0:QK_NOPE]
            dxp = dkp * gk[:, QK_NOPE:QK_PAD]
            mean = (jnp.sum(dxn * khat_n, axis=-1, keepdims=True)
                    + jnp.sum(dxp * khat_p, axis=-1, keepdims=True)) * (1.0 / QK_DIM)
            dkpe = dkpe + rk * (dxp - khat_p * mean)
            dkvr = jnp.concatenate([rk * (dxn - khat_n * mean), dv_ref[h, :, :]], axis=1).astype(BF16)
            dkvr_ref[:, cols] = dkvr
            dckv = dckv + _dot_nt(dkvr, wukv_ref[:, cols])
        dgq_ref[...] += dgq
        dgk_ref[:, 0:QK_NOPE] += dgk_n
        dgk_ref[:, QK_NOPE:QK_PAD] += dgk_p
        dgqa_ref[...] += jnp.sum(dcq * xq, axis=0, keepdims=True)
        dgkva_ref[...] += jnp.sum(dckv * xkv, axis=0, keepdims=True)
        da_ref[:, 0:Q_RANK] = _rms_bwd(xq, r_q, dcq * gqa_ref[...], Q_RANK).astype(BF16)
        da_ref[:, Q_RANK:Q_RANK + KV_RANK] = _rms_bwd(xkv, r_kv, dckv * gkva_ref[...], KV_RANK).astype(BF16)
        da_ref[:, Q_RANK + KV_RANK:DOWN_PAD] = dkpe.astype(BF16)

    row = lambda i: (i, 0)
    fixed = lambda i: (0, 0)
    head = lambda i: (0, i, 0)
    wide = N_HEADS * QK_PAD
    return pl.pallas_call(
        body, name=name, grid=(s // ts,),
        in_specs=[pl.BlockSpec((ts, DOWN_PAD), row), pl.BlockSpec((1, Q_RANK), fixed), pl.BlockSpec((1, KV_RANK), fixed),
                  pl.BlockSpec((Q_RANK, wide), fixed), pl.BlockSpec((KV_RANK, wide), fixed),
                  pl.BlockSpec((1, QK_PAD), fixed), pl.BlockSpec((1, QK_PAD), fixed),
                  pl.BlockSpec((ts, 128), row), pl.BlockSpec((ts, 128), row), pl.BlockSpec((ts, 128), row),
                  pl.BlockSpec((N_HEADS, ts, QK_PAD), head), pl.BlockSpec((N_HEADS, ts, QK_PAD), head),
                  pl.BlockSpec((N_HEADS, ts, V_DIM), head)],
        out_specs=[pl.BlockSpec((ts, wide), row), pl.BlockSpec((ts, wide), row), pl.BlockSpec((ts, DOWN_PAD), row),
                   pl.BlockSpec((1, QK_PAD), fixed), pl.BlockSpec((1, QK_PAD), fixed),
                   pl.BlockSpec((1, Q_RANK), fixed), pl.BlockSpec((1, KV_RANK), fixed)],
        out_shape=[jax.ShapeDtypeStruct((s, wide), BF16), jax.ShapeDtypeStruct((s, wide), BF16),
                   jax.ShapeDtypeStruct((s, DOWN_PAD), BF16),
                   jax.ShapeDtypeStruct((1, QK_PAD), F32), jax.ShapeDtypeStruct((1, QK_PAD), F32),
                   jax.ShapeDtypeStruct((1, Q_RANK), F32), jax.ShapeDtypeStruct((1, KV_RANK), F32)],
        compiler_params=_cparams("arbitrary"),
    )(a, g_qa, g_kva, w_uq, w_ukv, g_q, g_k, cc, sa, sb, dq, dk, dv)


def _flash_fwd(q, k, v, pos_col, pos_row, *, name, gather=None):
    nh, s, _ = q.shape
    tq = _tile(s, FWD_TQ)
    tk = _tile(s, FWD_TK)
    sq = tq // ATTN_CHAINS
    nq = s // tq
    names = list(gather or {})
    ng = len(names)

    def body(q_ref, k_ref, v_ref, pq_ref, pk_ref, *rest):
        o_ref, lse_ref = rest[ng:ng + 2]
        m_sc, acc_sc = rest[2 * ng + 2:2 * ng + 4]
        qb = pl.program_id(1)
        if ng:
            sends, recvs = _gather_ici_copies(rest[ng + 2:2 * ng + 2], names, *rest[2 * ng + 4:], base=0, stride=3)

            @pl.when((pl.program_id(0) == 0) & (qb == 0))
            def _():
                for cp in sends:
                    cp.start()

        m_sc[...] = jnp.full_like(m_sc, NEG)
        acc_sc[...] = jnp.zeros_like(acc_sc)

        def step(kb, masked):
            trim = masked and tq == tk
            chains = DIAG_CHAINS if trim else ATTN_CHAINS
            sq = tq // chains
            start = pl.multiple_of(kb * tk, tk)
            widths = [(u + 1) * sq if trim else tk for u in range(chains)]
            scores = [_dot_nt(q_ref[0, u * sq:(u + 1) * sq, :], k_ref[0, pl.ds(start, widths[u]), :])
                      for u in range(chains)]
            for u in range(chains):
                rows = slice(u * sq, (u + 1) * sq)
                keys = pl.ds(start, widths[u])
                sc = scores[u]
                if masked:
                    sc = jnp.where(pq_ref[rows, :] >= pk_ref[:, keys], sc, NEG)
                m_prev = m_sc[rows, :]
                m_new = jnp.maximum(m_prev, jnp.max(sc, axis=-1, keepdims=True))
                alpha = jnp.exp2(m_prev - m_new)
                p = jnp.exp2(sc - jnp.tile(m_new, (1, widths[u] // 128)))
                acc_sc[rows, :] = (jnp.tile(alpha, (1, 2)) * acc_sc[rows, :]
                                   + _dot(p.astype(BF16), v_ref[0, keys, :]))
                m_sc[rows, :] = m_new

        n_before = (qb * tq) // tk
        n_seen = (qb * tq + tq - 1) // tk + 1
        lax.fori_loop(0, n_before, lambda kb, c: (step(kb, False), c)[1], 0)
        lax.fori_loop(n_before, n_seen, lambda kb, c: (step(kb, True), c)[1], 0)
        l = acc_sc[:, V_DIM:2 * V_DIM]
        o_ref[...] = (acc_sc[:, 0:V_DIM] / l).astype(BF16)
        lse = m_sc[...] * (1.0 / LOG2E) + jnp.log(l)
        lse_ref[0] = lse.T[0:1, :]

        if ng:
            @pl.when((pl.program_id(0) == nh - 1) & (qb == nq - 1))
            def _():
                for cp in recvs:
                    cp.wait_recv()
                for cp in sends:
                    cp.wait_send()

    arrays = [gather[nm] for nm in names]
    out = pl.pallas_call(
        body, name=name, grid=(nh, nq),
        in_specs=[pl.BlockSpec((1, tq, QK_PAD), lambda h, qb: (h, qb, 0)),
                  pl.BlockSpec((1, s, QK_PAD), lambda h, qb: (h, 0, 0)),
                  pl.BlockSpec((1, s, 2 * V_DIM), lambda h, qb: (h, 0, 0)),
                  pl.BlockSpec((tq, 1), lambda h, qb: (qb, 0)),
                  pl.BlockSpec((1, s), lambda h, qb: (0, 0))] + [ANY] * ng,
        out_specs=[pl.BlockSpec((tq, V_DIM), lambda h, qb: (qb, h)),
                   pl.BlockSpec((1, 1, tq), lambda h, qb: (h, 0, qb))] + [ANY] * ng,
        scratch_shapes=[pltpu.VMEM((tq, 128), F32), pltpu.VMEM((tq, 2 * V_DIM), F32)]
        + ([pltpu.SemaphoreType.DMA((3 * ng,)), pltpu.SemaphoreType.DMA((3 * ng,))] if ng else []),
        out_shape=[jax.ShapeDtypeStruct((s, nh * V_DIM), BF16), jax.ShapeDtypeStruct((nh, 1, s), F32)]
        + [jax.ShapeDtypeStruct(a.shape, a.dtype) for a in arrays],
        input_output_aliases={5 + i: 2 + i for i in range(ng)},
        compiler_params=_cparams("arbitrary", "arbitrary") if ng else _cparams("parallel", "parallel"),
    )(q, k, v, pos_col, pos_row, *arrays)
    return out[0], out[1], dict(zip(names, out[2:]))


def _attn_out_bwd(dy, w_o, o, *, name, layer=0):
    s, d = dy.shape
    n = w_o.shape[-2]
    tm = _tile(s, TM)

    def body(dy_ref, w_ref, o_ref, do_ref, d_ref):
        do = _dot_nt(dy_ref[...].astype(BF16), w_ref[...]).astype(BF16)
        do_ref[...] = do
        for h in range(N_HEADS):
            cols = slice(h * V_DIM, (h + 1) * V_DIM)
            prod = do[:, cols].astype(F32) * o_ref[:, cols].astype(F32)
            d_ref[h] = jnp.sum(prod.T, axis=0, keepdims=True)

    return pl.pallas_call(
        body, name=name, grid=(s // tm,),
        in_specs=[pl.BlockSpec((tm, d), lambda i: (i, 0)), _wspec(w_o, layer), pl.BlockSpec((tm, n), lambda i: (i, 0))],
        out_specs=[pl.BlockSpec((tm, n), lambda i: (i, 0)), pl.BlockSpec((N_HEADS, 1, tm), lambda i: (0, 0, i))],
        out_shape=[jax.ShapeDtypeStruct((s, n), BF16), jax.ShapeDtypeStruct((N_HEADS, 1, s), F32)],
        compiler_params=_cparams("parallel"),
    )(dy, w_o, o)


def _flash_bwd(q, k, v, do, lse_row, delta_row, pos_col, pos_row, *, name, scatter=None):
    nh, s, _ = q.shape
    tq = _tile(s, BWD_TQ)
    tk = _tile(s, BWD_TK)
    nq, nk = s // tq, s // tk
    sk = tk // ATTN_CHAINS
    names = list(scatter or {})
    ng = len(names)

    def body(q_ref, k_ref, v_ref, do_ref, lse_ref, delta_ref, pq_ref, pk_ref, *rest):
        dq_ref, dk_ref, dv_ref = rest[ng:ng + 3]
        dk_sc, dv_sc = rest[2 * ng + 3:2 * ng + 5]
        kb = pl.program_id(1)
        if ng:
            copies = _scatter_copies(rest[:ng], rest[ng + 3:2 * ng + 3], names, *rest[2 * ng + 5:])

            @pl.when((pl.program_id(0) == 0) & (kb == 0))
            def _():
                for cp in copies:
                    cp.start()

        @pl.when(kb == 0)
        def _():
            dq_ref[...] = jnp.zeros_like(dq_ref)

        dk_sc[...] = jnp.zeros_like(dk_sc)
        dv_sc[...] = jnp.zeros_like(dv_sc)

        def step(qb, masked):
            trim = masked and tq == tk
            chains = DIAG_CHAINS if trim else ATTN_CHAINS
            sk = tk // chains
            start = pl.multiple_of(qb * tq, tq)
            offs = [u * sk if trim else 0 for u in range(chains)]
            qss = [pl.ds(start + offs[u], tq - offs[u]) for u in range(chains)]
            qts = [q_ref[0, qss[u], :] for u in range(chains)]
            dots = [do_ref[qss[u], :] for u in range(chains)]
            sts = [_dot_nt(k_ref[0, u * sk:(u + 1) * sk, :], qts[u]) for u in range(chains)]
            dpts = [_dot_nt(v_ref[0, u * sk:(u + 1) * sk, :], dots[u]) for u in range(chains)]
            parts = []
            for u in range(chains):
                rows = slice(u * sk, (u + 1) * sk)
                pt = jnp.exp2(sts[u] - lse_ref[0, :, qss[u]] * LOG2E)
                if masked:
                    pt = jnp.where(pq_ref[:, qss[u]] >= pk_ref[rows, :], pt, 0.0)
                dv_sc[rows, :] += _dot(pt.astype(BF16), dots[u])
                dst = (pt * (dpts[u] - delta_ref[0, :, qss[u]])).astype(BF16)
                dk_sc[rows, :] += _dot(dst, qts[u])
                parts.append(_dot_tn(dst, k_ref[0, rows, :]))
            if trim:
                for u in range(chains):
                    dq_ref[0, qss[u], :] += parts[u]
            else:
                dq_ref[0, qss[0], :] += functools.reduce(lambda a, b: a + b, parts)

        q_first = (kb * tk) // tq
        q_clear = (kb * tk + tk - 1) // tq + 1
        lax.fori_loop(q_first, q_clear, lambda qb, c: (step(qb, True), c)[1], 0)
        lax.fori_loop(q_clear, nq, lambda qb, c: (step(qb, False), c)[1], 0)
        dk_ref[0] = dk_sc[...] * (1.0 / LOG2E)
        dv_ref[0] = dv_sc[...]

        @pl.when(kb == nk - 1)
        def _():
            dq_ref[...] = dq_ref[...] * SCALE

        if ng:
            @pl.when((pl.program_id(0) == nh - 1) & (kb == nk - 1))
            def _():
                for cp in copies:
                    cp.wait()

    arrays = [scatter[nm] for nm in names]
    out = pl.pallas_call(
        body, name=name, grid=(nh, nk),
        in_specs=[pl.BlockSpec((1, s, QK_PAD), lambda h, kb: (h, 0, 0)),
                  pl.BlockSpec((1, tk, QK_PAD), lambda h, kb: (h, kb, 0)),
                  pl.BlockSpec((1, tk, V_DIM), lambda h, kb: (h, kb, 0)),
                  pl.BlockSpec((s, V_DIM), lambda h, kb: (0, h)),
                  pl.BlockSpec((1, 1, s), lambda h, kb: (h, 0, 0)),
                  pl.BlockSpec((1, 1, s), lambda h, kb: (h, 0, 0)),
                  pl.BlockSpec((1, s), lambda h, kb: (0, 0)),
                  pl.BlockSpec((tk, 1), lambda h, kb: (kb, 0))] + [ANY] * ng,
        out_specs=[pl.BlockSpec((1, s, QK_PAD), lambda h, kb: (h, 0, 0)),
                   pl.BlockSpec((1, tk, QK_PAD), lambda h, kb: (h, kb, 0)),
                   pl.BlockSpec((1, tk, V_DIM), lambda h, kb: (h, kb, 0))] + [ANY] * ng,
        scratch_shapes=[pltpu.VMEM((tk, QK_PAD), F32), pltpu.VMEM((tk, V_DIM), F32)]
        + ([pltpu.SemaphoreType.DMA((3 * ng,)), pltpu.SemaphoreType.DMA((3 * ng,))] if ng else []),
        out_shape=[jax.ShapeDtypeStruct((nh, s, QK_PAD), F32), jax.ShapeDtypeStruct((nh, s, QK_PAD), F32),
                   jax.ShapeDtypeStruct((nh, s, V_DIM), F32)] + _scatter_out_shapes(names, arrays),
        compiler_params=_cparams("arbitrary", "arbitrary"),
    )(q, k, v, do, lse_row, delta_row, pos_row, pos_col, *arrays)
    return out[0], out[1], out[2], dict(zip(names, out[3:]))


def _loss_head(y, target, *, name):
    s, d = y.shape
    tm = _tile(s, TM)
    nt = s // tm

    def body(y_ref, t_ref, dy_ref, loss_ref, acc):
        i = pl.program_id(0)

        @pl.when(i == 0)
        def _():
            acc[...] = jnp.zeros_like(acc)

        e = y_ref[...] - t_ref[...]
        dy_ref[...] = e * (1.0 / d)
        acc[...] += jnp.sum((e * e).reshape(tm // 8, 8, d), axis=0)

        @pl.when(i == nt - 1)
        def _():
            loss_ref[...] = jnp.full((1, 128), 0.5 / d, F32) * jnp.sum(acc[...])

    return pl.pallas_call(
        body, name=name, grid=(nt,),
        in_specs=[pl.BlockSpec((tm, d), lambda i: (i, 0))] * 2,
        out_specs=[pl.BlockSpec((tm, d), lambda i: (i, 0)), pl.BlockSpec((1, 128), lambda i: (0, 0))],
        out_shape=[jax.ShapeDtypeStruct((s, d), F32), jax.ShapeDtypeStruct((1, 128), F32)],
        scratch_shapes=[pltpu.VMEM((8, d), F32)],
        compiler_params=_cparams("arbitrary"),
    )(y, target)


def _adamw(w, g, m, v, *, name):
    r, c = w.shape
    tr = _tile(r, 512) if r % 8 == 0 else r

    def body(w_ref, g_ref, m_ref, v_ref, d_ref, nm_ref, nv_ref, go_ref):
        g_t = g_ref[...]
        go_ref[...] = g_t
        nm = ADAM_B1 * m_ref[...] + (1.0 - ADAM_B1) * g_t
        nv = ADAM_B2 * v_ref[...] + (1.0 - ADAM_B2) * (g_t * g_t)
        m_hat = nm / (1.0 - ADAM_B1 ** ADAM_STEP)
        v_hat = nv / (1.0 - ADAM_B2 ** ADAM_STEP)
        d_ref[...] = -ADAM_LR * (m_hat / (jnp.sqrt(v_hat) + ADAM_EPS) + ADAM_WD * w_ref[...])
        nm_ref[...] = nm
        nv_ref[...] = nv

    spec = pl.BlockSpec((tr, c), lambda i: (i, 0))
    return pl.pallas_call(
        body, name=name, grid=(r // tr,), in_specs=[spec] * 4, out_specs=[spec] * 4,
        out_shape=[jax.ShapeDtypeStruct((r, c), F32)] * 4,
        compiler_params=_cparams("parallel"),
    )(w, g, m, v)


def _place():
    return lax.axis_index("x"), lax.axis_index("y"), lax.axis_index("c")


def _other_chips(x, y):
    return [(1 - x, y), (x, 1 - y), (1 - x, 1 - y)]


BIG = {
    "attn_w_down": ((2, 1024, 448), 1), "attn_w_uq": ((2, 256, 1536), 2), "attn_w_ukv": ((2, 128, 2048), 2),
    "attn_w_o": ((2, 1024, 1024), 1), "conv_w_in": ((2, 1024, 3072), 2),
    "conv_w_out": ((2, 1024, 1024), 1), "mlp_w1": ((4, 1024, 4096), 2), "mlp_w2": ((4, 4096, 1024), 1),
}
CONV_W = (2, 3, 1024)


def _shard_shape(name):
    shape, axis = BIG[name]
    return tuple(n // N_CHIPS if i == axis else n for i, n in enumerate(shape))


def _band(ref, name, layers, chip):
    shape, axis = BIG[name]
    width = shape[axis] // N_CHIPS
    if axis == 1:
        return ref.at[layers, pl.ds(chip * width, width), :]
    return ref.at[layers, :, pl.ds(chip * width, width)]


def _half(name, c):
    hl = BIG[name][0][0] // 2
    return pl.ds(c * hl, hl)


def _place_own(w, nm, chip, *, name):
    shape, axis = BIG[nm]
    layers, rows, cols = w.shape
    tr = _sum_rows(rows, cols)
    nrb = rows // tr
    if axis == 1:
        band = lambda l, i, ch: (l, ch[0] * nrb + i, 0)
    else:
        band = lambda l, i, ch: (l, i, ch[0])

    def body(chip_ref, w_ref, o_ref):
        o_ref[...] = w_ref[...].astype(BF16)

    return pl.pallas_call(
        body, name=name,
        grid_spec=pltpu.PrefetchScalarGridSpec(
            num_scalar_prefetch=1, grid=(layers, nrb),
            in_specs=[pl.BlockSpec((1, tr, cols), lambda l, i, ch: (l, i, 0))],
            out_specs=pl.BlockSpec((1, tr, cols), band)),
        out_shape=jax.ShapeDtypeStruct(shape, BF16),
        compiler_params=_cparams("parallel", "parallel"),
    )(chip, w)


def _gather_copies(outs, names, send_sems, recv_sems, *, base, stride, to_sibling):
    x, y, c = _place()
    me = 2 * x + y

    def copy(k, ref, nm, layers, chip, to):
        band = _band(ref, nm, layers, chip)
        return pltpu.make_async_remote_copy(
            src_ref=band, dst_ref=band, send_sem=send_sems.at[k], recv_sem=recv_sems.at[k],
            device_id=to, device_id_type=MESH)

    sends, recvs = [], []
    for i, nm in enumerate(names):
        for j, (cx, cy) in enumerate(_other_chips(x, y)):
            k = base + stride * i + j
            if to_sibling:
                sends.append(copy(k, outs[i], nm, _half(nm, c), 2 * cx + cy, (x, y, 1 - c)))
                recvs.append(copy(k, outs[i], nm, _half(nm, 1 - c), 2 * cx + cy, (x, y, c)))
            else:
                sends.append(copy(k, outs[i], nm, _half(nm, c), me, (cx, cy, c)))
                recvs.append(copy(k, outs[i], nm, _half(nm, c), 2 * cx + cy, (x, y, c)))
    return sends, recvs


def _gather_ici_copies(outs, names, send_sems, recv_sems, *, base, stride):
    return _gather_copies(outs, names, send_sems, recv_sems, base=base, stride=stride, to_sibling=False)


def _gather_weights(fulls, *, name, ici=True):
    names = list(fulls)
    n = len(names)

    def body(*refs):
        outs = refs[n:2 * n]
        sems = refs[2 * n:]
        sent = []
        if ici:
            sends, recvs = _gather_copies(outs, names, *sems, base=0, stride=6, to_sibling=False)
            for cp in sends:
                cp.start()
            for cp in recvs:
                cp.wait_recv()
            sent += sends
        sends, recvs = _gather_copies(outs, names, *sems, base=3, stride=6, to_sibling=True)
        for cp in sends:
            cp.start()
        for cp in recvs:
            cp.wait_recv()
        for cp in sent + sends:
            cp.wait_send()

    arrays = [fulls[nm] for nm in names]
    out = pl.pallas_call(
        body, name=name, in_specs=[ANY] * n, out_specs=[ANY] * n,
        out_shape=[jax.ShapeDtypeStruct(a.shape, a.dtype) for a in arrays],
        input_output_aliases={i: i for i in range(n)},
        scratch_shapes=[pltpu.SemaphoreType.DMA((6 * n,)), pltpu.SemaphoreType.DMA((6 * n,))],
    )(*arrays)
    return dict(zip(names, out))


def _swap_halves(grads, *, name):
    names = list(grads)
    n = len(names)

    def body(*refs):
        copies = _swap_copies(refs[:n], refs[n:2 * n], names, *refs[2 * n:])
        for cp in copies:
            cp.start()
        for cp in copies:
            cp.wait()

    arrays = [grads[nm] for nm in names]
    out = pl.pallas_call(
        body, name=name, in_specs=[ANY] * n, out_specs=[ANY] * n,
        out_shape=_swap_out_shapes(arrays), scratch_shapes=_swap_sems(n),
    )(*arrays)
    return dict(zip(names, out))


def _swap_copies(ins, outs, names, send_sems, recv_sems):
    x, y, c = _place()
    return [pltpu.make_async_remote_copy(
        src_ref=ins[i].at[_half(nm, 1 - c)], dst_ref=outs[i], send_sem=send_sems.at[i], recv_sem=recv_sems.at[i],
        device_id=(x, y, 1 - c), device_id_type=MESH) for i, nm in enumerate(names)]


def _swap_out_shapes(arrays):
    return [jax.ShapeDtypeStruct((a.shape[0] // 2,) + a.shape[1:], a.dtype) for a in arrays]


def _swap_sems(n):
    return [pltpu.SemaphoreType.DMA((n,)), pltpu.SemaphoreType.DMA((n,))] if n else []


def _all_steps(grid, at):
    cond = None
    for axis, size in enumerate(grid):
        this = pl.program_id(axis) == (0 if at == "first" else size - 1)
        cond = this if cond is None else cond & this
    return cond


def _start_at_first_step(copies, grid):
    @pl.when(_all_steps(grid, "first"))
    def _():
        for cp in copies:
            cp.start()


def _wait_at_last_step(copies, grid):
    @pl.when(_all_steps(grid, "last"))
    def _():
        for cp in copies:
            cp.wait()


def _sum_rows(rows, cols):
    t = rows
    while t * cols * 4 > SUM_BLOCK_BYTES and t % 16 == 0:
        t //= 2
    return t


def _chip_sum(g, r1, core, *, name):
    layers, rows, cols = g.shape
    hl = layers // 2
    tr = _sum_rows(rows, cols)

    def body(core_ref, g_ref, r_ref, o_ref):
        o_ref[...] = (g_ref[...] + r_ref[...]).astype(BF16)

    return pl.pallas_call(
        body, name=name,
        grid_spec=pltpu.PrefetchScalarGridSpec(
            num_scalar_prefetch=1, grid=(hl, rows // tr),
            in_specs=[pl.BlockSpec((1, tr, cols), lambda l, i, cr: (cr[0] * hl + l, i, 0)),
                      pl.BlockSpec((1, tr, cols), lambda l, i, cr: (l, i, 0))],
            out_specs=pl.BlockSpec((1, tr, cols), lambda l, i, cr: (l, i, 0))),
        out_shape=jax.ShapeDtypeStruct((hl, rows, cols), BF16),
        compiler_params=_cparams("parallel", "parallel"),
    )(core, g, r1)


def _chip_partials(grads, names, *, tag):
    core = lax.axis_index("c").astype(jnp.int32).reshape(1)
    r1 = _swap_halves({n: grads[n] for n in names}, name=f"grad_swap_halves_{tag}")
    return r1, {n: _chip_sum(grads[n], r1[n], core, name=f"grad_chip_sum_{n}") for n in names}


def _scatter_partials(partials):
    names = list(partials)
    n = len(names)

    def body(*refs):
        copies = _scatter_copies(refs[:n], refs[n:2 * n], names, *refs[2 * n:])
        for cp in copies:
            cp.start()
        for cp in copies:
            cp.wait()

    arrays = [partials[nm] for nm in names]
    out = pl.pallas_call(
        body, name="grad_scatter_partials", in_specs=[ANY] * n, out_specs=[ANY] * n,
        out_shape=_scatter_out_shapes(names, arrays),
        scratch_shapes=[pltpu.SemaphoreType.DMA((3 * n,)), pltpu.SemaphoreType.DMA((3 * n,))],
    )(*arrays)
    return dict(zip(names, out))


def _scatter_copies(ins, outs, names, send_sems, recv_sems):
    x, y, c = _place()
    copies = []
    for i, nm in enumerate(names):
        for j, (cx, cy) in enumerate(_other_chips(x, y)):
            copies.append(pltpu.make_async_remote_copy(
                src_ref=_band(ins[i], nm, slice(None), 2 * cx + cy), dst_ref=outs[i].at[j],
                send_sem=send_sems.at[3 * i + j], recv_sem=recv_sems.at[3 * i + j],
                device_id=(cx, cy, c), device_id_type=MESH))
    return copies


def _scatter_out_shapes(names, arrays):
    return [jax.ShapeDtypeStruct((3, a.shape[0]) + _shard_shape(nm)[1:], a.dtype) for nm, a in zip(names, arrays)]


def _final_sum(g, r1, r2, place, nm, *, name):
    (layers, _, _), axis = BIG[nm]
    hl = layers // 2
    _, rows, cols = _shard_shape(nm)
    tr = _sum_rows(rows, cols)
    nrb = rows // tr
    if axis == 1:
        blk = lambda l, i, pc: (l, pc[1] * nrb + i, 0)
    else:
        blk = lambda l, i, pc: (l, i, pc[1])

    def body(place_ref, g_ref, r1_ref, r2_ref, o_ref):
        acc = g_ref[...] + r1_ref[...]
        for j in range(3):
            acc = acc + r2_ref[j].astype(F32)
        o_ref[...] = acc

    return pl.pallas_call(
        body, name=name,
        grid_spec=pltpu.PrefetchScalarGridSpec(
            num_scalar_prefetch=1, grid=(hl, nrb),
            in_specs=[pl.BlockSpec((1, tr, cols), lambda l, i, pc: blk(pc[0] * hl + l, i, pc)),
                      pl.BlockSpec((1, tr, cols), lambda l, i, pc: blk(l, i, pc)),
                      pl.BlockSpec((3, 1, tr, cols), lambda l, i, pc: (0, l, i, 0))],
            out_specs=pl.BlockSpec((1, tr, cols), lambda l, i, pc: (pc[0] * hl + l, i, 0))),
        out_shape=jax.ShapeDtypeStruct((layers, rows, cols), F32),
        compiler_params=_cparams("parallel", "parallel"),
    )(place, g, r1, r2)


def _join_halves(shards):
    names = list(shards)
    n = len(names)

    def body(*refs):
        outs = refs[n:2 * n]
        send_sems, recv_sems = refs[2 * n:]
        x, y, c = _place()
        copies = []
        for i, nm in enumerate(names):
            mine = outs[i].at[_half(nm, c)]
            cp = pltpu.make_async_remote_copy(
                src_ref=mine, dst_ref=mine, send_sem=send_sems.at[i], recv_sem=recv_sems.at[i],
                device_id=(x, y, 1 - c), device_id_type=MESH)
            cp.start()
            copies.append(cp)
        for i, nm in enumerate(names):
            theirs = outs[i].at[_half(nm, 1 - c)]
            pltpu.make_async_remote_copy(
                src_ref=theirs, dst_ref=theirs, send_sem=send_sems.at[i], recv_sem=recv_sems.at[i],
                device_id=(x, y, 1 - c), device_id_type=MESH).wait_recv()
        for cp in copies:
            cp.wait_send()

    arrays = [shards[nm] for nm in names]
    out = pl.pallas_call(
        body, name="grad_join_halves", in_specs=[ANY] * n, out_specs=[ANY] * n,
        out_shape=[jax.ShapeDtypeStruct(a.shape, a.dtype) for a in arrays],
        input_output_aliases={i: i for i in range(n)},
        scratch_shapes=[pltpu.SemaphoreType.DMA((n,)), pltpu.SemaphoreType.DMA((n,))],
    )(*arrays)
    return dict(zip(names, out))


def _all_reduce_small(part, *, name):
    rows, cols = part.shape
    vm = pl.BlockSpec(memory_space=pltpu.VMEM)

    def body(p_ref, o_ref, land, send_sems, recv_sems):
        x, y, c = _place()
        me = 4 * x + 2 * y + c
        flips = [(dx, dy, dc) for dx in (0, 1) for dy in (0, 1) for dc in (0, 1)][1:]
        copies = []
        for k, (dx, dy, dc) in enumerate(flips):
            cp = pltpu.make_async_remote_copy(
                src_ref=p_ref, dst_ref=land.at[me], send_sem=send_sems.at[k], recv_sem=recv_sems.at[k],
                device_id=(1 - x if dx else x, 1 - y if dy else y, 1 - c if dc else c), device_id_type=MESH)
            cp.start()
            copies.append(cp)
        land[me] = p_ref[...]
        for cp in copies:
            cp.wait()
        acc = land[0]
        for j in range(1, 8):
            acc = acc + land[j]
        o_ref[...] = acc

    return pl.pallas_call(
        body, name=name, in_specs=[vm], out_specs=vm,
        out_shape=jax.ShapeDtypeStruct((rows, cols), F32),
        scratch_shapes=[pltpu.VMEM((8, rows, cols), F32), pltpu.SemaphoreType.DMA((7,)), pltpu.SemaphoreType.DMA((7,))],
    )(part)


SMALL = {"g_mix": (4, 1024), "g_mlp": (4, 1024), "attn_g_q_a": (2, 256), "attn_g_kv_a": (2, 128),
         "attn_g_qnorm": (2, 192), "attn_g_knorm": (2, 192)}
SMALL_GRADS = {**SMALL, "conv_w": CONV_W}
WEIGHT_ORDER = ["g_mix", "g_mlp", "attn_w_down", "attn_g_q_a", "attn_g_kv_a", "attn_w_uq", "attn_w_ukv",
                "attn_g_qnorm", "attn_g_knorm", "attn_w_o", "conv_w_in", "conv_w", "conv_w_out", "mlp_w1", "mlp_w2"]


def _prod(shape):
    n = 1
    for v in shape:
        n *= v
    return n


def _pack_small(parts, table):
    flat = [parts[n].reshape(-1) for n in table]
    size = sum(_prod(s) for s in table.values())
    rows = -(-size // (8 * 128)) * 8
    flat.append(jnp.zeros((rows * 128 - size,), F32))
    return jnp.concatenate(flat).reshape(rows, 128)


def _unpack_small(buf, table):
    flat = buf.reshape(-1)
    out, off = {}, 0
    for n, shp in table.items():
        out[n] = flat[off:off + _prod(shp)].reshape(shp)
        off += _prod(shp)
    return out


def _rope_tables(positions):
    inv_freq = ROPE_THETA ** (-jnp.arange(0, QK_ROPE, 2, dtype=F32) / QK_ROPE)
    ang = positions.astype(F32)[:, None] * inv_freq
    cos, sin = jnp.cos(ang), jnp.sin(ang)
    z32 = jnp.zeros_like(cos)
    z64 = jnp.zeros((positions.shape[0], 64), F32)
    cc = jnp.concatenate([cos, cos, z64], axis=1)
    sa = jnp.concatenate([-sin, z32, z64], axis=1)
    sb = jnp.concatenate([z32, sin, z64], axis=1)
    return cc, sa, sb


def _pad_heads(w, width):
    k = w.shape[0]
    w = w.reshape(k, N_HEADS, width)
    return jnp.pad(w, ((0, 0), (0, 0), (0, QK_PAD - width))).reshape(k, N_HEADS * QK_PAD)


EARLY = ("mlp_w1", "mlp_w2", "conv_w_in", "conv_w_out")
LATE = ("attn_w_down", "attn_w_uq", "attn_w_ukv", "attn_w_o")
GATHER_LATER = EARLY


def _local_step(x, positions, target, wb, gains, later=None):
    s = x.shape[0]
    cc, sa, sb = _rope_tables(positions)
    pos_col = positions.reshape(s, 1)
    pos_row = positions.reshape(1, s)

    saved = []
    for i in range(4):
        j = i // 2
        g_mix = gains["g_mix"][i:i + 1]
        g_mlp = gains["g_mlp"][i:i + 1]
        if i % 2 == 0:
            w_down = jnp.pad(wb["attn_w_down"][j], ((0, 0), (0, DOWN_PAD - DOWN_DIM)))
            w_uq = _pad_heads(wb["attn_w_uq"][j], QK_DIM)
            w_ukv = wb["attn_w_ukv"][j]
            g_qa = gains["attn_g_q_a"][j:j + 1]
            g_kva = gains["attn_g_kv_a"][j:j + 1]
            g_q = jnp.pad(gains["attn_g_qnorm"][j:j + 1], ((0, 0), (0, QK_PAD - QK_DIM)))
            g_k = jnp.pad(gains["attn_g_knorm"][j:j + 1], ((0, 0), (0, QK_PAD - QK_DIM)))
            h, a = _norm_mm(x, g_mix, w_down, out_dtype=F32, name=f"mla_down_{j}")
            cq, ckv, q, k, v = _mla_prep(a, g_qa, g_kva, w_uq, w_ukv, g_q, g_k, cc, sa, sb, name=f"mla_prep_{j}")
            o, lse, got = _flash_fwd(q, k, v, pos_col, pos_row, name=f"flash_fwd_{j}",
                                     gather=later if i == 0 else None)
            if got:
                wb = {**wb, **_gather_weights(got, name="gather_later_forward", ici=False)}
            x_mid = _mm_nn(o, wb["attn_w_o"], layer=j, out_dtype=F32, residual=x, name=f"mla_out_{j}")
            mix = dict(h=h, a=a, cq=cq, ckv=ckv, q=q, k=k, v=v, o=o, lse=lse, w_down=w_down, w_uq=w_uq, w_ukv=w_ukv,
                       g_qa=g_qa, g_kva=g_kva, g_q=g_q, g_k=g_k)
        else:
            h, bcu = _norm_mm(x, g_mix, wb["conv_w_in"], layer=j, out_dtype=BF16, name=f"conv_in_{j}")
            z = _conv_gate(bcu, gains["conv_w"][j], name=f"conv_gate_{j}")
            x_mid = _mm_nn(z, wb["conv_w_out"], layer=j, out_dtype=F32, residual=x, name=f"conv_out_{j}")
            mix = dict(h=h, bcu=bcu, z=z)
        h2, u, x_out = _mlp_fwd(x_mid, g_mlp, wb["mlp_w1"], wb["mlp_w2"], layer=i, name=f"mlp_fwd_{i}")
        saved.append(dict(x_in=x, x_mid=x_mid, mix=mix, h2=h2, u=u, g_mix=g_mix, g_mlp=g_mlp))
        x = x_out

    dx, loss = _loss_head(x, target, name="loss_head")

    gw = {n: None for n in BIG}
    exchanged = None
    g_uq = [None, None]
    gs = {n: [None] * SMALL_GRADS[n][0] for n in SMALL_GRADS}

    def wgrad(nm, layer, a, b, **kw):
        out = _mm_tn(a, b, stack=gw[nm], layer=layer, layers=BIG[nm][0][0], name=f"{nm}_grad_{layer}", **kw)
        gw[nm], arrived = out if kw.get("swap") else (out, None)
        return arrived

    for i in reversed(range(4)):
        j = i // 2
        sv = saved[i]
        mix = sv["mix"]
        ride = i == 0 and later is not None
        du = _mlp_down_bwd(dx, wb["mlp_w2"], sv["u"], layer=i, name=f"mlp_down_bwd_{i}")
        wgrad("mlp_w2", i, sv["u"], dx, sqrelu_a=True)
        r1_early = wgrad("mlp_w1", i, sv["h2"], du,
                         swap={n: gw[n] for n in ("mlp_w2", "conv_w_in", "conv_w_out")} if ride else None)
        dx, dg, *arrived = _nt_rms_bwd(du, wb["mlp_w1"], sv["x_mid"], sv["g_mlp"], dx, layer=i, name=f"mlp_up_bwd_{i}",
                                       swap={"mlp_w1": gw["mlp_w1"]} if ride else None)
        if ride:
            r1_early.update(arrived[0])
        gs["g_mlp"][i] = dg[0]
        if i % 2 == 0:
            do, delta_row = _attn_out_bwd(dx, wb["attn_w_o"], mix["o"], layer=j, name=f"mla_out_bwd_{j}")
            wgrad("attn_w_o", j, mix["o"], dx)
            lse_row = mix["lse"]
            partials = None
            if ride:
                core = lax.axis_index("c").astype(jnp.int32).reshape(1)
                partials = {n: _chip_sum(gw[n], r1_early[n], core, name=f"grad_chip_sum_{n}") for n in EARLY}
            dq, dk, dv, arrived = _flash_bwd(mix["q"], mix["k"], mix["v"], do, lse_row, delta_row, pos_col, pos_row,
                                             name=f"flash_bwd_{j}", scatter=partials)
            if partials is not None:
                exchanged = (r1_early, arrived)
            dqr, dkvr, da, dgq, dgk, dgqa, dgkva = _mla_prep_bwd(
                mix["a"], mix["g_qa"], mix["g_kva"], mix["w_uq"], mix["w_ukv"], mix["g_q"], mix["g_k"], cc, sa, sb,
                dq, dk, dv, name=f"mla_prep_bwd_{j}")
            g_uq[j] = _mm_tn(mix["cq"], dqr, name=f"attn_w_uq_grad_{j}")[0]
            wgrad("attn_w_ukv", j, mix["ckv"], dkvr)
            wgrad("attn_w_down", j, mix["h"], da, keep=DOWN_DIM)
            dx, dg = _nt_rms_bwd(da, mix["w_down"], sv["x_in"], sv["g_mix"], dx, name=f"mla_down_bwd_{j}")
            gs["attn_g_qnorm"][j] = dgq[0, :QK_DIM]
            gs["attn_g_knorm"][j] = dgk[0, :QK_DIM]
            gs["attn_g_q_a"][j] = dgqa[0]
            gs["attn_g_kv_a"][j] = dgkva[0]
        else:
            dz = _mm_nt(dx, wb["conv_w_out"], layer=j, out_dtype=F32, name=f"conv_out_bwd_{j}")
            wgrad("conv_w_out", j, mix["z"], dx)
            dbcu, dcw = _conv_gate_bwd(mix["bcu"], dz, gains["conv_w"][j], name=f"conv_gate_bwd_{j}")
            gs["conv_w"][j] = dcw
            wgrad("conv_w_in", j, mix["h"], dbcu)
            dx, dg = _nt_rms_bwd(dbcu, wb["conv_w_in"], sv["x_in"], sv["g_mix"], dx, layer=j, name=f"conv_in_bwd_{j}")
        gs["g_mix"][i] = dg[0]

    gw["attn_w_uq"] = jnp.stack(g_uq).reshape(2, Q_RANK, N_HEADS, QK_PAD)[..., :QK_DIM].reshape(BIG["attn_w_uq"][0])
    grads_small = {n: jnp.stack(v) for n, v in gs.items()}
    return loss, dx, gw, grads_small, exchanged


def kernel(x, positions, g_mix, g_mlp, attn_w_down, attn_g_q_a, attn_g_kv_a, attn_w_uq, attn_w_ukv, attn_g_qnorm, attn_g_knorm, attn_w_o, conv_w_in, conv_w, conv_w_out, mlp_w1, mlp_w2, loss_target, m_g_mix, m_g_mlp, m_attn_w_down, m_attn_g_q_a, m_attn_g_kv_a, m_attn_w_uq, m_attn_w_ukv, m_attn_g_qnorm, m_attn_g_knorm, m_attn_w_o, m_conv_w_in, m_conv_w, m_conv_w_out, m_mlp_w1, m_mlp_w2, v_g_mix, v_g_mlp, v_attn_w_down, v_attn_g_q_a, v_attn_g_kv_a, v_attn_w_uq, v_attn_w_ukv, v_attn_g_qnorm, v_attn_g_knorm, v_attn_w_o, v_conv_w_in, v_conv_w, v_conv_w_out, v_mlp_w1, v_mlp_w2):
    w = dict(g_mix=g_mix, g_mlp=g_mlp, attn_w_down=attn_w_down, attn_g_q_a=attn_g_q_a, attn_g_kv_a=attn_g_kv_a,
             attn_w_uq=attn_w_uq, attn_w_ukv=attn_w_ukv, attn_g_qnorm=attn_g_qnorm, attn_g_knorm=attn_g_knorm,
             attn_w_o=attn_w_o, conv_w_in=conv_w_in, conv_w=conv_w, conv_w_out=conv_w_out, mlp_w1=mlp_w1, mlp_w2=mlp_w2)
    m = dict(g_mix=m_g_mix, g_mlp=m_g_mlp, attn_w_down=m_attn_w_down, attn_g_q_a=m_attn_g_q_a,
             attn_g_kv_a=m_attn_g_kv_a, attn_w_uq=m_attn_w_uq, attn_w_ukv=m_attn_w_ukv, attn_g_qnorm=m_attn_g_qnorm,
             attn_g_knorm=m_attn_g_knorm, attn_w_o=m_attn_w_o, conv_w_in=m_conv_w_in, conv_w=m_conv_w,
             conv_w_out=m_conv_w_out, mlp_w1=m_mlp_w1, mlp_w2=m_mlp_w2)
    v = dict(g_mix=v_g_mix, g_mlp=v_g_mlp, attn_w_down=v_attn_w_down, attn_g_q_a=v_attn_g_q_a,
             attn_g_kv_a=v_attn_g_kv_a, attn_w_uq=v_attn_w_uq, attn_w_ukv=v_attn_w_ukv, attn_g_qnorm=v_attn_g_qnorm,
             attn_g_knorm=v_attn_g_knorm, attn_w_o=v_attn_w_o, conv_w_in=v_conv_w_in, conv_w=v_conv_w,
             conv_w_out=v_conv_w_out, mlp_w1=v_mlp_w1, mlp_w2=v_mlp_w2)
    cx, cy, cc_ = _place()

    chip = 2 * cx + cy

    def own_offset(shape, axis):
        return tuple(chip * (shape[axis] // N_CHIPS) if i == axis else 0 for i in range(3))

    chip_arr = chip.astype(jnp.int32).reshape(1)
    fulls = {n: _place_own(w[n], n, chip_arr, name=f"place_{n}") for n in BIG}
    later = {n: fulls.pop(n) for n in GATHER_LATER}
    wb = _gather_weights(fulls, name="gather_weights")

    placed = lax.dynamic_update_slice(jnp.zeros(CONV_W, F32), conv_w, own_offset(CONV_W, 2))
    conv_w_full = 0.5 * _all_reduce_small(placed.reshape(-1, 128), name="conv_w_gather").reshape(CONV_W)

    gains = {n: w[n] for n in SMALL}
    gains["conv_w"] = conv_w_full

    loss, grad_x, grads_big, grads_small, (r1_early, r2_early) = _local_step(
        x[0], positions[0], loss_target[0], wb, gains, later)

    place = jnp.stack([cc_, chip]).astype(jnp.int32)
    r1_late, partials = _chip_partials(grads_big, LATE, tag="late")
    r1 = {**r1_early, **r1_late}
    r2 = {**r2_early, **_scatter_partials(partials)}
    halves = {n: _final_sum(grads_big[n], r1[n], r2[n], place, n, name=f"grad_final_sum_{n}") for n in BIG}
    grad_shards = _join_halves(halves)

    small = _unpack_small(_all_reduce_small(_pack_small(grads_small, SMALL_GRADS), name="gain_all_reduce"), SMALL_GRADS)
    grad_shards["conv_w"] = lax.dynamic_slice(small["conv_w"], own_offset(CONV_W, 2), conv_w.shape)

    loss_total = lax.psum(loss[0, 0], ("x", "y", "c"))

    grads, deltas, new_m, new_v = {}, {}, {}, {}
    for n in [*BIG, "conv_w"]:
        shp = w[n].shape
        two_d = (shp[0] * shp[1], shp[2])
        g2 = grad_shards[n].reshape(two_d)
        d, nm, nv, g = _adamw(w[n].reshape(two_d), g2, m[n].reshape(two_d), v[n].reshape(two_d), name=f"adamw_{n}")
        grads[n], deltas[n], new_m[n], new_v[n] = g.reshape(shp), d.reshape(shp), nm.reshape(shp), nv.reshape(shp)
    d, nm, nv, g = _adamw(_pack_small(w, SMALL), _pack_small(small, SMALL), _pack_small(m, SMALL),
                          _pack_small(v, SMALL), name="adamw_gains")
    d, nm, nv, g = (_unpack_small(t, SMALL) for t in (d, nm, nv, g))
    for n in SMALL:
        grads[n], deltas[n], new_m[n], new_v[n] = g[n], d[n], nm[n], nv[n]

    return (loss_total, grad_x[None],
            *[grads[n] for n in WEIGHT_ORDER], *[deltas[n] for n in WEIGHT_ORDER],
            *[new_m[n] for n in WEIGHT_ORDER], *[new_v[n] for n in WEIGHT_ORDER])
```

```python
import functools

import jax
import jax.numpy as jnp
from jax import lax
from jax.experimental import pallas as pl
from jax.experimental.pallas import tpu as pltpu

F32 = jnp.float32
BF16 = jnp.bfloat16

D_MODEL = 1024
N_HEADS = 8
QK_NOPE = 128
QK_ROPE = 64
QK_DIM = QK_NOPE + QK_ROPE
QK_PAD = 256
V_DIM = 128
Q_RANK = 256
KV_RANK = 128
DOWN_DIM = Q_RANK + KV_RANK + QK_ROPE
DOWN_PAD = 512
ROPE_THETA = 10000.0
EPS = 1e-6
NEG = -1e30
SCALE = QK_DIM ** -0.5
SCALE_LOG2E = SCALE * 1.4426950408889634
LOG2E = 1.4426950408889634
ATTN_CHAINS = 2
DIAG_CHAINS = 4

ADAM_LR = 0.001
ADAM_B1 = 0.9
ADAM_B2 = 0.999
ADAM_EPS = 1e-08
ADAM_WD = 0.01
ADAM_STEP = 10

N_CHIPS = 4
MESH = pl.DeviceIdType.MESH
ANY = pl.BlockSpec(memory_space=pl.ANY)

TM = 512
TM_NARROW = 1024
TM_WIDE = 512
FWD_TQ = 1024
FWD_TK = 1024
BWD_TQ = 1024
BWD_TK = 1024
HALO = 16
T_PREP = 1024
T_PREP_BWD = 512
T_RED = 2048
SUM_BLOCK_BYTES = 2 * 1024 * 1024


def _tile(n, pref):
    t = min(n, pref)
    assert n % t == 0, (n, t)
    return t


def _cparams(*sem):
    return pltpu.CompilerParams(dimension_semantics=sem)


def _dot(a, b):
    return jnp.dot(a, b, preferred_element_type=F32)


def _dot_nt(a, b):
    return lax.dot_general(a, b, (((1,), (1,)), ((), ())), preferred_element_type=F32)


def _dot_tn(a, b):
    return lax.dot_general(a, b, (((0,), (0,)), ((), ())), preferred_element_type=F32)


def _rms(x, width):
    r = lax.rsqrt(jnp.sum(x * x, axis=-1, keepdims=True) * (1.0 / width) + EPS)
    return x * r, r


def _rms_bwd(xhat, r, dxhat, width):
    return r * (dxhat - xhat * (jnp.sum(dxhat * xhat, axis=-1, keepdims=True) * (1.0 / width)))


def _rope(t, cc, sa, sb):
    return t * cc + pltpu.roll(t, 96, 1) * sa + pltpu.roll(t, 32, 1) * sb


def _rope_t(g, cc, sa, sb):
    return g * cc + pltpu.roll(g * sa, 32, 1) + pltpu.roll(g * sb, 96, 1)


def _wspec(w, layer):
    once = pl.Buffered(1)
    if w.ndim == 2:
        return pl.BlockSpec(w.shape, lambda *_: (0, 0), pipeline_mode=once)
    return pl.BlockSpec((None,) + w.shape[1:], lambda *_: (layer, 0, 0), pipeline_mode=once)


def _mm_nn(a, b, *, out_dtype, name, residual=None, layer=0):
    m, k = a.shape
    n = b.shape[-1]
    tm = _tile(m, TM_NARROW)

    def body(*refs):
        if residual is None:
            a_ref, b_ref, o_ref = refs
        else:
            a_ref, b_ref, r_ref, o_ref = refs
        acc = _dot(a_ref[...].astype(BF16), b_ref[...])
        if residual is not None:
            acc = acc + r_ref[...]
        o_ref[...] = acc.astype(o_ref.dtype)

    in_specs = [pl.BlockSpec((tm, k), lambda i: (i, 0)), _wspec(b, layer)]
    args = [a, b]
    if residual is not None:
        in_specs.append(pl.BlockSpec((tm, n), lambda i: (i, 0)))
        args.append(residual)
    return pl.pallas_call(
        body, name=name, grid=(m // tm,), in_specs=in_specs,
        out_specs=pl.BlockSpec((tm, n), lambda i: (i, 0)),
        out_shape=jax.ShapeDtypeStruct((m, n), out_dtype),
        compiler_params=_cparams("parallel"),
    )(*args)


def _mm_nt(a, b, *, out_dtype, name, layer=0):
    m, k = a.shape
    n = b.shape[-2]
    tm = _tile(m, TM_NARROW)

    def body(a_ref, b_ref, o_ref):
        o_ref[...] = _dot_nt(a_ref[...].astype(BF16), b_ref[...]).astype(o_ref.dtype)

    return pl.pallas_call(
        body, name=name, grid=(m // tm,),
        in_specs=[pl.BlockSpec((tm, k), lambda i: (i, 0)), _wspec(b, layer)],
        out_specs=pl.BlockSpec((tm, n), lambda i: (i, 0)),
        out_shape=jax.ShapeDtypeStruct((m, n), out_dtype),
        compiler_params=_cparams("parallel"),
    )(a, b)


def _mm_tn(a, b, *, name, stack=None, layer=0, layers=1, keep=None, sqrelu_a=False, swap=None):
    s, ka = a.shape
    n = b.shape[1]
    ts = _tile(s, T_RED)
    tka = _tile(ka, 1024)
    tn = _tile(n, 1024)
    n_out = n if keep is None else keep
    assert keep is None or tn == n
    grid = (ka // tka, n // tn, s // ts)
    names = list(swap or {})
    ns = len(names)
    n_in = 2 + (stack is not None)

    def body(*refs):
        a_ref, b_ref = refs[:2]
        o_ref = refs[n_in + ns]
        if ns:
            copies = _swap_copies(refs[n_in:n_in + ns], refs[n_in + ns + 1:n_in + 2 * ns + 1], names,
                                  *refs[n_in + 2 * ns + 1:])
            _start_at_first_step(copies, grid)

        @pl.when(pl.program_id(2) == 0)
        def _():
            o_ref[...] = jnp.zeros_like(o_ref)

        a_t = a_ref[...]
        if sqrelu_a:
            a_t = _sqrelu(a_t.astype(F32))
        o_ref[...] += _dot_tn(a_t.astype(BF16), b_ref[...].astype(BF16))[:, :n_out if keep else tn]
        if ns:
            _wait_at_last_step(copies, grid)

    in_specs = [pl.BlockSpec((ts, tka), lambda i, j, t: (t, i)), pl.BlockSpec((ts, tn), lambda i, j, t: (t, j))]
    args = [a, b]
    if stack is not None:
        in_specs.append(ANY)
        args.append(stack)
    sent = [swap[nm] for nm in names]
    out = pl.pallas_call(
        body, name=name, grid=grid, in_specs=in_specs + [ANY] * ns,
        out_specs=[pl.BlockSpec((None, tka, tn if keep is None else keep), lambda i, j, t: (layer, i, j))] + [ANY] * ns,
        out_shape=[jax.ShapeDtypeStruct((layers, ka, n_out), F32)] + _swap_out_shapes(sent),
        scratch_shapes=_swap_sems(ns),
        input_output_aliases={} if stack is None else {2: 0},
        compiler_params=_cparams(*(["arbitrary"] * 3 if ns else ["parallel", "parallel", "arbitrary"])),
    )(*args, *sent)
    return (out[0], dict(zip(names, out[1:]))) if ns else out[0]


def _norm_mm(x, g, w, *, out_dtype, name, layer=0):
    s, d = x.shape
    n = w.shape[-1]
    tm = _tile(s, TM_NARROW)

    def body(x_ref, g_ref, w_ref, h_ref, o_ref):
        xhat, _ = _rms(x_ref[...], d)
        h = (xhat * g_ref[...]).astype(BF16)
        h_ref[...] = h
        o_ref[...] = _dot(h, w_ref[...]).astype(o_ref.dtype)

    return pl.pallas_call(
        body, name=name, grid=(s // tm,),
        in_specs=[pl.BlockSpec((tm, d), lambda i: (i, 0)), pl.BlockSpec((1, d), lambda i: (0, 0)), _wspec(w, layer)],
        out_specs=[pl.BlockSpec((tm, d), lambda i: (i, 0)), pl.BlockSpec((tm, n), lambda i: (i, 0))],
        out_shape=[jax.ShapeDtypeStruct((s, d), BF16), jax.ShapeDtypeStruct((s, n), out_dtype)],
        compiler_params=_cparams("parallel"),
    )(x, g, w)


def _nt_rms_bwd(dy, w, x, g, dres, *, name, layer=0, swap=None):
    s, n = dy.shape
    d = x.shape[1]
    tm = _tile(s, TM)
    grid = (s // tm,)
    names = list(swap or {})
    ns = len(names)

    def body(dy_ref, w_ref, x_ref, g_ref, dres_ref, *rest):
        dx_ref, dg_ref = rest[ns:ns + 2]
        if ns:
            copies = _swap_copies(rest[:ns], rest[ns + 2:2 * ns + 2], names, *rest[2 * ns + 2:])
            _start_at_first_step(copies, grid)

        @pl.when(pl.program_id(0) == 0)
        def _():
            dg_ref[...] = jnp.zeros_like(dg_ref)

        dh = _dot_nt(dy_ref[...], w_ref[...])
        xhat, r = _rms(x_ref[...], d)
        dg_ref[...] += jnp.sum(dh * xhat, axis=0, keepdims=True)
        dx_ref[...] = dres_ref[...] + _rms_bwd(xhat, r, dh * g_ref[...], d)
        if ns:
            _wait_at_last_step(copies, grid)

    sent = [swap[nm] for nm in names]
    out = pl.pallas_call(
        body, name=name, grid=grid,
        in_specs=[pl.BlockSpec((tm, n), lambda i: (i, 0)), _wspec(w, layer),
                  pl.BlockSpec((tm, d), lambda i: (i, 0)), pl.BlockSpec((1, d), lambda i: (0, 0)),
                  pl.BlockSpec((tm, d), lambda i: (i, 0))] + [ANY] * ns,
        out_specs=[pl.BlockSpec((tm, d), lambda i: (i, 0)), pl.BlockSpec((1, d), lambda i: (0, 0))] + [ANY] * ns,
        out_shape=[jax.ShapeDtypeStruct((s, d), F32), jax.ShapeDtypeStruct((1, d), F32)] + _swap_out_shapes(sent),
        scratch_shapes=_swap_sems(ns),
        compiler_params=_cparams("arbitrary"),
    )(dy, w, x, g, dres, *sent)
    return (out[0], out[1], dict(zip(names, out[2:]))) if ns else (out[0], out[1])


def _sqrelu(u):
    return jnp.square(jnp.maximum(u, 0.0))


def _mlp_fwd(x, g, w1, w2, *, name, layer=0):
    s, d = x.shape
    n = w1.shape[-1]
    tm = _tile(s, TM_WIDE)

    def body(x_ref, g_ref, w1_ref, w2_ref, h_ref, u_ref, y_ref):
        x_t = x_ref[...]
        xhat, _ = _rms(x_t, d)
        h = (xhat * g_ref[...]).astype(BF16)
        h_ref[...] = h
        u = _dot(h, w1_ref[...])
        u_ref[...] = u.astype(BF16)
        y_ref[...] = x_t + _dot(_sqrelu(u).astype(BF16), w2_ref[...])

    return pl.pallas_call(
        body, name=name, grid=(s // tm,),
        in_specs=[pl.BlockSpec((tm, d), lambda i: (i, 0)), pl.BlockSpec((1, d), lambda i: (0, 0)),
                  _wspec(w1, layer), _wspec(w2, layer)],
        out_specs=[pl.BlockSpec((tm, d), lambda i: (i, 0)), pl.BlockSpec((tm, n), lambda i: (i, 0)),
                   pl.BlockSpec((tm, d), lambda i: (i, 0))],
        out_shape=[jax.ShapeDtypeStruct((s, d), BF16), jax.ShapeDtypeStruct((s, n), BF16),
                   jax.ShapeDtypeStruct((s, d), F32)],
        compiler_params=_cparams("parallel"),
    )(x, g, w1, w2)


def _mlp_down_bwd(dy, w2, u, *, name, layer=0):
    s, d = dy.shape
    n = w2.shape[-2]
    tm = _tile(s, TM_WIDE)

    def body(dy_ref, w_ref, u_ref, du_ref):
        dact = _dot_nt(dy_ref[...].astype(BF16), w_ref[...])
        du_ref[...] = (dact * (2.0 * jnp.maximum(u_ref[...].astype(F32), 0.0))).astype(BF16)

    return pl.pallas_call(
        body, name=name, grid=(s // tm,),
        in_specs=[pl.BlockSpec((tm, d), lambda i: (i, 0)), _wspec(w2, layer),
                  pl.BlockSpec((tm, n), lambda i: (i, 0))],
        out_specs=pl.BlockSpec((tm, n), lambda i: (i, 0)),
        out_shape=jax.ShapeDtypeStruct((s, n), BF16),
        compiler_params=_cparams("parallel"),
    )(dy, w2, u)


def _conv_gate(bcu, conv_w, *, name):
    s = bcu.shape[0]
    d = D_MODEL
    tm = _tile(s, TM)
    hb = tm // HALO

    def body(bcu_ref, prev_ref, w_ref, z_ref, pbuf):
        i = pl.program_id(0)
        gb = bcu_ref[:, 0:d].astype(F32)
        p = bcu_ref[:, d:2 * d].astype(F32) * bcu_ref[:, 2 * d:3 * d].astype(F32)
        pprev = prev_ref[:, d:2 * d].astype(F32) * prev_ref[:, 2 * d:3 * d].astype(F32)
        pbuf[0:HALO, :] = jnp.where(i > 0, pprev, 0.0)
        pbuf[HALO:HALO + tm, :] = p
        cv = (w_ref[2:3, :] * p + w_ref[1:2, :] * pbuf[HALO - 1:HALO - 1 + tm, :]
              + w_ref[0:1, :] * pbuf[HALO - 2:HALO - 2 + tm, :])
        z_ref[...] = (gb * cv).astype(BF16)

    return pl.pallas_call(
        body, name=name, grid=(s // tm,),
        in_specs=[pl.BlockSpec((tm, 3 * d), lambda i: (i, 0)),
                  pl.BlockSpec((HALO, 3 * d), lambda i: (jnp.maximum(i * hb - 1, 0), 0)),
                  pl.BlockSpec((3, d), lambda i: (0, 0))],
        out_specs=pl.BlockSpec((tm, d), lambda i: (i, 0)),
        out_shape=jax.ShapeDtypeStruct((s, d), BF16),
        scratch_shapes=[pltpu.VMEM((tm + HALO, d), F32)],
        compiler_params=_cparams("parallel"),
    )(bcu, bcu, conv_w)


def _conv_gate_bwd(bcu, dz, conv_w, *, name):
    s = bcu.shape[0]
    d = D_MODEL
    tm = _tile(s, TM)
    hb = tm // HALO
    nt = s // tm

    def body(bcu_ref, prev_ref, next_ref, dz_ref, dznext_ref, w_ref, dbcu_ref, dw_ref, pbuf, dbuf):
        i = pl.program_id(0)

        @pl.when(i == 0)
        def _():
            dw_ref[...] = jnp.zeros_like(dw_ref)

        gb = bcu_ref[:, 0:d].astype(F32)
        gc = bcu_ref[:, d:2 * d].astype(F32)
        uu = bcu_ref[:, 2 * d:3 * d].astype(F32)
        p = gc * uu
        pprev = prev_ref[:, d:2 * d].astype(F32) * prev_ref[:, 2 * d:3 * d].astype(F32)
        pbuf[0:HALO, :] = jnp.where(i > 0, pprev, 0.0)
        pbuf[HALO:HALO + tm, :] = p
        p1 = pbuf[HALO - 1:HALO - 1 + tm, :]
        p2 = pbuf[HALO - 2:HALO - 2 + tm, :]
        cv = w_ref[2:3, :] * p + w_ref[1:2, :] * p1 + w_ref[0:1, :] * p2
        dz_t = dz_ref[...]
        dcv = dz_t * gb
        dcv_next = dznext_ref[...] * next_ref[:, 0:d].astype(F32)
        dbuf[0:tm, :] = dcv
        dbuf[tm:tm + HALO, :] = jnp.where(i < nt - 1, dcv_next, 0.0)
        dp = w_ref[2:3, :] * dcv + w_ref[1:2, :] * dbuf[1:1 + tm, :] + w_ref[0:1, :] * dbuf[2:2 + tm, :]
        dw_ref[2:3, :] += jnp.sum(dcv * p, axis=0, keepdims=True)
        dw_ref[1:2, :] += jnp.sum(dcv * p1, axis=0, keepdims=True)
        dw_ref[0:1, :] += jnp.sum(dcv * p2, axis=0, keepdims=True)
        dbcu_ref[:, 0:d] = (dz_t * cv).astype(BF16)
        dbcu_ref[:, d:2 * d] = (dp * uu).astype(BF16)
        dbcu_ref[:, 2 * d:3 * d] = (dp * gc).astype(BF16)

    nxt = lambda i: (jnp.minimum((i + 1) * hb, s // HALO - 1), 0)
    return pl.pallas_call(
        body, name=name, grid=(nt,),
        in_specs=[pl.BlockSpec((tm, 3 * d), lambda i: (i, 0)),
                  pl.BlockSpec((HALO, 3 * d), lambda i: (jnp.maximum(i * hb - 1, 0), 0)),
                  pl.BlockSpec((HALO, 3 * d), nxt),
                  pl.BlockSpec((tm, d), lambda i: (i, 0)),
                  pl.BlockSpec((HALO, d), nxt),
                  pl.BlockSpec((3, d), lambda i: (0, 0))],
        out_specs=[pl.BlockSpec((tm, 3 * d), lambda i: (i, 0)), pl.BlockSpec((3, d), lambda i: (0, 0))],
        out_shape=[jax.ShapeDtypeStruct((s, 3 * d), BF16), jax.ShapeDtypeStruct((3, d), F32)],
        scratch_shapes=[pltpu.VMEM((tm + HALO, d), F32), pltpu.VMEM((tm + HALO, d), F32)],
        compiler_params=_cparams("arbitrary"),
    )(bcu, bcu, bcu, dz, dz, conv_w)


def _mla_prep(a, g_qa, g_kva, w_uq, w_ukv, g_q, g_k, cc, sa, sb, *, name):
    s = a.shape[0]
    ts = _tile(s, T_PREP)

    def body(a_ref, gqa_ref, gkva_ref, wuq_ref, wukv_ref, gq_ref, gk_ref, cc_ref, sa_ref, sb_ref,
             cq_ref, ckv_ref, q_ref, k_ref, v_ref):
        xq, _ = _rms(a_ref[:, 0:Q_RANK], Q_RANK)
        cq = (xq * gqa_ref[...]).astype(BF16)
        cq_ref[...] = cq
        xkv, _ = _rms(a_ref[:, Q_RANK:Q_RANK + KV_RANK], KV_RANK)
        ckv = (xkv * gkva_ref[...]).astype(BF16)
        ckv_ref[...] = ckv
        kpe = a_ref[:, Q_RANK + KV_RANK:DOWN_PAD]
        kpe_ss = jnp.sum(kpe * kpe, axis=-1, keepdims=True)
        cc_t, sa_t, sb_t = cc_ref[...], sa_ref[...], sb_ref[...]
        gq = gq_ref[...]
        gk = gk_ref[...]
        for h in range(N_HEADS):
            cols = slice(h * QK_PAD, (h + 1) * QK_PAD)
            qhat, _ = _rms(_dot(cq, wuq_ref[:, cols]), QK_DIM)
            qn = qhat * (gq * SCALE_LOG2E)
            q_ref[h, :, 0:QK_NOPE] = qn[:, 0:QK_NOPE].astype(BF16)
            q_ref[h, :, QK_NOPE:QK_PAD] = _rope(qn[:, QK_NOPE:QK_PAD], cc_t, sa_t, sb_t).astype(BF16)
            kvr = _dot(ckv, wukv_ref[:, cols])
            kn = kvr[:, 0:QK_NOPE]
            rk = lax.rsqrt((jnp.sum(kn * kn, axis=-1, keepdims=True) + kpe_ss) * (1.0 / QK_DIM) + EPS)
            k_ref[h, :, 0:QK_NOPE] = (kn * rk * gk[:, 0:QK_NOPE]).astype(BF16)
            k_ref[h, :, QK_NOPE:QK_PAD] = _rope(kpe * rk * gk[:, QK_NOPE:QK_PAD], cc_t, sa_t, sb_t).astype(BF16)
            v_ref[h, :, 0:V_DIM] = kvr[:, QK_NOPE:QK_PAD].astype(BF16)
            v_ref[h, :, V_DIM:2 * V_DIM] = jnp.ones((ts, V_DIM), BF16)

    row = lambda i: (i, 0)
    fixed = lambda i: (0, 0)
    head = lambda i: (0, i, 0)
    return pl.pallas_call(
        body, name=name, grid=(s // ts,),
        in_specs=[pl.BlockSpec((ts, DOWN_PAD), row), pl.BlockSpec((1, Q_RANK), fixed), pl.BlockSpec((1, KV_RANK), fixed),
                  pl.BlockSpec((Q_RANK, N_HEADS * QK_PAD), fixed), pl.BlockSpec((KV_RANK, N_HEADS * QK_PAD), fixed),
                  pl.BlockSpec((1, QK_PAD), fixed), pl.BlockSpec((1, QK_PAD), fixed),
                  pl.BlockSpec((ts, 128), row), pl.BlockSpec((ts, 128), row), pl.BlockSpec((ts, 128), row)],
        out_specs=[pl.BlockSpec((ts, Q_RANK), row), pl.BlockSpec((ts, KV_RANK), row),
                   pl.BlockSpec((N_HEADS, ts, QK_PAD), head), pl.BlockSpec((N_HEADS, ts, QK_PAD), head),
                   pl.BlockSpec((N_HEADS, ts, 2 * V_DIM), head)],
        out_shape=[jax.ShapeDtypeStruct((s, Q_RANK), BF16), jax.ShapeDtypeStruct((s, KV_RANK), BF16),
                   jax.ShapeDtypeStruct((N_HEADS, s, QK_PAD), BF16), jax.ShapeDtypeStruct((N_HEADS, s, QK_PAD), BF16),
                   jax.ShapeDtypeStruct((N_HEADS, s, 2 * V_DIM), BF16)],
        compiler_params=_cparams("parallel"),
    )(a, g_qa, g_kva, w_uq, w_ukv, g_q, g_k, cc, sa, sb)


def _mla_prep_bwd(a, g_qa, g_kva, w_uq, w_ukv, g_q, g_k, cc, sa, sb, dq, dk, dv, *, name):
    s = a.shape[0]
    ts = _tile(s, T_PREP_BWD)

    def body(a_ref, gqa_ref, gkva_ref, wuq_ref, wukv_ref, gq_ref, gk_ref, cc_ref, sa_ref, sb_ref,
             dq_ref, dk_ref, dv_ref, dqr_ref, dkvr_ref, da_ref, dgq_ref, dgk_ref, dgqa_ref, dgkva_ref):
        @pl.when(pl.program_id(0) == 0)
        def _():
            dgq_ref[...] = jnp.zeros_like(dgq_ref)
            dgk_ref[...] = jnp.zeros_like(dgk_ref)
            dgqa_ref[...] = jnp.zeros_like(dgqa_ref)
            dgkva_ref[...] = jnp.zeros_like(dgkva_ref)

        xq, r_q = _rms(a_ref[:, 0:Q_RANK], Q_RANK)
        cq = (xq * gqa_ref[...]).astype(BF16)
        xkv, r_kv = _rms(a_ref[:, Q_RANK:Q_RANK + KV_RANK], KV_RANK)
        ckv = (xkv * gkva_ref[...]).astype(BF16)
        kpe = a_ref[:, Q_RANK + KV_RANK:DOWN_PAD]
        kpe_ss = jnp.sum(kpe * kpe, axis=-1, keepdims=True)
        cc_t, sa_t, sb_t = cc_ref[...], sa_ref[...], sb_ref[...]
        gq = gq_ref[...]
        gk = gk_ref[...]
        dcq = jnp.zeros((ts, Q_RANK), F32)
        dckv = jnp.zeros((ts, KV_RANK), F32)
        dkpe = jnp.zeros((ts, 128), F32)
        dgq = jnp.zeros((1, QK_PAD), F32)
        dgk_n = jnp.zeros((1, QK_NOPE), F32)
        dgk_p = jnp.zeros((1, 128), F32)
        for h in range(N_HEADS):
            cols = slice(h * QK_PAD, (h + 1) * QK_PAD)
            qhat, rq = _rms(_dot(cq, wuq_ref[:, cols]), QK_DIM)
            dqn = jnp.concatenate(
                [dq_ref[h, :, 0:QK_NOPE], _rope_t(dq_ref[h, :, QK_NOPE:QK_PAD], cc_t, sa_t, sb_t)], axis=1)
            dgq = dgq + jnp.sum(dqn * qhat, axis=0, keepdims=True)
            dqr = _rms_bwd(qhat, rq, dqn * gq, QK_DIM).astype(BF16)
            dqr_ref[:, cols] = dqr
            dcq = dcq + _dot_nt(dqr, wuq_ref[:, cols])
            kn = _dot(ckv, wukv_ref[:, h * QK_PAD:h * QK_PAD + QK_NOPE])
            rk = lax.rsqrt((jnp.sum(kn * kn, axis=-1, keepdims=True) + kpe_ss) * (1.0 / QK_DIM) + EPS)
            khat_n = kn * rk
            khat_p = kpe * rk
            dkn = dk_ref[h, :, 0:QK_NOPE]
            dkp = _rope_t(dk_ref[h, :, QK_NOPE:QK_PAD], cc_t, sa_t, sb_t)
            dgk_n = dgk_n + jnp.sum(dkn * khat_n, axis=0, keepdims=True)
            dgk_p = dgk_p + jnp.sum(dkp * khat_p, axis=0, keepdims=True)
            dxn = dkn * gk[:, 0:QK_NOPE]
            dxp = dkp * gk[:, QK_NOPE:QK_PAD]
            mean = (jnp.sum(dxn * khat_n, axis=-1, keepdims=True)
                    + jnp.sum(dxp * khat_p, axis=-1, keepdims=True)) * (1.0 / QK_DIM)
            dkpe = dkpe + rk * (dxp - khat_p * mean)
            dkvr = jnp.concatenate([rk * (dxn - khat_n * mean), dv_ref[h, :, :]], axis=1).astype(BF16)
            dkvr_ref[:, cols] = dkvr
            dckv = dckv + _dot_nt(dkvr, wukv_ref[:, cols])
        dgq_ref[...] += dgq
        dgk_ref[:, 0:QK_NOPE] += dgk_n
        dgk_ref[:, QK_NOPE:QK_PAD] += dgk_p
        dgqa_ref[...] += jnp.sum(dcq * xq, axis=0, keepdims=True)
        dgkva_ref[...] += jnp.sum(dckv * xkv, axis=0, keepdims=True)
        da_ref[:, 0:Q_RANK] = _rms_bwd(xq, r_q, dcq * gqa_ref[...], Q_RANK).astype(BF16)
        da_ref[:, Q_RANK:Q_RANK + KV_RANK] = _rms_bwd(xkv, r_kv, dckv * gkva_ref[...], KV_RANK).astype(BF16)
        da_ref[:, Q_RANK + KV_RANK:DOWN_PAD] = dkpe.astype(BF16)

    row = lambda i: (i, 0)
    fixed = lambda i: (0, 0)
    head = lambda i: (0, i, 0)
    wide = N_HEADS * QK_PAD
    return pl.pallas_call(
        body, name=name, grid=(s // ts,),
        in_specs=[pl.BlockSpec((ts, DOWN_PAD), row), pl.BlockSpec((1, Q_RANK), fixed), pl.BlockSpec((1, KV_RANK), fixed),
                  pl.BlockSpec((Q_RANK, wide), fixed), pl.BlockSpec((KV_RANK, wide), fixed),
                  pl.BlockSpec((1, QK_PAD), fixed), pl.BlockSpec((1, QK_PAD), fixed),
                  pl.BlockSpec((ts, 128), row), pl.BlockSpec((ts, 128), row), pl.BlockSpec((ts, 128), row),
                  pl.BlockSpec((N_HEADS, ts, QK_PAD), head), pl.BlockSpec((N_HEADS, ts, QK_PAD), head),
                  pl.BlockSpec((N_HEADS, ts, V_DIM), head)],
        out_specs=[pl.BlockSpec((ts, wide), row), pl.BlockSpec((ts, wide), row), pl.BlockSpec((ts, DOWN_PAD), row),
                   pl.BlockSpec((1, QK_PAD), fixed), pl.BlockSpec((1, QK_PAD), fixed),
                   pl.BlockSpec((1, Q_RANK), fixed), pl.BlockSpec((1, KV_RANK), fixed)],
        out_shape=[jax.ShapeDtypeStruct((s, wide), BF16), jax.ShapeDtypeStruct((s, wide), BF16),
                   jax.ShapeDtypeStruct((s, DOWN_PAD), BF16),
                   jax.ShapeDtypeStruct((1, QK_PAD), F32), jax.ShapeDtypeStruct((1, QK_PAD), F32),
                   jax.ShapeDtypeStruct((1, Q_RANK), F32), jax.ShapeDtypeStruct((1, KV_RANK), F32)],
        compiler_params=_cparams("arbitrary"),
    )(a, g_qa, g_kva, w_uq, w_ukv, g_q, g_k, cc, sa, sb, dq, dk, dv)


def _flash_fwd(q, k, v, pos_col, pos_row, *, name, gather=None):
    nh, s, _ = q.shape
    tq = _tile(s, FWD_TQ)
    tk = _tile(s, FWD_TK)
    sq = tq // ATTN_CHAINS
    nq = s // tq
    names = list(gather or {})
    ng = len(names)

    def body(q_ref, k_ref, v_ref, pq_ref, pk_ref, *rest):
        o_ref, lse_ref = rest[ng:ng + 2]
        m_sc, acc_sc = rest[2 * ng + 2:2 * ng + 4]
        qb = pl.program_id(1)
        if ng:
            sends, recvs = _gather_ici_copies(rest[ng + 2:2 * ng + 2], names, *rest[2 * ng + 4:], base=0, stride=3)

            @pl.when((pl.program_id(0) == 0) & (qb == 0))
            def _():
                for cp in sends:
                    cp.start()

        m_sc[...] = jnp.full_like(m_sc, NEG)
        acc_sc[...] = jnp.zeros_like(acc_sc)

        def step(kb, masked):
            trim = masked and tq == tk
            chains = DIAG_CHAINS if trim else ATTN_CHAINS
            sq = tq // chains
            start = pl.multiple_of(kb * tk, tk)
            widths = [(u + 1) * sq if trim else tk for u in range(chains)]
            scores = [_dot_nt(q_ref[0, u * sq:(u + 1) * sq, :], k_ref[0, pl.ds(start, widths[u]), :])
                      for u in range(chains)]
            for u in range(chains):
                rows = slice(u * sq, (u + 1) * sq)
                keys = pl.ds(start, widths[u])
                sc = scores[u]
                if masked:
                    sc = jnp.where(pq_ref[rows, :] >= pk_ref[:, keys], sc, NEG)
                m_prev = m_sc[rows, :]
                m_new = jnp.maximum(m_prev, jnp.max(sc, axis=-1, keepdims=True))
                alpha = jnp.exp2(m_prev - m_new)
                p = jnp.exp2(sc - jnp.tile(m_new, (1, widths[u] // 128)))
                acc_sc[rows, :] = (jnp.tile(alpha, (1, 2)) * acc_sc[rows, :]
                                   + _dot(p.astype(BF16), v_ref[0, keys, :]))
                m_sc[rows, :] = m_new

        n_before = (qb * tq) // tk
        n_seen = (qb * tq + tq - 1) // tk + 1
        lax.fori_loop(0, n_before, lambda kb, c: (step(kb, False), c)[1], 0)
        lax.fori_loop(n_before, n_seen, lambda kb, c: (step(kb, True), c)[1], 0)
        l = acc_sc[:, V_DIM:2 * V_DIM]
        o_ref[...] = (acc_sc[:, 0:V_DIM] / l).astype(BF16)
        lse = m_sc[...] * (1.0 / LOG2E) + jnp.log(l)
        lse_ref[0] = lse.T[0:1, :]

        if ng:
            @pl.when((pl.program_id(0) == nh - 1) & (qb == nq - 1))
            def _():
                for cp in recvs:
                    cp.wait_recv()
                for cp in sends:
                    cp.wait_send()

    arrays = [gather[nm] for nm in names]
    out = pl.pallas_call(
        body, name=name, grid=(nh, nq),
        in_specs=[pl.BlockSpec((1, tq, QK_PAD), lambda h, qb: (h, qb, 0)),
                  pl.BlockSpec((1, s, QK_PAD), lambda h, qb: (h, 0, 0)),
                  pl.BlockSpec((1, s, 2 * V_DIM), lambda h, qb: (h, 0, 0)),
                  pl.BlockSpec((tq, 1), lambda h, qb: (qb, 0)),
                  pl.BlockSpec((1, s), lambda h, qb: (0, 0))] + [ANY] * ng,
        out_specs=[pl.BlockSpec((tq, V_DIM), lambda h, qb: (qb, h)),
                   pl.BlockSpec((1, 1, tq), lambda h, qb: (h, 0, qb))] + [ANY] * ng,
        scratch_shapes=[pltpu.VMEM((tq, 128), F32), pltpu.VMEM((tq, 2 * V_DIM), F32)]
        + ([pltpu.SemaphoreType.DMA((3 * ng,)), pltpu.SemaphoreType.DMA((3 * ng,))] if ng else []),
        out_shape=[jax.ShapeDtypeStruct((s, nh * V_DIM), BF16), jax.ShapeDtypeStruct((nh, 1, s), F32)]
        + [jax.ShapeDtypeStruct(a.shape, a.dtype) for a in arrays],
        input_output_aliases={5 + i: 2 + i for i in range(ng)},
        compiler_params=_cparams("arbitrary", "arbitrary") if ng else _cparams("parallel", "parallel"),
    )(q, k, v, pos_col, pos_row, *arrays)
    return out[0], out[1], dict(zip(names, out[2:]))


def _attn_out_bwd(dy, w_o, o, *, name, layer=0):
    s, d = dy.shape
    n = w_o.shape[-2]
    tm = _tile(s, TM)

    def body(dy_ref, w_ref, o_ref, do_ref, d_ref):
        do = _dot_nt(dy_ref[...].astype(BF16), w_ref[...]).astype(BF16)
        do_ref[...] = do
        for h in range(N_HEADS):
            cols = slice(h * V_DIM, (h + 1) * V_DIM)
            prod = do[:, cols].astype(F32) * o_ref[:, cols].astype(F32)
            d_ref[h] = jnp.sum(prod.T, axis=0, keepdims=True)

    return pl.pallas_call(
        body, name=name, grid=(s // tm,),
        in_specs=[pl.BlockSpec((tm, d), lambda i: (i, 0)), _wspec(w_o, layer), pl.BlockSpec((tm, n), lambda i: (i, 0))],
        out_specs=[pl.BlockSpec((tm, n), lambda i: (i, 0)), pl.BlockSpec((N_HEADS, 1, tm), lambda i: (0, 0, i))],
        out_shape=[jax.ShapeDtypeStruct((s, n), BF16), jax.ShapeDtypeStruct((N_HEADS, 1, s), F32)],
        compiler_params=_cparams("parallel"),
    )(dy, w_o, o)


def _flash_bwd(q, k, v, do, lse_row, delta_row, pos_col, pos_row, *, name, scatter=None):
    nh, s, _ = q.shape
    tq = _tile(s, BWD_TQ)
    tk = _tile(s, BWD_TK)
    nq, nk = s // tq, s // tk
    sk = tk // ATTN_CHAINS
    names = list(scatter or {})
    ng = len(names)

    def body(q_ref, k_ref, v_ref, do_ref, lse_ref, delta_ref, pq_ref, pk_ref, *rest):
        dq_ref, dk_ref, dv_ref = rest[ng:ng + 3]
        dk_sc, dv_sc = rest[2 * ng + 3:2 * ng + 5]
        kb = pl.program_id(1)
        if ng:
            copies = _scatter_copies(rest[:ng], rest[ng + 3:2 * ng + 3], names, *rest[2 * ng + 5:])

            @pl.when((pl.program_id(0) == 0) & (kb == 0))
            def _():
                for cp in copies:
                    cp.start()

        @pl.when(kb == 0)
        def _():
            dq_ref[...] = jnp.zeros_like(dq_ref)

        dk_sc[...] = jnp.zeros_like(dk_sc)
        dv_sc[...] = jnp.zeros_like(dv_sc)

        def step(qb, masked):
            trim = masked and tq == tk
            chains = DIAG_CHAINS if trim else ATTN_CHAINS
            sk = tk // chains
            start = pl.multiple_of(qb * tq, tq)
            offs = [u * sk if trim else 0 for u in range(chains)]
            qss = [pl.ds(start + offs[u], tq - offs[u]) for u in range(chains)]
            qts = [q_ref[0, qss[u], :] for u in range(chains)]
            dots = [do_ref[qss[u], :] for u in range(chains)]
            sts = [_dot_nt(k_ref[0, u * sk:(u + 1) * sk, :], qts[u]) for u in range(chains)]
            dpts = [_dot_nt(v_ref[0, u * sk:(u + 1) * sk, :], dots[u]) for u in range(chains)]
            parts = []
            for u in range(chains):
                rows = slice(u * sk, (u + 1) * sk)
                pt = jnp.exp2(sts[u] - lse_ref[0, :, qss[u]] * LOG2E)
                if masked:
                    pt = jnp.where(pq_ref[:, qss[u]] >= pk_ref[rows, :], pt, 0.0)
                dv_sc[rows, :] += _dot(pt.astype(BF16), dots[u])
                dst = (pt * (dpts[u] - delta_ref[0, :, qss[u]])).astype(BF16)
                dk_sc[rows, :] += _dot(dst, qts[u])
                parts.append(_dot_tn(dst, k_ref[0, rows, :]))
            if trim:
                for u in range(chains):
                    dq_ref[0, qss[u], :] += parts[u]
            else:
                dq_ref[0, qss[0], :] += functools.reduce(lambda a, b: a + b, parts)

        q_first = (kb * tk) // tq
        q_clear = (kb * tk + tk - 1) // tq + 1
        lax.fori_loop(q_first, q_clear, lambda qb, c: (step(qb, True), c)[1], 0)
        lax.fori_loop(q_clear, nq, lambda qb, c: (step(qb, False), c)[1], 0)
        dk_ref[0] = dk_sc[...] * (1.0 / LOG2E)
        dv_ref[0] = dv_sc[...]

        @pl.when(kb == nk - 1)
        def _():
            dq_ref[...] = dq_ref[...] * SCALE

        if ng:
            @pl.when((pl.program_id(0) == nh - 1) & (kb == nk - 1))
            def _():
                for cp in copies:
                    cp.wait()

    arrays = [scatter[nm] for nm in names]
    out = pl.pallas_call(
        body, name=name, grid=(nh, nk),
        in_specs=[pl.BlockSpec((1, s, QK_PAD), lambda h, kb: (h, 0, 0)),
                  pl.BlockSpec((1, tk, QK_PAD), lambda h, kb: (h, kb, 0)),
                  pl.BlockSpec((1, tk, V_DIM), lambda h, kb: (h, kb, 0)),
                  pl.BlockSpec((s, V_DIM), lambda h, kb: (0, h)),
                  pl.BlockSpec((1, 1, s), lambda h, kb: (h, 0, 0)),
                  pl.BlockSpec((1, 1, s), lambda h, kb: (h, 0, 0)),
                  pl.BlockSpec((1, s), lambda h, kb: (0, 0)),
                  pl.BlockSpec((tk, 1), lambda h, kb: (kb, 0))] + [ANY] * ng,
        out_specs=[pl.BlockSpec((1, s, QK_PAD), lambda h, kb: (h, 0, 0)),
                   pl.BlockSpec((1, tk, QK_PAD), lambda h, kb: (h, kb, 0)),
                   pl.BlockSpec((1, tk, V_DIM), lambda h, kb: (h, kb, 0))] + [ANY] * ng,
        scratch_shapes=[pltpu.VMEM((tk, QK_PAD), F32), pltpu.VMEM((tk, V_DIM), F32)]
        + ([pltpu.SemaphoreType.DMA((3 * ng,)), pltpu.SemaphoreType.DMA((3 * ng,))] if ng else []),
        out_shape=[jax.ShapeDtypeStruct((nh, s, QK_PAD), F32), jax.ShapeDtypeStruct((nh, s, QK_PAD), F32),
                   jax.ShapeDtypeStruct((nh, s, V_DIM), F32)] + _scatter_out_shapes(names, arrays),
        compiler_params=_cparams("arbitrary", "arbitrary"),
    )(q, k, v, do, lse_row, delta_row, pos_row, pos_col, *arrays)
    return out[0], out[1], out[2], dict(zip(names, out[3:]))


def _loss_head(y, target, *, name):
    s, d = y.shape
    tm = _tile(s, TM)
    nt = s // tm

    def body(y_ref, t_ref, dy_ref, loss_ref, acc):
        i = pl.program_id(0)

        @pl.when(i == 0)
        def _():
            acc[...] = jnp.zeros_like(acc)

        e = y_ref[...] - t_ref[...]
        dy_ref[...] = e * (1.0 / d)
        acc[...] += jnp.sum((e * e).reshape(tm // 8, 8, d), axis=0)

        @pl.when(i == nt - 1)
        def _():
            loss_ref[...] = jnp.full((1, 128), 0.5 / d, F32) * jnp.sum(acc[...])

    return pl.pallas_call(
        body, name=name, grid=(nt,),
        in_specs=[pl.BlockSpec((tm, d), lambda i: (i, 0))] * 2,
        out_specs=[pl.BlockSpec((tm, d), lambda i: (i, 0)), pl.BlockSpec((1, 128), lambda i: (0, 0))],
        out_shape=[jax.ShapeDtypeStruct((s, d), F32), jax.ShapeDtypeStruct((1, 128), F32)],
        scratch_shapes=[pltpu.VMEM((8, d), F32)],
        compiler_params=_cparams("arbitrary"),
    )(y, target)


def _adamw(w, g, m, v, *, name):
    r, c = w.shape
    tr = _tile(r, 512) if r % 8 == 0 else r

    def body(w_ref, g_ref, m_ref, v_ref, d_ref, nm_ref, nv_ref, go_ref):
        g_t = g_ref[...]
        go_ref[...] = g_t
        nm = ADAM_B1 * m_ref[...] + (1.0 - ADAM_B1) * g_t
        nv = ADAM_B2 * v_ref[...] + (1.0 - ADAM_B2) * (g_t * g_t)
        m_hat = nm / (1.0 - ADAM_B1 ** ADAM_STEP)
        v_hat = nv / (1.0 - ADAM_B2 ** ADAM_STEP)
        d_ref[...] = -ADAM_LR * (m_hat / (jnp.sqrt(v_hat) + ADAM_EPS) + ADAM_WD * w_ref[...])
        nm_ref[...] = nm
        nv_ref[...] = nv

    spec = pl.BlockSpec((tr, c), lambda i: (i, 0))
    return pl.pallas_call(
        body, name=name, grid=(r // tr,), in_specs=[spec] * 4, out_specs=[spec] * 4,
        out_shape=[jax.ShapeDtypeStruct((r, c), F32)] * 4,
        compiler_params=_cparams("parallel"),
    )(w, g, m, v)


def _place():
    return lax.axis_index("x"), lax.axis_index("y"), lax.axis_index("c")


def _other_chips(x, y):
    return [(1 - x, y), (x, 1 - y), (1 - x, 1 - y)]


BIG = {
    "attn_w_down": ((2, 1024, 448), 1), "attn_w_uq": ((2, 256, 1536), 2), "attn_w_ukv": ((2, 128, 2048), 2),
    "attn_w_o": ((2, 1024, 1024), 1), "conv_w_in": ((2, 1024, 3072), 2),
    "conv_w_out": ((2, 1024, 1024), 1), "mlp_w1": ((4, 1024, 4096), 2), "mlp_w2": ((4, 4096, 1024), 1),
}
CONV_W = (2, 3, 1024)


def _shard_shape(name):
    shape, axis = BIG[name]
    return tuple(n // N_CHIPS if i == axis else n for i, n in enumerate(shape))


def _band(ref, name, layers, chip):
    shape, axis = BIG[name]
    width = shape[axis] // N_CHIPS
    if axis == 1:
        return ref.at[layers, pl.ds(chip * width, width), :]
    return ref.at[layers, :, pl.ds(chip * width, width)]


def _half(name, c):
    hl = BIG[name][0][0] // 2
    return pl.ds(c * hl, hl)


def _place_own(w, nm, chip, *, name):
    shape, axis = BIG[nm]
    layers, rows, cols = w.shape
    tr = _sum_rows(rows, cols)
    nrb = rows // tr
    if axis == 1:
        band = lambda l, i, ch: (l, ch[0] * nrb + i, 0)
    else:
        band = lambda l, i, ch: (l, i, ch[0])

    def body(chip_ref, w_ref, o_ref):
        o_ref[...] = w_ref[...].astype(BF16)

    return pl.pallas_call(
        body, name=name,
        grid_spec=pltpu.PrefetchScalarGridSpec(
            num_scalar_prefetch=1, grid=(layers, nrb),
            in_specs=[pl.BlockSpec((1, tr, cols), lambda l, i, ch: (l, i, 0))],
            out_specs=pl.BlockSpec((1, tr, cols), band)),
        out_shape=jax.ShapeDtypeStruct(shape, BF16),
        compiler_params=_cparams("parallel", "parallel"),
    )(chip, w)


def _gather_copies(outs, names, send_sems, recv_sems, *, base, stride, to_sibling):
    x, y, c = _place()
    me = 2 * x + y

    def copy(k, ref, nm, layers, chip, to):
        band = _band(ref, nm, layers, chip)
        return pltpu.make_async_remote_copy(
            src_ref=band, dst_ref=band, send_sem=send_sems.at[k], recv_sem=recv_sems.at[k],
            device_id=to, device_id_type=MESH)

    sends, recvs = [], []
    for i, nm in enumerate(names):
        for j, (cx, cy) in enumerate(_other_chips(x, y)):
            k = base + stride * i + j
            if to_sibling:
                sends.append(copy(k, outs[i], nm, _half(nm, c), 2 * cx + cy, (x, y, 1 - c)))
                recvs.append(copy(k, outs[i], nm, _half(nm, 1 - c), 2 * cx + cy, (x, y, c)))
            else:
                sends.append(copy(k, outs[i], nm, _half(nm, c), me, (cx, cy, c)))
                recvs.append(copy(k, outs[i], nm, _half(nm, c), 2 * cx + cy, (x, y, c)))
    return sends, recvs


def _gather_ici_copies(outs, names, send_sems, recv_sems, *, base, stride):
    return _gather_copies(outs, names, send_sems, recv_sems, base=base, stride=stride, to_sibling=False)


def _gather_weights(fulls, *, name, ici=True):
    names = list(fulls)
    n = len(names)

    def body(*refs):
        outs = refs[n:2 * n]
        sems = refs[2 * n:]
        sent = []
        if ici:
            sends, recvs = _gather_copies(outs, names, *sems, base=0, stride=6, to_sibling=False)
            for cp in sends:
                cp.start()
            for cp in recvs:
                cp.wait_recv()
            sent += sends
        sends, recvs = _gather_copies(outs, names, *sems, base=3, stride=6, to_sibling=True)
        for cp in sends:
            cp.start()
        for cp in recvs:
            cp.wait_recv()
        for cp in sent + sends:
            cp.wait_send()

    arrays = [fulls[nm] for nm in names]
    out = pl.pallas_call(
        body, name=name, in_specs=[ANY] * n, out_specs=[ANY] * n,
        out_shape=[jax.ShapeDtypeStruct(a.shape, a.dtype) for a in arrays],
        input_output_aliases={i: i for i in range(n)},
        scratch_shapes=[pltpu.SemaphoreType.DMA((6 * n,)), pltpu.SemaphoreType.DMA((6 * n,))],
    )(*arrays)
    return dict(zip(names, out))


def _swap_halves(grads, *, name):
    names = list(grads)
    n = len(names)

    def body(*refs):
        copies = _swap_copies(refs[:n], refs[n:2 * n], names, *refs[2 * n:])
        for cp in copies:
            cp.start()
        for cp in copies:
            cp.wait()

    arrays = [grads[nm] for nm in names]
    out = pl.pallas_call(
        body, name=name, in_specs=[ANY] * n, out_specs=[ANY] * n,
        out_shape=_swap_out_shapes(arrays), scratch_shapes=_swap_sems(n),
    )(*arrays)
    return dict(zip(names, out))


def _swap_copies(ins, outs, names, send_sems, recv_sems):
    x, y, c = _place()
    return [pltpu.make_async_remote_copy(
        src_ref=ins[i].at[_half(nm, 1 - c)], dst_ref=outs[i], send_sem=send_sems.at[i], recv_sem=recv_sems.at[i],
        device_id=(x, y, 1 - c), device_id_type=MESH) for i, nm in enumerate(names)]


def _swap_out_shapes(arrays):
    return [jax.ShapeDtypeStruct((a.shape[0] // 2,) + a.shape[1:], a.dtype) for a in arrays]


def _swap_sems(n):
    return [pltpu.SemaphoreType.DMA((n,)), pltpu.SemaphoreType.DMA((n,))] if n else []


def _all_steps(grid, at):
    cond = None
    for axis, size in enumerate(grid):
        this = pl.program_id(axis) == (0 if at == "first" else size - 1)
        cond = this if cond is None else cond & this
    return cond


def _start_at_first_step(copies, grid):
    @pl.when(_all_steps(grid, "first"))
    def _():
        for cp in copies:
            cp.start()


def _wait_at_last_step(copies, grid):
    @pl.when(_all_steps(grid, "last"))
    def _():
        for cp in copies:
            cp.wait()


def _sum_rows(rows, cols):
    t = rows
    while t * cols * 4 > SUM_BLOCK_BYTES and t % 16 == 0:
        t //= 2
    return t


def _chip_sum(g, r1, core, *, name):
    layers, rows, cols = g.shape
    hl = layers // 2
    tr = _sum_rows(rows, cols)

    def body(core_ref, g_ref, r_ref, o_ref):
        o_ref[...] = (g_ref[...] + r_ref[...]).astype(BF16)

    return pl.pallas_call(
        body, name=name,
        grid_spec=pltpu.PrefetchScalarGridSpec(
            num_scalar_prefetch=1, grid=(hl, rows // tr),
            in_specs=[pl.BlockSpec((1, tr, cols), lambda l, i, cr: (cr[0] * hl + l, i, 0)),
                      pl.BlockSpec((1, tr, cols), lambda l, i, cr: (l, i, 0))],
            out_specs=pl.BlockSpec((1, tr, cols), lambda l, i, cr: (l, i, 0))),
        out_shape=jax.ShapeDtypeStruct((hl, rows, cols), BF16),
        compiler_params=_cparams("parallel", "parallel"),
    )(core, g, r1)


def _chip_partials(grads, names, *, tag):
    core = lax.axis_index("c").astype(jnp.int32).reshape(1)
    r1 = _swap_halves({n: grads[n] for n in names}, name=f"grad_swap_halves_{tag}")
    return r1, {n: _chip_sum(grads[n], r1[n], core, name=f"grad_chip_sum_{n}") for n in names}


def _scatter_partials(partials):
    names = list(partials)
    n = len(names)

    def body(*refs):
        copies = _scatter_copies(refs[:n], refs[n:2 * n], names, *refs[2 * n:])
        for cp in copies:
            cp.start()
        for cp in copies:
            cp.wait()

    arrays = [partials[nm] for nm in names]
    out = pl.pallas_call(
        body, name="grad_scatter_partials", in_specs=[ANY] * n, out_specs=[ANY] * n,
        out_shape=_scatter_out_shapes(names, arrays),
        scratch_shapes=[pltpu.SemaphoreType.DMA((3 * n,)), pltpu.SemaphoreType.DMA((3 * n,))],
    )(*arrays)
    return dict(zip(names, out))


def _scatter_copies(ins, outs, names, send_sems, recv_sems):
    x, y, c = _place()
    copies = []
    for i, nm in enumerate(names):
        for j, (cx, cy) in enumerate(_other_chips(x, y)):
            copies.append(pltpu.make_async_remote_copy(
                src_ref=_band(ins[i], nm, slice(None), 2 * cx + cy), dst_ref=outs[i].at[j],
                send_sem=send_sems.at[3 * i + j], recv_sem=recv_sems.at[3 * i + j],
                device_id=(cx, cy, c), device_id_type=MESH))
    return copies


def _scatter_out_shapes(names, arrays):
    return [jax.ShapeDtypeStruct((3, a.shape[0]) + _shard_shape(nm)[1:], a.dtype) for nm, a in zip(names, arrays)]


def _final_sum(g, r1, r2, place, nm, *, name):
    (layers, _, _), axis = BIG[nm]
    hl = layers // 2
    _, rows, cols = _shard_shape(nm)
    tr = _sum_rows(rows, cols)
    nrb = rows // tr
    if axis == 1:
        blk = lambda l, i, pc: (l, pc[1] * nrb + i, 0)
    else:
        blk = lambda l, i, pc: (l, i, pc[1])

    def body(place_ref, g_ref, r1_ref, r2_ref, o_ref):
        acc = g_ref[...] + r1_ref[...]
        for j in range(3):
            acc = acc + r2_ref[j].astype(F32)
        o_ref[...] = acc

    return pl.pallas_call(
        body, name=name,
        grid_spec=pltpu.PrefetchScalarGridSpec(
            num_scalar_prefetch=1, grid=(hl, nrb),
            in_specs=[pl.BlockSpec((1, tr, cols), lambda l, i, pc: blk(pc[0] * hl + l, i, pc)),
                      pl.BlockSpec((1, tr, cols), lambda l, i, pc: blk(l, i, pc)),
                      pl.BlockSpec((3, 1, tr, cols), lambda l, i, pc: (0, l, i, 0))],
            out_specs=pl.BlockSpec((1, tr, cols), lambda l, i, pc: (pc[0] * hl + l, i, 0))),
        out_shape=jax.ShapeDtypeStruct((layers, rows, cols), F32),
        compiler_params=_cparams("parallel", "parallel"),
    )(place, g, r1, r2)


def _join_halves(shards):
    names = list(shards)
    n = len(names)

    def body(*refs):
        outs = refs[n:2 * n]
        send_sems, recv_sems = refs[2 * n:]
        x, y, c = _place()
        copies = []
        for i, nm in enumerate(names):
            mine = outs[i].at[_half(nm, c)]
            cp = pltpu.make_async_remote_copy(
                src_ref=mine, dst_ref=mine, send_sem=send_sems.at[i], recv_sem=recv_sems.at[i],
                device_id=(x, y, 1 - c), device_id_type=MESH)
            cp.start()
            copies.append(cp)
        for i, nm in enumerate(names):
            theirs = outs[i].at[_half(nm, 1 - c)]
            pltpu.make_async_remote_copy(
                src_ref=theirs, dst_ref=theirs, send_sem=send_sems.at[i], recv_sem=recv_sems.at[i],
                device_id=(x, y, 1 - c), device_id_type=MESH).wait_recv()
        for cp in copies:
            cp.wait_send()

    arrays = [shards[nm] for nm in names]
    out = pl.pallas_call(
        body, name="grad_join_halves", in_specs=[ANY] * n, out_specs=[ANY] * n,
        out_shape=[jax.ShapeDtypeStruct(a.shape, a.dtype) for a in arrays],
        input_output_aliases={i: i for i in range(n)},
        scratch_shapes=[pltpu.SemaphoreType.DMA((n,)), pltpu.SemaphoreType.DMA((n,))],
    )(*arrays)
    return dict(zip(names, out))


def _all_reduce_small(part, *, name):
    rows, cols = part.shape
    vm = pl.BlockSpec(memory_space=pltpu.VMEM)

    def body(p_ref, o_ref, land, send_sems, recv_sems):
        x, y, c = _place()
        me = 4 * x + 2 * y + c
        flips = [(dx, dy, dc) for dx in (0, 1) for dy in (0, 1) for dc in (0, 1)][1:]
        copies = []
        for k, (dx, dy, dc) in enumerate(flips):
            cp = pltpu.make_async_remote_copy(
                src_ref=p_ref, dst_ref=land.at[me], send_sem=send_sems.at[k], recv_sem=recv_sems.at[k],
                device_id=(1 - x if dx else x, 1 - y if dy else y, 1 - c if dc else c), device_id_type=MESH)
            cp.start()
            copies.append(cp)
        land[me] = p_ref[...]
        for cp in copies:
            cp.wait()
        acc = land[0]
        for j in range(1, 8):
            acc = acc + land[j]
        o_ref[...] = acc

    return pl.pallas_call(
        body, name=name, in_specs=[vm], out_specs=vm,
        out_shape=jax.ShapeDtypeStruct((rows, cols), F32),
        scratch_shapes=[pltpu.VMEM((8, rows, cols), F32), pltpu.SemaphoreType.DMA((7,)), pltpu.SemaphoreType.DMA((7,))],
    )(part)


SMALL = {"g_mix": (4, 1024), "g_mlp": (4, 1024), "attn_g_q_a": (2, 256), "attn_g_kv_a": (2, 128),
         "attn_g_qnorm": (2, 192), "attn_g_knorm": (2, 192)}
SMALL_GRADS = {**SMALL, "conv_w": CONV_W}
WEIGHT_ORDER = ["g_mix", "g_mlp", "attn_w_down", "attn_g_q_a", "attn_g_kv_a", "attn_w_uq", "attn_w_ukv",
                "attn_g_qnorm", "attn_g_knorm", "attn_w_o", "conv_w_in", "conv_w", "conv_w_out", "mlp_w1", "mlp_w2"]


def _prod(shape):
    n = 1
    for v in shape:
        n *= v
    return n


def _pack_small(parts, table):
    flat = [parts[n].reshape(-1) for n in table]
    size = sum(_prod(s) for s in table.values())
    rows = -(-size // (8 * 128)) * 8
    flat.append(jnp.zeros((rows * 128 - size,), F32))
    return jnp.concatenate(flat).reshape(rows, 128)


def _unpack_small(buf, table):
    flat = buf.reshape(-1)
    out, off = {}, 0
    for n, shp in table.items():
        out[n] = flat[off:off + _prod(shp)].reshape(shp)
        off += _prod(shp)
    return out


def _rope_tables(positions):
    inv_freq = ROPE_THETA ** (-jnp.arange(0, QK_ROPE, 2, dtype=F32) / QK_ROPE)
    ang = positions.astype(F32)[:, None] * inv_freq
    cos, sin = jnp.cos(ang), jnp.sin(ang)
    z32 = jnp.zeros_like(cos)
    z64 = jnp.zeros((positions.shape[0], 64), F32)
    cc = jnp.concatenate([cos, cos, z64], axis=1)
    sa = jnp.concatenate([-sin, z32, z64], axis=1)
    sb = jnp.concatenate([z32, sin, z64], axis=1)
    return cc, sa, sb


def _pad_heads(w, width):
    k = w.shape[0]
    w = w.reshape(k, N_HEADS, width)
    return jnp.pad(w, ((0, 0), (0, 0), (0, QK_PAD - width))).reshape(k, N_HEADS * QK_PAD)


EARLY = ("mlp_w1", "mlp_w2", "conv_w_in", "conv_w_out")
LATE = ("attn_w_down", "attn_w_uq", "attn_w_ukv", "attn_w_o")
GATHER_LATER = EARLY


def _local_step(x, positions, target, wb, gains, later=None):
    s = x.shape[0]
    cc, sa, sb = _rope_tables(positions)
    pos_col = positions.reshape(s, 1)
    pos_row = positions.reshape(1, s)

    saved = []
    for i in range(4):
        j = i // 2
        g_mix = gains["g_mix"][i:i + 1]
        g_mlp = gains["g_mlp"][i:i + 1]
        if i % 2 == 0:
            w_down = jnp.pad(wb["attn_w_down"][j], ((0, 0), (0, DOWN_PAD - DOWN_DIM)))
            w_uq = _pad_heads(wb["attn_w_uq"][j], QK_DIM)
            w_ukv = wb["attn_w_ukv"][j]
            g_qa = gains["attn_g_q_a"][j:j + 1]
            g_kva = gains["attn_g_kv_a"][j:j + 1]
            g_q = jnp.pad(gains["attn_g_qnorm"][j:j + 1], ((0, 0), (0, QK_PAD - QK_DIM)))
            g_k = jnp.pad(gains["attn_g_knorm"][j:j + 1], ((0, 0), (0, QK_PAD - QK_DIM)))
            h, a = _norm_mm(x, g_mix, w_down, out_dtype=F32, name=f"mla_down_{j}")
            cq, ckv, q, k, v = _mla_prep(a, g_qa, g_kva, w_uq, w_ukv, g_q, g_k, cc, sa, sb, name=f"mla_prep_{j}")
            o, lse, got = _flash_fwd(q, k, v, pos_col, pos_row, name=f"flash_fwd_{j}",
                                     gather=later if i == 0 else None)
            if got:
                wb = {**wb, **_gather_weights(got, name="gather_later_forward", ici=False)}
            x_mid = _mm_nn(o, wb["attn_w_o"], layer=j, out_dtype=F32, residual=x, name=f"mla_out_{j}")
            mix = dict(h=h, a=a, cq=cq, ckv=ckv, q=q, k=k, v=v, o=o, lse=lse, w_down=w_down, w_uq=w_uq, w_ukv=w_ukv,
                       g_qa=g_qa, g_kva=g_kva, g_q=g_q, g_k=g_k)
        else:
            h, bcu = _norm_mm(x, g_mix, wb["conv_w_in"], layer=j, out_dtype=BF16, name=f"conv_in_{j}")
            z = _conv_gate(bcu, gains["conv_w"][j], name=f"conv_gate_{j}")
            x_mid = _mm_nn(z, wb["conv_w_out"], layer=j, out_dtype=F32, residual=x, name=f"conv_out_{j}")
            mix = dict(h=h, bcu=bcu, z=z)
        h2, u, x_out = _mlp_fwd(x_mid, g_mlp, wb["mlp_w1"], wb["mlp_w2"], layer=i, name=f"mlp_fwd_{i}")
        saved.append(dict(x_in=x, x_mid=x_mid, mix=mix, h2=h2, u=u, g_mix=g_mix, g_mlp=g_mlp))
        x = x_out

    dx, loss = _loss_head(x, target, name="loss_head")

    gw = {n: None for n in BIG}
    exchanged = None
    g_uq = [None, None]
    gs = {n: [None] * SMALL_GRADS[n][0] for n in SMALL_GRADS}

    def wgrad(nm, layer, a, b, **kw):
        out = _mm_tn(a, b, stack=gw[nm], layer=layer, layers=BIG[nm][0][0], name=f"{nm}_grad_{layer}", **kw)
        gw[nm], arrived = out if kw.get("swap") else (out, None)
        return arrived

    for i in reversed(range(4)):
        j = i // 2
        sv = saved[i]
        mix = sv["mix"]
        ride = i == 0 and later is not None
        du = _mlp_down_bwd(dx, wb["mlp_w2"], sv["u"], layer=i, name=f"mlp_down_bwd_{i}")
        wgrad("mlp_w2", i, sv["u"], dx, sqrelu_a=True)
        r1_early = wgrad("mlp_w1", i, sv["h2"], du,
                         swap={n: gw[n] for n in ("mlp_w2", "conv_w_in", "conv_w_out")} if ride else None)
        dx, dg, *arrived = _nt_rms_bwd(du, wb["mlp_w1"], sv["x_mid"], sv["g_mlp"], dx, layer=i, name=f"mlp_up_bwd_{i}",
                                       swap={"mlp_w1": gw["mlp_w1"]} if ride else None)
        if ride:
            r1_early.update(arrived[0])
        gs["g_mlp"][i] = dg[0]
        if i % 2 == 0:
            do, delta_row = _attn_out_bwd(dx, wb["attn_w_o"], mix["o"], layer=j, name=f"mla_out_bwd_{j}")
            wgrad("attn_w_o", j, mix["o"], dx)
            lse_row = mix["lse"]
            partials = None
            if ride:
                core = lax.axis_index("c").astype(jnp.int32).reshape(1)
                partials = {n: _chip_sum(gw[n], r1_early[n], core, name=f"grad_chip_sum_{n}") for n in EARLY}
            dq, dk, dv, arrived = _flash_bwd(mix["q"], mix["k"], mix["v"], do, lse_row, delta_row, pos_col, pos_row,
                                             name=f"flash_bwd_{j}", scatter=partials)
            if partials is not None:
                exchanged = (r1_early, arrived)
            dqr, dkvr, da, dgq, dgk, dgqa, dgkva = _mla_prep_bwd(
                mix["a"], mix["g_qa"], mix["g_kva"], mix["w_uq"], mix["w_ukv"], mix["g_q"], mix["g_k"], cc, sa, sb,
                dq, dk, dv, name=f"mla_prep_bwd_{j}")
            g_uq[j] = _mm_tn(mix["cq"], dqr, name=f"attn_w_uq_grad_{j}")[0]
            wgrad("attn_w_ukv", j, mix["ckv"], dkvr)
            wgrad("attn_w_down", j, mix["h"], da, keep=DOWN_DIM)
            dx, dg = _nt_rms_bwd(da, mix["w_down"], sv["x_in"], sv["g_mix"], dx, name=f"mla_down_bwd_{j}")
            gs["attn_g_qnorm"][j] = dgq[0, :QK_DIM]
            gs["attn_g_knorm"][j] = dgk[0, :QK_DIM]
            gs["attn_g_q_a"][j] = dgqa[0]
            gs["attn_g_kv_a"][j] = dgkva[0]
        else:
            dz = _mm_nt(dx, wb["conv_w_out"], layer=j, out_dtype=F32, name=f"conv_out_bwd_{j}")
            wgrad("conv_w_out", j, mix["z"], dx)
            dbcu, dcw = _conv_gate_bwd(mix["bcu"], dz, gains["conv_w"][j], name=f"conv_gate_bwd_{j}")
            gs["conv_w"][j] = dcw
            wgrad("conv_w_in", j, mix["h"], dbcu)
            dx, dg = _nt_rms_bwd(dbcu, wb["conv_w_in"], sv["x_in"], sv["g_mix"], dx, layer=j, name=f"conv_in_bwd_{j}")
        gs["g_mix"][i] = dg[0]

    gw["attn_w_uq"] = jnp.stack(g_uq).reshape(2, Q_RANK, N_HEADS, QK_PAD)[..., :QK_DIM].reshape(BIG["attn_w_uq"][0])
    grads_small = {n: jnp.stack(v) for n, v in gs.items()}
    return loss, dx, gw, grads_small, exchanged


def kernel(x, positions, g_mix, g_mlp, attn_w_down, attn_g_q_a, attn_g_kv_a, attn_w_uq, attn_w_ukv, attn_g_qnorm, attn_g_knorm, attn_w_o, conv_w_in, conv_w, conv_w_out, mlp_w1, mlp_w2, loss_target, m_g_mix, m_g_mlp, m_attn_w_down, m_attn_g_q_a, m_attn_g_kv_a, m_attn_w_uq, m_attn_w_ukv, m_attn_g_qnorm, m_attn_g_knorm, m_attn_w_o, m_conv_w_in, m_conv_w, m_conv_w_out, m_mlp_w1, m_mlp_w2, v_g_mix, v_g_mlp, v_attn_w_down, v_attn_g_q_a, v_attn_g_kv_a, v_attn_w_uq, v_attn_w_ukv, v_attn_g_qnorm, v_attn_g_knorm, v_attn_w_o, v_conv_w_in, v_conv_w, v_conv_w_out, v_mlp_w1, v_mlp_w2):
    w = dict(g_mix=g_mix, g_mlp=g_mlp, attn_w_down=attn_w_down, attn_g_q_a=attn_g_q_a, attn_g_kv_a=attn_g_kv_a,
             attn_w_uq=attn_w_uq, attn_w_ukv=attn_w_ukv, attn_g_qnorm=attn_g_qnorm, attn_g_knorm=attn_g_knorm,
             attn_w_o=attn_w_o, conv_w_in=conv_w_in, conv_w=conv_w, conv_w_out=conv_w_out, mlp_w1=mlp_w1, mlp_w2=mlp_w2)
    m = dict(g_mix=m_g_mix, g_mlp=m_g_mlp, attn_w_down=m_attn_w_down, attn_g_q_a=m_attn_g_q_a,
             attn_g_kv_a=m_attn_g_kv_a, attn_w_uq=m_attn_w_uq, attn_w_ukv=m_attn_w_ukv, attn_g_qnorm=m_attn_g_qnorm,
             attn_g_knorm=m_attn_g_knorm, attn_w_o=m_attn_w_o, conv_w_in=m_conv_w_in, conv_w=m_conv_w,
             conv_w_out=m_conv_w_out, mlp_w1=m_mlp_w1, mlp_w2=m_mlp_w2)
    v = dict(g_mix=v_g_mix, g_mlp=v_g_mlp, attn_w_down=v_attn_w_down, attn_g_q_a=v_attn_g_q_a,
             attn_g_kv_a=v_attn_g_kv_a, attn_w_uq=v_attn_w_uq, attn_w_ukv=v_attn_w_ukv, attn_g_qnorm=v_attn_g_qnorm,
             attn_g_knorm=v_attn_g_knorm, attn_w_o=v_attn_w_o, conv_w_in=v_conv_w_in, conv_w=v_conv_w,
             conv_w_out=v_conv_w_out, mlp_w1=v_mlp_w1, mlp_w2=v_mlp_w2)
    cx, cy, cc_ = _place()

    chip = 2 * cx + cy

    def own_offset(shape, axis):
        return tuple(chip * (shape[axis] // N_CHIPS) if i == axis else 0 for i in range(3))

    chip_arr = chip.astype(jnp.int32).reshape(1)
    fulls = {n: _place_own(w[n], n, chip_arr, name=f"place_{n}") for n in BIG}
    later = {n: fulls.pop(n) for n in GATHER_LATER}
    wb = _gather_weights(fulls, name="gather_weights")

    placed = lax.dynamic_update_slice(jnp.zeros(CONV_W, F32), conv_w, own_offset(CONV_W, 2))
    conv_w_full = 0.5 * _all_reduce_small(placed.reshape(-1, 128), name="conv_w_gather").reshape(CONV_W)

    gains = {n: w[n] for n in SMALL}
    gains["conv_w"] = conv_w_full

    loss, grad_x, grads_big, grads_small, (r1_early, r2_early) = _local_step(
        x[0], positions[0], loss_target[0], wb, gains, later)

    place = jnp.stack([cc_, chip]).astype(jnp.int32)
    r1_late, partials = _chip_partials(grads_big, LATE, tag="late")
    r1 = {**r1_early, **r1_late}
    r2 = {**r2_early, **_scatter_partials(partials)}
    halves = {n: _final_sum(grads_big[n], r1[n], r2[n], place, n, name=f"grad_final_sum_{n}") for n in BIG}
    grad_shards = _join_halves(halves)

    small = _unpack_small(_all_reduce_small(_pack_small(grads_small, SMALL_GRADS), name="gain_all_reduce"), SMALL_GRADS)
    grad_shards["conv_w"] = lax.dynamic_slice(small["conv_w"], own_offset(CONV_W, 2), conv_w.shape)

    loss_total = lax.psum(loss[0, 0], ("x", "y", "c"))

    grads, deltas, new_m, new_v = {}, {}, {}, {}
    for n in [*BIG, "conv_w"]:
        shp = w[n].shape
        two_d = (shp[0] * shp[1], shp[2])
        g2 = grad_shards[n].reshape(two_d)
        d, nm, nv, g = _adamw(w[n].reshape(two_d), g2, m[n].reshape(two_d), v[n].reshape(two_d), name=f"adamw_{n}")
        grads[n], deltas[n], new_m[n], new_v[n] = g.reshape(shp), d.reshape(shp), nm.reshape(shp), nv.reshape(shp)
    d, nm, nv, g = _adamw(_pack_small(w, SMALL), _pack_small(small, SMALL), _pack_small(m, SMALL),
                          _pack_small(v, SMALL), name="adamw_gains")
    d, nm, nv, g = (_unpack_small(t, SMALL) for t in (d, nm, nv, g))
    for n in SMALL:
        grads[n], deltas[n], new_m[n], new_v[n] = g[n], d[n], nm[n], nv[n]

    return (loss_total, grad_x[None],
            *[grads[n] for n in WEIGHT_ORDER], *[deltas[n] for n in WEIGHT_ORDER],
            *[new_m[n] for n in WEIGHT_ORDER], *[new_v[n] for n in WEIGHT_ORDER])
```

```python
import functools

import jax
import jax.numpy as jnp
from jax import lax
from jax.experimental import pallas as pl
from jax.experimental.pallas import tpu as pltpu

F32 = jnp.float32
BF16 = jnp.bfloat16

D_MODEL = 1024
N_HEADS = 8
QK_NOPE = 128
QK_ROPE = 64
QK_DIM = QK_NOPE + QK_ROPE
QK_PAD = 256
V_DIM = 128
Q_RANK = 256
KV_RANK = 128
DOWN_DIM = Q_RANK + KV_RANK + QK_ROPE
DOWN_PAD = 512
ROPE_THETA = 10000.0
EPS = 1e-6
NEG = -1e30
SCALE = QK_DIM ** -0.5
SCALE_LOG2E = SCALE * 1.4426950408889634
LOG2E = 1.4426950408889634
ATTN_CHAINS = 2
DIAG_CHAINS = 4

ADAM_LR = 0.001
ADAM_B1 = 0.9
ADAM_B2 = 0.999
ADAM_EPS = 1e-08
ADAM_WD = 0.01
ADAM_STEP = 10

N_CHIPS = 4
MESH = pl.DeviceIdType.MESH
ANY = pl.BlockSpec(memory_space=pl.ANY)

TM = 512
TM_NARROW = 1024
TM_WIDE = 512
FWD_TQ = 1024
FWD_TK = 1024
BWD_TQ = 1024
BWD_TK = 1024
HALO = 16
T_PREP = 1024
T_PREP_BWD = 512
T_RED = 2048
SUM_BLOCK_BYTES = 4 * 1024 * 1024


def _tile(n, pref):
    t = min(n, pref)
    assert n % t == 0, (n, t)
    return t


def _cparams(*sem):
    return pltpu.CompilerParams(dimension_semantics=sem)


def _dot(a, b):
    return jnp.dot(a, b, preferred_element_type=F32)


def _dot_nt(a, b):
    return lax.dot_general(a, b, (((1,), (1,)), ((), ())), preferred_element_type=F32)


def _dot_tn(a, b):
    return lax.dot_general(a, b, (((0,), (0,)), ((), ())), preferred_element_type=F32)


def _rms(x, width):
    r = lax.rsqrt(jnp.sum(x * x, axis=-1, keepdims=True) * (1.0 / width) + EPS)
    return x * r, r


def _rms_bwd(xhat, r, dxhat, width):
    return r * (dxhat - xhat * (jnp.sum(dxhat * xhat, axis=-1, keepdims=True) * (1.0 / width)))


def _rope(t, cc, sa, sb):
    return t * cc + pltpu.roll(t, 96, 1) * sa + pltpu.roll(t, 32, 1) * sb


def _rope_t(g, cc, sa, sb):
    return g * cc + pltpu.roll(g * sa, 32, 1) + pltpu.roll(g * sb, 96, 1)


def _wspec(w, layer):
    once = pl.Buffered(1)
    if w.ndim == 2:
        return pl.BlockSpec(w.shape, lambda *_: (0, 0), pipeline_mode=once)
    return pl.BlockSpec((None,) + w.shape[1:], lambda *_: (layer, 0, 0), pipeline_mode=once)


def _mm_nn(a, b, *, out_dtype, name, residual=None, layer=0):
    m, k = a.shape
    n = b.shape[-1]
    tm = _tile(m, TM_NARROW)

    def body(*refs):
        if residual is None:
            a_ref, b_ref, o_ref = refs
        else:
            a_ref, b_ref, r_ref, o_ref = refs
        acc = _dot(a_ref[...].astype(BF16), b_ref[...])
        if residual is not None:
            acc = acc + r_ref[...]
        o_ref[...] = acc.astype(o_ref.dtype)

    in_specs = [pl.BlockSpec((tm, k), lambda i: (i, 0)), _wspec(b, layer)]
    args = [a, b]
    if residual is not None:
        in_specs.append(pl.BlockSpec((tm, n), lambda i: (i, 0)))
        args.append(residual)
    return pl.pallas_call(
        body, name=name, grid=(m // tm,), in_specs=in_specs,
        out_specs=pl.BlockSpec((tm, n), lambda i: (i, 0)),
        out_shape=jax.ShapeDtypeStruct((m, n), out_dtype),
        compiler_params=_cparams("parallel"),
    )(*args)


def _mm_nt(a, b, *, out_dtype, name, layer=0):
    m, k = a.shape
    n = b.shape[-2]
    tm = _tile(m, TM_NARROW)

    def body(a_ref, b_ref, o_ref):
        o_ref[...] = _dot_nt(a_ref[...].astype(BF16), b_ref[...]).astype(o_ref.dtype)

    return pl.pallas_call(
        body, name=name, grid=(m // tm,),
        in_specs=[pl.BlockSpec((tm, k), lambda i: (i, 0)), _wspec(b, layer)],
        out_specs=pl.BlockSpec((tm, n), lambda i: (i, 0)),
        out_shape=jax.ShapeDtypeStruct((m, n), out_dtype),
        compiler_params=_cparams("parallel"),
    )(a, b)


def _mm_tn(a, b, *, name, stack=None, layer=0, layers=1, keep=None, sqrelu_a=False, swap=None):
    s, ka = a.shape
    n = b.shape[1]
    ts = _tile(s, T_RED)
    tka = _tile(ka, 1024)
    tn = _tile(n, 1024)
    n_out = n if keep is None else keep
    assert keep is None or tn == n
    grid = (ka // tka, n // tn, s // ts)
    names = list(swap or {})
    ns = len(names)
    n_in = 2 + (stack is not None)

    def body(*refs):
        a_ref, b_ref = refs[:2]
        o_ref = refs[n_in + ns]
        if ns:
            copies = _swap_copies(refs[n_in:n_in + ns], refs[n_in + ns + 1:n_in + 2 * ns + 1], names,
                                  *refs[n_in + 2 * ns + 1:])
            _start_at_first_step(copies, grid)

        @pl.when(pl.program_id(2) == 0)
        def _():
            o_ref[...] = jnp.zeros_like(o_ref)

        a_t = a_ref[...]
        if sqrelu_a:
            a_t = _sqrelu(a_t.astype(F32))
        o_ref[...] += _dot_tn(a_t.astype(BF16), b_ref[...].astype(BF16))[:, :n_out if keep else tn]
        if ns:
            _wait_at_last_step(copies, grid)

    in_specs = [pl.BlockSpec((ts, tka), lambda i, j, t: (t, i)), pl.BlockSpec((ts, tn), lambda i, j, t: (t, j))]
    args = [a, b]
    if stack is not None:
        in_specs.append(ANY)
        args.append(stack)
    sent = [swap[nm] for nm in names]
    out = pl.pallas_call(
        body, name=name, grid=grid, in_specs=in_specs + [ANY] * ns,
        out_specs=[pl.BlockSpec((None, tka, tn if keep is None else keep), lambda i, j, t: (layer, i, j))] + [ANY] * ns,
        out_shape=[jax.ShapeDtypeStruct((layers, ka, n_out), F32)] + _swap_out_shapes(sent),
        scratch_shapes=_swap_sems(ns),
        input_output_aliases={} if stack is None else {2: 0},
        compiler_params=_cparams(*(["arbitrary"] * 3 if ns else ["parallel", "parallel", "arbitrary"])),
    )(*args, *sent)
    return (out[0], dict(zip(names, out[1:]))) if ns else out[0]


def _norm_mm(x, g, w, *, out_dtype, name, layer=0):
    s, d = x.shape
    n = w.shape[-1]
    tm = _tile(s, TM_NARROW)

    def body(x_ref, g_ref, w_ref, h_ref, o_ref):
        xhat, _ = _rms(x_ref[...], d)
        h = (xhat * g_ref[...]).astype(BF16)
        h_ref[...] = h
        o_ref[...] = _dot(h, w_ref[...]).astype(o_ref.dtype)

    return pl.pallas_call(
        body, name=name, grid=(s // tm,),
        in_specs=[pl.BlockSpec((tm, d), lambda i: (i, 0)), pl.BlockSpec((1, d), lambda i: (0, 0)), _wspec(w, layer)],
        out_specs=[pl.BlockSpec((tm, d), lambda i: (i, 0)), pl.BlockSpec((tm, n), lambda i: (i, 0))],
        out_shape=[jax.ShapeDtypeStruct((s, d), BF16), jax.ShapeDtypeStruct((s, n), out_dtype)],
        compiler_params=_cparams("parallel"),
    )(x, g, w)


def _nt_rms_bwd(dy, w, x, g, dres, *, name, layer=0, swap=None):
    s, n = dy.shape
    d = x.shape[1]
    tm = _tile(s, TM if n > 3072 else TM_NARROW)
    grid = (s // tm,)
    names = list(swap or {})
    ns = len(names)

    def body(dy_ref, w_ref, x_ref, g_ref, dres_ref, *rest):
        dx_ref, dg_ref = rest[ns:ns + 2]
        if ns:
            copies = _swap_copies(rest[:ns], rest[ns + 2:2 * ns + 2], names, *rest[2 * ns + 2:])
            _start_at_first_step(copies, grid)

        @pl.when(pl.program_id(0) == 0)
        def _():
            dg_ref[...] = jnp.zeros_like(dg_ref)

        dh = _dot_nt(dy_ref[...], w_ref[...])
        xhat, r = _rms(x_ref[...], d)
        dg_ref[...] += jnp.sum(dh * xhat, axis=0, keepdims=True)
        dx_ref[...] = dres_ref[...] + _rms_bwd(xhat, r, dh * g_ref[...], d)
        if ns:
            _wait_at_last_step(copies, grid)

    sent = [swap[nm] for nm in names]
    out = pl.pallas_call(
        body, name=name, grid=grid,
        in_specs=[pl.BlockSpec((tm, n), lambda i: (i, 0)), _wspec(w, layer),
                  pl.BlockSpec((tm, d), lambda i: (i, 0)), pl.BlockSpec((1, d), lambda i: (0, 0)),
                  pl.BlockSpec((tm, d), lambda i: (i, 0))] + [ANY] * ns,
        out_specs=[pl.BlockSpec((tm, d), lambda i: (i, 0)), pl.BlockSpec((1, d), lambda i: (0, 0))] + [ANY] * ns,
        out_shape=[jax.ShapeDtypeStruct((s, d), F32), jax.ShapeDtypeStruct((1, d), F32)] + _swap_out_shapes(sent),
        scratch_shapes=_swap_sems(ns),
        compiler_params=_cparams("arbitrary"),
    )(dy, w, x, g, dres, *sent)
    return (out[0], out[1], dict(zip(names, out[2:]))) if ns else (out[0], out[1])


def _sqrelu(u):
    return jnp.square(jnp.maximum(u, 0.0))


def _mlp_fwd(x, g, w1, w2, *, name, layer=0):
    s, d = x.shape
    n = w1.shape[-1]
    tm = _tile(s, TM_WIDE)

    def body(x_ref, g_ref, w1_ref, w2_ref, h_ref, u_ref, y_ref):
        x_t = x_ref[...]
        xhat, _ = _rms(x_t, d)
        h = (xhat * g_ref[...]).astype(BF16)
        h_ref[...] = h
        u = _dot(h, w1_ref[...])
        u_ref[...] = u.astype(BF16)
        y_ref[...] = x_t + _dot(_sqrelu(u).astype(BF16), w2_ref[...])

    return pl.pallas_call(
        body, name=name, grid=(s // tm,),
        in_specs=[pl.BlockSpec((tm, d), lambda i: (i, 0)), pl.BlockSpec((1, d), lambda i: (0, 0)),
                  _wspec(w1, layer), _wspec(w2, layer)],
        out_specs=[pl.BlockSpec((tm, d), lambda i: (i, 0)), pl.BlockSpec((tm, n), lambda i: (i, 0)),
                   pl.BlockSpec((tm, d), lambda i: (i, 0))],
        out_shape=[jax.ShapeDtypeStruct((s, d), BF16), jax.ShapeDtypeStruct((s, n), BF16),
                   jax.ShapeDtypeStruct((s, d), F32)],
        compiler_params=_cparams("parallel"),
    )(x, g, w1, w2)


def _mlp_down_bwd(dy, w2, u, *, name, layer=0):
    s, d = dy.shape
    n = w2.shape[-2]
    tm = _tile(s, TM_WIDE)

    def body(dy_ref, w_ref, u_ref, du_ref):
        dact = _dot_nt(dy_ref[...].astype(BF16), w_ref[...])
        du_ref[...] = (dact * (2.0 * jnp.maximum(u_ref[...].astype(F32), 0.0))).astype(BF16)

    return pl.pallas_call(
        body, name=name, grid=(s // tm,),
        in_specs=[pl.BlockSpec((tm, d), lambda i: (i, 0)), _wspec(w2, layer),
                  pl.BlockSpec((tm, n), lambda i: (i, 0))],
        out_specs=pl.BlockSpec((tm, n), lambda i: (i, 0)),
        out_shape=jax.ShapeDtypeStruct((s, n), BF16),
        compiler_params=_cparams("parallel"),
    )(dy, w2, u)


def _conv_gate(bcu, conv_w, *, name):
    s = bcu.shape[0]
    d = D_MODEL
    tm = _tile(s, TM)
    hb = tm // HALO

    def body(bcu_ref, prev_ref, w_ref, z_ref, pbuf):
        i = pl.program_id(0)
        gb = bcu_ref[:, 0:d].astype(F32)
        p = bcu_ref[:, d:2 * d].astype(F32) * bcu_ref[:, 2 * d:3 * d].astype(F32)
        pprev = prev_ref[:, d:2 * d].astype(F32) * prev_ref[:, 2 * d:3 * d].astype(F32)
        pbuf[0:HALO, :] = jnp.where(i > 0, pprev, 0.0)
        pbuf[HALO:HALO + tm, :] = p
        cv = (w_ref[2:3, :] * p + w_ref[1:2, :] * pbuf[HALO - 1:HALO - 1 + tm, :]
              + w_ref[0:1, :] * pbuf[HALO - 2:HALO - 2 + tm, :])
        z_ref[...] = (gb * cv).astype(BF16)

    return pl.pallas_call(
        body, name=name, grid=(s // tm,),
        in_specs=[pl.BlockSpec((tm, 3 * d), lambda i: (i, 0)),
                  pl.BlockSpec((HALO, 3 * d), lambda i: (jnp.maximum(i * hb - 1, 0), 0)),
                  pl.BlockSpec((3, d), lambda i: (0, 0))],
        out_specs=pl.BlockSpec((tm, d), lambda i: (i, 0)),
        out_shape=jax.ShapeDtypeStruct((s, d), BF16),
        scratch_shapes=[pltpu.VMEM((tm + HALO, d), F32)],
        compiler_params=_cparams("parallel"),
    )(bcu, bcu, conv_w)


def _conv_gate_bwd(bcu, dz, conv_w, *, name):
    s = bcu.shape[0]
    d = D_MODEL
    tm = _tile(s, TM)
    hb = tm // HALO
    nt = s // tm

    def body(bcu_ref, prev_ref, next_ref, dz_ref, dznext_ref, w_ref, dbcu_ref, dw_ref, pbuf, dbuf):
        i = pl.program_id(0)

        @pl.when(i == 0)
        def _():
            dw_ref[...] = jnp.zeros_like(dw_ref)

        gb = bcu_ref[:, 0:d].astype(F32)
        gc = bcu_ref[:, d:2 * d].astype(F32)
        uu = bcu_ref[:, 2 * d:3 * d].astype(F32)
        p = gc * uu
        pprev = prev_ref[:, d:2 * d].astype(F32) * prev_ref[:, 2 * d:3 * d].astype(F32)
        pbuf[0:HALO, :] = jnp.where(i > 0, pprev, 0.0)
        pbuf[HALO:HALO + tm, :] = p
        p1 = pbuf[HALO - 1:HALO - 1 + tm, :]
        p2 = pbuf[HALO - 2:HALO - 2 + tm, :]
        cv = w_ref[2:3, :] * p + w_ref[1:2, :] * p1 + w_ref[0:1, :] * p2
        dz_t = dz_ref[...]
        dcv = dz_t * gb
        dcv_next = dznext_ref[...] * next_ref[:, 0:d].astype(F32)
        dbuf[0:tm, :] = dcv
        dbuf[tm:tm + HALO, :] = jnp.where(i < nt - 1, dcv_next, 0.0)
        dp = w_ref[2:3, :] * dcv + w_ref[1:2, :] * dbuf[1:1 + tm, :] + w_ref[0:1, :] * dbuf[2:2 + tm, :]
        dw_ref[2:3, :] += jnp.sum(dcv * p, axis=0, keepdims=True)
        dw_ref[1:2, :] += jnp.sum(dcv * p1, axis=0, keepdims=True)
        dw_ref[0:1, :] += jnp.sum(dcv * p2, axis=0, keepdims=True)
        dbcu_ref[:, 0:d] = (dz_t * cv).astype(BF16)
        dbcu_ref[:, d:2 * d] = (dp * uu).astype(BF16)
        dbcu_ref[:, 2 * d:3 * d] = (dp * gc).astype(BF16)

    nxt = lambda i: (jnp.minimum((i + 1) * hb, s // HALO - 1), 0)
    return pl.pallas_call(
        body, name=name, grid=(nt,),
        in_specs=[pl.BlockSpec((tm, 3 * d), lambda i: (i, 0)),
                  pl.BlockSpec((HALO, 3 * d), lambda i: (jnp.maximum(i * hb - 1, 0), 0)),
                  pl.BlockSpec((HALO, 3 * d), nxt),
                  pl.BlockSpec((tm, d), lambda i: (i, 0)),
                  pl.BlockSpec((HALO, d), nxt),
                  pl.BlockSpec((3, d), lambda i: (0, 0))],
        out_specs=[pl.BlockSpec((tm, 3 * d), lambda i: (i, 0)), pl.BlockSpec((3, d), lambda i: (0, 0))],
        out_shape=[jax.ShapeDtypeStruct((s, 3 * d), BF16), jax.ShapeDtypeStruct((3, d), F32)],
        scratch_shapes=[pltpu.VMEM((tm + HALO, d), F32), pltpu.VMEM((tm + HALO, d), F32)],
        compiler_params=_cparams("arbitrary"),
    )(bcu, bcu, bcu, dz, dz, conv_w)


def _mla_prep(a, g_qa, g_kva, w_uq, w_ukv, g_q, g_k, cc, sa, sb, *, name):
    s = a.shape[0]
    ts = _tile(s, T_PREP)

    def body(a_ref, gqa_ref, gkva_ref, wuq_ref, wukv_ref, gq_ref, gk_ref, cc_ref, sa_ref, sb_ref,
             cq_ref, ckv_ref, q_ref, k_ref, v_ref):
        xq, _ = _rms(a_ref[:, 0:Q_RANK], Q_RANK)
        cq = (xq * gqa_ref[...]).astype(BF16)
        cq_ref[...] = cq
        xkv, _ = _rms(a_ref[:, Q_RANK:Q_RANK + KV_RANK], KV_RANK)
        ckv = (xkv * gkva_ref[...]).astype(BF16)
        ckv_ref[...] = ckv
        kpe = a_ref[:, Q_RANK + KV_RANK:DOWN_PAD]
        kpe_ss = jnp.sum(kpe * kpe, axis=-1, keepdims=True)
        cc_t, sa_t, sb_t = cc_ref[...], sa_ref[...], sb_ref[...]
        gq = gq_ref[...]
        gk = gk_ref[...]
        for h in range(N_HEADS):
            cols = slice(h * QK_PAD, (h + 1) * QK_PAD)
            qhat, _ = _rms(_dot(cq, wuq_ref[:, cols]), QK_DIM)
            qn = qhat * (gq * SCALE_LOG2E)
            q_ref[h, :, 0:QK_NOPE] = qn[:, 0:QK_NOPE].astype(BF16)
            q_ref[h, :, QK_NOPE:QK_PAD] = _rope(qn[:, QK_NOPE:QK_PAD], cc_t, sa_t, sb_t).astype(BF16)
            kvr = _dot(ckv, wukv_ref[:, cols])
            kn = kvr[:, 0:QK_NOPE]
            rk = lax.rsqrt((jnp.sum(kn * kn, axis=-1, keepdims=True) + kpe_ss) * (1.0 / QK_DIM) + EPS)
            k_ref[h, :, 0:QK_NOPE] = (kn * rk * gk[:, 0:QK_NOPE]).astype(BF16)
            k_ref[h, :, QK_NOPE:QK_PAD] = _rope(kpe * rk * gk[:, QK_NOPE:QK_PAD], cc_t, sa_t, sb_t).astype(BF16)
            v_ref[h, :, 0:V_DIM] = kvr[:, QK_NOPE:QK_PAD].astype(BF16)
            v_ref[h, :, V_DIM:2 * V_DIM] = jnp.ones((ts, V_DIM), BF16)

    row = lambda i: (i, 0)
    fixed = lambda i: (0, 0)
    head = lambda i: (0, i, 0)
    return pl.pallas_call(
        body, name=name, grid=(s // ts,),
        in_specs=[pl.BlockSpec((ts, DOWN_PAD), row), pl.BlockSpec((1, Q_RANK), fixed), pl.BlockSpec((1, KV_RANK), fixed),
                  pl.BlockSpec((Q_RANK, N_HEADS * QK_PAD), fixed), pl.BlockSpec((KV_RANK, N_HEADS * QK_PAD), fixed),
                  pl.BlockSpec((1, QK_PAD), fixed), pl.BlockSpec((1, QK_PAD), fixed),
                  pl.BlockSpec((ts, 128), row), pl.BlockSpec((ts, 128), row), pl.BlockSpec((ts, 128), row)],
        out_specs=[pl.BlockSpec((ts, Q_RANK), row), pl.BlockSpec((ts, KV_RANK), row),
                   pl.BlockSpec((N_HEADS, ts, QK_PAD), head), pl.BlockSpec((N_HEADS, ts, QK_PAD), head),
                   pl.BlockSpec((N_HEADS, ts, 2 * V_DIM), head)],
        out_shape=[jax.ShapeDtypeStruct((s, Q_RANK), BF16), jax.ShapeDtypeStruct((s, KV_RANK), BF16),
                   jax.ShapeDtypeStruct((N_HEADS, s, QK_PAD), BF16), jax.ShapeDtypeStruct((N_HEADS, s, QK_PAD), BF16),
                   jax.ShapeDtypeStruct((N_HEADS, s, 2 * V_DIM), BF16)],
        compiler_params=_cparams("parallel"),
    )(a, g_qa, g_kva, w_uq, w_ukv, g_q, g_k, cc, sa, sb)


def _mla_prep_bwd(a, g_qa, g_kva, w_uq, w_ukv, g_q, g_k, cc, sa, sb, dq, dk, dv, *, name):
    s = a.shape[0]
    ts = _tile(s, T_PREP_BWD)

    def body(a_ref, gqa_ref, gkva_ref, wuq_ref, wukv_ref, gq_ref, gk_ref, cc_ref, sa_ref, sb_ref,
             dq_ref, dk_ref, dv_ref, dqr_ref, dkvr_ref, da_ref, dgq_ref, dgk_ref, dgqa_ref, dgkva_ref):
        @pl.when(pl.program_id(0) == 0)
        def _():
            dgq_ref[...] = jnp.zeros_like(dgq_ref)
            dgk_ref[...] = jnp.zeros_like(dgk_ref)
            dgqa_ref[...] = jnp.zeros_like(dgqa_ref)
            dgkva_ref[...] = jnp.zeros_like(dgkva_ref)

        xq, r_q = _rms(a_ref[:, 0:Q_RANK], Q_RANK)
        cq = (xq * gqa_ref[...]).astype(BF16)
        xkv, r_kv = _rms(a_ref[:, Q_RANK:Q_RANK + KV_RANK], KV_RANK)
        ckv = (xkv * gkva_ref[...]).astype(BF16)
        kpe = a_ref[:, Q_RANK + KV_RANK:DOWN_PAD]
        kpe_ss = jnp.sum(kpe * kpe, axis=-1, keepdims=True)
        cc_t, sa_t, sb_t = cc_ref[...], sa_ref[...], sb_ref[...]
        gq = gq_ref[...]
        gk = gk_ref[...]
        dcq = jnp.zeros((ts, Q_RANK), F32)
        dckv = jnp.zeros((ts, KV_RANK), F32)
        dkpe = jnp.zeros((ts, 128), F32)
        dgq = jnp.zeros((1, QK_PAD), F32)
        dgk_n = jnp.zeros((1, QK_NOPE), F32)
        dgk_p = jnp.zeros((1, 128), F32)
        for h in range(N_HEADS):
            cols = slice(h * QK_PAD, (h + 1) * QK_PAD)
            qhat, rq = _rms(_dot(cq, wuq_ref[:, cols]), QK_DIM)
            dqn = jnp.concatenate(
                [dq_ref[h, :, 0:QK_NOPE], _rope_t(dq_ref[h, :, QK_NOPE:QK_PAD], cc_t, sa_t, sb_t)], axis=1)
            dgq = dgq + jnp.sum(dqn * qhat, axis=0, keepdims=True)
            dqr = _rms_bwd(qhat, rq, dqn * gq, QK_DIM).astype(BF16)
            dqr_ref[:, cols] = dqr
            dcq = dcq + _dot_nt(dqr, wuq_ref[:, cols])
            kn = _dot(ckv, wukv_ref[:, h * QK_PAD:h * QK_PAD + QK_NOPE])
            rk = lax.rsqrt((jnp.sum(kn * kn, axis=-1, keepdims=True) + kpe_ss) * (1.0 / QK_DIM) + EPS)
            khat_n = kn * rk
            khat_p = kpe * rk
            dkn = dk_ref[h, :, 0:QK_NOPE]
            dkp = _rope_t(dk_ref[h, :, QK_NOPE:QK_PAD], cc_t, sa_t, sb_t)
            dgk_n = dgk_n + jnp.sum(dkn * khat_n, axis=0, keepdims=True)
            dgk_p = dgk_p + jnp.sum(dkp * khat_p, axis=0, keepdims=True)
            dxn = dkn * gk[:, 0:QK_NOPE]
            dxp = dkp * gk[:, QK_NOPE:QK_PAD]
            mean = (jnp.sum(dxn * khat_n, axis=-1, keepdims=True)
                    + jnp.sum(dxp * khat_p, axis=-1, keepdims=True)) * (1.0 / QK_DIM)
            dkpe = dkpe + rk * (dxp - khat_p * mean)
            dkvr = jnp.concatenate([rk * (dxn - khat_n * mean), dv_ref[h, :, :]], axis=1).astype(BF16)
            dkvr_ref[:, cols] = dkvr
            dckv = dckv + _dot_nt(dkvr, wukv_ref[:, cols])
        dgq_ref[...] += dgq
        dgk_ref[:, 0:QK_NOPE] += dgk_n
        dgk_ref[:, QK_NOPE:QK_PAD] += dgk_p
        dgqa_ref[...] += jnp.sum(dcq * xq, axis=0, keepdims=True)
        dgkva_ref[...] += jnp.sum(dckv * xkv, axis=0, keepdims=True)
        da_ref[:, 0:Q_RANK] = _rms_bwd(xq, r_q, dcq * gqa_ref[...], Q_RANK).astype(BF16)
        da_ref[:, Q_RANK:Q_RANK + KV_RANK] = _rms_bwd(xkv, r_kv, dckv * gkva_ref[...], KV_RANK).astype(BF16)
        da_ref[:, Q_RANK + KV_RANK:DOWN_PAD] = dkpe.astype(BF16)

    row = lambda i: (i, 0)
    fixed = lambda i: (0, 0)
    head = lambda i: (0, i, 0)
    wide = N_HEADS * QK_PAD
    return pl.pallas_call(
        body, name=name, grid=(s // ts,),
        in_specs=[pl.BlockSpec((ts, DOWN_PAD), row), pl.BlockSpec((1, Q_RANK), fixed), pl.BlockSpec((1, KV_RANK), fixed),
                  pl.BlockSpec((Q_RANK, wide), fixed), pl.BlockSpec((KV_RANK, wide), fixed),
                  pl.BlockSpec((1, QK_PAD), fixed), pl.BlockSpec((1, QK_PAD), fixed),
                  pl.BlockSpec((ts, 128), row), pl.BlockSpec((ts, 128), row), pl.BlockSpec((ts, 128), row),
                  pl.BlockSpec((N_HEADS, ts, QK_PAD), head), pl.BlockSpec((N_HEADS, ts, QK_PAD), head),
                  pl.BlockSpec((N_HEADS, ts, V_DIM), head)],
        out_specs=[pl.BlockSpec((ts, wide), row), pl.BlockSpec((ts, wide), row), pl.BlockSpec((ts, DOWN_PAD), row),
                   pl.BlockSpec((1, QK_PAD), fixed), pl.BlockSpec((1, QK_PAD), fixed),
                   pl.BlockSpec((1, Q_RANK), fixed), pl.BlockSpec((1, KV_RANK), fixed)],
        out_shape=[jax.ShapeDtypeStruct((s, wide), BF16), jax.ShapeDtypeStruct((s, wide), BF16),
                   jax.ShapeDtypeStruct((s, DOWN_PAD), BF16),
                   jax.ShapeDtypeStruct((1, QK_PAD), F32), jax.ShapeDtypeStruct((1, QK_PAD), F32),
                   jax.ShapeDtypeStruct((1, Q_RANK), F32), jax.ShapeDtypeStruct((1, KV_RANK), F32)],
        compiler_params=_cparams("arbitrary"),
    )(a, g_qa, g_kva, w_uq, w_ukv, g_q, g_k, cc, sa, sb, dq, dk, dv)


def _flash_fwd(q, k, v, pos_col, pos_row, *, name, gather=None):
    nh, s, _ = q.shape
    tq = _tile(s, FWD_TQ)
    tk = _tile(s, FWD_TK)
    sq = tq // ATTN_CHAINS
    nq = s // tq
    names = list(gather or {})
    ng = len(names)

    def body(q_ref, k_ref, v_ref, pq_ref, pk_ref, *rest):
        o_ref, lse_ref = rest[ng:ng + 2]
        m_sc, acc_sc = rest[2 * ng + 2:2 * ng + 4]
        qb = pl.program_id(1)
        if ng:
            sends, recvs = _gather_ici_copies(rest[ng + 2:2 * ng + 2], names, *rest[2 * ng + 4:], base=0, stride=3)

            @pl.when((pl.program_id(0) == 0) & (qb == 0))
            def _():
                for cp in sends:
                    cp.start()

        m_sc[...] = jnp.full_like(m_sc, NEG)
        acc_sc[...] = jnp.zeros_like(acc_sc)

        def step(kb, masked):
            trim = masked and tq == tk
            chains = DIAG_CHAINS if trim else ATTN_CHAINS
            sq = tq // chains
            start = pl.multiple_of(kb * tk, tk)
            widths = [(u + 1) * sq if trim else tk for u in range(chains)]
            scores = [_dot_nt(q_ref[0, u * sq:(u + 1) * sq, :], k_ref[0, pl.ds(start, widths[u]), :])
                      for u in range(chains)]
            for u in range(chains):
                rows = slice(u * sq, (u + 1) * sq)
                keys = pl.ds(start, widths[u])
                sc = scores[u]
                if masked:
                    sc = jnp.where(pq_ref[rows, :] >= pk_ref[:, keys], sc, NEG)
                m_prev = m_sc[rows, :]
                m_new = jnp.maximum(m_prev, jnp.max(sc, axis=-1, keepdims=True))
                alpha = jnp.exp2(m_prev - m_new)
                p = jnp.exp2(sc - jnp.tile(m_new, (1, widths[u] // 128)))
                acc_sc[rows, :] = (jnp.tile(alpha, (1, 2)) * acc_sc[rows, :]
                                   + _dot(p.astype(BF16), v_ref[0, keys, :]))
                m_sc[rows, :] = m_new

        n_before = (qb * tq) // tk
        n_seen = (qb * tq + tq - 1) // tk + 1
        lax.fori_loop(0, n_before, lambda kb, c: (step(kb, False), c)[1], 0)
        lax.fori_loop(n_before, n_seen, lambda kb, c: (step(kb, True), c)[1], 0)
        l = acc_sc[:, V_DIM:2 * V_DIM]
        o_ref[...] = (acc_sc[:, 0:V_DIM] / l).astype(BF16)
        lse = m_sc[...] * (1.0 / LOG2E) + jnp.log(l)
        lse_ref[0] = lse.T[0:1, :]

        if ng:
            @pl.when((pl.program_id(0) == nh - 1) & (qb == nq - 1))
            def _():
                for cp in recvs:
                    cp.wait_recv()
                for cp in sends:
                    cp.wait_send()

    arrays = [gather[nm] for nm in names]
    out = pl.pallas_call(
        body, name=name, grid=(nh, nq),
        in_specs=[pl.BlockSpec((1, tq, QK_PAD), lambda h, qb: (h, qb, 0)),
                  pl.BlockSpec((1, s, QK_PAD), lambda h, qb: (h, 0, 0)),
                  pl.BlockSpec((1, s, 2 * V_DIM), lambda h, qb: (h, 0, 0)),
                  pl.BlockSpec((tq, 1), lambda h, qb: (qb, 0)),
                  pl.BlockSpec((1, s), lambda h, qb: (0, 0))] + [ANY] * ng,
        out_specs=[pl.BlockSpec((tq, V_DIM), lambda h, qb: (qb, h)),
                   pl.BlockSpec((1, 1, tq), lambda h, qb: (h, 0, qb))] + [ANY] * ng,
        scratch_shapes=[pltpu.VMEM((tq, 128), F32), pltpu.VMEM((tq, 2 * V_DIM), F32)]
        + ([pltpu.SemaphoreType.DMA((3 * ng,)), pltpu.SemaphoreType.DMA((3 * ng,))] if ng else []),
        out_shape=[jax.ShapeDtypeStruct((s, nh * V_DIM), BF16), jax.ShapeDtypeStruct((nh, 1, s), F32)]
        + [jax.ShapeDtypeStruct(a.shape, a.dtype) for a in arrays],
        input_output_aliases={5 + i: 2 + i for i in range(ng)},
        compiler_params=_cparams("arbitrary", "arbitrary") if ng else _cparams("parallel", "parallel"),
    )(q, k, v, pos_col, pos_row, *arrays)
    return out[0], out[1], dict(zip(names, out[2:]))


def _attn_out_bwd(dy, w_o, o, *, name, layer=0):
    s, d = dy.shape
    n = w_o.shape[-2]
    tm = _tile(s, TM)

    def body(dy_ref, w_ref, o_ref, do_ref, d_ref):
        do = _dot_nt(dy_ref[...].astype(BF16), w_ref[...]).astype(BF16)
        do_ref[...] = do
        for h in range(N_HEADS):
            cols = slice(h * V_DIM, (h + 1) * V_DIM)
            prod = do[:, cols].astype(F32) * o_ref[:, cols].astype(F32)
            d_ref[h] = jnp.sum(prod.T, axis=0, keepdims=True)

    return pl.pallas_call(
        body, name=name, grid=(s // tm,),
        in_specs=[pl.BlockSpec((tm, d), lambda i: (i, 0)), _wspec(w_o, layer), pl.BlockSpec((tm, n), lambda i: (i, 0))],
        out_specs=[pl.BlockSpec((tm, n), lambda i: (i, 0)), pl.BlockSpec((N_HEADS, 1, tm), lambda i: (0, 0, i))],
        out_shape=[jax.ShapeDtypeStruct((s, n), BF16), jax.ShapeDtypeStruct((N_HEADS, 1, s), F32)],
        compiler_params=_cparams("parallel"),
    )(dy, w_o, o)


def _flash_bwd(q, k, v, do, lse_row, delta_row, pos_col, pos_row, *, name, scatter=None):
    nh, s, _ = q.shape
    tq = _tile(s, BWD_TQ)
    tk = _tile(s, BWD_TK)
    nq, nk = s // tq, s // tk
    sk = tk // ATTN_CHAINS
    names = list(scatter or {})
    ng = len(names)

    def body(q_ref, k_ref, v_ref, do_ref, lse_ref, delta_ref, pq_ref, pk_ref, *rest):
        dq_ref, dk_ref, dv_ref = rest[ng:ng + 3]
        dk_sc, dv_sc = rest[2 * ng + 3:2 * ng + 5]
        kb = pl.program_id(1)
        if ng:
            copies = _scatter_copies(rest[:ng], rest[ng + 3:2 * ng + 3], names, *rest[2 * ng + 5:])

            @pl.when((pl.program_id(0) == 0) & (kb == 0))
            def _():
                for cp in copies:
                    cp.start()

        @pl.when(kb == 0)
        def _():
            dq_ref[...] = jnp.zeros_like(dq_ref)

        dk_sc[...] = jnp.zeros_like(dk_sc)
        dv_sc[...] = jnp.zeros_like(dv_sc)

        def step(qb, masked):
            trim = masked and tq == tk
            chains = DIAG_CHAINS if trim else ATTN_CHAINS
            sk = tk // chains
            start = pl.multiple_of(qb * tq, tq)
            offs = [u * sk if trim else 0 for u in range(chains)]
            qss = [pl.ds(start + offs[u], tq - offs[u]) for u in range(chains)]
            qts = [q_ref[0, qss[u], :] for u in range(chains)]
            dots = [do_ref[qss[u], :] for u in range(chains)]
            sts = [_dot_nt(k_ref[0, u * sk:(u + 1) * sk, :], qts[u]) for u in range(chains)]
            dpts = [_dot_nt(v_ref[0, u * sk:(u + 1) * sk, :], dots[u]) for u in range(chains)]
            parts = []
            for u in range(chains):
                rows = slice(u * sk, (u + 1) * sk)
                pt = jnp.exp2(sts[u] - lse_ref[0, :, qss[u]] * LOG2E)
                if masked:
                    pt = jnp.where(pq_ref[:, qss[u]] >= pk_ref[rows, :], pt, 0.0)
                dv_sc[rows, :] += _dot(pt.astype(BF16), dots[u])
                dst = (pt * (dpts[u] - delta_ref[0, :, qss[u]])).astype(BF16)
                dk_sc[rows, :] += _dot(dst, qts[u])
                parts.append(_dot_tn(dst, k_ref[0, rows, :]))
            if trim:
                for u in range(chains):
                    dq_ref[0, qss[u], :] += parts[u]
            else:
                dq_ref[0, qss[0], :] += functools.reduce(lambda a, b: a + b, parts)

        q_first = (kb * tk) // tq
        q_clear = (kb * tk + tk - 1) // tq + 1
        lax.fori_loop(q_first, q_clear, lambda qb, c: (step(qb, True), c)[1], 0)
        lax.fori_loop(q_clear, nq, lambda qb, c: (step(qb, False), c)[1], 0)
        dk_ref[0] = dk_sc[...] * (1.0 / LOG2E)
        dv_ref[0] = dv_sc[...]

        @pl.when(kb == nk - 1)
        def _():
            dq_ref[...] = dq_ref[...] * SCALE

        if ng:
            @pl.when((pl.program_id(0) == nh - 1) & (kb == nk - 1))
            def _():
                for cp in copies:
                    cp.wait()

    arrays = [scatter[nm] for nm in names]
    out = pl.pallas_call(
        body, name=name, grid=(nh, nk),
        in_specs=[pl.BlockSpec((1, s, QK_PAD), lambda h, kb: (h, 0, 0)),
                  pl.BlockSpec((1, tk, QK_PAD), lambda h, kb: (h, kb, 0)),
                  pl.BlockSpec((1, tk, V_DIM), lambda h, kb: (h, kb, 0)),
                  pl.BlockSpec((s, V_DIM), lambda h, kb: (0, h)),
                  pl.BlockSpec((1, 1, s), lambda h, kb: (h, 0, 0)),
                  pl.BlockSpec((1, 1, s), lambda h, kb: (h, 0, 0)),
                  pl.BlockSpec((1, s), lambda h, kb: (0, 0)),
                  pl.BlockSpec((tk, 1), lambda h, kb: (kb, 0))] + [ANY] * ng,
        out_specs=[pl.BlockSpec((1, s, QK_PAD), lambda h, kb: (h, 0, 0)),
                   pl.BlockSpec((1, tk, QK_PAD), lambda h, kb: (h, kb, 0)),
                   pl.BlockSpec((1, tk, V_DIM), lambda h, kb: (h, kb, 0))] + [ANY] * ng,
        scratch_shapes=[pltpu.VMEM((tk, QK_PAD), F32), pltpu.VMEM((tk, V_DIM), F32)]
        + ([pltpu.SemaphoreType.DMA((3 * ng,)), pltpu.SemaphoreType.DMA((3 * ng,))] if ng else []),
        out_shape=[jax.ShapeDtypeStruct((nh, s, QK_PAD), F32), jax.ShapeDtypeStruct((nh, s, QK_PAD), F32),
                   jax.ShapeDtypeStruct((nh, s, V_DIM), F32)] + _scatter_out_shapes(names, arrays),
        compiler_params=_cparams("arbitrary", "arbitrary"),
    )(q, k, v, do, lse_row, delta_row, pos_row, pos_col, *arrays)
    return out[0], out[1], out[2], dict(zip(names, out[3:]))


def _loss_head(y, target, *, name):
    s, d = y.shape
    tm = _tile(s, TM_NARROW)
    nt = s // tm

    def body(y_ref, t_ref, dy_ref, loss_ref, acc):
        i = pl.program_id(0)

        @pl.when(i == 0)
        def _():
            acc[...] = jnp.zeros_like(acc)

        e = y_ref[...] - t_ref[...]
        dy_ref[...] = e * (1.0 / d)
        acc[...] += jnp.sum((e * e).reshape(tm // 8, 8, d), axis=0)

        @pl.when(i == nt - 1)
        def _():
            loss_ref[...] = jnp.full((1, 128), 0.5 / d, F32) * jnp.sum(acc[...])

    return pl.pallas_call(
        body, name=name, grid=(nt,),
        in_specs=[pl.BlockSpec((tm, d), lambda i: (i, 0))] * 2,
        out_specs=[pl.BlockSpec((tm, d), lambda i: (i, 0)), pl.BlockSpec((1, 128), lambda i: (0, 0))],
        out_shape=[jax.ShapeDtypeStruct((s, d), F32), jax.ShapeDtypeStruct((1, 128), F32)],
        scratch_shapes=[pltpu.VMEM((8, d), F32)],
        compiler_params=_cparams("arbitrary"),
    )(y, target)


def _adamw(w, g, m, v, *, name):
    r, c = w.shape
    tr = _tile(r, 512) if r % 8 == 0 else r

    def body(w_ref, g_ref, m_ref, v_ref, d_ref, nm_ref, nv_ref, go_ref):
        g_t = g_ref[...]
        go_ref[...] = g_t
        nm = ADAM_B1 * m_ref[...] + (1.0 - ADAM_B1) * g_t
        nv = ADAM_B2 * v_ref[...] + (1.0 - ADAM_B2) * (g_t * g_t)
        m_hat = nm / (1.0 - ADAM_B1 ** ADAM_STEP)
        v_hat = nv / (1.0 - ADAM_B2 ** ADAM_STEP)
        d_ref[...] = -ADAM_LR * (m_hat / (jnp.sqrt(v_hat) + ADAM_EPS) + ADAM_WD * w_ref[...])
        nm_ref[...] = nm
        nv_ref[...] = nv

    spec = pl.BlockSpec((tr, c), lambda i: (i, 0))
    return pl.pallas_call(
        body, name=name, grid=(r // tr,), in_specs=[spec] * 4, out_specs=[spec] * 4,
        out_shape=[jax.ShapeDtypeStruct((r, c), F32)] * 4,
        compiler_params=_cparams("parallel"),
    )(w, g, m, v)


def _place():
    return lax.axis_index("x"), lax.axis_index("y"), lax.axis_index("c")


def _other_chips(x, y):
    return [(1 - x, y), (x, 1 - y), (1 - x, 1 - y)]


BIG = {
    "attn_w_down": ((2, 1024, 448), 1), "attn_w_uq": ((2, 256, 1536), 2), "attn_w_ukv": ((2, 128, 2048), 2),
    "attn_w_o": ((2, 1024, 1024), 1), "conv_w_in": ((2, 1024, 3072), 2),
    "conv_w_out": ((2, 1024, 1024), 1), "mlp_w1": ((4, 1024, 4096), 2), "mlp_w2": ((4, 4096, 1024), 1),
}
CONV_W = (2, 3, 1024)


def _shard_shape(name):
    shape, axis = BIG[name]
    return tuple(n // N_CHIPS if i == axis else n for i, n in enumerate(shape))


def _band(ref, name, layers, chip):
    shape, axis = BIG[name]
    width = shape[axis] // N_CHIPS
    if axis == 1:
        return ref.at[layers, pl.ds(chip * width, width), :]
    return ref.at[layers, :, pl.ds(chip * width, width)]


def _half(name, c):
    hl = BIG[name][0][0] // 2
    return pl.ds(c * hl, hl)


def _place_own(w, nm, chip, *, name):
    shape, axis = BIG[nm]
    layers, rows, cols = w.shape
    tr = _sum_rows(rows, cols)
    nrb = rows // tr
    if axis == 1:
        band = lambda l, i, ch: (l, ch[0] * nrb + i, 0)
    else:
        band = lambda l, i, ch: (l, i, ch[0])

    def body(chip_ref, w_ref, o_ref):
        o_ref[...] = w_ref[...].astype(BF16)

    return pl.pallas_call(
        body, name=name,
        grid_spec=pltpu.PrefetchScalarGridSpec(
            num_scalar_prefetch=1, grid=(layers, nrb),
            in_specs=[pl.BlockSpec((1, tr, cols), lambda l, i, ch: (l, i, 0))],
            out_specs=pl.BlockSpec((1, tr, cols), band)),
        out_shape=jax.ShapeDtypeStruct(shape, BF16),
        compiler_params=_cparams("parallel", "parallel"),
    )(chip, w)


def _gather_copies(outs, names, send_sems, recv_sems, *, base, stride, to_sibling):
    x, y, c = _place()
    me = 2 * x + y

    def copy(k, ref, nm, layers, chip, to):
        band = _band(ref, nm, layers, chip)
        return pltpu.make_async_remote_copy(
            src_ref=band, dst_ref=band, send_sem=send_sems.at[k], recv_sem=recv_sems.at[k],
            device_id=to, device_id_type=MESH)

    sends, recvs = [], []
    for i, nm in enumerate(names):
        for j, (cx, cy) in enumerate(_other_chips(x, y)):
            k = base + stride * i + j
            if to_sibling:
                sends.append(copy(k, outs[i], nm, _half(nm, c), 2 * cx + cy, (x, y, 1 - c)))
                recvs.append(copy(k, outs[i], nm, _half(nm, 1 - c), 2 * cx + cy, (x, y, c)))
            else:
                sends.append(copy(k, outs[i], nm, _half(nm, c), me, (cx, cy, c)))
                recvs.append(copy(k, outs[i], nm, _half(nm, c), 2 * cx + cy, (x, y, c)))
    return sends, recvs


def _gather_ici_copies(outs, names, send_sems, recv_sems, *, base, stride):
    return _gather_copies(outs, names, send_sems, recv_sems, base=base, stride=stride, to_sibling=False)


def _gather_weights(fulls, *, name, ici=True):
    names = list(fulls)
    n = len(names)

    def body(*refs):
        outs = refs[n:2 * n]
        sems = refs[2 * n:]
        sent = []
        if ici:
            sends, recvs = _gather_copies(outs, names, *sems, base=0, stride=6, to_sibling=False)
            for cp in sends:
                cp.start()
            for cp in recvs:
                cp.wait_recv()
            sent += sends
        sends, recvs = _gather_copies(outs, names, *sems, base=3, stride=6, to_sibling=True)
        for cp in sends:
            cp.start()
        for cp in recvs:
            cp.wait_recv()
        for cp in sent + sends:
            cp.wait_send()

    arrays = [fulls[nm] for nm in names]
    out = pl.pallas_call(
        body, name=name, in_specs=[ANY] * n, out_specs=[ANY] * n,
        out_shape=[jax.ShapeDtypeStruct(a.shape, a.dtype) for a in arrays],
        input_output_aliases={i: i for i in range(n)},
        scratch_shapes=[pltpu.SemaphoreType.DMA((6 * n,)), pltpu.SemaphoreType.DMA((6 * n,))],
    )(*arrays)
    return dict(zip(names, out))


def _swap_halves(grads, *, name):
    names = list(grads)
    n = len(names)

    def body(*refs):
        copies = _swap_copies(refs[:n], refs[n:2 * n], names, *refs[2 * n:])
        for cp in copies:
            cp.start()
        for cp in copies:
            cp.wait()

    arrays = [grads[nm] for nm in names]
    out = pl.pallas_call(
        body, name=name, in_specs=[ANY] * n, out_specs=[ANY] * n,
        out_shape=_swap_out_shapes(arrays), scratch_shapes=_swap_sems(n),
    )(*arrays)
    return dict(zip(names, out))


def _swap_copies(ins, outs, names, send_sems, recv_sems):
    x, y, c = _place()
    return [pltpu.make_async_remote_copy(
        src_ref=ins[i].at[_half(nm, 1 - c)], dst_ref=outs[i], send_sem=send_sems.at[i], recv_sem=recv_sems.at[i],
        device_id=(x, y, 1 - c), device_id_type=MESH) for i, nm in enumerate(names)]


def _swap_out_shapes(arrays):
    return [jax.ShapeDtypeStruct((a.shape[0] // 2,) + a.shape[1:], a.dtype) for a in arrays]


def _swap_sems(n):
    return [pltpu.SemaphoreType.DMA((n,)), pltpu.SemaphoreType.DMA((n,))] if n else []


def _all_steps(grid, at):
    cond = None
    for axis, size in enumerate(grid):
        this = pl.program_id(axis) == (0 if at == "first" else size - 1)
        cond = this if cond is None else cond & this
    return cond


def _start_at_first_step(copies, grid):
    @pl.when(_all_steps(grid, "first"))
    def _():
        for cp in copies:
            cp.start()


def _wait_at_last_step(copies, grid):
    @pl.when(_all_steps(grid, "last"))
    def _():
        for cp in copies:
            cp.wait()


def _sum_rows(rows, cols):
    t = rows
    while t * cols * 4 > SUM_BLOCK_BYTES and t % 16 == 0:
        t //= 2
    return t


def _chip_sum(g, r1, core, *, name):
    layers, rows, cols = g.shape
    hl = layers // 2
    tr = _sum_rows(rows, cols)

    def body(core_ref, g_ref, r_ref, o_ref):
        o_ref[...] = (g_ref[...] + r_ref[...]).astype(BF16)

    return pl.pallas_call(
        body, name=name,
        grid_spec=pltpu.PrefetchScalarGridSpec(
            num_scalar_prefetch=1, grid=(hl, rows // tr),
            in_specs=[pl.BlockSpec((1, tr, cols), lambda l, i, cr: (cr[0] * hl + l, i, 0)),
                      pl.BlockSpec((1, tr, cols), lambda l, i, cr: (l, i, 0))],
            out_specs=pl.BlockSpec((1, tr, cols), lambda l, i, cr: (l, i, 0))),
        out_shape=jax.ShapeDtypeStruct((hl, rows, cols), BF16),
        compiler_params=_cparams("parallel", "parallel"),
    )(core, g, r1)


def _chip_partials(grads, names, *, tag):
    core = lax.axis_index("c").astype(jnp.int32).reshape(1)
    r1 = _swap_halves({n: grads[n] for n in names}, name=f"grad_swap_halves_{tag}")
    return r1, {n: _chip_sum(grads[n], r1[n], core, name=f"grad_chip_sum_{n}") for n in names}


def _scatter_partials(partials):
    names = list(partials)
    n = len(names)

    def body(*refs):
        copies = _scatter_copies(refs[:n], refs[n:2 * n], names, *refs[2 * n:])
        for cp in copies:
            cp.start()
        for cp in copies:
            cp.wait()

    arrays = [partials[nm] for nm in names]
    out = pl.pallas_call(
        body, name="grad_scatter_partials", in_specs=[ANY] * n, out_specs=[ANY] * n,
        out_shape=_scatter_out_shapes(names, arrays),
        scratch_shapes=[pltpu.SemaphoreType.DMA((3 * n,)), pltpu.SemaphoreType.DMA((3 * n,))],
    )(*arrays)
    return dict(zip(names, out))


def _scatter_copies(ins, outs, names, send_sems, recv_sems):
    x, y, c = _place()
    copies = []
    for i, nm in enumerate(names):
        for j, (cx, cy) in enumerate(_other_chips(x, y)):
            copies.append(pltpu.make_async_remote_copy(
                src_ref=_band(ins[i], nm, slice(None), 2 * cx + cy), dst_ref=outs[i].at[j],
                send_sem=send_sems.at[3 * i + j], recv_sem=recv_sems.at[3 * i + j],
                device_id=(cx, cy, c), device_id_type=MESH))
    return copies


def _scatter_out_shapes(names, arrays):
    return [jax.ShapeDtypeStruct((3, a.shape[0]) + _shard_shape(nm)[1:], a.dtype) for nm, a in zip(names, arrays)]


def _final_sum(g, r1, r2, place, nm, *, name):
    (layers, _, _), axis = BIG[nm]
    hl = layers // 2
    _, rows, cols = _shard_shape(nm)
    tr = _sum_rows(rows, cols)
    nrb = rows // tr
    if axis == 1:
        blk = lambda l, i, pc: (l, pc[1] * nrb + i, 0)
    else:
        blk = lambda l, i, pc: (l, i, pc[1])

    def body(place_ref, g_ref, r1_ref, r2_ref, o_ref):
        acc = g_ref[...] + r1_ref[...]
        for j in range(3):
            acc = acc + r2_ref[j].astype(F32)
        o_ref[...] = acc

    return pl.pallas_call(
        body, name=name,
        grid_spec=pltpu.PrefetchScalarGridSpec(
            num_scalar_prefetch=1, grid=(hl, nrb),
            in_specs=[pl.BlockSpec((1, tr, cols), lambda l, i, pc: blk(pc[0] * hl + l, i, pc)),
                      pl.BlockSpec((1, tr, cols), lambda l, i, pc: blk(l, i, pc)),
                      pl.BlockSpec((3, 1, tr, cols), lambda l, i, pc: (0, l, i, 0))],
            out_specs=pl.BlockSpec((1, tr, cols), lambda l, i, pc: (pc[0] * hl + l, i, 0))),
        out_shape=jax.ShapeDtypeStruct((layers, rows, cols), F32),
        compiler_params=_cparams("parallel", "parallel"),
    )(place, g, r1, r2)


def _join_halves(shards):
    names = list(shards)
    n = len(names)

    def body(*refs):
        outs = refs[n:2 * n]
        send_sems, recv_sems = refs[2 * n:]
        x, y, c = _place()
        copies = []
        for i, nm in enumerate(names):
            mine = outs[i].at[_half(nm, c)]
            cp = pltpu.make_async_remote_copy(
                src_ref=mine, dst_ref=mine, send_sem=send_sems.at[i], recv_sem=recv_sems.at[i],
                device_id=(x, y, 1 - c), device_id_type=MESH)
            cp.start()
            copies.append(cp)
        for i, nm in enumerate(names):
            theirs = outs[i].at[_half(nm, 1 - c)]
            pltpu.make_async_remote_copy(
                src_ref=theirs, dst_ref=theirs, send_sem=send_sems.at[i], recv_sem=recv_sems.at[i],
                device_id=(x, y, 1 - c), device_id_type=MESH).wait_recv()
        for cp in copies:
            cp.wait_send()

    arrays = [shards[nm] for nm in names]
    out = pl.pallas_call(
        body, name="grad_join_halves", in_specs=[ANY] * n, out_specs=[ANY] * n,
        out_shape=[jax.ShapeDtypeStruct(a.shape, a.dtype) for a in arrays],
        input_output_aliases={i: i for i in range(n)},
        scratch_shapes=[pltpu.SemaphoreType.DMA((n,)), pltpu.SemaphoreType.DMA((n,))],
    )(*arrays)
    return dict(zip(names, out))


def _all_reduce_small(part, *, name):
    rows, cols = part.shape
    vm = pl.BlockSpec(memory_space=pltpu.VMEM)

    def body(p_ref, o_ref, land, send_sems, recv_sems):
        x, y, c = _place()
        me = 4 * x + 2 * y + c
        flips = [(dx, dy, dc) for dx in (0, 1) for dy in (0, 1) for dc in (0, 1)][1:]
        copies = []
        for k, (dx, dy, dc) in enumerate(flips):
            cp = pltpu.make_async_remote_copy(
                src_ref=p_ref, dst_ref=land.at[me], send_sem=send_sems.at[k], recv_sem=recv_sems.at[k],
                device_id=(1 - x if dx else x, 1 - y if dy else y, 1 - c if dc else c), device_id_type=MESH)
            cp.start()
            copies.append(cp)
        land[me] = p_ref[...]
        for cp in copies:
            cp.wait()
        acc = land[0]
        for j in range(1, 8):
            acc = acc + land[j]
        o_ref[...] = acc

    return pl.pallas_call(
        body, name=name, in_specs=[vm], out_specs=vm,
        out_shape=jax.ShapeDtypeStruct((rows, cols), F32),
        scratch_shapes=[pltpu.VMEM((8, rows, cols), F32), pltpu.SemaphoreType.DMA((7,)), pltpu.SemaphoreType.DMA((7,))],
    )(part)


SMALL = {"g_mix": (4, 1024), "g_mlp": (4, 1024), "attn_g_q_a": (2, 256), "attn_g_kv_a": (2, 128),
         "attn_g_qnorm": (2, 192), "attn_g_knorm": (2, 192)}
SMALL_GRADS = {**SMALL, "conv_w": CONV_W}
WEIGHT_ORDER = ["g_mix", "g_mlp", "attn_w_down", "attn_g_q_a", "attn_g_kv_a", "attn_w_uq", "attn_w_ukv",
                "attn_g_qnorm", "attn_g_knorm", "attn_w_o", "conv_w_in", "conv_w", "conv_w_out", "mlp_w1", "mlp_w2"]


def _prod(shape):
    n = 1
    for v in shape:
        n *= v
    return n


def _pack_small(parts, table):
    flat = [parts[n].reshape(-1) for n in table]
    size = sum(_prod(s) for s in table.values())
    rows = -(-size // (8 * 128)) * 8
    flat.append(jnp.zeros((rows * 128 - size,), F32))
    return jnp.concatenate(flat).reshape(rows, 128)


def _unpack_small(buf, table):
    flat = buf.reshape(-1)
    out, off = {}, 0
    for n, shp in table.items():
        out[n] = flat[off:off + _prod(shp)].reshape(shp)
        off += _prod(shp)
    return out


def _rope_tables(positions):
    inv_freq = ROPE_THETA ** (-jnp.arange(0, QK_ROPE, 2, dtype=F32) / QK_ROPE)
    ang = positions.astype(F32)[:, None] * inv_freq
    cos, sin = jnp.cos(ang), jnp.sin(ang)
    z32 = jnp.zeros_like(cos)
    z64 = jnp.zeros((positions.shape[0], 64), F32)
    cc = jnp.concatenate([cos, cos, z64], axis=1)
    sa = jnp.concatenate([-sin, z32, z64], axis=1)
    sb = jnp.concatenate([z32, sin, z64], axis=1)
    return cc, sa, sb


def _pad_heads(w, width):
    k = w.shape[0]
    w = w.reshape(k, N_HEADS, width)
    return jnp.pad(w, ((0, 0), (0, 0), (0, QK_PAD - width))).reshape(k, N_HEADS * QK_PAD)


EARLY = ("mlp_w1", "mlp_w2", "conv_w_in", "conv_w_out")
LATE = ("attn_w_down", "attn_w_uq", "attn_w_ukv", "attn_w_o")
GATHER_LATER = EARLY


def _local_step(x, positions, target, wb, gains, later=None):
    s = x.shape[0]
    cc, sa, sb = _rope_tables(positions)
    pos_col = positions.reshape(s, 1)
    pos_row = positions.reshape(1, s)

    saved = []
    for i in range(4):
        j = i // 2
        g_mix = gains["g_mix"][i:i + 1]
        g_mlp = gains["g_mlp"][i:i + 1]
        if i % 2 == 0:
            w_down = jnp.pad(wb["attn_w_down"][j], ((0, 0), (0, DOWN_PAD - DOWN_DIM)))
            w_uq = _pad_heads(wb["attn_w_uq"][j], QK_DIM)
            w_ukv = wb["attn_w_ukv"][j]
            g_qa = gains["attn_g_q_a"][j:j + 1]
            g_kva = gains["attn_g_kv_a"][j:j + 1]
            g_q = jnp.pad(gains["attn_g_qnorm"][j:j + 1], ((0, 0), (0, QK_PAD - QK_DIM)))
            g_k = jnp.pad(gains["attn_g_knorm"][j:j + 1], ((0, 0), (0, QK_PAD - QK_DIM)))
            h, a = _norm_mm(x, g_mix, w_down, out_dtype=F32, name=f"mla_down_{j}")
            cq, ckv, q, k, v = _mla_prep(a, g_qa, g_kva, w_uq, w_ukv, g_q, g_k, cc, sa, sb, name=f"mla_prep_{j}")
            o, lse, got = _flash_fwd(q, k, v, pos_col, pos_row, name=f"flash_fwd_{j}",
                                     gather=later if i == 0 else None)
            if got:
                wb = {**wb, **_gather_weights(got, name="gather_later_forward", ici=False)}
            x_mid = _mm_nn(o, wb["attn_w_o"], layer=j, out_dtype=F32, residual=x, name=f"mla_out_{j}")
            mix = dict(h=h, a=a, cq=cq, ckv=ckv, q=q, k=k, v=v, o=o, lse=lse, w_down=w_down, w_uq=w_uq, w_ukv=w_ukv,
                       g_qa=g_qa, g_kva=g_kva, g_q=g_q, g_k=g_k)
        else:
            h, bcu = _norm_mm(x, g_mix, wb["conv_w_in"], layer=j, out_dtype=BF16, name=f"conv_in_{j}")
            z = _conv_gate(bcu, gains["conv_w"][j], name=f"conv_gate_{j}")
            x_mid = _mm_nn(z, wb["conv_w_out"], layer=j, out_dtype=F32, residual=x, name=f"conv_out_{j}")
            mix = dict(h=h, bcu=bcu, z=z)
        h2, u, x_out = _mlp_fwd(x_mid, g_mlp, wb["mlp_w1"], wb["mlp_w2"], layer=i, name=f"mlp_fwd_{i}")
        saved.append(dict(x_in=x, x_mid=x_mid, mix=mix, h2=h2, u=u, g_mix=g_mix, g_mlp=g_mlp))
        x = x_out

    dx, loss = _loss_head(x, target, name="loss_head")

    gw = {n: None for n in BIG}
    exchanged = None
    g_uq = [None, None]
    gs = {n: [None] * SMALL_GRADS[n][0] for n in SMALL_GRADS}

    def wgrad(nm, layer, a, b, **kw):
        out = _mm_tn(a, b, stack=gw[nm], layer=layer, layers=BIG[nm][0][0], name=f"{nm}_grad_{layer}", **kw)
        gw[nm], arrived = out if kw.get("swap") else (out, None)
        return arrived

    for i in reversed(range(4)):
        j = i // 2
        sv = saved[i]
        mix = sv["mix"]
        ride = i == 0 and later is not None
        du = _mlp_down_bwd(dx, wb["mlp_w2"], sv["u"], layer=i, name=f"mlp_down_bwd_{i}")
        wgrad("mlp_w2", i, sv["u"], dx, sqrelu_a=True)
        r1_early = wgrad("mlp_w1", i, sv["h2"], du,
                         swap={n: gw[n] for n in ("mlp_w2", "conv_w_in", "conv_w_out")} if ride else None)
        dx, dg, *arrived = _nt_rms_bwd(du, wb["mlp_w1"], sv["x_mid"], sv["g_mlp"], dx, layer=i, name=f"mlp_up_bwd_{i}",
                                       swap={"mlp_w1": gw["mlp_w1"]} if ride else None)
        if ride:
            r1_early.update(arrived[0])
        gs["g_mlp"][i] = dg[0]
        if i % 2 == 0:
            do, delta_row = _attn_out_bwd(dx, wb["attn_w_o"], mix["o"], layer=j, name=f"mla_out_bwd_{j}")
            wgrad("attn_w_o", j, mix["o"], dx)
            lse_row = mix["lse"]
            partials = None
            if ride:
                core = lax.axis_index("c").astype(jnp.int32).reshape(1)
                partials = {n: _chip_sum(gw[n], r1_early[n], core, name=f"grad_chip_sum_{n}") for n in EARLY}
            dq, dk, dv, arrived = _flash_bwd(mix["q"], mix["k"], mix["v"], do, lse_row, delta_row, pos_col, pos_row,
                                             name=f"flash_bwd_{j}", scatter=partials)
            if partials is not None:
                exchanged = (r1_early, arrived)
            dqr, dkvr, da, dgq, dgk, dgqa, dgkva = _mla_prep_bwd(
                mix["a"], mix["g_qa"], mix["g_kva"], mix["w_uq"], mix["w_ukv"], mix["g_q"], mix["g_k"], cc, sa, sb,
                dq, dk, dv, name=f"mla_prep_bwd_{j}")
            g_uq[j] = _mm_tn(mix["cq"], dqr, name=f"attn_w_uq_grad_{j}")[0]
            wgrad("attn_w_ukv", j, mix["ckv"], dkvr)
            wgrad("attn_w_down", j, mix["h"], da, keep=DOWN_DIM)
            dx, dg = _nt_rms_bwd(da, mix["w_down"], sv["x_in"], sv["g_mix"], dx, name=f"mla_down_bwd_{j}")
            gs["attn_g_qnorm"][j] = dgq[0, :QK_DIM]
            gs["attn_g_knorm"][j] = dgk[0, :QK_DIM]
            gs["attn_g_q_a"][j] = dgqa[0]
            gs["attn_g_kv_a"][j] = dgkva[0]
        else:
            dz = _mm_nt(dx, wb["conv_w_out"], layer=j, out_dtype=F32, name=f"conv_out_bwd_{j}")
            wgrad("conv_w_out", j, mix["z"], dx)
            dbcu, dcw = _conv_gate_bwd(mix["bcu"], dz, gains["conv_w"][j], name=f"conv_gate_bwd_{j}")
            gs["conv_w"][j] = dcw
            wgrad("conv_w_in", j, mix["h"], dbcu)
            dx, dg = _nt_rms_bwd(dbcu, wb["conv_w_in"], sv["x_in"], sv["g_mix"], dx, layer=j, name=f"conv_in_bwd_{j}")
        gs["g_mix"][i] = dg[0]

    gw["attn_w_uq"] = jnp.stack(g_uq).reshape(2, Q_RANK, N_HEADS, QK_PAD)[..., :QK_DIM].reshape(BIG["attn_w_uq"][0])
    grads_small = {n: jnp.stack(v) for n, v in gs.items()}
    return loss, dx, gw, grads_small, exchanged


def kernel(x, positions, g_mix, g_mlp, attn_w_down, attn_g_q_a, attn_g_kv_a, attn_w_uq, attn_w_ukv, attn_g_qnorm, attn_g_knorm, attn_w_o, conv_w_in, conv_w, conv_w_out, mlp_w1, mlp_w2, loss_target, m_g_mix, m_g_mlp, m_attn_w_down, m_attn_g_q_a, m_attn_g_kv_a, m_attn_w_uq, m_attn_w_ukv, m_attn_g_qnorm, m_attn_g_knorm, m_attn_w_o, m_conv_w_in, m_conv_w, m_conv_w_out, m_mlp_w1, m_mlp_w2, v_g_mix, v_g_mlp, v_attn_w_down, v_attn_g_q_a, v_attn_g_kv_a, v_attn_w_uq, v_attn_w_ukv, v_attn_g_qnorm, v_attn_g_knorm, v_attn_w_o, v_conv_w_in, v_conv_w, v_conv_w_out, v_mlp_w1, v_mlp_w2):
    w = dict(g_mix=g_mix, g_mlp=g_mlp, attn_w_down=attn_w_down, attn_g_q_a=attn_g_q_a, attn_g_kv_a=attn_g_kv_a,
             attn_w_uq=attn_w_uq, attn_w_ukv=attn_w_ukv, attn_g_qnorm=attn_g_qnorm, attn_g_knorm=attn_g_knorm,
             attn_w_o=attn_w_o, conv_w_in=conv_w_in, conv_w=conv_w, conv_w_out=conv_w_out, mlp_w1=mlp_w1, mlp_w2=mlp_w2)
    m = dict(g_mix=m_g_mix, g_mlp=m_g_mlp, attn_w_down=m_attn_w_down, attn_g_q_a=m_attn_g_q_a,
             attn_g_kv_a=m_attn_g_kv_a, attn_w_uq=m_attn_w_uq, attn_w_ukv=m_attn_w_ukv, attn_g_qnorm=m_attn_g_qnorm,
             attn_g_knorm=m_attn_g_knorm, attn_w_o=m_attn_w_o, conv_w_in=m_conv_w_in, conv_w=m_conv_w,
             conv_w_out=m_conv_w_out, mlp_w1=m_mlp_w1, mlp_w2=m_mlp_w2)
    v = dict(g_mix=v_g_mix, g_mlp=v_g_mlp, attn_w_down=v_attn_w_down, attn_g_q_a=v_attn_g_q_a,
             attn_g_kv_a=v_attn_g_kv_a, attn_w_uq=v_attn_w_uq, attn_w_ukv=v_attn_w_ukv, attn_g_qnorm=v_attn_g_qnorm,
             attn_g_knorm=v_attn_g_knorm, attn_w_o=v_attn_w_o, conv_w_in=v_conv_w_in, conv_w=v_conv_w,
             conv_w_out=v_conv_w_out, mlp_w1=v_mlp_w1, mlp_w2=v_mlp_w2)
    cx, cy, cc_ = _place()

    chip = 2 * cx + cy

    def own_offset(shape, axis):
        return tuple(chip * (shape[axis] // N_CHIPS) if i == axis else 0 for i in range(3))

    chip_arr = chip.astype(jnp.int32).reshape(1)
    fulls = {n: _place_own(w[n], n, chip_arr, name=f"place_{n}") for n in BIG}
    later = {n: fulls.pop(n) for n in GATHER_LATER}
    wb = _gather_weights(fulls, name="gather_weights")

    placed = lax.dynamic_update_slice(jnp.zeros(CONV_W, F32), conv_w, own_offset(CONV_W, 2))
    conv_w_full = 0.5 * _all_reduce_small(placed.reshape(-1, 128), name="conv_w_gather").reshape(CONV_W)

    gains = {n: w[n] for n in SMALL}
    gains["conv_w"] = conv_w_full

    loss, grad_x, grads_big, grads_small, (r1_early, r2_early) = _local_step(
        x[0], positions[0], loss_target[0], wb, gains, later)

    place = jnp.stack([cc_, chip]).astype(jnp.int32)
    r1_late, partials = _chip_partials(grads_big, LATE, tag="late")
    r1 = {**r1_early, **r1_late}
    r2 = {**r2_early, **_scatter_partials(partials)}
    halves = {n: _final_sum(grads_big[n], r1[n], r2[n], place, n, name=f"grad_final_sum_{n}") for n in BIG}
    grad_shards = _join_halves(halves)

    small = _unpack_small(_all_reduce_small(_pack_small(grads_small, SMALL_GRADS), name="gain_all_reduce"), SMALL_GRADS)
    grad_shards["conv_w"] = lax.dynamic_slice(small["conv_w"], own_offset(CONV_W, 2), conv_w.shape)

    loss_total = lax.psum(loss[0, 0], ("x", "y", "c"))

    grads, deltas, new_m, new_v = {}, {}, {}, {}
    for n in [*BIG, "conv_w"]:
        shp = w[n].shape
        two_d = (shp[0] * shp[1], shp[2])
        g2 = grad_shards[n].reshape(two_d)
        d, nm, nv, g = _adamw(w[n].reshape(two_d), g2, m[n].reshape(two_d), v[n].reshape(two_d), name=f"adamw_{n}")
        grads[n], deltas[n], new_m[n], new_v[n] = g.reshape(shp), d.reshape(shp), nm.reshape(shp), nv.reshape(shp)
    d, nm, nv, g = _adamw(_pack_small(w, SMALL), _pack_small(small, SMALL), _pack_small(m, SMALL),
                          _pack_small(v, SMALL), name="adamw_gains")
    d, nm, nv, g = (_unpack_small(t, SMALL) for t in (d, nm, nv, g))
    for n in SMALL:
        grads[n], deltas[n], new_m[n], new_v[n] = g[n], d[n], nm[n], nv[n]

    return (loss_total, grad_x[None],
            *[grads[n] for n in WEIGHT_ORDER], *[deltas[n] for n in WEIGHT_ORDER],
            *[new_m[n] for n in WEIGHT_ORDER], *[new_v[n] for n in WEIGHT_ORDER])
```

```python
import functools

import jax
import jax.numpy as jnp
from jax import lax
from jax.experimental import pallas as pl
from jax.experimental.pallas import tpu as pltpu

F32 = jnp.float32
BF16 = jnp.bfloat16

D_MODEL = 1024
N_HEADS = 8
QK_NOPE = 128
QK_ROPE = 64
QK_DIM = QK_NOPE + QK_ROPE
QK_PAD = 256
V_DIM = 128
Q_RANK = 256
KV_RANK = 128
DOWN_DIM = Q_RANK + KV_RANK + QK_ROPE
DOWN_PAD = 512
ROPE_THETA = 10000.0
EPS = 1e-6
NEG = -1e30
SCALE = QK_DIM ** -0.5
SCALE_LOG2E = SCALE * 1.4426950408889634
LOG2E = 1.4426950408889634
ATTN_CHAINS = 2
DIAG_CHAINS = 4

ADAM_LR = 0.001
ADAM_B1 = 0.9
ADAM_B2 = 0.999
ADAM_EPS = 1e-08
ADAM_WD = 0.01
ADAM_STEP = 10

N_CHIPS = 4
MESH = pl.DeviceIdType.MESH
ANY = pl.BlockSpec(memory_space=pl.ANY)

TM = 512
TM_NARROW = 1024
TM_WIDE = 512
FWD_TQ = 1024
FWD_TK = 1024
BWD_TQ = 1024
BWD_TK = 1024
HALO = 16
T_PREP = 1024
T_PREP_BWD = 512
T_RED = 2048
SUM_BLOCK_BYTES = 4 * 1024 * 1024


def _tile(n, pref):
    t = min(n, pref)
    assert n % t == 0, (n, t)
    return t


def _cparams(*sem):
    return pltpu.CompilerParams(dimension_semantics=sem)


def _dot(a, b):
    return jnp.dot(a, b, preferred_element_type=F32)


def _dot_nt(a, b):
    return lax.dot_general(a, b, (((1,), (1,)), ((), ())), preferred_element_type=F32)


def _dot_tn(a, b):
    return lax.dot_general(a, b, (((0,), (0,)), ((), ())), preferred_element_type=F32)


def _rms(x, width):
    r = lax.rsqrt(jnp.sum(x * x, axis=-1, keepdims=True) * (1.0 / width) + EPS)
    return x * r, r


def _rms_bwd(xhat, r, dxhat, width):
    return r * (dxhat - xhat * (jnp.sum(dxhat * xhat, axis=-1, keepdims=True) * (1.0 / width)))


def _rope(t, cc, sa, sb):
    return t * cc + pltpu.roll(t, 96, 1) * sa + pltpu.roll(t, 32, 1) * sb


def _rope_t(g, cc, sa, sb):
    return g * cc + pltpu.roll(g * sa, 32, 1) + pltpu.roll(g * sb, 96, 1)


def _wspec(w, layer):
    once = pl.Buffered(1)
    if w.ndim == 2:
        return pl.BlockSpec(w.shape, lambda *_: (0, 0), pipeline_mode=once)
    return pl.BlockSpec((None,) + w.shape[1:], lambda *_: (layer, 0, 0), pipeline_mode=once)


def _mm_nn(a, b, *, out_dtype, name, residual=None, layer=0, forward=None):
    m, k = a.shape
    n = b.shape[-1]
    tm = _tile(m, TM_NARROW)
    grid = (m // tm,)
    names = list(forward or {})
    nf = len(names)
    n_in = 2 + (residual is not None)

    def body(*refs):
        a_ref, b_ref = refs[:2]
        o_ref = refs[n_in + nf]
        if nf:
            sends, recvs = _gather_copies(refs[n_in + nf + 1:n_in + 2 * nf + 1], names, *refs[n_in + 2 * nf + 1:],
                                          base=0, stride=3, to_sibling=True)
            _start_at_first_step(sends, grid)
        acc = _dot(a_ref[...].astype(BF16), b_ref[...])
        if residual is not None:
            acc = acc + refs[2][...]
        o_ref[...] = acc.astype(o_ref.dtype)
        if nf:
            @pl.when(_all_steps(grid, "last"))
            def _():
                for cp in recvs:
                    cp.wait_recv()
                for cp in sends:
                    cp.wait_send()

    in_specs = [pl.BlockSpec((tm, k), lambda i: (i, 0)), _wspec(b, layer)]
    args = [a, b]
    if residual is not None:
        in_specs.append(pl.BlockSpec((tm, n), lambda i: (i, 0)))
        args.append(residual)
    arrays = [forward[nm] for nm in names]
    out = pl.pallas_call(
        body, name=name, grid=grid, in_specs=in_specs + [ANY] * nf,
        out_specs=[pl.BlockSpec((tm, n), lambda i: (i, 0))] + [ANY] * nf,
        out_shape=[jax.ShapeDtypeStruct((m, n), out_dtype)] + [jax.ShapeDtypeStruct(t.shape, t.dtype) for t in arrays],
        scratch_shapes=[pltpu.SemaphoreType.DMA((3 * nf,)), pltpu.SemaphoreType.DMA((3 * nf,))] if nf else [],
        input_output_aliases={n_in + i: 1 + i for i in range(nf)},
        compiler_params=_cparams("arbitrary" if nf else "parallel"),
    )(*args, *arrays)
    return (out[0], dict(zip(names, out[1:]))) if nf else out[0]


def _mm_nt(a, b, *, out_dtype, name, layer=0):
    m, k = a.shape
    n = b.shape[-2]
    tm = _tile(m, TM_NARROW)

    def body(a_ref, b_ref, o_ref):
        o_ref[...] = _dot_nt(a_ref[...].astype(BF16), b_ref[...]).astype(o_ref.dtype)

    return pl.pallas_call(
        body, name=name, grid=(m // tm,),
        in_specs=[pl.BlockSpec((tm, k), lambda i: (i, 0)), _wspec(b, layer)],
        out_specs=pl.BlockSpec((tm, n), lambda i: (i, 0)),
        out_shape=jax.ShapeDtypeStruct((m, n), out_dtype),
        compiler_params=_cparams("parallel"),
    )(a, b)


def _mm_tn(a, b, *, name, stack=None, layer=0, layers=1, keep=None, sqrelu_a=False, swap=None):
    s, ka = a.shape
    n = b.shape[1]
    ts = _tile(s, T_RED)
    tka = _tile(ka, 1024)
    tn = _tile(n, 1024)
    n_out = n if keep is None else keep
    assert keep is None or tn == n
    grid = (ka // tka, n // tn, s // ts)
    names = list(swap or {})
    ns = len(names)
    n_in = 2 + (stack is not None)

    def body(*refs):
        a_ref, b_ref = refs[:2]
        o_ref = refs[n_in + ns]
        if ns:
            copies = _swap_copies(refs[n_in:n_in + ns], refs[n_in + ns + 1:n_in + 2 * ns + 1], names,
                                  *refs[n_in + 2 * ns + 1:])
            _start_at_first_step(copies, grid)

        @pl.when(pl.program_id(2) == 0)
        def _():
            o_ref[...] = jnp.zeros_like(o_ref)

        a_t = a_ref[...]
        if sqrelu_a:
            a_t = _sqrelu(a_t.astype(F32))
        o_ref[...] += _dot_tn(a_t.astype(BF16), b_ref[...].astype(BF16))[:, :n_out if keep else tn]
        if ns:
            _wait_at_last_step(copies, grid)

    in_specs = [pl.BlockSpec((ts, tka), lambda i, j, t: (t, i)), pl.BlockSpec((ts, tn), lambda i, j, t: (t, j))]
    args = [a, b]
    if stack is not None:
        in_specs.append(ANY)
        args.append(stack)
    sent = [swap[nm] for nm in names]
    out = pl.pallas_call(
        body, name=name, grid=grid, in_specs=in_specs + [ANY] * ns,
        out_specs=[pl.BlockSpec((None, tka, tn if keep is None else keep), lambda i, j, t: (layer, i, j))] + [ANY] * ns,
        out_shape=[jax.ShapeDtypeStruct((layers, ka, n_out), F32)] + _swap_out_shapes(sent),
        scratch_shapes=_swap_sems(ns),
        input_output_aliases={} if stack is None else {2: 0},
        compiler_params=_cparams(*(["arbitrary"] * 3 if ns else ["parallel", "parallel", "arbitrary"])),
    )(*args, *sent)
    return (out[0], dict(zip(names, out[1:]))) if ns else out[0]


def _norm_mm(x, g, w, *, out_dtype, name, layer=0):
    s, d = x.shape
    n = w.shape[-1]
    tm = _tile(s, TM_NARROW)

    def body(x_ref, g_ref, w_ref, h_ref, o_ref):
        xhat, _ = _rms(x_ref[...], d)
        h = (xhat * g_ref[...]).astype(BF16)
        h_ref[...] = h
        o_ref[...] = _dot(h, w_ref[...]).astype(o_ref.dtype)

    return pl.pallas_call(
        body, name=name, grid=(s // tm,),
        in_specs=[pl.BlockSpec((tm, d), lambda i: (i, 0)), pl.BlockSpec((1, d), lambda i: (0, 0)), _wspec(w, layer)],
        out_specs=[pl.BlockSpec((tm, d), lambda i: (i, 0)), pl.BlockSpec((tm, n), lambda i: (i, 0))],
        out_shape=[jax.ShapeDtypeStruct((s, d), BF16), jax.ShapeDtypeStruct((s, n), out_dtype)],
        compiler_params=_cparams("parallel"),
    )(x, g, w)


def _nt_rms_bwd(dy, w, x, g, dres, *, name, layer=0, swap=None):
    s, n = dy.shape
    d = x.shape[1]
    tm = _tile(s, TM if n > 3072 else TM_NARROW)
    grid = (s // tm,)
    names = list(swap or {})
    ns = len(names)

    def body(dy_ref, w_ref, x_ref, g_ref, dres_ref, *rest):
        dx_ref, dg_ref = rest[ns:ns + 2]
        if ns:
            copies = _swap_copies(rest[:ns], rest[ns + 2:2 * ns + 2], names, *rest[2 * ns + 2:])
            _start_at_first_step(copies, grid)

        @pl.when(pl.program_id(0) == 0)
        def _():
            dg_ref[...] = jnp.zeros_like(dg_ref)

        dh = _dot_nt(dy_ref[...], w_ref[...])
        xhat, r = _rms(x_ref[...], d)
        dg_ref[...] += jnp.sum(dh * xhat, axis=0, keepdims=True)
        dx_ref[...] = dres_ref[...] + _rms_bwd(xhat, r, dh * g_ref[...], d)
        if ns:
            _wait_at_last_step(copies, grid)

    sent = [swap[nm] for nm in names]
    out = pl.pallas_call(
        body, name=name, grid=grid,
        in_specs=[pl.BlockSpec((tm, n), lambda i: (i, 0)), _wspec(w, layer),
                  pl.BlockSpec((tm, d), lambda i: (i, 0)), pl.BlockSpec((1, d), lambda i: (0, 0)),
                  pl.BlockSpec((tm, d), lambda i: (i, 0))] + [ANY] * ns,
        out_specs=[pl.BlockSpec((tm, d), lambda i: (i, 0)), pl.BlockSpec((1, d), lambda i: (0, 0))] + [ANY] * ns,
        out_shape=[jax.ShapeDtypeStruct((s, d), F32), jax.ShapeDtypeStruct((1, d), F32)] + _swap_out_shapes(sent),
        scratch_shapes=_swap_sems(ns),
        compiler_params=_cparams("arbitrary"),
    )(dy, w, x, g, dres, *sent)
    return (out[0], out[1], dict(zip(names, out[2:]))) if ns else (out[0], out[1])


def _sqrelu(u):
    return jnp.square(jnp.maximum(u, 0.0))


def _mlp_fwd(x, g, w1, w2, *, name, layer=0):
    s, d = x.shape
    n = w1.shape[-1]
    tm = _tile(s, TM_WIDE)

    def body(x_ref, g_ref, w1_ref, w2_ref, h_ref, u_ref, y_ref):
        x_t = x_ref[...]
        xhat, _ = _rms(x_t, d)
        h = (xhat * g_ref[...]).astype(BF16)
        h_ref[...] = h
        u = _dot(h, w1_ref[...])
        u_ref[...] = u.astype(BF16)
        y_ref[...] = x_t + _dot(_sqrelu(u).astype(BF16), w2_ref[...])

    return pl.pallas_call(
        body, name=name, grid=(s // tm,),
        in_specs=[pl.BlockSpec((tm, d), lambda i: (i, 0)), pl.BlockSpec((1, d), lambda i: (0, 0)),
                  _wspec(w1, layer), _wspec(w2, layer)],
        out_specs=[pl.BlockSpec((tm, d), lambda i: (i, 0)), pl.BlockSpec((tm, n), lambda i: (i, 0)),
                   pl.BlockSpec((tm, d), lambda i: (i, 0))],
        out_shape=[jax.ShapeDtypeStruct((s, d), BF16), jax.ShapeDtypeStruct((s, n), BF16),
                   jax.ShapeDtypeStruct((s, d), F32)],
        compiler_params=_cparams("parallel"),
    )(x, g, w1, w2)


def _mlp_down_bwd(dy, w2, u, *, name, layer=0):
    s, d = dy.shape
    n = w2.shape[-2]
    tm = _tile(s, TM_WIDE)

    def body(dy_ref, w_ref, u_ref, du_ref):
        dact = _dot_nt(dy_ref[...].astype(BF16), w_ref[...])
        du_ref[...] = (dact * (2.0 * jnp.maximum(u_ref[...].astype(F32), 0.0))).astype(BF16)

    return pl.pallas_call(
        body, name=name, grid=(s // tm,),
        in_specs=[pl.BlockSpec((tm, d), lambda i: (i, 0)), _wspec(w2, layer),
                  pl.BlockSpec((tm, n), lambda i: (i, 0))],
        out_specs=pl.BlockSpec((tm, n), lambda i: (i, 0)),
        out_shape=jax.ShapeDtypeStruct((s, n), BF16),
        compiler_params=_cparams("parallel"),
    )(dy, w2, u)


def _conv_gate(bcu, conv_w, *, name):
    s = bcu.shape[0]
    d = D_MODEL
    tm = _tile(s, TM)
    hb = tm // HALO

    def body(bcu_ref, prev_ref, w_ref, z_ref, pbuf):
        i = pl.program_id(0)
        gb = bcu_ref[:, 0:d].astype(F32)
        p = bcu_ref[:, d:2 * d].astype(F32) * bcu_ref[:, 2 * d:3 * d].astype(F32)
        pprev = prev_ref[:, d:2 * d].astype(F32) * prev_ref[:, 2 * d:3 * d].astype(F32)
        pbuf[0:HALO, :] = jnp.where(i > 0, pprev, 0.0)
        pbuf[HALO:HALO + tm, :] = p
        cv = (w_ref[2:3, :] * p + w_ref[1:2, :] * pbuf[HALO - 1:HALO - 1 + tm, :]
              + w_ref[0:1, :] * pbuf[HALO - 2:HALO - 2 + tm, :])
        z_ref[...] = (gb * cv).astype(BF16)

    return pl.pallas_call(
        body, name=name, grid=(s // tm,),
        in_specs=[pl.BlockSpec((tm, 3 * d), lambda i: (i, 0)),
                  pl.BlockSpec((HALO, 3 * d), lambda i: (jnp.maximum(i * hb - 1, 0), 0)),
                  pl.BlockSpec((3, d), lambda i: (0, 0))],
        out_specs=pl.BlockSpec((tm, d), lambda i: (i, 0)),
        out_shape=jax.ShapeDtypeStruct((s, d), BF16),
        scratch_shapes=[pltpu.VMEM((tm + HALO, d), F32)],
        compiler_params=_cparams("parallel"),
    )(bcu, bcu, conv_w)


def _conv_gate_bwd(bcu, dz, conv_w, *, name):
    s = bcu.shape[0]
    d = D_MODEL
    tm = _tile(s, TM)
    hb = tm // HALO
    nt = s // tm

    def body(bcu_ref, prev_ref, next_ref, dz_ref, dznext_ref, w_ref, dbcu_ref, dw_ref, pbuf, dbuf):
        i = pl.program_id(0)

        @pl.when(i == 0)
        def _():
            dw_ref[...] = jnp.zeros_like(dw_ref)

        gb = bcu_ref[:, 0:d].astype(F32)
        gc = bcu_ref[:, d:2 * d].astype(F32)
        uu = bcu_ref[:, 2 * d:3 * d].astype(F32)
        p = gc * uu
        pprev = prev_ref[:, d:2 * d].astype(F32) * prev_ref[:, 2 * d:3 * d].astype(F32)
        pbuf[0:HALO, :] = jnp.where(i > 0, pprev, 0.0)
        pbuf[HALO:HALO + tm, :] = p
        p1 = pbuf[HALO - 1:HALO - 1 + tm, :]
        p2 = pbuf[HALO - 2:HALO - 2 + tm, :]
        cv = w_ref[2:3, :] * p + w_ref[1:2, :] * p1 + w_ref[0:1, :] * p2
        dz_t = dz_ref[...]
        dcv = dz_t * gb
        dcv_next = dznext_ref[...] * next_ref[:, 0:d].astype(F32)
        dbuf[0:tm, :] = dcv
        dbuf[tm:tm + HALO, :] = jnp.where(i < nt - 1, dcv_next, 0.0)
        dp = w_ref[2:3, :] * dcv + w_ref[1:2, :] * dbuf[1:1 + tm, :] + w_ref[0:1, :] * dbuf[2:2 + tm, :]
        dw_ref[2:3, :] += jnp.sum(dcv * p, axis=0, keepdims=True)
        dw_ref[1:2, :] += jnp.sum(dcv * p1, axis=0, keepdims=True)
        dw_ref[0:1, :] += jnp.sum(dcv * p2, axis=0, keepdims=True)
        dbcu_ref[:, 0:d] = (dz_t * cv).astype(BF16)
        dbcu_ref[:, d:2 * d] = (dp * uu).astype(BF16)
        dbcu_ref[:, 2 * d:3 * d] = (dp * gc).astype(BF16)

    nxt = lambda i: (jnp.minimum((i + 1) * hb, s // HALO - 1), 0)
    return pl.pallas_call(
        body, name=name, grid=(nt,),
        in_specs=[pl.BlockSpec((tm, 3 * d), lambda i: (i, 0)),
                  pl.BlockSpec((HALO, 3 * d), lambda i: (jnp.maximum(i * hb - 1, 0), 0)),
                  pl.BlockSpec((HALO, 3 * d), nxt),
                  pl.BlockSpec((tm, d), lambda i: (i, 0)),
                  pl.BlockSpec((HALO, d), nxt),
                  pl.BlockSpec((3, d), lambda i: (0, 0))],
        out_specs=[pl.BlockSpec((tm, 3 * d), lambda i: (i, 0)), pl.BlockSpec((3, d), lambda i: (0, 0))],
        out_shape=[jax.ShapeDtypeStruct((s, 3 * d), BF16), jax.ShapeDtypeStruct((3, d), F32)],
        scratch_shapes=[pltpu.VMEM((tm + HALO, d), F32), pltpu.VMEM((tm + HALO, d), F32)],
        compiler_params=_cparams("arbitrary"),
    )(bcu, bcu, bcu, dz, dz, conv_w)


def _mla_prep(a, g_qa, g_kva, w_uq, w_ukv, g_q, g_k, cc, sa, sb, *, name):
    s = a.shape[0]
    ts = _tile(s, T_PREP)

    def body(a_ref, gqa_ref, gkva_ref, wuq_ref, wukv_ref, gq_ref, gk_ref, cc_ref, sa_ref, sb_ref,
             cq_ref, ckv_ref, q_ref, k_ref, v_ref):
        xq, _ = _rms(a_ref[:, 0:Q_RANK], Q_RANK)
        cq = (xq * gqa_ref[...]).astype(BF16)
        cq_ref[...] = cq
        xkv, _ = _rms(a_ref[:, Q_RANK:Q_RANK + KV_RANK], KV_RANK)
        ckv = (xkv * gkva_ref[...]).astype(BF16)
        ckv_ref[...] = ckv
        kpe = a_ref[:, Q_RANK + KV_RANK:DOWN_PAD]
        kpe_ss = jnp.sum(kpe * kpe, axis=-1, keepdims=True)
        cc_t, sa_t, sb_t = cc_ref[...], sa_ref[...], sb_ref[...]
        gq = gq_ref[...]
        gk = gk_ref[...]
        for h in range(N_HEADS):
            cols = slice(h * QK_PAD, (h + 1) * QK_PAD)
            qhat, _ = _rms(_dot(cq, wuq_ref[:, cols]), QK_DIM)
            qn = qhat * (gq * SCALE_LOG2E)
            q_ref[h, :, 0:QK_NOPE] = qn[:, 0:QK_NOPE].astype(BF16)
            q_ref[h, :, QK_NOPE:QK_PAD] = _rope(qn[:, QK_NOPE:QK_PAD], cc_t, sa_t, sb_t).astype(BF16)
            kvr = _dot(ckv, wukv_ref[:, cols])
            kn = kvr[:, 0:QK_NOPE]
            rk = lax.rsqrt((jnp.sum(kn * kn, axis=-1, keepdims=True) + kpe_ss) * (1.0 / QK_DIM) + EPS)
            k_ref[h, :, 0:QK_NOPE] = (kn * rk * gk[:, 0:QK_NOPE]).astype(BF16)
            k_ref[h, :, QK_NOPE:QK_PAD] = _rope(kpe * rk * gk[:, QK_NOPE:QK_PAD], cc_t, sa_t, sb_t).astype(BF16)
            v_ref[h, :, 0:V_DIM] = kvr[:, QK_NOPE:QK_PAD].astype(BF16)
            v_ref[h, :, V_DIM:2 * V_DIM] = jnp.ones((ts, V_DIM), BF16)

    row = lambda i: (i, 0)
    fixed = lambda i: (0, 0)
    head = lambda i: (0, i, 0)
    return pl.pallas_call(
        body, name=name, grid=(s // ts,),
        in_specs=[pl.BlockSpec((ts, DOWN_PAD), row), pl.BlockSpec((1, Q_RANK), fixed), pl.BlockSpec((1, KV_RANK), fixed),
                  pl.BlockSpec((Q_RANK, N_HEADS * QK_PAD), fixed), pl.BlockSpec((KV_RANK, N_HEADS * QK_PAD), fixed),
                  pl.BlockSpec((1, QK_PAD), fixed), pl.BlockSpec((1, QK_PAD), fixed),
                  pl.BlockSpec((ts, 128), row), pl.BlockSpec((ts, 128), row), pl.BlockSpec((ts, 128), row)],
        out_specs=[pl.BlockSpec((ts, Q_RANK), row), pl.BlockSpec((ts, KV_RANK), row),
                   pl.BlockSpec((N_HEADS, ts, QK_PAD), head), pl.BlockSpec((N_HEADS, ts, QK_PAD), head),
                   pl.BlockSpec((N_HEADS, ts, 2 * V_DIM), head)],
        out_shape=[jax.ShapeDtypeStruct((s, Q_RANK), BF16), jax.ShapeDtypeStruct((s, KV_RANK), BF16),
                   jax.ShapeDtypeStruct((N_HEADS, s, QK_PAD), BF16), jax.ShapeDtypeStruct((N_HEADS, s, QK_PAD), BF16),
                   jax.ShapeDtypeStruct((N_HEADS, s, 2 * V_DIM), BF16)],
        compiler_params=_cparams("parallel"),
    )(a, g_qa, g_kva, w_uq, w_ukv, g_q, g_k, cc, sa, sb)


def _mla_prep_bwd(a, g_qa, g_kva, w_uq, w_ukv, g_q, g_k, cc, sa, sb, dq, dk, dv, *, name):
    s = a.shape[0]
    ts = _tile(s, T_PREP_BWD)

    def body(a_ref, gqa_ref, gkva_ref, wuq_ref, wukv_ref, gq_ref, gk_ref, cc_ref, sa_ref, sb_ref,
             dq_ref, dk_ref, dv_ref, dqr_ref, dkvr_ref, da_ref, dgq_ref, dgk_ref, dgqa_ref, dgkva_ref):
        @pl.when(pl.program_id(0) == 0)
        def _():
            dgq_ref[...] = jnp.zeros_like(dgq_ref)
            dgk_ref[...] = jnp.zeros_like(dgk_ref)
            dgqa_ref[...] = jnp.zeros_like(dgqa_ref)
            dgkva_ref[...] = jnp.zeros_like(dgkva_ref)

        xq, r_q = _rms(a_ref[:, 0:Q_RANK], Q_RANK)
        cq = (xq * gqa_ref[...]).astype(BF16)
        xkv, r_kv = _rms(a_ref[:, Q_RANK:Q_RANK + KV_RANK], KV_RANK)
        ckv = (xkv * gkva_ref[...]).astype(BF16)
        kpe = a_ref[:, Q_RANK + KV_RANK:DOWN_PAD]
        kpe_ss = jnp.sum(kpe * kpe, axis=-1, keepdims=True)
        cc_t, sa_t, sb_t = cc_ref[...], sa_ref[...], sb_ref[...]
        gq = gq_ref[...]
        gk = gk_ref[...]
        dcq = jnp.zeros((ts, Q_RANK), F32)
        dckv = jnp.zeros((ts, KV_RANK), F32)
        dkpe = jnp.zeros((ts, 128), F32)
        dgq = jnp.zeros((1, QK_PAD), F32)
        dgk_n = jnp.zeros((1, QK_NOPE), F32)
        dgk_p = jnp.zeros((1, 128), F32)
        for h in range(N_HEADS):
            cols = slice(h * QK_PAD, (h + 1) * QK_PAD)
            qhat, rq = _rms(_dot(cq, wuq_ref[:, cols]), QK_DIM)
            dqn = jnp.concatenate(
                [dq_ref[h, :, 0:QK_NOPE], _rope_t(dq_ref[h, :, QK_NOPE:QK_PAD], cc_t, sa_t, sb_t)], axis=1)
            dgq = dgq + jnp.sum(dqn * qhat, axis=0, keepdims=True)
            dqr = _rms_bwd(qhat, rq, dqn * gq, QK_DIM).astype(BF16)
            dqr_ref[:, cols] = dqr
            dcq = dcq + _dot_nt(dqr, wuq_ref[:, cols])
            kn = _dot(ckv, wukv_ref[:, h * QK_PAD:h * QK_PAD + QK_NOPE])
            rk = lax.rsqrt((jnp.sum(kn * kn, axis=-1, keepdims=True) + kpe_ss) * (1.0 / QK_DIM) + EPS)
            khat_n = kn * rk
            khat_p = kpe * rk
            dkn = dk_ref[h, :, 0:QK_NOPE]
            dkp = _rope_t(dk_ref[h, :, QK_NOPE:QK_PAD], cc_t, sa_t, sb_t)
            dgk_n = dgk_n + jnp.sum(dkn * khat_n, axis=0, keepdims=True)
            dgk_p = dgk_p + jnp.sum(dkp * khat_p, axis=0, keepdims=True)
            dxn = dkn * gk[:, 0:QK_NOPE]
            dxp = dkp * gk[:, QK_NOPE:QK_PAD]
            mean = (jnp.sum(dxn * khat_n, axis=-1, keepdims=True)
                    + jnp.sum(dxp * khat_p, axis=-1, keepdims=True)) * (1.0 / QK_DIM)
            dkpe = dkpe + rk * (dxp - khat_p * mean)
            dkvr = jnp.concatenate([rk * (dxn - khat_n * mean), dv_ref[h, :, :]], axis=1).astype(BF16)
            dkvr_ref[:, cols] = dkvr
            dckv = dckv + _dot_nt(dkvr, wukv_ref[:, cols])
        dgq_ref[...] += dgq
        dgk_ref[:, 0:QK_NOPE] += dgk_n
        dgk_ref[:, QK_NOPE:QK_PAD] += dgk_p
        dgqa_ref[...] += jnp.sum(dcq * xq, axis=0, keepdims=True)
        dgkva_ref[...] += jnp.sum(dckv * xkv, axis=0, keepdims=True)
        da_ref[:, 0:Q_RANK] = _rms_bwd(xq, r_q, dcq * gqa_ref[...], Q_RANK).astype(BF16)
        da_ref[:, Q_RANK:Q_RANK + KV_RANK] = _rms_bwd(xkv, r_kv, dckv * gkva_ref[...], KV_RANK).astype(BF16)
        da_ref[:, Q_RANK + KV_RANK:DOWN_PAD] = dkpe.astype(BF16)

    row = lambda i: (i, 0)
    fixed = lambda i: (0, 0)
    head = lambda i: (0, i, 0)
    wide = N_HEADS * QK_PAD
    return pl.pallas_call(
        body, name=name, grid=(s // ts,),
        in_specs=[pl.BlockSpec((ts, DOWN_PAD), row), pl.BlockSpec((1, Q_RANK), fixed), pl.BlockSpec((1, KV_RANK), fixed),
                  pl.BlockSpec((Q_RANK, wide), fixed), pl.BlockSpec((KV_RANK, wide), fixed),
                  pl.BlockSpec((1, QK_PAD), fixed), pl.BlockSpec((1, QK_PAD), fixed),
                  pl.BlockSpec((ts, 128), row), pl.BlockSpec((ts, 128), row), pl.BlockSpec((ts, 128), row),
                  pl.BlockSpec((N_HEADS, ts, QK_PAD), head), pl.BlockSpec((N_HEADS, ts, QK_PAD), head),
                  pl.BlockSpec((N_HEADS, ts, V_DIM), head)],
        out_specs=[pl.BlockSpec((ts, wide), row), pl.BlockSpec((ts, wide), row), pl.BlockSpec((ts, DOWN_PAD), row),
                   pl.BlockSpec((1, QK_PAD), fixed), pl.BlockSpec((1, QK_PAD), fixed),
                   pl.BlockSpec((1, Q_RANK), fixed), pl.BlockSpec((1, KV_RANK), fixed)],
        out_shape=[jax.ShapeDtypeStruct((s, wide), BF16), jax.ShapeDtypeStruct((s, wide), BF16),
                   jax.ShapeDtypeStruct((s, DOWN_PAD), BF16),
                   jax.ShapeDtypeStruct((1, QK_PAD), F32), jax.ShapeDtypeStruct((1, QK_PAD), F32),
                   jax.ShapeDtypeStruct((1, Q_RANK), F32), jax.ShapeDtypeStruct((1, KV_RANK), F32)],
        compiler_params=_cparams("arbitrary"),
    )(a, g_qa, g_kva, w_uq, w_ukv, g_q, g_k, cc, sa, sb, dq, dk, dv)


def _flash_fwd(q, k, v, pos_col, pos_row, *, name, gather=None):
    nh, s, _ = q.shape
    tq = _tile(s, FWD_TQ)
    tk = _tile(s, FWD_TK)
    sq = tq // ATTN_CHAINS
    nq = s // tq
    names = list(gather or {})
    ng = len(names)

    def body(q_ref, k_ref, v_ref, pq_ref, pk_ref, *rest):
        o_ref, lse_ref = rest[ng:ng + 2]
        m_sc, acc_sc = rest[2 * ng + 2:2 * ng + 4]
        qb = pl.program_id(1)
        if ng:
            sends, recvs = _gather_ici_copies(rest[ng + 2:2 * ng + 2], names, *rest[2 * ng + 4:], base=0, stride=3)

            @pl.when((pl.program_id(0) == 0) & (qb == 0))
            def _():
                for cp in sends:
                    cp.start()

        m_sc[...] = jnp.full_like(m_sc, NEG)
        acc_sc[...] = jnp.zeros_like(acc_sc)

        def step(kb, masked):
            trim = masked and tq == tk
            chains = DIAG_CHAINS if trim else ATTN_CHAINS
            sq = tq // chains
            start = pl.multiple_of(kb * tk, tk)
            widths = [(u + 1) * sq if trim else tk for u in range(chains)]
            scores = [_dot_nt(q_ref[0, u * sq:(u + 1) * sq, :], k_ref[0, pl.ds(start, widths[u]), :])
                      for u in range(chains)]
            for u in range(chains):
                rows = slice(u * sq, (u + 1) * sq)
                keys = pl.ds(start, widths[u])
                sc = scores[u]
                if masked:
                    sc = jnp.where(pq_ref[rows, :] >= pk_ref[:, keys], sc, NEG)
                m_prev = m_sc[rows, :]
                m_new = jnp.maximum(m_prev, jnp.max(sc, axis=-1, keepdims=True))
                alpha = jnp.exp2(m_prev - m_new)
                p = jnp.exp2(sc - jnp.tile(m_new, (1, widths[u] // 128)))
                acc_sc[rows, :] = (jnp.tile(alpha, (1, 2)) * acc_sc[rows, :]
                                   + _dot(p.astype(BF16), v_ref[0, keys, :]))
                m_sc[rows, :] = m_new

        n_before = (qb * tq) // tk
        n_seen = (qb * tq + tq - 1) // tk + 1
        lax.fori_loop(0, n_before, lambda kb, c: (step(kb, False), c)[1], 0)
        lax.fori_loop(n_before, n_seen, lambda kb, c: (step(kb, True), c)[1], 0)
        l = acc_sc[:, V_DIM:2 * V_DIM]
        o_ref[...] = (acc_sc[:, 0:V_DIM] / l).astype(BF16)
        lse = m_sc[...] * (1.0 / LOG2E) + jnp.log(l)
        lse_ref[0] = lse.T[0:1, :]

        if ng:
            @pl.when((pl.program_id(0) == nh - 1) & (qb == nq - 1))
            def _():
                for cp in recvs:
                    cp.wait_recv()
                for cp in sends:
                    cp.wait_send()

    arrays = [gather[nm] for nm in names]
    out = pl.pallas_call(
        body, name=name, grid=(nh, nq),
        in_specs=[pl.BlockSpec((1, tq, QK_PAD), lambda h, qb: (h, qb, 0)),
                  pl.BlockSpec((1, s, QK_PAD), lambda h, qb: (h, 0, 0)),
                  pl.BlockSpec((1, s, 2 * V_DIM), lambda h, qb: (h, 0, 0)),
                  pl.BlockSpec((tq, 1), lambda h, qb: (qb, 0)),
                  pl.BlockSpec((1, s), lambda h, qb: (0, 0))] + [ANY] * ng,
        out_specs=[pl.BlockSpec((tq, V_DIM), lambda h, qb: (qb, h)),
                   pl.BlockSpec((1, 1, tq), lambda h, qb: (h, 0, qb))] + [ANY] * ng,
        scratch_shapes=[pltpu.VMEM((tq, 128), F32), pltpu.VMEM((tq, 2 * V_DIM), F32)]
        + ([pltpu.SemaphoreType.DMA((3 * ng,)), pltpu.SemaphoreType.DMA((3 * ng,))] if ng else []),
        out_shape=[jax.ShapeDtypeStruct((s, nh * V_DIM), BF16), jax.ShapeDtypeStruct((nh, 1, s), F32)]
        + [jax.ShapeDtypeStruct(a.shape, a.dtype) for a in arrays],
        input_output_aliases={5 + i: 2 + i for i in range(ng)},
        compiler_params=_cparams("arbitrary", "arbitrary") if ng else _cparams("parallel", "parallel"),
    )(q, k, v, pos_col, pos_row, *arrays)
    return out[0], out[1], dict(zip(names, out[2:]))


def _attn_out_bwd(dy, w_o, o, *, name, layer=0):
    s, d = dy.shape
    n = w_o.shape[-2]
    tm = _tile(s, TM)

    def body(dy_ref, w_ref, o_ref, do_ref, d_ref):
        do = _dot_nt(dy_ref[...].astype(BF16), w_ref[...]).astype(BF16)
        do_ref[...] = do
        for h in range(N_HEADS):
            cols = slice(h * V_DIM, (h + 1) * V_DIM)
            prod = do[:, cols].astype(F32) * o_ref[:, cols].astype(F32)
            d_ref[h] = jnp.sum(prod.T, axis=0, keepdims=True)

    return pl.pallas_call(
        body, name=name, grid=(s // tm,),
        in_specs=[pl.BlockSpec((tm, d), lambda i: (i, 0)), _wspec(w_o, layer), pl.BlockSpec((tm, n), lambda i: (i, 0))],
        out_specs=[pl.BlockSpec((tm, n), lambda i: (i, 0)), pl.BlockSpec((N_HEADS, 1, tm), lambda i: (0, 0, i))],
        out_shape=[jax.ShapeDtypeStruct((s, n), BF16), jax.ShapeDtypeStruct((N_HEADS, 1, s), F32)],
        compiler_params=_cparams("parallel"),
    )(dy, w_o, o)


def _flash_bwd(q, k, v, do, lse_row, delta_row, pos_col, pos_row, *, name, scatter=None):
    nh, s, _ = q.shape
    tq = _tile(s, BWD_TQ)
    tk = _tile(s, BWD_TK)
    nq, nk = s // tq, s // tk
    sk = tk // ATTN_CHAINS
    names = list(scatter or {})
    ng = len(names)

    def body(q_ref, k_ref, v_ref, do_ref, lse_ref, delta_ref, pq_ref, pk_ref, *rest):
        dq_ref, dk_ref, dv_ref = rest[ng:ng + 3]
        dk_sc, dv_sc = rest[2 * ng + 3:2 * ng + 5]
        kb = pl.program_id(1)
        if ng:
            copies = _scatter_copies(rest[:ng], rest[ng + 3:2 * ng + 3], names, *rest[2 * ng + 5:])

            @pl.when((pl.program_id(0) == 0) & (kb == 0))
            def _():
                for cp in copies:
                    cp.start()

        @pl.when(kb == 0)
        def _():
            dq_ref[...] = jnp.zeros_like(dq_ref)

        dk_sc[...] = jnp.zeros_like(dk_sc)
        dv_sc[...] = jnp.zeros_like(dv_sc)

        def step(qb, masked):
            trim = masked and tq == tk
            chains = DIAG_CHAINS if trim else ATTN_CHAINS
            sk = tk // chains
            start = pl.multiple_of(qb * tq, tq)
            offs = [u * sk if trim else 0 for u in range(chains)]
            qss = [pl.ds(start + offs[u], tq - offs[u]) for u in range(chains)]
            qts = [q_ref[0, qss[u], :] for u in range(chains)]
            dots = [do_ref[qss[u], :] for u in range(chains)]
            sts = [_dot_nt(k_ref[0, u * sk:(u + 1) * sk, :], qts[u]) for u in range(chains)]
            dpts = [_dot_nt(v_ref[0, u * sk:(u + 1) * sk, :], dots[u]) for u in range(chains)]
            parts = []
            for u in range(chains):
                rows = slice(u * sk, (u + 1) * sk)
                pt = jnp.exp2(sts[u] - lse_ref[0, :, qss[u]] * LOG2E)
                if masked:
                    pt = jnp.where(pq_ref[:, qss[u]] >= pk_ref[rows, :], pt, 0.0)
                dv_sc[rows, :] += _dot(pt.astype(BF16), dots[u])
                dst = (pt * (dpts[u] - delta_ref[0, :, qss[u]])).astype(BF16)
                dk_sc[rows, :] += _dot(dst, qts[u])
                parts.append(_dot_tn(dst, k_ref[0, rows, :]))
            if trim:
                for u in range(chains):
                    dq_ref[0, qss[u], :] += parts[u]
            else:
                dq_ref[0, qss[0], :] += functools.reduce(lambda a, b: a + b, parts)

        q_first = (kb * tk) // tq
        q_clear = (kb * tk + tk - 1) // tq + 1
        lax.fori_loop(q_first, q_clear, lambda qb, c: (step(qb, True), c)[1], 0)
        lax.fori_loop(q_clear, nq, lambda qb, c: (step(qb, False), c)[1], 0)
        dk_ref[0] = dk_sc[...] * (1.0 / LOG2E)
        dv_ref[0] = dv_sc[...]

        @pl.when(kb == nk - 1)
        def _():
            dq_ref[...] = dq_ref[...] * SCALE

        if ng:
            @pl.when((pl.program_id(0) == nh - 1) & (kb == nk - 1))
            def _():
                for cp in copies:
                    cp.wait()

    arrays = [scatter[nm] for nm in names]
    out = pl.pallas_call(
        body, name=name, grid=(nh, nk),
        in_specs=[pl.BlockSpec((1, s, QK_PAD), lambda h, kb: (h, 0, 0)),
                  pl.BlockSpec((1, tk, QK_PAD), lambda h, kb: (h, kb, 0)),
                  pl.BlockSpec((1, tk, V_DIM), lambda h, kb: (h, kb, 0)),
                  pl.BlockSpec((s, V_DIM), lambda h, kb: (0, h)),
                  pl.BlockSpec((1, 1, s), lambda h, kb: (h, 0, 0)),
                  pl.BlockSpec((1, 1, s), lambda h, kb: (h, 0, 0)),
                  pl.BlockSpec((1, s), lambda h, kb: (0, 0)),
                  pl.BlockSpec((tk, 1), lambda h, kb: (kb, 0))] + [ANY] * ng,
        out_specs=[pl.BlockSpec((1, s, QK_PAD), lambda h, kb: (h, 0, 0)),
                   pl.BlockSpec((1, tk, QK_PAD), lambda h, kb: (h, kb, 0)),
                   pl.BlockSpec((1, tk, V_DIM), lambda h, kb: (h, kb, 0))] + [ANY] * ng,
        scratch_shapes=[pltpu.VMEM((tk, QK_PAD), F32), pltpu.VMEM((tk, V_DIM), F32)]
        + ([pltpu.SemaphoreType.DMA((3 * ng,)), pltpu.SemaphoreType.DMA((3 * ng,))] if ng else []),
        out_shape=[jax.ShapeDtypeStruct((nh, s, QK_PAD), F32), jax.ShapeDtypeStruct((nh, s, QK_PAD), F32),
                   jax.ShapeDtypeStruct((nh, s, V_DIM), F32)] + _scatter_out_shapes(names, arrays),
        compiler_params=_cparams("arbitrary", "arbitrary"),
    )(q, k, v, do, lse_row, delta_row, pos_row, pos_col, *arrays)
    return out[0], out[1], out[2], dict(zip(names, out[3:]))


def _loss_head(y, target, *, name):
    s, d = y.shape
    tm = _tile(s, TM_NARROW)
    nt = s // tm

    def body(y_ref, t_ref, dy_ref, loss_ref, acc):
        i = pl.program_id(0)

        @pl.when(i == 0)
        def _():
            acc[...] = jnp.zeros_like(acc)

        e = y_ref[...] - t_ref[...]
        dy_ref[...] = e * (1.0 / d)
        acc[...] += jnp.sum((e * e).reshape(tm // 8, 8, d), axis=0)

        @pl.when(i == nt - 1)
        def _():
            loss_ref[...] = jnp.full((1, 128), 0.5 / d, F32) * jnp.sum(acc[...])

    return pl.pallas_call(
        body, name=name, grid=(nt,),
        in_specs=[pl.BlockSpec((tm, d), lambda i: (i, 0))] * 2,
        out_specs=[pl.BlockSpec((tm, d), lambda i: (i, 0)), pl.BlockSpec((1, 128), lambda i: (0, 0))],
        out_shape=[jax.ShapeDtypeStruct((s, d), F32), jax.ShapeDtypeStruct((1, 128), F32)],
        scratch_shapes=[pltpu.VMEM((8, d), F32)],
        compiler_params=_cparams("arbitrary"),
    )(y, target)


def _adamw(w, g, m, v, *, name):
    r, c = w.shape
    tr = _tile(r, 512) if r % 8 == 0 else r

    def body(w_ref, g_ref, m_ref, v_ref, d_ref, nm_ref, nv_ref, go_ref):
        g_t = g_ref[...]
        go_ref[...] = g_t
        nm = ADAM_B1 * m_ref[...] + (1.0 - ADAM_B1) * g_t
        nv = ADAM_B2 * v_ref[...] + (1.0 - ADAM_B2) * (g_t * g_t)
        m_hat = nm / (1.0 - ADAM_B1 ** ADAM_STEP)
        v_hat = nv / (1.0 - ADAM_B2 ** ADAM_STEP)
        d_ref[...] = -ADAM_LR * (m_hat / (jnp.sqrt(v_hat) + ADAM_EPS) + ADAM_WD * w_ref[...])
        nm_ref[...] = nm
        nv_ref[...] = nv

    spec = pl.BlockSpec((tr, c), lambda i: (i, 0))
    return pl.pallas_call(
        body, name=name, grid=(r // tr,), in_specs=[spec] * 4, out_specs=[spec] * 4,
        out_shape=[jax.ShapeDtypeStruct((r, c), F32)] * 4,
        compiler_params=_cparams("parallel"),
    )(w, g, m, v)


def _place():
    return lax.axis_index("x"), lax.axis_index("y"), lax.axis_index("c")


def _other_chips(x, y):
    return [(1 - x, y), (x, 1 - y), (1 - x, 1 - y)]


BIG = {
    "attn_w_down": ((2, 1024, 448), 1), "attn_w_uq": ((2, 256, 1536), 2), "attn_w_ukv": ((2, 128, 2048), 2),
    "attn_w_o": ((2, 1024, 1024), 1), "conv_w_in": ((2, 1024, 3072), 2),
    "conv_w_out": ((2, 1024, 1024), 1), "mlp_w1": ((4, 1024, 4096), 2), "mlp_w2": ((4, 4096, 1024), 1),
}
CONV_W = (2, 3, 1024)


def _shard_shape(name):
    shape, axis = BIG[name]
    return tuple(n // N_CHIPS if i == axis else n for i, n in enumerate(shape))


def _band(ref, name, layers, chip):
    shape, axis = BIG[name]
    width = shape[axis] // N_CHIPS
    if axis == 1:
        return ref.at[layers, pl.ds(chip * width, width), :]
    return ref.at[layers, :, pl.ds(chip * width, width)]


def _half(name, c):
    hl = BIG[name][0][0] // 2
    return pl.ds(c * hl, hl)


def _place_own(w, nm, chip, *, name):
    shape, axis = BIG[nm]
    layers, rows, cols = w.shape
    tr = _sum_rows(rows, cols)
    nrb = rows // tr
    if axis == 1:
        band = lambda l, i, ch: (l, ch[0] * nrb + i, 0)
    else:
        band = lambda l, i, ch: (l, i, ch[0])

    def body(chip_ref, w_ref, o_ref):
        o_ref[...] = w_ref[...].astype(BF16)

    return pl.pallas_call(
        body, name=name,
        grid_spec=pltpu.PrefetchScalarGridSpec(
            num_scalar_prefetch=1, grid=(layers, nrb),
            in_specs=[pl.BlockSpec((1, tr, cols), lambda l, i, ch: (l, i, 0))],
            out_specs=pl.BlockSpec((1, tr, cols), band)),
        out_shape=jax.ShapeDtypeStruct(shape, BF16),
        compiler_params=_cparams("parallel", "parallel"),
    )(chip, w)


def _gather_copies(outs, names, send_sems, recv_sems, *, base, stride, to_sibling):
    x, y, c = _place()
    me = 2 * x + y

    def copy(k, ref, nm, layers, chip, to):
        band = _band(ref, nm, layers, chip)
        return pltpu.make_async_remote_copy(
            src_ref=band, dst_ref=band, send_sem=send_sems.at[k], recv_sem=recv_sems.at[k],
            device_id=to, device_id_type=MESH)

    sends, recvs = [], []
    for i, nm in enumerate(names):
        for j, (cx, cy) in enumerate(_other_chips(x, y)):
            k = base + stride * i + j
            if to_sibling:
                sends.append(copy(k, outs[i], nm, _half(nm, c), 2 * cx + cy, (x, y, 1 - c)))
                recvs.append(copy(k, outs[i], nm, _half(nm, 1 - c), 2 * cx + cy, (x, y, c)))
            else:
                sends.append(copy(k, outs[i], nm, _half(nm, c), me, (cx, cy, c)))
                recvs.append(copy(k, outs[i], nm, _half(nm, c), 2 * cx + cy, (x, y, c)))
    return sends, recvs


def _gather_ici_copies(outs, names, send_sems, recv_sems, *, base, stride):
    return _gather_copies(outs, names, send_sems, recv_sems, base=base, stride=stride, to_sibling=False)


def _gather_weights(fulls, *, name):
    names = list(fulls)
    n = len(names)

    def body(*refs):
        outs = refs[n:2 * n]
        sems = refs[2 * n:]
        sent = []
        for to_sibling in (False, True):
            sends, recvs = _gather_copies(outs, names, *sems, base=3 * to_sibling, stride=6, to_sibling=to_sibling)
            for cp in sends:
                cp.start()
            for cp in recvs:
                cp.wait_recv()
            sent += sends
        for cp in sent:
            cp.wait_send()

    arrays = [fulls[nm] for nm in names]
    out = pl.pallas_call(
        body, name=name, in_specs=[ANY] * n, out_specs=[ANY] * n,
        out_shape=[jax.ShapeDtypeStruct(a.shape, a.dtype) for a in arrays],
        input_output_aliases={i: i for i in range(n)},
        scratch_shapes=[pltpu.SemaphoreType.DMA((6 * n,)), pltpu.SemaphoreType.DMA((6 * n,))],
    )(*arrays)
    return dict(zip(names, out))


def _swap_halves(grads, *, name):
    names = list(grads)
    n = len(names)

    def body(*refs):
        copies = _swap_copies(refs[:n], refs[n:2 * n], names, *refs[2 * n:])
        for cp in copies:
            cp.start()
        for cp in copies:
            cp.wait()

    arrays = [grads[nm] for nm in names]
    out = pl.pallas_call(
        body, name=name, in_specs=[ANY] * n, out_specs=[ANY] * n,
        out_shape=_swap_out_shapes(arrays), scratch_shapes=_swap_sems(n),
    )(*arrays)
    return dict(zip(names, out))


def _swap_copies(ins, outs, names, send_sems, recv_sems):
    x, y, c = _place()
    return [pltpu.make_async_remote_copy(
        src_ref=ins[i].at[_half(nm, 1 - c)], dst_ref=outs[i], send_sem=send_sems.at[i], recv_sem=recv_sems.at[i],
        device_id=(x, y, 1 - c), device_id_type=MESH) for i, nm in enumerate(names)]


def _swap_out_shapes(arrays):
    return [jax.ShapeDtypeStruct((a.shape[0] // 2,) + a.shape[1:], a.dtype) for a in arrays]


def _swap_sems(n):
    return [pltpu.SemaphoreType.DMA((n,)), pltpu.SemaphoreType.DMA((n,))] if n else []


def _all_steps(grid, at):
    cond = None
    for axis, size in enumerate(grid):
        this = pl.program_id(axis) == (0 if at == "first" else size - 1)
        cond = this if cond is None else cond & this
    return cond


def _start_at_first_step(copies, grid):
    @pl.when(_all_steps(grid, "first"))
    def _():
        for cp in copies:
            cp.start()


def _wait_at_last_step(copies, grid):
    @pl.when(_all_steps(grid, "last"))
    def _():
        for cp in copies:
            cp.wait()


def _sum_rows(rows, cols):
    t = rows
    while t * cols * 4 > SUM_BLOCK_BYTES and t % 16 == 0:
        t //= 2
    return t


def _chip_sum(g, r1, core, *, name):
    layers, rows, cols = g.shape
    hl = layers // 2
    tr = _sum_rows(rows, cols)

    def body(core_ref, g_ref, r_ref, o_ref):
        o_ref[...] = (g_ref[...] + r_ref[...]).astype(BF16)

    return pl.pallas_call(
        body, name=name,
        grid_spec=pltpu.PrefetchScalarGridSpec(
            num_scalar_prefetch=1, grid=(hl, rows // tr),
            in_specs=[pl.BlockSpec((1, tr, cols), lambda l, i, cr: (cr[0] * hl + l, i, 0)),
                      pl.BlockSpec((1, tr, cols), lambda l, i, cr: (l, i, 0))],
            out_specs=pl.BlockSpec((1, tr, cols), lambda l, i, cr: (l, i, 0))),
        out_shape=jax.ShapeDtypeStruct((hl, rows, cols), BF16),
        compiler_params=_cparams("parallel", "parallel"),
    )(core, g, r1)


def _chip_partials(grads, names, *, tag):
    core = lax.axis_index("c").astype(jnp.int32).reshape(1)
    r1 = _swap_halves({n: grads[n] for n in names}, name=f"grad_swap_halves_{tag}")
    return r1, {n: _chip_sum(grads[n], r1[n], core, name=f"grad_chip_sum_{n}") for n in names}


def _scatter_partials(partials):
    names = list(partials)
    n = len(names)

    def body(*refs):
        copies = _scatter_copies(refs[:n], refs[n:2 * n], names, *refs[2 * n:])
        for cp in copies:
            cp.start()
        for cp in copies:
            cp.wait()

    arrays = [partials[nm] for nm in names]
    out = pl.pallas_call(
        body, name="grad_scatter_partials", in_specs=[ANY] * n, out_specs=[ANY] * n,
        out_shape=_scatter_out_shapes(names, arrays),
        scratch_shapes=[pltpu.SemaphoreType.DMA((3 * n,)), pltpu.SemaphoreType.DMA((3 * n,))],
    )(*arrays)
    return dict(zip(names, out))


def _scatter_copies(ins, outs, names, send_sems, recv_sems):
    x, y, c = _place()
    copies = []
    for i, nm in enumerate(names):
        for j, (cx, cy) in enumerate(_other_chips(x, y)):
            copies.append(pltpu.make_async_remote_copy(
                src_ref=_band(ins[i], nm, slice(None), 2 * cx + cy), dst_ref=outs[i].at[j],
                send_sem=send_sems.at[3 * i + j], recv_sem=recv_sems.at[3 * i + j],
                device_id=(cx, cy, c), device_id_type=MESH))
    return copies


def _scatter_out_shapes(names, arrays):
    return [jax.ShapeDtypeStruct((3, a.shape[0]) + _shard_shape(nm)[1:], a.dtype) for nm, a in zip(names, arrays)]


def _final_sum(g, r1, r2, place, nm, *, name):
    (layers, _, _), axis = BIG[nm]
    hl = layers // 2
    _, rows, cols = _shard_shape(nm)
    tr = _sum_rows(rows, cols)
    nrb = rows // tr
    if axis == 1:
        blk = lambda l, i, pc: (l, pc[1] * nrb + i, 0)
    else:
        blk = lambda l, i, pc: (l, i, pc[1])

    def body(place_ref, g_ref, r1_ref, r2_ref, o_ref):
        acc = g_ref[...] + r1_ref[...]
        for j in range(3):
            acc = acc + r2_ref[j].astype(F32)
        o_ref[...] = acc

    return pl.pallas_call(
        body, name=name,
        grid_spec=pltpu.PrefetchScalarGridSpec(
            num_scalar_prefetch=1, grid=(hl, nrb),
            in_specs=[pl.BlockSpec((1, tr, cols), lambda l, i, pc: blk(pc[0] * hl + l, i, pc)),
                      pl.BlockSpec((1, tr, cols), lambda l, i, pc: blk(l, i, pc)),
                      pl.BlockSpec((3, 1, tr, cols), lambda l, i, pc: (0, l, i, 0))],
            out_specs=pl.BlockSpec((1, tr, cols), lambda l, i, pc: (pc[0] * hl + l, i, 0))),
        out_shape=jax.ShapeDtypeStruct((layers, rows, cols), F32),
        compiler_params=_cparams("parallel", "parallel"),
    )(place, g, r1, r2)


def _join_halves(shards):
    names = list(shards)
    n = len(names)

    def body(*refs):
        outs = refs[n:2 * n]
        send_sems, recv_sems = refs[2 * n:]
        x, y, c = _place()
        copies = []
        for i, nm in enumerate(names):
            mine = outs[i].at[_half(nm, c)]
            cp = pltpu.make_async_remote_copy(
                src_ref=mine, dst_ref=mine, send_sem=send_sems.at[i], recv_sem=recv_sems.at[i],
                device_id=(x, y, 1 - c), device_id_type=MESH)
            cp.start()
            copies.append(cp)
        for i, nm in enumerate(names):
            theirs = outs[i].at[_half(nm, 1 - c)]
            pltpu.make_async_remote_copy(
                src_ref=theirs, dst_ref=theirs, send_sem=send_sems.at[i], recv_sem=recv_sems.at[i],
                device_id=(x, y, 1 - c), device_id_type=MESH).wait_recv()
        for cp in copies:
            cp.wait_send()

    arrays = [shards[nm] for nm in names]
    out = pl.pallas_call(
        body, name="grad_join_halves", in_specs=[ANY] * n, out_specs=[ANY] * n,
        out_shape=[jax.ShapeDtypeStruct(a.shape, a.dtype) for a in arrays],
        input_output_aliases={i: i for i in range(n)},
        scratch_shapes=[pltpu.SemaphoreType.DMA((n,)), pltpu.SemaphoreType.DMA((n,))],
    )(*arrays)
    return dict(zip(names, out))


def _all_reduce_small(part, *, name):
    rows, cols = part.shape
    vm = pl.BlockSpec(memory_space=pltpu.VMEM)

    def body(p_ref, o_ref, land, send_sems, recv_sems):
        x, y, c = _place()
        me = 4 * x + 2 * y + c
        flips = [(dx, dy, dc) for dx in (0, 1) for dy in (0, 1) for dc in (0, 1)][1:]
        copies = []
        for k, (dx, dy, dc) in enumerate(flips):
            cp = pltpu.make_async_remote_copy(
                src_ref=p_ref, dst_ref=land.at[me], send_sem=send_sems.at[k], recv_sem=recv_sems.at[k],
                device_id=(1 - x if dx else x, 1 - y if dy else y, 1 - c if dc else c), device_id_type=MESH)
            cp.start()
            copies.append(cp)
        land[me] = p_ref[...]
        for cp in copies:
            cp.wait()
        acc = land[0]
        for j in range(1, 8):
            acc = acc + land[j]
        o_ref[...] = acc

    return pl.pallas_call(
        body, name=name, in_specs=[vm], out_specs=vm,
        out_shape=jax.ShapeDtypeStruct((rows, cols), F32),
        scratch_shapes=[pltpu.VMEM((8, rows, cols), F32), pltpu.SemaphoreType.DMA((7,)), pltpu.SemaphoreType.DMA((7,))],
    )(part)


SMALL = {"g_mix": (4, 1024), "g_mlp": (4, 1024), "attn_g_q_a": (2, 256), "attn_g_kv_a": (2, 128),
         "attn_g_qnorm": (2, 192), "attn_g_knorm": (2, 192)}
SMALL_GRADS = {**SMALL, "conv_w": CONV_W}
WEIGHT_ORDER = ["g_mix", "g_mlp", "attn_w_down", "attn_g_q_a", "attn_g_kv_a", "attn_w_uq", "attn_w_ukv",
                "attn_g_qnorm", "attn_g_knorm", "attn_w_o", "conv_w_in", "conv_w", "conv_w_out", "mlp_w1", "mlp_w2"]


def _prod(shape):
    n = 1
    for v in shape:
        n *= v
    return n


def _pack_small(parts, table):
    flat = [parts[n].reshape(-1) for n in table]
    size = sum(_prod(s) for s in table.values())
    rows = -(-size // (8 * 128)) * 8
    flat.append(jnp.zeros((rows * 128 - size,), F32))
    return jnp.concatenate(flat).reshape(rows, 128)


def _unpack_small(buf, table):
    flat = buf.reshape(-1)
    out, off = {}, 0
    for n, shp in table.items():
        out[n] = flat[off:off + _prod(shp)].reshape(shp)
        off += _prod(shp)
    return out


def _rope_tables(positions):
    inv_freq = ROPE_THETA ** (-jnp.arange(0, QK_ROPE, 2, dtype=F32) / QK_ROPE)
    ang = positions.astype(F32)[:, None] * inv_freq
    cos, sin = jnp.cos(ang), jnp.sin(ang)
    z32 = jnp.zeros_like(cos)
    z64 = jnp.zeros((positions.shape[0], 64), F32)
    cc = jnp.concatenate([cos, cos, z64], axis=1)
    sa = jnp.concatenate([-sin, z32, z64], axis=1)
    sb = jnp.concatenate([z32, sin, z64], axis=1)
    return cc, sa, sb


def _pad_heads(w, width):
    k = w.shape[0]
    w = w.reshape(k, N_HEADS, width)
    return jnp.pad(w, ((0, 0), (0, 0), (0, QK_PAD - width))).reshape(k, N_HEADS * QK_PAD)


EARLY = ("mlp_w1", "mlp_w2", "conv_w_in", "conv_w_out")
LATE = ("attn_w_down", "attn_w_uq", "attn_w_ukv", "attn_w_o")
GATHER_LATER = EARLY


def _local_step(x, positions, target, wb, gains, later=None):
    s = x.shape[0]
    cc, sa, sb = _rope_tables(positions)
    pos_col = positions.reshape(s, 1)
    pos_row = positions.reshape(1, s)

    saved = []
    for i in range(4):
        j = i // 2
        g_mix = gains["g_mix"][i:i + 1]
        g_mlp = gains["g_mlp"][i:i + 1]
        if i % 2 == 0:
            w_down = jnp.pad(wb["attn_w_down"][j], ((0, 0), (0, DOWN_PAD - DOWN_DIM)))
            w_uq = _pad_heads(wb["attn_w_uq"][j], QK_DIM)
            w_ukv = wb["attn_w_ukv"][j]
            g_qa = gains["attn_g_q_a"][j:j + 1]
            g_kva = gains["attn_g_kv_a"][j:j + 1]
            g_q = jnp.pad(gains["attn_g_qnorm"][j:j + 1], ((0, 0), (0, QK_PAD - QK_DIM)))
            g_k = jnp.pad(gains["attn_g_knorm"][j:j + 1], ((0, 0), (0, QK_PAD - QK_DIM)))
            h, a = _norm_mm(x, g_mix, w_down, out_dtype=F32, name=f"mla_down_{j}")
            cq, ckv, q, k, v = _mla_prep(a, g_qa, g_kva, w_uq, w_ukv, g_q, g_k, cc, sa, sb, name=f"mla_prep_{j}")
            o, lse, got = _flash_fwd(q, k, v, pos_col, pos_row, name=f"flash_fwd_{j}",
                                     gather=later if i == 0 else None)
            x_mid = _mm_nn(o, wb["attn_w_o"], layer=j, out_dtype=F32, residual=x, name=f"mla_out_{j}", forward=got)
            if got:
                x_mid, forwarded = x_mid
                wb = {**wb, **forwarded}
            mix = dict(h=h, a=a, cq=cq, ckv=ckv, q=q, k=k, v=v, o=o, lse=lse, w_down=w_down, w_uq=w_uq, w_ukv=w_ukv,
                       g_qa=g_qa, g_kva=g_kva, g_q=g_q, g_k=g_k)
        else:
            h, bcu = _norm_mm(x, g_mix, wb["conv_w_in"], layer=j, out_dtype=BF16, name=f"conv_in_{j}")
            z = _conv_gate(bcu, gains["conv_w"][j], name=f"conv_gate_{j}")
            x_mid = _mm_nn(z, wb["conv_w_out"], layer=j, out_dtype=F32, residual=x, name=f"conv_out_{j}")
            mix = dict(h=h, bcu=bcu, z=z)
        h2, u, x_out = _mlp_fwd(x_mid, g_mlp, wb["mlp_w1"], wb["mlp_w2"], layer=i, name=f"mlp_fwd_{i}")
        saved.append(dict(x_in=x, x_mid=x_mid, mix=mix, h2=h2, u=u, g_mix=g_mix, g_mlp=g_mlp))
        x = x_out

    dx, loss = _loss_head(x, target, name="loss_head")

    gw = {n: None for n in BIG}
    exchanged = None
    g_uq = [None, None]
    gs = {n: [None] * SMALL_GRADS[n][0] for n in SMALL_GRADS}

    def wgrad(nm, layer, a, b, **kw):
        out = _mm_tn(a, b, stack=gw[nm], layer=layer, layers=BIG[nm][0][0], name=f"{nm}_grad_{layer}", **kw)
        gw[nm], arrived = out if kw.get("swap") else (out, None)
        return arrived

    for i in reversed(range(4)):
        j = i // 2
        sv = saved[i]
        mix = sv["mix"]
        ride = i == 0 and later is not None
        du = _mlp_down_bwd(dx, wb["mlp_w2"], sv["u"], layer=i, name=f"mlp_down_bwd_{i}")
        wgrad("mlp_w2", i, sv["u"], dx, sqrelu_a=True)
        r1_early = wgrad("mlp_w1", i, sv["h2"], du,
                         swap={n: gw[n] for n in ("mlp_w2", "conv_w_in", "conv_w_out")} if ride else None)
        dx, dg, *arrived = _nt_rms_bwd(du, wb["mlp_w1"], sv["x_mid"], sv["g_mlp"], dx, layer=i, name=f"mlp_up_bwd_{i}",
                                       swap={"mlp_w1": gw["mlp_w1"]} if ride else None)
        if ride:
            r1_early.update(arrived[0])
        gs["g_mlp"][i] = dg[0]
        if i % 2 == 0:
            do, delta_row = _attn_out_bwd(dx, wb["attn_w_o"], mix["o"], layer=j, name=f"mla_out_bwd_{j}")
            wgrad("attn_w_o", j, mix["o"], dx)
            lse_row = mix["lse"]
            partials = None
            if ride:
                core = lax.axis_index("c").astype(jnp.int32).reshape(1)
                partials = {n: _chip_sum(gw[n], r1_early[n], core, name=f"grad_chip_sum_{n}") for n in EARLY}
            dq, dk, dv, arrived = _flash_bwd(mix["q"], mix["k"], mix["v"], do, lse_row, delta_row, pos_col, pos_row,
                                             name=f"flash_bwd_{j}", scatter=partials)
            if partials is not None:
                exchanged = (r1_early, arrived)
            dqr, dkvr, da, dgq, dgk, dgqa, dgkva = _mla_prep_bwd(
                mix["a"], mix["g_qa"], mix["g_kva"], mix["w_uq"], mix["w_ukv"], mix["g_q"], mix["g_k"], cc, sa, sb,
                dq, dk, dv, name=f"mla_prep_bwd_{j}")
            g_uq[j] = _mm_tn(mix["cq"], dqr, name=f"attn_w_uq_grad_{j}")[0]
            wgrad("attn_w_ukv", j, mix["ckv"], dkvr)
            wgrad("attn_w_down", j, mix["h"], da, keep=DOWN_DIM)
            dx, dg = _nt_rms_bwd(da, mix["w_down"], sv["x_in"], sv["g_mix"], dx, name=f"mla_down_bwd_{j}")
            gs["attn_g_qnorm"][j] = dgq[0, :QK_DIM]
            gs["attn_g_knorm"][j] = dgk[0, :QK_DIM]
            gs["attn_g_q_a"][j] = dgqa[0]
            gs["attn_g_kv_a"][j] = dgkva[0]
        else:
            dz = _mm_nt(dx, wb["conv_w_out"], layer=j, out_dtype=F32, name=f"conv_out_bwd_{j}")
            wgrad("conv_w_out", j, mix["z"], dx)
            dbcu, dcw = _conv_gate_bwd(mix["bcu"], dz, gains["conv_w"][j], name=f"conv_gate_bwd_{j}")
            gs["conv_w"][j] = dcw
            wgrad("conv_w_in", j, mix["h"], dbcu)
            dx, dg = _nt_rms_bwd(dbcu, wb["conv_w_in"], sv["x_in"], sv["g_mix"], dx, layer=j, name=f"conv_in_bwd_{j}")
        gs["g_mix"][i] = dg[0]

    gw["attn_w_uq"] = jnp.stack(g_uq).reshape(2, Q_RANK, N_HEADS, QK_PAD)[..., :QK_DIM].reshape(BIG["attn_w_uq"][0])
    grads_small = {n: jnp.stack(v) for n, v in gs.items()}
    return loss, dx, gw, grads_small, exchanged


def kernel(x, positions, g_mix, g_mlp, attn_w_down, attn_g_q_a, attn_g_kv_a, attn_w_uq, attn_w_ukv, attn_g_qnorm, attn_g_knorm, attn_w_o, conv_w_in, conv_w, conv_w_out, mlp_w1, mlp_w2, loss_target, m_g_mix, m_g_mlp, m_attn_w_down, m_attn_g_q_a, m_attn_g_kv_a, m_attn_w_uq, m_attn_w_ukv, m_attn_g_qnorm, m_attn_g_knorm, m_attn_w_o, m_conv_w_in, m_conv_w, m_conv_w_out, m_mlp_w1, m_mlp_w2, v_g_mix, v_g_mlp, v_attn_w_down, v_attn_g_q_a, v_attn_g_kv_a, v_attn_w_uq, v_attn_w_ukv, v_attn_g_qnorm, v_attn_g_knorm, v_attn_w_o, v_conv_w_in, v_conv_w, v_conv_w_out, v_mlp_w1, v_mlp_w2):
    w = dict(g_mix=g_mix, g_mlp=g_mlp, attn_w_down=attn_w_down, attn_g_q_a=attn_g_q_a, attn_g_kv_a=attn_g_kv_a,
             attn_w_uq=attn_w_uq, attn_w_ukv=attn_w_ukv, attn_g_qnorm=attn_g_qnorm, attn_g_knorm=attn_g_knorm,
             attn_w_o=attn_w_o, conv_w_in=conv_w_in, conv_w=conv_w, conv_w_out=conv_w_out, mlp_w1=mlp_w1, mlp_w2=mlp_w2)
    m = dict(g_mix=m_g_mix, g_mlp=m_g_mlp, attn_w_down=m_attn_w_down, attn_g_q_a=m_attn_g_q_a,
             attn_g_kv_a=m_attn_g_kv_a, attn_w_uq=m_attn_w_uq, attn_w_ukv=m_attn_w_ukv, attn_g_qnorm=m_attn_g_qnorm,
             attn_g_knorm=m_attn_g_knorm, attn_w_o=m_attn_w_o, conv_w_in=m_conv_w_in, conv_w=m_conv_w,
             conv_w_out=m_conv_w_out, mlp_w1=m_mlp_w1, mlp_w2=m_mlp_w2)
    v = dict(g_mix=v_g_mix, g_mlp=v_g_mlp, attn_w_down=v_attn_w_down, attn_g_q_a=v_attn_g_q_a,
             attn_g_kv_a=v_attn_g_kv_a, attn_w_uq=v_attn_w_uq, attn_w_ukv=v_attn_w_ukv, attn_g_qnorm=v_attn_g_qnorm,
             attn_g_knorm=v_attn_g_knorm, attn_w_o=v_attn_w_o, conv_w_in=v_conv_w_in, conv_w=v_conv_w,
             conv_w_out=v_conv_w_out, mlp_w1=v_mlp_w1, mlp_w2=v_mlp_w2)
    cx, cy, cc_ = _place()

    chip = 2 * cx + cy

    def own_offset(shape, axis):
        return tuple(chip * (shape[axis] // N_CHIPS) if i == axis else 0 for i in range(3))

    chip_arr = chip.astype(jnp.int32).reshape(1)
    fulls = {n: _place_own(w[n], n, chip_arr, name=f"place_{n}") for n in BIG}
    later = {n: fulls.pop(n) for n in GATHER_LATER}
    wb = _gather_weights(fulls, name="gather_weights")

    placed = lax.dynamic_update_slice(jnp.zeros(CONV_W, F32), conv_w, own_offset(CONV_W, 2))
    conv_w_full = 0.5 * _all_reduce_small(placed.reshape(-1, 128), name="conv_w_gather").reshape(CONV_W)

    gains = {n: w[n] for n in SMALL}
    gains["conv_w"] = conv_w_full

    loss, grad_x, grads_big, grads_small, (r1_early, r2_early) = _local_step(
        x[0], positions[0], loss_target[0], wb, gains, later)

    place = jnp.stack([cc_, chip]).astype(jnp.int32)
    r1_late, partials = _chip_partials(grads_big, LATE, tag="late")
    r1 = {**r1_early, **r1_late}
    r2 = {**r2_early, **_scatter_partials(partials)}
    halves = {n: _final_sum(grads_big[n], r1[n], r2[n], place, n, name=f"grad_final_sum_{n}") for n in BIG}
    grad_shards = _join_halves(halves)

    small = _unpack_small(_all_reduce_small(_pack_small(grads_small, SMALL_GRADS), name="gain_all_reduce"), SMALL_GRADS)
    grad_shards["conv_w"] = lax.dynamic_slice(small["conv_w"], own_offset(CONV_W, 2), conv_w.shape)

    loss_total = lax.psum(loss[0, 0], ("x", "y", "c"))

    grads, deltas, new_m, new_v = {}, {}, {}, {}
    for n in [*BIG, "conv_w"]:
        shp = w[n].shape
        two_d = (shp[0] * shp[1], shp[2])
        g2 = grad_shards[n].reshape(two_d)
        d, nm, nv, g = _adamw(w[n].reshape(two_d), g2, m[n].reshape(two_d), v[n].reshape(two_d), name=f"adamw_{n}")
        grads[n], deltas[n], new_m[n], new_v[n] = g.reshape(shp), d.reshape(shp), nm.reshape(shp), nv.reshape(shp)
    d, nm, nv, g = _adamw(_pack_small(w, SMALL), _pack_small(small, SMALL), _pack_small(m, SMALL),
                          _pack_small(v, SMALL), name="adamw_gains")
    d, nm, nv, g = (_unpack_small(t, SMALL) for t in (d, nm, nv, g))
    for n in SMALL:
        grads[n], deltas[n], new_m[n], new_v[n] = g[n], d[n], nm[n], nv[n]

    return (loss_total, grad_x[None],
            *[grads[n] for n in WEIGHT_ORDER], *[deltas[n] for n in WEIGHT_ORDER],
            *[new_m[n] for n in WEIGHT_ORDER], *[new_v[n] for n in WEIGHT_ORDER])
```

```python
import functools

import jax
import jax.numpy as jnp
from jax import lax
from jax.experimental import pallas as pl
from jax.experimental.pallas import tpu as pltpu

F32 = jnp.float32
BF16 = jnp.bfloat16

D_MODEL = 1024
N_HEADS = 8
QK_NOPE = 128
QK_ROPE = 64
QK_DIM = QK_NOPE + QK_ROPE
QK_PAD = 256
V_DIM = 128
Q_RANK = 256
KV_RANK = 128
DOWN_DIM = Q_RANK + KV_RANK + QK_ROPE
DOWN_PAD = 512
ROPE_THETA = 10000.0
EPS = 1e-6
NEG = -1e30
SCALE = QK_DIM ** -0.5
SCALE_LOG2E = SCALE * 1.4426950408889634
LOG2E = 1.4426950408889634
ATTN_CHAINS = 2
DIAG_CHAINS = 4

ADAM_LR = 0.001
ADAM_B1 = 0.9
ADAM_B2 = 0.999
ADAM_EPS = 1e-08
ADAM_WD = 0.01
ADAM_STEP = 10

N_CHIPS = 4
MESH = pl.DeviceIdType.MESH
ANY = pl.BlockSpec(memory_space=pl.ANY)

TM = 512
TM_NARROW = 1024
TM_WIDE = 512
FWD_TQ = 1024
FWD_TK = 1024
BWD_TQ = 1024
BWD_TK = 1024
HALO = 16
T_PREP = 1024
T_PREP_BWD = 512
T_RED = 2048
SUM_BLOCK_BYTES = 4 * 1024 * 1024


def _tile(n, pref):
    t = min(n, pref)
    assert n % t == 0, (n, t)
    return t


def _cparams(*sem):
    return pltpu.CompilerParams(dimension_semantics=sem)


def _dot(a, b):
    return jnp.dot(a, b, preferred_element_type=F32)


def _dot_nt(a, b):
    return lax.dot_general(a, b, (((1,), (1,)), ((), ())), preferred_element_type=F32)


def _dot_tn(a, b):
    return lax.dot_general(a, b, (((0,), (0,)), ((), ())), preferred_element_type=F32)


def _rms(x, width):
    r = lax.rsqrt(jnp.sum(x * x, axis=-1, keepdims=True) * (1.0 / width) + EPS)
    return x * r, r


def _rms_bwd(xhat, r, dxhat, width):
    return r * (dxhat - xhat * (jnp.sum(dxhat * xhat, axis=-1, keepdims=True) * (1.0 / width)))


def _rope(t, cc, sa, sb):
    return t * cc + pltpu.roll(t, 96, 1) * sa + pltpu.roll(t, 32, 1) * sb


def _rope_t(g, cc, sa, sb):
    return g * cc + pltpu.roll(g * sa, 32, 1) + pltpu.roll(g * sb, 96, 1)


def _wspec(w, layer):
    once = pl.Buffered(1)
    if w.ndim == 2:
        return pl.BlockSpec(w.shape, lambda *_: (0, 0), pipeline_mode=once)
    return pl.BlockSpec((None,) + w.shape[1:], lambda *_: (layer, 0, 0), pipeline_mode=once)


def _mm_nn(a, b, *, out_dtype, name, residual=None, layer=0, forward=None):
    m, k = a.shape
    n = b.shape[-1]
    tm = _tile(m, TM_NARROW)
    grid = (m // tm,)
    names = list(forward or {})
    nf = len(names)
    n_in = 2 + (residual is not None)

    def body(*refs):
        a_ref, b_ref = refs[:2]
        o_ref = refs[n_in + nf]
        if nf:
            sends, recvs = _gather_copies(refs[n_in + nf + 1:n_in + 2 * nf + 1], names, *refs[n_in + 2 * nf + 1:],
                                          base=0, stride=3, to_sibling=True)
            _start_at_first_step(sends, grid)
        acc = _dot(a_ref[...].astype(BF16), b_ref[...])
        if residual is not None:
            acc = acc + refs[2][...]
        o_ref[...] = acc.astype(o_ref.dtype)
        if nf:
            @pl.when(_all_steps(grid, "last"))
            def _():
                for cp in recvs:
                    cp.wait_recv()
                for cp in sends:
                    cp.wait_send()

    in_specs = [pl.BlockSpec((tm, k), lambda i: (i, 0)), _wspec(b, layer)]
    args = [a, b]
    if residual is not None:
        in_specs.append(pl.BlockSpec((tm, n), lambda i: (i, 0)))
        args.append(residual)
    arrays = [forward[nm] for nm in names]
    out = pl.pallas_call(
        body, name=name, grid=grid, in_specs=in_specs + [ANY] * nf,
        out_specs=[pl.BlockSpec((tm, n), lambda i: (i, 0))] + [ANY] * nf,
        out_shape=[jax.ShapeDtypeStruct((m, n), out_dtype)] + [jax.ShapeDtypeStruct(t.shape, t.dtype) for t in arrays],
        scratch_shapes=[pltpu.SemaphoreType.DMA((3 * nf,)), pltpu.SemaphoreType.DMA((3 * nf,))] if nf else [],
        input_output_aliases={n_in + i: 1 + i for i in range(nf)},
        compiler_params=_cparams("arbitrary" if nf else "parallel"),
    )(*args, *arrays)
    return (out[0], dict(zip(names, out[1:]))) if nf else out[0]


def _mm_nt(a, b, *, out_dtype, name, layer=0):
    m, k = a.shape
    n = b.shape[-2]
    tm = _tile(m, TM_NARROW)

    def body(a_ref, b_ref, o_ref):
        o_ref[...] = _dot_nt(a_ref[...].astype(BF16), b_ref[...]).astype(o_ref.dtype)

    return pl.pallas_call(
        body, name=name, grid=(m // tm,),
        in_specs=[pl.BlockSpec((tm, k), lambda i: (i, 0)), _wspec(b, layer)],
        out_specs=pl.BlockSpec((tm, n), lambda i: (i, 0)),
        out_shape=jax.ShapeDtypeStruct((m, n), out_dtype),
        compiler_params=_cparams("parallel"),
    )(a, b)


def _mm_tn(a, b, *, name, stack=None, layer=0, layers=1, keep=None, sqrelu_a=False, swap=None):
    s, ka = a.shape
    n = b.shape[1]
    ts = _tile(s, T_RED)
    tka = _tile(ka, 1024)
    tn = _tile(n, 1024)
    n_out = n if keep is None else keep
    assert keep is None or tn == n
    grid = (ka // tka, n // tn, s // ts)
    names = list(swap or {})
    ns = len(names)
    n_in = 2 + (stack is not None)

    def body(*refs):
        a_ref, b_ref = refs[:2]
        o_ref = refs[n_in + ns]
        if ns:
            copies = _swap_copies(refs[n_in:n_in + ns], refs[n_in + ns + 1:n_in + 2 * ns + 1], names,
                                  *refs[n_in + 2 * ns + 1:])
            _start_at_first_step(copies, grid)

        @pl.when(pl.program_id(2) == 0)
        def _():
            o_ref[...] = jnp.zeros_like(o_ref)

        a_t = a_ref[...]
        if sqrelu_a:
            a_t = _sqrelu(a_t.astype(F32))
        o_ref[...] += _dot_tn(a_t.astype(BF16), b_ref[...].astype(BF16))[:, :n_out if keep else tn]
        if ns:
            _wait_at_last_step(copies, grid)

    in_specs = [pl.BlockSpec((ts, tka), lambda i, j, t: (t, i)), pl.BlockSpec((ts, tn), lambda i, j, t: (t, j))]
    args = [a, b]
    if stack is not None:
        in_specs.append(ANY)
        args.append(stack)
    sent = [swap[nm] for nm in names]
    out = pl.pallas_call(
        body, name=name, grid=grid, in_specs=in_specs + [ANY] * ns,
        out_specs=[pl.BlockSpec((None, tka, tn if keep is None else keep), lambda i, j, t: (layer, i, j))] + [ANY] * ns,
        out_shape=[jax.ShapeDtypeStruct((layers, ka, n_out), F32)] + _swap_out_shapes(sent),
        scratch_shapes=_swap_sems(ns),
        input_output_aliases={} if stack is None else {2: 0},
        compiler_params=_cparams(*(["arbitrary"] * 3 if ns else ["parallel", "parallel", "arbitrary"])),
    )(*args, *sent)
    return (out[0], dict(zip(names, out[1:]))) if ns else out[0]


def _norm_mm(x, g, w, *, out_dtype, name, layer=0):
    s, d = x.shape
    n = w.shape[-1]
    tm = _tile(s, TM_NARROW)

    def body(x_ref, g_ref, w_ref, h_ref, o_ref):
        xhat, _ = _rms(x_ref[...], d)
        h = (xhat * g_ref[...]).astype(BF16)
        h_ref[...] = h
        o_ref[...] = _dot(h, w_ref[...]).astype(o_ref.dtype)

    return pl.pallas_call(
        body, name=name, grid=(s // tm,),
        in_specs=[pl.BlockSpec((tm, d), lambda i: (i, 0)), pl.BlockSpec((1, d), lambda i: (0, 0)), _wspec(w, layer)],
        out_specs=[pl.BlockSpec((tm, d), lambda i: (i, 0)), pl.BlockSpec((tm, n), lambda i: (i, 0))],
        out_shape=[jax.ShapeDtypeStruct((s, d), BF16), jax.ShapeDtypeStruct((s, n), out_dtype)],
        compiler_params=_cparams("parallel"),
    )(x, g, w)


def _nt_rms_bwd(dy, w, x, g, dres, *, name, layer=0, swap=None):
    s, n = dy.shape
    d = x.shape[1]
    tm = _tile(s, TM if n > 3072 else TM_NARROW)
    grid = (s // tm,)
    names = list(swap or {})
    ns = len(names)

    def body(dy_ref, w_ref, x_ref, g_ref, dres_ref, *rest):
        dx_ref, dg_ref = rest[ns:ns + 2]
        if ns:
            copies = _swap_copies(rest[:ns], rest[ns + 2:2 * ns + 2], names, *rest[2 * ns + 2:])
            _start_at_first_step(copies, grid)

        @pl.when(pl.program_id(0) == 0)
        def _():
            dg_ref[...] = jnp.zeros_like(dg_ref)

        dh = _dot_nt(dy_ref[...], w_ref[...])
        xhat, r = _rms(x_ref[...], d)
        dg_ref[...] += jnp.sum(dh * xhat, axis=0, keepdims=True)
        dx_ref[...] = dres_ref[...] + _rms_bwd(xhat, r, dh * g_ref[...], d)
        if ns:
            _wait_at_last_step(copies, grid)

    sent = [swap[nm] for nm in names]
    out = pl.pallas_call(
        body, name=name, grid=grid,
        in_specs=[pl.BlockSpec((tm, n), lambda i: (i, 0)), _wspec(w, layer),
                  pl.BlockSpec((tm, d), lambda i: (i, 0)), pl.BlockSpec((1, d), lambda i: (0, 0)),
                  pl.BlockSpec((tm, d), lambda i: (i, 0))] + [ANY] * ns,
        out_specs=[pl.BlockSpec((tm, d), lambda i: (i, 0)), pl.BlockSpec((1, d), lambda i: (0, 0))] + [ANY] * ns,
        out_shape=[jax.ShapeDtypeStruct((s, d), F32), jax.ShapeDtypeStruct((1, d), F32)] + _swap_out_shapes(sent),
        scratch_shapes=_swap_sems(ns),
        compiler_params=_cparams("arbitrary"),
    )(dy, w, x, g, dres, *sent)
    return (out[0], out[1], dict(zip(names, out[2:]))) if ns else (out[0], out[1])


def _sqrelu(u):
    return jnp.square(jnp.maximum(u, 0.0))


def _mlp_fwd(x, g, w1, w2, *, name, layer=0):
    s, d = x.shape
    n = w1.shape[-1]
    tm = _tile(s, TM_WIDE)

    def body(x_ref, g_ref, w1_ref, w2_ref, h_ref, u_ref, y_ref):
        x_t = x_ref[...]
        xhat, _ = _rms(x_t, d)
        h = (xhat * g_ref[...]).astype(BF16)
        h_ref[...] = h
        u = _dot(h, w1_ref[...])
        u_ref[...] = u.astype(BF16)
        y_ref[...] = x_t + _dot(_sqrelu(u).astype(BF16), w2_ref[...])

    return pl.pallas_call(
        body, name=name, grid=(s // tm,),
        in_specs=[pl.BlockSpec((tm, d), lambda i: (i, 0)), pl.BlockSpec((1, d), lambda i: (0, 0)),
                  _wspec(w1, layer), _wspec(w2, layer)],
        out_specs=[pl.BlockSpec((tm, d), lambda i: (i, 0)), pl.BlockSpec((tm, n), lambda i: (i, 0)),
                   pl.BlockSpec((tm, d), lambda i: (i, 0))],
        out_shape=[jax.ShapeDtypeStruct((s, d), BF16), jax.ShapeDtypeStruct((s, n), BF16),
                   jax.ShapeDtypeStruct((s, d), F32)],
        compiler_params=_cparams("parallel"),
    )(x, g, w1, w2)


def _mlp_down_bwd(dy, w2, u, *, name, layer=0):
    s, d = dy.shape
    n = w2.shape[-2]
    tm = _tile(s, TM_WIDE)

    def body(dy_ref, w_ref, u_ref, du_ref):
        dact = _dot_nt(dy_ref[...].astype(BF16), w_ref[...])
        du_ref[...] = (dact * (2.0 * jnp.maximum(u_ref[...].astype(F32), 0.0))).astype(BF16)

    return pl.pallas_call(
        body, name=name, grid=(s // tm,),
        in_specs=[pl.BlockSpec((tm, d), lambda i: (i, 0)), _wspec(w2, layer),
                  pl.BlockSpec((tm, n), lambda i: (i, 0))],
        out_specs=pl.BlockSpec((tm, n), lambda i: (i, 0)),
        out_shape=jax.ShapeDtypeStruct((s, n), BF16),
        compiler_params=_cparams("parallel"),
    )(dy, w2, u)


def _conv_gate(bcu, conv_w, *, name):
    s = bcu.shape[0]
    d = D_MODEL
    tm = _tile(s, TM)
    hb = tm // HALO

    def body(bcu_ref, prev_ref, w_ref, z_ref, pbuf):
        i = pl.program_id(0)
        pprev = prev_ref[:, d:2 * d].astype(F32) * prev_ref[:, 2 * d:3 * d].astype(F32)
        pbuf[0:HALO, :] = jnp.where(i > 0, pprev, 0.0)
        for r in range(0, tm, HALO):
            rows = slice(r, r + HALO)
            pbuf[HALO + r:2 * HALO + r, :] = bcu_ref[rows, d:2 * d].astype(F32) * bcu_ref[rows, 2 * d:3 * d].astype(F32)
        w0, w1, w2 = w_ref[0:1, :], w_ref[1:2, :], w_ref[2:3, :]
        for r in range(0, tm, HALO):
            rows = slice(r, r + HALO)
            b = HALO + r
            cv = w2 * pbuf[b:b + HALO, :] + w1 * pbuf[b - 1:b - 1 + HALO, :] + w0 * pbuf[b - 2:b - 2 + HALO, :]
            z_ref[rows, :] = (bcu_ref[rows, 0:d].astype(F32) * cv).astype(BF16)

    return pl.pallas_call(
        body, name=name, grid=(s // tm,),
        in_specs=[pl.BlockSpec((tm, 3 * d), lambda i: (i, 0)),
                  pl.BlockSpec((HALO, 3 * d), lambda i: (jnp.maximum(i * hb - 1, 0), 0)),
                  pl.BlockSpec((3, d), lambda i: (0, 0))],
        out_specs=pl.BlockSpec((tm, d), lambda i: (i, 0)),
        out_shape=jax.ShapeDtypeStruct((s, d), BF16),
        scratch_shapes=[pltpu.VMEM((tm + HALO, d), F32)],
        compiler_params=_cparams("parallel"),
    )(bcu, bcu, conv_w)


def _conv_gate_bwd(bcu, dz, conv_w, *, name):
    s = bcu.shape[0]
    d = D_MODEL
    tm = _tile(s, TM)
    hb = tm // HALO
    nt = s // tm

    def body(bcu_ref, prev_ref, next_ref, dz_ref, dznext_ref, w_ref, dbcu_ref, dw_ref, pbuf, dbuf):
        i = pl.program_id(0)

        @pl.when(i == 0)
        def _():
            dw_ref[...] = jnp.zeros_like(dw_ref)

        pprev = prev_ref[:, d:2 * d].astype(F32) * prev_ref[:, 2 * d:3 * d].astype(F32)
        pbuf[0:HALO, :] = jnp.where(i > 0, pprev, 0.0)
        dcv_next = dznext_ref[...] * next_ref[:, 0:d].astype(F32)
        dbuf[tm:tm + HALO, :] = jnp.where(i < nt - 1, dcv_next, 0.0)
        for r in range(0, tm, HALO):
            rows = slice(r, r + HALO)
            pbuf[HALO + r:2 * HALO + r, :] = bcu_ref[rows, d:2 * d].astype(F32) * bcu_ref[rows, 2 * d:3 * d].astype(F32)
            dbuf[rows, :] = dz_ref[rows, :] * bcu_ref[rows, 0:d].astype(F32)
        w0, w1, w2 = w_ref[0:1, :], w_ref[1:2, :], w_ref[2:3, :]
        fold = lambda t: jnp.sum(t.reshape(HALO // 8, 8, d), axis=0)
        acc = [jnp.zeros((8, d), F32) for _ in range(3)]
        for r in range(0, tm, HALO):
            rows = slice(r, r + HALO)
            b = HALO + r
            p, p1, p2 = pbuf[b:b + HALO, :], pbuf[b - 1:b - 1 + HALO, :], pbuf[b - 2:b - 2 + HALO, :]
            dcv = dbuf[rows, :]
            dp = w2 * dcv + w1 * dbuf[r + 1:r + 1 + HALO, :] + w0 * dbuf[r + 2:r + 2 + HALO, :]
            acc = [acc[0] + fold(dcv * p2), acc[1] + fold(dcv * p1), acc[2] + fold(dcv * p)]
            dbcu_ref[rows, 0:d] = (dz_ref[rows, :] * (w2 * p + w1 * p1 + w0 * p2)).astype(BF16)
            dbcu_ref[rows, d:2 * d] = (dp * bcu_ref[rows, 2 * d:3 * d].astype(F32)).astype(BF16)
            dbcu_ref[rows, 2 * d:3 * d] = (dp * bcu_ref[rows, d:2 * d].astype(F32)).astype(BF16)
        for k in range(3):
            dw_ref[k:k + 1, :] += jnp.sum(acc[k], axis=0, keepdims=True)

    nxt = lambda i: (jnp.minimum((i + 1) * hb, s // HALO - 1), 0)
    return pl.pallas_call(
        body, name=name, grid=(nt,),
        in_specs=[pl.BlockSpec((tm, 3 * d), lambda i: (i, 0)),
                  pl.BlockSpec((HALO, 3 * d), lambda i: (jnp.maximum(i * hb - 1, 0), 0)),
                  pl.BlockSpec((HALO, 3 * d), nxt),
                  pl.BlockSpec((tm, d), lambda i: (i, 0)),
                  pl.BlockSpec((HALO, d), nxt),
                  pl.BlockSpec((3, d), lambda i: (0, 0))],
        out_specs=[pl.BlockSpec((tm, 3 * d), lambda i: (i, 0)), pl.BlockSpec((3, d), lambda i: (0, 0))],
        out_shape=[jax.ShapeDtypeStruct((s, 3 * d), BF16), jax.ShapeDtypeStruct((3, d), F32)],
        scratch_shapes=[pltpu.VMEM((tm + HALO, d), F32), pltpu.VMEM((tm + HALO, d), F32)],
        compiler_params=_cparams("arbitrary"),
    )(bcu, bcu, bcu, dz, dz, conv_w)


def _mla_prep(a, g_qa, g_kva, w_uq, w_ukv, g_q, g_k, cc, sa, sb, *, name):
    s = a.shape[0]
    ts = _tile(s, T_PREP)

    def body(a_ref, gqa_ref, gkva_ref, wuq_ref, wukv_ref, gq_ref, gk_ref, cc_ref, sa_ref, sb_ref,
             cq_ref, ckv_ref, q_ref, k_ref, v_ref):
        xq, _ = _rms(a_ref[:, 0:Q_RANK], Q_RANK)
        cq = (xq * gqa_ref[...]).astype(BF16)
        cq_ref[...] = cq
        xkv, _ = _rms(a_ref[:, Q_RANK:Q_RANK + KV_RANK], KV_RANK)
        ckv = (xkv * gkva_ref[...]).astype(BF16)
        ckv_ref[...] = ckv
        kpe = a_ref[:, Q_RANK + KV_RANK:DOWN_PAD]
        kpe_ss = jnp.sum(kpe * kpe, axis=-1, keepdims=True)
        cc_t, sa_t, sb_t = cc_ref[...], sa_ref[...], sb_ref[...]
        gq = gq_ref[...]
        gk = gk_ref[...]
        for h in range(N_HEADS):
            cols = slice(h * QK_PAD, (h + 1) * QK_PAD)
            qhat, _ = _rms(_dot(cq, wuq_ref[:, cols]), QK_DIM)
            qn = qhat * (gq * SCALE_LOG2E)
            q_ref[h, :, 0:QK_NOPE] = qn[:, 0:QK_NOPE].astype(BF16)
            q_ref[h, :, QK_NOPE:QK_PAD] = _rope(qn[:, QK_NOPE:QK_PAD], cc_t, sa_t, sb_t).astype(BF16)
            kvr = _dot(ckv, wukv_ref[:, cols])
            kn = kvr[:, 0:QK_NOPE]
            rk = lax.rsqrt((jnp.sum(kn * kn, axis=-1, keepdims=True) + kpe_ss) * (1.0 / QK_DIM) + EPS)
            k_ref[h, :, 0:QK_NOPE] = (kn * rk * gk[:, 0:QK_NOPE]).astype(BF16)
            k_ref[h, :, QK_NOPE:QK_PAD] = _rope(kpe * rk * gk[:, QK_NOPE:QK_PAD], cc_t, sa_t, sb_t).astype(BF16)
            v_ref[h, :, 0:V_DIM] = kvr[:, QK_NOPE:QK_PAD].astype(BF16)
            v_ref[h, :, V_DIM:2 * V_DIM] = jnp.ones((ts, V_DIM), BF16)

    row = lambda i: (i, 0)
    fixed = lambda i: (0, 0)
    head = lambda i: (0, i, 0)
    return pl.pallas_call(
        body, name=name, grid=(s // ts,),
        in_specs=[pl.BlockSpec((ts, DOWN_PAD), row), pl.BlockSpec((1, Q_RANK), fixed), pl.BlockSpec((1, KV_RANK), fixed),
                  pl.BlockSpec((Q_RANK, N_HEADS * QK_PAD), fixed), pl.BlockSpec((KV_RANK, N_HEADS * QK_PAD), fixed),
                  pl.BlockSpec((1, QK_PAD), fixed), pl.BlockSpec((1, QK_PAD), fixed),
                  pl.BlockSpec((ts, 128), row), pl.BlockSpec((ts, 128), row), pl.BlockSpec((ts, 128), row)],
        out_specs=[pl.BlockSpec((ts, Q_RANK), row), pl.BlockSpec((ts, KV_RANK), row),
                   pl.BlockSpec((N_HEADS, ts, QK_PAD), head), pl.BlockSpec((N_HEADS, ts, QK_PAD), head),
                   pl.BlockSpec((N_HEADS, ts, 2 * V_DIM), head)],
        out_shape=[jax.ShapeDtypeStruct((s, Q_RANK), BF16), jax.ShapeDtypeStruct((s, KV_RANK), BF16),
                   jax.ShapeDtypeStruct((N_HEADS, s, QK_PAD), BF16), jax.ShapeDtypeStruct((N_HEADS, s, QK_PAD), BF16),
                   jax.ShapeDtypeStruct((N_HEADS, s, 2 * V_DIM), BF16)],
        compiler_params=_cparams("parallel"),
    )(a, g_qa, g_kva, w_uq, w_ukv, g_q, g_k, cc, sa, sb)


def _mla_prep_bwd(a, g_qa, g_kva, w_uq, w_ukv, g_q, g_k, cc, sa, sb, dq, dk, dv, *, name):
    s = a.shape[0]
    ts = _tile(s, T_PREP_BWD)

    def body(a_ref, gqa_ref, gkva_ref, wuq_ref, wukv_ref, gq_ref, gk_ref, cc_ref, sa_ref, sb_ref,
             dq_ref, dk_ref, dv_ref, dqr_ref, dkvr_ref, da_ref, dgq_ref, dgk_ref, dgqa_ref, dgkva_ref):
        @pl.when(pl.program_id(0) == 0)
        def _():
            dgq_ref[...] = jnp.zeros_like(dgq_ref)
            dgk_ref[...] = jnp.zeros_like(dgk_ref)
            dgqa_ref[...] = jnp.zeros_like(dgqa_ref)
            dgkva_ref[...] = jnp.zeros_like(dgkva_ref)

        xq, r_q = _rms(a_ref[:, 0:Q_RANK], Q_RANK)
        cq = (xq * gqa_ref[...]).astype(BF16)
        xkv, r_kv = _rms(a_ref[:, Q_RANK:Q_RANK + KV_RANK], KV_RANK)
        ckv = (xkv * gkva_ref[...]).astype(BF16)
        kpe = a_ref[:, Q_RANK + KV_RANK:DOWN_PAD]
        kpe_ss = jnp.sum(kpe * kpe, axis=-1, keepdims=True)
        cc_t, sa_t, sb_t = cc_ref[...], sa_ref[...], sb_ref[...]
        gq = gq_ref[...]
        gk = gk_ref[...]
        dcq = jnp.zeros((ts, Q_RANK), F32)
        dckv = jnp.zeros((ts, KV_RANK), F32)
        dkpe = jnp.zeros((ts, 128), F32)
        dgq = jnp.zeros((1, QK_PAD), F32)
        dgk_n = jnp.zeros((1, QK_NOPE), F32)
        dgk_p = jnp.zeros((1, 128), F32)
        for h in range(N_HEADS):
            cols = slice(h * QK_PAD, (h + 1) * QK_PAD)
            qhat, rq = _rms(_dot(cq, wuq_ref[:, cols]), QK_DIM)
            dqn = jnp.concatenate(
                [dq_ref[h, :, 0:QK_NOPE], _rope_t(dq_ref[h, :, QK_NOPE:QK_PAD], cc_t, sa_t, sb_t)], axis=1)
            dgq = dgq + jnp.sum(dqn * qhat, axis=0, keepdims=True)
            dqr = _rms_bwd(qhat, rq, dqn * gq, QK_DIM).astype(BF16)
            dqr_ref[:, cols] = dqr
            dcq = dcq + _dot_nt(dqr, wuq_ref[:, cols])
            kn = _dot(ckv, wukv_ref[:, h * QK_PAD:h * QK_PAD + QK_NOPE])
            rk = lax.rsqrt((jnp.sum(kn * kn, axis=-1, keepdims=True) + kpe_ss) * (1.0 / QK_DIM) + EPS)
            khat_n = kn * rk
            khat_p = kpe * rk
            dkn = dk_ref[h, :, 0:QK_NOPE]
            dkp = _rope_t(dk_ref[h, :, QK_NOPE:QK_PAD], cc_t, sa_t, sb_t)
            dgk_n = dgk_n + jnp.sum(dkn * khat_n, axis=0, keepdims=True)
            dgk_p = dgk_p + jnp.sum(dkp * khat_p, axis=0, keepdims=True)
            dxn = dkn * gk[:, 0:QK_NOPE]
            dxp = dkp * gk[:, QK_NOPE:QK_PAD]
            mean = (jnp.sum(dxn * khat_n, axis=-1, keepdims=True)
                    + jnp.sum(dxp * khat_p, axis=-1, keepdims=True)) * (1.0 / QK_DIM)
            dkpe = dkpe + rk * (dxp - khat_p * mean)
            dkvr = jnp.concatenate([rk * (dxn - khat_n * mean), dv_ref[h, :, :]], axis=1).astype(BF16)
            dkvr_ref[:, cols] = dkvr
            dckv = dckv + _dot_nt(dkvr, wukv_ref[:, cols])
        dgq_ref[...] += dgq
        dgk_ref[:, 0:QK_NOPE] += dgk_n
        dgk_ref[:, QK_NOPE:QK_PAD] += dgk_p
        dgqa_ref[...] += jnp.sum(dcq * xq, axis=0, keepdims=True)
        dgkva_ref[...] += jnp.sum(dckv * xkv, axis=0, keepdims=True)
        da_ref[:, 0:Q_RANK] = _rms_bwd(xq, r_q, dcq * gqa_ref[...], Q_RANK).astype(BF16)
        da_ref[:, Q_RANK:Q_RANK + KV_RANK] = _rms_bwd(xkv, r_kv, dckv * gkva_ref[...], KV_RANK).astype(BF16)
        da_ref[:, Q_RANK + KV_RANK:DOWN_PAD] = dkpe.astype(BF16)

    row = lambda i: (i, 0)
    fixed = lambda i: (0, 0)
    head = lambda i: (0, i, 0)
    wide = N_HEADS * QK_PAD
    return pl.pallas_call(
        body, name=name, grid=(s // ts,),
        in_specs=[pl.BlockSpec((ts, DOWN_PAD), row), pl.BlockSpec((1, Q_RANK), fixed), pl.BlockSpec((1, KV_RANK), fixed),
                  pl.BlockSpec((Q_RANK, wide), fixed), pl.BlockSpec((KV_RANK, wide), fixed),
                  pl.BlockSpec((1, QK_PAD), fixed), pl.BlockSpec((1, QK_PAD), fixed),
                  pl.BlockSpec((ts, 128), row), pl.BlockSpec((ts, 128), row), pl.BlockSpec((ts, 128), row),
                  pl.BlockSpec((N_HEADS, ts, QK_PAD), head), pl.BlockSpec((N_HEADS, ts, QK_PAD), head),
                  pl.BlockSpec((N_HEADS, ts, V_DIM), head)],
        out_specs=[pl.BlockSpec((ts, wide), row), pl.BlockSpec((ts, wide), row), pl.BlockSpec((ts, DOWN_PAD), row),
                   pl.BlockSpec((1, QK_PAD), fixed), pl.BlockSpec((1, QK_PAD), fixed),
                   pl.BlockSpec((1, Q_RANK), fixed), pl.BlockSpec((1, KV_RANK), fixed)],
        out_shape=[jax.ShapeDtypeStruct((s, wide), BF16), jax.ShapeDtypeStruct((s, wide), BF16),
                   jax.ShapeDtypeStruct((s, DOWN_PAD), BF16),
                   jax.ShapeDtypeStruct((1, QK_PAD), F32), jax.ShapeDtypeStruct((1, QK_PAD), F32),
                   jax.ShapeDtypeStruct((1, Q_RANK), F32), jax.ShapeDtypeStruct((1, KV_RANK), F32)],
        compiler_params=_cparams("arbitrary"),
    )(a, g_qa, g_kva, w_uq, w_ukv, g_q, g_k, cc, sa, sb, dq, dk, dv)


def _flash_fwd(q, k, v, pos_col, pos_row, *, name, gather=None):
    nh, s, _ = q.shape
    tq = _tile(s, FWD_TQ)
    tk = _tile(s, FWD_TK)
    sq = tq // ATTN_CHAINS
    nq = s // tq
    names = list(gather or {})
    ng = len(names)

    def body(q_ref, k_ref, v_ref, pq_ref, pk_ref, *rest):
        o_ref, lse_ref = rest[ng:ng + 2]
        m_sc, acc_sc = rest[2 * ng + 2:2 * ng + 4]
        qb = pl.program_id(1)
        if ng:
            sends, recvs = _gather_ici_copies(rest[ng + 2:2 * ng + 2], names, *rest[2 * ng + 4:], base=0, stride=3)

            @pl.when((pl.program_id(0) == 0) & (qb == 0))
            def _():
                for cp in sends:
                    cp.start()

        m_sc[...] = jnp.full_like(m_sc, NEG)
        acc_sc[...] = jnp.zeros_like(acc_sc)

        def step(kb, masked):
            trim = masked and tq == tk
            chains = DIAG_CHAINS if trim else ATTN_CHAINS
            sq = tq // chains
            start = pl.multiple_of(kb * tk, tk)
            widths = [(u + 1) * sq if trim else tk for u in range(chains)]
            scores = [_dot_nt(q_ref[0, u * sq:(u + 1) * sq, :], k_ref[0, pl.ds(start, widths[u]), :])
                      for u in range(chains)]
            for u in range(chains):
                rows = slice(u * sq, (u + 1) * sq)
                keys = pl.ds(start, widths[u])
                sc = scores[u]
                if masked:
                    sc = jnp.where(pq_ref[rows, :] >= pk_ref[:, keys], sc, NEG)
                m_prev = m_sc[rows, :]
                m_new = jnp.maximum(m_prev, jnp.max(sc, axis=-1, keepdims=True))
                alpha = jnp.exp2(m_prev - m_new)
                p = jnp.exp2(sc - jnp.tile(m_new, (1, widths[u] // 128)))
                acc_sc[rows, :] = (jnp.tile(alpha, (1, 2)) * acc_sc[rows, :]
                                   + _dot(p.astype(BF16), v_ref[0, keys, :]))
                m_sc[rows, :] = m_new

        n_before = (qb * tq) // tk
        n_seen = (qb * tq + tq - 1) // tk + 1
        lax.fori_loop(0, n_before, lambda kb, c: (step(kb, False), c)[1], 0)
        lax.fori_loop(n_before, n_seen, lambda kb, c: (step(kb, True), c)[1], 0)
        l = acc_sc[:, V_DIM:2 * V_DIM]
        o_ref[...] = (acc_sc[:, 0:V_DIM] / l).astype(BF16)
        lse = m_sc[...] * (1.0 / LOG2E) + jnp.log(l)
        lse_ref[0] = lse.T[0:1, :]

        if ng:
            @pl.when((pl.program_id(0) == nh - 1) & (qb == nq - 1))
            def _():
                for cp in recvs:
                    cp.wait_recv()
                for cp in sends:
                    cp.wait_send()

    arrays = [gather[nm] for nm in names]
    out = pl.pallas_call(
        body, name=name, grid=(nh, nq),
        in_specs=[pl.BlockSpec((1, tq, QK_PAD), lambda h, qb: (h, qb, 0)),
                  pl.BlockSpec((1, s, QK_PAD), lambda h, qb: (h, 0, 0)),
                  pl.BlockSpec((1, s, 2 * V_DIM), lambda h, qb: (h, 0, 0)),
                  pl.BlockSpec((tq, 1), lambda h, qb: (qb, 0)),
                  pl.BlockSpec((1, s), lambda h, qb: (0, 0))] + [ANY] * ng,
        out_specs=[pl.BlockSpec((tq, V_DIM), lambda h, qb: (qb, h)),
                   pl.BlockSpec((1, 1, tq), lambda h, qb: (h, 0, qb))] + [ANY] * ng,
        scratch_shapes=[pltpu.VMEM((tq, 128), F32), pltpu.VMEM((tq, 2 * V_DIM), F32)]
        + ([pltpu.SemaphoreType.DMA((3 * ng,)), pltpu.SemaphoreType.DMA((3 * ng,))] if ng else []),
        out_shape=[jax.ShapeDtypeStruct((s, nh * V_DIM), BF16), jax.ShapeDtypeStruct((nh, 1, s), F32)]
        + [jax.ShapeDtypeStruct(a.shape, a.dtype) for a in arrays],
        input_output_aliases={5 + i: 2 + i for i in range(ng)},
        compiler_params=_cparams("arbitrary", "arbitrary") if ng else _cparams("parallel", "parallel"),
    )(q, k, v, pos_col, pos_row, *arrays)
    return out[0], out[1], dict(zip(names, out[2:]))


def _attn_out_bwd(dy, w_o, o, *, name, layer=0):
    s, d = dy.shape
    n = w_o.shape[-2]
    tm = _tile(s, TM)

    def body(dy_ref, w_ref, o_ref, do_ref, d_ref):
        do = _dot_nt(dy_ref[...].astype(BF16), w_ref[...]).astype(BF16)
        do_ref[...] = do
        for h in range(N_HEADS):
            cols = slice(h * V_DIM, (h + 1) * V_DIM)
            prod = do[:, cols].astype(F32) * o_ref[:, cols].astype(F32)
            d_ref[h] = jnp.sum(prod.T, axis=0, keepdims=True)

    return pl.pallas_call(
        body, name=name, grid=(s // tm,),
        in_specs=[pl.BlockSpec((tm, d), lambda i: (i, 0)), _wspec(w_o, layer), pl.BlockSpec((tm, n), lambda i: (i, 0))],
        out_specs=[pl.BlockSpec((tm, n), lambda i: (i, 0)), pl.BlockSpec((N_HEADS, 1, tm), lambda i: (0, 0, i))],
        out_shape=[jax.ShapeDtypeStruct((s, n), BF16), jax.ShapeDtypeStruct((N_HEADS, 1, s), F32)],
        compiler_params=_cparams("parallel"),
    )(dy, w_o, o)


def _flash_bwd(q, k, v, do, lse_row, delta_row, pos_col, pos_row, *, name, scatter=None):
    nh, s, _ = q.shape
    tq = _tile(s, BWD_TQ)
    tk = _tile(s, BWD_TK)
    nq, nk = s // tq, s // tk
    sk = tk // ATTN_CHAINS
    names = list(scatter or {})
    ng = len(names)

    def body(q_ref, k_ref, v_ref, do_ref, lse_ref, delta_ref, pq_ref, pk_ref, *rest):
        dq_ref, dk_ref, dv_ref = rest[ng:ng + 3]
        dk_sc, dv_sc = rest[2 * ng + 3:2 * ng + 5]
        kb = pl.program_id(1)
        if ng:
            copies = _scatter_copies(rest[:ng], rest[ng + 3:2 * ng + 3], names, *rest[2 * ng + 5:])

            @pl.when((pl.program_id(0) == 0) & (kb == 0))
            def _():
                for cp in copies:
                    cp.start()

        @pl.when(kb == 0)
        def _():
            dq_ref[...] = jnp.zeros_like(dq_ref)

        dk_sc[...] = jnp.zeros_like(dk_sc)
        dv_sc[...] = jnp.zeros_like(dv_sc)

        def step(qb, masked):
            trim = masked and tq == tk
            chains = DIAG_CHAINS if trim else ATTN_CHAINS
            sk = tk // chains
            start = pl.multiple_of(qb * tq, tq)
            offs = [u * sk if trim else 0 for u in range(chains)]
            qss = [pl.ds(start + offs[u], tq - offs[u]) for u in range(chains)]
            qts = [q_ref[0, qss[u], :] for u in range(chains)]
            dots = [do_ref[qss[u], :] for u in range(chains)]
            sts = [_dot_nt(k_ref[0, u * sk:(u + 1) * sk, :], qts[u]) for u in range(chains)]
            dpts = [_dot_nt(v_ref[0, u * sk:(u + 1) * sk, :], dots[u]) for u in range(chains)]
            parts = []
            for u in range(chains):
                rows = slice(u * sk, (u + 1) * sk)
                pt = jnp.exp2(sts[u] - lse_ref[0, :, qss[u]] * LOG2E)
                if masked:
                    pt = jnp.where(pq_ref[:, qss[u]] >= pk_ref[rows, :], pt, 0.0)
                dv_sc[rows, :] += _dot(pt.astype(BF16), dots[u])
                dst = (pt * (dpts[u] - delta_ref[0, :, qss[u]])).astype(BF16)
                dk_sc[rows, :] += _dot(dst, qts[u])
                parts.append(_dot_tn(dst, k_ref[0, rows, :]))
            if trim:
                for u in range(chains):
                    dq_ref[0, qss[u], :] += parts[u]
            else:
                dq_ref[0, qss[0], :] += functools.reduce(lambda a, b: a + b, parts)

        q_first = (kb * tk) // tq
        q_clear = (kb * tk + tk - 1) // tq + 1
        lax.fori_loop(q_first, q_clear, lambda qb, c: (step(qb, True), c)[1], 0)
        lax.fori_loop(q_clear, nq, lambda qb, c: (step(qb, False), c)[1], 0)
        dk_ref[0] = dk_sc[...] * (1.0 / LOG2E)
        dv_ref[0] = dv_sc[...]

        @pl.when(kb == nk - 1)
        def _():
            dq_ref[...] = dq_ref[...] * SCALE

        if ng:
            @pl.when((pl.program_id(0) == nh - 1) & (kb == nk - 1))
            def _():
                for cp in copies:
                    cp.wait()

    arrays = [scatter[nm] for nm in names]
    out = pl.pallas_call(
        body, name=name, grid=(nh, nk),
        in_specs=[pl.BlockSpec((1, s, QK_PAD), lambda h, kb: (h, 0, 0)),
                  pl.BlockSpec((1, tk, QK_PAD), lambda h, kb: (h, kb, 0)),
                  pl.BlockSpec((1, tk, V_DIM), lambda h, kb: (h, kb, 0)),
                  pl.BlockSpec((s, V_DIM), lambda h, kb: (0, h)),
                  pl.BlockSpec((1, 1, s), lambda h, kb: (h, 0, 0)),
                  pl.BlockSpec((1, 1, s), lambda h, kb: (h, 0, 0)),
                  pl.BlockSpec((1, s), lambda h, kb: (0, 0)),
                  pl.BlockSpec((tk, 1), lambda h, kb: (kb, 0))] + [ANY] * ng,
        out_specs=[pl.BlockSpec((1, s, QK_PAD), lambda h, kb: (h, 0, 0)),
                   pl.BlockSpec((1, tk, QK_PAD), lambda h, kb: (h, kb, 0)),
                   pl.BlockSpec((1, tk, V_DIM), lambda h, kb: (h, kb, 0))] + [ANY] * ng,
        scratch_shapes=[pltpu.VMEM((tk, QK_PAD), F32), pltpu.VMEM((tk, V_DIM), F32)]
        + ([pltpu.SemaphoreType.DMA((3 * ng,)), pltpu.SemaphoreType.DMA((3 * ng,))] if ng else []),
        out_shape=[jax.ShapeDtypeStruct((nh, s, QK_PAD), F32), jax.ShapeDtypeStruct((nh, s, QK_PAD), F32),
                   jax.ShapeDtypeStruct((nh, s, V_DIM), F32)] + _scatter_out_shapes(names, arrays),
        compiler_params=_cparams("arbitrary", "arbitrary"),
    )(q, k, v, do, lse_row, delta_row, pos_row, pos_col, *arrays)
    return out[0], out[1], out[2], dict(zip(names, out[3:]))


def _loss_head(y, target, *, name):
    s, d = y.shape
    tm = _tile(s, TM_NARROW)
    nt = s // tm

    def body(y_ref, t_ref, dy_ref, loss_ref, acc):
        i = pl.program_id(0)

        @pl.when(i == 0)
        def _():
            acc[...] = jnp.zeros_like(acc)

        e = y_ref[...] - t_ref[...]
        dy_ref[...] = e * (1.0 / d)
        acc[...] += jnp.sum((e * e).reshape(tm // 8, 8, d), axis=0)

        @pl.when(i == nt - 1)
        def _():
            loss_ref[...] = jnp.full((1, 128), 0.5 / d, F32) * jnp.sum(acc[...])

    return pl.pallas_call(
        body, name=name, grid=(nt,),
        in_specs=[pl.BlockSpec((tm, d), lambda i: (i, 0))] * 2,
        out_specs=[pl.BlockSpec((tm, d), lambda i: (i, 0)), pl.BlockSpec((1, 128), lambda i: (0, 0))],
        out_shape=[jax.ShapeDtypeStruct((s, d), F32), jax.ShapeDtypeStruct((1, 128), F32)],
        scratch_shapes=[pltpu.VMEM((8, d), F32)],
        compiler_params=_cparams("arbitrary"),
    )(y, target)


def _adamw(w, g, m, v, *, name):
    r, c = w.shape
    tr = _tile(r, 512) if r % 8 == 0 else r

    def body(w_ref, g_ref, m_ref, v_ref, d_ref, nm_ref, nv_ref, go_ref):
        g_t = g_ref[...]
        go_ref[...] = g_t
        nm = ADAM_B1 * m_ref[...] + (1.0 - ADAM_B1) * g_t
        nv = ADAM_B2 * v_ref[...] + (1.0 - ADAM_B2) * (g_t * g_t)
        m_hat = nm / (1.0 - ADAM_B1 ** ADAM_STEP)
        v_hat = nv / (1.0 - ADAM_B2 ** ADAM_STEP)
        d_ref[...] = -ADAM_LR * (m_hat / (jnp.sqrt(v_hat) + ADAM_EPS) + ADAM_WD * w_ref[...])
        nm_ref[...] = nm
        nv_ref[...] = nv

    spec = pl.BlockSpec((tr, c), lambda i: (i, 0))
    return pl.pallas_call(
        body, name=name, grid=(r // tr,), in_specs=[spec] * 4, out_specs=[spec] * 4,
        out_shape=[jax.ShapeDtypeStruct((r, c), F32)] * 4,
        compiler_params=_cparams("parallel"),
    )(w, g, m, v)


def _place():
    return lax.axis_index("x"), lax.axis_index("y"), lax.axis_index("c")


def _other_chips(x, y):
    return [(1 - x, y), (x, 1 - y), (1 - x, 1 - y)]


BIG = {
    "attn_w_down": ((2, 1024, 448), 1), "attn_w_uq": ((2, 256, 1536), 2), "attn_w_ukv": ((2, 128, 2048), 2),
    "attn_w_o": ((2, 1024, 1024), 1), "conv_w_in": ((2, 1024, 3072), 2),
    "conv_w_out": ((2, 1024, 1024), 1), "mlp_w1": ((4, 1024, 4096), 2), "mlp_w2": ((4, 4096, 1024), 1),
}
CONV_W = (2, 3, 1024)


def _shard_shape(name):
    shape, axis = BIG[name]
    return tuple(n // N_CHIPS if i == axis else n for i, n in enumerate(shape))


def _band(ref, name, layers, chip):
    shape, axis = BIG[name]
    width = shape[axis] // N_CHIPS
    if axis == 1:
        return ref.at[layers, pl.ds(chip * width, width), :]
    return ref.at[layers, :, pl.ds(chip * width, width)]


def _half(name, c):
    hl = BIG[name][0][0] // 2
    return pl.ds(c * hl, hl)


def _place_own(w, nm, chip, *, name):
    shape, axis = BIG[nm]
    layers, rows, cols = w.shape
    tr = _sum_rows(rows, cols)
    nrb = rows // tr
    if axis == 1:
        band = lambda l, i, ch: (l, ch[0] * nrb + i, 0)
    else:
        band = lambda l, i, ch: (l, i, ch[0])

    def body(chip_ref, w_ref, o_ref):
        o_ref[...] = w_ref[...].astype(BF16)

    return pl.pallas_call(
        body, name=name,
        grid_spec=pltpu.PrefetchScalarGridSpec(
            num_scalar_prefetch=1, grid=(layers, nrb),
            in_specs=[pl.BlockSpec((1, tr, cols), lambda l, i, ch: (l, i, 0))],
            out_specs=pl.BlockSpec((1, tr, cols), band)),
        out_shape=jax.ShapeDtypeStruct(shape, BF16),
        compiler_params=_cparams("parallel", "parallel"),
    )(chip, w)


def _gather_copies(outs, names, send_sems, recv_sems, *, base, stride, to_sibling):
    x, y, c = _place()
    me = 2 * x + y

    def copy(k, ref, nm, layers, chip, to):
        band = _band(ref, nm, layers, chip)
        return pltpu.make_async_remote_copy(
            src_ref=band, dst_ref=band, send_sem=send_sems.at[k], recv_sem=recv_sems.at[k],
            device_id=to, device_id_type=MESH)

    sends, recvs = [], []
    for i, nm in enumerate(names):
        for j, (cx, cy) in enumerate(_other_chips(x, y)):
            k = base + stride * i + j
            if to_sibling:
                sends.append(copy(k, outs[i], nm, _half(nm, c), 2 * cx + cy, (x, y, 1 - c)))
                recvs.append(copy(k, outs[i], nm, _half(nm, 1 - c), 2 * cx + cy, (x, y, c)))
            else:
                sends.append(copy(k, outs[i], nm, _half(nm, c), me, (cx, cy, c)))
                recvs.append(copy(k, outs[i], nm, _half(nm, c), 2 * cx + cy, (x, y, c)))
    return sends, recvs


def _gather_ici_copies(outs, names, send_sems, recv_sems, *, base, stride):
    return _gather_copies(outs, names, send_sems, recv_sems, base=base, stride=stride, to_sibling=False)


def _gather_weights(fulls, *, name):
    names = list(fulls)
    n = len(names)

    def body(*refs):
        outs = refs[n:2 * n]
        sems = refs[2 * n:]
        sent = []
        for to_sibling in (False, True):
            sends, recvs = _gather_copies(outs, names, *sems, base=3 * to_sibling, stride=6, to_sibling=to_sibling)
            for cp in sends:
                cp.start()
            for cp in recvs:
                cp.wait_recv()
            sent += sends
        for cp in sent:
            cp.wait_send()

    arrays = [fulls[nm] for nm in names]
    out = pl.pallas_call(
        body, name=name, in_specs=[ANY] * n, out_specs=[ANY] * n,
        out_shape=[jax.ShapeDtypeStruct(a.shape, a.dtype) for a in arrays],
        input_output_aliases={i: i for i in range(n)},
        scratch_shapes=[pltpu.SemaphoreType.DMA((6 * n,)), pltpu.SemaphoreType.DMA((6 * n,))],
    )(*arrays)
    return dict(zip(names, out))


def _swap_halves(grads, *, name):
    names = list(grads)
    n = len(names)

    def body(*refs):
        copies = _swap_copies(refs[:n], refs[n:2 * n], names, *refs[2 * n:])
        for cp in copies:
            cp.start()
        for cp in copies:
            cp.wait()

    arrays = [grads[nm] for nm in names]
    out = pl.pallas_call(
        body, name=name, in_specs=[ANY] * n, out_specs=[ANY] * n,
        out_shape=_swap_out_shapes(arrays), scratch_shapes=_swap_sems(n),
    )(*arrays)
    return dict(zip(names, out))


def _swap_copies(ins, outs, names, send_sems, recv_sems):
    x, y, c = _place()
    return [pltpu.make_async_remote_copy(
        src_ref=ins[i].at[_half(nm, 1 - c)], dst_ref=outs[i], send_sem=send_sems.at[i], recv_sem=recv_sems.at[i],
        device_id=(x, y, 1 - c), device_id_type=MESH) for i, nm in enumerate(names)]


def _swap_out_shapes(arrays):
    return [jax.ShapeDtypeStruct((a.shape[0] // 2,) + a.shape[1:], a.dtype) for a in arrays]


def _swap_sems(n):
    return [pltpu.SemaphoreType.DMA((n,)), pltpu.SemaphoreType.DMA((n,))] if n else []


def _all_steps(grid, at):
    cond = None
    for axis, size in enumerate(grid):
        this = pl.program_id(axis) == (0 if at == "first" else size - 1)
        cond = this if cond is None else cond & this
    return cond


def _start_at_first_step(copies, grid):
    @pl.when(_all_steps(grid, "first"))
    def _():
        for cp in copies:
            cp.start()


def _wait_at_last_step(copies, grid):
    @pl.when(_all_steps(grid, "last"))
    def _():
        for cp in copies:
            cp.wait()


def _sum_rows(rows, cols):
    t = rows
    while t * cols * 4 > SUM_BLOCK_BYTES and t % 16 == 0:
        t //= 2
    return t


def _chip_sum(g, r1, core, *, name):
    layers, rows, cols = g.shape
    hl = layers // 2
    tr = _sum_rows(rows, cols)

    def body(core_ref, g_ref, r_ref, o_ref):
        o_ref[...] = (g_ref[...] + r_ref[...]).astype(BF16)

    return pl.pallas_call(
        body, name=name,
        grid_spec=pltpu.PrefetchScalarGridSpec(
            num_scalar_prefetch=1, grid=(hl, rows // tr),
            in_specs=[pl.BlockSpec((1, tr, cols), lambda l, i, cr: (cr[0] * hl + l, i, 0)),
                      pl.BlockSpec((1, tr, cols), lambda l, i, cr: (l, i, 0))],
            out_specs=pl.BlockSpec((1, tr, cols), lambda l, i, cr: (l, i, 0))),
        out_shape=jax.ShapeDtypeStruct((hl, rows, cols), BF16),
        compiler_params=_cparams("parallel", "parallel"),
    )(core, g, r1)


def _chip_partials(grads, names, *, tag):
    core = lax.axis_index("c").astype(jnp.int32).reshape(1)
    r1 = _swap_halves({n: grads[n] for n in names}, name=f"grad_swap_halves_{tag}")
    return r1, {n: _chip_sum(grads[n], r1[n], core, name=f"grad_chip_sum_{n}") for n in names}


def _scatter_partials(partials):
    names = list(partials)
    n = len(names)

    def body(*refs):
        copies = _scatter_copies(refs[:n], refs[n:2 * n], names, *refs[2 * n:])
        for cp in copies:
            cp.start()
        for cp in copies:
            cp.wait()

    arrays = [partials[nm] for nm in names]
    out = pl.pallas_call(
        body, name="grad_scatter_partials", in_specs=[ANY] * n, out_specs=[ANY] * n,
        out_shape=_scatter_out_shapes(names, arrays),
        scratch_shapes=[pltpu.SemaphoreType.DMA((3 * n,)), pltpu.SemaphoreType.DMA((3 * n,))],
    )(*arrays)
    return dict(zip(names, out))


def _scatter_copies(ins, outs, names, send_sems, recv_sems):
    x, y, c = _place()
    copies = []
    for i, nm in enumerate(names):
        for j, (cx, cy) in enumerate(_other_chips(x, y)):
            copies.append(pltpu.make_async_remote_copy(
                src_ref=_band(ins[i], nm, slice(None), 2 * cx + cy), dst_ref=outs[i].at[j],
                send_sem=send_sems.at[3 * i + j], recv_sem=recv_sems.at[3 * i + j],
                device_id=(cx, cy, c), device_id_type=MESH))
    return copies


def _scatter_out_shapes(names, arrays):
    return [jax.ShapeDtypeStruct((3, a.shape[0]) + _shard_shape(nm)[1:], a.dtype) for nm, a in zip(names, arrays)]


def _final_sum(g, r1, r2, place, nm, *, name):
    (layers, _, _), axis = BIG[nm]
    hl = layers // 2
    _, rows, cols = _shard_shape(nm)
    tr = _sum_rows(rows, cols)
    nrb = rows // tr
    if axis == 1:
        blk = lambda l, i, pc: (l, pc[1] * nrb + i, 0)
    else:
        blk = lambda l, i, pc: (l, i, pc[1])

    def body(place_ref, g_ref, r1_ref, r2_ref, o_ref):
        acc = g_ref[...] + r1_ref[...]
        for j in range(3):
            acc = acc + r2_ref[j].astype(F32)
        o_ref[...] = acc

    return pl.pallas_call(
        body, name=name,
        grid_spec=pltpu.PrefetchScalarGridSpec(
            num_scalar_prefetch=1, grid=(hl, nrb),
            in_specs=[pl.BlockSpec((1, tr, cols), lambda l, i, pc: blk(pc[0] * hl + l, i, pc)),
                      pl.BlockSpec((1, tr, cols), lambda l, i, pc: blk(l, i, pc)),
                      pl.BlockSpec((3, 1, tr, cols), lambda l, i, pc: (0, l, i, 0))],
            out_specs=pl.BlockSpec((1, tr, cols), lambda l, i, pc: (pc[0] * hl + l, i, 0))),
        out_shape=jax.ShapeDtypeStruct((layers, rows, cols), F32),
        compiler_params=_cparams("parallel", "parallel"),
    )(place, g, r1, r2)


def _join_halves(shards):
    names = list(shards)
    n = len(names)

    def body(*refs):
        outs = refs[n:2 * n]
        send_sems, recv_sems = refs[2 * n:]
        x, y, c = _place()
        copies = []
        for i, nm in enumerate(names):
            mine = outs[i].at[_half(nm, c)]
            cp = pltpu.make_async_remote_copy(
                src_ref=mine, dst_ref=mine, send_sem=send_sems.at[i], recv_sem=recv_sems.at[i],
                device_id=(x, y, 1 - c), device_id_type=MESH)
            cp.start()
            copies.append(cp)
        for i, nm in enumerate(names):
            theirs = outs[i].at[_half(nm, 1 - c)]
            pltpu.make_async_remote_copy(
                src_ref=theirs, dst_ref=theirs, send_sem=send_sems.at[i], recv_sem=recv_sems.at[i],
                device_id=(x, y, 1 - c), device_id_type=MESH).wait_recv()
        for cp in copies:
            cp.wait_send()

    arrays = [shards[nm] for nm in names]
    out = pl.pallas_call(
        body, name="grad_join_halves", in_specs=[ANY] * n, out_specs=[ANY] * n,
        out_shape=[jax.ShapeDtypeStruct(a.shape, a.dtype) for a in arrays],
        input_output_aliases={i: i for i in range(n)},
        scratch_shapes=[pltpu.SemaphoreType.DMA((n,)), pltpu.SemaphoreType.DMA((n,))],
    )(*arrays)
    return dict(zip(names, out))


def _all_reduce_small(part, *, name):
    rows, cols = part.shape
    vm = pl.BlockSpec(memory_space=pltpu.VMEM)

    def body(p_ref, o_ref, land, send_sems, recv_sems):
        x, y, c = _place()
        me = 4 * x + 2 * y + c
        flips = [(dx, dy, dc) for dx in (0, 1) for dy in (0, 1) for dc in (0, 1)][1:]
        copies = []
        for k, (dx, dy, dc) in enumerate(flips):
            cp = pltpu.make_async_remote_copy(
                src_ref=p_ref, dst_ref=land.at[me], send_sem=send_sems.at[k], recv_sem=recv_sems.at[k],
                device_id=(1 - x if dx else x, 1 - y if dy else y, 1 - c if dc else c), device_id_type=MESH)
            cp.start()
            copies.append(cp)
        land[me] = p_ref[...]
        for cp in copies:
            cp.wait()
        acc = land[0]
        for j in range(1, 8):
            acc = acc + land[j]
        o_ref[...] = acc

    return pl.pallas_call(
        body, name=name, in_specs=[vm], out_specs=vm,
        out_shape=jax.ShapeDtypeStruct((rows, cols), F32),
        scratch_shapes=[pltpu.VMEM((8, rows, cols), F32), pltpu.SemaphoreType.DMA((7,)), pltpu.SemaphoreType.DMA((7,))],
    )(part)


SMALL = {"g_mix": (4, 1024), "g_mlp": (4, 1024), "attn_g_q_a": (2, 256), "attn_g_kv_a": (2, 128),
         "attn_g_qnorm": (2, 192), "attn_g_knorm": (2, 192)}
SMALL_GRADS = {**SMALL, "conv_w": CONV_W}
WEIGHT_ORDER = ["g_mix", "g_mlp", "attn_w_down", "attn_g_q_a", "attn_g_kv_a", "attn_w_uq", "attn_w_ukv",
                "attn_g_qnorm", "attn_g_knorm", "attn_w_o", "conv_w_in", "conv_w", "conv_w_out", "mlp_w1", "mlp_w2"]


def _prod(shape):
    n = 1
    for v in shape:
        n *= v
    return n


def _pack_small(parts, table):
    flat = [parts[n].reshape(-1) for n in table]
    size = sum(_prod(s) for s in table.values())
    rows = -(-size // (8 * 128)) * 8
    flat.append(jnp.zeros((rows * 128 - size,), F32))
    return jnp.concatenate(flat).reshape(rows, 128)


def _unpack_small(buf, table):
    flat = buf.reshape(-1)
    out, off = {}, 0
    for n, shp in table.items():
        out[n] = flat[off:off + _prod(shp)].reshape(shp)
        off += _prod(shp)
    return out


def _rope_tables(positions):
    inv_freq = ROPE_THETA ** (-jnp.arange(0, QK_ROPE, 2, dtype=F32) / QK_ROPE)
    ang = positions.astype(F32)[:, None] * inv_freq
    cos, sin = jnp.cos(ang), jnp.sin(ang)
    z32 = jnp.zeros_like(cos)
    z64 = jnp.zeros((positions.shape[0], 64), F32)
    cc = jnp.concatenate([cos, cos, z64], axis=1)
    sa = jnp.concatenate([-sin, z32, z64], axis=1)
    sb = jnp.concatenate([z32, sin, z64], axis=1)
    return cc, sa, sb


def _pad_heads(w, width):
    k = w.shape[0]
    w = w.reshape(k, N_HEADS, width)
    return jnp.pad(w, ((0, 0), (0, 0), (0, QK_PAD - width))).reshape(k, N_HEADS * QK_PAD)


EARLY = ("mlp_w1", "mlp_w2", "conv_w_in", "conv_w_out")
LATE = ("attn_w_down", "attn_w_uq", "attn_w_ukv", "attn_w_o")
GATHER_LATER = EARLY


def _local_step(x, positions, target, wb, gains, later=None):
    s = x.shape[0]
    cc, sa, sb = _rope_tables(positions)
    pos_col = positions.reshape(s, 1)
    pos_row = positions.reshape(1, s)

    saved = []
    for i in range(4):
        j = i // 2
        g_mix = gains["g_mix"][i:i + 1]
        g_mlp = gains["g_mlp"][i:i + 1]
        if i % 2 == 0:
            w_down = jnp.pad(wb["attn_w_down"][j], ((0, 0), (0, DOWN_PAD - DOWN_DIM)))
            w_uq = _pad_heads(wb["attn_w_uq"][j], QK_DIM)
            w_ukv = wb["attn_w_ukv"][j]
            g_qa = gains["attn_g_q_a"][j:j + 1]
            g_kva = gains["attn_g_kv_a"][j:j + 1]
            g_q = jnp.pad(gains["attn_g_qnorm"][j:j + 1], ((0, 0), (0, QK_PAD - QK_DIM)))
            g_k = jnp.pad(gains["attn_g_knorm"][j:j + 1], ((0, 0), (0, QK_PAD - QK_DIM)))
            h, a = _norm_mm(x, g_mix, w_down, out_dtype=F32, name=f"mla_down_{j}")
            cq, ckv, q, k, v = _mla_prep(a, g_qa, g_kva, w_uq, w_ukv, g_q, g_k, cc, sa, sb, name=f"mla_prep_{j}")
            o, lse, got = _flash_fwd(q, k, v, pos_col, pos_row, name=f"flash_fwd_{j}",
                                     gather=later if i == 0 else None)
            x_mid = _mm_nn(o, wb["attn_w_o"], layer=j, out_dtype=F32, residual=x, name=f"mla_out_{j}", forward=got)
            if got:
                x_mid, forwarded = x_mid
                wb = {**wb, **forwarded}
            mix = dict(h=h, a=a, cq=cq, ckv=ckv, q=q, k=k, v=v, o=o, lse=lse, w_down=w_down, w_uq=w_uq, w_ukv=w_ukv,
                       g_qa=g_qa, g_kva=g_kva, g_q=g_q, g_k=g_k)
        else:
            h, bcu = _norm_mm(x, g_mix, wb["conv_w_in"], layer=j, out_dtype=BF16, name=f"conv_in_{j}")
            z = _conv_gate(bcu, gains["conv_w"][j], name=f"conv_gate_{j}")
            x_mid = _mm_nn(z, wb["conv_w_out"], layer=j, out_dtype=F32, residual=x, name=f"conv_out_{j}")
            mix = dict(h=h, bcu=bcu, z=z)
        h2, u, x_out = _mlp_fwd(x_mid, g_mlp, wb["mlp_w1"], wb["mlp_w2"], layer=i, name=f"mlp_fwd_{i}")
        saved.append(dict(x_in=x, x_mid=x_mid, mix=mix, h2=h2, u=u, g_mix=g_mix, g_mlp=g_mlp))
        x = x_out

    dx, loss = _loss_head(x, target, name="loss_head")

    gw = {n: None for n in BIG}
    exchanged = None
    g_uq = [None, None]
    gs = {n: [None] * SMALL_GRADS[n][0] for n in SMALL_GRADS}

    def wgrad(nm, layer, a, b, **kw):
        out = _mm_tn(a, b, stack=gw[nm], layer=layer, layers=BIG[nm][0][0], name=f"{nm}_grad_{layer}", **kw)
        gw[nm], arrived = out if kw.get("swap") else (out, None)
        return arrived

    for i in reversed(range(4)):
        j = i // 2
        sv = saved[i]
        mix = sv["mix"]
        ride = i == 0 and later is not None
        du = _mlp_down_bwd(dx, wb["mlp_w2"], sv["u"], layer=i, name=f"mlp_down_bwd_{i}")
        wgrad("mlp_w2", i, sv["u"], dx, sqrelu_a=True)
        r1_early = wgrad("mlp_w1", i, sv["h2"], du,
                         swap={n: gw[n] for n in ("mlp_w2", "conv_w_in", "conv_w_out")} if ride else None)
        dx, dg, *arrived = _nt_rms_bwd(du, wb["mlp_w1"], sv["x_mid"], sv["g_mlp"], dx, layer=i, name=f"mlp_up_bwd_{i}",
                                       swap={"mlp_w1": gw["mlp_w1"]} if ride else None)
        if ride:
            r1_early.update(arrived[0])
        gs["g_mlp"][i] = dg[0]
        if i % 2 == 0:
            do, delta_row = _attn_out_bwd(dx, wb["attn_w_o"], mix["o"], layer=j, name=f"mla_out_bwd_{j}")
            wgrad("attn_w_o", j, mix["o"], dx)
            lse_row = mix["lse"]
            partials = None
            if ride:
                core = lax.axis_index("c").astype(jnp.int32).reshape(1)
                partials = {n: _chip_sum(gw[n], r1_early[n], core, name=f"grad_chip_sum_{n}") for n in EARLY}
            dq, dk, dv, arrived = _flash_bwd(mix["q"], mix["k"], mix["v"], do, lse_row, delta_row, pos_col, pos_row,
                                             name=f"flash_bwd_{j}", scatter=partials)
            if partials is not None:
                exchanged = (r1_early, arrived)
            dqr, dkvr, da, dgq, dgk, dgqa, dgkva = _mla_prep_bwd(
                mix["a"], mix["g_qa"], mix["g_kva"], mix["w_uq"], mix["w_ukv"], mix["g_q"], mix["g_k"], cc, sa, sb,
                dq, dk, dv, name=f"mla_prep_bwd_{j}")
            g_uq[j] = _mm_tn(mix["cq"], dqr, name=f"attn_w_uq_grad_{j}")[0]
            wgrad("attn_w_ukv", j, mix["ckv"], dkvr)
            wgrad("attn_w_down", j, mix["h"], da, keep=DOWN_DIM)
            dx, dg = _nt_rms_bwd(da, mix["w_down"], sv["x_in"], sv["g_mix"], dx, name=f"mla_down_bwd_{j}")
            gs["attn_g_qnorm"][j] = dgq[0, :QK_DIM]
            gs["attn_g_knorm"][j] = dgk[0, :QK_DIM]
            gs["attn_g_q_a"][j] = dgqa[0]
            gs["attn_g_kv_a"][j] = dgkva[0]
        else:
            dz = _mm_nt(dx, wb["conv_w_out"], layer=j, out_dtype=F32, name=f"conv_out_bwd_{j}")
            wgrad("conv_w_out", j, mix["z"], dx)
            dbcu, dcw = _conv_gate_bwd(mix["bcu"], dz, gains["conv_w"][j], name=f"conv_gate_bwd_{j}")
            gs["conv_w"][j] = dcw
            wgrad("conv_w_in", j, mix["h"], dbcu)
            dx, dg = _nt_rms_bwd(dbcu, wb["conv_w_in"], sv["x_in"], sv["g_mix"], dx, layer=j, name=f"conv_in_bwd_{j}")
        gs["g_mix"][i] = dg[0]

    gw["attn_w_uq"] = jnp.stack(g_uq).reshape(2, Q_RANK, N_HEADS, QK_PAD)[..., :QK_DIM].reshape(BIG["attn_w_uq"][0])
    grads_small = {n: jnp.stack(v) for n, v in gs.items()}
    return loss, dx, gw, grads_small, exchanged


def kernel(x, positions, g_mix, g_mlp, attn_w_down, attn_g_q_a, attn_g_kv_a, attn_w_uq, attn_w_ukv, attn_g_qnorm, attn_g_knorm, attn_w_o, conv_w_in, conv_w, conv_w_out, mlp_w1, mlp_w2, loss_target, m_g_mix, m_g_mlp, m_attn_w_down, m_attn_g_q_a, m_attn_g_kv_a, m_attn_w_uq, m_attn_w_ukv, m_attn_g_qnorm, m_attn_g_knorm, m_attn_w_o, m_conv_w_in, m_conv_w, m_conv_w_out, m_mlp_w1, m_mlp_w2, v_g_mix, v_g_mlp, v_attn_w_down, v_attn_g_q_a, v_attn_g_kv_a, v_attn_w_uq, v_attn_w_ukv, v_attn_g_qnorm, v_attn_g_knorm, v_attn_w_o, v_conv_w_in, v_conv_w, v_conv_w_out, v_mlp_w1, v_mlp_w2):
    w = dict(g_mix=g_mix, g_mlp=g_mlp, attn_w_down=attn_w_down, attn_g_q_a=attn_g_q_a, attn_g_kv_a=attn_g_kv_a,
             attn_w_uq=attn_w_uq, attn_w_ukv=attn_w_ukv, attn_g_qnorm=attn_g_qnorm, attn_g_knorm=attn_g_knorm,
             attn_w_o=attn_w_o, conv_w_in=conv_w_in, conv_w=conv_w, conv_w_out=conv_w_out, mlp_w1=mlp_w1, mlp_w2=mlp_w2)
    m = dict(g_mix=m_g_mix, g_mlp=m_g_mlp, attn_w_down=m_attn_w_down, attn_g_q_a=m_attn_g_q_a,
             attn_g_kv_a=m_attn_g_kv_a, attn_w_uq=m_attn_w_uq, attn_w_ukv=m_attn_w_ukv, attn_g_qnorm=m_attn_g_qnorm,
             attn_g_knorm=m_attn_g_knorm, attn_w_o=m_attn_w_o, conv_w_in=m_conv_w_in, conv_w=m_conv_w,
             conv_w_out=m_conv_w_out, mlp_w1=m_mlp_w1, mlp_w2=m_mlp_w2)
    v = dict(g_mix=v_g_mix, g_mlp=v_g_mlp, attn_w_down=v_attn_w_down, attn_g_q_a=v_attn_g_q_a,
             attn_g_kv_a=v_attn_g_kv_a, attn_w_uq=v_attn_w_uq, attn_w_ukv=v_attn_w_ukv, attn_g_qnorm=v_attn_g_qnorm,
             attn_g_knorm=v_attn_g_knorm, attn_w_o=v_attn_w_o, conv_w_in=v_conv_w_in, conv_w=v_conv_w,
             conv_w_out=v_conv_w_out, mlp_w1=v_mlp_w1, mlp_w2=v_mlp_w2)
    cx, cy, cc_ = _place()

    chip = 2 * cx + cy

    def own_offset(shape, axis):
        return tuple(chip * (shape[axis] // N_CHIPS) if i == axis else 0 for i in range(3))

    chip_arr = chip.astype(jnp.int32).reshape(1)
    fulls = {n: _place_own(w[n], n, chip_arr, name=f"place_{n}") for n in BIG}
    later = {n: fulls.pop(n) for n in GATHER_LATER}
    wb = _gather_weights(fulls, name="gather_weights")

    placed = lax.dynamic_update_slice(jnp.zeros(CONV_W, F32), conv_w, own_offset(CONV_W, 2))
    conv_w_full = 0.5 * _all_reduce_small(placed.reshape(-1, 128), name="conv_w_gather").reshape(CONV_W)

    gains = {n: w[n] for n in SMALL}
    gains["conv_w"] = conv_w_full

    loss, grad_x, grads_big, grads_small, (r1_early, r2_early) = _local_step(
        x[0], positions[0], loss_target[0], wb, gains, later)

    place = jnp.stack([cc_, chip]).astype(jnp.int32)
    r1_late, partials = _chip_partials(grads_big, LATE, tag="late")
    r1 = {**r1_early, **r1_late}
    r2 = {**r2_early, **_scatter_partials(partials)}
    halves = {n: _final_sum(grads_big[n], r1[n], r2[n], place, n, name=f"grad_final_sum_{n}") for n in BIG}
    grad_shards = _join_halves(halves)

    small = _unpack_small(_all_reduce_small(_pack_small(grads_small, SMALL_GRADS), name="gain_all_reduce"), SMALL_GRADS)
    grad_shards["conv_w"] = lax.dynamic_slice(small["conv_w"], own_offset(CONV_W, 2), conv_w.shape)

    loss_total = lax.psum(loss[0, 0], ("x", "y", "c"))

    grads, deltas, new_m, new_v = {}, {}, {}, {}
    for n in [*BIG, "conv_w"]:
        shp = w[n].shape
        two_d = (shp[0] * shp[1], shp[2])
        g2 = grad_shards[n].reshape(two_d)
        d, nm, nv, g = _adamw(w[n].reshape(two_d), g2, m[n].reshape(two_d), v[n].reshape(two_d), name=f"adamw_{n}")
        grads[n], deltas[n], new_m[n], new_v[n] = g.reshape(shp), d.reshape(shp), nm.reshape(shp), nv.reshape(shp)
    d, nm, nv, g = _adamw(_pack_small(w, SMALL), _pack_small(small, SMALL), _pack_small(m, SMALL),
                          _pack_small(v, SMALL), name="adamw_gains")
    d, nm, nv, g = (_unpack_small(t, SMALL) for t in (d, nm, nv, g))
    for n in SMALL:
        grads[n], deltas[n], new_m[n], new_v[n] = g[n], d[n], nm[n], nv[n]

    return (loss_total, grad_x[None],
            *[grads[n] for n in WEIGHT_ORDER], *[deltas[n] for n in WEIGHT_ORDER],
            *[new_m[n] for n in WEIGHT_ORDER], *[new_v[n] for n in WEIGHT_ORDER])
```
